```python
import math
import jax, jax.numpy as jnp
from jax import lax
import numpy as np

D_MODEL = 1024
BATCH = 32
SEQ = 2048
DEPTH = 4

N_MEM = 256
HEAD_DIM = 64
N_DIL_HEADS = 12
DIL_WIDTH = N_DIL_HEADS * HEAD_DIM
DIL_PATTERNS = ((128, 1), (512, 4), (2048, 16))
BLOCK = 128
N_SGU_GROUPS = 12
SGU_GROUP_DIM = 64
SGU_WIDTH = N_SGU_GROUPS * SGU_GROUP_DIM
CHUNK = 128
N_MEM_HEADS = 4
MEM_WIDTH = N_MEM_HEADS * HEAD_DIM
MIX_WIDTH = DIL_WIDTH + MEM_WIDTH
D_FF = ((-(-8 * D_MODEL // 3) + 255) // 256) * 256
DN_ALPHA = (2 * DEPTH) ** 0.25
DN_BETA = (8 * DEPTH) ** -0.25
N_A = (DEPTH + 1) // 2
N_B = DEPTH // 2
IN_A = 3 * DIL_WIDTH + MEM_WIDTH
IN_B = 2 * SGU_WIDTH + MEM_WIDTH
LN_EPS = 1e-5

kernel_name = "hybrid_dilated_gmlp_memory_deepnorm"


def layer_norm(x, g, b):
    xf = x.astype(jnp.float32)
    mu = xf.mean(-1, keepdims=True)
    var = jnp.square(xf - mu).mean(-1, keepdims=True)
    y = (xf - mu) * lax.rsqrt(var + LN_EPS)
    return (y * g.astype(jnp.float32) + b.astype(jnp.float32)).astype(x.dtype)


def alibi_slopes(n):
    return jnp.exp2(-8.0 * (jnp.arange(n, dtype=jnp.float32) + 1.0) / n)


def band_pattern(q, k, v, slopes, window, dil):
    B, S, H, Dh = q.shape
    steps_max = window // dil
    L = S // dil
    n_blk = -(-L // BLOCK)
    Lp = n_blk * BLOCK
    def sub(t):
        return t.reshape(B, L, dil, H, Dh)
    qs = jnp.pad(sub(q), ((0, 0), (0, Lp - L), (0, 0), (0, 0), (0, 0)))
    qs = qs.reshape(B, n_blk, BLOCK, dil, H, Dh)
    def banded_keys(t):
        tp = jnp.pad(sub(t), ((0, 0), (BLOCK, Lp - L), (0, 0), (0, 0), (0, 0)))
        prev = tp[:, :Lp].reshape(B, n_blk, BLOCK, dil, H, Dh)
        cur = tp[:, BLOCK:].reshape(B, n_blk, BLOCK, dil, H, Dh)
        return jnp.concatenate([prev, cur], axis=2)
    kb = banded_keys(k)
    vb = banded_keys(v)
    s = jnp.einsum('bnqrhd,bnkrhd->bnrhqk', qs, kb).astype(jnp.float32) * (Dh ** -0.5)
    qi = jnp.arange(BLOCK)[:, None]
    ki = jnp.arange(2 * BLOCK)[None, :]
    steps = qi + BLOCK - ki
    key_idx = (jnp.arange(n_blk) * BLOCK)[:, None, None] - BLOCK + ki[None]
    valid = (steps >= 0) & (steps <= steps_max) & (key_idx >= 0)
    bias = -slopes[:, None, None] * (steps * dil).astype(jnp.float32)[None]
    s = s + bias[None, None, None]
    s = jnp.where(valid[None, :, None, None], s, -jnp.inf)
    m = s.max(-1, keepdims=True)
    p = jnp.exp(s - m)
    den = p.sum(-1)
    lse = m[..., 0] + jnp.log(den)
    o = jnp.einsum('bnrhqk,bnkrhd->bnqrhd', p.astype(v.dtype), vb)
    o = o / jnp.moveaxis(den, -1, 2)[..., None].astype(o.dtype)
    o = o.reshape(B, Lp, dil, H, Dh)[:, :L].reshape(B, S, H, Dh)
    lse = jnp.moveaxis(lse, -1, 2).reshape(B, Lp, dil, H)[:, :L].reshape(B, S, H)
    return o, lse


def dilated_attention(q, k, v):
    slopes = alibi_slopes(q.shape[2])
    outs, lses = [], []
    for window, dil in DIL_PATTERNS:
        o, lse = band_pattern(q, k, v, slopes, window, dil)
        outs.append(o)
        lses.append(lse)
    w = jax.nn.softmax(jnp.stack(lses, 0), axis=0).astype(q.dtype)
    return jnp.einsum('pbsh,pbshd->bshd', w, jnp.stack(outs, 0))


def causal_chunk_sgu(u, v, ln_g, ln_b, w_s, b_s):
    B, S, _ = v.shape
    v = layer_norm(v, ln_g, ln_b)
    vc = v.reshape(B, S // CHUNK, CHUNK, N_SGU_GROUPS, SGU_GROUP_DIM)
    ws = w_s * jnp.tril(jnp.ones((CHUNK, CHUNK), w_s.dtype))
    mixed = jnp.einsum('gts,bnsgc->bntgc', ws, vc) + b_s.T[None, None, :, :, None]
    return u * mixed.reshape(B, S, SGU_WIDTH)


def memory_attention(qm, mk, mv):
    s = jnp.einsum('bshd,bmhd->bhsm', qm, mk).astype(jnp.float32) * (HEAD_DIM ** -0.5)
    p = jax.nn.softmax(s, axis=-1).astype(mv.dtype)
    return jnp.einsum('bhsm,bmhd->bshd', p, mv)


def _fwd_setup_inputs(seed: int = 0) -> dict:
    key = jax.random.key(seed)
    ks = jax.random.split(key, 18)
    f32 = jnp.float32
    D = D_MODEL
    def nrm(k, shape, scale):
        return jax.random.normal(k, shape, f32) * scale
    return {
        "x": nrm(ks[0], (BATCH, SEQ, D), 1.0),
        "mem": nrm(ks[1], (BATCH, N_MEM, D), 1.0),
        "a_w_in": nrm(ks[2], (N_A, D, IN_A), D ** -0.5),
        "b_w_in": nrm(ks[3], (N_B, D, IN_B), D ** -0.5),
        "sgu_ln_g": 1.0 + nrm(ks[4], (N_B, SGU_WIDTH), 0.02),
        "sgu_ln_b": nrm(ks[5], (N_B, SGU_WIDTH), 0.02),
        "sgu_w_s": nrm(ks[6], (N_B, N_SGU_GROUPS, CHUNK, CHUNK), CHUNK ** -0.5),
        "sgu_b_s": 1.0 + nrm(ks[7], (N_B, N_SGU_GROUPS, CHUNK), 0.02),
        "w_mem_kv": nrm(ks[8], (DEPTH, D, 2 * MEM_WIDTH), D ** -0.5),
        "w_out": nrm(ks[9], (DEPTH, MIX_WIDTH, D), DN_BETA * MIX_WIDTH ** -0.5),
        "ln_mix_g": 1.0 + nrm(ks[10], (DEPTH, D), 0.02),
        "ln_mix_b": nrm(ks[11], (DEPTH, D), 0.02),
        "w_gate": nrm(ks[12], (DEPTH, D, D_FF), D ** -0.5),
        "w_up": nrm(ks[13], (DEPTH, D, D_FF), D ** -0.5),
        "w_down": nrm(ks[14], (DEPTH, D_FF, D), DN_BETA * D_FF ** -0.5),
        "ln_ffn_g": 1.0 + nrm(ks[15], (DEPTH, D), 0.02),
        "ln_ffn_b": nrm(ks[16], (DEPTH, D), 0.02),
    }


def _fwd_reference(x, mem, a_w_in, b_w_in, sgu_ln_g, sgu_ln_b, sgu_w_s, sgu_b_s, w_mem_kv, w_out,
              ln_mix_g, ln_mix_b, w_gate, w_up, w_down, ln_ffn_g, ln_ffn_b):
    B, S, _ = x.shape
    for i in range(DEPTH):
        j = i // 2
        mk, mv = jnp.split(mem @ w_mem_kv[i], 2, axis=-1)
        mk = mk.reshape(B, N_MEM, N_MEM_HEADS, HEAD_DIM)
        mv = mv.reshape(B, N_MEM, N_MEM_HEADS, HEAD_DIM)
        if i % 2 == 0:
            h = x @ a_w_in[j]
            q, k, v, qm = jnp.split(h, [DIL_WIDTH, 2 * DIL_WIDTH, 3 * DIL_WIDTH], axis=-1)
            hs = (B, S, N_DIL_HEADS, HEAD_DIM)
            mix = dilated_attention(q.reshape(hs), k.reshape(hs), v.reshape(hs)).reshape(B, S, DIL_WIDTH)
        else:
            h = x @ b_w_in[j]
            u, v, qm = jnp.split(h, [SGU_WIDTH, 2 * SGU_WIDTH], axis=-1)
            mix = causal_chunk_sgu(jax.nn.gelu(u), jax.nn.gelu(v), sgu_ln_g[j], sgu_ln_b[j],
                                   sgu_w_s[j], sgu_b_s[j])
        mo = memory_attention(qm.reshape(B, S, N_MEM_HEADS, HEAD_DIM), mk, mv).reshape(B, S, MEM_WIDTH)
        y = jnp.concatenate([mix, mo], axis=-1) @ w_out[i]
        x = layer_norm(DN_ALPHA * x + y, ln_mix_g[i], ln_mix_b[i])
        f = (jax.nn.silu(x @ w_gate[i]) * (x @ w_up[i])) @ w_down[i]
        x = layer_norm(DN_ALPHA * x + f, ln_ffn_g[i], ln_ffn_b[i])
    return x


import jax as _jax
import jax.numpy as _jnp

TWIN_FORMAT = 'train_step'
FWD_PARAMS = ['x', 'mem', 'a_w_in', 'b_w_in', 'sgu_ln_g', 'sgu_ln_b', 'sgu_w_s', 'sgu_b_s', 'w_mem_kv', 'w_out', 'ln_mix_g', 'ln_mix_b', 'w_gate', 'w_up', 'w_down', 'ln_ffn_g', 'ln_ffn_b']
TWIN_WEIGHTS = ['a_w_in', 'b_w_in', 'sgu_ln_g', 'sgu_ln_b', 'sgu_w_s', 'sgu_b_s', 'w_mem_kv', 'w_out', 'ln_mix_g', 'ln_mix_b', 'w_gate', 'w_up', 'w_down', 'ln_ffn_g', 'ln_ffn_b']
TWIN_DIFF_INPUT = 'x'
TWIN_INPUTS = ['x', 'mem', 'a_w_in', 'b_w_in', 'sgu_ln_g', 'sgu_ln_b', 'sgu_w_s', 'sgu_b_s', 'w_mem_kv', 'w_out', 'ln_mix_g', 'ln_mix_b', 'w_gate', 'w_up', 'w_down', 'ln_ffn_g', 'ln_ffn_b', 'loss_target', 'm_a_w_in', 'm_b_w_in', 'm_sgu_ln_g', 'm_sgu_ln_b', 'm_sgu_w_s', 'm_sgu_b_s', 'm_w_mem_kv', 'm_w_out', 'm_ln_mix_g', 'm_ln_mix_b', 'm_w_gate', 'm_w_up', 'm_w_down', 'm_ln_ffn_g', 'm_ln_ffn_b', 'v_a_w_in', 'v_b_w_in', 'v_sgu_ln_g', 'v_sgu_ln_b', 'v_sgu_w_s', 'v_sgu_b_s', 'v_w_mem_kv', 'v_w_out', 'v_ln_mix_g', 'v_ln_mix_b', 'v_w_gate', 'v_w_up', 'v_w_down', 'v_ln_ffn_g', 'v_ln_ffn_b']
TWIN_OUTPUTS = ['loss', 'grad_x', 'grad_a_w_in', 'grad_b_w_in', 'grad_sgu_ln_g', 'grad_sgu_ln_b', 'grad_sgu_w_s', 'grad_sgu_b_s', 'grad_w_mem_kv', 'grad_w_out', 'grad_ln_mix_g', 'grad_ln_mix_b', 'grad_w_gate', 'grad_w_up', 'grad_w_down', 'grad_ln_ffn_g', 'grad_ln_ffn_b', 'delta_a_w_in', 'delta_b_w_in', 'delta_sgu_ln_g', 'delta_sgu_ln_b', 'delta_sgu_w_s', 'delta_sgu_b_s', 'delta_w_mem_kv', 'delta_w_out', 'delta_ln_mix_g', 'delta_ln_mix_b', 'delta_w_gate', 'delta_w_up', 'delta_w_down', 'delta_ln_ffn_g', 'delta_ln_ffn_b', 'new_m_a_w_in', 'new_m_b_w_in', 'new_m_sgu_ln_g', 'new_m_sgu_ln_b', 'new_m_sgu_w_s', 'new_m_sgu_b_s', 'new_m_w_mem_kv', 'new_m_w_out', 'new_m_ln_mix_g', 'new_m_ln_mix_b', 'new_m_w_gate', 'new_m_w_up', 'new_m_w_down', 'new_m_ln_ffn_g', 'new_m_ln_ffn_b', 'new_v_a_w_in', 'new_v_b_w_in', 'new_v_sgu_ln_g', 'new_v_sgu_ln_b', 'new_v_sgu_w_s', 'new_v_sgu_b_s', 'new_v_w_mem_kv', 'new_v_w_out', 'new_v_ln_mix_g', 'new_v_ln_mix_b', 'new_v_w_gate', 'new_v_w_up', 'new_v_w_down', 'new_v_ln_ffn_g', 'new_v_ln_ffn_b']
TWIN_LEAF_KINDS = {'loss': 'loss', 'grad_x': 'grad_x', 'grad_a_w_in': 'grad_w', 'grad_b_w_in': 'grad_w', 'grad_sgu_ln_g': 'grad_w', 'grad_sgu_ln_b': 'grad_w', 'grad_sgu_w_s': 'grad_w', 'grad_sgu_b_s': 'grad_w', 'grad_w_mem_kv': 'grad_w', 'grad_w_out': 'grad_w', 'grad_ln_mix_g': 'grad_w', 'grad_ln_mix_b': 'grad_w', 'grad_w_gate': 'grad_w', 'grad_w_up': 'grad_w', 'grad_w_down': 'grad_w', 'grad_ln_ffn_g': 'grad_w', 'grad_ln_ffn_b': 'grad_w', 'delta_a_w_in': 'delta_w', 'delta_b_w_in': 'delta_w', 'delta_sgu_ln_g': 'delta_w', 'delta_sgu_ln_b': 'delta_w', 'delta_sgu_w_s': 'delta_w', 'delta_sgu_b_s': 'delta_w', 'delta_w_mem_kv': 'delta_w', 'delta_w_out': 'delta_w', 'delta_ln_mix_g': 'delta_w', 'delta_ln_mix_b': 'delta_w', 'delta_w_gate': 'delta_w', 'delta_w_up': 'delta_w', 'delta_w_down': 'delta_w', 'delta_ln_ffn_g': 'delta_w', 'delta_ln_ffn_b': 'delta_w', 'new_m_a_w_in': 'new_m', 'new_m_b_w_in': 'new_m', 'new_m_sgu_ln_g': 'new_m', 'new_m_sgu_ln_b': 'new_m', 'new_m_sgu_w_s': 'new_m', 'new_m_sgu_b_s': 'new_m', 'new_m_w_mem_kv': 'new_m', 'new_m_w_out': 'new_m', 'new_m_ln_mix_g': 'new_m', 'new_m_ln_mix_b': 'new_m', 'new_m_w_gate': 'new_m', 'new_m_w_up': 'new_m', 'new_m_w_down': 'new_m', 'new_m_ln_ffn_g': 'new_m', 'new_m_ln_ffn_b': 'new_m', 'new_v_a_w_in': 'new_v', 'new_v_b_w_in': 'new_v', 'new_v_sgu_ln_g': 'new_v', 'new_v_sgu_ln_b': 'new_v', 'new_v_sgu_w_s': 'new_v', 'new_v_sgu_b_s': 'new_v', 'new_v_w_mem_kv': 'new_v', 'new_v_w_out': 'new_v', 'new_v_ln_mix_g': 'new_v', 'new_v_ln_mix_b': 'new_v', 'new_v_w_gate': 'new_v', 'new_v_w_up': 'new_v', 'new_v_w_down': 'new_v', 'new_v_ln_ffn_g': 'new_v', 'new_v_ln_ffn_b': 'new_v'}


def _forward(args):
    return _fwd_reference(*[args[k] for k in FWD_PARAMS])


def _output_shape():
    out = _jax.eval_shape(lambda: _forward(_fwd_setup_inputs(0)))
    return out.shape, out.dtype

N_MICROBATCH = 1
ADAM_LR = 0.001
ADAM_B1 = 0.9
ADAM_B2 = 0.999
ADAM_EPS = 1e-08
ADAM_WD = 0.01
ADAM_STEP = 10
PER_EXAMPLE_BATCH_AXIS = {'x': 0, 'mem': 0, 'loss_target': 0}
SHARED_INPUTS = []
_WEIGHT_DTYPES = {'a_w_in': _jnp.float32, 'b_w_in': _jnp.float32, 'sgu_ln_g': _jnp.float32, 'sgu_ln_b': _jnp.float32, 'sgu_w_s': _jnp.float32, 'sgu_b_s': _jnp.float32, 'w_mem_kv': _jnp.float32, 'w_out': _jnp.float32, 'ln_mix_g': _jnp.float32, 'ln_mix_b': _jnp.float32, 'w_gate': _jnp.float32, 'w_up': _jnp.float32, 'w_down': _jnp.float32, 'ln_ffn_g': _jnp.float32, 'ln_ffn_b': _jnp.float32}
MOMENT_SCALE = {'a_w_in': 1.993175e-02, 'b_w_in': 4.097782e-02, 'sgu_ln_g': 2.930257e-02, 'sgu_ln_b': 2.916233e-02, 'sgu_w_s': 2.099987e-02, 'sgu_b_s': 2.914034e-02, 'w_mem_kv': 7.333232e-03, 'w_out': 9.905202e-02, 'ln_mix_g': 1.673984e+00, 'ln_mix_b': 7.679635e-01, 'w_gate': 2.360074e-02, 'w_up': 2.285556e-02, 'w_down': 9.038386e-02, 'ln_ffn_g': 3.212289e+01, 'ln_ffn_b': 2.910492e+00}


def _to_microbatches(a, axis):
    t = _jnp.moveaxis(a, axis, 0)
    t = t.reshape((N_MICROBATCH, t.shape[0] // N_MICROBATCH) + t.shape[1:])
    return _jnp.moveaxis(t, 1, axis + 1)


def setup_inputs(seed: int = 0) -> dict:
    inp = _fwd_setup_inputs(seed)
    key = _jax.random.fold_in(_jax.random.key(seed), 7919)
    shape, _ = _output_shape()
    out = dict(inp)
    out["loss_target"] = _jax.random.normal(_jax.random.fold_in(key, 0), shape, _jnp.float32)
    for i, name in enumerate(TWIN_WEIGHTS):
        w = inp[name].astype(_jnp.float32)
        if MOMENT_SCALE is None:
            s = _jnp.sqrt(_jnp.mean(_jnp.square(w)) + 1e-30)
        else:
            s = MOMENT_SCALE[name]
        km, kv = _jax.random.split(_jax.random.fold_in(key, i + 1))
        out[name] = w
        out["m_" + name] = s * _jax.random.normal(km, w.shape, _jnp.float32)
        out["v_" + name] = (s * s) * _jax.random.uniform(kv, w.shape, _jnp.float32, 0.5, 1.5)
    if N_MICROBATCH > 1:
        for name, axis in PER_EXAMPLE_BATCH_AXIS.items():
            out[name] = _to_microbatches(out[name], axis)
    return {'x': out['x'], 'mem': out['mem'], 'a_w_in': out['a_w_in'], 'b_w_in': out['b_w_in'], 'sgu_ln_g': out['sgu_ln_g'], 'sgu_ln_b': out['sgu_ln_b'], 'sgu_w_s': out['sgu_w_s'], 'sgu_b_s': out['sgu_b_s'], 'w_mem_kv': out['w_mem_kv'], 'w_out': out['w_out'], 'ln_mix_g': out['ln_mix_g'], 'ln_mix_b': out['ln_mix_b'], 'w_gate': out['w_gate'], 'w_up': out['w_up'], 'w_down': out['w_down'], 'ln_ffn_g': out['ln_ffn_g'], 'ln_ffn_b': out['ln_ffn_b'], 'loss_target': out['loss_target'], 'm_a_w_in': out['m_a_w_in'], 'm_b_w_in': out['m_b_w_in'], 'm_sgu_ln_g': out['m_sgu_ln_g'], 'm_sgu_ln_b': out['m_sgu_ln_b'], 'm_sgu_w_s': out['m_sgu_w_s'], 'm_sgu_b_s': out['m_sgu_b_s'], 'm_w_mem_kv': out['m_w_mem_kv'], 'm_w_out': out['m_w_out'], 'm_ln_mix_g': out['m_ln_mix_g'], 'm_ln_mix_b': out['m_ln_mix_b'], 'm_w_gate': out['m_w_gate'], 'm_w_up': out['m_w_up'], 'm_w_down': out['m_w_down'], 'm_ln_ffn_g': out['m_ln_ffn_g'], 'm_ln_ffn_b': out['m_ln_ffn_b'], 'v_a_w_in': out['v_a_w_in'], 'v_b_w_in': out['v_b_w_in'], 'v_sgu_ln_g': out['v_sgu_ln_g'], 'v_sgu_ln_b': out['v_sgu_ln_b'], 'v_sgu_w_s': out['v_sgu_w_s'], 'v_sgu_b_s': out['v_sgu_b_s'], 'v_w_mem_kv': out['v_w_mem_kv'], 'v_w_out': out['v_w_out'], 'v_ln_mix_g': out['v_ln_mix_g'], 'v_ln_mix_b': out['v_ln_mix_b'], 'v_w_gate': out['v_w_gate'], 'v_w_up': out['v_w_up'], 'v_w_down': out['v_w_down'], 'v_ln_ffn_g': out['v_ln_ffn_g'], 'v_ln_ffn_b': out['v_ln_ffn_b']}


def _loss(weights, diff, rest, loss_target):
    with _jax.named_scope("forward"):
        args = {**rest, TWIN_DIFF_INPUT: diff, **{k: w.astype(_WEIGHT_DTYPES[k]) for k, w in weights.items()}}
        y = _forward(args)
    with _jax.named_scope("loss_head"):
        err = _jnp.square(y.astype(_jnp.float32) - loss_target)
        return 0.5 * _jnp.sum(_jnp.mean(err, axis=-1)) if err.ndim else 0.5 * err


def _adamw(w, g, m, v):
    m = ADAM_B1 * m + (1.0 - ADAM_B1) * g
    v = ADAM_B2 * v + (1.0 - ADAM_B2) * _jnp.square(g)
    m_hat = m / (1.0 - ADAM_B1 ** ADAM_STEP)
    v_hat = v / (1.0 - ADAM_B2 ** ADAM_STEP)
    delta = -ADAM_LR * (m_hat / (_jnp.sqrt(v_hat) + ADAM_EPS) + ADAM_WD * w)
    return delta, m, v


def reference(x, mem, a_w_in, b_w_in, sgu_ln_g, sgu_ln_b, sgu_w_s, sgu_b_s, w_mem_kv, w_out, ln_mix_g, ln_mix_b, w_gate, w_up, w_down, ln_ffn_g, ln_ffn_b, loss_target, m_a_w_in, m_b_w_in, m_sgu_ln_g, m_sgu_ln_b, m_sgu_w_s, m_sgu_b_s, m_w_mem_kv, m_w_out, m_ln_mix_g, m_ln_mix_b, m_w_gate, m_w_up, m_w_down, m_ln_ffn_g, m_ln_ffn_b, v_a_w_in, v_b_w_in, v_sgu_ln_g, v_sgu_ln_b, v_sgu_w_s, v_sgu_b_s, v_w_mem_kv, v_w_out, v_ln_mix_g, v_ln_mix_b, v_w_gate, v_w_up, v_w_down, v_ln_ffn_g, v_ln_ffn_b):
    given = dict(x=x, mem=mem, a_w_in=a_w_in, b_w_in=b_w_in, sgu_ln_g=sgu_ln_g, sgu_ln_b=sgu_ln_b, sgu_w_s=sgu_w_s, sgu_b_s=sgu_b_s, w_mem_kv=w_mem_kv, w_out=w_out, ln_mix_g=ln_mix_g, ln_mix_b=ln_mix_b, w_gate=w_gate, w_up=w_up, w_down=w_down, ln_ffn_g=ln_ffn_g, ln_ffn_b=ln_ffn_b, loss_target=loss_target, m_a_w_in=m_a_w_in, m_b_w_in=m_b_w_in, m_sgu_ln_g=m_sgu_ln_g, m_sgu_ln_b=m_sgu_ln_b, m_sgu_w_s=m_sgu_w_s, m_sgu_b_s=m_sgu_b_s, m_w_mem_kv=m_w_mem_kv, m_w_out=m_w_out, m_ln_mix_g=m_ln_mix_g, m_ln_mix_b=m_ln_mix_b, m_w_gate=m_w_gate, m_w_up=m_w_up, m_w_down=m_w_down, m_ln_ffn_g=m_ln_ffn_g, m_ln_ffn_b=m_ln_ffn_b, v_a_w_in=v_a_w_in, v_b_w_in=v_b_w_in, v_sgu_ln_g=v_sgu_ln_g, v_sgu_ln_b=v_sgu_ln_b, v_sgu_w_s=v_sgu_w_s, v_sgu_b_s=v_sgu_b_s, v_w_mem_kv=v_w_mem_kv, v_w_out=v_w_out, v_ln_mix_g=v_ln_mix_g, v_ln_mix_b=v_ln_mix_b, v_w_gate=v_w_gate, v_w_up=v_w_up, v_w_down=v_w_down, v_ln_ffn_g=v_ln_ffn_g, v_ln_ffn_b=v_ln_ffn_b)
    weights = {n: given[n] for n in TWIN_WEIGHTS}
    shared = {n: given[n] for n in SHARED_INPUTS}
    per_example = {n: given[n] for n in ['x', 'mem']}
    grad_fn = _jax.value_and_grad(_loss, argnums=(0, 1))

    def one_microbatch(ex, loss_target):
        ex = dict(ex)
        diff = ex.pop(TWIN_DIFF_INPUT)
        return grad_fn(weights, diff, {**shared, **ex}, loss_target)

    if N_MICROBATCH == 1:
        loss, (grad_w, grad_x) = one_microbatch(per_example, given["loss_target"])
    else:
        def body(carry, xs):
            loss_sum, grad_sum = carry
            l_k, (gw_k, gx_k) = one_microbatch(xs[0], xs[1])
            with _jax.named_scope("update"):
                return (loss_sum + l_k, _jax.tree.map(_jnp.add, grad_sum, gw_k)), gx_k

        init = (_jnp.zeros((), _jnp.float32), _jax.tree.map(_jnp.zeros_like, weights))
        (loss, grad_w), grad_x = _jax.lax.scan(body, init, (per_example, given["loss_target"]))
    with _jax.named_scope("update"):
        delta_w, new_m, new_v = {}, {}, {}
        for n in TWIN_WEIGHTS:
            delta_w[n], new_m[n], new_v[n] = _adamw(weights[n], grad_w[n], given["m_" + n], given["v_" + n])
    return (loss, grad_x, *[grad_w[n] for n in TWIN_WEIGHTS], *[delta_w[n] for n in TWIN_WEIGHTS],
            *[new_m[n] for n in TWIN_WEIGHTS], *[new_v[n] for n in TWIN_WEIGHTS])
```

```python
import functools
import math

import numpy as np
import jax
import jax.numpy as jnp
from jax import lax
from jax.experimental import pallas as pl
from jax.experimental.pallas import tpu as pltpu

F32 = jnp.float32
BF16 = jnp.bfloat16

D_MODEL = 1024
DEPTH = 4
N_MEM = 256
HEAD_DIM = 64
N_HEADS = 12
MIX_W = N_HEADS * HEAD_DIM
MEM_W = 4 * HEAD_DIM
DIL_PATTERNS = ((128, 1), (512, 4), (2048, 16))
BLK = 128
HEAD_GROUP = 4
N_GROUPS = N_HEADS // HEAD_GROUP
D_FF = 2816
ALPHA = (2 * DEPTH) ** 0.25
LN_EPS = 1e-5
SCALE = HEAD_DIM ** -0.5
NEG = -1e30
N_DEV = 8

ADAM_LR, ADAM_B1, ADAM_B2, ADAM_EPS, ADAM_WD, ADAM_STEP = 0.001, 0.9, 0.999, 1e-08, 0.01, 10

VMEM_LIMIT = 56 * 2 ** 20
STAT_LANES = 32
STAT_W = N_HEADS * STAT_LANES


def _dot_nn(a, b):
    return lax.dot_general(a, b, (((1,), (0,)), ((), ())), preferred_element_type=F32)


def _dot_nt(a, b):
    return lax.dot_general(a, b, (((1,), (1,)), ((), ())), preferred_element_type=F32)


def _dot_tn(a, b):
    return lax.dot_general(a, b, (((0,), (0,)), ((), ())), preferred_element_type=F32)


def _ln_hat(r):
    mu = jnp.mean(r, axis=-1, keepdims=True)
    xc = r - mu
    var = jnp.mean(xc * xc, axis=-1, keepdims=True)
    rstd = lax.rsqrt(var + LN_EPS)
    return xc * rstd, rstd


def _params(sem):
    return pltpu.CompilerParams(dimension_semantics=sem, vmem_limit_bytes=VMEM_LIMIT)


def _rows(tm, c, col=0):
    return pl.BlockSpec((tm, c), lambda i: (i, col))


def _whole(shape):
    nd = len(shape)
    return pl.BlockSpec(tuple(shape), lambda *_: (0,) * nd)


def _resident(shape):
    nd = len(shape)
    return pl.BlockSpec(tuple(shape), lambda *_: (0,) * nd, pipeline_mode=pl.Buffered(1))


def _sds(shape, dtype):
    return jax.ShapeDtypeStruct(tuple(shape), dtype)


def _linear_nn(name, a, w, tm=512):
    t, k = a.shape
    n = w.shape[1]
    tm = min(tm, t)

    def body(a_ref, w_ref, o_ref):
        o_ref[...] = _dot_nn(a_ref[...], w_ref[...]).astype(BF16)

    return pl.pallas_call(
        body, name=name, grid=(t // tm,), in_specs=[_rows(tm, k), _resident(w.shape)], out_specs=_rows(tm, n),
        out_shape=_sds((t, n), BF16), compiler_params=_params(("parallel",)))(a, w)


def _proj_ln_fwd(name, lhs, w, x_res, g, b, tm=256):
    t = x_res.shape[0]
    n_lhs = len(lhs)

    def body(*refs):
        lhs_refs = refs[:n_lhs]
        w_ref, x_ref, g_ref, b_ref, r_ref, xn_ref, xnb_ref = refs[n_lhs:]
        y, off = None, 0
        for lr in lhs_refs:
            k = lr.shape[1]
            term = _dot_nn(lr[...], w_ref[off:off + k, :])
            y = term if y is None else y + term
            off += k
        r = ALPHA * x_ref[...] + y
        xhat, _ = _ln_hat(r)
        xn = xhat * g_ref[...] + b_ref[...]
        r_ref[...] = r
        xn_ref[...] = xn
        xnb_ref[...] = xn.astype(BF16)

    in_specs = [_rows(tm, a.shape[1]) for a in lhs] + [_resident(w.shape), _rows(tm, D_MODEL), _whole(g.shape), _whole(b.shape)]
    return pl.pallas_call(
        body, name=name, grid=(t // tm,), in_specs=in_specs,
        out_specs=[_rows(tm, D_MODEL)] * 3,
        out_shape=[_sds((t, D_MODEL), F32), _sds((t, D_MODEL), F32), _sds((t, D_MODEL), BF16)],
        compiler_params=_params(("parallel",)))(*lhs, w, x_res, g, b)


def _ffn_up_fwd(xb, wgu, tm=256):
    t = xb.shape[0]

    def body(x_ref, w_ref, g_ref, u_ref, a_ref):
        gu = _dot_nn(x_ref[...], w_ref[...])
        gt, up = gu[:, :D_FF], gu[:, D_FF:]
        g_ref[...] = gt.astype(BF16)
        u_ref[...] = up.astype(BF16)
        a_ref[...] = (gt * jax.nn.sigmoid(gt) * up).astype(BF16)

    return pl.pallas_call(
        body, name="ffn_up_fwd", grid=(t // tm,), in_specs=[_rows(tm, D_MODEL), _resident(wgu.shape)],
        out_specs=[_rows(tm, D_FF)] * 3, out_shape=[_sds((t, D_FF), BF16)] * 3,
        compiler_params=_params(("parallel",)))(xb, wgu)


def _loss_fwd_bwd(xn, tgt, tm=512):
    t = xn.shape[0]

    def body(x_ref, t_ref, dx_ref, l_ref):
        @pl.when(pl.program_id(0) == 0)
        def _():
            l_ref[...] = jnp.zeros_like(l_ref)

        e = x_ref[...] - t_ref[...]
        dx_ref[...] = e * (1.0 / D_MODEL)
        l_ref[...] += jnp.sum(e * e) * (0.5 / D_MODEL)

    return pl.pallas_call(
        body, name="loss", grid=(t // tm,), in_specs=[_rows(tm, D_MODEL)] * 2,
        out_specs=[_rows(tm, D_MODEL), _whole((1, 128))], out_shape=[_sds((t, D_MODEL), F32), _sds((1, 128), F32)],
        compiler_params=_params(("arbitrary",)))(xn, tgt)


def _ln_bwd(dx, r, g, tm=512):
    t = dx.shape[0]

    def body(dx_ref, r_ref, g_ref, dr_ref, drb_ref, dg_ref, db_ref):
        @pl.when(pl.program_id(0) == 0)
        def _():
            dg_ref[...] = jnp.zeros_like(dg_ref)
            db_ref[...] = jnp.zeros_like(db_ref)

        dxv = dx_ref[...]
        xhat, rstd = _ln_hat(r_ref[...])
        dxh = dxv * g_ref[...]
        m1 = jnp.mean(dxh, axis=-1, keepdims=True)
        m2 = jnp.mean(dxh * xhat, axis=-1, keepdims=True)
        dr = rstd * (dxh - m1 - xhat * m2)
        dr_ref[...] = dr
        drb_ref[...] = dr.astype(BF16)
        dg_ref[...] += jnp.sum(dxv * xhat, axis=0, keepdims=True)
        db_ref[...] += jnp.sum(dxv, axis=0, keepdims=True)

    return pl.pallas_call(
        body, name="ln_bwd", grid=(t // tm,), in_specs=[_rows(tm, D_MODEL), _rows(tm, D_MODEL), _whole(g.shape)],
        out_specs=[_rows(tm, D_MODEL), _rows(tm, D_MODEL), _whole((1, D_MODEL)), _whole((1, D_MODEL))],
        out_shape=[_sds((t, D_MODEL), F32), _sds((t, D_MODEL), BF16), _sds((1, D_MODEL), F32), _sds((1, D_MODEL), F32)],
        compiler_params=_params(("arbitrary",)))(dx, r, g)


def _ffn_down_bwd(drb, wd, gt, up, tm=256):
    t = drb.shape[0]

    def body(d_ref, w_ref, g_ref, u_ref, o_ref):
        da = _dot_nt(d_ref[...], w_ref[...])
        g = g_ref[...].astype(F32)
        u = u_ref[...].astype(F32)
        sg = jax.nn.sigmoid(g)
        o_ref[:, :D_FF] = (da * u * (sg * (1.0 + g * (1.0 - sg)))).astype(BF16)
        o_ref[:, D_FF:] = (da * (g * sg)).astype(BF16)

    return pl.pallas_call(
        body, name="ffn_down_bwd", grid=(t // tm,),
        in_specs=[_rows(tm, D_MODEL), _resident(wd.shape), _rows(tm, D_FF), _rows(tm, D_FF)],
        out_specs=_rows(tm, 2 * D_FF), out_shape=_sds((t, 2 * D_FF), BF16),
        compiler_params=_params(("parallel",)))(drb, wd, gt, up)


def _linear_nt(name, lhs, w, res, out_dtype, tm=256):
    t = lhs[0].shape[0]
    n_lhs = len(lhs)
    n_out = w.shape[0]

    def body(*refs):
        lhs_refs = refs[:n_lhs]
        w_ref = refs[n_lhs]
        o_ref = refs[-1]
        y, off = None, 0
        for lr in lhs_refs:
            k = lr.shape[1]
            term = _dot_nt(lr[...], w_ref[:, off:off + k])
            y = term if y is None else y + term
            off += k
        if res is not None:
            y = ALPHA * refs[n_lhs + 1][...] + y
        o_ref[...] = y.astype(out_dtype)

    in_specs = [_rows(tm, a.shape[1]) for a in lhs] + [_resident(w.shape)]
    args = list(lhs) + [w]
    if res is not None:
        in_specs.append(_rows(tm, n_out))
        args.append(res)
    return pl.pallas_call(
        body, name=name, grid=(t // tm,), in_specs=in_specs, out_specs=_rows(tm, n_out),
        out_shape=_sds((t, n_out), out_dtype), compiler_params=_params(("parallel",)))(*args)


def _pick_tile(n, limit):
    if n <= limit:
        return n
    best = 128
    for cand in range(128, limit + 1, 128):
        if n % cand == 0:
            best = cand
    return best


def _mm_tn(name, a, b, tt=1024):
    t, k = a.shape
    n = b.shape[1]
    tt = min(tt, t)
    tk = _pick_tile(k, 1408)
    tn = _pick_tile(n, (6 * 2 ** 20) // (4 * tk) // 128 * 128)

    def body(a_ref, b_ref, o_ref):
        @pl.when(pl.program_id(2) == 0)
        def _():
            o_ref[...] = jnp.zeros_like(o_ref)

        o_ref[...] += _dot_tn(a_ref[...], b_ref[...])

    return pl.pallas_call(
        body, name=name, grid=(k // tk, n // tn, t // tt),
        in_specs=[pl.BlockSpec((tt, tk), lambda i, j, s: (s, i)), pl.BlockSpec((tt, tn), lambda i, j, s: (s, j))],
        out_specs=pl.BlockSpec((tk, tn), lambda i, j, s: (i, j)), out_shape=_sds((k, n), F32),
        compiler_params=_params(("parallel", "parallel", "arbitrary")))(a, b)


def _alibi_table():
    arr = np.zeros((N_GROUPS, 8, 128), np.float32)
    for g in range(N_GROUPS):
        for hh in range(HEAD_GROUP):
            arr[g, hh, :] = 2.0 ** (-8.0 * (g * HEAD_GROUP + hh + 1) / N_HEADS)
    return jnp.asarray(arr)


def _band_mask(n, dil):
    qi = lax.broadcasted_iota(jnp.int32, (BLK, 2 * BLK), 0)
    ki = lax.broadcasted_iota(jnp.int32, (BLK, 2 * BLK), 1)
    steps = qi + BLK - ki
    valid = (steps >= 0) & (steps <= BLK) & ((ki >= BLK) | (n > 0))
    return valid, (steps * dil).astype(F32)


def _band_specs(bsz, seq, dil, width):
    cb = width // 256

    def spec(off, prev=False):
        if prev:
            return pl.BlockSpec((None, BLK, 256), lambda b, r, g, n: (b, jnp.maximum(n - 1, 0), r * cb + off + g))
        return pl.BlockSpec((None, BLK, 256), lambda b, r, g, n: (b, n, r * cb + off + g))

    return spec


def _spread_stats(cols):
    lane = lax.broadcasted_iota(jnp.int32, (BLK, HEAD_GROUP * STAT_LANES), 1)
    tile = cols[HEAD_GROUP - 1]
    for hh in range(HEAD_GROUP - 2, -1, -1):
        tile = jnp.where(lane < (hh + 1) * STAT_LANES, cols[hh], tile)
    return tile


def _band_attn_fwd(h, slopes, bsz, seq, dil):
    width = h.shape[1]
    length = seq // dil
    nblk = length // BLK
    hv = h.reshape(bsz, length, dil * width)
    spec = _band_specs(bsz, seq, dil, width)
    k_off, v_off = MIX_W // 256, 2 * MIX_W // 256

    def body(sl_ref, q_ref, kc_ref, kp_ref, vc_ref, vp_ref, o_ref, lse_ref):
        valid, dist = _band_mask(pl.program_id(3), dil)
        q = q_ref[...]
        k2 = jnp.concatenate([kp_ref[...], kc_ref[...]], axis=0)
        v2 = jnp.concatenate([vp_ref[...], vc_ref[...]], axis=0)
        lses = []
        for hh in range(HEAD_GROUP):
            sl = slice(hh * HEAD_DIM, (hh + 1) * HEAD_DIM)
            s = _dot_nt(q[:, sl], k2[:, sl]) * SCALE - sl_ref[hh:hh + 1, 0:1] * dist
            s = jnp.where(valid, s, NEG)
            m = jnp.max(s, axis=-1, keepdims=True)
            p = jnp.exp(s - m)
            l = jnp.sum(p, axis=-1, keepdims=True)
            acc = _dot_nn(p.astype(BF16), v2[:, sl])
            o_ref[:, sl] = (acc / l).astype(BF16)
            lses.append(m + jnp.log(l))
        lse_ref[...] = _spread_stats(lses)

    out, lse = pl.pallas_call(
        body, name=f"band_attn_fwd_d{dil}", grid=(bsz, dil, N_GROUPS, nblk),
        in_specs=[pl.BlockSpec((None, 8, 128), lambda b, r, g, n: (g, 0, 0)),
                  spec(0), spec(k_off), spec(k_off, True), spec(v_off), spec(v_off, True)],
        out_specs=[pl.BlockSpec((None, BLK, 256), lambda b, r, g, n: (b, n, r * N_GROUPS + g)),
                   pl.BlockSpec((None, BLK, 128), lambda b, r, g, n: (b, n, r * N_GROUPS + g))],
        out_shape=[_sds((bsz, length, dil * MIX_W), BF16), _sds((bsz, length, dil * STAT_W), F32)],
        compiler_params=_params(("parallel", "parallel", "parallel", "arbitrary")))(slopes, hv, hv, hv, hv, hv)
    return out.reshape(bsz * seq, MIX_W), lse.reshape(bsz * seq, STAT_W)


def _band_merge(outs, lses, tm=512):
    t = outs[0].shape[0]

    def body(o1, o2, o3, l1, l2, l3, mix_ref, lse_ref):
        ls = [l1[...], l2[...], l3[...]]
        m = jnp.maximum(jnp.maximum(ls[0], ls[1]), ls[2])
        tot = m + jnp.log(jnp.exp(ls[0] - m) + jnp.exp(ls[1] - m) + jnp.exp(ls[2] - m))
        ws = [jnp.exp(x - tot) for x in ls]
        lse_ref[...] = tot
        for hd in range(N_HEADS):
            sl = slice(hd * HEAD_DIM, (hd + 1) * HEAD_DIM)
            acc = None
            for w, o in zip(ws, (o1, o2, o3)):
                term = w[:, hd * STAT_LANES:hd * STAT_LANES + 1] * o[:, sl].astype(F32)
                acc = term if acc is None else acc + term
            mix_ref[:, sl] = acc.astype(BF16)

    return pl.pallas_call(
        body, name="band_merge", grid=(t // tm,), in_specs=[_rows(tm, MIX_W)] * 3 + [_rows(tm, STAT_W)] * 3,
        out_specs=[_rows(tm, MIX_W), _rows(tm, STAT_W)], out_shape=[_sds((t, MIX_W), BF16), _sds((t, STAT_W), F32)],
        compiler_params=_params(("parallel",)))(*outs, *lses)


def _band_delta(dcat, mix, tm=512):
    t = mix.shape[0]

    def body(d_ref, o_ref, dd_ref):
        prod = d_ref[...].astype(F32) * o_ref[...].astype(F32)
        for hd in range(N_HEADS):
            rsum = jnp.sum(prod[:, hd * HEAD_DIM:(hd + 1) * HEAD_DIM], axis=-1, keepdims=True)
            dd_ref[:, hd * STAT_LANES:(hd + 1) * STAT_LANES] = jnp.broadcast_to(rsum, (tm, STAT_LANES))

    return pl.pallas_call(
        body, name="band_delta", grid=(t // tm,), in_specs=[_rows(tm, MIX_W), _rows(tm, MIX_W)],
        out_specs=_rows(tm, STAT_W), out_shape=_sds((t, STAT_W), F32),
        compiler_params=_params(("parallel",)))(dcat, mix)


def _band_attn_bwd(h, dcat, slopes, lse, delta, bsz, seq, dil):
    width = h.shape[1]
    length = seq // dil
    nblk = length // BLK
    hv = h.reshape(bsz, length, dil * width)
    dv_ = dcat.reshape(bsz, length, dil * D_MODEL)
    k_off, v_off = MIX_W // 256, 2 * MIX_W // 256
    cb, dcb = width // 256, D_MODEL // 256

    def cur(off, c):
        return pl.BlockSpec((None, BLK, 256), lambda b, r, g, n: (b, jnp.minimum(n, nblk - 1), r * c + off + g))

    def prev(off, c):
        return pl.BlockSpec((None, BLK, 256), lambda b, r, g, n: (b, jnp.maximum(jnp.minimum(n, nblk - 1) - 1, 0), r * c + off + g))

    stat = pl.BlockSpec((None, BLK, 128), lambda b, r, g, n: (b, jnp.minimum(n, nblk - 1), r * N_GROUPS + g))
    dq_spec = pl.BlockSpec((None, BLK, 256), lambda b, r, g, n: (b, jnp.minimum(n, nblk - 1), r * N_GROUPS + g))
    dkv_spec = pl.BlockSpec((None, BLK, 256), lambda b, r, g, n: (b, jnp.maximum(n - 1, 0), r * N_GROUPS + g))

    def body(sl_ref, q_ref, kc_ref, kp_ref, vc_ref, vp_ref, do_ref, lse_ref, dd_ref, dq_ref, dk_ref, dv_ref, kcar, vcar):
        n = pl.program_id(3)

        @pl.when(n == 0)
        def _():
            kcar[...] = jnp.zeros_like(kcar)
            vcar[...] = jnp.zeros_like(vcar)

        @pl.when(n < nblk)
        def _():
            valid, dist = _band_mask(n, dil)
            q = q_ref[...]
            do = do_ref[...]
            k2 = jnp.concatenate([kp_ref[...], kc_ref[...]], axis=0)
            v2 = jnp.concatenate([vp_ref[...], vc_ref[...]], axis=0)
            for hh in range(HEAD_GROUP):
                sl = slice(hh * HEAD_DIM, (hh + 1) * HEAD_DIM)
                s = _dot_nt(q[:, sl], k2[:, sl]) * SCALE - sl_ref[hh:hh + 1, 0:1] * dist
                s = jnp.where(valid, s, NEG)
                st = slice(hh * STAT_LANES, hh * STAT_LANES + 1)
                p = jnp.exp(s - lse_ref[:, st])
                dp = _dot_nt(do[:, sl], v2[:, sl])
                ds = (p * (dp - dd_ref[:, st]) * SCALE).astype(BF16)
                dq_ref[:, sl] = _dot_nn(ds, k2[:, sl]).astype(BF16)
                dk2 = _dot_tn(ds, q[:, sl])
                dv2 = _dot_tn(p.astype(BF16), do[:, sl])
                dk_ref[:, sl] = (kcar[:, sl] + dk2[:BLK]).astype(BF16)
                dv_ref[:, sl] = (vcar[:, sl] + dv2[:BLK]).astype(BF16)
                kcar[:, sl] = dk2[BLK:]
                vcar[:, sl] = dv2[BLK:]

        @pl.when(n == nblk)
        def _():
            dk_ref[...] = kcar[...].astype(BF16)
            dv_ref[...] = vcar[...].astype(BF16)

    outs = pl.pallas_call(
        body, name=f"band_attn_bwd_d{dil}", grid=(bsz, dil, N_GROUPS, nblk + 1),
        in_specs=[pl.BlockSpec((None, 8, 128), lambda b, r, g, n: (g, 0, 0)),
                  cur(0, cb), cur(k_off, cb), prev(k_off, cb), cur(v_off, cb), prev(v_off, cb), cur(0, dcb), stat, stat],
        out_specs=[dq_spec, dkv_spec, dkv_spec],
        out_shape=[_sds((bsz, length, dil * MIX_W), BF16)] * 3,
        scratch_shapes=[pltpu.VMEM((BLK, 256), F32), pltpu.VMEM((BLK, 256), F32)],
        compiler_params=_params(("parallel", "parallel", "parallel", "arbitrary")))(
            slopes, hv, hv, hv, hv, hv, dv_, lse.reshape(bsz, length, dil * STAT_W), delta.reshape(bsz, length, dil * STAT_W))
    return [o.reshape(bsz * seq, MIX_W) for o in outs]


def _sum_patterns(parts, tm=512):
    t = parts[0][0].shape[0]

    def body(*refs):
        o_ref = refs[-1]
        for j in range(3):
            acc = refs[j][...].astype(F32) + refs[3 + j][...].astype(F32) + refs[6 + j][...].astype(F32)
            o_ref[:, j * MIX_W:(j + 1) * MIX_W] = acc.astype(BF16)

    flat = [x for p in parts for x in p]
    return pl.pallas_call(
        body, name="band_sum", grid=(t // tm,), in_specs=[_rows(tm, MIX_W)] * 9, out_specs=_rows(tm, 3 * MIX_W),
        out_shape=_sds((t, 3 * MIX_W), BF16), compiler_params=_params(("parallel",)))(*flat)


def _mem_attn_fwd(h, mkv, bsz, seq, q_col, tq=512):
    nq = seq // tq

    def body(q_ref, kv_ref, o_ref):
        q = q_ref[...]
        kv = kv_ref[...]
        for hh in range(4):
            sl = slice(hh * HEAD_DIM, (hh + 1) * HEAD_DIM)
            s = _dot_nt(q[:, sl], kv[:, sl]) * SCALE
            m = jnp.max(s, axis=-1, keepdims=True)
            p = jnp.exp(s - m)
            l = jnp.sum(p, axis=-1, keepdims=True)
            acc = _dot_nn(p.astype(BF16), kv[:, MEM_W + hh * HEAD_DIM:MEM_W + (hh + 1) * HEAD_DIM])
            o_ref[:, sl] = (acc / l).astype(BF16)

    return pl.pallas_call(
        body, name="mem_attn_fwd", grid=(bsz, nq),
        in_specs=[pl.BlockSpec((tq, MEM_W), lambda b, i: (b * nq + i, q_col)),
                  pl.BlockSpec((N_MEM, 2 * MEM_W), lambda b, i: (b, 0))],
        out_specs=pl.BlockSpec((tq, MEM_W), lambda b, i: (b * nq + i, 0)),
        out_shape=_sds((bsz * seq, MEM_W), BF16), compiler_params=_params(("parallel", "parallel")))(h, mkv)


def _mem_attn_bwd(h, mkv, dcat, bsz, seq, q_col, tq=512):
    nq = seq // tq
    do_col = MIX_W // MEM_W

    def body(q_ref, kv_ref, do_ref, dq_ref, dkv_ref):
        @pl.when(pl.program_id(1) == 0)
        def _():
            dkv_ref[...] = jnp.zeros_like(dkv_ref)

        q = q_ref[...]
        kv = kv_ref[...]
        do = do_ref[...]
        for hh in range(4):
            sl = slice(hh * HEAD_DIM, (hh + 1) * HEAD_DIM)
            vsl = slice(MEM_W + hh * HEAD_DIM, MEM_W + (hh + 1) * HEAD_DIM)
            s = _dot_nt(q[:, sl], kv[:, sl]) * SCALE
            m = jnp.max(s, axis=-1, keepdims=True)
            e = jnp.exp(s - m)
            p = e / jnp.sum(e, axis=-1, keepdims=True)
            dp = _dot_nt(do[:, sl], kv[:, vsl])
            dd = jnp.sum(p * dp, axis=-1, keepdims=True)
            ds = (p * (dp - dd) * SCALE).astype(BF16)
            dq_ref[:, sl] = _dot_nn(ds, kv[:, sl]).astype(BF16)
            dkv_ref[:, sl] += _dot_tn(ds, q[:, sl])
            dkv_ref[:, vsl] += _dot_tn(p.astype(BF16), do[:, sl])

    return pl.pallas_call(
        body, name="mem_attn_bwd", grid=(bsz, nq),
        in_specs=[pl.BlockSpec((tq, MEM_W), lambda b, i: (b * nq + i, q_col)),
                  pl.BlockSpec((N_MEM, 2 * MEM_W), lambda b, i: (b, 0)),
                  pl.BlockSpec((tq, MEM_W), lambda b, i: (b * nq + i, do_col))],
        out_specs=[pl.BlockSpec((tq, MEM_W), lambda b, i: (b * nq + i, 0)),
                   pl.BlockSpec((N_MEM, 2 * MEM_W), lambda b, i: (b, 0))],
        out_shape=[_sds((bsz * seq, MEM_W), BF16), _sds((bsz * N_MEM, 2 * MEM_W), F32)],
        compiler_params=_params(("parallel", "arbitrary")))(h, mkv, dcat)


_GELU_C = math.sqrt(2.0 / math.pi)
_GELU_A = 0.044715


def _gelu(x):
    return 0.5 * x * (1.0 + jnp.tanh(_GELU_C * (x + _GELU_A * x * x * x)))


def _gelu_grad(x):
    th = jnp.tanh(_GELU_C * (x + _GELU_A * x * x * x))
    return 0.5 * (1.0 + th) + 0.5 * x * (1.0 - th * th) * (_GELU_C * (1.0 + 3.0 * _GELU_A * x * x))


def _tril_mask(lower):
    ri = lax.broadcasted_iota(jnp.int32, (BLK, BLK), 0)
    ci = lax.broadcasted_iota(jnp.int32, (BLK, BLK), 1)
    return (ri >= ci) if lower else (ci >= ri)


def _sgu_fwd(h, ws, bs_t, ln_g, ln_b, tm=512):
    t = h.shape[0]

    def body(u_ref, v_ref, ws_ref, bs_ref, g_ref, b_ref, o_ref):
        ug = _gelu(u_ref[...].astype(F32))
        vhat, _ = _ln_hat(_gelu(v_ref[...].astype(F32)))
        vn = (vhat * g_ref[...] + b_ref[...]).astype(BF16)
        mask = _tril_mask(True)
        for g in range(N_HEADS):
            sl = slice(g * HEAD_DIM, (g + 1) * HEAD_DIM)
            w = jnp.where(mask, ws_ref[g], 0).astype(BF16)
            for c in range(tm // BLK):
                rs = slice(c * BLK, (c + 1) * BLK)
                mixed = _dot_nn(w, vn[rs, sl]) + bs_ref[:, g:g + 1]
                o_ref[rs, sl] = (ug[rs, sl] * mixed).astype(BF16)

    return pl.pallas_call(
        body, name="sgu_fwd", grid=(t // tm,),
        in_specs=[_rows(tm, MIX_W, 0), _rows(tm, MIX_W, 1), _whole(ws.shape), _whole(bs_t.shape), _whole(ln_g.shape), _whole(ln_b.shape)],
        out_specs=_rows(tm, MIX_W), out_shape=_sds((t, MIX_W), BF16),
        compiler_params=_params(("parallel",)))(h, h, ws, bs_t, ln_g, ln_b)


def _sgu_bwd(h, dcat, ws, ws_t, bs_t, ln_g, ln_b, tm=512):
    t = h.shape[0]

    def body(u_ref, v_ref, do_ref, ws_ref, wst_ref, bs_ref, g_ref, b_ref, dh_ref, dws_ref, dbs_ref, dg_ref, db_ref, dvn_ref):
        @pl.when(pl.program_id(0) == 0)
        def _():
            dws_ref[...] = jnp.zeros_like(dws_ref)
            dbs_ref[...] = jnp.zeros_like(dbs_ref)
            dg_ref[...] = jnp.zeros_like(dg_ref)
            db_ref[...] = jnp.zeros_like(db_ref)

        u = u_ref[...].astype(F32)
        v = v_ref[...].astype(F32)
        do = do_ref[...].astype(F32)
        ug = _gelu(u)
        vhat, rstd = _ln_hat(_gelu(v))
        vn = (vhat * g_ref[...] + b_ref[...]).astype(BF16)
        dmixed_f = do * ug
        dmixed = dmixed_f.astype(BF16)
        low, upp = _tril_mask(True), _tril_mask(False)
        for g in range(N_HEADS):
            sl = slice(g * HEAD_DIM, (g + 1) * HEAD_DIM)
            w = jnp.where(low, ws_ref[g], 0).astype(BF16)
            wt = jnp.where(upp, wst_ref[g], 0).astype(BF16)
            dws_acc = None
            dbs_acc = None
            for c in range(tm // BLK):
                rs = slice(c * BLK, (c + 1) * BLK)
                mixed = _dot_nn(w, vn[rs, sl]) + bs_ref[:, g:g + 1]
                dh_ref[rs, sl] = (do[rs, sl] * mixed * _gelu_grad(u[rs, sl])).astype(BF16)
                dm = dmixed[rs, sl]
                term = _dot_nt(dm, vn[rs, sl])
                dws_acc = term if dws_acc is None else dws_acc + term
                rsum = jnp.sum(dmixed_f[rs, sl], axis=-1, keepdims=True)
                dbs_acc = rsum if dbs_acc is None else dbs_acc + rsum
                dvn_ref[rs, sl] = _dot_nn(wt, dm)
            dws_ref[g] += jnp.where(low, dws_acc, 0.0)
            dbs_ref[:, g:g + 1] += dbs_acc
        dvn = dvn_ref[...]
        dg_ref[...] += jnp.sum(dvn * vhat, axis=0, keepdims=True)
        db_ref[...] += jnp.sum(dvn, axis=0, keepdims=True)
        dxh = dvn * g_ref[...]
        m1 = jnp.mean(dxh, axis=-1, keepdims=True)
        m2 = jnp.mean(dxh * vhat, axis=-1, keepdims=True)
        dvg = rstd * (dxh - m1 - vhat * m2)
        dh_ref[:, MIX_W:] = (dvg * _gelu_grad(v)).astype(BF16)

    return pl.pallas_call(
        body, name="sgu_bwd", grid=(t // tm,),
        in_specs=[_rows(tm, MIX_W, 0), _rows(tm, MIX_W, 1), _rows(tm, MIX_W, 0), _whole(ws.shape), _whole(ws_t.shape),
                  _whole(bs_t.shape), _whole(ln_g.shape), _whole(ln_b.shape)],
        out_specs=[_rows(tm, 2 * MIX_W), _whole(ws.shape), _whole(bs_t.shape), _whole((1, MIX_W)), _whole((1, MIX_W))],
        out_shape=[_sds((t, 2 * MIX_W), BF16), _sds(ws.shape, F32), _sds(bs_t.shape, F32), _sds((1, MIX_W), F32), _sds((1, MIX_W), F32)],
        scratch_shapes=[pltpu.VMEM((tm, MIX_W), F32)],
        compiler_params=_params(("arbitrary",)))(h, h, dcat, ws, ws_t, bs_t, ln_g, ln_b)


def _row_tile(rows, cols, itemsize=4, limit=2 ** 20):
    best = rows
    for cand in (4096, 2048, 1024, 512, 256, 128, 64, 32, 16):
        if rows % cand == 0 and rows > cand:
            best = cand
            if cand * cols * itemsize <= limit:
                break
    return best


def _adamw(w, m, v, grad=None, parts=None):
    rows, cols = w.shape
    tr = _row_tile(rows, cols)

    def body(w_ref, m_ref, v_ref, g_ref, go_ref, d_ref, nm_ref, nv_ref):
        if parts is None:
            gv = g_ref[...]
        else:
            gv = g_ref[0].astype(F32)
            for k in range(1, 4):
                gv = gv + g_ref[k].astype(F32)
        nm = ADAM_B1 * m_ref[...] + (1.0 - ADAM_B1) * gv
        nv = ADAM_B2 * v_ref[...] + (1.0 - ADAM_B2) * (gv * gv)
        m_hat = nm / (1.0 - ADAM_B1 ** ADAM_STEP)
        v_hat = nv / (1.0 - ADAM_B2 ** ADAM_STEP)
        go_ref[...] = gv
        d_ref[...] = -ADAM_LR * (m_hat / (jnp.sqrt(v_hat) + ADAM_EPS) + ADAM_WD * w_ref[...])
        nm_ref[...] = nm
        nv_ref[...] = nv

    spec = _rows(tr, cols)
    g_spec = spec if parts is None else pl.BlockSpec((4, tr, cols), lambda i: (0, i, 0))
    return pl.pallas_call(
        body, name="adamw" if parts is None else "adamw_sum_chips", grid=(rows // tr,), in_specs=[spec] * 3 + [g_spec],
        out_specs=[spec] * 4, out_shape=[_sds(w.shape, F32)] * 4,
        compiler_params=_params(("parallel",)))(w, m, v, grad if parts is None else parts)


_ANY = pl.BlockSpec(memory_space=pl.ANY)
_MESH = pl.DeviceIdType.MESH


def _all_gather(name, blocks):
    nt = len(blocks)

    def body(*refs):
        x_refs, out_refs = refs[:nt], refs[nt:2 * nt]
        send_sems, recv_sems, local_sems = refs[2 * nt:]
        x, y, c = lax.axis_index("x"), lax.axis_index("y"), lax.axis_index("c")
        me, sibling = (x, y, c), (x, y, 1 - c)
        chips = [(1 - x, y), (x, 1 - y), (1 - x, 1 - y)]

        def slot(t, px, py, pc):
            return out_refs[t].at[4 * px + 2 * py + pc]

        def copy(t, k, blk, to, src=None):
            return pltpu.make_async_remote_copy(
                src_ref=slot(t, *blk) if src is None else src, dst_ref=slot(t, *blk),
                send_sem=send_sems.at[t, k], recv_sem=recv_sems.at[t, k], device_id=to, device_id_type=_MESH)

        mine = [pltpu.make_async_copy(x_refs[t], slot(t, *me), local_sems.at[t]) for t in range(nt)]
        for cp in mine:
            cp.start()
        first = []
        for t in range(nt):
            first.append(copy(t, 0, me, sibling, src=x_refs[t]))
            first += [copy(t, 1 + j, me, (*chip, c), src=x_refs[t]) for j, chip in enumerate(chips)]
        for cp in first:
            cp.start()
        passed = []
        for j, chip in enumerate(chips):
            for t in range(nt):
                copy(t, 1 + j, (*chip, c), me).wait_recv()
                fwd = copy(t, 4 + j, (*chip, c), sibling)
                fwd.start()
                passed.append(fwd)
        for t in range(nt):
            copy(t, 0, sibling, me).wait_recv()
        for j, chip in enumerate(chips):
            for t in range(nt):
                copy(t, 4 + j, (*chip, 1 - c), me).wait_recv()
        for cp in first + passed:
            cp.wait_send()
        for cp in mine:
            cp.wait()

    return pl.pallas_call(
        body, name=name, out_shape=[_sds((N_DEV,) + b.shape, b.dtype) for b in blocks], in_specs=[_ANY] * nt,
        out_specs=[_ANY] * nt,
        scratch_shapes=[pltpu.SemaphoreType.DMA((nt, 7)), pltpu.SemaphoreType.DMA((nt, 7)), pltpu.SemaphoreType.DMA((nt,))])(*blocks)


def _swap_with_sibling(packed):
    nt = len(packed)

    def body(*refs):
        p_refs, got_refs = refs[:nt], refs[nt:2 * nt]
        send_sems, recv_sems = refs[2 * nt:]
        x, y, c = lax.axis_index("x"), lax.axis_index("y"), lax.axis_index("c")
        copies = [
            pltpu.make_async_remote_copy(
                src_ref=p_refs[t].at[1 - c], dst_ref=got_refs[t], send_sem=send_sems.at[t], recv_sem=recv_sems.at[t],
                device_id=(x, y, 1 - c), device_id_type=_MESH)
            for t in range(nt)]
        for cp in copies:
            cp.start()
        for cp in copies:
            cp.wait_recv()
        for cp in copies:
            cp.wait_send()

    return pl.pallas_call(
        body, name="grad_swap_sibling", out_shape=[_sds(p.shape[1:], p.dtype) for p in packed], in_specs=[_ANY] * nt,
        out_specs=[_ANY] * nt,
        scratch_shapes=[pltpu.SemaphoreType.DMA((nt,)), pltpu.SemaphoreType.DMA((nt,))])(*packed)


def _chip_sum(packed, got):
    _, nchip, rows, cols = packed.shape
    tr = _row_tile(rows, cols, 2)
    core = lax.axis_index("c").astype(jnp.int32).reshape(1)

    def body(c_ref, p_ref, g_ref, o_ref):
        o_ref[...] = (p_ref[...].astype(F32) + g_ref[...].astype(F32)).astype(o_ref.dtype)

    grid_spec = pltpu.PrefetchScalarGridSpec(
        num_scalar_prefetch=1, grid=(nchip, rows // tr),
        in_specs=[pl.BlockSpec((None, None, tr, cols), lambda k, i, c: (c[0], k, i, 0)),
                  pl.BlockSpec((None, tr, cols), lambda k, i, c: (k, i, 0))],
        out_specs=pl.BlockSpec((None, tr, cols), lambda k, i, c: (k, i, 0)))
    return pl.pallas_call(
        body, name="grad_chip_sum", grid_spec=grid_spec, out_shape=_sds(got.shape, got.dtype),
        compiler_params=_params(("parallel", "parallel")))(core, packed, got)


def _exchange_chips(chip_sums):
    nt = len(chip_sums)

    def body(*refs):
        s_refs, got_refs = refs[:nt], refs[nt:2 * nt]
        send_sems, recv_sems, local_sems = refs[2 * nt:]
        x, y, c = lax.axis_index("x"), lax.axis_index("y"), lax.axis_index("c")
        my_chip = 2 * x + y
        chips = [(1 - x, y), (x, 1 - y), (1 - x, 1 - y)]
        mine = [pltpu.make_async_copy(s_refs[t].at[my_chip], got_refs[t].at[my_chip], local_sems.at[t]) for t in range(nt)]
        for cp in mine:
            cp.start()

        def copy(t, j, src_chip, dst_chip):
            px, py = chips[j]
            return pltpu.make_async_remote_copy(
                src_ref=s_refs[t].at[src_chip], dst_ref=got_refs[t].at[dst_chip], send_sem=send_sems.at[t, j],
                recv_sem=recv_sems.at[t, j], device_id=(px, py, c), device_id_type=_MESH)

        sends = [copy(t, j, 2 * px + py, my_chip) for t in range(nt) for j, (px, py) in enumerate(chips)]
        for cp in sends:
            cp.start()
        for j, (px, py) in enumerate(chips):
            for t in range(nt):
                copy(t, j, my_chip, 2 * px + py).wait_recv()
        for cp in sends:
            cp.wait_send()
        for cp in mine:
            cp.wait()

    return pl.pallas_call(
        body, name="grad_exchange_chips", out_shape=[_sds(s.shape, s.dtype) for s in chip_sums], in_specs=[_ANY] * nt,
        out_specs=[_ANY] * nt,
        scratch_shapes=[pltpu.SemaphoreType.DMA((nt, 3)), pltpu.SemaphoreType.DMA((nt, 3)), pltpu.SemaphoreType.DMA((nt,))])(*chip_sums)


def _sum_chips(got):
    _, rows, cols = got.shape
    tr = _row_tile(rows, cols)

    def body(g_ref, o_ref):
        acc = g_ref[0].astype(F32)
        for k in range(1, 4):
            acc = acc + g_ref[k].astype(F32)
        o_ref[...] = acc

    return pl.pallas_call(
        body, name="grad_sum_chips", grid=(rows // tr,), in_specs=[pl.BlockSpec((4, tr, cols), lambda i: (0, i, 0))],
        out_specs=pl.BlockSpec((tr, cols), lambda i: (i, 0)), out_shape=_sds((rows, cols), F32),
        compiler_params=_params(("parallel",)))(got)


_COL_SHARDED = ("a_w_in", "b_w_in", "w_gate", "w_up")
_ROW_SHARDED = ("w_mem_kv", "w_out", "w_down")
_BIG = ("a_w_in", "b_w_in", "w_mem_kv", "w_out", "w_gate", "w_up", "w_down")
_SGU_LN = ("sgu_ln_g", "sgu_ln_b")
_LN4 = ("ln_mix_g", "ln_mix_b", "ln_ffn_g", "ln_ffn_b")
_REPLICATED = ("sgu_w_s", "sgu_b_s") + _LN4


def _unshard(name, gathered):
    if name in _COL_SHARDED or name in _SGU_LN:
        moved = jnp.moveaxis(gathered, 0, -2)
        return moved.reshape(moved.shape[:-2] + (moved.shape[-2] * moved.shape[-1],))
    moved = jnp.moveaxis(gathered, 0, 1)
    return moved.reshape((moved.shape[0], moved.shape[1] * moved.shape[2]) + moved.shape[3:])


def _by_shard(name, full):
    if name in _COL_SHARDED or name in _SGU_LN:
        split = full.reshape(full.shape[:-1] + (N_DEV, full.shape[-1] // N_DEV))
        return jnp.moveaxis(split, -2, 0)
    split = full.reshape((full.shape[0], N_DEV, full.shape[1] // N_DEV) + full.shape[2:])
    return jnp.moveaxis(split, 1, 0)


def _gather_weights(shards):
    gathered = _all_gather("weights_all_gather", [shards[n].astype(BF16) for n in _BIG])
    return {n: _unshard(n, g) for n, g in zip(_BIG, gathered)}


def _gather_small_f32(named):
    names = list(named)
    gathered = _all_gather("small_all_gather", [named[n] for n in names])
    return {n: _unshard(n, g) for n, g in zip(names, gathered)}


def _two_level(by_dest):
    shp = by_dest.shape[1:]
    split = by_dest.astype(BF16).reshape((4, 2) + shp).swapaxes(0, 1)
    return split.reshape(2, 4, int(np.prod(shp[:-1])), shp[-1])


def _reduce_gradients(grads):
    sharded = _BIG + _SGU_LN
    packed = [_two_level(_by_shard(n, grads[n])) for n in sharded]
    ln4 = jnp.stack([grads[n] for n in _LN4])
    rep = [grads["sgu_w_s"].reshape(N_DEV, -1, BLK), grads["sgu_b_s"].reshape(N_DEV, -1, BLK), ln4.reshape(N_DEV, -1, D_MODEL)]
    packed += [_two_level(r) for r in rep]
    got = _swap_with_sibling(packed)
    sums = [_chip_sum(p, g) for p, g in zip(packed, got)]
    parts = _exchange_chips(sums)
    out_parts = dict(zip(sharded, parts[:len(sharded)]))
    mine = [_sum_chips(p) for p in parts[len(sharded):]]
    w_s, b_s, ln_all = _all_gather("replicated_grads_all_gather", mine)
    ln_all = ln_all.reshape(ln4.shape)
    rep_grads = {"sgu_w_s": w_s.reshape(grads["sgu_w_s"].shape), "sgu_b_s": b_s.reshape(grads["sgu_b_s"].shape)}
    rep_grads.update({n: ln_all[i] for i, n in enumerate(_LN4)})
    return out_parts, rep_grads


def _as_2d(a):
    if a.ndim == 1:
        return a.reshape(1, -1)
    return a.reshape(-1, a.shape[-1])


def kernel(x, mem, a_w_in, b_w_in, sgu_ln_g, sgu_ln_b, sgu_w_s, sgu_b_s, w_mem_kv, w_out, ln_mix_g, ln_mix_b, w_gate, w_up, w_down, ln_ffn_g, ln_ffn_b, loss_target, m_a_w_in, m_b_w_in, m_sgu_ln_g, m_sgu_ln_b, m_sgu_w_s, m_sgu_b_s, m_w_mem_kv, m_w_out, m_ln_mix_g, m_ln_mix_b, m_w_gate, m_w_up, m_w_down, m_ln_ffn_g, m_ln_ffn_b, v_a_w_in, v_b_w_in, v_sgu_ln_g, v_sgu_ln_b, v_sgu_w_s, v_sgu_b_s, v_w_mem_kv, v_w_out, v_ln_mix_g, v_ln_mix_b, v_w_gate, v_w_up, v_w_down, v_ln_ffn_g, v_ln_ffn_b):
    names = ("a_w_in", "b_w_in", "sgu_ln_g", "sgu_ln_b", "sgu_w_s", "sgu_b_s", "w_mem_kv", "w_out", "ln_mix_g", "ln_mix_b",
             "w_gate", "w_up", "w_down", "ln_ffn_g", "ln_ffn_b")
    weights = dict(zip(names, (a_w_in, b_w_in, sgu_ln_g, sgu_ln_b, sgu_w_s, sgu_b_s, w_mem_kv, w_out, ln_mix_g, ln_mix_b,
                               w_gate, w_up, w_down, ln_ffn_g, ln_ffn_b)))
    mom_m = dict(zip(names, (m_a_w_in, m_b_w_in, m_sgu_ln_g, m_sgu_ln_b, m_sgu_w_s, m_sgu_b_s, m_w_mem_kv, m_w_out, m_ln_mix_g,
                             m_ln_mix_b, m_w_gate, m_w_up, m_w_down, m_ln_ffn_g, m_ln_ffn_b)))
    mom_v = dict(zip(names, (v_a_w_in, v_b_w_in, v_sgu_ln_g, v_sgu_ln_b, v_sgu_w_s, v_sgu_b_s, v_w_mem_kv, v_w_out, v_ln_mix_g,
                             v_ln_mix_b, v_w_gate, v_w_up, v_w_down, v_ln_ffn_g, v_ln_ffn_b)))
    full = _gather_weights(weights)
    sgu_ln = _gather_small_f32({n: weights[n] for n in _SGU_LN})
    loss_part, grad_x, local = _local_step(x, mem, loss_target, full, sgu_ln, {n: weights[n] for n in _REPLICATED})
    loss = lax.psum(loss_part[0, 0], ("x", "y", "c"))
    parts, rep_grads = _reduce_gradients(local)

    reduced, deltas, new_m, new_v = {}, {}, {}, {}
    for n in names:
        w2, m2, v2 = _as_2d(weights[n]), _as_2d(mom_m[n]), _as_2d(mom_v[n])
        if n in parts:
            outs = _adamw(w2, m2, v2, parts=parts[n])
        else:
            outs = _adamw(w2, m2, v2, grad=_as_2d(rep_grads[n]))
        reduced[n], deltas[n], new_m[n], new_v[n] = (a.reshape(weights[n].shape) for a in outs)

    return (loss, grad_x, *[reduced[n] for n in names], *[deltas[n] for n in names],
            *[new_m[n] for n in names], *[new_v[n] for n in names])


def _local_step(x, mem, loss_target, full, sgu_ln, small):
    sgu_w_s, sgu_b_s = small["sgu_w_s"], small["sgu_b_s"]
    ln_mix_g, ln_mix_b, ln_ffn_g, ln_ffn_b = (small[n] for n in ("ln_mix_g", "ln_mix_b", "ln_ffn_g", "ln_ffn_b"))
    bsz, seq, _ = x.shape
    tokens = bsz * seq
    w_gu = jnp.concatenate([full["w_gate"], full["w_up"]], axis=-1)
    slopes = _alibi_table()

    xf = x.reshape(tokens, D_MODEL)
    xb = xf.astype(BF16)
    memb = mem.reshape(bsz * N_MEM, D_MODEL).astype(BF16)
    tgt = loss_target.reshape(tokens, D_MODEL)

    saved = []
    for i in range(DEPTH):
        j = i // 2
        dil_layer = i % 2 == 0
        w_in = full["a_w_in"][j] if dil_layer else full["b_w_in"][j]
        mkv = _linear_nn("mem_kv", memb, full["w_mem_kv"][i])
        h = _linear_nn("in_proj_a" if dil_layer else "in_proj_b", xb, w_in)
        st = dict(x=xf, xb=xb, h=h, mkv=mkv)
        if dil_layer:
            outs, lses = [], []
            for _, dil in DIL_PATTERNS:
                o, lse = _band_attn_fwd(h, slopes, bsz, seq, dil)
                outs.append(o)
                lses.append(lse)
            mix, st["lse"] = _band_merge(outs, lses)
            q_col = 3 * MIX_W // MEM_W
        else:
            st["ws"] = sgu_w_s[j]
            st["bs_t"] = sgu_b_s[j].T
            st["ln_g"] = sgu_ln["sgu_ln_g"][j].reshape(1, MIX_W)
            st["ln_b"] = sgu_ln["sgu_ln_b"][j].reshape(1, MIX_W)
            mix = _sgu_fwd(h, st["ws"], st["bs_t"], st["ln_g"], st["ln_b"])
            q_col = 2 * MIX_W // MEM_W
        mo = _mem_attn_fwd(h, mkv, bsz, seq, q_col)
        r1, x1, x1b = _proj_ln_fwd("out_proj_ln", [mix, mo], full["w_out"][i], xf,
                                   ln_mix_g[i].reshape(1, D_MODEL), ln_mix_b[i].reshape(1, D_MODEL))
        gt, up, act = _ffn_up_fwd(x1b, w_gu[i])
        r2, x2, x2b = _proj_ln_fwd("ffn_down_ln", [act], full["w_down"][i], x1,
                                   ln_ffn_g[i].reshape(1, D_MODEL), ln_ffn_b[i].reshape(1, D_MODEL))
        st.update(mix=mix, mo=mo, q_col=q_col, r1=r1, x1b=x1b, gt=gt, up=up, act=act, r2=r2)
        saved.append(st)
        xf, xb = x2, x2b

    dx, loss_part = _loss_fwd_bwd(xf, tgt)

    per_pair = ("a_w_in", "b_w_in", "sgu_ln_g", "sgu_ln_b", "sgu_w_s", "sgu_b_s")
    grads = {n: [None] * (DEPTH // 2 if n in per_pair else DEPTH) for n in _BIG + _SGU_LN + _REPLICATED}
    for i in reversed(range(DEPTH)):
        j = i // 2
        st = saved[i]
        dil_layer = i % 2 == 0
        w_in = full["a_w_in"][j] if dil_layer else full["b_w_in"][j]
        dr2, dr2b, dg, db = _ln_bwd(dx, st["r2"], ln_ffn_g[i].reshape(1, D_MODEL))
        grads["ln_ffn_g"][i], grads["ln_ffn_b"][i] = dg[0], db[0]
        dgu = _ffn_down_bwd(dr2b, full["w_down"][i], st["gt"], st["up"])
        grads["w_down"][i] = _mm_tn("grad_w_down", st["act"], dr2b)
        dx1 = _linear_nt("ffn_up_bwd", [dgu], w_gu[i], dr2, F32)
        dw_gu = _mm_tn("grad_w_gate_up", st["x1b"], dgu)
        grads["w_gate"][i], grads["w_up"][i] = dw_gu[:, :D_FF], dw_gu[:, D_FF:]
        dr1, dr1b, dg, db = _ln_bwd(dx1, st["r1"], ln_mix_g[i].reshape(1, D_MODEL))
        grads["ln_mix_g"][i], grads["ln_mix_b"][i] = dg[0], db[0]
        dcat = _linear_nt("out_proj_bwd", [dr1b], full["w_out"][i], None, BF16)
        grads["w_out"][i] = jnp.concatenate(
            [_mm_tn("grad_w_out_mix", st["mix"], dr1b), _mm_tn("grad_w_out_mem", st["mo"], dr1b)], axis=0)
        dqm, dmkv = _mem_attn_bwd(st["h"], st["mkv"], dcat, bsz, seq, st["q_col"])
        grads["w_mem_kv"][i] = _mm_tn("grad_w_mem_kv", memb, dmkv.astype(BF16))
        if dil_layer:
            delta = _band_delta(dcat, st["mix"])
            parts = []
            for _, dil in DIL_PATTERNS:
                parts.append(_band_attn_bwd(st["h"], dcat, slopes, st["lse"], delta, bsz, seq, dil))
            dh_main = _sum_patterns(parts)
        else:
            ws_t = jnp.swapaxes(st["ws"], -1, -2)
            dh_main, dws, dbs_t, dlg, dlb = _sgu_bwd(st["h"], dcat, st["ws"], ws_t, st["bs_t"], st["ln_g"], st["ln_b"])
            grads["sgu_w_s"][j], grads["sgu_b_s"][j] = dws, dbs_t.T
            grads["sgu_ln_g"][j], grads["sgu_ln_b"][j] = dlg[0], dlb[0]
        dx = _linear_nt("in_proj_bwd_a" if dil_layer else "in_proj_bwd_b", [dh_main, dqm], w_in, dr1, F32)
        grads["a_w_in" if dil_layer else "b_w_in"][j] = jnp.concatenate(
            [_mm_tn("grad_w_in_main_a" if dil_layer else "grad_w_in_main_b", st["xb"], dh_main),
             _mm_tn("grad_w_in_qm", st["xb"], dqm)], axis=1)
    return loss_part, dx.reshape(x.shape), {n: jnp.stack(g) for n, g in grads.items()}
```

```python
import functools
import math

import numpy as np
import jax
import jax.numpy as jnp
from jax import lax
from jax.experimental import pallas as pl
from jax.experimental.pallas import tpu as pltpu

F32 = jnp.float32
BF16 = jnp.bfloat16

D_MODEL = 1024
DEPTH = 4
N_MEM = 256
HEAD_DIM = 64
N_HEADS = 12
MIX_W = N_HEADS * HEAD_DIM
MEM_W = 4 * HEAD_DIM
DIL_PATTERNS = ((128, 1), (512, 4), (2048, 16))
BLK = 128
HEAD_GROUP = 4
N_GROUPS = N_HEADS // HEAD_GROUP
D_FF = 2816
ALPHA = (2 * DEPTH) ** 0.25
LN_EPS = 1e-5
SCALE = HEAD_DIM ** -0.5
NEG = -1e30
N_DEV = 8

ADAM_LR, ADAM_B1, ADAM_B2, ADAM_EPS, ADAM_WD, ADAM_STEP = 0.001, 0.9, 0.999, 1e-08, 0.01, 10

VMEM_LIMIT = 56 * 2 ** 20
STAT_LANES = 32
STAT_W = N_HEADS * STAT_LANES


def _dot_nn(a, b):
    return lax.dot_general(a, b, (((1,), (0,)), ((), ())), preferred_element_type=F32)


def _dot_nt(a, b):
    return lax.dot_general(a, b, (((1,), (1,)), ((), ())), preferred_element_type=F32)


def _dot_tn(a, b):
    return lax.dot_general(a, b, (((0,), (0,)), ((), ())), preferred_element_type=F32)


def _ln_hat(r):
    mu = jnp.mean(r, axis=-1, keepdims=True)
    xc = r - mu
    var = jnp.mean(xc * xc, axis=-1, keepdims=True)
    rstd = lax.rsqrt(var + LN_EPS)
    return xc * rstd, rstd


def _params(sem):
    return pltpu.CompilerParams(dimension_semantics=sem, vmem_limit_bytes=VMEM_LIMIT)


def _rows(tm, c, col=0):
    return pl.BlockSpec((tm, c), lambda i: (i, col))


def _whole(shape):
    nd = len(shape)
    return pl.BlockSpec(tuple(shape), lambda *_: (0,) * nd)


def _resident(shape):
    nd = len(shape)
    return pl.BlockSpec(tuple(shape), lambda *_: (0,) * nd, pipeline_mode=pl.Buffered(1))


def _sds(shape, dtype):
    return jax.ShapeDtypeStruct(tuple(shape), dtype)


def _linear_nn(name, a, w, tm=512):
    t, k = a.shape
    n = w.shape[1]
    tm = min(tm, t)

    def body(a_ref, w_ref, o_ref):
        o_ref[...] = _dot_nn(a_ref[...], w_ref[...]).astype(BF16)

    return pl.pallas_call(
        body, name=name, grid=(t // tm,), in_specs=[_rows(tm, k), _resident(w.shape)], out_specs=_rows(tm, n),
        out_shape=_sds((t, n), BF16), compiler_params=_params(("parallel",)))(a, w)


def _proj_ln_fwd(name, lhs, w, x_res, g, b, tm=256):
    t = x_res.shape[0]
    n_lhs = len(lhs)

    def body(*refs):
        lhs_refs = refs[:n_lhs]
        w_ref, x_ref, g_ref, b_ref, r_ref, xn_ref, xnb_ref = refs[n_lhs:]
        y, off = None, 0
        for lr in lhs_refs:
            k = lr.shape[1]
            term = _dot_nn(lr[...], w_ref[off:off + k, :])
            y = term if y is None else y + term
            off += k
        r = ALPHA * x_ref[...] + y
        xhat, _ = _ln_hat(r)
        xn = xhat * g_ref[...] + b_ref[...]
        r_ref[...] = r
        xn_ref[...] = xn
        xnb_ref[...] = xn.astype(BF16)

    in_specs = [_rows(tm, a.shape[1]) for a in lhs] + [_resident(w.shape), _rows(tm, D_MODEL), _whole(g.shape), _whole(b.shape)]
    return pl.pallas_call(
        body, name=name, grid=(t // tm,), in_specs=in_specs,
        out_specs=[_rows(tm, D_MODEL)] * 3,
        out_shape=[_sds((t, D_MODEL), F32), _sds((t, D_MODEL), F32), _sds((t, D_MODEL), BF16)],
        compiler_params=_params(("parallel",)))(*lhs, w, x_res, g, b)


def _ffn_up_fwd(xb, wgu, tm=256):
    t = xb.shape[0]

    def body(x_ref, w_ref, g_ref, u_ref, a_ref):
        gu = _dot_nn(x_ref[...], w_ref[...])
        gt, up = gu[:, :D_FF], gu[:, D_FF:]
        g_ref[...] = gt.astype(BF16)
        u_ref[...] = up.astype(BF16)
        a_ref[...] = (gt * jax.nn.sigmoid(gt) * up).astype(BF16)

    return pl.pallas_call(
        body, name="ffn_up_fwd", grid=(t // tm,), in_specs=[_rows(tm, D_MODEL), _resident(wgu.shape)],
        out_specs=[_rows(tm, D_FF)] * 3, out_shape=[_sds((t, D_FF), BF16)] * 3,
        compiler_params=_params(("parallel",)))(xb, wgu)


def _loss_fwd_bwd(xn, tgt, tm=512):
    t = xn.shape[0]

    def body(x_ref, t_ref, dx_ref, l_ref):
        @pl.when(pl.program_id(0) == 0)
        def _():
            l_ref[...] = jnp.zeros_like(l_ref)

        e = x_ref[...] - t_ref[...]
        dx_ref[...] = e * (1.0 / D_MODEL)
        l_ref[...] += jnp.sum(e * e) * (0.5 / D_MODEL)

    return pl.pallas_call(
        body, name="loss", grid=(t // tm,), in_specs=[_rows(tm, D_MODEL)] * 2,
        out_specs=[_rows(tm, D_MODEL), _whole((1, 128))], out_shape=[_sds((t, D_MODEL), F32), _sds((1, 128), F32)],
        compiler_params=_params(("arbitrary",)))(xn, tgt)


def _ln_bwd(dx, r, g, tm=512):
    t = dx.shape[0]

    def body(dx_ref, r_ref, g_ref, dr_ref, drb_ref, dg_ref, db_ref):
        @pl.when(pl.program_id(0) == 0)
        def _():
            dg_ref[...] = jnp.zeros_like(dg_ref)
            db_ref[...] = jnp.zeros_like(db_ref)

        dxv = dx_ref[...]
        xhat, rstd = _ln_hat(r_ref[...])
        dxh = dxv * g_ref[...]
        m1 = jnp.mean(dxh, axis=-1, keepdims=True)
        m2 = jnp.mean(dxh * xhat, axis=-1, keepdims=True)
        dr = rstd * (dxh - m1 - xhat * m2)
        dr_ref[...] = dr
        drb_ref[...] = dr.astype(BF16)
        dg_ref[...] += jnp.sum(dxv * xhat, axis=0, keepdims=True)
        db_ref[...] += jnp.sum(dxv, axis=0, keepdims=True)

    return pl.pallas_call(
        body, name="ln_bwd", grid=(t // tm,), in_specs=[_rows(tm, D_MODEL), _rows(tm, D_MODEL), _whole(g.shape)],
        out_specs=[_rows(tm, D_MODEL), _rows(tm, D_MODEL), _whole((1, D_MODEL)), _whole((1, D_MODEL))],
        out_shape=[_sds((t, D_MODEL), F32), _sds((t, D_MODEL), BF16), _sds((1, D_MODEL), F32), _sds((1, D_MODEL), F32)],
        compiler_params=_params(("arbitrary",)))(dx, r, g)


def _ffn_down_bwd(drb, wd, gt, up, tm=256):
    t = drb.shape[0]

    def body(d_ref, w_ref, g_ref, u_ref, o_ref):
        da = _dot_nt(d_ref[...], w_ref[...])
        g = g_ref[...].astype(F32)
        u = u_ref[...].astype(F32)
        sg = jax.nn.sigmoid(g)
        o_ref[:, :D_FF] = (da * u * (sg * (1.0 + g * (1.0 - sg)))).astype(BF16)
        o_ref[:, D_FF:] = (da * (g * sg)).astype(BF16)

    return pl.pallas_call(
        body, name="ffn_down_bwd", grid=(t // tm,),
        in_specs=[_rows(tm, D_MODEL), _resident(wd.shape), _rows(tm, D_FF), _rows(tm, D_FF)],
        out_specs=_rows(tm, 2 * D_FF), out_shape=_sds((t, 2 * D_FF), BF16),
        compiler_params=_params(("parallel",)))(drb, wd, gt, up)


def _linear_nt(name, lhs, w, res, out_dtype, tm=256):
    t = lhs[0].shape[0]
    n_lhs = len(lhs)
    n_out = w.shape[0]

    def body(*refs):
        lhs_refs = refs[:n_lhs]
        w_ref = refs[n_lhs]
        o_ref = refs[-1]
        y, off = None, 0
        for lr in lhs_refs:
            k = lr.shape[1]
            term = _dot_nt(lr[...], w_ref[:, off:off + k])
            y = term if y is None else y + term
            off += k
        if res is not None:
            y = ALPHA * refs[n_lhs + 1][...] + y
        o_ref[...] = y.astype(out_dtype)

    in_specs = [_rows(tm, a.shape[1]) for a in lhs] + [_resident(w.shape)]
    args = list(lhs) + [w]
    if res is not None:
        in_specs.append(_rows(tm, n_out))
        args.append(res)
    return pl.pallas_call(
        body, name=name, grid=(t // tm,), in_specs=in_specs, out_specs=_rows(tm, n_out),
        out_shape=_sds((t, n_out), out_dtype), compiler_params=_params(("parallel",)))(*args)


def _pick_tile(n, limit):
    if n <= limit:
        return n
    best = 128
    for cand in range(128, limit + 1, 128):
        if n % cand == 0:
            best = cand
    return best


def _mm_tn(name, a, b, tt=1024):
    t, k = a.shape
    n = b.shape[1]
    tt = min(tt, t)
    tk = _pick_tile(k, 1408)
    tn = _pick_tile(n, (6 * 2 ** 20) // (4 * tk) // 128 * 128)

    def body(a_ref, b_ref, o_ref):
        @pl.when(pl.program_id(2) == 0)
        def _():
            o_ref[...] = jnp.zeros_like(o_ref)

        o_ref[...] += _dot_tn(a_ref[...], b_ref[...])

    return pl.pallas_call(
        body, name=name, grid=(k // tk, n // tn, t // tt),
        in_specs=[pl.BlockSpec((tt, tk), lambda i, j, s: (s, i)), pl.BlockSpec((tt, tn), lambda i, j, s: (s, j))],
        out_specs=pl.BlockSpec((tk, tn), lambda i, j, s: (i, j)), out_shape=_sds((k, n), F32),
        compiler_params=_params(("parallel", "parallel", "arbitrary")))(a, b)


def _alibi_table():
    arr = np.zeros((N_GROUPS, 8, 128), np.float32)
    for g in range(N_GROUPS):
        for hh in range(HEAD_GROUP):
            arr[g, hh, :] = 2.0 ** (-8.0 * (g * HEAD_GROUP + hh + 1) / N_HEADS)
    return jnp.asarray(arr)


def _band_mask(n, dil):
    qi = lax.broadcasted_iota(jnp.int32, (BLK, 2 * BLK), 0)
    ki = lax.broadcasted_iota(jnp.int32, (BLK, 2 * BLK), 1)
    steps = qi + BLK - ki
    valid = (steps >= 0) & (steps <= BLK) & ((ki >= BLK) | (n > 0))
    return valid, (steps * dil).astype(F32)


def _band_specs(bsz, seq, dil, width):
    cb = width // 256

    def spec(off, prev=False):
        if prev:
            return pl.BlockSpec((None, BLK, 256), lambda b, r, g, n: (b, jnp.maximum(n - 1, 0), r * cb + off + g))
        return pl.BlockSpec((None, BLK, 256), lambda b, r, g, n: (b, n, r * cb + off + g))

    return spec


def _spread_stats(cols):
    lane = lax.broadcasted_iota(jnp.int32, (BLK, HEAD_GROUP * STAT_LANES), 1)
    tile = cols[HEAD_GROUP - 1]
    for hh in range(HEAD_GROUP - 2, -1, -1):
        tile = jnp.where(lane < (hh + 1) * STAT_LANES, cols[hh], tile)
    return tile


def _band_attn_fwd(h, slopes, bsz, seq, dil):
    width = h.shape[1]
    length = seq // dil
    nblk = length // BLK
    hv = h.reshape(bsz, length, dil * width)
    spec = _band_specs(bsz, seq, dil, width)
    k_off, v_off = MIX_W // 256, 2 * MIX_W // 256

    def body(sl_ref, q_ref, kc_ref, kp_ref, vc_ref, vp_ref, o_ref, lse_ref):
        valid, dist = _band_mask(pl.program_id(3), dil)
        q = q_ref[...]
        k2 = jnp.concatenate([kp_ref[...], kc_ref[...]], axis=0)
        v2 = jnp.concatenate([vp_ref[...], vc_ref[...]], axis=0)
        lses = []
        for hh in range(HEAD_GROUP):
            sl = slice(hh * HEAD_DIM, (hh + 1) * HEAD_DIM)
            s = _dot_nt(q[:, sl], k2[:, sl]) * SCALE - sl_ref[hh:hh + 1, 0:1] * dist
            s = jnp.where(valid, s, NEG)
            m = jnp.max(s, axis=-1, keepdims=True)
            p = jnp.exp(s - m)
            l = jnp.sum(p, axis=-1, keepdims=True)
            acc = _dot_nn(p.astype(BF16), v2[:, sl])
            o_ref[:, sl] = (acc / l).astype(BF16)
            lses.append(m + jnp.log(l))
        lse_ref[...] = _spread_stats(lses)

    out, lse = pl.pallas_call(
        body, name=f"band_attn_fwd_d{dil}", grid=(bsz, dil, N_GROUPS, nblk),
        in_specs=[pl.BlockSpec((None, 8, 128), lambda b, r, g, n: (g, 0, 0)),
                  spec(0), spec(k_off), spec(k_off, True), spec(v_off), spec(v_off, True)],
        out_specs=[pl.BlockSpec((None, BLK, 256), lambda b, r, g, n: (b, n, r * N_GROUPS + g)),
                   pl.BlockSpec((None, BLK, 128), lambda b, r, g, n: (b, n, r * N_GROUPS + g))],
        out_shape=[_sds((bsz, length, dil * MIX_W), BF16), _sds((bsz, length, dil * STAT_W), F32)],
        compiler_params=_params(("parallel", "parallel", "parallel", "arbitrary")))(slopes, hv, hv, hv, hv, hv)
    return out.reshape(bsz * seq, MIX_W), lse.reshape(bsz * seq, STAT_W)


def _band_merge(outs, lses, tm=512):
    t = outs[0].shape[0]

    def body(o1, o2, o3, l1, l2, l3, mix_ref, lse_ref):
        ls = [l1[...], l2[...], l3[...]]
        m = jnp.maximum(jnp.maximum(ls[0], ls[1]), ls[2])
        tot = m + jnp.log(jnp.exp(ls[0] - m) + jnp.exp(ls[1] - m) + jnp.exp(ls[2] - m))
        ws = [jnp.exp(x - tot) for x in ls]
        lse_ref[...] = tot
        for hd in range(N_HEADS):
            sl = slice(hd * HEAD_DIM, (hd + 1) * HEAD_DIM)
            acc = None
            for w, o in zip(ws, (o1, o2, o3)):
                term = w[:, hd * STAT_LANES:hd * STAT_LANES + 1] * o[:, sl].astype(F32)
                acc = term if acc is None else acc + term
            mix_ref[:, sl] = acc.astype(BF16)

    return pl.pallas_call(
        body, name="band_merge", grid=(t // tm,), in_specs=[_rows(tm, MIX_W)] * 3 + [_rows(tm, STAT_W)] * 3,
        out_specs=[_rows(tm, MIX_W), _rows(tm, STAT_W)], out_shape=[_sds((t, MIX_W), BF16), _sds((t, STAT_W), F32)],
        compiler_params=_params(("parallel",)))(*outs, *lses)


def _band_delta(dcat, mix, tm=512):
    t = mix.shape[0]

    def body(d_ref, o_ref, dd_ref):
        prod = d_ref[...].astype(F32) * o_ref[...].astype(F32)
        for hd in range(N_HEADS):
            rsum = jnp.sum(prod[:, hd * HEAD_DIM:(hd + 1) * HEAD_DIM], axis=-1, keepdims=True)
            dd_ref[:, hd * STAT_LANES:(hd + 1) * STAT_LANES] = jnp.broadcast_to(rsum, (tm, STAT_LANES))

    return pl.pallas_call(
        body, name="band_delta", grid=(t // tm,), in_specs=[_rows(tm, MIX_W), _rows(tm, MIX_W)],
        out_specs=_rows(tm, STAT_W), out_shape=_sds((t, STAT_W), F32),
        compiler_params=_params(("parallel",)))(dcat, mix)


def _band_attn_bwd(h, dcat, slopes, lse, delta, bsz, seq, dil):
    width = h.shape[1]
    length = seq // dil
    nblk = length // BLK
    hv = h.reshape(bsz, length, dil * width)
    dv_ = dcat.reshape(bsz, length, dil * D_MODEL)
    k_off, v_off = MIX_W // 256, 2 * MIX_W // 256
    cb, dcb = width // 256, D_MODEL // 256

    def cur(off, c):
        return pl.BlockSpec((None, BLK, 256), lambda b, r, g, n: (b, jnp.minimum(n, nblk - 1), r * c + off + g))

    def prev(off, c):
        return pl.BlockSpec((None, BLK, 256), lambda b, r, g, n: (b, jnp.maximum(jnp.minimum(n, nblk - 1) - 1, 0), r * c + off + g))

    stat = pl.BlockSpec((None, BLK, 128), lambda b, r, g, n: (b, jnp.minimum(n, nblk - 1), r * N_GROUPS + g))
    dq_spec = pl.BlockSpec((None, BLK, 256), lambda b, r, g, n: (b, jnp.minimum(n, nblk - 1), r * N_GROUPS + g))
    dkv_spec = pl.BlockSpec((None, BLK, 256), lambda b, r, g, n: (b, jnp.maximum(n - 1, 0), r * N_GROUPS + g))

    def body(sl_ref, q_ref, kc_ref, kp_ref, vc_ref, vp_ref, do_ref, lse_ref, dd_ref, dq_ref, dk_ref, dv_ref, kcar, vcar):
        n = pl.program_id(3)

        @pl.when(n == 0)
        def _():
            kcar[...] = jnp.zeros_like(kcar)
            vcar[...] = jnp.zeros_like(vcar)

        @pl.when(n < nblk)
        def _():
            valid, dist = _band_mask(n, dil)
            q = q_ref[...]
            do = do_ref[...]
            k2 = jnp.concatenate([kp_ref[...], kc_ref[...]], axis=0)
            v2 = jnp.concatenate([vp_ref[...], vc_ref[...]], axis=0)
            for hh in range(HEAD_GROUP):
                sl = slice(hh * HEAD_DIM, (hh + 1) * HEAD_DIM)
                s = _dot_nt(q[:, sl], k2[:, sl]) * SCALE - sl_ref[hh:hh + 1, 0:1] * dist
                s = jnp.where(valid, s, NEG)
                st = slice(hh * STAT_LANES, hh * STAT_LANES + 1)
                p = jnp.exp(s - lse_ref[:, st])
                dp = _dot_nt(do[:, sl], v2[:, sl])
                ds = (p * (dp - dd_ref[:, st]) * SCALE).astype(BF16)
                dq_ref[:, sl] = _dot_nn(ds, k2[:, sl]).astype(BF16)
                dk2 = _dot_tn(ds, q[:, sl])
                dv2 = _dot_tn(p.astype(BF16), do[:, sl])
                dk_ref[:, sl] = (kcar[:, sl] + dk2[:BLK]).astype(BF16)
                dv_ref[:, sl] = (vcar[:, sl] + dv2[:BLK]).astype(BF16)
                kcar[:, sl] = dk2[BLK:]
                vcar[:, sl] = dv2[BLK:]

        @pl.when(n == nblk)
        def _():
            dk_ref[...] = kcar[...].astype(BF16)
            dv_ref[...] = vcar[...].astype(BF16)

    outs = pl.pallas_call(
        body, name=f"band_attn_bwd_d{dil}", grid=(bsz, dil, N_GROUPS, nblk + 1),
        in_specs=[pl.BlockSpec((None, 8, 128), lambda b, r, g, n: (g, 0, 0)),
                  cur(0, cb), cur(k_off, cb), prev(k_off, cb), cur(v_off, cb), prev(v_off, cb), cur(0, dcb), stat, stat],
        out_specs=[dq_spec, dkv_spec, dkv_spec],
        out_shape=[_sds((bsz, length, dil * MIX_W), BF16)] * 3,
        scratch_shapes=[pltpu.VMEM((BLK, 256), F32), pltpu.VMEM((BLK, 256), F32)],
        compiler_params=_params(("parallel", "parallel", "parallel", "arbitrary")))(
            slopes, hv, hv, hv, hv, hv, dv_, lse.reshape(bsz, length, dil * STAT_W), delta.reshape(bsz, length, dil * STAT_W))
    return [o.reshape(bsz * seq, MIX_W) for o in outs]


def _sum_patterns(parts, tm=512):
    t = parts[0][0].shape[0]

    def body(*refs):
        o_ref = refs[-1]
        for j in range(3):
            acc = refs[j][...].astype(F32) + refs[3 + j][...].astype(F32) + refs[6 + j][...].astype(F32)
            o_ref[:, j * MIX_W:(j + 1) * MIX_W] = acc.astype(BF16)

    flat = [x for p in parts for x in p]
    return pl.pallas_call(
        body, name="band_sum", grid=(t // tm,), in_specs=[_rows(tm, MIX_W)] * 9, out_specs=_rows(tm, 3 * MIX_W),
        out_shape=_sds((t, 3 * MIX_W), BF16), compiler_params=_params(("parallel",)))(*flat)


def _block_mask(has_prev, dil):
    qi = lax.broadcasted_iota(jnp.int32, (BLK, 2 * BLK), 0)
    ki = lax.broadcasted_iota(jnp.int32, (BLK, 2 * BLK), 1)
    steps = qi + BLK - ki
    valid = (steps >= 0) & (steps <= BLK) & ((ki >= BLK) | has_prev)
    return valid, (steps * dil).astype(F32)


def _rows_of(j):
    return pl.ds(pl.multiple_of(j * BLK, BLK), BLK)


def _lane_half(hf):
    return slice(hf * 128, (hf + 1) * 128)


def _deinterleave(src, dst, seq, dil, dtype):
    length = seq // dil
    for r in range(dil):
        for c in range(length // BLK):
            rows = pl.ds(r + c * BLK * dil, BLK, stride=dil)
            out = slice(r * length + c * BLK, r * length + (c + 1) * BLK)
            if len(src.shape) == 2:
                dst[out, :] = src[rows, :].astype(dtype)
            else:
                for hf in range(2):
                    dst[out, _lane_half(hf)] = src.at[hf][rows, :].astype(dtype)


def _interleave(src, dst, seq, dil, accumulate):
    length = seq // dil
    for r in range(dil):
        for c in range(length // BLK):
            rows = pl.ds(r + c * BLK * dil, BLK, stride=dil)
            inp = slice(r * length + c * BLK, r * length + (c + 1) * BLK)
            if len(dst.shape) == 2:
                dst[rows, :] = dst[rows, :] + src[inp, :] if accumulate else src[inp, :]
            else:
                for hf in range(2):
                    val = src[inp, _lane_half(hf)]
                    half = dst.at[hf]
                    half[rows, :] = half[rows, :] + val if accumulate else val


def _split_halves(src, dst, seq):
    def step(i, carry):
        for hf in range(2):
            dst[hf, _rows_of(i), :] = src[_rows_of(i), _lane_half(hf)].astype(F32)
        return carry

    lax.fori_loop(0, seq // BLK, step, 0)


def _band_attn_fwd_fused(h, slopes, bsz, seq):
    width = h.shape[1]
    cb = width // 256
    k_off, v_off = MIX_W // 256, 2 * MIX_W // 256
    nb = seq // BLK

    def body(sl_ref, q_ref, k_ref, v_ref, mix_ref, lse_ref, qf, kf, vf, qd, kd, vd, od, ld, o1, o2, o3, l1, l2, l3):
        def run(dil, qs, ks, vs, o_dst, l_dst):
            nblk = seq // dil // BLK

            def block(j, carry):
                rows, prows = _rows_of(j), _rows_of(jnp.maximum(j - 1, 0))
                valid, dist = _block_mask((j % nblk) != 0, dil)
                q = qs[rows, :]
                k2 = jnp.concatenate([ks[prows, :], ks[rows, :]], axis=0)
                v2 = jnp.concatenate([vs[prows, :], vs[rows, :]], axis=0)
                lses = []
                for hh in range(HEAD_GROUP):
                    sl = slice(hh * HEAD_DIM, (hh + 1) * HEAD_DIM)
                    s = _dot_nt(q[:, sl], k2[:, sl]) * SCALE - sl_ref[hh:hh + 1, 0:1] * dist
                    s = jnp.where(valid, s, NEG)
                    m = jnp.max(s, axis=-1, keepdims=True)
                    p = jnp.exp(s - m)
                    l = jnp.sum(p, axis=-1, keepdims=True)
                    o_dst[rows, sl] = _dot_nn(p.astype(BF16), v2[:, sl]) / l
                    lses.append(m + jnp.log(l))
                l_dst[rows, :] = _spread_stats(lses)
                return carry

            lax.fori_loop(0, nb, block, 0)

        run(1, q_ref, k_ref, v_ref, o1, l1)
        _split_halves(q_ref, qf, seq)
        _split_halves(k_ref, kf, seq)
        _split_halves(v_ref, vf, seq)
        for dil, o_tok, l_tok in ((4, o2, l2), (16, o3, l3)):
            _deinterleave(qf, qd, seq, dil, BF16)
            _deinterleave(kf, kd, seq, dil, BF16)
            _deinterleave(vf, vd, seq, dil, BF16)
            run(dil, qd, kd, vd, od, ld)
            _interleave(od, o_tok, seq, dil, False)
            _interleave(ld, l_tok, seq, dil, False)

        def merge(i, carry):
            rows = _rows_of(i)
            ls = [l1[rows, :], l2[rows, :], l3[rows, :]]
            m = jnp.maximum(jnp.maximum(ls[0], ls[1]), ls[2])
            tot = m + jnp.log(jnp.exp(ls[0] - m) + jnp.exp(ls[1] - m) + jnp.exp(ls[2] - m))
            ws = [jnp.exp(x - tot) for x in ls]
            lse_ref[rows, :] = tot
            for hh in range(HEAD_GROUP):
                sl = slice(hh * HEAD_DIM, (hh + 1) * HEAD_DIM)
                st = slice(hh * STAT_LANES, hh * STAT_LANES + 1)
                hf, hl = hh // 2, slice((hh % 2) * HEAD_DIM, (hh % 2 + 1) * HEAD_DIM)
                acc = ws[0][:, st] * o1[rows, sl] + ws[1][:, st] * o2[hf, rows, hl] + ws[2][:, st] * o3[hf, rows, hl]
                mix_ref[rows, sl] = acc.astype(BF16)
            return carry

        lax.fori_loop(0, nb, merge, 0)

    def hspec(off):
        return pl.BlockSpec((seq, 256), lambda b, g: (b, off + g))

    big = lambda dt: pltpu.VMEM((seq, 256), dt)
    halves = lambda: pltpu.VMEM((2, seq, 128), F32)
    stat = lambda: pltpu.VMEM((seq, 128), F32)
    return pl.pallas_call(
        body, name="band_attn_fwd", grid=(bsz, N_GROUPS),
        in_specs=[pl.BlockSpec((None, 8, 128), lambda b, g: (g, 0, 0)), hspec(0), hspec(k_off), hspec(v_off)],
        out_specs=[pl.BlockSpec((seq, 256), lambda b, g: (b, g)), pl.BlockSpec((seq, 128), lambda b, g: (b, g))],
        out_shape=[_sds((bsz * seq, MIX_W), BF16), _sds((bsz * seq, STAT_W), F32)],
        scratch_shapes=[halves(), halves(), halves(), big(BF16), big(BF16), big(BF16), big(F32), stat(),
                        big(F32), halves(), halves(), stat(), stat(), stat()],
        compiler_params=_params(("parallel", "parallel")))(slopes, h, h, h)


def _band_attn_bwd_fused(h, dcat, mix, lse, slopes, bsz, seq):
    width = h.shape[1]
    k_off, v_off = MIX_W // 256, 2 * MIX_W // 256
    nb = seq // BLK

    def body(sl_ref, q_ref, k_ref, v_ref, do_ref, o_ref, lse_ref, dq_ref, dk_ref, dv_ref,
             qf, kf, vf, dof, ddt, qd, kd, vd, dod, lsd, ddd, gq, gk, gv, aq, ak, av):
        def delta(i, carry):
            rows = _rows_of(i)
            prod = do_ref[rows, :].astype(F32) * o_ref[rows, :].astype(F32)
            ddt[rows, :] = _spread_stats(
                [jnp.sum(prod[:, hh * HEAD_DIM:(hh + 1) * HEAD_DIM], axis=-1, keepdims=True) for hh in range(HEAD_GROUP)])
            return carry

        lax.fori_loop(0, nb, delta, 0)

        def zero(i, carry):
            rows = _rows_of(i)
            for ref in (gk, gv):
                ref[rows, :] = jnp.zeros((BLK, 256), F32)
            return carry

        def run(dil, qs, ks, vs, dos, lss, dds):
            nblk = seq // dil // BLK
            lax.fori_loop(0, nb, zero, 0)

            def block(j, carry):
                rows, prows = _rows_of(j), _rows_of(jnp.maximum(j - 1, 0))
                valid, dist = _block_mask((j % nblk) != 0, dil)
                q = qs[rows, :]
                do = dos[rows, :]
                k2 = jnp.concatenate([ks[prows, :], ks[rows, :]], axis=0)
                v2 = jnp.concatenate([vs[prows, :], vs[rows, :]], axis=0)
                for hh in range(HEAD_GROUP):
                    sl = slice(hh * HEAD_DIM, (hh + 1) * HEAD_DIM)
                    st = slice(hh * STAT_LANES, hh * STAT_LANES + 1)
                    s = _dot_nt(q[:, sl], k2[:, sl]) * SCALE - sl_ref[hh:hh + 1, 0:1] * dist
                    s = jnp.where(valid, s, NEG)
                    p = jnp.exp(s - lss[rows, st])
                    dp = _dot_nt(do[:, sl], v2[:, sl])
                    ds = (p * (dp - dds[rows, st]) * SCALE).astype(BF16)
                    gq[rows, sl] = _dot_nn(ds, k2[:, sl])
                    dk2 = _dot_tn(ds, q[:, sl])
                    dv2 = _dot_tn(p.astype(BF16), do[:, sl])
                    gk[prows, sl] += dk2[:BLK]
                    gv[prows, sl] += dv2[:BLK]
                    gk[rows, sl] += dk2[BLK:]
                    gv[rows, sl] += dv2[BLK:]
                return carry

            lax.fori_loop(0, nb, block, 0)

        run(1, q_ref, k_ref, v_ref, do_ref, lse_ref, ddt)

        for src, dst in ((gq, aq), (gk, ak), (gv, av), (q_ref, qf), (k_ref, kf), (v_ref, vf), (do_ref, dof)):
            _split_halves(src, dst, seq)
        for dil in (4, 16):
            for src, dst in ((qf, qd), (kf, kd), (vf, vd), (dof, dod)):
                _deinterleave(src, dst, seq, dil, BF16)
            _deinterleave(lse_ref, lsd, seq, dil, F32)
            _deinterleave(ddt, ddd, seq, dil, F32)
            run(dil, qd, kd, vd, dod, lsd, ddd)
            for src, dst in ((gq, aq), (gk, ak), (gv, av)):
                _interleave(src, dst, seq, dil, True)

        def write(i, carry):
            rows = _rows_of(i)
            for src, dst in ((aq, dq_ref), (ak, dk_ref), (av, dv_ref)):
                for hf in range(2):
                    dst[rows, _lane_half(hf)] = src[hf, rows, :].astype(BF16)
            return carry

        lax.fori_loop(0, nb, write, 0)

    def hspec(off):
        return pl.BlockSpec((seq, 256), lambda b, g: (b, off + g))

    io = pl.BlockSpec((seq, 256), lambda b, g: (b, g))
    big = lambda dt: pltpu.VMEM((seq, 256), dt)
    halves = lambda: pltpu.VMEM((2, seq, 128), F32)
    stat = lambda: pltpu.VMEM((seq, 128), F32)
    return pl.pallas_call(
        body, name="band_attn_bwd", grid=(bsz, N_GROUPS),
        in_specs=[pl.BlockSpec((None, 8, 128), lambda b, g: (g, 0, 0)), hspec(0), hspec(k_off), hspec(v_off), io, io,
                  pl.BlockSpec((seq, 128), lambda b, g: (b, g))],
        out_specs=[io, io, io], out_shape=[_sds((bsz * seq, MIX_W), BF16)] * 3,
        scratch_shapes=[halves(), halves(), halves(), halves(), stat(),
                        big(BF16), big(BF16), big(BF16), big(BF16), stat(), stat(),
                        big(F32), big(F32), big(F32), halves(), halves(), halves()],
        compiler_params=_params(("parallel", "parallel")))(slopes, h, h, h, dcat, mix, lse)


def _mem_attn_fwd(h, mkv, bsz, seq, q_col, tq=512):
    nq = seq // tq

    def body(q_ref, kv_ref, o_ref):
        q = q_ref[...]
        kv = kv_ref[...]
        for hh in range(4):
            sl = slice(hh * HEAD_DIM, (hh + 1) * HEAD_DIM)
            s = _dot_nt(q[:, sl], kv[:, sl]) * SCALE
            m = jnp.max(s, axis=-1, keepdims=True)
            p = jnp.exp(s - m)
            l = jnp.sum(p, axis=-1, keepdims=True)
            acc = _dot_nn(p.astype(BF16), kv[:, MEM_W + hh * HEAD_DIM:MEM_W + (hh + 1) * HEAD_DIM])
            o_ref[:, sl] = (acc / l).astype(BF16)

    return pl.pallas_call(
        body, name="mem_attn_fwd", grid=(bsz, nq),
        in_specs=[pl.BlockSpec((tq, MEM_W), lambda b, i: (b * nq + i, q_col)),
                  pl.BlockSpec((N_MEM, 2 * MEM_W), lambda b, i: (b, 0))],
        out_specs=pl.BlockSpec((tq, MEM_W), lambda b, i: (b * nq + i, 0)),
        out_shape=_sds((bsz * seq, MEM_W), BF16), compiler_params=_params(("parallel", "parallel")))(h, mkv)


def _mem_attn_bwd(h, mkv, dcat, bsz, seq, q_col, tq=512):
    nq = seq // tq
    do_col = MIX_W // MEM_W

    def body(q_ref, kv_ref, do_ref, dq_ref, dkv_ref):
        @pl.when(pl.program_id(1) == 0)
        def _():
            dkv_ref[...] = jnp.zeros_like(dkv_ref)

        q = q_ref[...]
        kv = kv_ref[...]
        do = do_ref[...]
        for hh in range(4):
            sl = slice(hh * HEAD_DIM, (hh + 1) * HEAD_DIM)
            vsl = slice(MEM_W + hh * HEAD_DIM, MEM_W + (hh + 1) * HEAD_DIM)
            s = _dot_nt(q[:, sl], kv[:, sl]) * SCALE
            m = jnp.max(s, axis=-1, keepdims=True)
            e = jnp.exp(s - m)
            p = e / jnp.sum(e, axis=-1, keepdims=True)
            dp = _dot_nt(do[:, sl], kv[:, vsl])
            dd = jnp.sum(p * dp, axis=-1, keepdims=True)
            ds = (p * (dp - dd) * SCALE).astype(BF16)
            dq_ref[:, sl] = _dot_nn(ds, kv[:, sl]).astype(BF16)
            dkv_ref[:, sl] += _dot_tn(ds, q[:, sl])
            dkv_ref[:, vsl] += _dot_tn(p.astype(BF16), do[:, sl])

    return pl.pallas_call(
        body, name="mem_attn_bwd", grid=(bsz, nq),
        in_specs=[pl.BlockSpec((tq, MEM_W), lambda b, i: (b * nq + i, q_col)),
                  pl.BlockSpec((N_MEM, 2 * MEM_W), lambda b, i: (b, 0)),
                  pl.BlockSpec((tq, MEM_W), lambda b, i: (b * nq + i, do_col))],
        out_specs=[pl.BlockSpec((tq, MEM_W), lambda b, i: (b * nq + i, 0)),
                   pl.BlockSpec((N_MEM, 2 * MEM_W), lambda b, i: (b, 0))],
        out_shape=[_sds((bsz * seq, MEM_W), BF16), _sds((bsz * N_MEM, 2 * MEM_W), F32)],
        compiler_params=_params(("parallel", "arbitrary")))(h, mkv, dcat)


_GELU_C = math.sqrt(2.0 / math.pi)
_GELU_A = 0.044715


def _gelu(x):
    return 0.5 * x * (1.0 + jnp.tanh(_GELU_C * (x + _GELU_A * x * x * x)))


def _gelu_grad(x):
    th = jnp.tanh(_GELU_C * (x + _GELU_A * x * x * x))
    return 0.5 * (1.0 + th) + 0.5 * x * (1.0 - th * th) * (_GELU_C * (1.0 + 3.0 * _GELU_A * x * x))


def _tril_mask(lower):
    ri = lax.broadcasted_iota(jnp.int32, (BLK, BLK), 0)
    ci = lax.broadcasted_iota(jnp.int32, (BLK, BLK), 1)
    return (ri >= ci) if lower else (ci >= ri)


def _sgu_fwd(h, ws, bs_t, ln_g, ln_b, tm=512):
    t = h.shape[0]

    def body(u_ref, v_ref, ws_ref, bs_ref, g_ref, b_ref, o_ref):
        ug = _gelu(u_ref[...].astype(F32))
        vhat, _ = _ln_hat(_gelu(v_ref[...].astype(F32)))
        vn = (vhat * g_ref[...] + b_ref[...]).astype(BF16)
        mask = _tril_mask(True)
        for g in range(N_HEADS):
            sl = slice(g * HEAD_DIM, (g + 1) * HEAD_DIM)
            w = jnp.where(mask, ws_ref[g], 0).astype(BF16)
            for c in range(tm // BLK):
                rs = slice(c * BLK, (c + 1) * BLK)
                mixed = _dot_nn(w, vn[rs, sl]) + bs_ref[:, g:g + 1]
                o_ref[rs, sl] = (ug[rs, sl] * mixed).astype(BF16)

    return pl.pallas_call(
        body, name="sgu_fwd", grid=(t // tm,),
        in_specs=[_rows(tm, MIX_W, 0), _rows(tm, MIX_W, 1), _whole(ws.shape), _whole(bs_t.shape), _whole(ln_g.shape), _whole(ln_b.shape)],
        out_specs=_rows(tm, MIX_W), out_shape=_sds((t, MIX_W), BF16),
        compiler_params=_params(("parallel",)))(h, h, ws, bs_t, ln_g, ln_b)


def _sgu_bwd(h, dcat, ws, ws_t, bs_t, ln_g, ln_b, tm=512):
    t = h.shape[0]

    def body(u_ref, v_ref, do_ref, ws_ref, wst_ref, bs_ref, g_ref, b_ref, dh_ref, dws_ref, dbs_ref, dg_ref, db_ref, dvn_ref):
        @pl.when(pl.program_id(0) == 0)
        def _():
            dws_ref[...] = jnp.zeros_like(dws_ref)
            dbs_ref[...] = jnp.zeros_like(dbs_ref)
            dg_ref[...] = jnp.zeros_like(dg_ref)
            db_ref[...] = jnp.zeros_like(db_ref)

        u = u_ref[...].astype(F32)
        v = v_ref[...].astype(F32)
        do = do_ref[...].astype(F32)
        ug = _gelu(u)
        vhat, rstd = _ln_hat(_gelu(v))
        vn = (vhat * g_ref[...] + b_ref[...]).astype(BF16)
        dmixed_f = do * ug
        dmixed = dmixed_f.astype(BF16)
        low, upp = _tril_mask(True), _tril_mask(False)
        for g in range(N_HEADS):
            sl = slice(g * HEAD_DIM, (g + 1) * HEAD_DIM)
            w = jnp.where(low, ws_ref[g], 0).astype(BF16)
            wt = jnp.where(upp, wst_ref[g], 0).astype(BF16)
            dws_acc = None
            dbs_acc = None
            for c in range(tm // BLK):
                rs = slice(c * BLK, (c + 1) * BLK)
                mixed = _dot_nn(w, vn[rs, sl]) + bs_ref[:, g:g + 1]
                dh_ref[rs, sl] = (do[rs, sl] * mixed * _gelu_grad(u[rs, sl])).astype(BF16)
                dm = dmixed[rs, sl]
                term = _dot_nt(dm, vn[rs, sl])
                dws_acc = term if dws_acc is None else dws_acc + term
                rsum = jnp.sum(dmixed_f[rs, sl], axis=-1, keepdims=True)
                dbs_acc = rsum if dbs_acc is None else dbs_acc + rsum
                dvn_ref[rs, sl] = _dot_nn(wt, dm)
            dws_ref[g] += jnp.where(low, dws_acc, 0.0)
            dbs_ref[:, g:g + 1] += dbs_acc
        dvn = dvn_ref[...]
        dg_ref[...] += jnp.sum(dvn * vhat, axis=0, keepdims=True)
        db_ref[...] += jnp.sum(dvn, axis=0, keepdims=True)
        dxh = dvn * g_ref[...]
        m1 = jnp.mean(dxh, axis=-1, keepdims=True)
        m2 = jnp.mean(dxh * vhat, axis=-1, keepdims=True)
        dvg = rstd * (dxh - m1 - vhat * m2)
        dh_ref[:, MIX_W:] = (dvg * _gelu_grad(v)).astype(BF16)

    return pl.pallas_call(
        body, name="sgu_bwd", grid=(t // tm,),
        in_specs=[_rows(tm, MIX_W, 0), _rows(tm, MIX_W, 1), _rows(tm, MIX_W, 0), _whole(ws.shape), _whole(ws_t.shape),
                  _whole(bs_t.shape), _whole(ln_g.shape), _whole(ln_b.shape)],
        out_specs=[_rows(tm, 2 * MIX_W), _whole(ws.shape), _whole(bs_t.shape), _whole((1, MIX_W)), _whole((1, MIX_W))],
        out_shape=[_sds((t, 2 * MIX_W), BF16), _sds(ws.shape, F32), _sds(bs_t.shape, F32), _sds((1, MIX_W), F32), _sds((1, MIX_W), F32)],
        scratch_shapes=[pltpu.VMEM((tm, MIX_W), F32)],
        compiler_params=_params(("arbitrary",)))(h, h, dcat, ws, ws_t, bs_t, ln_g, ln_b)


def _row_tile(rows, cols, itemsize=4, limit=2 ** 20):
    best = rows
    for cand in (4096, 2048, 1024, 512, 256, 128, 64, 32, 16):
        if rows % cand == 0 and rows > cand:
            best = cand
            if cand * cols * itemsize <= limit:
                break
    return best


def _adamw(w, m, v, grad=None, parts=None):
    rows, cols = w.shape
    tr = _row_tile(rows, cols)

    def body(w_ref, m_ref, v_ref, g_ref, go_ref, d_ref, nm_ref, nv_ref):
        if parts is None:
            gv = g_ref[...]
        else:
            gv = g_ref[0].astype(F32)
            for k in range(1, 4):
                gv = gv + g_ref[k].astype(F32)
        nm = ADAM_B1 * m_ref[...] + (1.0 - ADAM_B1) * gv
        nv = ADAM_B2 * v_ref[...] + (1.0 - ADAM_B2) * (gv * gv)
        m_hat = nm / (1.0 - ADAM_B1 ** ADAM_STEP)
        v_hat = nv / (1.0 - ADAM_B2 ** ADAM_STEP)
        go_ref[...] = gv
        d_ref[...] = -ADAM_LR * (m_hat / (jnp.sqrt(v_hat) + ADAM_EPS) + ADAM_WD * w_ref[...])
        nm_ref[...] = nm
        nv_ref[...] = nv

    spec = _rows(tr, cols)
    g_spec = spec if parts is None else pl.BlockSpec((4, tr, cols), lambda i: (0, i, 0))
    return pl.pallas_call(
        body, name="adamw" if parts is None else "adamw_sum_chips", grid=(rows // tr,), in_specs=[spec] * 3 + [g_spec],
        out_specs=[spec] * 4, out_shape=[_sds(w.shape, F32)] * 4,
        compiler_params=_params(("parallel",)))(w, m, v, grad if parts is None else parts)


_ANY = pl.BlockSpec(memory_space=pl.ANY)
_MESH = pl.DeviceIdType.MESH


def _all_gather(name, blocks):
    nt = len(blocks)

    def body(*refs):
        x_refs, out_refs = refs[:nt], refs[nt:2 * nt]
        send_sems, recv_sems, local_sems = refs[2 * nt:]
        x, y, c = lax.axis_index("x"), lax.axis_index("y"), lax.axis_index("c")
        me, sibling = (x, y, c), (x, y, 1 - c)
        chips = [(1 - x, y), (x, 1 - y), (1 - x, 1 - y)]

        def slot(t, px, py, pc):
            return out_refs[t].at[4 * px + 2 * py + pc]

        def copy(t, k, blk, to, src=None):
            return pltpu.make_async_remote_copy(
                src_ref=slot(t, *blk) if src is None else src, dst_ref=slot(t, *blk),
                send_sem=send_sems.at[t, k], recv_sem=recv_sems.at[t, k], device_id=to, device_id_type=_MESH)

        mine = [pltpu.make_async_copy(x_refs[t], slot(t, *me), local_sems.at[t]) for t in range(nt)]
        for cp in mine:
            cp.start()
        first = []
        for t in range(nt):
            first.append(copy(t, 0, me, sibling, src=x_refs[t]))
            first += [copy(t, 1 + j, me, (*chip, c), src=x_refs[t]) for j, chip in enumerate(chips)]
        for cp in first:
            cp.start()
        passed = []
        for j, chip in enumerate(chips):
            for t in range(nt):
                copy(t, 1 + j, (*chip, c), me).wait_recv()
                fwd = copy(t, 4 + j, (*chip, c), sibling)
                fwd.start()
                passed.append(fwd)
        for t in range(nt):
            copy(t, 0, sibling, me).wait_recv()
        for j, chip in enumerate(chips):
            for t in range(nt):
                copy(t, 4 + j, (*chip, 1 - c), me).wait_recv()
        for cp in first + passed:
            cp.wait_send()
        for cp in mine:
            cp.wait()

    return pl.pallas_call(
        body, name=name, out_shape=[_sds((N_DEV,) + b.shape, b.dtype) for b in blocks], in_specs=[_ANY] * nt,
        out_specs=[_ANY] * nt,
        scratch_shapes=[pltpu.SemaphoreType.DMA((nt, 7)), pltpu.SemaphoreType.DMA((nt, 7)), pltpu.SemaphoreType.DMA((nt,))])(*blocks)


def _swap_with_sibling(packed):
    nt = len(packed)

    def body(*refs):
        p_refs, got_refs = refs[:nt], refs[nt:2 * nt]
        send_sems, recv_sems = refs[2 * nt:]
        x, y, c = lax.axis_index("x"), lax.axis_index("y"), lax.axis_index("c")
        copies = [
            pltpu.make_async_remote_copy(
                src_ref=p_refs[t].at[1 - c], dst_ref=got_refs[t], send_sem=send_sems.at[t], recv_sem=recv_sems.at[t],
                device_id=(x, y, 1 - c), device_id_type=_MESH)
            for t in range(nt)]
        for cp in copies:
            cp.start()
        for cp in copies:
            cp.wait_recv()
        for cp in copies:
            cp.wait_send()

    return pl.pallas_call(
        body, name="grad_swap_sibling", out_shape=[_sds(p.shape[1:], p.dtype) for p in packed], in_specs=[_ANY] * nt,
        out_specs=[_ANY] * nt,
        scratch_shapes=[pltpu.SemaphoreType.DMA((nt,)), pltpu.SemaphoreType.DMA((nt,))])(*packed)


def _chip_sum(packed, got):
    _, nchip, rows, cols = packed.shape
    tr = _row_tile(rows, cols, 2)
    core = lax.axis_index("c").astype(jnp.int32).reshape(1)

    def body(c_ref, p_ref, g_ref, o_ref):
        o_ref[...] = (p_ref[...].astype(F32) + g_ref[...].astype(F32)).astype(o_ref.dtype)

    grid_spec = pltpu.PrefetchScalarGridSpec(
        num_scalar_prefetch=1, grid=(nchip, rows // tr),
        in_specs=[pl.BlockSpec((None, None, tr, cols), lambda k, i, c: (c[0], k, i, 0)),
                  pl.BlockSpec((None, tr, cols), lambda k, i, c: (k, i, 0))],
        out_specs=pl.BlockSpec((None, tr, cols), lambda k, i, c: (k, i, 0)))
    return pl.pallas_call(
        body, name="grad_chip_sum", grid_spec=grid_spec, out_shape=_sds(got.shape, got.dtype),
        compiler_params=_params(("parallel", "parallel")))(core, packed, got)


def _exchange_chips(chip_sums):
    nt = len(chip_sums)

    def body(*refs):
        s_refs, got_refs = refs[:nt], refs[nt:2 * nt]
        send_sems, recv_sems, local_sems = refs[2 * nt:]
        x, y, c = lax.axis_index("x"), lax.axis_index("y"), lax.axis_index("c")
        my_chip = 2 * x + y
        chips = [(1 - x, y), (x, 1 - y), (1 - x, 1 - y)]
        mine = [pltpu.make_async_copy(s_refs[t].at[my_chip], got_refs[t].at[my_chip], local_sems.at[t]) for t in range(nt)]
        for cp in mine:
            cp.start()

        def copy(t, j, src_chip, dst_chip):
            px, py = chips[j]
            return pltpu.make_async_remote_copy(
                src_ref=s_refs[t].at[src_chip], dst_ref=got_refs[t].at[dst_chip], send_sem=send_sems.at[t, j],
                recv_sem=recv_sems.at[t, j], device_id=(px, py, c), device_id_type=_MESH)

        sends = [copy(t, j, 2 * px + py, my_chip) for t in range(nt) for j, (px, py) in enumerate(chips)]
        for cp in sends:
            cp.start()
        for j, (px, py) in enumerate(chips):
            for t in range(nt):
                copy(t, j, my_chip, 2 * px + py).wait_recv()
        for cp in sends:
            cp.wait_send()
        for cp in mine:
            cp.wait()

    return pl.pallas_call(
        body, name="grad_exchange_chips", out_shape=[_sds(s.shape, s.dtype) for s in chip_sums], in_specs=[_ANY] * nt,
        out_specs=[_ANY] * nt,
        scratch_shapes=[pltpu.SemaphoreType.DMA((nt, 3)), pltpu.SemaphoreType.DMA((nt, 3)), pltpu.SemaphoreType.DMA((nt,))])(*chip_sums)


def _sum_chips(got):
    _, rows, cols = got.shape
    tr = _row_tile(rows, cols)

    def body(g_ref, o_ref):
        acc = g_ref[0].astype(F32)
        for k in range(1, 4):
            acc = acc + g_ref[k].astype(F32)
        o_ref[...] = acc

    return pl.pallas_call(
        body, name="grad_sum_chips", grid=(rows // tr,), in_specs=[pl.BlockSpec((4, tr, cols), lambda i: (0, i, 0))],
        out_specs=pl.BlockSpec((tr, cols), lambda i: (i, 0)), out_shape=_sds((rows, cols), F32),
        compiler_params=_params(("parallel",)))(got)


_COL_SHARDED = ("a_w_in", "b_w_in", "w_gate", "w_up")
_ROW_SHARDED = ("w_mem_kv", "w_out", "w_down")
_BIG = ("a_w_in", "b_w_in", "w_mem_kv", "w_out", "w_gate", "w_up", "w_down")
_SGU_LN = ("sgu_ln_g", "sgu_ln_b")
_LN4 = ("ln_mix_g", "ln_mix_b", "ln_ffn_g", "ln_ffn_b")
_REPLICATED = ("sgu_w_s", "sgu_b_s") + _LN4


def _unshard(name, gathered):
    if name in _COL_SHARDED or name in _SGU_LN:
        moved = jnp.moveaxis(gathered, 0, -2)
        return moved.reshape(moved.shape[:-2] + (moved.shape[-2] * moved.shape[-1],))
    moved = jnp.moveaxis(gathered, 0, 1)
    return moved.reshape((moved.shape[0], moved.shape[1] * moved.shape[2]) + moved.shape[3:])


def _by_shard(name, full):
    if name in _COL_SHARDED or name in _SGU_LN:
        split = full.reshape(full.shape[:-1] + (N_DEV, full.shape[-1] // N_DEV))
        return jnp.moveaxis(split, -2, 0)
    split = full.reshape((full.shape[0], N_DEV, full.shape[1] // N_DEV) + full.shape[2:])
    return jnp.moveaxis(split, 1, 0)


def _gather_weights(shards):
    gathered = _all_gather("weights_all_gather", [shards[n].astype(BF16) for n in _BIG])
    return {n: _unshard(n, g) for n, g in zip(_BIG, gathered)}


def _gather_small_f32(named):
    names = list(named)
    gathered = _all_gather("small_all_gather", [named[n] for n in names])
    return {n: _unshard(n, g) for n, g in zip(names, gathered)}


def _two_level(by_dest):
    shp = by_dest.shape[1:]
    split = by_dest.astype(BF16).reshape((4, 2) + shp).swapaxes(0, 1)
    return split.reshape(2, 4, int(np.prod(shp[:-1])), shp[-1])


def _reduce_gradients(grads):
    sharded = _BIG + _SGU_LN
    packed = [_two_level(_by_shard(n, grads[n])) for n in sharded]
    ln4 = jnp.stack([grads[n] for n in _LN4])
    rep = [grads["sgu_w_s"].reshape(N_DEV, -1, BLK), grads["sgu_b_s"].reshape(N_DEV, -1, BLK), ln4.reshape(N_DEV, -1, D_MODEL)]
    packed += [_two_level(r) for r in rep]
    got = _swap_with_sibling(packed)
    sums = [_chip_sum(p, g) for p, g in zip(packed, got)]
    parts = _exchange_chips(sums)
    out_parts = dict(zip(sharded, parts[:len(sharded)]))
    mine = [_sum_chips(p) for p in parts[len(sharded):]]
    w_s, b_s, ln_all = _all_gather("replicated_grads_all_gather", mine)
    ln_all = ln_all.reshape(ln4.shape)
    rep_grads = {"sgu_w_s": w_s.reshape(grads["sgu_w_s"].shape), "sgu_b_s": b_s.reshape(grads["sgu_b_s"].shape)}
    rep_grads.update({n: ln_all[i] for i, n in enumerate(_LN4)})
    return out_parts, rep_grads


def _as_2d(a):
    if a.ndim == 1:
        return a.reshape(1, -1)
    return a.reshape(-1, a.shape[-1])


def kernel(x, mem, a_w_in, b_w_in, sgu_ln_g, sgu_ln_b, sgu_w_s, sgu_b_s, w_mem_kv, w_out, ln_mix_g, ln_mix_b, w_gate, w_up, w_down, ln_ffn_g, ln_ffn_b, loss_target, m_a_w_in, m_b_w_in, m_sgu_ln_g, m_sgu_ln_b, m_sgu_w_s, m_sgu_b_s, m_w_mem_kv, m_w_out, m_ln_mix_g, m_ln_mix_b, m_w_gate, m_w_up, m_w_down, m_ln_ffn_g, m_ln_ffn_b, v_a_w_in, v_b_w_in, v_sgu_ln_g, v_sgu_ln_b, v_sgu_w_s, v_sgu_b_s, v_w_mem_kv, v_w_out, v_ln_mix_g, v_ln_mix_b, v_w_gate, v_w_up, v_w_down, v_ln_ffn_g, v_ln_ffn_b):
    names = ("a_w_in", "b_w_in", "sgu_ln_g", "sgu_ln_b", "sgu_w_s", "sgu_b_s", "w_mem_kv", "w_out", "ln_mix_g", "ln_mix_b",
             "w_gate", "w_up", "w_down", "ln_ffn_g", "ln_ffn_b")
    weights = dict(zip(names, (a_w_in, b_w_in, sgu_ln_g, sgu_ln_b, sgu_w_s, sgu_b_s, w_mem_kv, w_out, ln_mix_g, ln_mix_b,
                               w_gate, w_up, w_down, ln_ffn_g, ln_ffn_b)))
    mom_m = dict(zip(names, (m_a_w_in, m_b_w_in, m_sgu_ln_g, m_sgu_ln_b, m_sgu_w_s, m_sgu_b_s, m_w_mem_kv, m_w_out, m_ln_mix_g,
                             m_ln_mix_b, m_w_gate, m_w_up, m_w_down, m_ln_ffn_g, m_ln_ffn_b)))
    mom_v = dict(zip(names, (v_a_w_in, v_b_w_in, v_sgu_ln_g, v_sgu_ln_b, v_sgu_w_s, v_sgu_b_s, v_w_mem_kv, v_w_out, v_ln_mix_g,
                             v_ln_mix_b, v_w_gate, v_w_up, v_w_down, v_ln_ffn_g, v_ln_ffn_b)))
    full = _gather_weights(weights)
    sgu_ln = _gather_small_f32({n: weights[n] for n in _SGU_LN})
    loss_part, grad_x, local = _local_step(x, mem, loss_target, full, sgu_ln, {n: weights[n] for n in _REPLICATED})
    loss = lax.psum(loss_part[0, 0], ("x", "y", "c"))
    parts, rep_grads = _reduce_gradients(local)

    reduced, deltas, new_m, new_v = {}, {}, {}, {}
    for n in names:
        w2, m2, v2 = _as_2d(weights[n]), _as_2d(mom_m[n]), _as_2d(mom_v[n])
        if n in parts:
            outs = _adamw(w2, m2, v2, parts=parts[n])
        else:
            outs = _adamw(w2, m2, v2, grad=_as_2d(rep_grads[n]))
        reduced[n], deltas[n], new_m[n], new_v[n] = (a.reshape(weights[n].shape) for a in outs)

    return (loss, grad_x, *[reduced[n] for n in names], *[deltas[n] for n in names],
            *[new_m[n] for n in names], *[new_v[n] for n in names])


def _local_step(x, mem, loss_target, full, sgu_ln, small):
    sgu_w_s, sgu_b_s = small["sgu_w_s"], small["sgu_b_s"]
    ln_mix_g, ln_mix_b, ln_ffn_g, ln_ffn_b = (small[n] for n in ("ln_mix_g", "ln_mix_b", "ln_ffn_g", "ln_ffn_b"))
    bsz, seq, _ = x.shape
    tokens = bsz * seq
    w_gu = jnp.concatenate([full["w_gate"], full["w_up"]], axis=-1)
    slopes = _alibi_table()

    xf = x.reshape(tokens, D_MODEL)
    xb = xf.astype(BF16)
    memb = mem.reshape(bsz * N_MEM, D_MODEL).astype(BF16)
    tgt = loss_target.reshape(tokens, D_MODEL)

    saved = []
    for i in range(DEPTH):
        j = i // 2
        dil_layer = i % 2 == 0
        w_in = full["a_w_in"][j] if dil_layer else full["b_w_in"][j]
        mkv = _linear_nn("mem_kv", memb, full["w_mem_kv"][i])
        h = _linear_nn("in_proj_a" if dil_layer else "in_proj_b", xb, w_in)
        st = dict(x=xf, xb=xb, h=h, mkv=mkv)
        if dil_layer:
            mix, st["lse"] = _band_attn_fwd_fused(h, slopes, bsz, seq)
            q_col = 3 * MIX_W // MEM_W
        else:
            st["ws"] = sgu_w_s[j]
            st["bs_t"] = sgu_b_s[j].T
            st["ln_g"] = sgu_ln["sgu_ln_g"][j].reshape(1, MIX_W)
            st["ln_b"] = sgu_ln["sgu_ln_b"][j].reshape(1, MIX_W)
            mix = _sgu_fwd(h, st["ws"], st["bs_t"], st["ln_g"], st["ln_b"])
            q_col = 2 * MIX_W // MEM_W
        mo = _mem_attn_fwd(h, mkv, bsz, seq, q_col)
        r1, x1, x1b = _proj_ln_fwd("out_proj_ln", [mix, mo], full["w_out"][i], xf,
                                   ln_mix_g[i].reshape(1, D_MODEL), ln_mix_b[i].reshape(1, D_MODEL))
        gt, up, act = _ffn_up_fwd(x1b, w_gu[i])
        r2, x2, x2b = _proj_ln_fwd("ffn_down_ln", [act], full["w_down"][i], x1,
                                   ln_ffn_g[i].reshape(1, D_MODEL), ln_ffn_b[i].reshape(1, D_MODEL))
        st.update(mix=mix, mo=mo, q_col=q_col, r1=r1, x1b=x1b, gt=gt, up=up, act=act, r2=r2)
        saved.append(st)
        xf, xb = x2, x2b

    dx, loss_part = _loss_fwd_bwd(xf, tgt)

    per_pair = ("a_w_in", "b_w_in", "sgu_ln_g", "sgu_ln_b", "sgu_w_s", "sgu_b_s")
    grads = {n: [None] * (DEPTH // 2 if n in per_pair else DEPTH) for n in _BIG + _SGU_LN + _REPLICATED}
    for i in reversed(range(DEPTH)):
        j = i // 2
        st = saved[i]
        dil_layer = i % 2 == 0
        w_in = full["a_w_in"][j] if dil_layer else full["b_w_in"][j]
        dr2, dr2b, dg, db = _ln_bwd(dx, st["r2"], ln_ffn_g[i].reshape(1, D_MODEL))
        grads["ln_ffn_g"][i], grads["ln_ffn_b"][i] = dg[0], db[0]
        dgu = _ffn_down_bwd(dr2b, full["w_down"][i], st["gt"], st["up"])
        grads["w_down"][i] = _mm_tn("grad_w_down", st["act"], dr2b)
        dx1 = _linear_nt("ffn_up_bwd", [dgu], w_gu[i], dr2, F32)
        dw_gu = _mm_tn("grad_w_gate_up", st["x1b"], dgu)
        grads["w_gate"][i], grads["w_up"][i] = dw_gu[:, :D_FF], dw_gu[:, D_FF:]
        dr1, dr1b, dg, db = _ln_bwd(dx1, st["r1"], ln_mix_g[i].reshape(1, D_MODEL))
        grads["ln_mix_g"][i], grads["ln_mix_b"][i] = dg[0], db[0]
        dcat = _linear_nt("out_proj_bwd", [dr1b], full["w_out"][i], None, BF16)
        grads["w_out"][i] = jnp.concatenate(
            [_mm_tn("grad_w_out_mix", st["mix"], dr1b), _mm_tn("grad_w_out_mem", st["mo"], dr1b)], axis=0)
        dqm, dmkv = _mem_attn_bwd(st["h"], st["mkv"], dcat, bsz, seq, st["q_col"])
        grads["w_mem_kv"][i] = _mm_tn("grad_w_mem_kv", memb, dmkv.astype(BF16))
        if dil_layer:
            dh_parts = _band_attn_bwd_fused(st["h"], dcat, st["mix"], st["lse"], slopes, bsz, seq)
        else:
            ws_t = jnp.swapaxes(st["ws"], -1, -2)
            dh_main, dws, dbs_t, dlg, dlb = _sgu_bwd(st["h"], dcat, st["ws"], ws_t, st["bs_t"], st["ln_g"], st["ln_b"])
            grads["sgu_w_s"][j], grads["sgu_b_s"][j] = dws, dbs_t.T
            grads["sgu_ln_g"][j], grads["sgu_ln_b"][j] = dlg[0], dlb[0]
            dh_parts = [dh_main]
        dx = _linear_nt("in_proj_bwd_a" if dil_layer else "in_proj_bwd_b", [*dh_parts, dqm], w_in, dr1, F32)
        grads["a_w_in" if dil_layer else "b_w_in"][j] = jnp.concatenate(
            [_mm_tn("grad_w_in_part", st["xb"], part) for part in dh_parts] + [_mm_tn("grad_w_in_qm", st["xb"], dqm)], axis=1)
    return loss_part, dx.reshape(x.shape), {n: jnp.stack(g) for n, g in grads.items()}
```

```python
import functools
import math

import numpy as np
import jax
import jax.numpy as jnp
from jax import lax
from jax.experimental import pallas as pl
from jax.experimental.pallas import tpu as pltpu

F32 = jnp.float32
BF16 = jnp.bfloat16

D_MODEL = 1024
DEPTH = 4
N_MEM = 256
HEAD_DIM = 64
N_HEADS = 12
MIX_W = N_HEADS * HEAD_DIM
MEM_W = 4 * HEAD_DIM
DIL_PATTERNS = ((128, 1), (512, 4), (2048, 16))
BLK = 128
HEAD_GROUP = 4
N_GROUPS = N_HEADS // HEAD_GROUP
D_FF = 2816
ALPHA = (2 * DEPTH) ** 0.25
LN_EPS = 1e-5
SCALE = HEAD_DIM ** -0.5
NEG = -1e30
N_DEV = 8

ADAM_LR, ADAM_B1, ADAM_B2, ADAM_EPS, ADAM_WD, ADAM_STEP = 0.001, 0.9, 0.999, 1e-08, 0.01, 10

VMEM_LIMIT = 56 * 2 ** 20
STAT_LANES = 32
STAT_W = N_HEADS * STAT_LANES


def _dot_nn(a, b):
    return lax.dot_general(a, b, (((1,), (0,)), ((), ())), preferred_element_type=F32)


def _dot_nt(a, b):
    return lax.dot_general(a, b, (((1,), (1,)), ((), ())), preferred_element_type=F32)


def _dot_tn(a, b):
    return lax.dot_general(a, b, (((0,), (0,)), ((), ())), preferred_element_type=F32)


def _ln_hat(r):
    mu = jnp.mean(r, axis=-1, keepdims=True)
    xc = r - mu
    var = jnp.mean(xc * xc, axis=-1, keepdims=True)
    rstd = lax.rsqrt(var + LN_EPS)
    return xc * rstd, rstd


def _params(sem):
    return pltpu.CompilerParams(dimension_semantics=sem, vmem_limit_bytes=VMEM_LIMIT)


def _rows(tm, c, col=0):
    return pl.BlockSpec((tm, c), lambda i: (i, col))


def _whole(shape):
    nd = len(shape)
    return pl.BlockSpec(tuple(shape), lambda *_: (0,) * nd)


def _resident(shape):
    nd = len(shape)
    return pl.BlockSpec(tuple(shape), lambda *_: (0,) * nd, pipeline_mode=pl.Buffered(1))


def _sds(shape, dtype):
    return jax.ShapeDtypeStruct(tuple(shape), dtype)


def _linear_nn(name, a, w, tm=512):
    t, k = a.shape
    n = w.shape[1]
    tm = min(tm, t)

    def body(a_ref, w_ref, o_ref):
        o_ref[...] = _dot_nn(a_ref[...], w_ref[...]).astype(BF16)

    return pl.pallas_call(
        body, name=name, grid=(t // tm,), in_specs=[_rows(tm, k), _resident(w.shape)], out_specs=_rows(tm, n),
        out_shape=_sds((t, n), BF16), compiler_params=_params(("parallel",)))(a, w)


def _proj_ln_fwd(name, lhs, w, x_res, g, b, tm=256):
    t = x_res.shape[0]
    n_lhs = len(lhs)

    def body(*refs):
        lhs_refs = refs[:n_lhs]
        w_ref, x_ref, g_ref, b_ref, r_ref, xn_ref, xnb_ref = refs[n_lhs:]
        y, off = None, 0
        for lr in lhs_refs:
            k = lr.shape[1]
            term = _dot_nn(lr[...], w_ref[off:off + k, :])
            y = term if y is None else y + term
            off += k
        r = ALPHA * x_ref[...] + y
        xhat, _ = _ln_hat(r)
        xn = xhat * g_ref[...] + b_ref[...]
        r_ref[...] = r
        xn_ref[...] = xn
        xnb_ref[...] = xn.astype(BF16)

    in_specs = [_rows(tm, a.shape[1]) for a in lhs] + [_resident(w.shape), _rows(tm, D_MODEL), _whole(g.shape), _whole(b.shape)]
    return pl.pallas_call(
        body, name=name, grid=(t // tm,), in_specs=in_specs,
        out_specs=[_rows(tm, D_MODEL)] * 3,
        out_shape=[_sds((t, D_MODEL), F32), _sds((t, D_MODEL), F32), _sds((t, D_MODEL), BF16)],
        compiler_params=_params(("parallel",)))(*lhs, w, x_res, g, b)


def _ffn_up_fwd(xb, wgu, tm=256):
    t = xb.shape[0]

    def body(x_ref, w_ref, g_ref, u_ref, a_ref):
        gu = _dot_nn(x_ref[...], w_ref[...])
        gt, up = gu[:, :D_FF], gu[:, D_FF:]
        g_ref[...] = gt.astype(BF16)
        u_ref[...] = up.astype(BF16)
        a_ref[...] = (gt * jax.nn.sigmoid(gt) * up).astype(BF16)

    return pl.pallas_call(
        body, name="ffn_up_fwd", grid=(t // tm,), in_specs=[_rows(tm, D_MODEL), _resident(wgu.shape)],
        out_specs=[_rows(tm, D_FF)] * 3, out_shape=[_sds((t, D_FF), BF16)] * 3,
        compiler_params=_params(("parallel",)))(xb, wgu)


def _loss_fwd_bwd(xn, tgt, tm=512):
    t = xn.shape[0]

    def body(x_ref, t_ref, dx_ref, l_ref):
        @pl.when(pl.program_id(0) == 0)
        def _():
            l_ref[...] = jnp.zeros_like(l_ref)

        e = x_ref[...] - t_ref[...]
        dx_ref[...] = e * (1.0 / D_MODEL)
        l_ref[...] += jnp.sum(e * e) * (0.5 / D_MODEL)

    return pl.pallas_call(
        body, name="loss", grid=(t // tm,), in_specs=[_rows(tm, D_MODEL)] * 2,
        out_specs=[_rows(tm, D_MODEL), _whole((1, 128))], out_shape=[_sds((t, D_MODEL), F32), _sds((1, 128), F32)],
        compiler_params=_params(("arbitrary",)))(xn, tgt)


def _ln_bwd(dx, r, g, tm=512):
    t = dx.shape[0]

    def body(dx_ref, r_ref, g_ref, dr_ref, drb_ref, dg_ref, db_ref):
        @pl.when(pl.program_id(0) == 0)
        def _():
            dg_ref[...] = jnp.zeros_like(dg_ref)
            db_ref[...] = jnp.zeros_like(db_ref)

        dxv = dx_ref[...]
        xhat, rstd = _ln_hat(r_ref[...])
        dxh = dxv * g_ref[...]
        m1 = jnp.mean(dxh, axis=-1, keepdims=True)
        m2 = jnp.mean(dxh * xhat, axis=-1, keepdims=True)
        dr = rstd * (dxh - m1 - xhat * m2)
        dr_ref[...] = dr
        drb_ref[...] = dr.astype(BF16)
        dg_ref[...] += jnp.sum(dxv * xhat, axis=0, keepdims=True)
        db_ref[...] += jnp.sum(dxv, axis=0, keepdims=True)

    return pl.pallas_call(
        body, name="ln_bwd", grid=(t // tm,), in_specs=[_rows(tm, D_MODEL), _rows(tm, D_MODEL), _whole(g.shape)],
        out_specs=[_rows(tm, D_MODEL), _rows(tm, D_MODEL), _whole((1, D_MODEL)), _whole((1, D_MODEL))],
        out_shape=[_sds((t, D_MODEL), F32), _sds((t, D_MODEL), BF16), _sds((1, D_MODEL), F32), _sds((1, D_MODEL), F32)],
        compiler_params=_params(("arbitrary",)))(dx, r, g)


def _ffn_down_bwd(drb, wd, gt, up, tm=256):
    t = drb.shape[0]

    def body(d_ref, w_ref, g_ref, u_ref, o_ref):
        da = _dot_nt(d_ref[...], w_ref[...])
        g = g_ref[...].astype(F32)
        u = u_ref[...].astype(F32)
        sg = jax.nn.sigmoid(g)
        o_ref[:, :D_FF] = (da * u * (sg * (1.0 + g * (1.0 - sg)))).astype(BF16)
        o_ref[:, D_FF:] = (da * (g * sg)).astype(BF16)

    return pl.pallas_call(
        body, name="ffn_down_bwd", grid=(t // tm,),
        in_specs=[_rows(tm, D_MODEL), _resident(wd.shape), _rows(tm, D_FF), _rows(tm, D_FF)],
        out_specs=_rows(tm, 2 * D_FF), out_shape=_sds((t, 2 * D_FF), BF16),
        compiler_params=_params(("parallel",)))(drb, wd, gt, up)


def _linear_nt(name, lhs, w, res, out_dtype, tm=256):
    t = lhs[0].shape[0]
    n_lhs = len(lhs)
    n_out = w.shape[0]

    def body(*refs):
        lhs_refs = refs[:n_lhs]
        w_ref = refs[n_lhs]
        o_ref = refs[-1]
        y, off = None, 0
        for lr in lhs_refs:
            k = lr.shape[1]
            term = _dot_nt(lr[...], w_ref[:, off:off + k])
            y = term if y is None else y + term
            off += k
        if res is not None:
            y = ALPHA * refs[n_lhs + 1][...] + y
        o_ref[...] = y.astype(out_dtype)

    in_specs = [_rows(tm, a.shape[1]) for a in lhs] + [_resident(w.shape)]
    args = list(lhs) + [w]
    if res is not None:
        in_specs.append(_rows(tm, n_out))
        args.append(res)
    return pl.pallas_call(
        body, name=name, grid=(t // tm,), in_specs=in_specs, out_specs=_rows(tm, n_out),
        out_shape=_sds((t, n_out), out_dtype), compiler_params=_params(("parallel",)))(*args)


def _pick_tile(n, limit):
    if n <= limit:
        return n
    best = 128
    for cand in range(128, limit + 1, 128):
        if n % cand == 0:
            best = cand
    return best


def _mm_tn(name, a, b, tt=1024):
    t, k = a.shape
    n = b.shape[1]
    tt = min(tt, t)
    tk = _pick_tile(k, 1408)
    tn = _pick_tile(n, (6 * 2 ** 20) // (4 * tk) // 128 * 128)

    def body(a_ref, b_ref, o_ref):
        @pl.when(pl.program_id(2) == 0)
        def _():
            o_ref[...] = jnp.zeros_like(o_ref)

        o_ref[...] += _dot_tn(a_ref[...], b_ref[...])

    return pl.pallas_call(
        body, name=name, grid=(k // tk, n // tn, t // tt),
        in_specs=[pl.BlockSpec((tt, tk), lambda i, j, s: (s, i)), pl.BlockSpec((tt, tn), lambda i, j, s: (s, j))],
        out_specs=pl.BlockSpec((tk, tn), lambda i, j, s: (i, j)), out_shape=_sds((k, n), F32),
        compiler_params=_params(("parallel", "parallel", "arbitrary")))(a, b)


def _alibi_table():
    arr = np.zeros((N_GROUPS, 8, 128), np.float32)
    for g in range(N_GROUPS):
        for hh in range(HEAD_GROUP):
            arr[g, hh, :] = 2.0 ** (-8.0 * (g * HEAD_GROUP + hh + 1) / N_HEADS)
    return jnp.asarray(arr)


def _band_mask(n, dil):
    qi = lax.broadcasted_iota(jnp.int32, (BLK, 2 * BLK), 0)
    ki = lax.broadcasted_iota(jnp.int32, (BLK, 2 * BLK), 1)
    steps = qi + BLK - ki
    valid = (steps >= 0) & (steps <= BLK) & ((ki >= BLK) | (n > 0))
    return valid, (steps * dil).astype(F32)


def _band_specs(bsz, seq, dil, width):
    cb = width // 256

    def spec(off, prev=False):
        if prev:
            return pl.BlockSpec((None, BLK, 256), lambda b, r, g, n: (b, jnp.maximum(n - 1, 0), r * cb + off + g))
        return pl.BlockSpec((None, BLK, 256), lambda b, r, g, n: (b, n, r * cb + off + g))

    return spec


def _spread_stats(cols):
    lane = lax.broadcasted_iota(jnp.int32, (BLK, HEAD_GROUP * STAT_LANES), 1)
    tile = cols[HEAD_GROUP - 1]
    for hh in range(HEAD_GROUP - 2, -1, -1):
        tile = jnp.where(lane < (hh + 1) * STAT_LANES, cols[hh], tile)
    return tile


def _band_attn_fwd(h, slopes, bsz, seq, dil):
    width = h.shape[1]
    length = seq // dil
    nblk = length // BLK
    hv = h.reshape(bsz, length, dil * width)
    spec = _band_specs(bsz, seq, dil, width)
    k_off, v_off = MIX_W // 256, 2 * MIX_W // 256

    def body(sl_ref, q_ref, kc_ref, kp_ref, vc_ref, vp_ref, o_ref, lse_ref):
        valid, dist = _band_mask(pl.program_id(3), dil)
        q = q_ref[...]
        k2 = jnp.concatenate([kp_ref[...], kc_ref[...]], axis=0)
        v2 = jnp.concatenate([vp_ref[...], vc_ref[...]], axis=0)
        lses = []
        for hh in range(HEAD_GROUP):
            sl = slice(hh * HEAD_DIM, (hh + 1) * HEAD_DIM)
            s = _dot_nt(q[:, sl], k2[:, sl]) * SCALE - sl_ref[hh:hh + 1, 0:1] * dist
            s = jnp.where(valid, s, NEG)
            m = jnp.max(s, axis=-1, keepdims=True)
            p = jnp.exp(s - m)
            l = jnp.sum(p, axis=-1, keepdims=True)
            acc = _dot_nn(p.astype(BF16), v2[:, sl])
            o_ref[:, sl] = (acc / l).astype(BF16)
            lses.append(m + jnp.log(l))
        lse_ref[...] = _spread_stats(lses)

    out, lse = pl.pallas_call(
        body, name=f"band_attn_fwd_d{dil}", grid=(bsz, dil, N_GROUPS, nblk),
        in_specs=[pl.BlockSpec((None, 8, 128), lambda b, r, g, n: (g, 0, 0)),
                  spec(0), spec(k_off), spec(k_off, True), spec(v_off), spec(v_off, True)],
        out_specs=[pl.BlockSpec((None, BLK, 256), lambda b, r, g, n: (b, n, r * N_GROUPS + g)),
                   pl.BlockSpec((None, BLK, 128), lambda b, r, g, n: (b, n, r * N_GROUPS + g))],
        out_shape=[_sds((bsz, length, dil * MIX_W), BF16), _sds((bsz, length, dil * STAT_W), F32)],
        compiler_params=_params(("parallel", "parallel", "parallel", "arbitrary")))(slopes, hv, hv, hv, hv, hv)
    return out.reshape(bsz * seq, MIX_W), lse.reshape(bsz * seq, STAT_W)


def _band_merge(outs, lses, tm=512):
    t = outs[0].shape[0]

    def body(o1, o2, o3, l1, l2, l3, mix_ref, lse_ref):
        ls = [l1[...], l2[...], l3[...]]
        m = jnp.maximum(jnp.maximum(ls[0], ls[1]), ls[2])
        tot = m + jnp.log(jnp.exp(ls[0] - m) + jnp.exp(ls[1] - m) + jnp.exp(ls[2] - m))
        ws = [jnp.exp(x - tot) for x in ls]
        lse_ref[...] = tot
        for hd in range(N_HEADS):
            sl = slice(hd * HEAD_DIM, (hd + 1) * HEAD_DIM)
            acc = None
            for w, o in zip(ws, (o1, o2, o3)):
                term = w[:, hd * STAT_LANES:hd * STAT_LANES + 1] * o[:, sl].astype(F32)
                acc = term if acc is None else acc + term
            mix_ref[:, sl] = acc.astype(BF16)

    return pl.pallas_call(
        body, name="band_merge", grid=(t // tm,), in_specs=[_rows(tm, MIX_W)] * 3 + [_rows(tm, STAT_W)] * 3,
        out_specs=[_rows(tm, MIX_W), _rows(tm, STAT_W)], out_shape=[_sds((t, MIX_W), BF16), _sds((t, STAT_W), F32)],
        compiler_params=_params(("parallel",)))(*outs, *lses)


def _band_delta(dcat, mix, tm=512):
    t = mix.shape[0]

    def body(d_ref, o_ref, dd_ref):
        prod = d_ref[...].astype(F32) * o_ref[...].astype(F32)
        for hd in range(N_HEADS):
            rsum = jnp.sum(prod[:, hd * HEAD_DIM:(hd + 1) * HEAD_DIM], axis=-1, keepdims=True)
            dd_ref[:, hd * STAT_LANES:(hd + 1) * STAT_LANES] = jnp.broadcast_to(rsum, (tm, STAT_LANES))

    return pl.pallas_call(
        body, name="band_delta", grid=(t // tm,), in_specs=[_rows(tm, MIX_W), _rows(tm, MIX_W)],
        out_specs=_rows(tm, STAT_W), out_shape=_sds((t, STAT_W), F32),
        compiler_params=_params(("parallel",)))(dcat, mix)


def _band_attn_bwd(h, dcat, slopes, lse, delta, bsz, seq, dil):
    width = h.shape[1]
    length = seq // dil
    nblk = length // BLK
    hv = h.reshape(bsz, length, dil * width)
    dv_ = dcat.reshape(bsz, length, dil * D_MODEL)
    k_off, v_off = MIX_W // 256, 2 * MIX_W // 256
    cb, dcb = width // 256, D_MODEL // 256

    def cur(off, c):
        return pl.BlockSpec((None, BLK, 256), lambda b, r, g, n: (b, jnp.minimum(n, nblk - 1), r * c + off + g))

    def prev(off, c):
        return pl.BlockSpec((None, BLK, 256), lambda b, r, g, n: (b, jnp.maximum(jnp.minimum(n, nblk - 1) - 1, 0), r * c + off + g))

    stat = pl.BlockSpec((None, BLK, 128), lambda b, r, g, n: (b, jnp.minimum(n, nblk - 1), r * N_GROUPS + g))
    dq_spec = pl.BlockSpec((None, BLK, 256), lambda b, r, g, n: (b, jnp.minimum(n, nblk - 1), r * N_GROUPS + g))
    dkv_spec = pl.BlockSpec((None, BLK, 256), lambda b, r, g, n: (b, jnp.maximum(n - 1, 0), r * N_GROUPS + g))

    def body(sl_ref, q_ref, kc_ref, kp_ref, vc_ref, vp_ref, do_ref, lse_ref, dd_ref, dq_ref, dk_ref, dv_ref, kcar, vcar):
        n = pl.program_id(3)

        @pl.when(n == 0)
        def _():
            kcar[...] = jnp.zeros_like(kcar)
            vcar[...] = jnp.zeros_like(vcar)

        @pl.when(n < nblk)
        def _():
            valid, dist = _band_mask(n, dil)
            q = q_ref[...]
            do = do_ref[...]
            k2 = jnp.concatenate([kp_ref[...], kc_ref[...]], axis=0)
            v2 = jnp.concatenate([vp_ref[...], vc_ref[...]], axis=0)
            for hh in range(HEAD_GROUP):
                sl = slice(hh * HEAD_DIM, (hh + 1) * HEAD_DIM)
                s = _dot_nt(q[:, sl], k2[:, sl]) * SCALE - sl_ref[hh:hh + 1, 0:1] * dist
                s = jnp.where(valid, s, NEG)
                st = slice(hh * STAT_LANES, hh * STAT_LANES + 1)
                p = jnp.exp(s - lse_ref[:, st])
                dp = _dot_nt(do[:, sl], v2[:, sl])
                ds = (p * (dp - dd_ref[:, st]) * SCALE).astype(BF16)
                dq_ref[:, sl] = _dot_nn(ds, k2[:, sl]).astype(BF16)
                dk2 = _dot_tn(ds, q[:, sl])
                dv2 = _dot_tn(p.astype(BF16), do[:, sl])
                dk_ref[:, sl] = (kcar[:, sl] + dk2[:BLK]).astype(BF16)
                dv_ref[:, sl] = (vcar[:, sl] + dv2[:BLK]).astype(BF16)
                kcar[:, sl] = dk2[BLK:]
                vcar[:, sl] = dv2[BLK:]

        @pl.when(n == nblk)
        def _():
            dk_ref[...] = kcar[...].astype(BF16)
            dv_ref[...] = vcar[...].astype(BF16)

    outs = pl.pallas_call(
        body, name=f"band_attn_bwd_d{dil}", grid=(bsz, dil, N_GROUPS, nblk + 1),
        in_specs=[pl.BlockSpec((None, 8, 128), lambda b, r, g, n: (g, 0, 0)),
                  cur(0, cb), cur(k_off, cb), prev(k_off, cb), cur(v_off, cb), prev(v_off, cb), cur(0, dcb), stat, stat],
        out_specs=[dq_spec, dkv_spec, dkv_spec],
        out_shape=[_sds((bsz, length, dil * MIX_W), BF16)] * 3,
        scratch_shapes=[pltpu.VMEM((BLK, 256), F32), pltpu.VMEM((BLK, 256), F32)],
        compiler_params=_params(("parallel", "parallel", "parallel", "arbitrary")))(
            slopes, hv, hv, hv, hv, hv, dv_, lse.reshape(bsz, length, dil * STAT_W), delta.reshape(bsz, length, dil * STAT_W))
    return [o.reshape(bsz * seq, MIX_W) for o in outs]


def _sum_patterns(parts, tm=512):
    t = parts[0][0].shape[0]

    def body(*refs):
        o_ref = refs[-1]
        for j in range(3):
            acc = refs[j][...].astype(F32) + refs[3 + j][...].astype(F32) + refs[6 + j][...].astype(F32)
            o_ref[:, j * MIX_W:(j + 1) * MIX_W] = acc.astype(BF16)

    flat = [x for p in parts for x in p]
    return pl.pallas_call(
        body, name="band_sum", grid=(t // tm,), in_specs=[_rows(tm, MIX_W)] * 9, out_specs=_rows(tm, 3 * MIX_W),
        out_shape=_sds((t, 3 * MIX_W), BF16), compiler_params=_params(("parallel",)))(*flat)


def _block_mask(has_prev, dil):
    qi = lax.broadcasted_iota(jnp.int32, (BLK, 2 * BLK), 0)
    ki = lax.broadcasted_iota(jnp.int32, (BLK, 2 * BLK), 1)
    steps = qi + BLK - ki
    valid = (steps >= 0) & (steps <= BLK) & ((ki >= BLK) | has_prev)
    return valid, (steps * dil).astype(F32)


def _rows_of(j):
    return pl.ds(pl.multiple_of(j * BLK, BLK), BLK)


def _lane_half(hf):
    return slice(hf * 128, (hf + 1) * 128)


def _split_pair(x):
    first = lax.broadcasted_iota(jnp.int32, (1, 2 * HEAD_DIM), 1) < HEAD_DIM
    zero = jnp.zeros_like(x)
    return jnp.where(first, x, zero), jnp.where(first, zero, x)


def _deinterleave(src, dst, seq, dil, dtype):
    length = seq // dil
    for r in range(dil):
        for c in range(length // BLK):
            rows = pl.ds(r + c * BLK * dil, BLK, stride=dil)
            out = slice(r * length + c * BLK, r * length + (c + 1) * BLK)
            if len(src.shape) == 2:
                dst[out, :] = src[rows, :].astype(dtype)
            else:
                for hf in range(2):
                    dst[out, _lane_half(hf)] = src.at[hf][rows, :].astype(dtype)


def _interleave(src, dst, seq, dil, accumulate):
    length = seq // dil
    for r in range(dil):
        for c in range(length // BLK):
            rows = pl.ds(r + c * BLK * dil, BLK, stride=dil)
            inp = slice(r * length + c * BLK, r * length + (c + 1) * BLK)
            if len(dst.shape) == 2:
                dst[rows, :] = dst[rows, :] + src[inp, :] if accumulate else src[inp, :]
            else:
                for hf in range(2):
                    val = src[inp, _lane_half(hf)]
                    half = dst.at[hf]
                    half[rows, :] = half[rows, :] + val if accumulate else val


def _split_halves(src, dst, seq):
    def step(i, carry):
        for hf in range(2):
            dst[hf, _rows_of(i), :] = src[_rows_of(i), _lane_half(hf)].astype(F32)
        return carry

    lax.fori_loop(0, seq // BLK, step, 0)


def _band_attn_fwd_fused(h, slopes, bsz, seq):
    width = h.shape[1]
    cb = width // 256
    k_off, v_off = MIX_W // 256, 2 * MIX_W // 256
    nb = seq // BLK

    def body(sl_ref, q_ref, k_ref, v_ref, mix_ref, lse_ref, qf, kf, vf, qd, kd, vd, od, ld, o1, o2, o3, l1, l2, l3):
        def run(dil, qs, ks, vs, o_dst, l_dst):
            nblk = seq // dil // BLK

            def block(j, carry):
                rows, prows = _rows_of(j), _rows_of(jnp.maximum(j - 1, 0))
                valid, dist = _block_mask((j % nblk) != 0, dil)
                lses = []
                for pr in range(HEAD_GROUP // 2):
                    lanes = _lane_half(pr)
                    q_ab = _split_pair(qs[rows, lanes])
                    k2 = jnp.concatenate([ks[prows, lanes], ks[rows, lanes]], axis=0)
                    v_ab = _split_pair(jnp.concatenate([vs[prows, lanes], vs[rows, lanes]], axis=0))
                    out = None
                    for ab in range(2):
                        hh = 2 * pr + ab
                        s = _dot_nt(q_ab[ab], k2) * SCALE - sl_ref[hh:hh + 1, 0:1] * dist
                        s = jnp.where(valid, s, NEG)
                        m = jnp.max(s, axis=-1, keepdims=True)
                        p = jnp.exp(s - m)
                        l = jnp.sum(p, axis=-1, keepdims=True)
                        term = _dot_nn(p.astype(BF16), v_ab[ab]) / l
                        out = term if out is None else out + term
                        lses.append(m + jnp.log(l))
                    o_dst[rows, lanes] = out
                l_dst[rows, :] = _spread_stats(lses)
                return carry

            lax.fori_loop(0, nb, block, 0, unroll=2)

        run(1, q_ref, k_ref, v_ref, o1, l1)
        _split_halves(q_ref, qf, seq)
        _split_halves(k_ref, kf, seq)
        _split_halves(v_ref, vf, seq)
        for dil, o_tok, l_tok in ((4, o2, l2), (16, o3, l3)):
            _deinterleave(qf, qd, seq, dil, BF16)
            _deinterleave(kf, kd, seq, dil, BF16)
            _deinterleave(vf, vd, seq, dil, BF16)
            run(dil, qd, kd, vd, od, ld)
            _interleave(od, o_tok, seq, dil, False)
            _interleave(ld, l_tok, seq, dil, False)

        def merge(i, carry):
            rows = _rows_of(i)
            ls = [l1[rows, :], l2[rows, :], l3[rows, :]]
            m = jnp.maximum(jnp.maximum(ls[0], ls[1]), ls[2])
            tot = m + jnp.log(jnp.exp(ls[0] - m) + jnp.exp(ls[1] - m) + jnp.exp(ls[2] - m))
            ws = [jnp.exp(x - tot) for x in ls]
            lse_ref[rows, :] = tot
            for hh in range(HEAD_GROUP):
                sl = slice(hh * HEAD_DIM, (hh + 1) * HEAD_DIM)
                st = slice(hh * STAT_LANES, hh * STAT_LANES + 1)
                hf, hl = hh // 2, slice((hh % 2) * HEAD_DIM, (hh % 2 + 1) * HEAD_DIM)
                acc = ws[0][:, st] * o1[rows, sl] + ws[1][:, st] * o2[hf, rows, hl] + ws[2][:, st] * o3[hf, rows, hl]
                mix_ref[rows, sl] = acc.astype(BF16)
            return carry

        lax.fori_loop(0, nb, merge, 0)

    def hspec(off):
        return pl.BlockSpec((seq, 256), lambda b, g: (b, off + g))

    big = lambda dt: pltpu.VMEM((seq, 256), dt)
    halves = lambda: pltpu.VMEM((2, seq, 128), F32)
    stat = lambda: pltpu.VMEM((seq, 128), F32)
    return pl.pallas_call(
        body, name="band_attn_fwd", grid=(bsz, N_GROUPS),
        in_specs=[pl.BlockSpec((None, 8, 128), lambda b, g: (g, 0, 0)), hspec(0), hspec(k_off), hspec(v_off)],
        out_specs=[pl.BlockSpec((seq, 256), lambda b, g: (b, g)), pl.BlockSpec((seq, 128), lambda b, g: (b, g))],
        out_shape=[_sds((bsz * seq, MIX_W), BF16), _sds((bsz * seq, STAT_W), F32)],
        scratch_shapes=[halves(), halves(), halves(), big(BF16), big(BF16), big(BF16), big(F32), stat(),
                        big(F32), halves(), halves(), stat(), stat(), stat()],
        compiler_params=_params(("parallel", "parallel")))(slopes, h, h, h)


def _band_attn_bwd_fused(h, dcat, mix, lse, slopes, bsz, seq):
    width = h.shape[1]
    k_off, v_off = MIX_W // 256, 2 * MIX_W // 256
    nb = seq // BLK

    def body(sl_ref, q_ref, k_ref, v_ref, do_ref, o_ref, lse_ref, dq_ref, dk_ref, dv_ref,
             qf, kf, vf, dof, ddt, qd, kd, vd, dod, lsd, ddd, gq, gk, gv, aq, ak, av):
        def delta(i, carry):
            rows = _rows_of(i)
            prod = do_ref[rows, :].astype(F32) * o_ref[rows, :].astype(F32)
            ddt[rows, :] = _spread_stats(
                [jnp.sum(prod[:, hh * HEAD_DIM:(hh + 1) * HEAD_DIM], axis=-1, keepdims=True) for hh in range(HEAD_GROUP)])
            return carry

        lax.fori_loop(0, nb, delta, 0)

        def zero(i, carry):
            rows = _rows_of(i)
            for ref in (gk, gv):
                ref[rows, :] = jnp.zeros((BLK, 256), F32)
            return carry

        def run(dil, qs, ks, vs, dos, lss, dds):
            nblk = seq // dil // BLK
            lax.fori_loop(0, nb, zero, 0)

            def block(j, carry):
                rows, prows = _rows_of(j), _rows_of(jnp.maximum(j - 1, 0))
                valid, dist = _block_mask((j % nblk) != 0, dil)
                for pr in range(HEAD_GROUP // 2):
                    lanes = _lane_half(pr)
                    q_ab = _split_pair(qs[rows, lanes])
                    do_ab = _split_pair(dos[rows, lanes])
                    k2 = jnp.concatenate([ks[prows, lanes], ks[rows, lanes]], axis=0)
                    v2 = jnp.concatenate([vs[prows, lanes], vs[rows, lanes]], axis=0)
                    k_ab = _split_pair(k2)
                    dq, dk2, dv2 = None, None, None
                    for ab in range(2):
                        hh = 2 * pr + ab
                        st = slice(hh * STAT_LANES, hh * STAT_LANES + 1)
                        s = _dot_nt(q_ab[ab], k2) * SCALE - sl_ref[hh:hh + 1, 0:1] * dist
                        s = jnp.where(valid, s, NEG)
                        p = jnp.exp(s - lss[rows, st])
                        dp = _dot_nt(do_ab[ab], v2)
                        ds = (p * (dp - dds[rows, st]) * SCALE).astype(BF16)
                        terms = (_dot_nn(ds, k_ab[ab]), _dot_tn(ds, q_ab[ab]), _dot_tn(p.astype(BF16), do_ab[ab]))
                        dq, dk2, dv2 = terms if dq is None else (dq + terms[0], dk2 + terms[1], dv2 + terms[2])
                    gq[rows, lanes] = dq
                    gk[prows, lanes] += dk2[:BLK]
                    gv[prows, lanes] += dv2[:BLK]
                    gk[rows, lanes] += dk2[BLK:]
                    gv[rows, lanes] += dv2[BLK:]
                return carry

            lax.fori_loop(0, nb, block, 0, unroll=2)

        run(1, q_ref, k_ref, v_ref, do_ref, lse_ref, ddt)

        for src, dst in ((gq, aq), (gk, ak), (gv, av), (q_ref, qf), (k_ref, kf), (v_ref, vf), (do_ref, dof)):
            _split_halves(src, dst, seq)
        for dil in (4, 16):
            for src, dst in ((qf, qd), (kf, kd), (vf, vd), (dof, dod)):
                _deinterleave(src, dst, seq, dil, BF16)
            _deinterleave(lse_ref, lsd, seq, dil, F32)
            _deinterleave(ddt, ddd, seq, dil, F32)
            run(dil, qd, kd, vd, dod, lsd, ddd)
            for src, dst in ((gq, aq), (gk, ak), (gv, av)):
                _interleave(src, dst, seq, dil, True)

        def write(i, carry):
            rows = _rows_of(i)
            for src, dst in ((aq, dq_ref), (ak, dk_ref), (av, dv_ref)):
                for hf in range(2):
                    dst[rows, _lane_half(hf)] = src[hf, rows, :].astype(BF16)
            return carry

        lax.fori_loop(0, nb, write, 0)

    def hspec(off):
        return pl.BlockSpec((seq, 256), lambda b, g: (b, off + g))

    io = pl.BlockSpec((seq, 256), lambda b, g: (b, g))
    big = lambda dt: pltpu.VMEM((seq, 256), dt)
    halves = lambda: pltpu.VMEM((2, seq, 128), F32)
    stat = lambda: pltpu.VMEM((seq, 128), F32)
    return pl.pallas_call(
        body, name="band_attn_bwd", grid=(bsz, N_GROUPS),
        in_specs=[pl.BlockSpec((None, 8, 128), lambda b, g: (g, 0, 0)), hspec(0), hspec(k_off), hspec(v_off), io, io,
                  pl.BlockSpec((seq, 128), lambda b, g: (b, g))],
        out_specs=[io, io, io], out_shape=[_sds((bsz * seq, MIX_W), BF16)] * 3,
        scratch_shapes=[halves(), halves(), halves(), halves(), stat(),
                        big(BF16), big(BF16), big(BF16), big(BF16), stat(), stat(),
                        big(F32), big(F32), big(F32), halves(), halves(), halves()],
        compiler_params=_params(("parallel", "parallel")))(slopes, h, h, h, dcat, mix, lse)


def _mem_attn_fwd(h, mkv, bsz, seq, q_col, tq=512):
    nq = seq // tq

    def body(q_ref, kv_ref, o_ref):
        q = q_ref[...]
        kv = kv_ref[...]
        for hh in range(4):
            sl = slice(hh * HEAD_DIM, (hh + 1) * HEAD_DIM)
            s = _dot_nt(q[:, sl], kv[:, sl]) * SCALE
            m = jnp.max(s, axis=-1, keepdims=True)
            p = jnp.exp(s - m)
            l = jnp.sum(p, axis=-1, keepdims=True)
            acc = _dot_nn(p.astype(BF16), kv[:, MEM_W + hh * HEAD_DIM:MEM_W + (hh + 1) * HEAD_DIM])
            o_ref[:, sl] = (acc / l).astype(BF16)

    return pl.pallas_call(
        body, name="mem_attn_fwd", grid=(bsz, nq),
        in_specs=[pl.BlockSpec((tq, MEM_W), lambda b, i: (b * nq + i, q_col)),
                  pl.BlockSpec((N_MEM, 2 * MEM_W), lambda b, i: (b, 0))],
        out_specs=pl.BlockSpec((tq, MEM_W), lambda b, i: (b * nq + i, 0)),
        out_shape=_sds((bsz * seq, MEM_W), BF16), compiler_params=_params(("parallel", "parallel")))(h, mkv)


def _mem_attn_bwd(h, mkv, dcat, bsz, seq, q_col, tq=512):
    nq = seq // tq
    do_col = MIX_W // MEM_W

    def body(q_ref, kv_ref, do_ref, dq_ref, dkv_ref):
        @pl.when(pl.program_id(1) == 0)
        def _():
            dkv_ref[...] = jnp.zeros_like(dkv_ref)

        q = q_ref[...]
        kv = kv_ref[...]
        do = do_ref[...]
        for hh in range(4):
            sl = slice(hh * HEAD_DIM, (hh + 1) * HEAD_DIM)
            vsl = slice(MEM_W + hh * HEAD_DIM, MEM_W + (hh + 1) * HEAD_DIM)
            s = _dot_nt(q[:, sl], kv[:, sl]) * SCALE
            m = jnp.max(s, axis=-1, keepdims=True)
            e = jnp.exp(s - m)
            p = e / jnp.sum(e, axis=-1, keepdims=True)
            dp = _dot_nt(do[:, sl], kv[:, vsl])
            dd = jnp.sum(p * dp, axis=-1, keepdims=True)
            ds = (p * (dp - dd) * SCALE).astype(BF16)
            dq_ref[:, sl] = _dot_nn(ds, kv[:, sl]).astype(BF16)
            dkv_ref[:, sl] += _dot_tn(ds, q[:, sl])
            dkv_ref[:, vsl] += _dot_tn(p.astype(BF16), do[:, sl])

    return pl.pallas_call(
        body, name="mem_attn_bwd", grid=(bsz, nq),
        in_specs=[pl.BlockSpec((tq, MEM_W), lambda b, i: (b * nq + i, q_col)),
                  pl.BlockSpec((N_MEM, 2 * MEM_W), lambda b, i: (b, 0)),
                  pl.BlockSpec((tq, MEM_W), lambda b, i: (b * nq + i, do_col))],
        out_specs=[pl.BlockSpec((tq, MEM_W), lambda b, i: (b * nq + i, 0)),
                   pl.BlockSpec((N_MEM, 2 * MEM_W), lambda b, i: (b, 0))],
        out_shape=[_sds((bsz * seq, MEM_W), BF16), _sds((bsz * N_MEM, 2 * MEM_W), F32)],
        compiler_params=_params(("parallel", "arbitrary")))(h, mkv, dcat)


_GELU_C = math.sqrt(2.0 / math.pi)
_GELU_A = 0.044715


def _gelu(x):
    return 0.5 * x * (1.0 + jnp.tanh(_GELU_C * (x + _GELU_A * x * x * x)))


def _gelu_grad(x):
    th = jnp.tanh(_GELU_C * (x + _GELU_A * x * x * x))
    return 0.5 * (1.0 + th) + 0.5 * x * (1.0 - th * th) * (_GELU_C * (1.0 + 3.0 * _GELU_A * x * x))


def _tril_mask(lower):
    ri = lax.broadcasted_iota(jnp.int32, (BLK, BLK), 0)
    ci = lax.broadcasted_iota(jnp.int32, (BLK, BLK), 1)
    return (ri >= ci) if lower else (ci >= ri)


def _sgu_fwd(h, ws, bs_t, ln_g, ln_b, tm=512):
    t = h.shape[0]

    def body(u_ref, v_ref, ws_ref, bs_ref, g_ref, b_ref, o_ref):
        ug = _gelu(u_ref[...].astype(F32))
        vhat, _ = _ln_hat(_gelu(v_ref[...].astype(F32)))
        vn = (vhat * g_ref[...] + b_ref[...]).astype(BF16)
        mask = _tril_mask(True)
        for g in range(N_HEADS):
            sl = slice(g * HEAD_DIM, (g + 1) * HEAD_DIM)
            w = jnp.where(mask, ws_ref[g], 0).astype(BF16)
            for c in range(tm // BLK):
                rs = slice(c * BLK, (c + 1) * BLK)
                mixed = _dot_nn(w, vn[rs, sl]) + bs_ref[:, g:g + 1]
                o_ref[rs, sl] = (ug[rs, sl] * mixed).astype(BF16)

    return pl.pallas_call(
        body, name="sgu_fwd", grid=(t // tm,),
        in_specs=[_rows(tm, MIX_W, 0), _rows(tm, MIX_W, 1), _whole(ws.shape), _whole(bs_t.shape), _whole(ln_g.shape), _whole(ln_b.shape)],
        out_specs=_rows(tm, MIX_W), out_shape=_sds((t, MIX_W), BF16),
        compiler_params=_params(("parallel",)))(h, h, ws, bs_t, ln_g, ln_b)


def _sgu_bwd(h, dcat, ws, ws_t, bs_t, ln_g, ln_b, tm=512):
    t = h.shape[0]

    def body(u_ref, v_ref, do_ref, ws_ref, wst_ref, bs_ref, g_ref, b_ref, dh_ref, dws_ref, dbs_ref, dg_ref, db_ref, dvn_ref):
        @pl.when(pl.program_id(0) == 0)
        def _():
            dws_ref[...] = jnp.zeros_like(dws_ref)
            dbs_ref[...] = jnp.zeros_like(dbs_ref)
            dg_ref[...] = jnp.zeros_like(dg_ref)
            db_ref[...] = jnp.zeros_like(db_ref)

        u = u_ref[...].astype(F32)
        v = v_ref[...].astype(F32)
        do = do_ref[...].astype(F32)
        ug = _gelu(u)
        vhat, rstd = _ln_hat(_gelu(v))
        vn = (vhat * g_ref[...] + b_ref[...]).astype(BF16)
        dmixed_f = do * ug
        dmixed = dmixed_f.astype(BF16)
        low, upp = _tril_mask(True), _tril_mask(False)
        for g in range(N_HEADS):
            sl = slice(g * HEAD_DIM, (g + 1) * HEAD_DIM)
            w = jnp.where(low, ws_ref[g], 0).astype(BF16)
            wt = jnp.where(upp, wst_ref[g], 0).astype(BF16)
            dws_acc = None
            dbs_acc = None
            for c in range(tm // BLK):
                rs = slice(c * BLK, (c + 1) * BLK)
                mixed = _dot_nn(w, vn[rs, sl]) + bs_ref[:, g:g + 1]
                dh_ref[rs, sl] = (do[rs, sl] * mixed * _gelu_grad(u[rs, sl])).astype(BF16)
                dm = dmixed[rs, sl]
                term = _dot_nt(dm, vn[rs, sl])
                dws_acc = term if dws_acc is None else dws_acc + term
                rsum = jnp.sum(dmixed_f[rs, sl], axis=-1, keepdims=True)
                dbs_acc = rsum if dbs_acc is None else dbs_acc + rsum
                dvn_ref[rs, sl] = _dot_nn(wt, dm)
            dws_ref[g] += jnp.where(low, dws_acc, 0.0)
            dbs_ref[:, g:g + 1] += dbs_acc
        dvn = dvn_ref[...]
        dg_ref[...] += jnp.sum(dvn * vhat, axis=0, keepdims=True)
        db_ref[...] += jnp.sum(dvn, axis=0, keepdims=True)
        dxh = dvn * g_ref[...]
        m1 = jnp.mean(dxh, axis=-1, keepdims=True)
        m2 = jnp.mean(dxh * vhat, axis=-1, keepdims=True)
        dvg = rstd * (dxh - m1 - vhat * m2)
        dh_ref[:, MIX_W:] = (dvg * _gelu_grad(v)).astype(BF16)

    return pl.pallas_call(
        body, name="sgu_bwd", grid=(t // tm,),
        in_specs=[_rows(tm, MIX_W, 0), _rows(tm, MIX_W, 1), _rows(tm, MIX_W, 0), _whole(ws.shape), _whole(ws_t.shape),
                  _whole(bs_t.shape), _whole(ln_g.shape), _whole(ln_b.shape)],
        out_specs=[_rows(tm, 2 * MIX_W), _whole(ws.shape), _whole(bs_t.shape), _whole((1, MIX_W)), _whole((1, MIX_W))],
        out_shape=[_sds((t, 2 * MIX_W), BF16), _sds(ws.shape, F32), _sds(bs_t.shape, F32), _sds((1, MIX_W), F32), _sds((1, MIX_W), F32)],
        scratch_shapes=[pltpu.VMEM((tm, MIX_W), F32)],
        compiler_params=_params(("arbitrary",)))(h, h, dcat, ws, ws_t, bs_t, ln_g, ln_b)


def _row_tile(rows, cols, itemsize=4, limit=2 ** 20):
    best = rows
    for cand in (4096, 2048, 1024, 512, 256, 128, 64, 32, 16):
        if rows % cand == 0 and rows > cand:
            best = cand
            if cand * cols * itemsize <= limit:
                break
    return best


def _adamw(w, m, v, grad=None, parts=None):
    rows, cols = w.shape
    tr = _row_tile(rows, cols)

    def body(w_ref, m_ref, v_ref, g_ref, go_ref, d_ref, nm_ref, nv_ref):
        if parts is None:
            gv = g_ref[...]
        else:
            gv = g_ref[0].astype(F32)
            for k in range(1, 4):
                gv = gv + g_ref[k].astype(F32)
        nm = ADAM_B1 * m_ref[...] + (1.0 - ADAM_B1) * gv
        nv = ADAM_B2 * v_ref[...] + (1.0 - ADAM_B2) * (gv * gv)
        m_hat = nm / (1.0 - ADAM_B1 ** ADAM_STEP)
        v_hat = nv / (1.0 - ADAM_B2 ** ADAM_STEP)
        go_ref[...] = gv
        d_ref[...] = -ADAM_LR * (m_hat / (jnp.sqrt(v_hat) + ADAM_EPS) + ADAM_WD * w_ref[...])
        nm_ref[...] = nm
        nv_ref[...] = nv

    spec = _rows(tr, cols)
    g_spec = spec if parts is None else pl.BlockSpec((4, tr, cols), lambda i: (0, i, 0))
    return pl.pallas_call(
        body, name="adamw" if parts is None else "adamw_sum_chips", grid=(rows // tr,), in_specs=[spec] * 3 + [g_spec],
        out_specs=[spec] * 4, out_shape=[_sds(w.shape, F32)] * 4,
        compiler_params=_params(("parallel",)))(w, m, v, grad if parts is None else parts)


_ANY = pl.BlockSpec(memory_space=pl.ANY)
_MESH = pl.DeviceIdType.MESH


def _all_gather(name, blocks):
    nt = len(blocks)

    def body(*refs):
        x_refs, out_refs = refs[:nt], refs[nt:2 * nt]
        send_sems, recv_sems, local_sems = refs[2 * nt:]
        x, y, c = lax.axis_index("x"), lax.axis_index("y"), lax.axis_index("c")
        me, sibling = (x, y, c), (x, y, 1 - c)
        chips = [(1 - x, y), (x, 1 - y), (1 - x, 1 - y)]

        def slot(t, px, py, pc):
            return out_refs[t].at[4 * px + 2 * py + pc]

        def copy(t, k, blk, to, src=None):
            return pltpu.make_async_remote_copy(
                src_ref=slot(t, *blk) if src is None else src, dst_ref=slot(t, *blk),
                send_sem=send_sems.at[t, k], recv_sem=recv_sems.at[t, k], device_id=to, device_id_type=_MESH)

        mine = [pltpu.make_async_copy(x_refs[t], slot(t, *me), local_sems.at[t]) for t in range(nt)]
        for cp in mine:
            cp.start()
        first = []
        for t in range(nt):
            first.append(copy(t, 0, me, sibling, src=x_refs[t]))
            first += [copy(t, 1 + j, me, (*chip, c), src=x_refs[t]) for j, chip in enumerate(chips)]
        for cp in first:
            cp.start()
        passed = []
        for j, chip in enumerate(chips):
            for t in range(nt):
                copy(t, 1 + j, (*chip, c), me).wait_recv()
                fwd = copy(t, 4 + j, (*chip, c), sibling)
                fwd.start()
                passed.append(fwd)
        for t in range(nt):
            copy(t, 0, sibling, me).wait_recv()
        for j, chip in enumerate(chips):
            for t in range(nt):
                copy(t, 4 + j, (*chip, 1 - c), me).wait_recv()
        for cp in first + passed:
            cp.wait_send()
        for cp in mine:
            cp.wait()

    return pl.pallas_call(
        body, name=name, out_shape=[_sds((N_DEV,) + b.shape, b.dtype) for b in blocks], in_specs=[_ANY] * nt,
        out_specs=[_ANY] * nt,
        scratch_shapes=[pltpu.SemaphoreType.DMA((nt, 7)), pltpu.SemaphoreType.DMA((nt, 7)), pltpu.SemaphoreType.DMA((nt,))])(*blocks)


def _swap_with_sibling(packed):
    nt = len(packed)

    def body(*refs):
        p_refs, got_refs = refs[:nt], refs[nt:2 * nt]
        send_sems, recv_sems = refs[2 * nt:]
        x, y, c = lax.axis_index("x"), lax.axis_index("y"), lax.axis_index("c")
        copies = [
            pltpu.make_async_remote_copy(
                src_ref=p_refs[t].at[1 - c], dst_ref=got_refs[t], send_sem=send_sems.at[t], recv_sem=recv_sems.at[t],
                device_id=(x, y, 1 - c), device_id_type=_MESH)
            for t in range(nt)]
        for cp in copies:
            cp.start()
        for cp in copies:
            cp.wait_recv()
        for cp in copies:
            cp.wait_send()

    return pl.pallas_call(
        body, name="grad_swap_sibling", out_shape=[_sds(p.shape[1:], p.dtype) for p in packed], in_specs=[_ANY] * nt,
        out_specs=[_ANY] * nt,
        scratch_shapes=[pltpu.SemaphoreType.DMA((nt,)), pltpu.SemaphoreType.DMA((nt,))])(*packed)


def _chip_sum(packed, got):
    _, nchip, rows, cols = packed.shape
    tr = _row_tile(rows, cols, 2)
    core = lax.axis_index("c").astype(jnp.int32).reshape(1)

    def body(c_ref, p_ref, g_ref, o_ref):
        o_ref[...] = (p_ref[...].astype(F32) + g_ref[...].astype(F32)).astype(o_ref.dtype)

    grid_spec = pltpu.PrefetchScalarGridSpec(
        num_scalar_prefetch=1, grid=(nchip, rows // tr),
        in_specs=[pl.BlockSpec((None, None, tr, cols), lambda k, i, c: (c[0], k, i, 0)),
                  pl.BlockSpec((None, tr, cols), lambda k, i, c: (k, i, 0))],
        out_specs=pl.BlockSpec((None, tr, cols), lambda k, i, c: (k, i, 0)))
    return pl.pallas_call(
        body, name="grad_chip_sum", grid_spec=grid_spec, out_shape=_sds(got.shape, got.dtype),
        compiler_params=_params(("parallel", "parallel")))(core, packed, got)


def _exchange_chips(chip_sums):
    nt = len(chip_sums)

    def body(*refs):
        s_refs, got_refs = refs[:nt], refs[nt:2 * nt]
        send_sems, recv_sems, local_sems = refs[2 * nt:]
        x, y, c = lax.axis_index("x"), lax.axis_index("y"), lax.axis_index("c")
        my_chip = 2 * x + y
        chips = [(1 - x, y), (x, 1 - y), (1 - x, 1 - y)]
        mine = [pltpu.make_async_copy(s_refs[t].at[my_chip], got_refs[t].at[my_chip], local_sems.at[t]) for t in range(nt)]
        for cp in mine:
            cp.start()

        def copy(t, j, src_chip, dst_chip):
            px, py = chips[j]
            return pltpu.make_async_remote_copy(
                src_ref=s_refs[t].at[src_chip], dst_ref=got_refs[t].at[dst_chip], send_sem=send_sems.at[t, j],
                recv_sem=recv_sems.at[t, j], device_id=(px, py, c), device_id_type=_MESH)

        sends = [copy(t, j, 2 * px + py, my_chip) for t in range(nt) for j, (px, py) in enumerate(chips)]
        for cp in sends:
            cp.start()
        for j, (px, py) in enumerate(chips):
            for t in range(nt):
                copy(t, j, my_chip, 2 * px + py).wait_recv()
        for cp in sends:
            cp.wait_send()
        for cp in mine:
            cp.wait()

    return pl.pallas_call(
        body, name="grad_exchange_chips", out_shape=[_sds(s.shape, s.dtype) for s in chip_sums], in_specs=[_ANY] * nt,
        out_specs=[_ANY] * nt,
        scratch_shapes=[pltpu.SemaphoreType.DMA((nt, 3)), pltpu.SemaphoreType.DMA((nt, 3)), pltpu.SemaphoreType.DMA((nt,))])(*chip_sums)


def _sum_chips(got):
    _, rows, cols = got.shape
    tr = _row_tile(rows, cols)

    def body(g_ref, o_ref):
        acc = g_ref[0].astype(F32)
        for k in range(1, 4):
            acc = acc + g_ref[k].astype(F32)
        o_ref[...] = acc

    return pl.pallas_call(
        body, name="grad_sum_chips", grid=(rows // tr,), in_specs=[pl.BlockSpec((4, tr, cols), lambda i: (0, i, 0))],
        out_specs=pl.BlockSpec((tr, cols), lambda i: (i, 0)), out_shape=_sds((rows, cols), F32),
        compiler_params=_params(("parallel",)))(got)


_COL_SHARDED = ("a_w_in", "b_w_in", "w_gate", "w_up")
_ROW_SHARDED = ("w_mem_kv", "w_out", "w_down")
_BIG = ("a_w_in", "b_w_in", "w_mem_kv", "w_out", "w_gate", "w_up", "w_down")
_SGU_LN = ("sgu_ln_g", "sgu_ln_b")
_LN4 = ("ln_mix_g", "ln_mix_b", "ln_ffn_g", "ln_ffn_b")
_REPLICATED = ("sgu_w_s", "sgu_b_s") + _LN4


def _unshard(name, gathered):
    if name in _COL_SHARDED or name in _SGU_LN:
        moved = jnp.moveaxis(gathered, 0, -2)
        return moved.reshape(moved.shape[:-2] + (moved.shape[-2] * moved.shape[-1],))
    moved = jnp.moveaxis(gathered, 0, 1)
    return moved.reshape((moved.shape[0], moved.shape[1] * moved.shape[2]) + moved.shape[3:])


def _by_shard(name, full):
    if name in _COL_SHARDED or name in _SGU_LN:
        split = full.reshape(full.shape[:-1] + (N_DEV, full.shape[-1] // N_DEV))
        return jnp.moveaxis(split, -2, 0)
    split = full.reshape((full.shape[0], N_DEV, full.shape[1] // N_DEV) + full.shape[2:])
    return jnp.moveaxis(split, 1, 0)


def _gather_weights(shards):
    gathered = _all_gather("weights_all_gather", [shards[n].astype(BF16) for n in _BIG])
    return {n: _unshard(n, g) for n, g in zip(_BIG, gathered)}


def _gather_small_f32(named):
    names = list(named)
    gathered = _all_gather("small_all_gather", [named[n] for n in names])
    return {n: _unshard(n, g) for n, g in zip(names, gathered)}


def _two_level(by_dest):
    shp = by_dest.shape[1:]
    split = by_dest.astype(BF16).reshape((4, 2) + shp).swapaxes(0, 1)
    return split.reshape(2, 4, int(np.prod(shp[:-1])), shp[-1])


def _reduce_gradients(grads):
    sharded = _BIG + _SGU_LN
    packed = [_two_level(_by_shard(n, grads[n])) for n in sharded]
    ln4 = jnp.stack([grads[n] for n in _LN4])
    rep = [grads["sgu_w_s"].reshape(N_DEV, -1, BLK), grads["sgu_b_s"].reshape(N_DEV, -1, BLK), ln4.reshape(N_DEV, -1, D_MODEL)]
    packed += [_two_level(r) for r in rep]
    got = _swap_with_sibling(packed)
    sums = [_chip_sum(p, g) for p, g in zip(packed, got)]
    parts = _exchange_chips(sums)
    out_parts = dict(zip(sharded, parts[:len(sharded)]))
    mine = [_sum_chips(p) for p in parts[len(sharded):]]
    w_s, b_s, ln_all = _all_gather("replicated_grads_all_gather", mine)
    ln_all = ln_all.reshape(ln4.shape)
    rep_grads = {"sgu_w_s": w_s.reshape(grads["sgu_w_s"].shape), "sgu_b_s": b_s.reshape(grads["sgu_b_s"].shape)}
    rep_grads.update({n: ln_all[i] for i, n in enumerate(_LN4)})
    return out_parts, rep_grads


def _as_2d(a):
    if a.ndim == 1:
        return a.reshape(1, -1)
    return a.reshape(-1, a.shape[-1])


def kernel(x, mem, a_w_in, b_w_in, sgu_ln_g, sgu_ln_b, sgu_w_s, sgu_b_s, w_mem_kv, w_out, ln_mix_g, ln_mix_b, w_gate, w_up, w_down, ln_ffn_g, ln_ffn_b, loss_target, m_a_w_in, m_b_w_in, m_sgu_ln_g, m_sgu_ln_b, m_sgu_w_s, m_sgu_b_s, m_w_mem_kv, m_w_out, m_ln_mix_g, m_ln_mix_b, m_w_gate, m_w_up, m_w_down, m_ln_ffn_g, m_ln_ffn_b, v_a_w_in, v_b_w_in, v_sgu_ln_g, v_sgu_ln_b, v_sgu_w_s, v_sgu_b_s, v_w_mem_kv, v_w_out, v_ln_mix_g, v_ln_mix_b, v_w_gate, v_w_up, v_w_down, v_ln_ffn_g, v_ln_ffn_b):
    names = ("a_w_in", "b_w_in", "sgu_ln_g", "sgu_ln_b", "sgu_w_s", "sgu_b_s", "w_mem_kv", "w_out", "ln_mix_g", "ln_mix_b",
             "w_gate", "w_up", "w_down", "ln_ffn_g", "ln_ffn_b")
    weights = dict(zip(names, (a_w_in, b_w_in, sgu_ln_g, sgu_ln_b, sgu_w_s, sgu_b_s, w_mem_kv, w_out, ln_mix_g, ln_mix_b,
                               w_gate, w_up, w_down, ln_ffn_g, ln_ffn_b)))
    mom_m = dict(zip(names, (m_a_w_in, m_b_w_in, m_sgu_ln_g, m_sgu_ln_b, m_sgu_w_s, m_sgu_b_s, m_w_mem_kv, m_w_out, m_ln_mix_g,
                             m_ln_mix_b, m_w_gate, m_w_up, m_w_down, m_ln_ffn_g, m_ln_ffn_b)))
    mom_v = dict(zip(names, (v_a_w_in, v_b_w_in, v_sgu_ln_g, v_sgu_ln_b, v_sgu_w_s, v_sgu_b_s, v_w_mem_kv, v_w_out, v_ln_mix_g,
                             v_ln_mix_b, v_w_gate, v_w_up, v_w_down, v_ln_ffn_g, v_ln_ffn_b)))
    full = _gather_weights(weights)
    sgu_ln = _gather_small_f32({n: weights[n] for n in _SGU_LN})
    loss_part, grad_x, local = _local_step(x, mem, loss_target, full, sgu_ln, {n: weights[n] for n in _REPLICATED})
    loss = lax.psum(loss_part[0, 0], ("x", "y", "c"))
    parts, rep_grads = _reduce_gradients(local)

    reduced, deltas, new_m, new_v = {}, {}, {}, {}
    for n in names:
        w2, m2, v2 = _as_2d(weights[n]), _as_2d(mom_m[n]), _as_2d(mom_v[n])
        if n in parts:
            outs = _adamw(w2, m2, v2, parts=parts[n])
        else:
            outs = _adamw(w2, m2, v2, grad=_as_2d(rep_grads[n]))
        reduced[n], deltas[n], new_m[n], new_v[n] = (a.reshape(weights[n].shape) for a in outs)

    return (loss, grad_x, *[reduced[n] for n in names], *[deltas[n] for n in names],
            *[new_m[n] for n in names], *[new_v[n] for n in names])


def _local_step(x, mem, loss_target, full, sgu_ln, small):
    sgu_w_s, sgu_b_s = small["sgu_w_s"], small["sgu_b_s"]
    ln_mix_g, ln_mix_b, ln_ffn_g, ln_ffn_b = (small[n] for n in ("ln_mix_g", "ln_mix_b", "ln_ffn_g", "ln_ffn_b"))
    bsz, seq, _ = x.shape
    tokens = bsz * seq
    w_gu = jnp.concatenate([full["w_gate"], full["w_up"]], axis=-1)
    slopes = _alibi_table()

    xf = x.reshape(tokens, D_MODEL)
    xb = xf.astype(BF16)
    memb = mem.reshape(bsz * N_MEM, D_MODEL).astype(BF16)
    tgt = loss_target.reshape(tokens, D_MODEL)

    saved = []
    for i in range(DEPTH):
        j = i // 2
        dil_layer = i % 2 == 0
        w_in = full["a_w_in"][j] if dil_layer else full["b_w_in"][j]
        mkv = _linear_nn("mem_kv", memb, full["w_mem_kv"][i])
        h = _linear_nn("in_proj_a" if dil_layer else "in_proj_b", xb, w_in)
        st = dict(x=xf, xb=xb, h=h, mkv=mkv)
        if dil_layer:
            mix, st["lse"] = _band_attn_fwd_fused(h, slopes, bsz, seq)
            q_col = 3 * MIX_W // MEM_W
        else:
            st["ws"] = sgu_w_s[j]
            st["bs_t"] = sgu_b_s[j].T
            st["ln_g"] = sgu_ln["sgu_ln_g"][j].reshape(1, MIX_W)
            st["ln_b"] = sgu_ln["sgu_ln_b"][j].reshape(1, MIX_W)
            mix = _sgu_fwd(h, st["ws"], st["bs_t"], st["ln_g"], st["ln_b"])
            q_col = 2 * MIX_W // MEM_W
        mo = _mem_attn_fwd(h, mkv, bsz, seq, q_col)
        r1, x1, x1b = _proj_ln_fwd("out_proj_ln", [mix, mo], full["w_out"][i], xf,
                                   ln_mix_g[i].reshape(1, D_MODEL), ln_mix_b[i].reshape(1, D_MODEL))
        gt, up, act = _ffn_up_fwd(x1b, w_gu[i])
        r2, x2, x2b = _proj_ln_fwd("ffn_down_ln", [act], full["w_down"][i], x1,
                                   ln_ffn_g[i].reshape(1, D_MODEL), ln_ffn_b[i].reshape(1, D_MODEL))
        st.update(mix=mix, mo=mo, q_col=q_col, r1=r1, x1b=x1b, gt=gt, up=up, act=act, r2=r2)
        saved.append(st)
        xf, xb = x2, x2b

    dx, loss_part = _loss_fwd_bwd(xf, tgt)

    per_pair = ("a_w_in", "b_w_in", "sgu_ln_g", "sgu_ln_b", "sgu_w_s", "sgu_b_s")
    grads = {n: [None] * (DEPTH // 2 if n in per_pair else DEPTH) for n in _BIG + _SGU_LN + _REPLICATED}
    for i in reversed(range(DEPTH)):
        j = i // 2
        st = saved[i]
        dil_layer = i % 2 == 0
        w_in = full["a_w_in"][j] if dil_layer else full["b_w_in"][j]
        dr2, dr2b, dg, db = _ln_bwd(dx, st["r2"], ln_ffn_g[i].reshape(1, D_MODEL))
        grads["ln_ffn_g"][i], grads["ln_ffn_b"][i] = dg[0], db[0]
        dgu = _ffn_down_bwd(dr2b, full["w_down"][i], st["gt"], st["up"])
        grads["w_down"][i] = _mm_tn("grad_w_down", st["act"], dr2b)
        dx1 = _linear_nt("ffn_up_bwd", [dgu], w_gu[i], dr2, F32)
        dw_gu = _mm_tn("grad_w_gate_up", st["x1b"], dgu)
        grads["w_gate"][i], grads["w_up"][i] = dw_gu[:, :D_FF], dw_gu[:, D_FF:]
        dr1, dr1b, dg, db = _ln_bwd(dx1, st["r1"], ln_mix_g[i].reshape(1, D_MODEL))
        grads["ln_mix_g"][i], grads["ln_mix_b"][i] = dg[0], db[0]
        dcat = _linear_nt("out_proj_bwd", [dr1b], full["w_out"][i], None, BF16)
        grads["w_out"][i] = jnp.concatenate(
            [_mm_tn("grad_w_out_mix", st["mix"], dr1b), _mm_tn("grad_w_out_mem", st["mo"], dr1b)], axis=0)
        dqm, dmkv = _mem_attn_bwd(st["h"], st["mkv"], dcat, bsz, seq, st["q_col"])
        grads["w_mem_kv"][i] = _mm_tn("grad_w_mem_kv", memb, dmkv.astype(BF16))
        if dil_layer:
            dh_parts = _band_attn_bwd_fused(st["h"], dcat, st["mix"], st["lse"], slopes, bsz, seq)
        else:
            ws_t = jnp.swapaxes(st["ws"], -1, -2)
            dh_main, dws, dbs_t, dlg, dlb = _sgu_bwd(st["h"], dcat, st["ws"], ws_t, st["bs_t"], st["ln_g"], st["ln_b"])
            grads["sgu_w_s"][j], grads["sgu_b_s"][j] = dws, dbs_t.T
            grads["sgu_ln_g"][j], grads["sgu_ln_b"][j] = dlg[0], dlb[0]
            dh_parts = [dh_main]
        dx = _linear_nt("in_proj_bwd_a" if dil_layer else "in_proj_bwd_b", [*dh_parts, dqm], w_in, dr1, F32)
        grads["a_w_in" if dil_layer else "b_w_in"][j] = jnp.concatenate(
            [_mm_tn("grad_w_in_part", st["xb"], part) for part in dh_parts] + [_mm_tn("grad_w_in_qm", st["xb"], dqm)], axis=1)
    return loss_part, dx.reshape(x.shape), {n: jnp.stack(g) for n, g in grads.items()}
```

```python
import functools
import math

import numpy as np
import jax
import jax.numpy as jnp
from jax import lax
from jax.experimental import pallas as pl
from jax.experimental.pallas import tpu as pltpu

F32 = jnp.float32
BF16 = jnp.bfloat16

D_MODEL = 1024
DEPTH = 4
N_MEM = 256
HEAD_DIM = 64
N_HEADS = 12
MIX_W = N_HEADS * HEAD_DIM
MEM_W = 4 * HEAD_DIM
DIL_PATTERNS = ((128, 1), (512, 4), (2048, 16))
BLK = 128
HEAD_GROUP = 4
N_GROUPS = N_HEADS // HEAD_GROUP
D_FF = 2816
ALPHA = (2 * DEPTH) ** 0.25
LN_EPS = 1e-5
SCALE = HEAD_DIM ** -0.5
NEG = -1e30
N_DEV = 8

ADAM_LR, ADAM_B1, ADAM_B2, ADAM_EPS, ADAM_WD, ADAM_STEP = 0.001, 0.9, 0.999, 1e-08, 0.01, 10

VMEM_LIMIT = 56 * 2 ** 20
STAT_LANES = 32
STAT_W = N_HEADS * STAT_LANES


def _dot_nn(a, b):
    return lax.dot_general(a, b, (((1,), (0,)), ((), ())), preferred_element_type=F32)


def _dot_nt(a, b):
    return lax.dot_general(a, b, (((1,), (1,)), ((), ())), preferred_element_type=F32)


def _dot_tn(a, b):
    return lax.dot_general(a, b, (((0,), (0,)), ((), ())), preferred_element_type=F32)


def _ln_hat(r):
    mu = jnp.mean(r, axis=-1, keepdims=True)
    xc = r - mu
    var = jnp.mean(xc * xc, axis=-1, keepdims=True)
    rstd = lax.rsqrt(var + LN_EPS)
    return xc * rstd, rstd


def _params(sem):
    return pltpu.CompilerParams(dimension_semantics=sem, vmem_limit_bytes=VMEM_LIMIT)


def _rows(tm, c, col=0):
    return pl.BlockSpec((tm, c), lambda i: (i, col))


def _whole(shape):
    nd = len(shape)
    return pl.BlockSpec(tuple(shape), lambda *_: (0,) * nd)


def _resident(shape):
    nd = len(shape)
    return pl.BlockSpec(tuple(shape), lambda *_: (0,) * nd, pipeline_mode=pl.Buffered(1))


def _sds(shape, dtype):
    return jax.ShapeDtypeStruct(tuple(shape), dtype)


def _linear_nn(name, a, w, tm=512):
    t, k = a.shape
    n = w.shape[1]
    tm = min(tm, t)

    def body(a_ref, w_ref, o_ref):
        o_ref[...] = _dot_nn(a_ref[...], w_ref[...]).astype(BF16)

    return pl.pallas_call(
        body, name=name, grid=(t // tm,), in_specs=[_rows(tm, k), _resident(w.shape)], out_specs=_rows(tm, n),
        out_shape=_sds((t, n), BF16), compiler_params=_params(("parallel",)))(a, w)


def _proj_ln_fwd(name, lhs, w, x_res, g, b, tm=256):
    t = x_res.shape[0]
    n_lhs = len(lhs)

    def body(*refs):
        lhs_refs = refs[:n_lhs]
        w_ref, x_ref, g_ref, b_ref, r_ref, xn_ref, xnb_ref = refs[n_lhs:]
        y, off = None, 0
        for lr in lhs_refs:
            k = lr.shape[1]
            term = _dot_nn(lr[...], w_ref[off:off + k, :])
            y = term if y is None else y + term
            off += k
        r = ALPHA * x_ref[...] + y
        xhat, _ = _ln_hat(r)
        xn = xhat * g_ref[...] + b_ref[...]
        r_ref[...] = r
        xn_ref[...] = xn
        xnb_ref[...] = xn.astype(BF16)

    in_specs = [_rows(tm, a.shape[1]) for a in lhs] + [_resident(w.shape), _rows(tm, D_MODEL), _whole(g.shape), _whole(b.shape)]
    return pl.pallas_call(
        body, name=name, grid=(t // tm,), in_specs=in_specs,
        out_specs=[_rows(tm, D_MODEL)] * 3,
        out_shape=[_sds((t, D_MODEL), F32), _sds((t, D_MODEL), F32), _sds((t, D_MODEL), BF16)],
        compiler_params=_params(("parallel",)))(*lhs, w, x_res, g, b)


def _ffn_up_fwd(xb, wgu, tm=256):
    t = xb.shape[0]

    def body(x_ref, w_ref, g_ref, u_ref, a_ref):
        gu = _dot_nn(x_ref[...], w_ref[...])
        gt, up = gu[:, :D_FF], gu[:, D_FF:]
        g_ref[...] = gt.astype(BF16)
        u_ref[...] = up.astype(BF16)
        a_ref[...] = (gt * jax.nn.sigmoid(gt) * up).astype(BF16)

    return pl.pallas_call(
        body, name="ffn_up_fwd", grid=(t // tm,), in_specs=[_rows(tm, D_MODEL), _resident(wgu.shape)],
        out_specs=[_rows(tm, D_FF)] * 3, out_shape=[_sds((t, D_FF), BF16)] * 3,
        compiler_params=_params(("parallel",)))(xb, wgu)


def _loss_fwd_bwd(xn, tgt, tm=512):
    t = xn.shape[0]

    def body(x_ref, t_ref, dx_ref, l_ref):
        @pl.when(pl.program_id(0) == 0)
        def _():
            l_ref[...] = jnp.zeros_like(l_ref)

        e = x_ref[...] - t_ref[...]
        dx_ref[...] = e * (1.0 / D_MODEL)
        l_ref[...] += jnp.sum(e * e) * (0.5 / D_MODEL)

    return pl.pallas_call(
        body, name="loss", grid=(t // tm,), in_specs=[_rows(tm, D_MODEL)] * 2,
        out_specs=[_rows(tm, D_MODEL), _whole((1, 128))], out_shape=[_sds((t, D_MODEL), F32), _sds((1, 128), F32)],
        compiler_params=_params(("arbitrary",)))(xn, tgt)


def _ln_bwd(dx, r, g, tm=512):
    t = dx.shape[0]

    def body(dx_ref, r_ref, g_ref, dr_ref, drb_ref, dg_ref, db_ref):
        @pl.when(pl.program_id(0) == 0)
        def _():
            dg_ref[...] = jnp.zeros_like(dg_ref)
            db_ref[...] = jnp.zeros_like(db_ref)

        dxv = dx_ref[...]
        xhat, rstd = _ln_hat(r_ref[...])
        dxh = dxv * g_ref[...]
        m1 = jnp.mean(dxh, axis=-1, keepdims=True)
        m2 = jnp.mean(dxh * xhat, axis=-1, keepdims=True)
        dr = rstd * (dxh - m1 - xhat * m2)
        dr_ref[...] = dr
        drb_ref[...] = dr.astype(BF16)
        dg_ref[...] += jnp.sum(dxv * xhat, axis=0, keepdims=True)
        db_ref[...] += jnp.sum(dxv, axis=0, keepdims=True)

    return pl.pallas_call(
        body, name="ln_bwd", grid=(t // tm,), in_specs=[_rows(tm, D_MODEL), _rows(tm, D_MODEL), _whole(g.shape)],
        out_specs=[_rows(tm, D_MODEL), _rows(tm, D_MODEL), _whole((1, D_MODEL)), _whole((1, D_MODEL))],
        out_shape=[_sds((t, D_MODEL), F32), _sds((t, D_MODEL), BF16), _sds((1, D_MODEL), F32), _sds((1, D_MODEL), F32)],
        compiler_params=_params(("arbitrary",)))(dx, r, g)


def _ffn_down_bwd(drb, wd, gt, up, tm=256):
    t = drb.shape[0]

    def body(d_ref, w_ref, g_ref, u_ref, o_ref):
        da = _dot_nt(d_ref[...], w_ref[...])
        g = g_ref[...].astype(F32)
        u = u_ref[...].astype(F32)
        sg = jax.nn.sigmoid(g)
        o_ref[:, :D_FF] = (da * u * (sg * (1.0 + g * (1.0 - sg)))).astype(BF16)
        o_ref[:, D_FF:] = (da * (g * sg)).astype(BF16)

    return pl.pallas_call(
        body, name="ffn_down_bwd", grid=(t // tm,),
        in_specs=[_rows(tm, D_MODEL), _resident(wd.shape), _rows(tm, D_FF), _rows(tm, D_FF)],
        out_specs=_rows(tm, 2 * D_FF), out_shape=_sds((t, 2 * D_FF), BF16),
        compiler_params=_params(("parallel",)))(drb, wd, gt, up)


def _linear_nt(name, lhs, w, res, out_dtype, tm=256):
    t = lhs[0].shape[0]
    n_lhs = len(lhs)
    n_out = w.shape[0]

    def body(*refs):
        lhs_refs = refs[:n_lhs]
        w_ref = refs[n_lhs]
        o_ref = refs[-1]
        y, off = None, 0
        for lr in lhs_refs:
            k = lr.shape[1]
            term = _dot_nt(lr[...], w_ref[:, off:off + k])
            y = term if y is None else y + term
            off += k
        if res is not None:
            y = ALPHA * refs[n_lhs + 1][...] + y
        o_ref[...] = y.astype(out_dtype)

    in_specs = [_rows(tm, a.shape[1]) for a in lhs] + [_resident(w.shape)]
    args = list(lhs) + [w]
    if res is not None:
        in_specs.append(_rows(tm, n_out))
        args.append(res)
    return pl.pallas_call(
        body, name=name, grid=(t // tm,), in_specs=in_specs, out_specs=_rows(tm, n_out),
        out_shape=_sds((t, n_out), out_dtype), compiler_params=_params(("parallel",)))(*args)


def _pick_tile(n, limit):
    if n <= limit:
        return n
    best = 128
    for cand in range(128, limit + 1, 128):
        if n % cand == 0:
            best = cand
    return best


def _mm_tn(name, a, b, tt=1024):
    t, k = a.shape
    n = b.shape[1]
    tt = min(tt, t)
    tk = _pick_tile(k, 1408)
    tn = _pick_tile(n, (6 * 2 ** 20) // (4 * tk) // 128 * 128)

    def body(a_ref, b_ref, o_ref):
        @pl.when(pl.program_id(2) == 0)
        def _():
            o_ref[...] = jnp.zeros_like(o_ref)

        o_ref[...] += _dot_tn(a_ref[...], b_ref[...])

    return pl.pallas_call(
        body, name=name, grid=(k // tk, n // tn, t // tt),
        in_specs=[pl.BlockSpec((tt, tk), lambda i, j, s: (s, i)), pl.BlockSpec((tt, tn), lambda i, j, s: (s, j))],
        out_specs=pl.BlockSpec((tk, tn), lambda i, j, s: (i, j)), out_shape=_sds((k, n), F32),
        compiler_params=_params(("parallel", "parallel", "arbitrary")))(a, b)


def _alibi_table():
    arr = np.zeros((N_GROUPS, 8, 128), np.float32)
    for g in range(N_GROUPS):
        for hh in range(HEAD_GROUP):
            arr[g, hh, :] = 2.0 ** (-8.0 * (g * HEAD_GROUP + hh + 1) / N_HEADS)
    return jnp.asarray(arr)


def _band_mask(n, dil):
    qi = lax.broadcasted_iota(jnp.int32, (BLK, 2 * BLK), 0)
    ki = lax.broadcasted_iota(jnp.int32, (BLK, 2 * BLK), 1)
    steps = qi + BLK - ki
    valid = (steps >= 0) & (steps <= BLK) & ((ki >= BLK) | (n > 0))
    return valid, (steps * dil).astype(F32)


def _band_specs(bsz, seq, dil, width):
    cb = width // 256

    def spec(off, prev=False):
        if prev:
            return pl.BlockSpec((None, BLK, 256), lambda b, r, g, n: (b, jnp.maximum(n - 1, 0), r * cb + off + g))
        return pl.BlockSpec((None, BLK, 256), lambda b, r, g, n: (b, n, r * cb + off + g))

    return spec


def _spread_stats(cols):
    lane = lax.broadcasted_iota(jnp.int32, (BLK, HEAD_GROUP * STAT_LANES), 1)
    tile = cols[HEAD_GROUP - 1]
    for hh in range(HEAD_GROUP - 2, -1, -1):
        tile = jnp.where(lane < (hh + 1) * STAT_LANES, cols[hh], tile)
    return tile


def _band_attn_fwd(h, slopes, bsz, seq, dil):
    width = h.shape[1]
    length = seq // dil
    nblk = length // BLK
    hv = h.reshape(bsz, length, dil * width)
    spec = _band_specs(bsz, seq, dil, width)
    k_off, v_off = MIX_W // 256, 2 * MIX_W // 256

    def body(sl_ref, q_ref, kc_ref, kp_ref, vc_ref, vp_ref, o_ref, lse_ref):
        valid, dist = _band_mask(pl.program_id(3), dil)
        q = q_ref[...]
        k2 = jnp.concatenate([kp_ref[...], kc_ref[...]], axis=0)
        v2 = jnp.concatenate([vp_ref[...], vc_ref[...]], axis=0)
        lses = []
        for hh in range(HEAD_GROUP):
            sl = slice(hh * HEAD_DIM, (hh + 1) * HEAD_DIM)
            s = _dot_nt(q[:, sl], k2[:, sl]) * SCALE - sl_ref[hh:hh + 1, 0:1] * dist
            s = jnp.where(valid, s, NEG)
            m = jnp.max(s, axis=-1, keepdims=True)
            p = jnp.exp(s - m)
            l = jnp.sum(p, axis=-1, keepdims=True)
            acc = _dot_nn(p.astype(BF16), v2[:, sl])
            o_ref[:, sl] = (acc / l).astype(BF16)
            lses.append(m + jnp.log(l))
        lse_ref[...] = _spread_stats(lses)

    out, lse = pl.pallas_call(
        body, name=f"band_attn_fwd_d{dil}", grid=(bsz, dil, N_GROUPS, nblk),
        in_specs=[pl.BlockSpec((None, 8, 128), lambda b, r, g, n: (g, 0, 0)),
                  spec(0), spec(k_off), spec(k_off, True), spec(v_off), spec(v_off, True)],
        out_specs=[pl.BlockSpec((None, BLK, 256), lambda b, r, g, n: (b, n, r * N_GROUPS + g)),
                   pl.BlockSpec((None, BLK, 128), lambda b, r, g, n: (b, n, r * N_GROUPS + g))],
        out_shape=[_sds((bsz, length, dil * MIX_W), BF16), _sds((bsz, length, dil * STAT_W), F32)],
        compiler_params=_params(("parallel", "parallel", "parallel", "arbitrary")))(slopes, hv, hv, hv, hv, hv)
    return out.reshape(bsz * seq, MIX_W), lse.reshape(bsz * seq, STAT_W)


def _band_merge(outs, lses, tm=512):
    t = outs[0].shape[0]

    def body(o1, o2, o3, l1, l2, l3, mix_ref, lse_ref):
        ls = [l1[...], l2[...], l3[...]]
        m = jnp.maximum(jnp.maximum(ls[0], ls[1]), ls[2])
        tot = m + jnp.log(jnp.exp(ls[0] - m) + jnp.exp(ls[1] - m) + jnp.exp(ls[2] - m))
        ws = [jnp.exp(x - tot) for x in ls]
        lse_ref[...] = tot
        for hd in range(N_HEADS):
            sl = slice(hd * HEAD_DIM, (hd + 1) * HEAD_DIM)
            acc = None
            for w, o in zip(ws, (o1, o2, o3)):
                term = w[:, hd * STAT_LANES:hd * STAT_LANES + 1] * o[:, sl].astype(F32)
                acc = term if acc is None else acc + term
            mix_ref[:, sl] = acc.astype(BF16)

    return pl.pallas_call(
        body, name="band_merge", grid=(t // tm,), in_specs=[_rows(tm, MIX_W)] * 3 + [_rows(tm, STAT_W)] * 3,
        out_specs=[_rows(tm, MIX_W), _rows(tm, STAT_W)], out_shape=[_sds((t, MIX_W), BF16), _sds((t, STAT_W), F32)],
        compiler_params=_params(("parallel",)))(*outs, *lses)


def _band_delta(dcat, mix, tm=512):
    t = mix.shape[0]

    def body(d_ref, o_ref, dd_ref):
        prod = d_ref[...].astype(F32) * o_ref[...].astype(F32)
        for hd in range(N_HEADS):
            rsum = jnp.sum(prod[:, hd * HEAD_DIM:(hd + 1) * HEAD_DIM], axis=-1, keepdims=True)
            dd_ref[:, hd * STAT_LANES:(hd + 1) * STAT_LANES] = jnp.broadcast_to(rsum, (tm, STAT_LANES))

    return pl.pallas_call(
        body, name="band_delta", grid=(t // tm,), in_specs=[_rows(tm, MIX_W), _rows(tm, MIX_W)],
        out_specs=_rows(tm, STAT_W), out_shape=_sds((t, STAT_W), F32),
        compiler_params=_params(("parallel",)))(dcat, mix)


def _band_attn_bwd(h, dcat, slopes, lse, delta, bsz, seq, dil):
    width = h.shape[1]
    length = seq // dil
    nblk = length // BLK
    hv = h.reshape(bsz, length, dil * width)
    dv_ = dcat.reshape(bsz, length, dil * D_MODEL)
    k_off, v_off = MIX_W // 256, 2 * MIX_W // 256
    cb, dcb = width // 256, D_MODEL // 256

    def cur(off, c):
        return pl.BlockSpec((None, BLK, 256), lambda b, r, g, n: (b, jnp.minimum(n, nblk - 1), r * c + off + g))

    def prev(off, c):
        return pl.BlockSpec((None, BLK, 256), lambda b, r, g, n: (b, jnp.maximum(jnp.minimum(n, nblk - 1) - 1, 0), r * c + off + g))

    stat = pl.BlockSpec((None, BLK, 128), lambda b, r, g, n: (b, jnp.minimum(n, nblk - 1), r * N_GROUPS + g))
    dq_spec = pl.BlockSpec((None, BLK, 256), lambda b, r, g, n: (b, jnp.minimum(n, nblk - 1), r * N_GROUPS + g))
    dkv_spec = pl.BlockSpec((None, BLK, 256), lambda b, r, g, n: (b, jnp.maximum(n - 1, 0), r * N_GROUPS + g))

    def body(sl_ref, q_ref, kc_ref, kp_ref, vc_ref, vp_ref, do_ref, lse_ref, dd_ref, dq_ref, dk_ref, dv_ref, kcar, vcar):
        n = pl.program_id(3)

        @pl.when(n == 0)
        def _():
            kcar[...] = jnp.zeros_like(kcar)
            vcar[...] = jnp.zeros_like(vcar)

        @pl.when(n < nblk)
        def _():
            valid, dist = _band_mask(n, dil)
            q = q_ref[...]
            do = do_ref[...]
            k2 = jnp.concatenate([kp_ref[...], kc_ref[...]], axis=0)
            v2 = jnp.concatenate([vp_ref[...], vc_ref[...]], axis=0)
            for hh in range(HEAD_GROUP):
                sl = slice(hh * HEAD_DIM, (hh + 1) * HEAD_DIM)
                s = _dot_nt(q[:, sl], k2[:, sl]) * SCALE - sl_ref[hh:hh + 1, 0:1] * dist
                s = jnp.where(valid, s, NEG)
                st = slice(hh * STAT_LANES, hh * STAT_LANES + 1)
                p = jnp.exp(s - lse_ref[:, st])
                dp = _dot_nt(do[:, sl], v2[:, sl])
                ds = (p * (dp - dd_ref[:, st]) * SCALE).astype(BF16)
                dq_ref[:, sl] = _dot_nn(ds, k2[:, sl]).astype(BF16)
                dk2 = _dot_tn(ds, q[:, sl])
                dv2 = _dot_tn(p.astype(BF16), do[:, sl])
                dk_ref[:, sl] = (kcar[:, sl] + dk2[:BLK]).astype(BF16)
                dv_ref[:, sl] = (vcar[:, sl] + dv2[:BLK]).astype(BF16)
                kcar[:, sl] = dk2[BLK:]
                vcar[:, sl] = dv2[BLK:]

        @pl.when(n == nblk)
        def _():
            dk_ref[...] = kcar[...].astype(BF16)
            dv_ref[...] = vcar[...].astype(BF16)

    outs = pl.pallas_call(
        body, name=f"band_attn_bwd_d{dil}", grid=(bsz, dil, N_GROUPS, nblk + 1),
        in_specs=[pl.BlockSpec((None, 8, 128), lambda b, r, g, n: (g, 0, 0)),
                  cur(0, cb), cur(k_off, cb), prev(k_off, cb), cur(v_off, cb), prev(v_off, cb), cur(0, dcb), stat, stat],
        out_specs=[dq_spec, dkv_spec, dkv_spec],
        out_shape=[_sds((bsz, length, dil * MIX_W), BF16)] * 3,
        scratch_shapes=[pltpu.VMEM((BLK, 256), F32), pltpu.VMEM((BLK, 256), F32)],
        compiler_params=_params(("parallel", "parallel", "parallel", "arbitrary")))(
            slopes, hv, hv, hv, hv, hv, dv_, lse.reshape(bsz, length, dil * STAT_W), delta.reshape(bsz, length, dil * STAT_W))
    return [o.reshape(bsz * seq, MIX_W) for o in outs]


def _sum_patterns(parts, tm=512):
    t = parts[0][0].shape[0]

    def body(*refs):
        o_ref = refs[-1]
        for j in range(3):
            acc = refs[j][...].astype(F32) + refs[3 + j][...].astype(F32) + refs[6 + j][...].astype(F32)
            o_ref[:, j * MIX_W:(j + 1) * MIX_W] = acc.astype(BF16)

    flat = [x for p in parts for x in p]
    return pl.pallas_call(
        body, name="band_sum", grid=(t // tm,), in_specs=[_rows(tm, MIX_W)] * 9, out_specs=_rows(tm, 3 * MIX_W),
        out_shape=_sds((t, 3 * MIX_W), BF16), compiler_params=_params(("parallel",)))(*flat)


def _block_mask(has_prev, dil):
    qi = lax.broadcasted_iota(jnp.int32, (BLK, 2 * BLK), 0)
    ki = lax.broadcasted_iota(jnp.int32, (BLK, 2 * BLK), 1)
    steps = qi + BLK - ki
    valid = (steps >= 0) & (steps <= BLK) & ((ki >= BLK) | has_prev)
    return valid, (steps * dil).astype(F32)


def _rows_of(j):
    return pl.ds(pl.multiple_of(j * BLK, BLK), BLK)


def _lane_half(hf):
    return slice(hf * 128, (hf + 1) * 128)


def _split_pair(x):
    first = lax.broadcasted_iota(jnp.int32, (1, 2 * HEAD_DIM), 1) < HEAD_DIM
    zero = jnp.zeros_like(x)
    return jnp.where(first, x, zero), jnp.where(first, zero, x)


def _deinterleave(src, dst, seq, dil, dtype):
    length = seq // dil
    for r in range(dil):
        for c in range(length // BLK):
            rows = pl.ds(r + c * BLK * dil, BLK, stride=dil)
            out = slice(r * length + c * BLK, r * length + (c + 1) * BLK)
            if len(src.shape) == 2:
                dst[out, :] = src[rows, :].astype(dtype)
            else:
                for hf in range(2):
                    dst[out, _lane_half(hf)] = src.at[hf][rows, :].astype(dtype)


def _interleave(src, dst, seq, dil, accumulate):
    length = seq // dil
    for r in range(dil):
        for c in range(length // BLK):
            rows = pl.ds(r + c * BLK * dil, BLK, stride=dil)
            inp = slice(r * length + c * BLK, r * length + (c + 1) * BLK)
            if len(dst.shape) == 2:
                dst[rows, :] = dst[rows, :] + src[inp, :] if accumulate else src[inp, :]
            else:
                for hf in range(2):
                    val = src[inp, _lane_half(hf)]
                    half = dst.at[hf]
                    half[rows, :] = half[rows, :] + val if accumulate else val


def _split_halves(src, dst, seq):
    def step(i, carry):
        for hf in range(2):
            dst[hf, _rows_of(i), :] = src[_rows_of(i), _lane_half(hf)].astype(F32)
        return carry

    lax.fori_loop(0, seq // BLK, step, 0)


def _band_attn_fwd_fused(h, slopes, bsz, seq, gather=None):
    width = h.shape[1]
    cb = width // 256
    k_off, v_off = MIX_W // 256, 2 * MIX_W // 256
    nb = seq // BLK

    ng = 0 if gather is None else len(gather)

    def body(*refs):
        sl_ref, q_ref, k_ref, v_ref = refs[:4]
        mix_ref, lse_ref = refs[4 + ng:6 + ng]
        qf, kf, vf, qd, kd, vd, od, ld, o1, o2, o3, l1, l2, l3 = refs[6 + 2 * ng:20 + 2 * ng]
        if ng:
            start, finish = _gather_protocol(refs[4:4 + ng], refs[6 + ng:6 + 2 * ng], *refs[20 + 2 * ng:])
            pl.when((pl.program_id(0) == 0) & (pl.program_id(1) == 0))(start)

        def run(dil, qs, ks, vs, o_dst, l_dst):
            nblk = seq // dil // BLK

            def block(j, carry):
                rows, prows = _rows_of(j), _rows_of(jnp.maximum(j - 1, 0))
                valid, dist = _block_mask((j % nblk) != 0, dil)
                lses = []
                for pr in range(HEAD_GROUP // 2):
                    lanes = _lane_half(pr)
                    q_ab = _split_pair(qs[rows, lanes])
                    k2 = jnp.concatenate([ks[prows, lanes], ks[rows, lanes]], axis=0)
                    v_ab = _split_pair(jnp.concatenate([vs[prows, lanes], vs[rows, lanes]], axis=0))
                    out = None
                    for ab in range(2):
                        hh = 2 * pr + ab
                        s = _dot_nt(q_ab[ab], k2) * SCALE - sl_ref[hh:hh + 1, 0:1] * dist
                        s = jnp.where(valid, s, NEG)
                        m = jnp.max(s, axis=-1, keepdims=True)
                        p = jnp.exp(s - m)
                        l = jnp.sum(p, axis=-1, keepdims=True)
                        term = _dot_nn(p.astype(BF16), v_ab[ab]) / l
                        out = term if out is None else out + term
                        lses.append(m + jnp.log(l))
                    o_dst[rows, lanes] = out
                l_dst[rows, :] = _spread_stats(lses)
                return carry

            lax.fori_loop(0, nb, block, 0, unroll=2)

        run(1, q_ref, k_ref, v_ref, o1, l1)
        _split_halves(q_ref, qf, seq)
        _split_halves(k_ref, kf, seq)
        _split_halves(v_ref, vf, seq)
        for dil, o_tok, l_tok in ((4, o2, l2), (16, o3, l3)):
            _deinterleave(qf, qd, seq, dil, BF16)
            _deinterleave(kf, kd, seq, dil, BF16)
            _deinterleave(vf, vd, seq, dil, BF16)
            run(dil, qd, kd, vd, od, ld)
            _interleave(od, o_tok, seq, dil, False)
            _interleave(ld, l_tok, seq, dil, False)

        def merge(i, carry):
            rows = _rows_of(i)
            ls = [l1[rows, :], l2[rows, :], l3[rows, :]]
            m = jnp.maximum(jnp.maximum(ls[0], ls[1]), ls[2])
            tot = m + jnp.log(jnp.exp(ls[0] - m) + jnp.exp(ls[1] - m) + jnp.exp(ls[2] - m))
            ws = [jnp.exp(x - tot) for x in ls]
            lse_ref[rows, :] = tot
            for hh in range(HEAD_GROUP):
                sl = slice(hh * HEAD_DIM, (hh + 1) * HEAD_DIM)
                st = slice(hh * STAT_LANES, hh * STAT_LANES + 1)
                hf, hl = hh // 2, slice((hh % 2) * HEAD_DIM, (hh % 2 + 1) * HEAD_DIM)
                acc = ws[0][:, st] * o1[rows, sl] + ws[1][:, st] * o2[hf, rows, hl] + ws[2][:, st] * o3[hf, rows, hl]
                mix_ref[rows, sl] = acc.astype(BF16)
            return carry

        lax.fori_loop(0, nb, merge, 0)
        if ng:
            pl.when((pl.program_id(0) == bsz - 1) & (pl.program_id(1) == N_GROUPS - 1))(finish)

    def hspec(off):
        return pl.BlockSpec((seq, 256), lambda b, g: (b, off + g))

    big = lambda dt: pltpu.VMEM((seq, 256), dt)
    halves = lambda: pltpu.VMEM((2, seq, 128), F32)
    stat = lambda: pltpu.VMEM((seq, 128), F32)
    outs = pl.pallas_call(
        body, name="band_attn_fwd_gather" if ng else "band_attn_fwd", grid=(bsz, N_GROUPS),
        in_specs=[pl.BlockSpec((None, 8, 128), lambda b, g: (g, 0, 0)), hspec(0), hspec(k_off), hspec(v_off)] + [_ANY] * ng,
        out_specs=[pl.BlockSpec((seq, 256), lambda b, g: (b, g)), pl.BlockSpec((seq, 128), lambda b, g: (b, g))] + [_ANY] * ng,
        out_shape=[_sds((bsz * seq, MIX_W), BF16), _sds((bsz * seq, STAT_W), F32)] + (_gather_shapes(gather) if ng else []),
        scratch_shapes=[halves(), halves(), halves(), big(BF16), big(BF16), big(BF16), big(F32), stat(),
                        big(F32), halves(), halves(), stat(), stat(), stat()] + (_gather_sems(ng) if ng else []),
        compiler_params=_params(("arbitrary", "arbitrary")))(slopes, h, h, h, *(gather or []))
    return outs[0], outs[1], list(outs[2:])


def _band_attn_bwd_fused(h, dcat, mix, lse, slopes, bsz, seq, exchange=None):
    width = h.shape[1]
    k_off, v_off = MIX_W // 256, 2 * MIX_W // 256
    nb = seq // BLK

    ne = 0 if exchange is None else len(exchange)

    def body(*refs):
        sl_ref, q_ref, k_ref, v_ref, do_ref, o_ref, lse_ref = refs[:7]
        dq_ref, dk_ref, dv_ref = refs[7 + ne:10 + ne]
        qf, kf, vf, dof, ddt, qd, kd, vd, dod, lsd, ddd, gq, gk, gv, aq, ak, av = refs[10 + 2 * ne:27 + 2 * ne]
        if ne:
            start, finish = _exchange_protocol(refs[7:7 + ne], refs[10 + ne:10 + 2 * ne], *refs[27 + 2 * ne:])
            pl.when((pl.program_id(0) == 0) & (pl.program_id(1) == 0))(start)

        def delta(i, carry):
            rows = _rows_of(i)
            prod = do_ref[rows, :].astype(F32) * o_ref[rows, :].astype(F32)
            ddt[rows, :] = _spread_stats(
                [jnp.sum(prod[:, hh * HEAD_DIM:(hh + 1) * HEAD_DIM], axis=-1, keepdims=True) for hh in range(HEAD_GROUP)])
            return carry

        lax.fori_loop(0, nb, delta, 0)

        def zero(i, carry):
            rows = _rows_of(i)
            for ref in (gk, gv):
                ref[rows, :] = jnp.zeros((BLK, 256), F32)
            return carry

        def run(dil, qs, ks, vs, dos, lss, dds):
            nblk = seq // dil // BLK
            lax.fori_loop(0, nb, zero, 0)

            def block(j, carry):
                rows, prows = _rows_of(j), _rows_of(jnp.maximum(j - 1, 0))
                valid, dist = _block_mask((j % nblk) != 0, dil)
                for pr in range(HEAD_GROUP // 2):
                    lanes = _lane_half(pr)
                    q_ab = _split_pair(qs[rows, lanes])
                    do_ab = _split_pair(dos[rows, lanes])
                    k2 = jnp.concatenate([ks[prows, lanes], ks[rows, lanes]], axis=0)
                    v2 = jnp.concatenate([vs[prows, lanes], vs[rows, lanes]], axis=0)
                    k_ab = _split_pair(k2)
                    dq, dk2, dv2 = None, None, None
                    for ab in range(2):
                        hh = 2 * pr + ab
                        st = slice(hh * STAT_LANES, hh * STAT_LANES + 1)
                        s = _dot_nt(q_ab[ab], k2) * SCALE - sl_ref[hh:hh + 1, 0:1] * dist
                        s = jnp.where(valid, s, NEG)
                        p = jnp.exp(s - lss[rows, st])
                        dp = _dot_nt(do_ab[ab], v2)
                        ds = (p * (dp - dds[rows, st]) * SCALE).astype(BF16)
                        terms = (_dot_nn(ds, k_ab[ab]), _dot_tn(ds, q_ab[ab]), _dot_tn(p.astype(BF16), do_ab[ab]))
                        dq, dk2, dv2 = terms if dq is None else (dq + terms[0], dk2 + terms[1], dv2 + terms[2])
                    gq[rows, lanes] = dq
                    gk[prows, lanes] += dk2[:BLK]
                    gv[prows, lanes] += dv2[:BLK]
                    gk[rows, lanes] += dk2[BLK:]
                    gv[rows, lanes] += dv2[BLK:]
                return carry

            lax.fori_loop(0, nb, block, 0, unroll=2)

        run(1, q_ref, k_ref, v_ref, do_ref, lse_ref, ddt)

        for src, dst in ((gq, aq), (gk, ak), (gv, av), (q_ref, qf), (k_ref, kf), (v_ref, vf), (do_ref, dof)):
            _split_halves(src, dst, seq)
        for dil in (4, 16):
            for src, dst in ((qf, qd), (kf, kd), (vf, vd), (dof, dod)):
                _deinterleave(src, dst, seq, dil, BF16)
            _deinterleave(lse_ref, lsd, seq, dil, F32)
            _deinterleave(ddt, ddd, seq, dil, F32)
            run(dil, qd, kd, vd, dod, lsd, ddd)
            for src, dst in ((gq, aq), (gk, ak), (gv, av)):
                _interleave(src, dst, seq, dil, True)

        def write(i, carry):
            rows = _rows_of(i)
            for src, dst in ((aq, dq_ref), (ak, dk_ref), (av, dv_ref)):
                for hf in range(2):
                    dst[rows, _lane_half(hf)] = src[hf, rows, :].astype(BF16)
            return carry

        lax.fori_loop(0, nb, write, 0)
        if ne:
            pl.when((pl.program_id(0) == bsz - 1) & (pl.program_id(1) == N_GROUPS - 1))(finish)

    def hspec(off):
        return pl.BlockSpec((seq, 256), lambda b, g: (b, off + g))

    io = pl.BlockSpec((seq, 256), lambda b, g: (b, g))
    big = lambda dt: pltpu.VMEM((seq, 256), dt)
    halves = lambda: pltpu.VMEM((2, seq, 128), F32)
    stat = lambda: pltpu.VMEM((seq, 128), F32)
    outs = pl.pallas_call(
        body, name="band_attn_bwd_exchange" if ne else "band_attn_bwd", grid=(bsz, N_GROUPS),
        in_specs=[pl.BlockSpec((None, 8, 128), lambda b, g: (g, 0, 0)), hspec(0), hspec(k_off), hspec(v_off), io, io,
                  pl.BlockSpec((seq, 128), lambda b, g: (b, g))] + [_ANY] * ne,
        out_specs=[io, io, io] + [_ANY] * ne,
        out_shape=[_sds((bsz * seq, MIX_W), BF16)] * 3 + [_sds(s.shape, s.dtype) for s in (exchange or [])],
        scratch_shapes=[halves(), halves(), halves(), halves(), stat(),
                        big(BF16), big(BF16), big(BF16), big(BF16), stat(), stat(),
                        big(F32), big(F32), big(F32), halves(), halves(), halves()] + (_exchange_sems(ne) if ne else []),
        compiler_params=_params(("arbitrary", "arbitrary")))(slopes, h, h, h, dcat, mix, lse, *(exchange or []))
    return list(outs[:3]), list(outs[3:])


def _mem_attn_fwd(h, mkv, bsz, seq, q_col, tq=512):
    nq = seq // tq

    def body(q_ref, kv_ref, o_ref):
        q = q_ref[...]
        kv = kv_ref[...]
        for hh in range(4):
            sl = slice(hh * HEAD_DIM, (hh + 1) * HEAD_DIM)
            s = _dot_nt(q[:, sl], kv[:, sl]) * SCALE
            m = jnp.max(s, axis=-1, keepdims=True)
            p = jnp.exp(s - m)
            l = jnp.sum(p, axis=-1, keepdims=True)
            acc = _dot_nn(p.astype(BF16), kv[:, MEM_W + hh * HEAD_DIM:MEM_W + (hh + 1) * HEAD_DIM])
            o_ref[:, sl] = (acc / l).astype(BF16)

    return pl.pallas_call(
        body, name="mem_attn_fwd", grid=(bsz, nq),
        in_specs=[pl.BlockSpec((tq, MEM_W), lambda b, i: (b * nq + i, q_col)),
                  pl.BlockSpec((N_MEM, 2 * MEM_W), lambda b, i: (b, 0))],
        out_specs=pl.BlockSpec((tq, MEM_W), lambda b, i: (b * nq + i, 0)),
        out_shape=_sds((bsz * seq, MEM_W), BF16), compiler_params=_params(("parallel", "parallel")))(h, mkv)


def _mem_attn_bwd(h, mkv, dcat, bsz, seq, q_col, tq=512):
    nq = seq // tq
    do_col = MIX_W // MEM_W

    def body(q_ref, kv_ref, do_ref, dq_ref, dkv_ref):
        @pl.when(pl.program_id(1) == 0)
        def _():
            dkv_ref[...] = jnp.zeros_like(dkv_ref)

        q = q_ref[...]
        kv = kv_ref[...]
        do = do_ref[...]
        for hh in range(4):
            sl = slice(hh * HEAD_DIM, (hh + 1) * HEAD_DIM)
            vsl = slice(MEM_W + hh * HEAD_DIM, MEM_W + (hh + 1) * HEAD_DIM)
            s = _dot_nt(q[:, sl], kv[:, sl]) * SCALE
            m = jnp.max(s, axis=-1, keepdims=True)
            e = jnp.exp(s - m)
            p = e / jnp.sum(e, axis=-1, keepdims=True)
            dp = _dot_nt(do[:, sl], kv[:, vsl])
            dd = jnp.sum(p * dp, axis=-1, keepdims=True)
            ds = (p * (dp - dd) * SCALE).astype(BF16)
            dq_ref[:, sl] = _dot_nn(ds, kv[:, sl]).astype(BF16)
            dkv_ref[:, sl] += _dot_tn(ds, q[:, sl])
            dkv_ref[:, vsl] += _dot_tn(p.astype(BF16), do[:, sl])

    return pl.pallas_call(
        body, name="mem_attn_bwd", grid=(bsz, nq),
        in_specs=[pl.BlockSpec((tq, MEM_W), lambda b, i: (b * nq + i, q_col)),
                  pl.BlockSpec((N_MEM, 2 * MEM_W), lambda b, i: (b, 0)),
                  pl.BlockSpec((tq, MEM_W), lambda b, i: (b * nq + i, do_col))],
        out_specs=[pl.BlockSpec((tq, MEM_W), lambda b, i: (b * nq + i, 0)),
                   pl.BlockSpec((N_MEM, 2 * MEM_W), lambda b, i: (b, 0))],
        out_shape=[_sds((bsz * seq, MEM_W), BF16), _sds((bsz * N_MEM, 2 * MEM_W), F32)],
        compiler_params=_params(("parallel", "arbitrary")))(h, mkv, dcat)


_GELU_C = math.sqrt(2.0 / math.pi)
_GELU_A = 0.044715


def _gelu(x):
    return 0.5 * x * (1.0 + jnp.tanh(_GELU_C * (x + _GELU_A * x * x * x)))


def _gelu_grad(x):
    th = jnp.tanh(_GELU_C * (x + _GELU_A * x * x * x))
    return 0.5 * (1.0 + th) + 0.5 * x * (1.0 - th * th) * (_GELU_C * (1.0 + 3.0 * _GELU_A * x * x))


def _tril_mask(lower):
    ri = lax.broadcasted_iota(jnp.int32, (BLK, BLK), 0)
    ci = lax.broadcasted_iota(jnp.int32, (BLK, BLK), 1)
    return (ri >= ci) if lower else (ci >= ri)


def _sgu_fwd(h, ws, bs_t, ln_g, ln_b, tm=512):
    t = h.shape[0]

    def body(u_ref, v_ref, ws_ref, bs_ref, g_ref, b_ref, o_ref):
        ug = _gelu(u_ref[...].astype(F32))
        vhat, _ = _ln_hat(_gelu(v_ref[...].astype(F32)))
        vn = (vhat * g_ref[...] + b_ref[...]).astype(BF16)
        mask = _tril_mask(True)
        for g in range(N_HEADS):
            sl = slice(g * HEAD_DIM, (g + 1) * HEAD_DIM)
            w = jnp.where(mask, ws_ref[g], 0).astype(BF16)
            for c in range(tm // BLK):
                rs = slice(c * BLK, (c + 1) * BLK)
                mixed = _dot_nn(w, vn[rs, sl]) + bs_ref[:, g:g + 1]
                o_ref[rs, sl] = (ug[rs, sl] * mixed).astype(BF16)

    return pl.pallas_call(
        body, name="sgu_fwd", grid=(t // tm,),
        in_specs=[_rows(tm, MIX_W, 0), _rows(tm, MIX_W, 1), _whole(ws.shape), _whole(bs_t.shape), _whole(ln_g.shape), _whole(ln_b.shape)],
        out_specs=_rows(tm, MIX_W), out_shape=_sds((t, MIX_W), BF16),
        compiler_params=_params(("parallel",)))(h, h, ws, bs_t, ln_g, ln_b)


def _sgu_bwd(h, dcat, ws, ws_t, bs_t, ln_g, ln_b, tm=512):
    t = h.shape[0]

    def body(u_ref, v_ref, do_ref, ws_ref, wst_ref, bs_ref, g_ref, b_ref, dh_ref, dws_ref, dbs_ref, dg_ref, db_ref, dvn_ref):
        @pl.when(pl.program_id(0) == 0)
        def _():
            dws_ref[...] = jnp.zeros_like(dws_ref)
            dbs_ref[...] = jnp.zeros_like(dbs_ref)
            dg_ref[...] = jnp.zeros_like(dg_ref)
            db_ref[...] = jnp.zeros_like(db_ref)

        u = u_ref[...].astype(F32)
        v = v_ref[...].astype(F32)
        do = do_ref[...].astype(F32)
        ug = _gelu(u)
        vhat, rstd = _ln_hat(_gelu(v))
        vn = (vhat * g_ref[...] + b_ref[...]).astype(BF16)
        dmixed_f = do * ug
        dmixed = dmixed_f.astype(BF16)
        low, upp = _tril_mask(True), _tril_mask(False)
        for g in range(N_HEADS):
            sl = slice(g * HEAD_DIM, (g + 1) * HEAD_DIM)
            w = jnp.where(low, ws_ref[g], 0).astype(BF16)
            wt = jnp.where(upp, wst_ref[g], 0).astype(BF16)
            dws_acc = None
            dbs_acc = None
            for c in range(tm // BLK):
                rs = slice(c * BLK, (c + 1) * BLK)
                mixed = _dot_nn(w, vn[rs, sl]) + bs_ref[:, g:g + 1]
                dh_ref[rs, sl] = (do[rs, sl] * mixed * _gelu_grad(u[rs, sl])).astype(BF16)
                dm = dmixed[rs, sl]
                term = _dot_nt(dm, vn[rs, sl])
                dws_acc = term if dws_acc is None else dws_acc + term
                rsum = jnp.sum(dmixed_f[rs, sl], axis=-1, keepdims=True)
                dbs_acc = rsum if dbs_acc is None else dbs_acc + rsum
                dvn_ref[rs, sl] = _dot_nn(wt, dm)
            dws_ref[g] += jnp.where(low, dws_acc, 0.0)
            dbs_ref[:, g:g + 1] += dbs_acc
        dvn = dvn_ref[...]
        dg_ref[...] += jnp.sum(dvn * vhat, axis=0, keepdims=True)
        db_ref[...] += jnp.sum(dvn, axis=0, keepdims=True)
        dxh = dvn * g_ref[...]
        m1 = jnp.mean(dxh, axis=-1, keepdims=True)
        m2 = jnp.mean(dxh * vhat, axis=-1, keepdims=True)
        dvg = rstd * (dxh - m1 - vhat * m2)
        dh_ref[:, MIX_W:] = (dvg * _gelu_grad(v)).astype(BF16)

    return pl.pallas_call(
        body, name="sgu_bwd", grid=(t // tm,),
        in_specs=[_rows(tm, MIX_W, 0), _rows(tm, MIX_W, 1), _rows(tm, MIX_W, 0), _whole(ws.shape), _whole(ws_t.shape),
                  _whole(bs_t.shape), _whole(ln_g.shape), _whole(ln_b.shape)],
        out_specs=[_rows(tm, 2 * MIX_W), _whole(ws.shape), _whole(bs_t.shape), _whole((1, MIX_W)), _whole((1, MIX_W))],
        out_shape=[_sds((t, 2 * MIX_W), BF16), _sds(ws.shape, F32), _sds(bs_t.shape, F32), _sds((1, MIX_W), F32), _sds((1, MIX_W), F32)],
        scratch_shapes=[pltpu.VMEM((tm, MIX_W), F32)],
        compiler_params=_params(("arbitrary",)))(h, h, dcat, ws, ws_t, bs_t, ln_g, ln_b)


def _row_tile(rows, cols, itemsize=4, limit=2 ** 20):
    best = rows
    for cand in (4096, 2048, 1024, 512, 256, 128, 64, 32, 16):
        if rows % cand == 0 and rows > cand:
            best = cand
            if cand * cols * itemsize <= limit:
                break
    return best


def _adamw(w, m, v, grad=None, parts=None, first_parts=None):
    rows, cols = w.shape
    rows0 = 0 if first_parts is None else first_parts.shape[1]
    tr = _row_tile(rows0 if rows0 else rows, cols)
    n0 = rows0 // tr

    def chip_sum(ref):
        acc = ref[0].astype(F32)
        for k in range(1, 4):
            acc = acc + ref[k].astype(F32)
        return acc

    def body(*refs):
        w_ref, m_ref, v_ref = refs[:3]
        go_ref, d_ref, nm_ref, nv_ref = refs[-4:]
        if parts is None:
            gv = refs[3][...]
        elif first_parts is None:
            gv = chip_sum(refs[3])
        else:
            gv = jnp.where(pl.program_id(0) < n0, chip_sum(refs[3]), chip_sum(refs[4]))
        nm = ADAM_B1 * m_ref[...] + (1.0 - ADAM_B1) * gv
        nv = ADAM_B2 * v_ref[...] + (1.0 - ADAM_B2) * (gv * gv)
        m_hat = nm / (1.0 - ADAM_B1 ** ADAM_STEP)
        v_hat = nv / (1.0 - ADAM_B2 ** ADAM_STEP)
        go_ref[...] = gv
        d_ref[...] = -ADAM_LR * (m_hat / (jnp.sqrt(v_hat) + ADAM_EPS) + ADAM_WD * w_ref[...])
        nm_ref[...] = nm
        nv_ref[...] = nv

    spec = _rows(tr, cols)
    if parts is None:
        g_specs, g_args = [spec], [grad]
    elif first_parts is None:
        g_specs, g_args = [pl.BlockSpec((4, tr, cols), lambda i: (0, i, 0))], [parts]
    else:
        g_specs = [pl.BlockSpec((4, tr, cols), lambda i: (0, jnp.minimum(i, n0 - 1), 0)),
                   pl.BlockSpec((4, tr, cols), lambda i: (0, jnp.maximum(i - n0, 0), 0))]
        g_args = [first_parts, parts]
    return pl.pallas_call(
        body, name="adamw" if parts is None else "adamw_sum_chips", grid=(rows // tr,), in_specs=[spec] * 3 + g_specs,
        out_specs=[spec] * 4, out_shape=[_sds(w.shape, F32)] * 4,
        compiler_params=_params(("parallel",)))(w, m, v, *g_args)


_ANY = pl.BlockSpec(memory_space=pl.ANY)
_MESH = pl.DeviceIdType.MESH


def _all_gather(name, blocks):
    nt = len(blocks)

    def body(*refs):
        start, finish = _gather_protocol(refs[:nt], refs[nt:2 * nt], *refs[2 * nt:])
        start()
        finish()

    return pl.pallas_call(
        body, name=name, out_shape=_gather_shapes(blocks), in_specs=[_ANY] * nt, out_specs=[_ANY] * nt,
        scratch_shapes=_gather_sems(nt))(*blocks)


def _gather_shapes(blocks):
    return [_sds((N_DEV,) + b.shape, b.dtype) for b in blocks]


def _gather_sems(nt):
    return [pltpu.SemaphoreType.DMA((nt, 7)), pltpu.SemaphoreType.DMA((nt, 7)), pltpu.SemaphoreType.DMA((nt,))]


def _gather_protocol(x_refs, out_refs, send_sems, recv_sems, local_sems):
    nt = len(x_refs)
    x, y, c = lax.axis_index("x"), lax.axis_index("y"), lax.axis_index("c")
    me, sibling = (x, y, c), (x, y, 1 - c)
    chips = [(1 - x, y), (x, 1 - y), (1 - x, 1 - y)]

    def slot(t, px, py, pc):
        return out_refs[t].at[4 * px + 2 * py + pc]

    def copy(t, k, blk, to, src=None):
        return pltpu.make_async_remote_copy(
            src_ref=slot(t, *blk) if src is None else src, dst_ref=slot(t, *blk),
            send_sem=send_sems.at[t, k], recv_sem=recv_sems.at[t, k], device_id=to, device_id_type=_MESH)

    def own_copies():
        mine = [pltpu.make_async_copy(x_refs[t], slot(t, *me), local_sems.at[t]) for t in range(nt)]
        first = []
        for t in range(nt):
            first.append(copy(t, 0, me, sibling, src=x_refs[t]))
            first += [copy(t, 1 + j, me, (*chip, c), src=x_refs[t]) for j, chip in enumerate(chips)]
        return mine, first

    def start():
        mine, first = own_copies()
        for cp in mine + first:
            cp.start()

    def finish():
        mine, first = own_copies()
        passed = []
        for j, chip in enumerate(chips):
            for t in range(nt):
                copy(t, 1 + j, (*chip, c), me).wait_recv()
                fwd = copy(t, 4 + j, (*chip, c), sibling)
                fwd.start()
                passed.append(fwd)
        for t in range(nt):
            copy(t, 0, sibling, me).wait_recv()
        for j, chip in enumerate(chips):
            for t in range(nt):
                copy(t, 4 + j, (*chip, 1 - c), me).wait_recv()
        for cp in first + passed:
            cp.wait_send()
        for cp in mine:
            cp.wait()

    return start, finish


def _swap_with_sibling(packed):
    nt = len(packed)

    def body(*refs):
        p_refs, got_refs = refs[:nt], refs[nt:2 * nt]
        send_sems, recv_sems = refs[2 * nt:]
        x, y, c = lax.axis_index("x"), lax.axis_index("y"), lax.axis_index("c")
        copies = [
            pltpu.make_async_remote_copy(
                src_ref=p_refs[t].at[1 - c], dst_ref=got_refs[t], send_sem=send_sems.at[t], recv_sem=recv_sems.at[t],
                device_id=(x, y, 1 - c), device_id_type=_MESH)
            for t in range(nt)]
        for cp in copies:
            cp.start()
        for cp in copies:
            cp.wait_recv()
        for cp in copies:
            cp.wait_send()

    return pl.pallas_call(
        body, name="grad_swap_sibling", out_shape=[_sds(p.shape[1:], p.dtype) for p in packed], in_specs=[_ANY] * nt,
        out_specs=[_ANY] * nt,
        scratch_shapes=[pltpu.SemaphoreType.DMA((nt,)), pltpu.SemaphoreType.DMA((nt,))])(*packed)


def _chip_sum(packed, got):
    _, nchip, rows, cols = packed.shape
    tr = _row_tile(rows, cols, 2)
    core = lax.axis_index("c").astype(jnp.int32).reshape(1)

    def body(c_ref, p_ref, g_ref, o_ref):
        o_ref[...] = (p_ref[...].astype(F32) + g_ref[...].astype(F32)).astype(o_ref.dtype)

    grid_spec = pltpu.PrefetchScalarGridSpec(
        num_scalar_prefetch=1, grid=(nchip, rows // tr),
        in_specs=[pl.BlockSpec((None, None, tr, cols), lambda k, i, c: (c[0], k, i, 0)),
                  pl.BlockSpec((None, tr, cols), lambda k, i, c: (k, i, 0))],
        out_specs=pl.BlockSpec((None, tr, cols), lambda k, i, c: (k, i, 0)))
    return pl.pallas_call(
        body, name="grad_chip_sum", grid_spec=grid_spec, out_shape=_sds(got.shape, got.dtype),
        compiler_params=_params(("parallel", "parallel")))(core, packed, got)


def _exchange_chips(chip_sums):
    nt = len(chip_sums)

    def body(*refs):
        start, finish = _exchange_protocol(refs[:nt], refs[nt:2 * nt], *refs[2 * nt:])
        start()
        finish()

    return pl.pallas_call(
        body, name="grad_exchange_chips", out_shape=[_sds(s.shape, s.dtype) for s in chip_sums], in_specs=[_ANY] * nt,
        out_specs=[_ANY] * nt, scratch_shapes=_exchange_sems(nt))(*chip_sums)


def _exchange_sems(nt):
    return [pltpu.SemaphoreType.DMA((nt, 3)), pltpu.SemaphoreType.DMA((nt, 3)), pltpu.SemaphoreType.DMA((nt,))]


def _exchange_protocol(s_refs, got_refs, send_sems, recv_sems, local_sems):
    nt = len(s_refs)
    x, y, c = lax.axis_index("x"), lax.axis_index("y"), lax.axis_index("c")
    my_chip = 2 * x + y
    chips = [(1 - x, y), (x, 1 - y), (1 - x, 1 - y)]

    def copy(t, j, src_chip, dst_chip):
        px, py = chips[j]
        return pltpu.make_async_remote_copy(
            src_ref=s_refs[t].at[src_chip], dst_ref=got_refs[t].at[dst_chip], send_sem=send_sems.at[t, j],
            recv_sem=recv_sems.at[t, j], device_id=(px, py, c), device_id_type=_MESH)

    def own_copies():
        mine = [pltpu.make_async_copy(s_refs[t].at[my_chip], got_refs[t].at[my_chip], local_sems.at[t]) for t in range(nt)]
        sends = [copy(t, j, 2 * px + py, my_chip) for t in range(nt) for j, (px, py) in enumerate(chips)]
        return mine, sends

    def start():
        mine, sends = own_copies()
        for cp in mine + sends:
            cp.start()

    def finish():
        mine, sends = own_copies()
        for j, (px, py) in enumerate(chips):
            for t in range(nt):
                copy(t, j, my_chip, 2 * px + py).wait_recv()
        for cp in sends:
            cp.wait_send()
        for cp in mine:
            cp.wait()

    return start, finish


def _sum_chips(got):
    _, rows, cols = got.shape
    tr = _row_tile(rows, cols)

    def body(g_ref, o_ref):
        acc = g_ref[0].astype(F32)
        for k in range(1, 4):
            acc = acc + g_ref[k].astype(F32)
        o_ref[...] = acc

    return pl.pallas_call(
        body, name="grad_sum_chips", grid=(rows // tr,), in_specs=[pl.BlockSpec((4, tr, cols), lambda i: (0, i, 0))],
        out_specs=pl.BlockSpec((tr, cols), lambda i: (i, 0)), out_shape=_sds((rows, cols), F32),
        compiler_params=_params(("parallel",)))(got)


_COL_SHARDED = ("a_w_in", "b_w_in", "w_gate", "w_up")
_ROW_SHARDED = ("w_mem_kv", "w_out", "w_down")
_BIG = ("a_w_in", "b_w_in", "w_mem_kv", "w_out", "w_gate", "w_up", "w_down")
_SGU_LN = ("sgu_ln_g", "sgu_ln_b")
_LN4 = ("ln_mix_g", "ln_mix_b", "ln_ffn_g", "ln_ffn_b")
_REPLICATED = ("sgu_w_s", "sgu_b_s") + _LN4


def _unshard(name, gathered):
    if name in _COL_SHARDED or name in _SGU_LN:
        moved = jnp.moveaxis(gathered, 0, -2)
        return moved.reshape(moved.shape[:-2] + (moved.shape[-2] * moved.shape[-1],))
    moved = jnp.moveaxis(gathered, 0, 1)
    return moved.reshape((moved.shape[0], moved.shape[1] * moved.shape[2]) + moved.shape[3:])


def _by_shard(name, full):
    if name in _COL_SHARDED or name in _SGU_LN:
        split = full.reshape(full.shape[:-1] + (N_DEV, full.shape[-1] // N_DEV))
        return jnp.moveaxis(split, -2, 0)
    split = full.reshape((full.shape[0], N_DEV, full.shape[1] // N_DEV) + full.shape[2:])
    return jnp.moveaxis(split, 1, 0)


_FIRST = tuple(n for n in _BIG if n != "b_w_in")


def _first_slice(name, a):
    return None if name == "b_w_in" else a[:1]


def _rest_slice(name, a):
    return a if name == "b_w_in" else a[1:]


def _gather_first(shards):
    blocks = [shards[n][:1].astype(BF16) for n in _FIRST] + [shards[n] for n in _SGU_LN]
    gathered = _all_gather("first_layer_all_gather", blocks)
    names = _FIRST + _SGU_LN
    full = {n: _unshard(n, g) for n, g in zip(names, gathered)}
    return {n: full[n] for n in _FIRST}, {n: full[n] for n in _SGU_LN}


def _rest_blocks(shards):
    return [_rest_slice(n, shards[n]).astype(BF16) for n in _BIG]


def _two_level(by_dest):
    shp = by_dest.shape[1:]
    split = by_dest.astype(BF16).reshape((4, 2) + shp).swapaxes(0, 1)
    return split.reshape(2, 4, int(np.prod(shp[:-1])), shp[-1])


def _chip_sums_of_rest(grads):
    packed = [_two_level(_by_shard(n, grads[n])) for n in _BIG]
    got = _swap_with_sibling(packed)
    return [_chip_sum(p, g) for p, g in zip(packed, got)]


def _reduce_last(grads):
    sharded = _FIRST + _SGU_LN
    packed = [_two_level(_by_shard(n, grads[n][:1] if n in _FIRST else grads[n])) for n in sharded]
    ln4 = jnp.stack([grads[n] for n in _LN4])
    rep = [grads["sgu_w_s"].reshape(N_DEV, -1, BLK), grads["sgu_b_s"].reshape(N_DEV, -1, BLK), ln4.reshape(N_DEV, -1, D_MODEL)]
    packed += [_two_level(r) for r in rep]
    got = _swap_with_sibling(packed)
    sums = [_chip_sum(p, g) for p, g in zip(packed, got)]
    parts = _exchange_chips(sums)
    out_parts = dict(zip(sharded, parts[:len(sharded)]))
    mine = [_sum_chips(p) for p in parts[len(sharded):]]
    w_s, b_s, ln_all = _all_gather("replicated_grads_all_gather", mine)
    ln_all = ln_all.reshape(ln4.shape)
    rep_grads = {"sgu_w_s": w_s.reshape(grads["sgu_w_s"].shape), "sgu_b_s": b_s.reshape(grads["sgu_b_s"].shape)}
    rep_grads.update({n: ln_all[i] for i, n in enumerate(_LN4)})
    return out_parts, rep_grads


def _as_2d(a):
    if a.ndim == 1:
        return a.reshape(1, -1)
    return a.reshape(-1, a.shape[-1])


def kernel(x, mem, a_w_in, b_w_in, sgu_ln_g, sgu_ln_b, sgu_w_s, sgu_b_s, w_mem_kv, w_out, ln_mix_g, ln_mix_b, w_gate, w_up, w_down, ln_ffn_g, ln_ffn_b, loss_target, m_a_w_in, m_b_w_in, m_sgu_ln_g, m_sgu_ln_b, m_sgu_w_s, m_sgu_b_s, m_w_mem_kv, m_w_out, m_ln_mix_g, m_ln_mix_b, m_w_gate, m_w_up, m_w_down, m_ln_ffn_g, m_ln_ffn_b, v_a_w_in, v_b_w_in, v_sgu_ln_g, v_sgu_ln_b, v_sgu_w_s, v_sgu_b_s, v_w_mem_kv, v_w_out, v_ln_mix_g, v_ln_mix_b, v_w_gate, v_w_up, v_w_down, v_ln_ffn_g, v_ln_ffn_b):
    names = ("a_w_in", "b_w_in", "sgu_ln_g", "sgu_ln_b", "sgu_w_s", "sgu_b_s", "w_mem_kv", "w_out", "ln_mix_g", "ln_mix_b",
             "w_gate", "w_up", "w_down", "ln_ffn_g", "ln_ffn_b")
    weights = dict(zip(names, (a_w_in, b_w_in, sgu_ln_g, sgu_ln_b, sgu_w_s, sgu_b_s, w_mem_kv, w_out, ln_mix_g, ln_mix_b,
                               w_gate, w_up, w_down, ln_ffn_g, ln_ffn_b)))
    mom_m = dict(zip(names, (m_a_w_in, m_b_w_in, m_sgu_ln_g, m_sgu_ln_b, m_sgu_w_s, m_sgu_b_s, m_w_mem_kv, m_w_out, m_ln_mix_g,
                             m_ln_mix_b, m_w_gate, m_w_up, m_w_down, m_ln_ffn_g, m_ln_ffn_b)))
    mom_v = dict(zip(names, (v_a_w_in, v_b_w_in, v_sgu_ln_g, v_sgu_ln_b, v_sgu_w_s, v_sgu_b_s, v_w_mem_kv, v_w_out, v_ln_mix_g,
                             v_ln_mix_b, v_w_gate, v_w_up, v_w_down, v_ln_ffn_g, v_ln_ffn_b)))
    full_first, sgu_ln = _gather_first(weights)
    loss_part, grad_x, local, rest_parts = _local_step(
        x, mem, loss_target, full_first, sgu_ln, {n: weights[n] for n in _REPLICATED}, _rest_blocks(weights), True)
    loss = lax.psum(loss_part[0, 0], ("x", "y", "c"))
    last_parts, rep_grads = _reduce_last(local)

    reduced, deltas, new_m, new_v = {}, {}, {}, {}
    for n in names:
        w2, m2, v2 = _as_2d(weights[n]), _as_2d(mom_m[n]), _as_2d(mom_v[n])
        if n in _BIG:
            outs = _adamw(w2, m2, v2, parts=rest_parts[n], first_parts=last_parts.get(n))
        elif n in last_parts:
            outs = _adamw(w2, m2, v2, parts=last_parts[n])
        else:
            outs = _adamw(w2, m2, v2, grad=_as_2d(rep_grads[n]))
        reduced[n], deltas[n], new_m[n], new_v[n] = (a.reshape(weights[n].shape) for a in outs)

    return (loss, grad_x, *[reduced[n] for n in names], *[deltas[n] for n in names],
            *[new_m[n] for n in names], *[new_v[n] for n in names])


def _local_step(x, mem, loss_target, full_first, sgu_ln, small, rest, rest_sharded):
    sgu_w_s, sgu_b_s = small["sgu_w_s"], small["sgu_b_s"]
    ln_mix_g, ln_mix_b, ln_ffn_g, ln_ffn_b = (small[n] for n in ("ln_mix_g", "ln_mix_b", "ln_ffn_g", "ln_ffn_b"))
    bsz, seq, _ = x.shape
    tokens = bsz * seq
    slopes = _alibi_table()
    later = {} if rest_sharded else dict(zip(_BIG, rest))

    def weight(name, idx):
        if name != "b_w_in" and idx == 0:
            return full_first[name][0]
        return later[name][idx if name == "b_w_in" else idx - 1]

    xf = x.reshape(tokens, D_MODEL)
    xb = xf.astype(BF16)
    memb = mem.reshape(bsz * N_MEM, D_MODEL).astype(BF16)
    tgt = loss_target.reshape(tokens, D_MODEL)

    saved = []
    for i in range(DEPTH):
        j = i // 2
        dil_layer = i % 2 == 0
        w_in = weight("a_w_in" if dil_layer else "b_w_in", j)
        mkv = _linear_nn("mem_kv", memb, weight("w_mem_kv", i))
        h = _linear_nn("in_proj_a" if dil_layer else "in_proj_b", xb, w_in)
        st = dict(x=xf, xb=xb, h=h, mkv=mkv, w_in=w_in)
        if dil_layer:
            carry = rest if (i == 0 and rest_sharded) else None
            mix, st["lse"], gathered = _band_attn_fwd_fused(h, slopes, bsz, seq, gather=carry)
            if carry is not None:
                later = {n: _unshard(n, g) for n, g in zip(_BIG, gathered)}
            q_col = 3 * MIX_W // MEM_W
        else:
            st["ws"] = sgu_w_s[j]
            st["bs_t"] = sgu_b_s[j].T
            st["ln_g"] = sgu_ln["sgu_ln_g"][j].reshape(1, MIX_W)
            st["ln_b"] = sgu_ln["sgu_ln_b"][j].reshape(1, MIX_W)
            mix = _sgu_fwd(h, st["ws"], st["bs_t"], st["ln_g"], st["ln_b"])
            q_col = 2 * MIX_W // MEM_W
        mo = _mem_attn_fwd(h, mkv, bsz, seq, q_col)
        w_out, w_down = weight("w_out", i), weight("w_down", i)
        w_gu = jnp.concatenate([weight("w_gate", i), weight("w_up", i)], axis=-1)
        r1, x1, x1b = _proj_ln_fwd("out_proj_ln", [mix, mo], w_out, xf,
                                   ln_mix_g[i].reshape(1, D_MODEL), ln_mix_b[i].reshape(1, D_MODEL))
        gt, up, act = _ffn_up_fwd(x1b, w_gu)
        r2, x2, x2b = _proj_ln_fwd("ffn_down_ln", [act], w_down, x1,
                                   ln_ffn_g[i].reshape(1, D_MODEL), ln_ffn_b[i].reshape(1, D_MODEL))
        st.update(mix=mix, mo=mo, q_col=q_col, r1=r1, x1b=x1b, gt=gt, up=up, act=act, r2=r2,
                  w_out=w_out, w_down=w_down, w_gu=w_gu)
        saved.append(st)
        xf, xb = x2, x2b

    dx, loss_part = _loss_fwd_bwd(xf, tgt)

    rest_sums, rest_parts = None, None
    per_pair = ("a_w_in", "b_w_in", "sgu_ln_g", "sgu_ln_b", "sgu_w_s", "sgu_b_s")
    grads = {n: [None] * (DEPTH // 2 if n in per_pair else DEPTH) for n in _BIG + _SGU_LN + _REPLICATED}
    for i in reversed(range(DEPTH)):
        j = i // 2
        st = saved[i]
        dil_layer = i % 2 == 0
        w_in = st["w_in"]
        dr2, dr2b, dg, db = _ln_bwd(dx, st["r2"], ln_ffn_g[i].reshape(1, D_MODEL))
        grads["ln_ffn_g"][i], grads["ln_ffn_b"][i] = dg[0], db[0]
        dgu = _ffn_down_bwd(dr2b, st["w_down"], st["gt"], st["up"])
        grads["w_down"][i] = _mm_tn("grad_w_down", st["act"], dr2b)
        dx1 = _linear_nt("ffn_up_bwd", [dgu], st["w_gu"], dr2, F32)
        dw_gu = _mm_tn("grad_w_gate_up", st["x1b"], dgu)
        grads["w_gate"][i], grads["w_up"][i] = dw_gu[:, :D_FF], dw_gu[:, D_FF:]
        dr1, dr1b, dg, db = _ln_bwd(dx1, st["r1"], ln_mix_g[i].reshape(1, D_MODEL))
        grads["ln_mix_g"][i], grads["ln_mix_b"][i] = dg[0], db[0]
        dcat = _linear_nt("out_proj_bwd", [dr1b], st["w_out"], None, BF16)
        grads["w_out"][i] = jnp.concatenate(
            [_mm_tn("grad_w_out_mix", st["mix"], dr1b), _mm_tn("grad_w_out_mem", st["mo"], dr1b)], axis=0)
        dqm, dmkv = _mem_attn_bwd(st["h"], st["mkv"], dcat, bsz, seq, st["q_col"])
        grads["w_mem_kv"][i] = _mm_tn("grad_w_mem_kv", memb, dmkv.astype(BF16))
        if dil_layer:
            dh_parts, exchanged = _band_attn_bwd_fused(st["h"], dcat, st["mix"], st["lse"], slopes, bsz, seq,
                                                       exchange=rest_sums if i == 0 else None)
            if i == 0 and rest_sums is not None:
                rest_parts = dict(zip(_BIG, exchanged))
        else:
            ws_t = jnp.swapaxes(st["ws"], -1, -2)
            dh_main, dws, dbs_t, dlg, dlb = _sgu_bwd(st["h"], dcat, st["ws"], ws_t, st["bs_t"], st["ln_g"], st["ln_b"])
            grads["sgu_w_s"][j], grads["sgu_b_s"][j] = dws, dbs_t.T
            grads["sgu_ln_g"][j], grads["sgu_ln_b"][j] = dlg[0], dlb[0]
            dh_parts = [dh_main]
        dx = _linear_nt("in_proj_bwd_a" if dil_layer else "in_proj_bwd_b", [*dh_parts, dqm], w_in, dr1, F32)
        grads["a_w_in" if dil_layer else "b_w_in"][j] = jnp.concatenate(
            [_mm_tn("grad_w_in_part", st["xb"], part) for part in dh_parts] + [_mm_tn("grad_w_in_qm", st["xb"], dqm)], axis=1)
        if i == 1 and rest_sharded:
            rest_sums = _chip_sums_of_rest({n: jnp.stack(_rest_slice(n, grads[n])) for n in _BIG})
    return loss_part, dx.reshape(x.shape), {n: jnp.stack(g) for n, g in grads.items()}, rest_parts
```

```python
import functools
import math

import numpy as np
import jax
import jax.numpy as jnp
from jax import lax
from jax.experimental import pallas as pl
from jax.experimental.pallas import tpu as pltpu

F32 = jnp.float32
BF16 = jnp.bfloat16

D_MODEL = 1024
DEPTH = 4
N_MEM = 256
HEAD_DIM = 64
N_HEADS = 12
MIX_W = N_HEADS * HEAD_DIM
MEM_W = 4 * HEAD_DIM
DIL_PATTERNS = ((128, 1), (512, 4), (2048, 16))
BLK = 128
HEAD_GROUP = 4
N_GROUPS = N_HEADS // HEAD_GROUP
D_FF = 2816
ALPHA = (2 * DEPTH) ** 0.25
LN_EPS = 1e-5
SCALE = HEAD_DIM ** -0.5
NEG = -1e30
N_DEV = 8

ADAM_LR, ADAM_B1, ADAM_B2, ADAM_EPS, ADAM_WD, ADAM_STEP = 0.001, 0.9, 0.999, 1e-08, 0.01, 10

VMEM_LIMIT = 56 * 2 ** 20
STAT_LANES = 32
STAT_W = N_HEADS * STAT_LANES


def _dot_nn(a, b):
    return lax.dot_general(a, b, (((1,), (0,)), ((), ())), preferred_element_type=F32)


def _dot_nt(a, b):
    return lax.dot_general(a, b, (((1,), (1,)), ((), ())), preferred_element_type=F32)


def _dot_tn(a, b):
    return lax.dot_general(a, b, (((0,), (0,)), ((), ())), preferred_element_type=F32)


def _ln_hat(r):
    mu = jnp.mean(r, axis=-1, keepdims=True)
    xc = r - mu
    var = jnp.mean(xc * xc, axis=-1, keepdims=True)
    rstd = lax.rsqrt(var + LN_EPS)
    return xc * rstd, rstd


def _params(sem):
    return pltpu.CompilerParams(dimension_semantics=sem, vmem_limit_bytes=VMEM_LIMIT)


def _rows(tm, c, col=0):
    return pl.BlockSpec((tm, c), lambda i: (i, col))


def _whole(shape):
    nd = len(shape)
    return pl.BlockSpec(tuple(shape), lambda *_: (0,) * nd)


def _resident(shape):
    nd = len(shape)
    return pl.BlockSpec(tuple(shape), lambda *_: (0,) * nd, pipeline_mode=pl.Buffered(1))


def _sds(shape, dtype):
    return jax.ShapeDtypeStruct(tuple(shape), dtype)


def _linear_nn(name, a, w, tm=512):
    t, k = a.shape
    n = w.shape[1]
    tm = min(tm, t)

    def body(a_ref, w_ref, o_ref):
        o_ref[...] = _dot_nn(a_ref[...], w_ref[...]).astype(BF16)

    return pl.pallas_call(
        body, name=name, grid=(t // tm,), in_specs=[_rows(tm, k), _resident(w.shape)], out_specs=_rows(tm, n),
        out_shape=_sds((t, n), BF16), compiler_params=_params(("parallel",)))(a, w)


def _proj_ln_fwd(name, lhs, w, x_res, g, b, tm=256):
    t = x_res.shape[0]
    n_lhs = len(lhs)

    def body(*refs):
        lhs_refs = refs[:n_lhs]
        w_ref, x_ref, g_ref, b_ref, r_ref, xn_ref, xnb_ref = refs[n_lhs:]
        y, off = None, 0
        for lr in lhs_refs:
            k = lr.shape[1]
            term = _dot_nn(lr[...], w_ref[off:off + k, :])
            y = term if y is None else y + term
            off += k
        r = ALPHA * x_ref[...] + y
        xhat, _ = _ln_hat(r)
        xn = xhat * g_ref[...] + b_ref[...]
        r_ref[...] = r
        xn_ref[...] = xn
        xnb_ref[...] = xn.astype(BF16)

    in_specs = [_rows(tm, a.shape[1]) for a in lhs] + [_resident(w.shape), _rows(tm, D_MODEL), _whole(g.shape), _whole(b.shape)]
    return pl.pallas_call(
        body, name=name, grid=(t // tm,), in_specs=in_specs,
        out_specs=[_rows(tm, D_MODEL)] * 3,
        out_shape=[_sds((t, D_MODEL), F32), _sds((t, D_MODEL), F32), _sds((t, D_MODEL), BF16)],
        compiler_params=_params(("parallel",)))(*lhs, w, x_res, g, b)


def _ffn_up_fwd(xb, wgu, tm=256):
    t = xb.shape[0]

    def body(x_ref, w_ref, g_ref, u_ref, a_ref):
        gu = _dot_nn(x_ref[...], w_ref[...])
        gt, up = gu[:, :D_FF], gu[:, D_FF:]
        g_ref[...] = gt.astype(BF16)
        u_ref[...] = up.astype(BF16)
        a_ref[...] = (gt * jax.nn.sigmoid(gt) * up).astype(BF16)

    return pl.pallas_call(
        body, name="ffn_up_fwd", grid=(t // tm,), in_specs=[_rows(tm, D_MODEL), _resident(wgu.shape)],
        out_specs=[_rows(tm, D_FF)] * 3, out_shape=[_sds((t, D_FF), BF16)] * 3,
        compiler_params=_params(("parallel",)))(xb, wgu)


def _loss_fwd_bwd(xn, tgt, tm=512):
    t = xn.shape[0]

    def body(x_ref, t_ref, dx_ref, l_ref):
        @pl.when(pl.program_id(0) == 0)
        def _():
            l_ref[...] = jnp.zeros_like(l_ref)

        e = x_ref[...] - t_ref[...]
        dx_ref[...] = e * (1.0 / D_MODEL)
        l_ref[...] += jnp.sum(e * e) * (0.5 / D_MODEL)

    return pl.pallas_call(
        body, name="loss", grid=(t // tm,), in_specs=[_rows(tm, D_MODEL)] * 2,
        out_specs=[_rows(tm, D_MODEL), _whole((1, 128))], out_shape=[_sds((t, D_MODEL), F32), _sds((1, 128), F32)],
        compiler_params=_params(("arbitrary",)))(xn, tgt)


def _ln_bwd(dx, r, g, tm=512):
    t = dx.shape[0]

    def body(dx_ref, r_ref, g_ref, dr_ref, drb_ref, dg_ref, db_ref):
        @pl.when(pl.program_id(0) == 0)
        def _():
            dg_ref[...] = jnp.zeros_like(dg_ref)
            db_ref[...] = jnp.zeros_like(db_ref)

        dxv = dx_ref[...]
        xhat, rstd = _ln_hat(r_ref[...])
        dxh = dxv * g_ref[...]
        m1 = jnp.mean(dxh, axis=-1, keepdims=True)
        m2 = jnp.mean(dxh * xhat, axis=-1, keepdims=True)
        dr = rstd * (dxh - m1 - xhat * m2)
        dr_ref[...] = dr
        drb_ref[...] = dr.astype(BF16)
        dg_ref[...] += jnp.sum(dxv * xhat, axis=0, keepdims=True)
        db_ref[...] += jnp.sum(dxv, axis=0, keepdims=True)

    return pl.pallas_call(
        body, name="ln_bwd", grid=(t // tm,), in_specs=[_rows(tm, D_MODEL), _rows(tm, D_MODEL), _whole(g.shape)],
        out_specs=[_rows(tm, D_MODEL), _rows(tm, D_MODEL), _whole((1, D_MODEL)), _whole((1, D_MODEL))],
        out_shape=[_sds((t, D_MODEL), F32), _sds((t, D_MODEL), BF16), _sds((1, D_MODEL), F32), _sds((1, D_MODEL), F32)],
        compiler_params=_params(("arbitrary",)))(dx, r, g)


def _ffn_down_bwd(drb, wd, gt, up, tm=256):
    t = drb.shape[0]

    def body(d_ref, w_ref, g_ref, u_ref, o_ref):
        da = _dot_nt(d_ref[...], w_ref[...])
        g = g_ref[...].astype(F32)
        u = u_ref[...].astype(F32)
        sg = jax.nn.sigmoid(g)
        o_ref[:, :D_FF] = (da * u * (sg * (1.0 + g * (1.0 - sg)))).astype(BF16)
        o_ref[:, D_FF:] = (da * (g * sg)).astype(BF16)

    return pl.pallas_call(
        body, name="ffn_down_bwd", grid=(t // tm,),
        in_specs=[_rows(tm, D_MODEL), _resident(wd.shape), _rows(tm, D_FF), _rows(tm, D_FF)],
        out_specs=_rows(tm, 2 * D_FF), out_shape=_sds((t, 2 * D_FF), BF16),
        compiler_params=_params(("parallel",)))(drb, wd, gt, up)


def _linear_nt(name, lhs, w, res, out_dtype, tm=256):
    t = lhs[0].shape[0]
    n_lhs = len(lhs)
    n_out = w.shape[0]

    def body(*refs):
        lhs_refs = refs[:n_lhs]
        w_ref = refs[n_lhs]
        o_ref = refs[-1]
        y, off = None, 0
        for lr in lhs_refs:
            k = lr.shape[1]
            term = _dot_nt(lr[...], w_ref[:, off:off + k])
            y = term if y is None else y + term
            off += k
        if res is not None:
            y = ALPHA * refs[n_lhs + 1][...] + y
        o_ref[...] = y.astype(out_dtype)

    in_specs = [_rows(tm, a.shape[1]) for a in lhs] + [_resident(w.shape)]
    args = list(lhs) + [w]
    if res is not None:
        in_specs.append(_rows(tm, n_out))
        args.append(res)
    return pl.pallas_call(
        body, name=name, grid=(t // tm,), in_specs=in_specs, out_specs=_rows(tm, n_out),
        out_shape=_sds((t, n_out), out_dtype), compiler_params=_params(("parallel",)))(*args)


def _pick_tile(n, limit):
    if n <= limit:
        return n
    best = 128
    for cand in range(128, limit + 1, 128):
        if n % cand == 0:
            best = cand
    return best


def _mm_tn(name, a, b, tt=1024):
    t, k = a.shape
    n = b.shape[1]
    tt = min(tt, t)
    tk = _pick_tile(k, 1408)
    tn = _pick_tile(n, (6 * 2 ** 20) // (4 * tk) // 128 * 128)

    def body(a_ref, b_ref, o_ref):
        @pl.when(pl.program_id(2) == 0)
        def _():
            o_ref[...] = jnp.zeros_like(o_ref)

        o_ref[...] += _dot_tn(a_ref[...], b_ref[...])

    return pl.pallas_call(
        body, name=name, grid=(k // tk, n // tn, t // tt),
        in_specs=[pl.BlockSpec((tt, tk), lambda i, j, s: (s, i)), pl.BlockSpec((tt, tn), lambda i, j, s: (s, j))],
        out_specs=pl.BlockSpec((tk, tn), lambda i, j, s: (i, j)), out_shape=_sds((k, n), F32),
        compiler_params=_params(("parallel", "parallel", "arbitrary")))(a, b)


def _alibi_table():
    arr = np.zeros((N_GROUPS, 8, 128), np.float32)
    for g in range(N_GROUPS):
        for hh in range(HEAD_GROUP):
            arr[g, hh, :] = 2.0 ** (-8.0 * (g * HEAD_GROUP + hh + 1) / N_HEADS)
    return jnp.asarray(arr)


def _band_mask(n, dil):
    qi = lax.broadcasted_iota(jnp.int32, (BLK, 2 * BLK), 0)
    ki = lax.broadcasted_iota(jnp.int32, (BLK, 2 * BLK), 1)
    steps = qi + BLK - ki
    valid = (steps >= 0) & (steps <= BLK) & ((ki >= BLK) | (n > 0))
    return valid, (steps * dil).astype(F32)


def _band_specs(bsz, seq, dil, width):
    cb = width // 256

    def spec(off, prev=False):
        if prev:
            return pl.BlockSpec((None, BLK, 256), lambda b, r, g, n: (b, jnp.maximum(n - 1, 0), r * cb + off + g))
        return pl.BlockSpec((None, BLK, 256), lambda b, r, g, n: (b, n, r * cb + off + g))

    return spec


def _spread_stats(cols):
    lane = lax.broadcasted_iota(jnp.int32, (BLK, HEAD_GROUP * STAT_LANES), 1)
    tile = cols[HEAD_GROUP - 1]
    for hh in range(HEAD_GROUP - 2, -1, -1):
        tile = jnp.where(lane < (hh + 1) * STAT_LANES, cols[hh], tile)
    return tile


def _band_attn_fwd(h, slopes, bsz, seq, dil):
    width = h.shape[1]
    length = seq // dil
    nblk = length // BLK
    hv = h.reshape(bsz, length, dil * width)
    spec = _band_specs(bsz, seq, dil, width)
    k_off, v_off = MIX_W // 256, 2 * MIX_W // 256

    def body(sl_ref, q_ref, kc_ref, kp_ref, vc_ref, vp_ref, o_ref, lse_ref):
        valid, dist = _band_mask(pl.program_id(3), dil)
        q = q_ref[...]
        k2 = jnp.concatenate([kp_ref[...], kc_ref[...]], axis=0)
        v2 = jnp.concatenate([vp_ref[...], vc_ref[...]], axis=0)
        lses = []
        for hh in range(HEAD_GROUP):
            sl = slice(hh * HEAD_DIM, (hh + 1) * HEAD_DIM)
            s = _dot_nt(q[:, sl], k2[:, sl]) * SCALE - sl_ref[hh:hh + 1, 0:1] * dist
            s = jnp.where(valid, s, NEG)
            m = jnp.max(s, axis=-1, keepdims=True)
            p = jnp.exp(s - m)
            l = jnp.sum(p, axis=-1, keepdims=True)
            acc = _dot_nn(p.astype(BF16), v2[:, sl])
            o_ref[:, sl] = (acc / l).astype(BF16)
            lses.append(m + jnp.log(l))
        lse_ref[...] = _spread_stats(lses)

    out, lse = pl.pallas_call(
        body, name=f"band_attn_fwd_d{dil}", grid=(bsz, dil, N_GROUPS, nblk),
        in_specs=[pl.BlockSpec((None, 8, 128), lambda b, r, g, n: (g, 0, 0)),
                  spec(0), spec(k_off), spec(k_off, True), spec(v_off), spec(v_off, True)],
        out_specs=[pl.BlockSpec((None, BLK, 256), lambda b, r, g, n: (b, n, r * N_GROUPS + g)),
                   pl.BlockSpec((None, BLK, 128), lambda b, r, g, n: (b, n, r * N_GROUPS + g))],
        out_shape=[_sds((bsz, length, dil * MIX_W), BF16), _sds((bsz, length, dil * STAT_W), F32)],
        compiler_params=_params(("parallel", "parallel", "parallel", "arbitrary")))(slopes, hv, hv, hv, hv, hv)
    return out.reshape(bsz * seq, MIX_W), lse.reshape(bsz * seq, STAT_W)


def _band_merge(outs, lses, tm=512):
    t = outs[0].shape[0]

    def body(o1, o2, o3, l1, l2, l3, mix_ref, lse_ref):
        ls = [l1[...], l2[...], l3[...]]
        m = jnp.maximum(jnp.maximum(ls[0], ls[1]), ls[2])
        tot = m + jnp.log(jnp.exp(ls[0] - m) + jnp.exp(ls[1] - m) + jnp.exp(ls[2] - m))
        ws = [jnp.exp(x - tot) for x in ls]
        lse_ref[...] = tot
        for hd in range(N_HEADS):
            sl = slice(hd * HEAD_DIM, (hd + 1) * HEAD_DIM)
            acc = None
            for w, o in zip(ws, (o1, o2, o3)):
                term = w[:, hd * STAT_LANES:hd * STAT_LANES + 1] * o[:, sl].astype(F32)
                acc = term if acc is None else acc + term
            mix_ref[:, sl] = acc.astype(BF16)

    return pl.pallas_call(
        body, name="band_merge", grid=(t // tm,), in_specs=[_rows(tm, MIX_W)] * 3 + [_rows(tm, STAT_W)] * 3,
        out_specs=[_rows(tm, MIX_W), _rows(tm, STAT_W)], out_shape=[_sds((t, MIX_W), BF16), _sds((t, STAT_W), F32)],
        compiler_params=_params(("parallel",)))(*outs, *lses)


def _band_delta(dcat, mix, tm=512):
    t = mix.shape[0]

    def body(d_ref, o_ref, dd_ref):
        prod = d_ref[...].astype(F32) * o_ref[...].astype(F32)
        for hd in range(N_HEADS):
            rsum = jnp.sum(prod[:, hd * HEAD_DIM:(hd + 1) * HEAD_DIM], axis=-1, keepdims=True)
            dd_ref[:, hd * STAT_LANES:(hd + 1) * STAT_LANES] = jnp.broadcast_to(rsum, (tm, STAT_LANES))

    return pl.pallas_call(
        body, name="band_delta", grid=(t // tm,), in_specs=[_rows(tm, MIX_W), _rows(tm, MIX_W)],
        out_specs=_rows(tm, STAT_W), out_shape=_sds((t, STAT_W), F32),
        compiler_params=_params(("parallel",)))(dcat, mix)


def _band_attn_bwd(h, dcat, slopes, lse, delta, bsz, seq, dil):
    width = h.shape[1]
    length = seq // dil
    nblk = length // BLK
    hv = h.reshape(bsz, length, dil * width)
    dv_ = dcat.reshape(bsz, length, dil * D_MODEL)
    k_off, v_off = MIX_W // 256, 2 * MIX_W // 256
    cb, dcb = width // 256, D_MODEL // 256

    def cur(off, c):
        return pl.BlockSpec((None, BLK, 256), lambda b, r, g, n: (b, jnp.minimum(n, nblk - 1), r * c + off + g))

    def prev(off, c):
        return pl.BlockSpec((None, BLK, 256), lambda b, r, g, n: (b, jnp.maximum(jnp.minimum(n, nblk - 1) - 1, 0), r * c + off + g))

    stat = pl.BlockSpec((None, BLK, 128), lambda b, r, g, n: (b, jnp.minimum(n, nblk - 1), r * N_GROUPS + g))
    dq_spec = pl.BlockSpec((None, BLK, 256), lambda b, r, g, n: (b, jnp.minimum(n, nblk - 1), r * N_GROUPS + g))
    dkv_spec = pl.BlockSpec((None, BLK, 256), lambda b, r, g, n: (b, jnp.maximum(n - 1, 0), r * N_GROUPS + g))

    def body(sl_ref, q_ref, kc_ref, kp_ref, vc_ref, vp_ref, do_ref, lse_ref, dd_ref, dq_ref, dk_ref, dv_ref, kcar, vcar):
        n = pl.program_id(3)

        @pl.when(n == 0)
        def _():
            kcar[...] = jnp.zeros_like(kcar)
            vcar[...] = jnp.zeros_like(vcar)

        @pl.when(n < nblk)
        def _():
            valid, dist = _band_mask(n, dil)
            q = q_ref[...]
            do = do_ref[...]
            k2 = jnp.concatenate([kp_ref[...], kc_ref[...]], axis=0)
            v2 = jnp.concatenate([vp_ref[...], vc_ref[...]], axis=0)
            for hh in range(HEAD_GROUP):
                sl = slice(hh * HEAD_DIM, (hh + 1) * HEAD_DIM)
                s = _dot_nt(q[:, sl], k2[:, sl]) * SCALE - sl_ref[hh:hh + 1, 0:1] * dist
                s = jnp.where(valid, s, NEG)
                st = slice(hh * STAT_LANES, hh * STAT_LANES + 1)
                p = jnp.exp(s - lse_ref[:, st])
                dp = _dot_nt(do[:, sl], v2[:, sl])
                ds = (p * (dp - dd_ref[:, st]) * SCALE).astype(BF16)
                dq_ref[:, sl] = _dot_nn(ds, k2[:, sl]).astype(BF16)
                dk2 = _dot_tn(ds, q[:, sl])
                dv2 = _dot_tn(p.astype(BF16), do[:, sl])
                dk_ref[:, sl] = (kcar[:, sl] + dk2[:BLK]).astype(BF16)
                dv_ref[:, sl] = (vcar[:, sl] + dv2[:BLK]).astype(BF16)
                kcar[:, sl] = dk2[BLK:]
                vcar[:, sl] = dv2[BLK:]

        @pl.when(n == nblk)
        def _():
            dk_ref[...] = kcar[...].astype(BF16)
            dv_ref[...] = vcar[...].astype(BF16)

    outs = pl.pallas_call(
        body, name=f"band_attn_bwd_d{dil}", grid=(bsz, dil, N_GROUPS, nblk + 1),
        in_specs=[pl.BlockSpec((None, 8, 128), lambda b, r, g, n: (g, 0, 0)),
                  cur(0, cb), cur(k_off, cb), prev(k_off, cb), cur(v_off, cb), prev(v_off, cb), cur(0, dcb), stat, stat],
        out_specs=[dq_spec, dkv_spec, dkv_spec],
        out_shape=[_sds((bsz, length, dil * MIX_W), BF16)] * 3,
        scratch_shapes=[pltpu.VMEM((BLK, 256), F32), pltpu.VMEM((BLK, 256), F32)],
        compiler_params=_params(("parallel", "parallel", "parallel", "arbitrary")))(
            slopes, hv, hv, hv, hv, hv, dv_, lse.reshape(bsz, length, dil * STAT_W), delta.reshape(bsz, length, dil * STAT_W))
    return [o.reshape(bsz * seq, MIX_W) for o in outs]


def _sum_patterns(parts, tm=512):
    t = parts[0][0].shape[0]

    def body(*refs):
        o_ref = refs[-1]
        for j in range(3):
            acc = refs[j][...].astype(F32) + refs[3 + j][...].astype(F32) + refs[6 + j][...].astype(F32)
            o_ref[:, j * MIX_W:(j + 1) * MIX_W] = acc.astype(BF16)

    flat = [x for p in parts for x in p]
    return pl.pallas_call(
        body, name="band_sum", grid=(t // tm,), in_specs=[_rows(tm, MIX_W)] * 9, out_specs=_rows(tm, 3 * MIX_W),
        out_shape=_sds((t, 3 * MIX_W), BF16), compiler_params=_params(("parallel",)))(*flat)


def _block_mask(has_prev, dil):
    qi = lax.broadcasted_iota(jnp.int32, (BLK, 2 * BLK), 0)
    ki = lax.broadcasted_iota(jnp.int32, (BLK, 2 * BLK), 1)
    steps = qi + BLK - ki
    valid = (steps >= 0) & (steps <= BLK) & ((ki >= BLK) | has_prev)
    return valid, (steps * dil).astype(F32)


def _rows_of(j):
    return pl.ds(pl.multiple_of(j * BLK, BLK), BLK)


def _lane_half(hf):
    return slice(hf * 128, (hf + 1) * 128)


def _split_pair(x):
    first = lax.broadcasted_iota(jnp.int32, (1, 2 * HEAD_DIM), 1) < HEAD_DIM
    zero = jnp.zeros_like(x)
    return jnp.where(first, x, zero), jnp.where(first, zero, x)


def _deinterleave(src, dst, seq, dil, dtype):
    length = seq // dil
    for r in range(dil):
        for c in range(length // BLK):
            rows = pl.ds(r + c * BLK * dil, BLK, stride=dil)
            out = slice(r * length + c * BLK, r * length + (c + 1) * BLK)
            if len(src.shape) == 2:
                dst[out, :] = src[rows, :].astype(dtype)
            else:
                for hf in range(2):
                    dst[out, _lane_half(hf)] = src.at[hf][rows, :].astype(dtype)


def _interleave(src, dst, seq, dil, accumulate):
    length = seq // dil
    for r in range(dil):
        for c in range(length // BLK):
            rows = pl.ds(r + c * BLK * dil, BLK, stride=dil)
            inp = slice(r * length + c * BLK, r * length + (c + 1) * BLK)
            if len(dst.shape) == 2:
                dst[rows, :] = dst[rows, :] + src[inp, :] if accumulate else src[inp, :]
            else:
                for hf in range(2):
                    val = src[inp, _lane_half(hf)]
                    half = dst.at[hf]
                    half[rows, :] = half[rows, :] + val if accumulate else val


def _split_halves(src, dst, seq):
    def step(i, carry):
        for hf in range(2):
            dst[hf, _rows_of(i), :] = src[_rows_of(i), _lane_half(hf)].astype(F32)
        return carry

    lax.fori_loop(0, seq // BLK, step, 0)


def _band_attn_fwd_fused(h, slopes, bsz, seq, gather=None):
    width = h.shape[1]
    cb = width // 256
    k_off, v_off = MIX_W // 256, 2 * MIX_W // 256
    nb = seq // BLK

    ng = 0 if gather is None else len(gather)

    def body(*refs):
        sl_ref, q_ref, k_ref, v_ref = refs[:4]
        mix_ref, lse_ref = refs[4 + ng:6 + ng]
        qf, kf, vf, qd, kd, vd, od, ld, o1, o2, o3, l1, l2, l3 = refs[6 + 2 * ng:20 + 2 * ng]
        if ng:
            start, finish = _gather_protocol(refs[4:4 + ng], refs[6 + ng:6 + 2 * ng], *refs[20 + 2 * ng:])
            pl.when((pl.program_id(0) == 0) & (pl.program_id(1) == 0))(start)

        def run(dil, qs, ks, vs, o_dst, l_dst):
            nblk = seq // dil // BLK

            def block(j, carry):
                rows, prows = _rows_of(j), _rows_of(jnp.maximum(j - 1, 0))
                valid, dist = _block_mask((j % nblk) != 0, dil)
                lses = []
                for pr in range(HEAD_GROUP // 2):
                    lanes = _lane_half(pr)
                    q_ab = _split_pair(qs[rows, lanes])
                    k2 = jnp.concatenate([ks[prows, lanes], ks[rows, lanes]], axis=0)
                    v_ab = _split_pair(jnp.concatenate([vs[prows, lanes], vs[rows, lanes]], axis=0))
                    out = None
                    for ab in range(2):
                        hh = 2 * pr + ab
                        s = _dot_nt(q_ab[ab], k2) * SCALE - sl_ref[hh:hh + 1, 0:1] * dist
                        s = jnp.where(valid, s, NEG)
                        m = jnp.max(s, axis=-1, keepdims=True)
                        p = jnp.exp(s - m)
                        l = jnp.sum(p, axis=-1, keepdims=True)
                        term = _dot_nn(p.astype(BF16), v_ab[ab]) / l
                        out = term if out is None else out + term
                        lses.append(m + jnp.log(l))
                    o_dst[rows, lanes] = out
                l_dst[rows, :] = _spread_stats(lses)
                return carry

            lax.fori_loop(0, nb, block, 0, unroll=2)

        run(1, q_ref, k_ref, v_ref, o1, l1)
        _split_halves(q_ref, qf, seq)
        _split_halves(k_ref, kf, seq)
        _split_halves(v_ref, vf, seq)
        for dil, o_tok, l_tok in ((4, o2, l2), (16, o3, l3)):
            _deinterleave(qf, qd, seq, dil, BF16)
            _deinterleave(kf, kd, seq, dil, BF16)
            _deinterleave(vf, vd, seq, dil, BF16)
            run(dil, qd, kd, vd, od, ld)
            _interleave(od, o_tok, seq, dil, False)
            _interleave(ld, l_tok, seq, dil, False)

        def merge(i, carry):
            rows = _rows_of(i)
            ls = [l1[rows, :], l2[rows, :], l3[rows, :]]
            m = jnp.maximum(jnp.maximum(ls[0], ls[1]), ls[2])
            tot = m + jnp.log(jnp.exp(ls[0] - m) + jnp.exp(ls[1] - m) + jnp.exp(ls[2] - m))
            ws = [jnp.exp(x - tot) for x in ls]
            lse_ref[rows, :] = tot
            for hh in range(HEAD_GROUP):
                sl = slice(hh * HEAD_DIM, (hh + 1) * HEAD_DIM)
                st = slice(hh * STAT_LANES, hh * STAT_LANES + 1)
                hf, hl = hh // 2, slice((hh % 2) * HEAD_DIM, (hh % 2 + 1) * HEAD_DIM)
                acc = ws[0][:, st] * o1[rows, sl] + ws[1][:, st] * o2[hf, rows, hl] + ws[2][:, st] * o3[hf, rows, hl]
                mix_ref[rows, sl] = acc.astype(BF16)
            return carry

        lax.fori_loop(0, nb, merge, 0)
        if ng:
            pl.when((pl.program_id(0) == bsz - 1) & (pl.program_id(1) == N_GROUPS - 1))(finish)

    def hspec(off):
        return pl.BlockSpec((seq, 256), lambda b, g: (b, off + g))

    big = lambda dt: pltpu.VMEM((seq, 256), dt)
    halves = lambda: pltpu.VMEM((2, seq, 128), F32)
    stat = lambda: pltpu.VMEM((seq, 128), F32)
    outs = pl.pallas_call(
        body, name="band_attn_fwd_gather" if ng else "band_attn_fwd", grid=(bsz, N_GROUPS),
        in_specs=[pl.BlockSpec((None, 8, 128), lambda b, g: (g, 0, 0)), hspec(0), hspec(k_off), hspec(v_off)] + [_ANY] * ng,
        out_specs=[pl.BlockSpec((seq, 256), lambda b, g: (b, g)), pl.BlockSpec((seq, 128), lambda b, g: (b, g))] + [_ANY] * ng,
        out_shape=[_sds((bsz * seq, MIX_W), BF16), _sds((bsz * seq, STAT_W), F32)] + (_gather_shapes(gather) if ng else []),
        scratch_shapes=[halves(), halves(), halves(), big(BF16), big(BF16), big(BF16), big(F32), stat(),
                        big(F32), halves(), halves(), stat(), stat(), stat()] + (_gather_sems(ng) if ng else []),
        compiler_params=_params(("arbitrary", "arbitrary")))(slopes, h, h, h, *(gather or []))
    return outs[0], outs[1], list(outs[2:])


def _band_attn_bwd_fused(h, dcat, mix, lse, slopes, bsz, seq, exchange=None):
    width = h.shape[1]
    k_off, v_off = MIX_W // 256, 2 * MIX_W // 256
    nb = seq // BLK

    ne = 0 if exchange is None else len(exchange)

    def body(*refs):
        sl_ref, q_ref, k_ref, v_ref, do_ref, o_ref, lse_ref = refs[:7]
        dq_ref, dk_ref, dv_ref = refs[7 + ne:10 + ne]
        qf, kf, vf, dof, ddt, qd, kd, vd, dod, lsd, ddd, gq, gk, gv, aq, ak, av = refs[10 + 2 * ne:27 + 2 * ne]
        if ne:
            start, finish = _exchange_protocol(refs[7:7 + ne], refs[10 + ne:10 + 2 * ne], *refs[27 + 2 * ne:])
            pl.when((pl.program_id(0) == 0) & (pl.program_id(1) == 0))(start)

        def delta(i, carry):
            rows = _rows_of(i)
            prod = do_ref[rows, :].astype(F32) * o_ref[rows, :].astype(F32)
            ddt[rows, :] = _spread_stats(
                [jnp.sum(prod[:, hh * HEAD_DIM:(hh + 1) * HEAD_DIM], axis=-1, keepdims=True) for hh in range(HEAD_GROUP)])
            return carry

        lax.fori_loop(0, nb, delta, 0)

        def zero(i, carry):
            rows = _rows_of(i)
            for ref in (gk, gv):
                ref[rows, :] = jnp.zeros((BLK, 256), F32)
            return carry

        def run(dil, qs, ks, vs, dos, lss, dds):
            nblk = seq // dil // BLK
            lax.fori_loop(0, nb, zero, 0)

            def block(j, carry):
                rows, prows = _rows_of(j), _rows_of(jnp.maximum(j - 1, 0))
                valid, dist = _block_mask((j % nblk) != 0, dil)
                for pr in range(HEAD_GROUP // 2):
                    lanes = _lane_half(pr)
                    q_ab = _split_pair(qs[rows, lanes])
                    do_ab = _split_pair(dos[rows, lanes])
                    k2 = jnp.concatenate([ks[prows, lanes], ks[rows, lanes]], axis=0)
                    v2 = jnp.concatenate([vs[prows, lanes], vs[rows, lanes]], axis=0)
                    k_ab = _split_pair(k2)
                    dq, dk2, dv2 = None, None, None
                    for ab in range(2):
                        hh = 2 * pr + ab
                        st = slice(hh * STAT_LANES, hh * STAT_LANES + 1)
                        s = _dot_nt(q_ab[ab], k2) * SCALE - sl_ref[hh:hh + 1, 0:1] * dist
                        s = jnp.where(valid, s, NEG)
                        p = jnp.exp(s - lss[rows, st])
                        dp = _dot_nt(do_ab[ab], v2)
                        ds = (p * (dp - dds[rows, st]) * SCALE).astype(BF16)
                        terms = (_dot_nn(ds, k_ab[ab]), _dot_tn(ds, q_ab[ab]), _dot_tn(p.astype(BF16), do_ab[ab]))
                        dq, dk2, dv2 = terms if dq is None else (dq + terms[0], dk2 + terms[1], dv2 + terms[2])
                    gq[rows, lanes] = dq
                    gk[prows, lanes] += dk2[:BLK]
                    gv[prows, lanes] += dv2[:BLK]
                    gk[rows, lanes] += dk2[BLK:]
                    gv[rows, lanes] += dv2[BLK:]
                return carry

            lax.fori_loop(0, nb, block, 0, unroll=2)

        run(1, q_ref, k_ref, v_ref, do_ref, lse_ref, ddt)

        for src, dst in ((gq, aq), (gk, ak), (gv, av), (q_ref, qf), (k_ref, kf), (v_ref, vf), (do_ref, dof)):
            _split_halves(src, dst, seq)
        for dil in (4, 16):
            for src, dst in ((qf, qd), (kf, kd), (vf, vd), (dof, dod)):
                _deinterleave(src, dst, seq, dil, BF16)
            _deinterleave(lse_ref, lsd, seq, dil, F32)
            _deinterleave(ddt, ddd, seq, dil, F32)
            run(dil, qd, kd, vd, dod, lsd, ddd)
            for src, dst in ((gq, aq), (gk, ak), (gv, av)):
                _interleave(src, dst, seq, dil, True)

        def write(i, carry):
            rows = _rows_of(i)
            for src, dst in ((aq, dq_ref), (ak, dk_ref), (av, dv_ref)):
                for hf in range(2):
                    dst[rows, _lane_half(hf)] = src[hf, rows, :].astype(BF16)
            return carry

        lax.fori_loop(0, nb, write, 0)
        if ne:
            pl.when((pl.program_id(0) == bsz - 1) & (pl.program_id(1) == N_GROUPS - 1))(finish)

    def hspec(off):
        return pl.BlockSpec((seq, 256), lambda b, g: (b, off + g))

    io = pl.BlockSpec((seq, 256), lambda b, g: (b, g))
    big = lambda dt: pltpu.VMEM((seq, 256), dt)
    halves = lambda: pltpu.VMEM((2, seq, 128), F32)
    stat = lambda: pltpu.VMEM((seq, 128), F32)
    outs = pl.pallas_call(
        body, name="band_attn_bwd_exchange" if ne else "band_attn_bwd", grid=(bsz, N_GROUPS),
        in_specs=[pl.BlockSpec((None, 8, 128), lambda b, g: (g, 0, 0)), hspec(0), hspec(k_off), hspec(v_off), io, io,
                  pl.BlockSpec((seq, 128), lambda b, g: (b, g))] + [_ANY] * ne,
        out_specs=[io, io, io] + [_ANY] * ne,
        out_shape=[_sds((bsz * seq, MIX_W), BF16)] * 3 + [_sds(s.shape, s.dtype) for s in (exchange or [])],
        scratch_shapes=[halves(), halves(), halves(), halves(), stat(),
                        big(BF16), big(BF16), big(BF16), big(BF16), stat(), stat(),
                        big(F32), big(F32), big(F32), halves(), halves(), halves()] + (_exchange_sems(ne) if ne else []),
        compiler_params=_params(("arbitrary", "arbitrary")))(slopes, h, h, h, dcat, mix, lse, *(exchange or []))
    return list(outs[:3]), list(outs[3:])


def _mem_attn_fwd(h, mkv, bsz, seq, q_col, tq=512):
    nq = seq // tq

    def body(q_ref, kv_ref, o_ref):
        q = q_ref[...]
        kv = kv_ref[...]
        for hh in range(4):
            sl = slice(hh * HEAD_DIM, (hh + 1) * HEAD_DIM)
            s = _dot_nt(q[:, sl], kv[:, sl]) * SCALE
            m = jnp.max(s, axis=-1, keepdims=True)
            p = jnp.exp(s - m)
            l = jnp.sum(p, axis=-1, keepdims=True)
            acc = _dot_nn(p.astype(BF16), kv[:, MEM_W + hh * HEAD_DIM:MEM_W + (hh + 1) * HEAD_DIM])
            o_ref[:, sl] = (acc / l).astype(BF16)

    return pl.pallas_call(
        body, name="mem_attn_fwd", grid=(bsz, nq),
        in_specs=[pl.BlockSpec((tq, MEM_W), lambda b, i: (b * nq + i, q_col)),
                  pl.BlockSpec((N_MEM, 2 * MEM_W), lambda b, i: (b, 0))],
        out_specs=pl.BlockSpec((tq, MEM_W), lambda b, i: (b * nq + i, 0)),
        out_shape=_sds((bsz * seq, MEM_W), BF16), compiler_params=_params(("parallel", "parallel")))(h, mkv)


def _mem_attn_bwd(h, mkv, dcat, bsz, seq, q_col, tq=512):
    nq = seq // tq
    do_col = MIX_W // MEM_W

    def body(q_ref, kv_ref, do_ref, dq_ref, dkv_ref):
        @pl.when(pl.program_id(1) == 0)
        def _():
            dkv_ref[...] = jnp.zeros_like(dkv_ref)

        q = q_ref[...]
        kv = kv_ref[...]
        do = do_ref[...]
        for hh in range(4):
            sl = slice(hh * HEAD_DIM, (hh + 1) * HEAD_DIM)
            vsl = slice(MEM_W + hh * HEAD_DIM, MEM_W + (hh + 1) * HEAD_DIM)
            s = _dot_nt(q[:, sl], kv[:, sl]) * SCALE
            m = jnp.max(s, axis=-1, keepdims=True)
            e = jnp.exp(s - m)
            p = e / jnp.sum(e, axis=-1, keepdims=True)
            dp = _dot_nt(do[:, sl], kv[:, vsl])
            dd = jnp.sum(p * dp, axis=-1, keepdims=True)
            ds = (p * (dp - dd) * SCALE).astype(BF16)
            dq_ref[:, sl] = _dot_nn(ds, kv[:, sl]).astype(BF16)
            dkv_ref[:, sl] += _dot_tn(ds, q[:, sl])
            dkv_ref[:, vsl] += _dot_tn(p.astype(BF16), do[:, sl])

    return pl.pallas_call(
        body, name="mem_attn_bwd", grid=(bsz, nq),
        in_specs=[pl.BlockSpec((tq, MEM_W), lambda b, i: (b * nq + i, q_col)),
                  pl.BlockSpec((N_MEM, 2 * MEM_W), lambda b, i: (b, 0)),
                  pl.BlockSpec((tq, MEM_W), lambda b, i: (b * nq + i, do_col))],
        out_specs=[pl.BlockSpec((tq, MEM_W), lambda b, i: (b * nq + i, 0)),
                   pl.BlockSpec((N_MEM, 2 * MEM_W), lambda b, i: (b, 0))],
        out_shape=[_sds((bsz * seq, MEM_W), BF16), _sds((bsz * N_MEM, 2 * MEM_W), F32)],
        compiler_params=_params(("parallel", "arbitrary")))(h, mkv, dcat)


_GELU_C = math.sqrt(2.0 / math.pi)
_GELU_A = 0.044715


def _gelu(x):
    return 0.5 * x * (1.0 + jnp.tanh(_GELU_C * (x + _GELU_A * x * x * x)))


def _gelu_grad(x):
    th = jnp.tanh(_GELU_C * (x + _GELU_A * x * x * x))
    return 0.5 * (1.0 + th) + 0.5 * x * (1.0 - th * th) * (_GELU_C * (1.0 + 3.0 * _GELU_A * x * x))


def _tril_mask(lower):
    ri = lax.broadcasted_iota(jnp.int32, (BLK, BLK), 0)
    ci = lax.broadcasted_iota(jnp.int32, (BLK, BLK), 1)
    return (ri >= ci) if lower else (ci >= ri)


def _sgu_fwd(h, ws, bs_t, ln_g, ln_b, tm=512):
    t = h.shape[0]

    def body(u_ref, v_ref, ws_ref, bs_ref, g_ref, b_ref, o_ref):
        ug = _gelu(u_ref[...].astype(F32))
        vhat, _ = _ln_hat(_gelu(v_ref[...].astype(F32)))
        vn = (vhat * g_ref[...] + b_ref[...]).astype(BF16)
        mask = _tril_mask(True)
        first = lax.broadcasted_iota(jnp.int32, (1, 2 * HEAD_DIM), 1) < HEAD_DIM
        for pr in range(N_HEADS // 2):
            lanes = _lane_half(pr)
            w_ab = [jnp.where(mask, ws_ref[2 * pr + ab], 0).astype(BF16) for ab in range(2)]
            bias = jnp.where(first, bs_ref[:, 2 * pr:2 * pr + 1], bs_ref[:, 2 * pr + 1:2 * pr + 2])
            for c in range(tm // BLK):
                rs = slice(c * BLK, (c + 1) * BLK)
                v_ab = _split_pair(vn[rs, lanes])
                mixed = _dot_nn(w_ab[0], v_ab[0]) + _dot_nn(w_ab[1], v_ab[1]) + bias
                o_ref[rs, lanes] = (ug[rs, lanes] * mixed).astype(BF16)

    return pl.pallas_call(
        body, name="sgu_fwd", grid=(t // tm,),
        in_specs=[_rows(tm, MIX_W, 0), _rows(tm, MIX_W, 1), _whole(ws.shape), _whole(bs_t.shape), _whole(ln_g.shape), _whole(ln_b.shape)],
        out_specs=_rows(tm, MIX_W), out_shape=_sds((t, MIX_W), BF16),
        compiler_params=_params(("parallel",)))(h, h, ws, bs_t, ln_g, ln_b)


def _sgu_bwd(h, dcat, ws, ws_t, bs_t, ln_g, ln_b, tm=512):
    t = h.shape[0]

    def body(u_ref, v_ref, do_ref, ws_ref, wst_ref, bs_ref, g_ref, b_ref, dh_ref, dws_ref, dbs_ref, dg_ref, db_ref, dvn_ref):
        @pl.when(pl.program_id(0) == 0)
        def _():
            dws_ref[...] = jnp.zeros_like(dws_ref)
            dbs_ref[...] = jnp.zeros_like(dbs_ref)
            dg_ref[...] = jnp.zeros_like(dg_ref)
            db_ref[...] = jnp.zeros_like(db_ref)

        u = u_ref[...].astype(F32)
        v = v_ref[...].astype(F32)
        do = do_ref[...].astype(F32)
        ug = _gelu(u)
        vhat, rstd = _ln_hat(_gelu(v))
        vn = (vhat * g_ref[...] + b_ref[...]).astype(BF16)
        dmixed_f = do * ug
        dmixed = dmixed_f.astype(BF16)
        low, upp = _tril_mask(True), _tril_mask(False)
        first = lax.broadcasted_iota(jnp.int32, (1, 2 * HEAD_DIM), 1) < HEAD_DIM
        for pr in range(N_HEADS // 2):
            lanes = _lane_half(pr)
            w_ab = [jnp.where(low, ws_ref[2 * pr + ab], 0).astype(BF16) for ab in range(2)]
            wt_ab = [jnp.where(upp, wst_ref[2 * pr + ab], 0).astype(BF16) for ab in range(2)]
            bias = jnp.where(first, bs_ref[:, 2 * pr:2 * pr + 1], bs_ref[:, 2 * pr + 1:2 * pr + 2])
            dws_acc = [None, None]
            dbs_acc = [None, None]
            for c in range(tm // BLK):
                rs = slice(c * BLK, (c + 1) * BLK)
                vn_pair = vn[rs, lanes]
                v_ab = _split_pair(vn_pair)
                mixed = _dot_nn(w_ab[0], v_ab[0]) + _dot_nn(w_ab[1], v_ab[1]) + bias
                dh_ref[rs, lanes] = (do[rs, lanes] * mixed * _gelu_grad(u[rs, lanes])).astype(BF16)
                dm_ab = _split_pair(dmixed[rs, lanes])
                dmf_ab = _split_pair(dmixed_f[rs, lanes])
                for ab in range(2):
                    term = _dot_nt(dm_ab[ab], vn_pair)
                    dws_acc[ab] = term if dws_acc[ab] is None else dws_acc[ab] + term
                    rsum = jnp.sum(dmf_ab[ab], axis=-1, keepdims=True)
                    dbs_acc[ab] = rsum if dbs_acc[ab] is None else dbs_acc[ab] + rsum
                dvn_ref[rs, lanes] = _dot_nn(wt_ab[0], dm_ab[0]) + _dot_nn(wt_ab[1], dm_ab[1])
            for ab in range(2):
                g = 2 * pr + ab
                dws_ref[g] += jnp.where(low, dws_acc[ab], 0.0)
                dbs_ref[:, g:g + 1] += dbs_acc[ab]
        dvn = dvn_ref[...]
        dg_ref[...] += jnp.sum(dvn * vhat, axis=0, keepdims=True)
        db_ref[...] += jnp.sum(dvn, axis=0, keepdims=True)
        dxh = dvn * g_ref[...]
        m1 = jnp.mean(dxh, axis=-1, keepdims=True)
        m2 = jnp.mean(dxh * vhat, axis=-1, keepdims=True)
        dvg = rstd * (dxh - m1 - vhat * m2)
        dh_ref[:, MIX_W:] = (dvg * _gelu_grad(v)).astype(BF16)

    return pl.pallas_call(
        body, name="sgu_bwd", grid=(t // tm,),
        in_specs=[_rows(tm, MIX_W, 0), _rows(tm, MIX_W, 1), _rows(tm, MIX_W, 0), _whole(ws.shape), _whole(ws_t.shape),
                  _whole(bs_t.shape), _whole(ln_g.shape), _whole(ln_b.shape)],
        out_specs=[_rows(tm, 2 * MIX_W), _whole(ws.shape), _whole(bs_t.shape), _whole((1, MIX_W)), _whole((1, MIX_W))],
        out_shape=[_sds((t, 2 * MIX_W), BF16), _sds(ws.shape, F32), _sds(bs_t.shape, F32), _sds((1, MIX_W), F32), _sds((1, MIX_W), F32)],
        scratch_shapes=[pltpu.VMEM((tm, MIX_W), F32)],
        compiler_params=_params(("arbitrary",)))(h, h, dcat, ws, ws_t, bs_t, ln_g, ln_b)


def _row_tile(rows, cols, itemsize=4, limit=2 ** 20):
    best = rows
    for cand in (4096, 2048, 1024, 512, 256, 128, 64, 32, 16):
        if rows % cand == 0 and rows > cand:
            best = cand
            if cand * cols * itemsize <= limit:
                break
    return best


def _adamw(w, m, v, grad=None, parts=None, first_parts=None):
    rows, cols = w.shape
    rows0 = 0 if first_parts is None else first_parts.shape[1]
    tr = _row_tile(rows0 if rows0 else rows, cols)
    n0 = rows0 // tr

    def chip_sum(ref):
        acc = ref[0].astype(F32)
        for k in range(1, 4):
            acc = acc + ref[k].astype(F32)
        return acc

    def body(*refs):
        w_ref, m_ref, v_ref = refs[:3]
        go_ref, d_ref, nm_ref, nv_ref = refs[-4:]
        if parts is None:
            gv = refs[3][...]
        elif first_parts is None:
            gv = chip_sum(refs[3])
        else:
            gv = jnp.where(pl.program_id(0) < n0, chip_sum(refs[3]), chip_sum(refs[4]))
        nm = ADAM_B1 * m_ref[...] + (1.0 - ADAM_B1) * gv
        nv = ADAM_B2 * v_ref[...] + (1.0 - ADAM_B2) * (gv * gv)
        m_hat = nm / (1.0 - ADAM_B1 ** ADAM_STEP)
        v_hat = nv / (1.0 - ADAM_B2 ** ADAM_STEP)
        go_ref[...] = gv
        d_ref[...] = -ADAM_LR * (m_hat / (jnp.sqrt(v_hat) + ADAM_EPS) + ADAM_WD * w_ref[...])
        nm_ref[...] = nm
        nv_ref[...] = nv

    spec = _rows(tr, cols)
    if parts is None:
        g_specs, g_args = [spec], [grad]
    elif first_parts is None:
        g_specs, g_args = [pl.BlockSpec((4, tr, cols), lambda i: (0, i, 0))], [parts]
    else:
        g_specs = [pl.BlockSpec((4, tr, cols), lambda i: (0, jnp.minimum(i, n0 - 1), 0)),
                   pl.BlockSpec((4, tr, cols), lambda i: (0, jnp.maximum(i - n0, 0), 0))]
        g_args = [first_parts, parts]
    return pl.pallas_call(
        body, name="adamw" if parts is None else "adamw_sum_chips", grid=(rows // tr,), in_specs=[spec] * 3 + g_specs,
        out_specs=[spec] * 4, out_shape=[_sds(w.shape, F32)] * 4,
        compiler_params=_params(("parallel",)))(w, m, v, *g_args)


_ANY = pl.BlockSpec(memory_space=pl.ANY)
_MESH = pl.DeviceIdType.MESH


def _all_gather(name, blocks):
    nt = len(blocks)

    def body(*refs):
        start, finish = _gather_protocol(refs[:nt], refs[nt:2 * nt], *refs[2 * nt:])
        start()
        finish()

    return pl.pallas_call(
        body, name=name, out_shape=_gather_shapes(blocks), in_specs=[_ANY] * nt, out_specs=[_ANY] * nt,
        scratch_shapes=_gather_sems(nt))(*blocks)


def _gather_shapes(blocks):
    return [_sds((N_DEV,) + b.shape, b.dtype) for b in blocks]


def _gather_sems(nt):
    return [pltpu.SemaphoreType.DMA((nt, 7)), pltpu.SemaphoreType.DMA((nt, 7)), pltpu.SemaphoreType.DMA((nt,))]


def _gather_protocol(x_refs, out_refs, send_sems, recv_sems, local_sems):
    nt = len(x_refs)
    x, y, c = lax.axis_index("x"), lax.axis_index("y"), lax.axis_index("c")
    me, sibling = (x, y, c), (x, y, 1 - c)
    chips = [(1 - x, y), (x, 1 - y), (1 - x, 1 - y)]

    def slot(t, px, py, pc):
        return out_refs[t].at[4 * px + 2 * py + pc]

    def copy(t, k, blk, to, src=None):
        return pltpu.make_async_remote_copy(
            src_ref=slot(t, *blk) if src is None else src, dst_ref=slot(t, *blk),
            send_sem=send_sems.at[t, k], recv_sem=recv_sems.at[t, k], device_id=to, device_id_type=_MESH)

    def own_copies():
        mine = [pltpu.make_async_copy(x_refs[t], slot(t, *me), local_sems.at[t]) for t in range(nt)]
        first = []
        for t in range(nt):
            first.append(copy(t, 0, me, sibling, src=x_refs[t]))
            first += [copy(t, 1 + j, me, (*chip, c), src=x_refs[t]) for j, chip in enumerate(chips)]
        return mine, first

    def start():
        mine, first = own_copies()
        for cp in mine + first:
            cp.start()

    def finish():
        mine, first = own_copies()
        passed = []
        for j, chip in enumerate(chips):
            for t in range(nt):
                copy(t, 1 + j, (*chip, c), me).wait_recv()
                fwd = copy(t, 4 + j, (*chip, c), sibling)
                fwd.start()
                passed.append(fwd)
        for t in range(nt):
            copy(t, 0, sibling, me).wait_recv()
        for j, chip in enumerate(chips):
            for t in range(nt):
                copy(t, 4 + j, (*chip, 1 - c), me).wait_recv()
        for cp in first + passed:
            cp.wait_send()
        for cp in mine:
            cp.wait()

    return start, finish


def _swap_with_sibling(packed):
    nt = len(packed)

    def body(*refs):
        p_refs, got_refs = refs[:nt], refs[nt:2 * nt]
        send_sems, recv_sems = refs[2 * nt:]
        x, y, c = lax.axis_index("x"), lax.axis_index("y"), lax.axis_index("c")
        copies = [
            pltpu.make_async_remote_copy(
                src_ref=p_refs[t].at[1 - c], dst_ref=got_refs[t], send_sem=send_sems.at[t], recv_sem=recv_sems.at[t],
                device_id=(x, y, 1 - c), device_id_type=_MESH)
            for t in range(nt)]
        for cp in copies:
            cp.start()
        for cp in copies:
            cp.wait_recv()
        for cp in copies:
            cp.wait_send()

    return pl.pallas_call(
        body, name="grad_swap_sibling", out_shape=[_sds(p.shape[1:], p.dtype) for p in packed], in_specs=[_ANY] * nt,
        out_specs=[_ANY] * nt,
        scratch_shapes=[pltpu.SemaphoreType.DMA((nt,)), pltpu.SemaphoreType.DMA((nt,))])(*packed)


def _chip_sum(packed, got):
    _, nchip, rows, cols = packed.shape
    tr = _row_tile(rows, cols, 2)
    core = lax.axis_index("c").astype(jnp.int32).reshape(1)

    def body(c_ref, p_ref, g_ref, o_ref):
        o_ref[...] = (p_ref[...].astype(F32) + g_ref[...].astype(F32)).astype(o_ref.dtype)

    grid_spec = pltpu.PrefetchScalarGridSpec(
        num_scalar_prefetch=1, grid=(nchip, rows // tr),
        in_specs=[pl.BlockSpec((None, None, tr, cols), lambda k, i, c: (c[0], k, i, 0)),
                  pl.BlockSpec((None, tr, cols), lambda k, i, c: (k, i, 0))],
        out_specs=pl.BlockSpec((None, tr, cols), lambda k, i, c: (k, i, 0)))
    return pl.pallas_call(
        body, name="grad_chip_sum", grid_spec=grid_spec, out_shape=_sds(got.shape, got.dtype),
        compiler_params=_params(("parallel", "parallel")))(core, packed, got)


def _exchange_chips(chip_sums):
    nt = len(chip_sums)

    def body(*refs):
        start, finish = _exchange_protocol(refs[:nt], refs[nt:2 * nt], *refs[2 * nt:])
        start()
        finish()

    return pl.pallas_call(
        body, name="grad_exchange_chips", out_shape=[_sds(s.shape, s.dtype) for s in chip_sums], in_specs=[_ANY] * nt,
        out_specs=[_ANY] * nt, scratch_shapes=_exchange_sems(nt))(*chip_sums)


def _exchange_sems(nt):
    return [pltpu.SemaphoreType.DMA((nt, 3)), pltpu.SemaphoreType.DMA((nt, 3)), pltpu.SemaphoreType.DMA((nt,))]


def _exchange_protocol(s_refs, got_refs, send_sems, recv_sems, local_sems):
    nt = len(s_refs)
    x, y, c = lax.axis_index("x"), lax.axis_index("y"), lax.axis_index("c")
    my_chip = 2 * x + y
    chips = [(1 - x, y), (x, 1 - y), (1 - x, 1 - y)]

    def copy(t, j, src_chip, dst_chip):
        px, py = chips[j]
        return pltpu.make_async_remote_copy(
            src_ref=s_refs[t].at[src_chip], dst_ref=got_refs[t].at[dst_chip], send_sem=send_sems.at[t, j],
            recv_sem=recv_sems.at[t, j], device_id=(px, py, c), device_id_type=_MESH)

    def own_copies():
        mine = [pltpu.make_async_copy(s_refs[t].at[my_chip], got_refs[t].at[my_chip], local_sems.at[t]) for t in range(nt)]
        sends = [copy(t, j, 2 * px + py, my_chip) for t in range(nt) for j, (px, py) in enumerate(chips)]
        return mine, sends

    def start():
        mine, sends = own_copies()
        for cp in mine + sends:
            cp.start()

    def finish():
        mine, sends = own_copies()
        for j, (px, py) in enumerate(chips):
            for t in range(nt):
                copy(t, j, my_chip, 2 * px + py).wait_recv()
        for cp in sends:
            cp.wait_send()
        for cp in mine:
            cp.wait()

    return start, finish


def _sum_chips(got):
    _, rows, cols = got.shape
    tr = _row_tile(rows, cols)

    def body(g_ref, o_ref):
        acc = g_ref[0].astype(F32)
        for k in range(1, 4):
            acc = acc + g_ref[k].astype(F32)
        o_ref[...] = acc

    return pl.pallas_call(
        body, name="grad_sum_chips", grid=(rows // tr,), in_specs=[pl.BlockSpec((4, tr, cols), lambda i: (0, i, 0))],
        out_specs=pl.BlockSpec((tr, cols), lambda i: (i, 0)), out_shape=_sds((rows, cols), F32),
        compiler_params=_params(("parallel",)))(got)


_COL_SHARDED = ("a_w_in", "b_w_in", "w_gate", "w_up")
_ROW_SHARDED = ("w_mem_kv", "w_out", "w_down")
_BIG = ("a_w_in", "b_w_in", "w_mem_kv", "w_out", "w_gate", "w_up", "w_down")
_SGU_LN = ("sgu_ln_g", "sgu_ln_b")
_LN4 = ("ln_mix_g", "ln_mix_b", "ln_ffn_g", "ln_ffn_b")
_REPLICATED = ("sgu_w_s", "sgu_b_s") + _LN4


def _unshard(name, gathered):
    if name in _COL_SHARDED or name in _SGU_LN:
        moved = jnp.moveaxis(gathered, 0, -2)
        return moved.reshape(moved.shape[:-2] + (moved.shape[-2] * moved.shape[-1],))
    moved = jnp.moveaxis(gathered, 0, 1)
    return moved.reshape((moved.shape[0], moved.shape[1] * moved.shape[2]) + moved.shape[3:])


def _by_shard(name, full):
    if name in _COL_SHARDED or name in _SGU_LN:
        split = full.reshape(full.shape[:-1] + (N_DEV, full.shape[-1] // N_DEV))
        return jnp.moveaxis(split, -2, 0)
    split = full.reshape((full.shape[0], N_DEV, full.shape[1] // N_DEV) + full.shape[2:])
    return jnp.moveaxis(split, 1, 0)


_FIRST = tuple(n for n in _BIG if n != "b_w_in")


def _first_slice(name, a):
    return None if name == "b_w_in" else a[:1]


def _rest_slice(name, a):
    return a if name == "b_w_in" else a[1:]


def _gather_first(shards):
    blocks = [shards[n][:1].astype(BF16) for n in _FIRST] + [shards[n] for n in _SGU_LN]
    gathered = _all_gather("first_layer_all_gather", blocks)
    names = _FIRST + _SGU_LN
    full = {n: _unshard(n, g) for n, g in zip(names, gathered)}
    return {n: full[n] for n in _FIRST}, {n: full[n] for n in _SGU_LN}


def _rest_blocks(shards):
    return [_rest_slice(n, shards[n]).astype(BF16) for n in _BIG]


def _two_level(by_dest):
    shp = by_dest.shape[1:]
    split = by_dest.astype(BF16).reshape((4, 2) + shp).swapaxes(0, 1)
    return split.reshape(2, 4, int(np.prod(shp[:-1])), shp[-1])


_EARLY = _BIG + _SGU_LN


def _chip_sums_of_early(grads):
    packed = [_two_level(_by_shard(n, jnp.stack(grads[n][1:] if n == "a_w_in" else grads[n]))) for n in _EARLY]
    ln4 = jnp.stack([jnp.stack(grads[n]) for n in _LN4])
    rep = [jnp.stack(grads["sgu_w_s"]).reshape(N_DEV, -1, BLK), jnp.stack(grads["sgu_b_s"]).reshape(N_DEV, -1, BLK),
           ln4.reshape(N_DEV, -1, D_MODEL)]
    packed += [_two_level(r) for r in rep]
    got = _swap_with_sibling(packed)
    return [_chip_sum(p, g) for p, g in zip(packed, got)]


def _finish_replicated(parts, shapes):
    w_s, b_s, ln_all = _all_gather("replicated_grads_all_gather", [_sum_chips(p) for p in parts])
    ln_all = ln_all.reshape((len(_LN4),) + tuple(shapes[_LN4[0]]))
    rep_grads = {"sgu_w_s": w_s.reshape(shapes["sgu_w_s"]), "sgu_b_s": b_s.reshape(shapes["sgu_b_s"])}
    rep_grads.update({n: ln_all[i] for i, n in enumerate(_LN4)})
    return rep_grads


def _reduce_last(grad_a_first):
    packed = [_two_level(_by_shard("a_w_in", grad_a_first))]
    got = _swap_with_sibling(packed)
    return _exchange_chips([_chip_sum(packed[0], got[0])])[0]


def _as_2d(a):
    if a.ndim == 1:
        return a.reshape(1, -1)
    return a.reshape(-1, a.shape[-1])


def kernel(x, mem, a_w_in, b_w_in, sgu_ln_g, sgu_ln_b, sgu_w_s, sgu_b_s, w_mem_kv, w_out, ln_mix_g, ln_mix_b, w_gate, w_up, w_down, ln_ffn_g, ln_ffn_b, loss_target, m_a_w_in, m_b_w_in, m_sgu_ln_g, m_sgu_ln_b, m_sgu_w_s, m_sgu_b_s, m_w_mem_kv, m_w_out, m_ln_mix_g, m_ln_mix_b, m_w_gate, m_w_up, m_w_down, m_ln_ffn_g, m_ln_ffn_b, v_a_w_in, v_b_w_in, v_sgu_ln_g, v_sgu_ln_b, v_sgu_w_s, v_sgu_b_s, v_w_mem_kv, v_w_out, v_ln_mix_g, v_ln_mix_b, v_w_gate, v_w_up, v_w_down, v_ln_ffn_g, v_ln_ffn_b):
    names = ("a_w_in", "b_w_in", "sgu_ln_g", "sgu_ln_b", "sgu_w_s", "sgu_b_s", "w_mem_kv", "w_out", "ln_mix_g", "ln_mix_b",
             "w_gate", "w_up", "w_down", "ln_ffn_g", "ln_ffn_b")
    weights = dict(zip(names, (a_w_in, b_w_in, sgu_ln_g, sgu_ln_b, sgu_w_s, sgu_b_s, w_mem_kv, w_out, ln_mix_g, ln_mix_b,
                               w_gate, w_up, w_down, ln_ffn_g, ln_ffn_b)))
    mom_m = dict(zip(names, (m_a_w_in, m_b_w_in, m_sgu_ln_g, m_sgu_ln_b, m_sgu_w_s, m_sgu_b_s, m_w_mem_kv, m_w_out, m_ln_mix_g,
                             m_ln_mix_b, m_w_gate, m_w_up, m_w_down, m_ln_ffn_g, m_ln_ffn_b)))
    mom_v = dict(zip(names, (v_a_w_in, v_b_w_in, v_sgu_ln_g, v_sgu_ln_b, v_sgu_w_s, v_sgu_b_s, v_w_mem_kv, v_w_out, v_ln_mix_g,
                             v_ln_mix_b, v_w_gate, v_w_up, v_w_down, v_ln_ffn_g, v_ln_ffn_b)))
    full_first, sgu_ln = _gather_first(weights)
    loss_part, grad_x, local, early = _local_step(
        x, mem, loss_target, full_first, sgu_ln, {n: weights[n] for n in _REPLICATED}, _rest_blocks(weights), True)
    loss = lax.psum(loss_part[0, 0], ("x", "y", "c"))
    early_parts = dict(zip(_EARLY, early))
    rep_grads = _finish_replicated(early[len(_EARLY):], {n: weights[n].shape for n in _REPLICATED})
    a_first_parts = _reduce_last(local["a_w_in"][:1])

    reduced, deltas, new_m, new_v = {}, {}, {}, {}
    for n in names:
        w2, m2, v2 = _as_2d(weights[n]), _as_2d(mom_m[n]), _as_2d(mom_v[n])
        if n in early_parts:
            outs = _adamw(w2, m2, v2, parts=early_parts[n], first_parts=a_first_parts if n == "a_w_in" else None)
        else:
            outs = _adamw(w2, m2, v2, grad=_as_2d(rep_grads[n]))
        reduced[n], deltas[n], new_m[n], new_v[n] = (a.reshape(weights[n].shape) for a in outs)

    return (loss, grad_x, *[reduced[n] for n in names], *[deltas[n] for n in names],
            *[new_m[n] for n in names], *[new_v[n] for n in names])


def _local_step(x, mem, loss_target, full_first, sgu_ln, small, rest, rest_sharded):
    sgu_w_s, sgu_b_s = small["sgu_w_s"], small["sgu_b_s"]
    ln_mix_g, ln_mix_b, ln_ffn_g, ln_ffn_b = (small[n] for n in ("ln_mix_g", "ln_mix_b", "ln_ffn_g", "ln_ffn_b"))
    bsz, seq, _ = x.shape
    tokens = bsz * seq
    slopes = _alibi_table()
    later = {} if rest_sharded else dict(zip(_BIG, rest))

    def weight(name, idx):
        if name != "b_w_in" and idx == 0:
            return full_first[name][0]
        return later[name][idx if name == "b_w_in" else idx - 1]

    xf = x.reshape(tokens, D_MODEL)
    xb = xf.astype(BF16)
    memb = mem.reshape(bsz * N_MEM, D_MODEL).astype(BF16)
    tgt = loss_target.reshape(tokens, D_MODEL)

    saved = []
    for i in range(DEPTH):
        j = i // 2
        dil_layer = i % 2 == 0
        w_in = weight("a_w_in" if dil_layer else "b_w_in", j)
        mkv = _linear_nn("mem_kv", memb, weight("w_mem_kv", i))
        h = _linear_nn("in_proj_a" if dil_layer else "in_proj_b", xb, w_in)
        st = dict(x=xf, xb=xb, h=h, mkv=mkv, w_in=w_in)
        if dil_layer:
            carry = rest if (i == 0 and rest_sharded) else None
            mix, st["lse"], gathered = _band_attn_fwd_fused(h, slopes, bsz, seq, gather=carry)
            if carry is not None:
                later = {n: _unshard(n, g) for n, g in zip(_BIG, gathered)}
            q_col = 3 * MIX_W // MEM_W
        else:
            st["ws"] = sgu_w_s[j]
            st["bs_t"] = sgu_b_s[j].T
            st["ln_g"] = sgu_ln["sgu_ln_g"][j].reshape(1, MIX_W)
            st["ln_b"] = sgu_ln["sgu_ln_b"][j].reshape(1, MIX_W)
            mix = _sgu_fwd(h, st["ws"], st["bs_t"], st["ln_g"], st["ln_b"])
            q_col = 2 * MIX_W // MEM_W
        mo = _mem_attn_fwd(h, mkv, bsz, seq, q_col)
        w_out, w_down = weight("w_out", i), weight("w_down", i)
        w_gu = jnp.concatenate([weight("w_gate", i), weight("w_up", i)], axis=-1)
        r1, x1, x1b = _proj_ln_fwd("out_proj_ln", [mix, mo], w_out, xf,
                                   ln_mix_g[i].reshape(1, D_MODEL), ln_mix_b[i].reshape(1, D_MODEL))
        gt, up, act = _ffn_up_fwd(x1b, w_gu)
        r2, x2, x2b = _proj_ln_fwd("ffn_down_ln", [act], w_down, x1,
                                   ln_ffn_g[i].reshape(1, D_MODEL), ln_ffn_b[i].reshape(1, D_MODEL))
        st.update(mix=mix, mo=mo, q_col=q_col, r1=r1, x1b=x1b, gt=gt, up=up, act=act, r2=r2,
                  w_out=w_out, w_down=w_down, w_gu=w_gu)
        saved.append(st)
        xf, xb = x2, x2b

    dx, loss_part = _loss_fwd_bwd(xf, tgt)

    early_parts = None
    per_pair = ("a_w_in", "b_w_in", "sgu_ln_g", "sgu_ln_b", "sgu_w_s", "sgu_b_s")
    grads = {n: [None] * (DEPTH // 2 if n in per_pair else DEPTH) for n in _BIG + _SGU_LN + _REPLICATED}
    for i in reversed(range(DEPTH)):
        j = i // 2
        st = saved[i]
        dil_layer = i % 2 == 0
        w_in = st["w_in"]
        dr2, dr2b, dg, db = _ln_bwd(dx, st["r2"], ln_ffn_g[i].reshape(1, D_MODEL))
        grads["ln_ffn_g"][i], grads["ln_ffn_b"][i] = dg[0], db[0]
        dgu = _ffn_down_bwd(dr2b, st["w_down"], st["gt"], st["up"])
        grads["w_down"][i] = _mm_tn("grad_w_down", st["act"], dr2b)
        dx1 = _linear_nt("ffn_up_bwd", [dgu], st["w_gu"], dr2, F32)
        dw_gu = _mm_tn("grad_w_gate_up", st["x1b"], dgu)
        grads["w_gate"][i], grads["w_up"][i] = dw_gu[:, :D_FF], dw_gu[:, D_FF:]
        dr1, dr1b, dg, db = _ln_bwd(dx1, st["r1"], ln_mix_g[i].reshape(1, D_MODEL))
        grads["ln_mix_g"][i], grads["ln_mix_b"][i] = dg[0], db[0]
        dcat = _linear_nt("out_proj_bwd", [dr1b], st["w_out"], None, BF16)
        grads["w_out"][i] = jnp.concatenate(
            [_mm_tn("grad_w_out_mix", st["mix"], dr1b), _mm_tn("grad_w_out_mem", st["mo"], dr1b)], axis=0)
        dqm, dmkv = _mem_attn_bwd(st["h"], st["mkv"], dcat, bsz, seq, st["q_col"])
        grads["w_mem_kv"][i] = _mm_tn("grad_w_mem_kv", memb, dmkv.astype(BF16))
        if dil_layer:
            early_sums = _chip_sums_of_early(grads) if (i == 0 and rest_sharded) else None
            dh_parts, exchanged = _band_attn_bwd_fused(st["h"], dcat, st["mix"], st["lse"], slopes, bsz, seq,
                                                       exchange=early_sums)
            if early_sums is not None:
                early_parts = exchanged
        else:
            ws_t = jnp.swapaxes(st["ws"], -1, -2)
            dh_main, dws, dbs_t, dlg, dlb = _sgu_bwd(st["h"], dcat, st["ws"], ws_t, st["bs_t"], st["ln_g"], st["ln_b"])
            grads["sgu_w_s"][j], grads["sgu_b_s"][j] = dws, dbs_t.T
            grads["sgu_ln_g"][j], grads["sgu_ln_b"][j] = dlg[0], dlb[0]
            dh_parts = [dh_main]
        dx = _linear_nt("in_proj_bwd_a" if dil_layer else "in_proj_bwd_b", [*dh_parts, dqm], w_in, dr1, F32)
        grads["a_w_in" if dil_layer else "b_w_in"][j] = jnp.concatenate(
            [_mm_tn("grad_w_in_part", st["xb"], part) for part in dh_parts] + [_mm_tn("grad_w_in_qm", st["xb"], dqm)], axis=1)
    return loss_part, dx.reshape(x.shape), {n: jnp.stack(g) for n, g in grads.items()}, early_parts
```

```python
import functools
import math

import numpy as np
import jax
import jax.numpy as jnp
from jax import lax
from jax.experimental import pallas as pl
from jax.experimental.pallas import tpu as pltpu

F32 = jnp.float32
BF16 = jnp.bfloat16

D_MODEL = 1024
DEPTH = 4
N_MEM = 256
HEAD_DIM = 64
N_HEADS = 12
MIX_W = N_HEADS * HEAD_DIM
MEM_W = 4 * HEAD_DIM
DIL_PATTERNS = ((128, 1), (512, 4), (2048, 16))
BLK = 128
HEAD_GROUP = 4
N_GROUPS = N_HEADS // HEAD_GROUP
D_FF = 2816
ALPHA = (2 * DEPTH) ** 0.25
LN_EPS = 1e-5
SCALE = HEAD_DIM ** -0.5
NEG = -1e30
N_DEV = 8

ADAM_LR, ADAM_B1, ADAM_B2, ADAM_EPS, ADAM_WD, ADAM_STEP = 0.001, 0.9, 0.999, 1e-08, 0.01, 10

VMEM_LIMIT = 56 * 2 ** 20
STAT_LANES = 32
STAT_W = N_HEADS * STAT_LANES


def _dot_nn(a, b):
    return lax.dot_general(a, b, (((1,), (0,)), ((), ())), preferred_element_type=F32)


def _dot_nt(a, b):
    return lax.dot_general(a, b, (((1,), (1,)), ((), ())), preferred_element_type=F32)


def _dot_tn(a, b):
    return lax.dot_general(a, b, (((0,), (0,)), ((), ())), preferred_element_type=F32)


def _ln_hat(r):
    mu = jnp.mean(r, axis=-1, keepdims=True)
    xc = r - mu
    var = jnp.mean(xc * xc, axis=-1, keepdims=True)
    rstd = lax.rsqrt(var + LN_EPS)
    return xc * rstd, rstd


def _params(sem):
    return pltpu.CompilerParams(dimension_semantics=sem, vmem_limit_bytes=VMEM_LIMIT)


def _rows(tm, c, col=0):
    return pl.BlockSpec((tm, c), lambda i: (i, col))


def _whole(shape):
    nd = len(shape)
    return pl.BlockSpec(tuple(shape), lambda *_: (0,) * nd)


def _resident(shape):
    nd = len(shape)
    return pl.BlockSpec(tuple(shape), lambda *_: (0,) * nd, pipeline_mode=pl.Buffered(1))


def _sds(shape, dtype):
    return jax.ShapeDtypeStruct(tuple(shape), dtype)


def _linear_nn(name, a, w, tm=512):
    t, k = a.shape
    n = w.shape[1]
    tm = min(tm, t)

    def body(a_ref, w_ref, o_ref):
        o_ref[...] = _dot_nn(a_ref[...], w_ref[...]).astype(BF16)

    return pl.pallas_call(
        body, name=name, grid=(t // tm,), in_specs=[_rows(tm, k), _resident(w.shape)], out_specs=_rows(tm, n),
        out_shape=_sds((t, n), BF16), compiler_params=_params(("parallel",)))(a, w)


def _proj_ln_fwd(name, lhs, w, res, g, b, tm=256):
    t = res[0].shape[0]
    n_lhs = len(lhs)
    n_res = len(res)

    def body(*refs):
        lhs_refs = refs[:n_lhs]
        w_ref = refs[n_lhs]
        res_refs = refs[n_lhs + 1:n_lhs + 1 + n_res]
        g_ref, b_ref, r_ref, xnb_ref = refs[n_lhs + 1 + n_res:]
        y, off = None, 0
        for lr in lhs_refs:
            k = lr.shape[1]
            term = _dot_nn(lr[...], w_ref[off:off + k, :])
            y = term if y is None else y + term
            off += k
        x_res = res_refs[0][...]
        if n_res == 3:
            x_res = _ln_hat(x_res)[0] * res_refs[1][...] + res_refs[2][...]
        r = ALPHA * x_res + y
        r_ref[...] = r
        xnb_ref[...] = (_ln_hat(r)[0] * g_ref[...] + b_ref[...]).astype(BF16)

    vec = _whole((1, D_MODEL))
    in_specs = ([_rows(tm, a.shape[1]) for a in lhs] + [_resident(w.shape), _rows(tm, D_MODEL)] + [vec] * (n_res - 1) + [vec, vec])
    return pl.pallas_call(
        body, name=name, grid=(t // tm,), in_specs=in_specs, out_specs=[_rows(tm, D_MODEL)] * 2,
        out_shape=[_sds((t, D_MODEL), F32), _sds((t, D_MODEL), BF16)],
        compiler_params=_params(("parallel",)))(*lhs, w, *res, g, b)


def _ffn_up_fwd(xb, wgu, tm=256):
    t = xb.shape[0]

    def body(x_ref, w_ref, g_ref, u_ref, a_ref):
        gu = _dot_nn(x_ref[...], w_ref[...])
        gt, up = gu[:, :D_FF], gu[:, D_FF:]
        g_ref[...] = gt.astype(BF16)
        u_ref[...] = up.astype(BF16)
        a_ref[...] = (gt * jax.nn.sigmoid(gt) * up).astype(BF16)

    return pl.pallas_call(
        body, name="ffn_up_fwd", grid=(t // tm,), in_specs=[_rows(tm, D_MODEL), _resident(wgu.shape)],
        out_specs=[_rows(tm, D_FF)] * 3, out_shape=[_sds((t, D_FF), BF16)] * 3,
        compiler_params=_params(("parallel",)))(xb, wgu)


def _ln_bwd_rows(dxn, xhat, rstd, g_ref, dr_ref, drb_ref, dg_ref, db_ref):
    @pl.when(pl.program_id(0) == 0)
    def _():
        dg_ref[...] = jnp.zeros_like(dg_ref)
        db_ref[...] = jnp.zeros_like(db_ref)

    dxh = dxn * g_ref[...]
    m1 = jnp.mean(dxh, axis=-1, keepdims=True)
    m2 = jnp.mean(dxh * xhat, axis=-1, keepdims=True)
    dr = rstd * (dxh - m1 - xhat * m2)
    dr_ref[...] = dr
    drb_ref[...] = dr.astype(BF16)
    dg_ref[...] += jnp.sum(dxn * xhat, axis=0, keepdims=True)
    db_ref[...] += jnp.sum(dxn, axis=0, keepdims=True)


def _ln_bwd_outs(t, tm):
    vec = _whole((1, D_MODEL))
    specs = [_rows(tm, D_MODEL), _rows(tm, D_MODEL), vec, vec]
    shapes = [_sds((t, D_MODEL), F32), _sds((t, D_MODEL), BF16), _sds((1, D_MODEL), F32), _sds((1, D_MODEL), F32)]
    return specs, shapes


def _loss_ln_bwd(r, g, b, tgt, tm=512):
    t = r.shape[0]

    def body(r_ref, g_ref, b_ref, t_ref, dr_ref, drb_ref, dg_ref, db_ref, l_ref):
        @pl.when(pl.program_id(0) == 0)
        def _():
            l_ref[...] = jnp.zeros_like(l_ref)

        xhat, rstd = _ln_hat(r_ref[...])
        e = xhat * g_ref[...] + b_ref[...] - t_ref[...]
        l_ref[...] += jnp.sum(e * e) * (0.5 / D_MODEL)
        _ln_bwd_rows(e * (1.0 / D_MODEL), xhat, rstd, g_ref, dr_ref, drb_ref, dg_ref, db_ref)

    vec = _whole((1, D_MODEL))
    specs, shapes = _ln_bwd_outs(t, tm)
    return pl.pallas_call(
        body, name="loss_ln_bwd", grid=(t // tm,), in_specs=[_rows(tm, D_MODEL), vec, vec, _rows(tm, D_MODEL)],
        out_specs=specs + [_whole((1, 128))], out_shape=shapes + [_sds((1, 128), F32)],
        compiler_params=_params(("arbitrary",)))(r, g, b, tgt)


def _ffn_down_bwd(drb, wd, gt, up, tm=256):
    t = drb.shape[0]

    def body(d_ref, w_ref, g_ref, u_ref, o_ref):
        da = _dot_nt(d_ref[...], w_ref[...])
        g = g_ref[...].astype(F32)
        u = u_ref[...].astype(F32)
        sg = jax.nn.sigmoid(g)
        o_ref[:, :D_FF] = (da * u * (sg * (1.0 + g * (1.0 - sg)))).astype(BF16)
        o_ref[:, D_FF:] = (da * (g * sg)).astype(BF16)

    return pl.pallas_call(
        body, name="ffn_down_bwd", grid=(t // tm,),
        in_specs=[_rows(tm, D_MODEL), _resident(wd.shape), _rows(tm, D_FF), _rows(tm, D_FF)],
        out_specs=_rows(tm, 2 * D_FF), out_shape=_sds((t, 2 * D_FF), BF16),
        compiler_params=_params(("parallel",)))(drb, wd, gt, up)


def _linear_nt(name, lhs, w, res, out_dtype, ln=None, tm=256):
    t = lhs[0].shape[0]
    n_lhs = len(lhs)
    n_out = w.shape[0]
    n_in = n_lhs + 1 + (res is not None) + (2 if ln else 0)

    def body(*refs):
        lhs_refs = refs[:n_lhs]
        w_ref = refs[n_lhs]
        y, off = None, 0
        for lr in lhs_refs:
            k = lr.shape[1]
            term = _dot_nt(lr[...], w_ref[:, off:off + k])
            y = term if y is None else y + term
            off += k
        if res is not None:
            y = ALPHA * refs[n_lhs + 1][...] + y
        if ln is None:
            refs[-1][...] = y.astype(out_dtype)
        else:
            r_ref, g_ref = refs[n_in - 2:n_in]
            xhat, rstd = _ln_hat(r_ref[...])
            _ln_bwd_rows(y, xhat, rstd, g_ref, *refs[n_in:])

    in_specs = [_rows(tm, a.shape[1]) for a in lhs] + [_resident(w.shape)]
    args = list(lhs) + [w]
    if res is not None:
        in_specs.append(_rows(tm, n_out))
        args.append(res)
    if ln is None:
        out_specs, out_shape, sem = _rows(tm, n_out), _sds((t, n_out), out_dtype), "parallel"
    else:
        in_specs += [_rows(tm, D_MODEL), _whole((1, D_MODEL))]
        args += list(ln)
        (out_specs, out_shape), sem = _ln_bwd_outs(t, tm), "arbitrary"
    return pl.pallas_call(
        body, name=name, grid=(t // tm,), in_specs=in_specs, out_specs=out_specs, out_shape=out_shape,
        compiler_params=_params((sem,)))(*args)


def _pick_tile(n, limit):
    if n <= limit:
        return n
    best = 128
    for cand in range(128, limit + 1, 128):
        if n % cand == 0:
            best = cand
    return best


def _mm_tn(name, a, b, tt=1024):
    t, k = a.shape
    n = b.shape[1]
    tt = min(tt, t)
    tk = _pick_tile(k, 1408)
    tn = _pick_tile(n, (6 * 2 ** 20) // (4 * tk) // 128 * 128)
    steps = t // tt

    def body(a_ref, b_ref, o_ref, acc_ref):
        @pl.when(pl.program_id(2) == 0)
        def _():
            acc_ref[...] = jnp.zeros_like(acc_ref)

        acc_ref[...] += _dot_tn(a_ref[...], b_ref[...])

        @pl.when(pl.program_id(2) == steps - 1)
        def _():
            o_ref[...] = acc_ref[...].astype(BF16)

    return pl.pallas_call(
        body, name=name, grid=(k // tk, n // tn, steps),
        in_specs=[pl.BlockSpec((tt, tk), lambda i, j, s: (s, i)), pl.BlockSpec((tt, tn), lambda i, j, s: (s, j))],
        out_specs=pl.BlockSpec((tk, tn), lambda i, j, s: (i, j)), out_shape=_sds((k, n), BF16),
        scratch_shapes=[pltpu.VMEM((tk, tn), F32)],
        compiler_params=_params(("parallel", "parallel", "arbitrary")))(a, b)


def _alibi_table():
    arr = np.zeros((N_GROUPS, 8, 128), np.float32)
    for g in range(N_GROUPS):
        for hh in range(HEAD_GROUP):
            arr[g, hh, :] = 2.0 ** (-8.0 * (g * HEAD_GROUP + hh + 1) / N_HEADS)
    return jnp.asarray(arr)


def _band_mask(n, dil):
    qi = lax.broadcasted_iota(jnp.int32, (BLK, 2 * BLK), 0)
    ki = lax.broadcasted_iota(jnp.int32, (BLK, 2 * BLK), 1)
    steps = qi + BLK - ki
    valid = (steps >= 0) & (steps <= BLK) & ((ki >= BLK) | (n > 0))
    return valid, (steps * dil).astype(F32)


def _band_specs(bsz, seq, dil, width):
    cb = width // 256

    def spec(off, prev=False):
        if prev:
            return pl.BlockSpec((None, BLK, 256), lambda b, r, g, n: (b, jnp.maximum(n - 1, 0), r * cb + off + g))
        return pl.BlockSpec((None, BLK, 256), lambda b, r, g, n: (b, n, r * cb + off + g))

    return spec


def _spread_stats(cols):
    lane = lax.broadcasted_iota(jnp.int32, (BLK, HEAD_GROUP * STAT_LANES), 1)
    tile = cols[HEAD_GROUP - 1]
    for hh in range(HEAD_GROUP - 2, -1, -1):
        tile = jnp.where(lane < (hh + 1) * STAT_LANES, cols[hh], tile)
    return tile


def _band_attn_fwd(h, slopes, bsz, seq, dil):
    width = h.shape[1]
    length = seq // dil
    nblk = length // BLK
    hv = h.reshape(bsz, length, dil * width)
    spec = _band_specs(bsz, seq, dil, width)
    k_off, v_off = MIX_W // 256, 2 * MIX_W // 256

    def body(sl_ref, q_ref, kc_ref, kp_ref, vc_ref, vp_ref, o_ref, lse_ref):
        valid, dist = _band_mask(pl.program_id(3), dil)
        q = q_ref[...]
        k2 = jnp.concatenate([kp_ref[...], kc_ref[...]], axis=0)
        v2 = jnp.concatenate([vp_ref[...], vc_ref[...]], axis=0)
        lses = []
        for hh in range(HEAD_GROUP):
            sl = slice(hh * HEAD_DIM, (hh + 1) * HEAD_DIM)
            s = _dot_nt(q[:, sl], k2[:, sl]) * SCALE - sl_ref[hh:hh + 1, 0:1] * dist
            s = jnp.where(valid, s, NEG)
            m = jnp.max(s, axis=-1, keepdims=True)
            p = jnp.exp(s - m)
            l = jnp.sum(p, axis=-1, keepdims=True)
            acc = _dot_nn(p.astype(BF16), v2[:, sl])
            o_ref[:, sl] = (acc / l).astype(BF16)
            lses.append(m + jnp.log(l))
        lse_ref[...] = _spread_stats(lses)

    out, lse = pl.pallas_call(
        body, name=f"band_attn_fwd_d{dil}", grid=(bsz, dil, N_GROUPS, nblk),
        in_specs=[pl.BlockSpec((None, 8, 128), lambda b, r, g, n: (g, 0, 0)),
                  spec(0), spec(k_off), spec(k_off, True), spec(v_off), spec(v_off, True)],
        out_specs=[pl.BlockSpec((None, BLK, 256), lambda b, r, g, n: (b, n, r * N_GROUPS + g)),
                   pl.BlockSpec((None, BLK, 128), lambda b, r, g, n: (b, n, r * N_GROUPS + g))],
        out_shape=[_sds((bsz, length, dil * MIX_W), BF16), _sds((bsz, length, dil * STAT_W), F32)],
        compiler_params=_params(("parallel", "parallel", "parallel", "arbitrary")))(slopes, hv, hv, hv, hv, hv)
    return out.reshape(bsz * seq, MIX_W), lse.reshape(bsz * seq, STAT_W)


def _band_merge(outs, lses, tm=512):
    t = outs[0].shape[0]

    def body(o1, o2, o3, l1, l2, l3, mix_ref, lse_ref):
        ls = [l1[...], l2[...], l3[...]]
        m = jnp.maximum(jnp.maximum(ls[0], ls[1]), ls[2])
        tot = m + jnp.log(jnp.exp(ls[0] - m) + jnp.exp(ls[1] - m) + jnp.exp(ls[2] - m))
        ws = [jnp.exp(x - tot) for x in ls]
        lse_ref[...] = tot
        for hd in range(N_HEADS):
            sl = slice(hd * HEAD_DIM, (hd + 1) * HEAD_DIM)
            acc = None
            for w, o in zip(ws, (o1, o2, o3)):
                term = w[:, hd * STAT_LANES:hd * STAT_LANES + 1] * o[:, sl].astype(F32)
                acc = term if acc is None else acc + term
            mix_ref[:, sl] = acc.astype(BF16)

    return pl.pallas_call(
        body, name="band_merge", grid=(t // tm,), in_specs=[_rows(tm, MIX_W)] * 3 + [_rows(tm, STAT_W)] * 3,
        out_specs=[_rows(tm, MIX_W), _rows(tm, STAT_W)], out_shape=[_sds((t, MIX_W), BF16), _sds((t, STAT_W), F32)],
        compiler_params=_params(("parallel",)))(*outs, *lses)


def _band_delta(dcat, mix, tm=512):
    t = mix.shape[0]

    def body(d_ref, o_ref, dd_ref):
        prod = d_ref[...].astype(F32) * o_ref[...].astype(F32)
        for hd in range(N_HEADS):
            rsum = jnp.sum(prod[:, hd * HEAD_DIM:(hd + 1) * HEAD_DIM], axis=-1, keepdims=True)
            dd_ref[:, hd * STAT_LANES:(hd + 1) * STAT_LANES] = jnp.broadcast_to(rsum, (tm, STAT_LANES))

    return pl.pallas_call(
        body, name="band_delta", grid=(t // tm,), in_specs=[_rows(tm, MIX_W), _rows(tm, MIX_W)],
        out_specs=_rows(tm, STAT_W), out_shape=_sds((t, STAT_W), F32),
        compiler_params=_params(("parallel",)))(dcat, mix)


def _band_attn_bwd(h, dcat, slopes, lse, delta, bsz, seq, dil):
    width = h.shape[1]
    length = seq // dil
    nblk = length // BLK
    hv = h.reshape(bsz, length, dil * width)
    dv_ = dcat.reshape(bsz, length, dil * D_MODEL)
    k_off, v_off = MIX_W // 256, 2 * MIX_W // 256
    cb, dcb = width // 256, D_MODEL // 256

    def cur(off, c):
        return pl.BlockSpec((None, BLK, 256), lambda b, r, g, n: (b, jnp.minimum(n, nblk - 1), r * c + off + g))

    def prev(off, c):
        return pl.BlockSpec((None, BLK, 256), lambda b, r, g, n: (b, jnp.maximum(jnp.minimum(n, nblk - 1) - 1, 0), r * c + off + g))

    stat = pl.BlockSpec((None, BLK, 128), lambda b, r, g, n: (b, jnp.minimum(n, nblk - 1), r * N_GROUPS + g))
    dq_spec = pl.BlockSpec((None, BLK, 256), lambda b, r, g, n: (b, jnp.minimum(n, nblk - 1), r * N_GROUPS + g))
    dkv_spec = pl.BlockSpec((None, BLK, 256), lambda b, r, g, n: (b, jnp.maximum(n - 1, 0), r * N_GROUPS + g))

    def body(sl_ref, q_ref, kc_ref, kp_ref, vc_ref, vp_ref, do_ref, lse_ref, dd_ref, dq_ref, dk_ref, dv_ref, kcar, vcar):
        n = pl.program_id(3)

        @pl.when(n == 0)
        def _():
            kcar[...] = jnp.zeros_like(kcar)
            vcar[...] = jnp.zeros_like(vcar)

        @pl.when(n < nblk)
        def _():
            valid, dist = _band_mask(n, dil)
            q = q_ref[...]
            do = do_ref[...]
            k2 = jnp.concatenate([kp_ref[...], kc_ref[...]], axis=0)
            v2 = jnp.concatenate([vp_ref[...], vc_ref[...]], axis=0)
            for hh in range(HEAD_GROUP):
                sl = slice(hh * HEAD_DIM, (hh + 1) * HEAD_DIM)
                s = _dot_nt(q[:, sl], k2[:, sl]) * SCALE - sl_ref[hh:hh + 1, 0:1] * dist
                s = jnp.where(valid, s, NEG)
                st = slice(hh * STAT_LANES, hh * STAT_LANES + 1)
                p = jnp.exp(s - lse_ref[:, st])
                dp = _dot_nt(do[:, sl], v2[:, sl])
                ds = (p * (dp - dd_ref[:, st]) * SCALE).astype(BF16)
                dq_ref[:, sl] = _dot_nn(ds, k2[:, sl]).astype(BF16)
                dk2 = _dot_tn(ds, q[:, sl])
                dv2 = _dot_tn(p.astype(BF16), do[:, sl])
                dk_ref[:, sl] = (kcar[:, sl] + dk2[:BLK]).astype(BF16)
                dv_ref[:, sl] = (vcar[:, sl] + dv2[:BLK]).astype(BF16)
                kcar[:, sl] = dk2[BLK:]
                vcar[:, sl] = dv2[BLK:]

        @pl.when(n == nblk)
        def _():
            dk_ref[...] = kcar[...].astype(BF16)
            dv_ref[...] = vcar[...].astype(BF16)

    outs = pl.pallas_call(
        body, name=f"band_attn_bwd_d{dil}", grid=(bsz, dil, N_GROUPS, nblk + 1),
        in_specs=[pl.BlockSpec((None, 8, 128), lambda b, r, g, n: (g, 0, 0)),
                  cur(0, cb), cur(k_off, cb), prev(k_off, cb), cur(v_off, cb), prev(v_off, cb), cur(0, dcb), stat, stat],
        out_specs=[dq_spec, dkv_spec, dkv_spec],
        out_shape=[_sds((bsz, length, dil * MIX_W), BF16)] * 3,
        scratch_shapes=[pltpu.VMEM((BLK, 256), F32), pltpu.VMEM((BLK, 256), F32)],
        compiler_params=_params(("parallel", "parallel", "parallel", "arbitrary")))(
            slopes, hv, hv, hv, hv, hv, dv_, lse.reshape(bsz, length, dil * STAT_W), delta.reshape(bsz, length, dil * STAT_W))
    return [o.reshape(bsz * seq, MIX_W) for o in outs]


def _sum_patterns(parts, tm=512):
    t = parts[0][0].shape[0]

    def body(*refs):
        o_ref = refs[-1]
        for j in range(3):
            acc = refs[j][...].astype(F32) + refs[3 + j][...].astype(F32) + refs[6 + j][...].astype(F32)
            o_ref[:, j * MIX_W:(j + 1) * MIX_W] = acc.astype(BF16)

    flat = [x for p in parts for x in p]
    return pl.pallas_call(
        body, name="band_sum", grid=(t // tm,), in_specs=[_rows(tm, MIX_W)] * 9, out_specs=_rows(tm, 3 * MIX_W),
        out_shape=_sds((t, 3 * MIX_W), BF16), compiler_params=_params(("parallel",)))(*flat)


def _block_mask(has_prev, dil):
    if has_prev is None:
        steps = lax.broadcasted_iota(jnp.int32, (BLK, BLK), 0) - lax.broadcasted_iota(jnp.int32, (BLK, BLK), 1)
        return steps >= 0, (steps * dil).astype(F32)
    qi = lax.broadcasted_iota(jnp.int32, (BLK, 2 * BLK), 0)
    ki = lax.broadcasted_iota(jnp.int32, (BLK, 2 * BLK), 1)
    steps = qi + BLK - ki
    valid = (steps >= 0) & (steps <= BLK) & ((ki >= BLK) | has_prev)
    return valid, (steps * dil).astype(F32)


def _rows_of(j):
    return pl.ds(pl.multiple_of(j * BLK, BLK), BLK)


def _lane_half(hf):
    return slice(hf * 128, (hf + 1) * 128)


def _split_pair(x):
    first = lax.broadcasted_iota(jnp.int32, (1, 2 * HEAD_DIM), 1) < HEAD_DIM
    zero = jnp.zeros_like(x)
    return jnp.where(first, x, zero), jnp.where(first, zero, x)


def _deinterleave(src, dst, seq, dil, dtype):
    length = seq // dil
    for r in range(dil):
        for c in range(length // BLK):
            rows = pl.ds(r + c * BLK * dil, BLK, stride=dil)
            out = slice(r * length + c * BLK, r * length + (c + 1) * BLK)
            if len(src.shape) == 2:
                dst[out, :] = src[rows, :].astype(dtype)
            else:
                for hf in range(2):
                    dst[out, _lane_half(hf)] = src.at[hf][rows, :].astype(dtype)


def _interleave(src, dst, seq, dil, accumulate):
    length = seq // dil
    for r in range(dil):
        for c in range(length // BLK):
            rows = pl.ds(r + c * BLK * dil, BLK, stride=dil)
            inp = slice(r * length + c * BLK, r * length + (c + 1) * BLK)
            if len(dst.shape) == 2:
                dst[rows, :] = dst[rows, :] + src[inp, :] if accumulate else src[inp, :]
            else:
                for hf in range(2):
                    val = src[inp, _lane_half(hf)]
                    half = dst.at[hf]
                    half[rows, :] = half[rows, :] + val if accumulate else val


def _split_halves(src, dst, seq):
    def step(i, carry):
        for hf in range(2):
            dst[hf, _rows_of(i), :] = src[_rows_of(i), _lane_half(hf)].astype(F32)
        return carry

    lax.fori_loop(0, seq // BLK, step, 0)


def _band_attn_fwd_fused(h, slopes, bsz, seq, gather=None):
    width = h.shape[1]
    cb = width // 256
    k_off, v_off = MIX_W // 256, 2 * MIX_W // 256
    nb = seq // BLK

    ng = 0 if gather is None else len(gather)

    def body(*refs):
        sl_ref, q_ref, k_ref, v_ref = refs[:4]
        mix_ref, lse_ref = refs[4 + ng:6 + ng]
        qf, kf, vf, qd, kd, vd, od, ld, o1, o2, o3, l1, l2, l3 = refs[6 + 2 * ng:20 + 2 * ng]
        if ng:
            start, finish = _gather_protocol(refs[4:4 + ng], refs[6 + ng:6 + 2 * ng], *refs[20 + 2 * ng:])
            pl.when((pl.program_id(0) == 0) & (pl.program_id(1) == 0))(start)

        def run(dil, qs, ks, vs, o_dst, l_dst):
            nblk = seq // dil // BLK

            def block(j, carry):
                rows, prows = _rows_of(j), _rows_of(jnp.maximum(j - 1, 0))
                valid, dist = _block_mask((j % nblk) != 0 if nblk > 1 else None, dil)

                def keys(ref, lanes):
                    if nblk == 1:
                        return ref[rows, lanes]
                    return jnp.concatenate([ref[prows, lanes], ref[rows, lanes]], axis=0)

                lses = []
                for pr in range(HEAD_GROUP // 2):
                    lanes = _lane_half(pr)
                    q_ab = _split_pair(qs[rows, lanes])
                    k2 = keys(ks, lanes)
                    v_ab = _split_pair(keys(vs, lanes))
                    out = None
                    for ab in range(2):
                        hh = 2 * pr + ab
                        s = _dot_nt(q_ab[ab], k2) * SCALE - sl_ref[hh:hh + 1, 0:1] * dist
                        s = jnp.where(valid, s, NEG)
                        m = jnp.max(s, axis=-1, keepdims=True)
                        p = jnp.exp(s - m)
                        l = jnp.sum(p, axis=-1, keepdims=True)
                        term = _dot_nn(p.astype(BF16), v_ab[ab]) / l
                        out = term if out is None else out + term
                        lses.append(m + jnp.log(l))
                    o_dst[rows, lanes] = out
                l_dst[rows, :] = _spread_stats(lses)
                return carry

            lax.fori_loop(0, nb, block, 0, unroll=2)

        run(1, q_ref, k_ref, v_ref, o1, l1)
        _split_halves(q_ref, qf, seq)
        _split_halves(k_ref, kf, seq)
        _split_halves(v_ref, vf, seq)
        for dil, o_tok, l_tok in ((4, o2, l2), (16, o3, l3)):
            _deinterleave(qf, qd, seq, dil, BF16)
            _deinterleave(kf, kd, seq, dil, BF16)
            _deinterleave(vf, vd, seq, dil, BF16)
            run(dil, qd, kd, vd, od, ld)
            _interleave(od, o_tok, seq, dil, False)
            _interleave(ld, l_tok, seq, dil, False)

        def merge(i, carry):
            rows = _rows_of(i)
            ls = [l1[rows, :], l2[rows, :], l3[rows, :]]
            m = jnp.maximum(jnp.maximum(ls[0], ls[1]), ls[2])
            tot = m + jnp.log(jnp.exp(ls[0] - m) + jnp.exp(ls[1] - m) + jnp.exp(ls[2] - m))
            ws = [jnp.exp(x - tot) for x in ls]
            lse_ref[rows, :] = tot
            for hh in range(HEAD_GROUP):
                sl = slice(hh * HEAD_DIM, (hh + 1) * HEAD_DIM)
                st = slice(hh * STAT_LANES, hh * STAT_LANES + 1)
                hf, hl = hh // 2, slice((hh % 2) * HEAD_DIM, (hh % 2 + 1) * HEAD_DIM)
                acc = ws[0][:, st] * o1[rows, sl] + ws[1][:, st] * o2[hf, rows, hl] + ws[2][:, st] * o3[hf, rows, hl]
                mix_ref[rows, sl] = acc.astype(BF16)
            return carry

        lax.fori_loop(0, nb, merge, 0)
        if ng:
            pl.when((pl.program_id(0) == bsz - 1) & (pl.program_id(1) == N_GROUPS - 1))(finish)

    def hspec(off):
        return pl.BlockSpec((seq, 256), lambda b, g: (b, off + g))

    big = lambda dt: pltpu.VMEM((seq, 256), dt)
    halves = lambda: pltpu.VMEM((2, seq, 128), F32)
    stat = lambda: pltpu.VMEM((seq, 128), F32)
    outs = pl.pallas_call(
        body, name="band_attn_fwd_gather" if ng else "band_attn_fwd", grid=(bsz, N_GROUPS),
        in_specs=[pl.BlockSpec((None, 8, 128), lambda b, g: (g, 0, 0)), hspec(0), hspec(k_off), hspec(v_off)] + [_ANY] * ng,
        out_specs=[pl.BlockSpec((seq, 256), lambda b, g: (b, g)), pl.BlockSpec((seq, 128), lambda b, g: (b, g))] + [_ANY] * ng,
        out_shape=[_sds((bsz * seq, MIX_W), BF16), _sds((bsz * seq, STAT_W), F32)] + (_gather_shapes(gather) if ng else []),
        scratch_shapes=[halves(), halves(), halves(), big(BF16), big(BF16), big(BF16), big(F32), stat(),
                        big(F32), halves(), halves(), stat(), stat(), stat()] + (_gather_sems(ng) if ng else []),
        compiler_params=_params(("arbitrary", "arbitrary")))(slopes, h, h, h, *(gather or []))
    return outs[0], outs[1], list(outs[2:])


def _band_attn_bwd_fused(h, dcat, mix, lse, slopes, bsz, seq, exchange=None):
    width = h.shape[1]
    k_off, v_off = MIX_W // 256, 2 * MIX_W // 256
    nb = seq // BLK

    ne = 0 if exchange is None else len(exchange)

    def body(*refs):
        sl_ref, q_ref, k_ref, v_ref, do_ref, o_ref, lse_ref = refs[:7]
        dq_ref, dk_ref, dv_ref = refs[7 + ne:10 + ne]
        qf, kf, vf, dof, ddt, qd, kd, vd, dod, lsd, ddd, gq, gk, gv, aq, ak, av = refs[10 + 2 * ne:27 + 2 * ne]
        if ne:
            start, finish = _exchange_protocol(refs[7:7 + ne], refs[10 + ne:10 + 2 * ne], *refs[27 + 2 * ne:])
            pl.when((pl.program_id(0) == 0) & (pl.program_id(1) == 0))(start)

        def delta(i, carry):
            rows = _rows_of(i)
            prod = do_ref[rows, :].astype(F32) * o_ref[rows, :].astype(F32)
            ddt[rows, :] = _spread_stats(
                [jnp.sum(prod[:, hh * HEAD_DIM:(hh + 1) * HEAD_DIM], axis=-1, keepdims=True) for hh in range(HEAD_GROUP)])
            return carry

        lax.fori_loop(0, nb, delta, 0)

        def zero(i, carry):
            rows = _rows_of(i)
            for ref in (gk, gv):
                ref[rows, :] = jnp.zeros((BLK, 256), F32)
            return carry

        def run(dil, qs, ks, vs, dos, lss, dds):
            nblk = seq // dil // BLK
            if nblk > 1:
                lax.fori_loop(0, nb, zero, 0)

            def block(j, carry):
                rows, prows = _rows_of(j), _rows_of(jnp.maximum(j - 1, 0))
                valid, dist = _block_mask((j % nblk) != 0 if nblk > 1 else None, dil)

                def keys(ref, lanes):
                    if nblk == 1:
                        return ref[rows, lanes]
                    return jnp.concatenate([ref[prows, lanes], ref[rows, lanes]], axis=0)

                for pr in range(HEAD_GROUP // 2):
                    lanes = _lane_half(pr)
                    q_ab = _split_pair(qs[rows, lanes])
                    do_ab = _split_pair(dos[rows, lanes])
                    k2, v2 = keys(ks, lanes), keys(vs, lanes)
                    k_ab = _split_pair(k2)
                    dq, dk2, dv2 = None, None, None
                    for ab in range(2):
                        hh = 2 * pr + ab
                        st = slice(hh * STAT_LANES, hh * STAT_LANES + 1)
                        s = _dot_nt(q_ab[ab], k2) * SCALE - sl_ref[hh:hh + 1, 0:1] * dist
                        s = jnp.where(valid, s, NEG)
                        p = jnp.exp(s - lss[rows, st])
                        dp = _dot_nt(do_ab[ab], v2)
                        ds = (p * (dp - dds[rows, st]) * SCALE).astype(BF16)
                        terms = (_dot_nn(ds, k_ab[ab]), _dot_tn(ds, q_ab[ab]), _dot_tn(p.astype(BF16), do_ab[ab]))
                        dq, dk2, dv2 = terms if dq is None else (dq + terms[0], dk2 + terms[1], dv2 + terms[2])
                    gq[rows, lanes] = dq
                    if nblk == 1:
                        gk[rows, lanes] = dk2
                        gv[rows, lanes] = dv2
                    else:
                        gk[prows, lanes] += dk2[:BLK]
                        gv[prows, lanes] += dv2[:BLK]
                        gk[rows, lanes] += dk2[BLK:]
                        gv[rows, lanes] += dv2[BLK:]
                return carry

            lax.fori_loop(0, nb, block, 0, unroll=2)

        run(1, q_ref, k_ref, v_ref, do_ref, lse_ref, ddt)

        for src, dst in ((gq, aq), (gk, ak), (gv, av), (q_ref, qf), (k_ref, kf), (v_ref, vf), (do_ref, dof)):
            _split_halves(src, dst, seq)
        for dil in (4, 16):
            for src, dst in ((qf, qd), (kf, kd), (vf, vd), (dof, dod)):
                _deinterleave(src, dst, seq, dil, BF16)
            _deinterleave(lse_ref, lsd, seq, dil, F32)
            _deinterleave(ddt, ddd, seq, dil, F32)
            run(dil, qd, kd, vd, dod, lsd, ddd)
            for src, dst in ((gq, aq), (gk, ak), (gv, av)):
                _interleave(src, dst, seq, dil, True)

        def write(i, carry):
            rows = _rows_of(i)
            for src, dst in ((aq, dq_ref), (ak, dk_ref), (av, dv_ref)):
                for hf in range(2):
                    dst[rows, _lane_half(hf)] = src[hf, rows, :].astype(BF16)
            return carry

        lax.fori_loop(0, nb, write, 0)
        if ne:
            pl.when((pl.program_id(0) == bsz - 1) & (pl.program_id(1) == N_GROUPS - 1))(finish)

    def hspec(off):
        return pl.BlockSpec((seq, 256), lambda b, g: (b, off + g))

    io = pl.BlockSpec((seq, 256), lambda b, g: (b, g))
    big = lambda dt: pltpu.VMEM((seq, 256), dt)
    halves = lambda: pltpu.VMEM((2, seq, 128), F32)
    stat = lambda: pltpu.VMEM((seq, 128), F32)
    outs = pl.pallas_call(
        body, name="band_attn_bwd_exchange" if ne else "band_attn_bwd", grid=(bsz, N_GROUPS),
        in_specs=[pl.BlockSpec((None, 8, 128), lambda b, g: (g, 0, 0)), hspec(0), hspec(k_off), hspec(v_off), io, io,
                  pl.BlockSpec((seq, 128), lambda b, g: (b, g))] + [_ANY] * ne,
        out_specs=[io, io, io] + [_ANY] * ne,
        out_shape=[_sds((bsz * seq, MIX_W), BF16)] * 3 + [_sds(s.shape, s.dtype) for s in (exchange or [])],
        scratch_shapes=[halves(), halves(), halves(), halves(), stat(),
                        big(BF16), big(BF16), big(BF16), big(BF16), stat(), stat(),
                        big(F32), big(F32), big(F32), halves(), halves(), halves()] + (_exchange_sems(ne) if ne else []),
        compiler_params=_params(("arbitrary", "arbitrary")))(slopes, h, h, h, dcat, mix, lse, *(exchange or []))
    return list(outs[:3]), list(outs[3:])


def _mem_attn_fwd(h, mkv, bsz, seq, q_col, tq=512):
    nq = seq // tq

    def body(q_ref, kv_ref, o_ref):
        for pr in range(2):
            lanes = _lane_half(pr)
            q_ab = _split_pair(q_ref[:, lanes])
            k = kv_ref[:, lanes]
            v_ab = _split_pair(kv_ref[:, MEM_W + pr * 128:MEM_W + (pr + 1) * 128])
            out = None
            for ab in range(2):
                s = _dot_nt(q_ab[ab], k) * SCALE
                m = jnp.max(s, axis=-1, keepdims=True)
                p = jnp.exp(s - m)
                l = jnp.sum(p, axis=-1, keepdims=True)
                term = _dot_nn(p.astype(BF16), v_ab[ab]) / l
                out = term if out is None else out + term
            o_ref[:, lanes] = out.astype(BF16)

    return pl.pallas_call(
        body, name="mem_attn_fwd", grid=(bsz, nq),
        in_specs=[pl.BlockSpec((tq, MEM_W), lambda b, i: (b * nq + i, q_col)),
                  pl.BlockSpec((N_MEM, 2 * MEM_W), lambda b, i: (b, 0))],
        out_specs=pl.BlockSpec((tq, MEM_W), lambda b, i: (b * nq + i, 0)),
        out_shape=_sds((bsz * seq, MEM_W), BF16), compiler_params=_params(("parallel", "parallel")))(h, mkv)


def _mem_attn_bwd(h, mkv, dcat, bsz, seq, q_col, tq=512):
    nq = seq // tq
    do_col = MIX_W // MEM_W

    def body(q_ref, kv_ref, do_ref, dq_ref, dkv_ref):
        @pl.when(pl.program_id(1) == 0)
        def _():
            dkv_ref[...] = jnp.zeros_like(dkv_ref)

        for pr in range(2):
            lanes = _lane_half(pr)
            vlanes = slice(MEM_W + pr * 128, MEM_W + (pr + 1) * 128)
            q_ab = _split_pair(q_ref[:, lanes])
            do_ab = _split_pair(do_ref[:, lanes])
            k, v = kv_ref[:, lanes], kv_ref[:, vlanes]
            k_ab = _split_pair(k)
            dq, dk, dv = None, None, None
            for ab in range(2):
                s = _dot_nt(q_ab[ab], k) * SCALE
                m = jnp.max(s, axis=-1, keepdims=True)
                e = jnp.exp(s - m)
                p = e / jnp.sum(e, axis=-1, keepdims=True)
                dp = _dot_nt(do_ab[ab], v)
                dd = jnp.sum(p * dp, axis=-1, keepdims=True)
                ds = (p * (dp - dd) * SCALE).astype(BF16)
                terms = (_dot_nn(ds, k_ab[ab]), _dot_tn(ds, q_ab[ab]), _dot_tn(p.astype(BF16), do_ab[ab]))
                dq, dk, dv = terms if dq is None else (dq + terms[0], dk + terms[1], dv + terms[2])
            dq_ref[:, lanes] = dq.astype(BF16)
            dkv_ref[:, lanes] += dk
            dkv_ref[:, vlanes] += dv

    return pl.pallas_call(
        body, name="mem_attn_bwd", grid=(bsz, nq),
        in_specs=[pl.BlockSpec((tq, MEM_W), lambda b, i: (b * nq + i, q_col)),
                  pl.BlockSpec((N_MEM, 2 * MEM_W), lambda b, i: (b, 0)),
                  pl.BlockSpec((tq, MEM_W), lambda b, i: (b * nq + i, do_col))],
        out_specs=[pl.BlockSpec((tq, MEM_W), lambda b, i: (b * nq + i, 0)),
                   pl.BlockSpec((N_MEM, 2 * MEM_W), lambda b, i: (b, 0))],
        out_shape=[_sds((bsz * seq, MEM_W), BF16), _sds((bsz * N_MEM, 2 * MEM_W), F32)],
        compiler_params=_params(("parallel", "arbitrary")))(h, mkv, dcat)


_GELU_C = math.sqrt(2.0 / math.pi)
_GELU_A = 0.044715


def _gelu(x):
    return 0.5 * x * (1.0 + jnp.tanh(_GELU_C * (x + _GELU_A * x * x * x)))


def _gelu_grad(x):
    th = jnp.tanh(_GELU_C * (x + _GELU_A * x * x * x))
    return 0.5 * (1.0 + th) + 0.5 * x * (1.0 - th * th) * (_GELU_C * (1.0 + 3.0 * _GELU_A * x * x))


def _tril_mask(lower):
    ri = lax.broadcasted_iota(jnp.int32, (BLK, BLK), 0)
    ci = lax.broadcasted_iota(jnp.int32, (BLK, BLK), 1)
    return (ri >= ci) if lower else (ci >= ri)


def _sgu_fwd(h, ws, bs_t, ln_g, ln_b, tm=512):
    t = h.shape[0]

    def body(u_ref, v_ref, ws_ref, bs_ref, g_ref, b_ref, o_ref):
        ug = _gelu(u_ref[...].astype(F32))
        vhat, _ = _ln_hat(_gelu(v_ref[...].astype(F32)))
        vn = (vhat * g_ref[...] + b_ref[...]).astype(BF16)
        mask = _tril_mask(True)
        first = lax.broadcasted_iota(jnp.int32, (1, 2 * HEAD_DIM), 1) < HEAD_DIM
        for pr in range(N_HEADS // 2):
            lanes = _lane_half(pr)
            w_ab = [jnp.where(mask, ws_ref[2 * pr + ab], 0).astype(BF16) for ab in range(2)]
            bias = jnp.where(first, bs_ref[:, 2 * pr:2 * pr + 1], bs_ref[:, 2 * pr + 1:2 * pr + 2])
            for c in range(tm // BLK):
                rs = slice(c * BLK, (c + 1) * BLK)
                v_ab = _split_pair(vn[rs, lanes])
                mixed = _dot_nn(w_ab[0], v_ab[0]) + _dot_nn(w_ab[1], v_ab[1]) + bias
                o_ref[rs, lanes] = (ug[rs, lanes] * mixed).astype(BF16)

    return pl.pallas_call(
        body, name="sgu_fwd", grid=(t // tm,),
        in_specs=[_rows(tm, MIX_W, 0), _rows(tm, MIX_W, 1), _whole(ws.shape), _whole(bs_t.shape), _whole(ln_g.shape), _whole(ln_b.shape)],
        out_specs=_rows(tm, MIX_W), out_shape=_sds((t, MIX_W), BF16),
        compiler_params=_params(("parallel",)))(h, h, ws, bs_t, ln_g, ln_b)


def _sgu_bwd(h, dcat, ws, ws_t, bs_t, ln_g, ln_b, tm=512):
    t = h.shape[0]

    def body(u_ref, v_ref, do_ref, ws_ref, wst_ref, bs_ref, g_ref, b_ref, dh_ref, dws_ref, dbs_ref, dg_ref, db_ref, dvn_ref):
        @pl.when(pl.program_id(0) == 0)
        def _():
            dws_ref[...] = jnp.zeros_like(dws_ref)
            dbs_ref[...] = jnp.zeros_like(dbs_ref)
            dg_ref[...] = jnp.zeros_like(dg_ref)
            db_ref[...] = jnp.zeros_like(db_ref)

        u = u_ref[...].astype(F32)
        v = v_ref[...].astype(F32)
        do = do_ref[...].astype(F32)
        ug = _gelu(u)
        vhat, rstd = _ln_hat(_gelu(v))
        vn = (vhat * g_ref[...] + b_ref[...]).astype(BF16)
        dmixed_f = do * ug
        dmixed = dmixed_f.astype(BF16)
        low, upp = _tril_mask(True), _tril_mask(False)
        first = lax.broadcasted_iota(jnp.int32, (1, 2 * HEAD_DIM), 1) < HEAD_DIM
        for pr in range(N_HEADS // 2):
            lanes = _lane_half(pr)
            w_ab = [jnp.where(low, ws_ref[2 * pr + ab], 0).astype(BF16) for ab in range(2)]
            wt_ab = [jnp.where(upp, wst_ref[2 * pr + ab], 0).astype(BF16) for ab in range(2)]
            bias = jnp.where(first, bs_ref[:, 2 * pr:2 * pr + 1], bs_ref[:, 2 * pr + 1:2 * pr + 2])
            dws_acc = [None, None]
            dbs_acc = [None, None]
            for c in range(tm // BLK):
                rs = slice(c * BLK, (c + 1) * BLK)
                vn_pair = vn[rs, lanes]
                v_ab = _split_pair(vn_pair)
                mixed = _dot_nn(w_ab[0], v_ab[0]) + _dot_nn(w_ab[1], v_ab[1]) + bias
                dh_ref[rs, lanes] = (do[rs, lanes] * mixed * _gelu_grad(u[rs, lanes])).astype(BF16)
                dm_ab = _split_pair(dmixed[rs, lanes])
                dmf_ab = _split_pair(dmixed_f[rs, lanes])
                for ab in range(2):
                    term = _dot_nt(dm_ab[ab], vn_pair)
                    dws_acc[ab] = term if dws_acc[ab] is None else dws_acc[ab] + term
                    rsum = jnp.sum(dmf_ab[ab], axis=-1, keepdims=True)
                    dbs_acc[ab] = rsum if dbs_acc[ab] is None else dbs_acc[ab] + rsum
                dvn_ref[rs, lanes] = _dot_nn(wt_ab[0], dm_ab[0]) + _dot_nn(wt_ab[1], dm_ab[1])
            for ab in range(2):
                g = 2 * pr + ab
                dws_ref[g] += jnp.where(low, dws_acc[ab], 0.0)
                dbs_ref[:, g:g + 1] += dbs_acc[ab]
        dvn = dvn_ref[...]
        dg_ref[...] += jnp.sum(dvn * vhat, axis=0, keepdims=True)
        db_ref[...] += jnp.sum(dvn, axis=0, keepdims=True)
        dxh = dvn * g_ref[...]
        m1 = jnp.mean(dxh, axis=-1, keepdims=True)
        m2 = jnp.mean(dxh * vhat, axis=-1, keepdims=True)
        dvg = rstd * (dxh - m1 - vhat * m2)
        dh_ref[:, MIX_W:] = (dvg * _gelu_grad(v)).astype(BF16)

    return pl.pallas_call(
        body, name="sgu_bwd", grid=(t // tm,),
        in_specs=[_rows(tm, MIX_W, 0), _rows(tm, MIX_W, 1), _rows(tm, MIX_W, 0), _whole(ws.shape), _whole(ws_t.shape),
                  _whole(bs_t.shape), _whole(ln_g.shape), _whole(ln_b.shape)],
        out_specs=[_rows(tm, 2 * MIX_W), _whole(ws.shape), _whole(bs_t.shape), _whole((1, MIX_W)), _whole((1, MIX_W))],
        out_shape=[_sds((t, 2 * MIX_W), BF16), _sds(ws.shape, F32), _sds(bs_t.shape, F32), _sds((1, MIX_W), F32), _sds((1, MIX_W), F32)],
        scratch_shapes=[pltpu.VMEM((tm, MIX_W), F32)],
        compiler_params=_params(("arbitrary",)))(h, h, dcat, ws, ws_t, bs_t, ln_g, ln_b)


def _row_tile(rows, cols, itemsize=4, limit=2 ** 20):
    best = rows
    for cand in (4096, 2048, 1024, 512, 256, 128, 64, 32, 16):
        if rows % cand == 0 and rows > cand:
            best = cand
            if cand * cols * itemsize <= limit:
                break
    return best


def _adamw(w, m, v, grad=None, parts=None, first_parts=None):
    rows, cols = w.shape
    rows0 = 0 if first_parts is None else first_parts.shape[1]
    tr = _row_tile(rows0 if rows0 else rows, cols)
    n0 = rows0 // tr

    def chip_sum(ref):
        acc = ref[0].astype(F32)
        for k in range(1, 4):
            acc = acc + ref[k].astype(F32)
        return acc

    def body(*refs):
        w_ref, m_ref, v_ref = refs[:3]
        go_ref, d_ref, nm_ref, nv_ref = refs[-4:]
        if parts is None:
            gv = refs[3][...]
        elif first_parts is None:
            gv = chip_sum(refs[3])
        else:
            gv = jnp.where(pl.program_id(0) < n0, chip_sum(refs[3]), chip_sum(refs[4]))
        nm = ADAM_B1 * m_ref[...] + (1.0 - ADAM_B1) * gv
        nv = ADAM_B2 * v_ref[...] + (1.0 - ADAM_B2) * (gv * gv)
        m_hat = nm / (1.0 - ADAM_B1 ** ADAM_STEP)
        v_hat = nv / (1.0 - ADAM_B2 ** ADAM_STEP)
        go_ref[...] = gv
        d_ref[...] = -ADAM_LR * (m_hat / (jnp.sqrt(v_hat) + ADAM_EPS) + ADAM_WD * w_ref[...])
        nm_ref[...] = nm
        nv_ref[...] = nv

    spec = _rows(tr, cols)
    if parts is None:
        g_specs, g_args = [spec], [grad]
    elif first_parts is None:
        g_specs, g_args = [pl.BlockSpec((4, tr, cols), lambda i: (0, i, 0))], [parts]
    else:
        g_specs = [pl.BlockSpec((4, tr, cols), lambda i: (0, jnp.minimum(i, n0 - 1), 0)),
                   pl.BlockSpec((4, tr, cols), lambda i: (0, jnp.maximum(i - n0, 0), 0))]
        g_args = [first_parts, parts]
    return pl.pallas_call(
        body, name="adamw" if parts is None else "adamw_sum_chips", grid=(rows // tr,), in_specs=[spec] * 3 + g_specs,
        out_specs=[spec] * 4, out_shape=[_sds(w.shape, F32)] * 4,
        compiler_params=_params(("parallel",)))(w, m, v, *g_args)


_ANY = pl.BlockSpec(memory_space=pl.ANY)
_MESH = pl.DeviceIdType.MESH


def _all_gather(name, blocks):
    nt = len(blocks)

    def body(*refs):
        start, finish = _gather_protocol(refs[:nt], refs[nt:2 * nt], *refs[2 * nt:])
        start()
        finish()

    return pl.pallas_call(
        body, name=name, out_shape=_gather_shapes(blocks), in_specs=[_ANY] * nt, out_specs=[_ANY] * nt,
        scratch_shapes=_gather_sems(nt))(*blocks)


def _gather_shapes(blocks):
    return [_sds((N_DEV,) + b.shape, b.dtype) for b in blocks]


def _gather_sems(nt):
    return [pltpu.SemaphoreType.DMA((nt, 7)), pltpu.SemaphoreType.DMA((nt, 7)), pltpu.SemaphoreType.DMA((nt,))]


def _gather_protocol(x_refs, out_refs, send_sems, recv_sems, local_sems):
    nt = len(x_refs)
    x, y, c = lax.axis_index("x"), lax.axis_index("y"), lax.axis_index("c")
    me, sibling = (x, y, c), (x, y, 1 - c)
    chips = [(1 - x, y), (x, 1 - y), (1 - x, 1 - y)]

    def slot(t, px, py, pc):
        return out_refs[t].at[4 * px + 2 * py + pc]

    def copy(t, k, blk, to, src=None):
        return pltpu.make_async_remote_copy(
            src_ref=slot(t, *blk) if src is None else src, dst_ref=slot(t, *blk),
            send_sem=send_sems.at[t, k], recv_sem=recv_sems.at[t, k], device_id=to, device_id_type=_MESH)

    def own_copies():
        mine = [pltpu.make_async_copy(x_refs[t], slot(t, *me), local_sems.at[t]) for t in range(nt)]
        first = []
        for t in range(nt):
            first.append(copy(t, 0, me, sibling, src=x_refs[t]))
            first += [copy(t, 1 + j, me, (*chip, c), src=x_refs[t]) for j, chip in enumerate(chips)]
        return mine, first

    def start():
        mine, first = own_copies()
        for cp in mine + first:
            cp.start()

    def finish():
        mine, first = own_copies()
        passed = []
        for j, chip in enumerate(chips):
            for t in range(nt):
                copy(t, 1 + j, (*chip, c), me).wait_recv()
                fwd = copy(t, 4 + j, (*chip, c), sibling)
                fwd.start()
                passed.append(fwd)
        for t in range(nt):
            copy(t, 0, sibling, me).wait_recv()
        for j, chip in enumerate(chips):
            for t in range(nt):
                copy(t, 4 + j, (*chip, 1 - c), me).wait_recv()
        for cp in first + passed:
            cp.wait_send()
        for cp in mine:
            cp.wait()

    return start, finish


def _swap_with_sibling(packed):
    nt = len(packed)

    def body(*refs):
        p_refs, got_refs = refs[:nt], refs[nt:2 * nt]
        send_sems, recv_sems = refs[2 * nt:]
        x, y, c = lax.axis_index("x"), lax.axis_index("y"), lax.axis_index("c")
        copies = [
            pltpu.make_async_remote_copy(
                src_ref=p_refs[t].at[1 - c], dst_ref=got_refs[t], send_sem=send_sems.at[t], recv_sem=recv_sems.at[t],
                device_id=(x, y, 1 - c), device_id_type=_MESH)
            for t in range(nt)]
        for cp in copies:
            cp.start()
        for cp in copies:
            cp.wait_recv()
        for cp in copies:
            cp.wait_send()

    return pl.pallas_call(
        body, name="grad_swap_sibling", out_shape=[_sds(p.shape[1:], p.dtype) for p in packed], in_specs=[_ANY] * nt,
        out_specs=[_ANY] * nt,
        scratch_shapes=[pltpu.SemaphoreType.DMA((nt,)), pltpu.SemaphoreType.DMA((nt,))])(*packed)


def _chip_sum(packed, got):
    _, nchip, rows, cols = packed.shape
    tr = _row_tile(rows, cols, 2)
    core = lax.axis_index("c").astype(jnp.int32).reshape(1)

    def body(c_ref, p_ref, g_ref, o_ref):
        o_ref[...] = (p_ref[...].astype(F32) + g_ref[...].astype(F32)).astype(o_ref.dtype)

    grid_spec = pltpu.PrefetchScalarGridSpec(
        num_scalar_prefetch=1, grid=(nchip, rows // tr),
        in_specs=[pl.BlockSpec((None, None, tr, cols), lambda k, i, c: (c[0], k, i, 0)),
                  pl.BlockSpec((None, tr, cols), lambda k, i, c: (k, i, 0))],
        out_specs=pl.BlockSpec((None, tr, cols), lambda k, i, c: (k, i, 0)))
    return pl.pallas_call(
        body, name="grad_chip_sum", grid_spec=grid_spec, out_shape=_sds(got.shape, got.dtype),
        compiler_params=_params(("parallel", "parallel")))(core, packed, got)


def _exchange_chips(chip_sums):
    nt = len(chip_sums)

    def body(*refs):
        start, finish = _exchange_protocol(refs[:nt], refs[nt:2 * nt], *refs[2 * nt:])
        start()
        finish()

    return pl.pallas_call(
        body, name="grad_exchange_chips", out_shape=[_sds(s.shape, s.dtype) for s in chip_sums], in_specs=[_ANY] * nt,
        out_specs=[_ANY] * nt, scratch_shapes=_exchange_sems(nt))(*chip_sums)


def _exchange_sems(nt):
    return [pltpu.SemaphoreType.DMA((nt, 3)), pltpu.SemaphoreType.DMA((nt, 3)), pltpu.SemaphoreType.DMA((nt,))]


def _exchange_protocol(s_refs, got_refs, send_sems, recv_sems, local_sems):
    nt = len(s_refs)
    x, y, c = lax.axis_index("x"), lax.axis_index("y"), lax.axis_index("c")
    my_chip = 2 * x + y
    chips = [(1 - x, y), (x, 1 - y), (1 - x, 1 - y)]

    def copy(t, j, src_chip, dst_chip):
        px, py = chips[j]
        return pltpu.make_async_remote_copy(
            src_ref=s_refs[t].at[src_chip], dst_ref=got_refs[t].at[dst_chip], send_sem=send_sems.at[t, j],
            recv_sem=recv_sems.at[t, j], device_id=(px, py, c), device_id_type=_MESH)

    def own_copies():
        mine = [pltpu.make_async_copy(s_refs[t].at[my_chip], got_refs[t].at[my_chip], local_sems.at[t]) for t in range(nt)]
        sends = [copy(t, j, 2 * px + py, my_chip) for t in range(nt) for j, (px, py) in enumerate(chips)]
        return mine, sends

    def start():
        mine, sends = own_copies()
        for cp in mine + sends:
            cp.start()

    def finish():
        mine, sends = own_copies()
        for j, (px, py) in enumerate(chips):
            for t in range(nt):
                copy(t, j, my_chip, 2 * px + py).wait_recv()
        for cp in sends:
            cp.wait_send()
        for cp in mine:
            cp.wait()

    return start, finish


def _sum_chips(got):
    _, rows, cols = got.shape
    tr = _row_tile(rows, cols)

    def body(g_ref, o_ref):
        acc = g_ref[0].astype(F32)
        for k in range(1, 4):
            acc = acc + g_ref[k].astype(F32)
        o_ref[...] = acc

    return pl.pallas_call(
        body, name="grad_sum_chips", grid=(rows // tr,), in_specs=[pl.BlockSpec((4, tr, cols), lambda i: (0, i, 0))],
        out_specs=pl.BlockSpec((tr, cols), lambda i: (i, 0)), out_shape=_sds((rows, cols), F32),
        compiler_params=_params(("parallel",)))(got)


_COL_SHARDED = ("a_w_in", "b_w_in", "w_gate", "w_up")
_ROW_SHARDED = ("w_mem_kv", "w_out", "w_down")
_BIG = ("a_w_in", "b_w_in", "w_mem_kv", "w_out", "w_gate", "w_up", "w_down")
_SGU_LN = ("sgu_ln_g", "sgu_ln_b")
_LN4 = ("ln_mix_g", "ln_mix_b", "ln_ffn_g", "ln_ffn_b")
_REPLICATED = ("sgu_w_s", "sgu_b_s") + _LN4


def _unshard(name, gathered):
    if name in _COL_SHARDED or name in _SGU_LN:
        moved = jnp.moveaxis(gathered, 0, -2)
        return moved.reshape(moved.shape[:-2] + (moved.shape[-2] * moved.shape[-1],))
    moved = jnp.moveaxis(gathered, 0, 1)
    return moved.reshape((moved.shape[0], moved.shape[1] * moved.shape[2]) + moved.shape[3:])


def _by_shard(name, full):
    if name in _COL_SHARDED or name in _SGU_LN:
        split = full.reshape(full.shape[:-1] + (N_DEV, full.shape[-1] // N_DEV))
        return jnp.moveaxis(split, -2, 0)
    split = full.reshape((full.shape[0], N_DEV, full.shape[1] // N_DEV) + full.shape[2:])
    return jnp.moveaxis(split, 1, 0)


_FIRST = tuple(n for n in _BIG if n != "b_w_in")


def _first_slice(name, a):
    return None if name == "b_w_in" else a[:1]


def _rest_slice(name, a):
    return a if name == "b_w_in" else a[1:]


def _gather_first(shards):
    blocks = [shards[n][:1].astype(BF16) for n in _FIRST] + [shards[n] for n in _SGU_LN]
    gathered = _all_gather("first_layer_all_gather", blocks)
    names = _FIRST + _SGU_LN
    full = {n: _unshard(n, g) for n, g in zip(names, gathered)}
    return {n: full[n] for n in _FIRST}, {n: full[n] for n in _SGU_LN}


def _rest_blocks(shards):
    return [_rest_slice(n, shards[n]).astype(BF16) for n in _BIG]


def _two_level(by_dest):
    shp = by_dest.shape[1:]
    split = by_dest.astype(BF16).reshape((4, 2) + shp).swapaxes(0, 1)
    return split.reshape(2, 4, int(np.prod(shp[:-1])), shp[-1])


_EARLY = _BIG + _SGU_LN


def _chip_sums_of_early(grads):
    packed = [_two_level(_by_shard(n, jnp.stack(grads[n][1:] if n == "a_w_in" else grads[n]))) for n in _EARLY]
    ln4 = jnp.stack([jnp.stack(grads[n]) for n in _LN4])
    rep = [jnp.stack(grads["sgu_w_s"]).reshape(N_DEV, -1, BLK), jnp.stack(grads["sgu_b_s"]).reshape(N_DEV, -1, BLK),
           ln4.reshape(N_DEV, -1, D_MODEL)]
    packed += [_two_level(r) for r in rep]
    got = _swap_with_sibling(packed)
    return [_chip_sum(p, g) for p, g in zip(packed, got)]


def _finish_replicated(parts, shapes):
    w_s, b_s, ln_all = _all_gather("replicated_grads_all_gather", [_sum_chips(p) for p in parts])
    ln_all = ln_all.reshape((len(_LN4),) + tuple(shapes[_LN4[0]]))
    rep_grads = {"sgu_w_s": w_s.reshape(shapes["sgu_w_s"]), "sgu_b_s": b_s.reshape(shapes["sgu_b_s"])}
    rep_grads.update({n: ln_all[i] for i, n in enumerate(_LN4)})
    return rep_grads


def _reduce_last(grad_a_first):
    packed = [_two_level(_by_shard("a_w_in", grad_a_first))]
    got = _swap_with_sibling(packed)
    return _exchange_chips([_chip_sum(packed[0], got[0])])[0]


def _as_2d(a):
    if a.ndim == 1:
        return a.reshape(1, -1)
    return a.reshape(-1, a.shape[-1])


def kernel(x, mem, a_w_in, b_w_in, sgu_ln_g, sgu_ln_b, sgu_w_s, sgu_b_s, w_mem_kv, w_out, ln_mix_g, ln_mix_b, w_gate, w_up, w_down, ln_ffn_g, ln_ffn_b, loss_target, m_a_w_in, m_b_w_in, m_sgu_ln_g, m_sgu_ln_b, m_sgu_w_s, m_sgu_b_s, m_w_mem_kv, m_w_out, m_ln_mix_g, m_ln_mix_b, m_w_gate, m_w_up, m_w_down, m_ln_ffn_g, m_ln_ffn_b, v_a_w_in, v_b_w_in, v_sgu_ln_g, v_sgu_ln_b, v_sgu_w_s, v_sgu_b_s, v_w_mem_kv, v_w_out, v_ln_mix_g, v_ln_mix_b, v_w_gate, v_w_up, v_w_down, v_ln_ffn_g, v_ln_ffn_b):
    names = ("a_w_in", "b_w_in", "sgu_ln_g", "sgu_ln_b", "sgu_w_s", "sgu_b_s", "w_mem_kv", "w_out", "ln_mix_g", "ln_mix_b",
             "w_gate", "w_up", "w_down", "ln_ffn_g", "ln_ffn_b")
    weights = dict(zip(names, (a_w_in, b_w_in, sgu_ln_g, sgu_ln_b, sgu_w_s, sgu_b_s, w_mem_kv, w_out, ln_mix_g, ln_mix_b,
                               w_gate, w_up, w_down, ln_ffn_g, ln_ffn_b)))
    mom_m = dict(zip(names, (m_a_w_in, m_b_w_in, m_sgu_ln_g, m_sgu_ln_b, m_sgu_w_s, m_sgu_b_s, m_w_mem_kv, m_w_out, m_ln_mix_g,
                             m_ln_mix_b, m_w_gate, m_w_up, m_w_down, m_ln_ffn_g, m_ln_ffn_b)))
    mom_v = dict(zip(names, (v_a_w_in, v_b_w_in, v_sgu_ln_g, v_sgu_ln_b, v_sgu_w_s, v_sgu_b_s, v_w_mem_kv, v_w_out, v_ln_mix_g,
                             v_ln_mix_b, v_w_gate, v_w_up, v_w_down, v_ln_ffn_g, v_ln_ffn_b)))
    full_first, sgu_ln = _gather_first(weights)
    loss_part, grad_x, local, early = _local_step(
        x, mem, loss_target, full_first, sgu_ln, {n: weights[n] for n in _REPLICATED}, _rest_blocks(weights), True)
    loss = lax.psum(loss_part[0, 0], ("x", "y", "c"))
    early_parts = dict(zip(_EARLY, early))
    rep_grads = _finish_replicated(early[len(_EARLY):], {n: weights[n].shape for n in _REPLICATED})
    a_first_parts = _reduce_last(local["a_w_in"][:1])

    reduced, deltas, new_m, new_v = {}, {}, {}, {}
    for n in names:
        w2, m2, v2 = _as_2d(weights[n]), _as_2d(mom_m[n]), _as_2d(mom_v[n])
        if n in early_parts:
            outs = _adamw(w2, m2, v2, parts=early_parts[n], first_parts=a_first_parts if n == "a_w_in" else None)
        else:
            outs = _adamw(w2, m2, v2, grad=_as_2d(rep_grads[n]))
        reduced[n], deltas[n], new_m[n], new_v[n] = (a.reshape(weights[n].shape) for a in outs)

    return (loss, grad_x, *[reduced[n] for n in names], *[deltas[n] for n in names],
            *[new_m[n] for n in names], *[new_v[n] for n in names])


def _local_step(x, mem, loss_target, full_first, sgu_ln, small, rest, rest_sharded):
    sgu_w_s, sgu_b_s = small["sgu_w_s"], small["sgu_b_s"]
    ln_mix_g, ln_mix_b, ln_ffn_g, ln_ffn_b = (small[n] for n in ("ln_mix_g", "ln_mix_b", "ln_ffn_g", "ln_ffn_b"))
    bsz, seq, _ = x.shape
    tokens = bsz * seq
    slopes = _alibi_table()
    later = {} if rest_sharded else dict(zip(_BIG, rest))

    def weight(name, idx):
        if name != "b_w_in" and idx == 0:
            return full_first[name][0]
        return later[name][idx if name == "b_w_in" else idx - 1]

    res = (x.reshape(tokens, D_MODEL),)
    xb = res[0].astype(BF16)
    memb = mem.reshape(bsz * N_MEM, D_MODEL).astype(BF16)
    tgt = loss_target.reshape(tokens, D_MODEL)

    saved = []
    for i in range(DEPTH):
        j = i // 2
        dil_layer = i % 2 == 0
        w_in = weight("a_w_in" if dil_layer else "b_w_in", j)
        mkv = _linear_nn("mem_kv", memb, weight("w_mem_kv", i))
        h = _linear_nn("in_proj_a" if dil_layer else "in_proj_b", xb, w_in)
        st = dict(xb=xb, h=h, mkv=mkv, w_in=w_in)
        if dil_layer:
            carry = rest if (i == 0 and rest_sharded) else None
            mix, st["lse"], gathered = _band_attn_fwd_fused(h, slopes, bsz, seq, gather=carry)
            if carry is not None:
                later = {n: _unshard(n, g) for n, g in zip(_BIG, gathered)}
            q_col = 3 * MIX_W // MEM_W
        else:
            st["ws"] = sgu_w_s[j]
            st["bs_t"] = sgu_b_s[j].T
            st["ln_g"] = sgu_ln["sgu_ln_g"][j].reshape(1, MIX_W)
            st["ln_b"] = sgu_ln["sgu_ln_b"][j].reshape(1, MIX_W)
            mix = _sgu_fwd(h, st["ws"], st["bs_t"], st["ln_g"], st["ln_b"])
            q_col = 2 * MIX_W // MEM_W
        mo = _mem_attn_fwd(h, mkv, bsz, seq, q_col)
        w_out, w_down = weight("w_out", i), weight("w_down", i)
        w_gu = jnp.concatenate([weight("w_gate", i), weight("w_up", i)], axis=-1)
        mix_ln = (ln_mix_g[i].reshape(1, D_MODEL), ln_mix_b[i].reshape(1, D_MODEL))
        ffn_ln = (ln_ffn_g[i].reshape(1, D_MODEL), ln_ffn_b[i].reshape(1, D_MODEL))
        r1, x1b = _proj_ln_fwd("out_proj_ln", [mix, mo], w_out, res, *mix_ln)
        gt, up, act = _ffn_up_fwd(x1b, w_gu)
        r2, xb = _proj_ln_fwd("ffn_down_ln", [act], w_down, (r1, *mix_ln), *ffn_ln)
        res = (r2, *ffn_ln)
        st.update(mix=mix, mo=mo, q_col=q_col, r1=r1, x1b=x1b, gt=gt, up=up, act=act, r2=r2,
                  w_out=w_out, w_down=w_down, w_gu=w_gu)
        saved.append(st)

    dr2, dr2b, dg, db, loss_part = _loss_ln_bwd(*res, tgt)

    early_parts = None
    per_pair = ("a_w_in", "b_w_in", "sgu_ln_g", "sgu_ln_b", "sgu_w_s", "sgu_b_s")
    grads = {n: [None] * (DEPTH // 2 if n in per_pair else DEPTH) for n in _BIG + _SGU_LN + _REPLICATED}
    for i in reversed(range(DEPTH)):
        j = i // 2
        st = saved[i]
        dil_layer = i % 2 == 0
        w_in = st["w_in"]
        grads["ln_ffn_g"][i], grads["ln_ffn_b"][i] = dg[0], db[0]
        dgu = _ffn_down_bwd(dr2b, st["w_down"], st["gt"], st["up"])
        grads["w_down"][i] = _mm_tn("grad_w_down", st["act"], dr2b)
        dr1, dr1b, dg, db = _linear_nt("ffn_up_bwd", [dgu], st["w_gu"], dr2, F32,
                                       ln=(st["r1"], ln_mix_g[i].reshape(1, D_MODEL)))
        grads["ln_mix_g"][i], grads["ln_mix_b"][i] = dg[0], db[0]
        dw_gu = _mm_tn("grad_w_gate_up", st["x1b"], dgu)
        grads["w_gate"][i], grads["w_up"][i] = dw_gu[:, :D_FF], dw_gu[:, D_FF:]
        dcat = _linear_nt("out_proj_bwd", [dr1b], st["w_out"], None, BF16)
        grads["w_out"][i] = jnp.concatenate(
            [_mm_tn("grad_w_out_mix", st["mix"], dr1b), _mm_tn("grad_w_out_mem", st["mo"], dr1b)], axis=0)
        dqm, dmkv = _mem_attn_bwd(st["h"], st["mkv"], dcat, bsz, seq, st["q_col"])
        grads["w_mem_kv"][i] = _mm_tn("grad_w_mem_kv", memb, dmkv.astype(BF16))
        if dil_layer:
            early_sums = _chip_sums_of_early(grads) if (i == 0 and rest_sharded) else None
            dh_parts, exchanged = _band_attn_bwd_fused(st["h"], dcat, st["mix"], st["lse"], slopes, bsz, seq,
                                                       exchange=early_sums)
            if early_sums is not None:
                early_parts = exchanged
        else:
            ws_t = jnp.swapaxes(st["ws"], -1, -2)
            dh_main, dws, dbs_t, dlg, dlb = _sgu_bwd(st["h"], dcat, st["ws"], ws_t, st["bs_t"], st["ln_g"], st["ln_b"])
            grads["sgu_w_s"][j], grads["sgu_b_s"][j] = dws, dbs_t.T
            grads["sgu_ln_g"][j], grads["sgu_ln_b"][j] = dlg[0], dlb[0]
            dh_parts = [dh_main]
        name = "in_proj_bwd_a" if dil_layer else "in_proj_bwd_b"
        if i > 0:
            dr2, dr2b, dg, db = _linear_nt(name, [*dh_parts, dqm], w_in, dr1, F32,
                                           ln=(saved[i - 1]["r2"], ln_ffn_g[i - 1].reshape(1, D_MODEL)))
        else:
            grad_x = _linear_nt(name + "_x", [*dh_parts, dqm], w_in, dr1, F32).reshape(x.shape)
        grads["a_w_in" if dil_layer else "b_w_in"][j] = jnp.concatenate(
            [_mm_tn("grad_w_in_part", st["xb"], part) for part in dh_parts] + [_mm_tn("grad_w_in_qm", st["xb"], dqm)], axis=1)
    return loss_part, grad_x, {n: jnp.stack(g) for n, g in grads.items()}, early_parts
```

```python
import functools
import math

import numpy as np
import jax
import jax.numpy as jnp
from jax import lax
from jax.experimental import pallas as pl
from jax.experimental.pallas import tpu as pltpu

F32 = jnp.float32
BF16 = jnp.bfloat16

D_MODEL = 1024
DEPTH = 4
N_MEM = 256
HEAD_DIM = 64
N_HEADS = 12
MIX_W = N_HEADS * HEAD_DIM
MEM_W = 4 * HEAD_DIM
DIL_PATTERNS = ((128, 1), (512, 4), (2048, 16))
BLK = 128
HEAD_GROUP = 4
N_GROUPS = N_HEADS // HEAD_GROUP
D_FF = 2816
ALPHA = (2 * DEPTH) ** 0.25
LN_EPS = 1e-5
SCALE = HEAD_DIM ** -0.5
NEG = -1e30
N_DEV = 8

ADAM_LR, ADAM_B1, ADAM_B2, ADAM_EPS, ADAM_WD, ADAM_STEP = 0.001, 0.9, 0.999, 1e-08, 0.01, 10

VMEM_LIMIT = 56 * 2 ** 20
STAT_LANES = 32
STAT_W = N_HEADS * STAT_LANES


def _dot_nn(a, b):
    return lax.dot_general(a, b, (((1,), (0,)), ((), ())), preferred_element_type=F32)


def _dot_nt(a, b):
    return lax.dot_general(a, b, (((1,), (1,)), ((), ())), preferred_element_type=F32)


def _dot_tn(a, b):
    return lax.dot_general(a, b, (((0,), (0,)), ((), ())), preferred_element_type=F32)


def _ln_hat(r):
    mu = jnp.mean(r, axis=-1, keepdims=True)
    xc = r - mu
    var = jnp.mean(xc * xc, axis=-1, keepdims=True)
    rstd = lax.rsqrt(var + LN_EPS)
    return xc * rstd, rstd


def _params(sem):
    return pltpu.CompilerParams(dimension_semantics=sem, vmem_limit_bytes=VMEM_LIMIT)


def _rows(tm, c, col=0):
    return pl.BlockSpec((tm, c), lambda i: (i, col))


def _whole(shape):
    nd = len(shape)
    return pl.BlockSpec(tuple(shape), lambda *_: (0,) * nd)


def _resident(shape):
    nd = len(shape)
    return pl.BlockSpec(tuple(shape), lambda *_: (0,) * nd, pipeline_mode=pl.Buffered(1))


def _sds(shape, dtype):
    return jax.ShapeDtypeStruct(tuple(shape), dtype)


def _linear_nn(name, a, w, tm=512):
    t, k = a.shape
    n = w.shape[1]
    tm = min(tm, t)

    def body(a_ref, w_ref, o_ref):
        o_ref[...] = _dot_nn(a_ref[...], w_ref[...]).astype(BF16)

    return pl.pallas_call(
        body, name=name, grid=(t // tm,), in_specs=[_rows(tm, k), _resident(w.shape)], out_specs=_rows(tm, n),
        out_shape=_sds((t, n), BF16), compiler_params=_params(("parallel",)))(a, w)


def _proj_ln_fwd(name, lhs, w, res, g, b, tm=256):
    t = res[0].shape[0]
    n_lhs = len(lhs)
    n_res = len(res)

    def body(*refs):
        lhs_refs = refs[:n_lhs]
        w_ref = refs[n_lhs]
        res_refs = refs[n_lhs + 1:n_lhs + 1 + n_res]
        g_ref, b_ref, r_ref, xnb_ref = refs[n_lhs + 1 + n_res:]
        y, off = None, 0
        for lr in lhs_refs:
            k = lr.shape[1]
            term = _dot_nn(lr[...], w_ref[off:off + k, :])
            y = term if y is None else y + term
            off += k
        x_res = res_refs[0][...]
        if n_res == 3:
            x_res = _ln_hat(x_res)[0] * res_refs[1][...] + res_refs[2][...]
        r = ALPHA * x_res + y
        r_ref[...] = r
        xnb_ref[...] = (_ln_hat(r)[0] * g_ref[...] + b_ref[...]).astype(BF16)

    vec = _whole((1, D_MODEL))
    in_specs = ([_rows(tm, a.shape[1]) for a in lhs] + [_resident(w.shape), _rows(tm, D_MODEL)] + [vec] * (n_res - 1) + [vec, vec])
    return pl.pallas_call(
        body, name=name, grid=(t // tm,), in_specs=in_specs, out_specs=[_rows(tm, D_MODEL)] * 2,
        out_shape=[_sds((t, D_MODEL), F32), _sds((t, D_MODEL), BF16)],
        compiler_params=_params(("parallel",)))(*lhs, w, *res, g, b)


def _ffn_up_fwd(xb, wgu, tm=256):
    t = xb.shape[0]

    def body(x_ref, w_ref, g_ref, u_ref, a_ref):
        gu = _dot_nn(x_ref[...], w_ref[...])
        gt, up = gu[:, :D_FF], gu[:, D_FF:]
        g_ref[...] = gt.astype(BF16)
        u_ref[...] = up.astype(BF16)
        a_ref[...] = (gt * jax.nn.sigmoid(gt) * up).astype(BF16)

    return pl.pallas_call(
        body, name="ffn_up_fwd", grid=(t // tm,), in_specs=[_rows(tm, D_MODEL), _resident(wgu.shape)],
        out_specs=[_rows(tm, D_FF)] * 3, out_shape=[_sds((t, D_FF), BF16)] * 3,
        compiler_params=_params(("parallel",)))(xb, wgu)


def _ln_bwd_rows(dxn, xhat, rstd, g_ref, dr_ref, drb_ref, dg_ref, db_ref):
    @pl.when(pl.program_id(0) == 0)
    def _():
        dg_ref[...] = jnp.zeros_like(dg_ref)
        db_ref[...] = jnp.zeros_like(db_ref)

    dxh = dxn * g_ref[...]
    m1 = jnp.mean(dxh, axis=-1, keepdims=True)
    m2 = jnp.mean(dxh * xhat, axis=-1, keepdims=True)
    dr = rstd * (dxh - m1 - xhat * m2)
    dr_ref[...] = dr
    drb_ref[...] = dr.astype(BF16)
    dg_ref[...] += jnp.sum(dxn * xhat, axis=0, keepdims=True)
    db_ref[...] += jnp.sum(dxn, axis=0, keepdims=True)


def _ln_bwd_outs(t, tm):
    vec = _whole((1, D_MODEL))
    specs = [_rows(tm, D_MODEL), _rows(tm, D_MODEL), vec, vec]
    shapes = [_sds((t, D_MODEL), F32), _sds((t, D_MODEL), BF16), _sds((1, D_MODEL), F32), _sds((1, D_MODEL), F32)]
    return specs, shapes


def _loss_ln_bwd(r, g, b, tgt, tm=512):
    t = r.shape[0]

    def body(r_ref, g_ref, b_ref, t_ref, dr_ref, drb_ref, dg_ref, db_ref, l_ref):
        @pl.when(pl.program_id(0) == 0)
        def _():
            l_ref[...] = jnp.zeros_like(l_ref)

        xhat, rstd = _ln_hat(r_ref[...])
        e = xhat * g_ref[...] + b_ref[...] - t_ref[...]
        l_ref[...] += jnp.sum(e * e) * (0.5 / D_MODEL)
        _ln_bwd_rows(e * (1.0 / D_MODEL), xhat, rstd, g_ref, dr_ref, drb_ref, dg_ref, db_ref)

    vec = _whole((1, D_MODEL))
    specs, shapes = _ln_bwd_outs(t, tm)
    return pl.pallas_call(
        body, name="loss_ln_bwd", grid=(t // tm,), in_specs=[_rows(tm, D_MODEL), vec, vec, _rows(tm, D_MODEL)],
        out_specs=specs + [_whole((1, 128))], out_shape=shapes + [_sds((1, 128), F32)],
        compiler_params=_params(("arbitrary",)))(r, g, b, tgt)


def _ffn_down_bwd(drb, wd, gt, up, tm=256):
    t = drb.shape[0]

    def body(d_ref, w_ref, g_ref, u_ref, o_ref):
        da = _dot_nt(d_ref[...], w_ref[...])
        g = g_ref[...].astype(F32)
        u = u_ref[...].astype(F32)
        sg = jax.nn.sigmoid(g)
        o_ref[:, :D_FF] = (da * u * (sg * (1.0 + g * (1.0 - sg)))).astype(BF16)
        o_ref[:, D_FF:] = (da * (g * sg)).astype(BF16)

    return pl.pallas_call(
        body, name="ffn_down_bwd", grid=(t // tm,),
        in_specs=[_rows(tm, D_MODEL), _resident(wd.shape), _rows(tm, D_FF), _rows(tm, D_FF)],
        out_specs=_rows(tm, 2 * D_FF), out_shape=_sds((t, 2 * D_FF), BF16),
        compiler_params=_params(("parallel",)))(drb, wd, gt, up)


def _linear_nt(name, lhs, w, res, out_dtype, ln=None, tm=256):
    t = lhs[0].shape[0]
    n_lhs = len(lhs)
    n_out = w.shape[0]
    n_in = n_lhs + 1 + (res is not None) + (2 if ln else 0)

    def body(*refs):
        lhs_refs = refs[:n_lhs]
        w_ref = refs[n_lhs]
        y, off = None, 0
        for lr in lhs_refs:
            k = lr.shape[1]
            term = _dot_nt(lr[...], w_ref[:, off:off + k])
            y = term if y is None else y + term
            off += k
        if res is not None:
            y = ALPHA * refs[n_lhs + 1][...] + y
        if ln is None:
            refs[-1][...] = y.astype(out_dtype)
        else:
            r_ref, g_ref = refs[n_in - 2:n_in]
            xhat, rstd = _ln_hat(r_ref[...])
            _ln_bwd_rows(y, xhat, rstd, g_ref, *refs[n_in:])

    in_specs = [_rows(tm, a.shape[1]) for a in lhs] + [_resident(w.shape)]
    args = list(lhs) + [w]
    if res is not None:
        in_specs.append(_rows(tm, n_out))
        args.append(res)
    if ln is None:
        out_specs, out_shape, sem = _rows(tm, n_out), _sds((t, n_out), out_dtype), "parallel"
    else:
        in_specs += [_rows(tm, D_MODEL), _whole((1, D_MODEL))]
        args += list(ln)
        (out_specs, out_shape), sem = _ln_bwd_outs(t, tm), "arbitrary"
    return pl.pallas_call(
        body, name=name, grid=(t // tm,), in_specs=in_specs, out_specs=out_specs, out_shape=out_shape,
        compiler_params=_params((sem,)))(*args)


def _pick_tile(n, limit):
    if n <= limit:
        return n
    best = 128
    for cand in range(128, limit + 1, 128):
        if n % cand == 0:
            best = cand
    return best


def _mm_tn(name, a, b, tt=1024):
    t, k = a.shape
    n = b.shape[1]
    tt = min(tt, t)
    tk = _pick_tile(k, 1408)
    tn = _pick_tile(n, (6 * 2 ** 20) // (4 * tk) // 128 * 128)
    steps = t // tt

    def body(a_ref, b_ref, o_ref, acc_ref):
        @pl.when(pl.program_id(2) == 0)
        def _():
            acc_ref[...] = jnp.zeros_like(acc_ref)

        acc_ref[...] += _dot_tn(a_ref[...], b_ref[...])

        @pl.when(pl.program_id(2) == steps - 1)
        def _():
            o_ref[...] = acc_ref[...].astype(BF16)

    return pl.pallas_call(
        body, name=name, grid=(k // tk, n // tn, steps),
        in_specs=[pl.BlockSpec((tt, tk), lambda i, j, s: (s, i)), pl.BlockSpec((tt, tn), lambda i, j, s: (s, j))],
        out_specs=pl.BlockSpec((tk, tn), lambda i, j, s: (i, j)), out_shape=_sds((k, n), BF16),
        scratch_shapes=[pltpu.VMEM((tk, tn), F32)],
        compiler_params=_params(("parallel", "parallel", "arbitrary")))(a, b)


def _alibi_table():
    arr = np.zeros((N_GROUPS, 8, 128), np.float32)
    for g in range(N_GROUPS):
        for hh in range(HEAD_GROUP):
            arr[g, hh, :] = 2.0 ** (-8.0 * (g * HEAD_GROUP + hh + 1) / N_HEADS)
    return jnp.asarray(arr)


def _band_mask(n, dil):
    qi = lax.broadcasted_iota(jnp.int32, (BLK, 2 * BLK), 0)
    ki = lax.broadcasted_iota(jnp.int32, (BLK, 2 * BLK), 1)
    steps = qi + BLK - ki
    valid = (steps >= 0) & (steps <= BLK) & ((ki >= BLK) | (n > 0))
    return valid, (steps * dil).astype(F32)


def _band_specs(bsz, seq, dil, width):
    cb = width // 256

    def spec(off, prev=False):
        if prev:
            return pl.BlockSpec((None, BLK, 256), lambda b, r, g, n: (b, jnp.maximum(n - 1, 0), r * cb + off + g))
        return pl.BlockSpec((None, BLK, 256), lambda b, r, g, n: (b, n, r * cb + off + g))

    return spec


def _spread_stats(cols, per_head=STAT_LANES):
    lane = lax.broadcasted_iota(jnp.int32, (BLK, HEAD_GROUP * per_head), 1)
    tile = cols[HEAD_GROUP - 1]
    for hh in range(HEAD_GROUP - 2, -1, -1):
        tile = jnp.where(lane < (hh + 1) * per_head, cols[hh], tile)
    return tile


def _band_attn_fwd(h, slopes, bsz, seq, dil):
    width = h.shape[1]
    length = seq // dil
    nblk = length // BLK
    hv = h.reshape(bsz, length, dil * width)
    spec = _band_specs(bsz, seq, dil, width)
    k_off, v_off = MIX_W // 256, 2 * MIX_W // 256

    def body(sl_ref, q_ref, kc_ref, kp_ref, vc_ref, vp_ref, o_ref, lse_ref):
        valid, dist = _band_mask(pl.program_id(3), dil)
        q = q_ref[...]
        k2 = jnp.concatenate([kp_ref[...], kc_ref[...]], axis=0)
        v2 = jnp.concatenate([vp_ref[...], vc_ref[...]], axis=0)
        lses = []
        for hh in range(HEAD_GROUP):
            sl = slice(hh * HEAD_DIM, (hh + 1) * HEAD_DIM)
            s = _dot_nt(q[:, sl], k2[:, sl]) * SCALE - sl_ref[hh:hh + 1, 0:1] * dist
            s = jnp.where(valid, s, NEG)
            m = jnp.max(s, axis=-1, keepdims=True)
            p = jnp.exp(s - m)
            l = jnp.sum(p, axis=-1, keepdims=True)
            acc = _dot_nn(p.astype(BF16), v2[:, sl])
            o_ref[:, sl] = (acc / l).astype(BF16)
            lses.append(m + jnp.log(l))
        lse_ref[...] = _spread_stats(lses)

    out, lse = pl.pallas_call(
        body, name=f"band_attn_fwd_d{dil}", grid=(bsz, dil, N_GROUPS, nblk),
        in_specs=[pl.BlockSpec((None, 8, 128), lambda b, r, g, n: (g, 0, 0)),
                  spec(0), spec(k_off), spec(k_off, True), spec(v_off), spec(v_off, True)],
        out_specs=[pl.BlockSpec((None, BLK, 256), lambda b, r, g, n: (b, n, r * N_GROUPS + g)),
                   pl.BlockSpec((None, BLK, 128), lambda b, r, g, n: (b, n, r * N_GROUPS + g))],
        out_shape=[_sds((bsz, length, dil * MIX_W), BF16), _sds((bsz, length, dil * STAT_W), F32)],
        compiler_params=_params(("parallel", "parallel", "parallel", "arbitrary")))(slopes, hv, hv, hv, hv, hv)
    return out.reshape(bsz * seq, MIX_W), lse.reshape(bsz * seq, STAT_W)


def _band_merge(outs, lses, tm=512):
    t = outs[0].shape[0]

    def body(o1, o2, o3, l1, l2, l3, mix_ref, lse_ref):
        ls = [l1[...], l2[...], l3[...]]
        m = jnp.maximum(jnp.maximum(ls[0], ls[1]), ls[2])
        tot = m + jnp.log(jnp.exp(ls[0] - m) + jnp.exp(ls[1] - m) + jnp.exp(ls[2] - m))
        ws = [jnp.exp(x - tot) for x in ls]
        lse_ref[...] = tot
        for hd in range(N_HEADS):
            sl = slice(hd * HEAD_DIM, (hd + 1) * HEAD_DIM)
            acc = None
            for w, o in zip(ws, (o1, o2, o3)):
                term = w[:, hd * STAT_LANES:hd * STAT_LANES + 1] * o[:, sl].astype(F32)
                acc = term if acc is None else acc + term
            mix_ref[:, sl] = acc.astype(BF16)

    return pl.pallas_call(
        body, name="band_merge", grid=(t // tm,), in_specs=[_rows(tm, MIX_W)] * 3 + [_rows(tm, STAT_W)] * 3,
        out_specs=[_rows(tm, MIX_W), _rows(tm, STAT_W)], out_shape=[_sds((t, MIX_W), BF16), _sds((t, STAT_W), F32)],
        compiler_params=_params(("parallel",)))(*outs, *lses)


def _band_delta(dcat, mix, tm=512):
    t = mix.shape[0]

    def body(d_ref, o_ref, dd_ref):
        prod = d_ref[...].astype(F32) * o_ref[...].astype(F32)
        for hd in range(N_HEADS):
            rsum = jnp.sum(prod[:, hd * HEAD_DIM:(hd + 1) * HEAD_DIM], axis=-1, keepdims=True)
            dd_ref[:, hd * STAT_LANES:(hd + 1) * STAT_LANES] = jnp.broadcast_to(rsum, (tm, STAT_LANES))

    return pl.pallas_call(
        body, name="band_delta", grid=(t // tm,), in_specs=[_rows(tm, MIX_W), _rows(tm, MIX_W)],
        out_specs=_rows(tm, STAT_W), out_shape=_sds((t, STAT_W), F32),
        compiler_params=_params(("parallel",)))(dcat, mix)


def _band_attn_bwd(h, dcat, slopes, lse, delta, bsz, seq, dil):
    width = h.shape[1]
    length = seq // dil
    nblk = length // BLK
    hv = h.reshape(bsz, length, dil * width)
    dv_ = dcat.reshape(bsz, length, dil * D_MODEL)
    k_off, v_off = MIX_W // 256, 2 * MIX_W // 256
    cb, dcb = width // 256, D_MODEL // 256

    def cur(off, c):
        return pl.BlockSpec((None, BLK, 256), lambda b, r, g, n: (b, jnp.minimum(n, nblk - 1), r * c + off + g))

    def prev(off, c):
        return pl.BlockSpec((None, BLK, 256), lambda b, r, g, n: (b, jnp.maximum(jnp.minimum(n, nblk - 1) - 1, 0), r * c + off + g))

    stat = pl.BlockSpec((None, BLK, 128), lambda b, r, g, n: (b, jnp.minimum(n, nblk - 1), r * N_GROUPS + g))
    dq_spec = pl.BlockSpec((None, BLK, 256), lambda b, r, g, n: (b, jnp.minimum(n, nblk - 1), r * N_GROUPS + g))
    dkv_spec = pl.BlockSpec((None, BLK, 256), lambda b, r, g, n: (b, jnp.maximum(n - 1, 0), r * N_GROUPS + g))

    def body(sl_ref, q_ref, kc_ref, kp_ref, vc_ref, vp_ref, do_ref, lse_ref, dd_ref, dq_ref, dk_ref, dv_ref, kcar, vcar):
        n = pl.program_id(3)

        @pl.when(n == 0)
        def _():
            kcar[...] = jnp.zeros_like(kcar)
            vcar[...] = jnp.zeros_like(vcar)

        @pl.when(n < nblk)
        def _():
            valid, dist = _band_mask(n, dil)
            q = q_ref[...]
            do = do_ref[...]
            k2 = jnp.concatenate([kp_ref[...], kc_ref[...]], axis=0)
            v2 = jnp.concatenate([vp_ref[...], vc_ref[...]], axis=0)
            for hh in range(HEAD_GROUP):
                sl = slice(hh * HEAD_DIM, (hh + 1) * HEAD_DIM)
                s = _dot_nt(q[:, sl], k2[:, sl]) * SCALE - sl_ref[hh:hh + 1, 0:1] * dist
                s = jnp.where(valid, s, NEG)
                st = slice(hh * STAT_LANES, hh * STAT_LANES + 1)
                p = jnp.exp(s - lse_ref[:, st])
                dp = _dot_nt(do[:, sl], v2[:, sl])
                ds = (p * (dp - dd_ref[:, st]) * SCALE).astype(BF16)
                dq_ref[:, sl] = _dot_nn(ds, k2[:, sl]).astype(BF16)
                dk2 = _dot_tn(ds, q[:, sl])
                dv2 = _dot_tn(p.astype(BF16), do[:, sl])
                dk_ref[:, sl] = (kcar[:, sl] + dk2[:BLK]).astype(BF16)
                dv_ref[:, sl] = (vcar[:, sl] + dv2[:BLK]).astype(BF16)
                kcar[:, sl] = dk2[BLK:]
                vcar[:, sl] = dv2[BLK:]

        @pl.when(n == nblk)
        def _():
            dk_ref[...] = kcar[...].astype(BF16)
            dv_ref[...] = vcar[...].astype(BF16)

    outs = pl.pallas_call(
        body, name=f"band_attn_bwd_d{dil}", grid=(bsz, dil, N_GROUPS, nblk + 1),
        in_specs=[pl.BlockSpec((None, 8, 128), lambda b, r, g, n: (g, 0, 0)),
                  cur(0, cb), cur(k_off, cb), prev(k_off, cb), cur(v_off, cb), prev(v_off, cb), cur(0, dcb), stat, stat],
        out_specs=[dq_spec, dkv_spec, dkv_spec],
        out_shape=[_sds((bsz, length, dil * MIX_W), BF16)] * 3,
        scratch_shapes=[pltpu.VMEM((BLK, 256), F32), pltpu.VMEM((BLK, 256), F32)],
        compiler_params=_params(("parallel", "parallel", "parallel", "arbitrary")))(
            slopes, hv, hv, hv, hv, hv, dv_, lse.reshape(bsz, length, dil * STAT_W), delta.reshape(bsz, length, dil * STAT_W))
    return [o.reshape(bsz * seq, MIX_W) for o in outs]


def _sum_patterns(parts, tm=512):
    t = parts[0][0].shape[0]

    def body(*refs):
        o_ref = refs[-1]
        for j in range(3):
            acc = refs[j][...].astype(F32) + refs[3 + j][...].astype(F32) + refs[6 + j][...].astype(F32)
            o_ref[:, j * MIX_W:(j + 1) * MIX_W] = acc.astype(BF16)

    flat = [x for p in parts for x in p]
    return pl.pallas_call(
        body, name="band_sum", grid=(t // tm,), in_specs=[_rows(tm, MIX_W)] * 9, out_specs=_rows(tm, 3 * MIX_W),
        out_shape=_sds((t, 3 * MIX_W), BF16), compiler_params=_params(("parallel",)))(*flat)


def _block_mask(has_prev, dil):
    if has_prev is None:
        steps = lax.broadcasted_iota(jnp.int32, (BLK, BLK), 0) - lax.broadcasted_iota(jnp.int32, (BLK, BLK), 1)
        return steps >= 0, (steps * dil).astype(F32)
    qi = lax.broadcasted_iota(jnp.int32, (BLK, 2 * BLK), 0)
    ki = lax.broadcasted_iota(jnp.int32, (BLK, 2 * BLK), 1)
    steps = qi + BLK - ki
    valid = (steps >= 0) & (steps <= BLK) & ((ki >= BLK) | has_prev)
    return valid, (steps * dil).astype(F32)


def _bias_scratch():
    return pltpu.VMEM((2, HEAD_GROUP, BLK, 2 * BLK), F32)


def _fill_bias(bias, sl_ref, dil):
    for p in range(2):
        valid, dist = _block_mask(p == 1, dil)
        for hh in range(HEAD_GROUP):
            bias[p, hh] = jnp.where(valid, -sl_ref[hh:hh + 1, 0:1] * dist, NEG)


def _rows_of(j):
    return pl.ds(pl.multiple_of(j * BLK, BLK), BLK)


def _lane_half(hf):
    return slice(hf * 128, (hf + 1) * 128)


def _split_pair(x):
    first = lax.broadcasted_iota(jnp.int32, (1, 2 * HEAD_DIM), 1) < HEAD_DIM
    zero = jnp.zeros_like(x)
    return jnp.where(first, x, zero), jnp.where(first, zero, x)


def _deinterleave(src, dst, seq, dil, dtype):
    length = seq // dil
    for r in range(dil):
        for c in range(length // BLK):
            rows = pl.ds(r + c * BLK * dil, BLK, stride=dil)
            out = slice(r * length + c * BLK, r * length + (c + 1) * BLK)
            if len(src.shape) == 2:
                dst[out, :] = src[rows, :].astype(dtype)
            else:
                for hf in range(2):
                    dst[out, _lane_half(hf)] = src.at[hf][rows, :].astype(dtype)


def _interleave(src, dst, seq, dil, accumulate):
    length = seq // dil
    for r in range(dil):
        for c in range(length // BLK):
            rows = pl.ds(r + c * BLK * dil, BLK, stride=dil)
            inp = slice(r * length + c * BLK, r * length + (c + 1) * BLK)
            if len(dst.shape) == 2:
                dst[rows, :] = dst[rows, :] + src[inp, :] if accumulate else src[inp, :]
            else:
                for hf in range(2):
                    val = src[inp, _lane_half(hf)]
                    half = dst.at[hf]
                    half[rows, :] = half[rows, :] + val if accumulate else val


def _split_halves(src, dst, seq):
    def step(i, carry):
        for hf in range(2):
            dst[hf, _rows_of(i), :] = src[_rows_of(i), _lane_half(hf)].astype(F32)
        return carry

    lax.fori_loop(0, seq // BLK, step, 0)


def _band_attn_fwd_fused(h, slopes, bsz, seq, gather=None):
    width = h.shape[1]
    cb = width // 256
    k_off, v_off = MIX_W // 256, 2 * MIX_W // 256
    nb = seq // BLK

    ng = 0 if gather is None else len(gather)

    def body(*refs):
        sl_ref, q_ref, k_ref, v_ref = refs[:4]
        mix_ref, lse_ref = refs[4 + ng:6 + ng]
        qf, kf, vf, qd, kd, vd, od, ld, o1, o2, o3, l1, l2, l3, bias = refs[6 + 2 * ng:21 + 2 * ng]
        if ng:
            start, finish = _gather_protocol(refs[4:4 + ng], refs[6 + ng:6 + 2 * ng], *refs[21 + 2 * ng:])
            pl.when((pl.program_id(0) == 0) & (pl.program_id(1) == 0))(start)

        def run(dil, qs, ks, vs, o_dst, l_dst):
            nblk = seq // dil // BLK
            _fill_bias(bias, sl_ref, dil)

            def block(j, carry):
                rows, prows = _rows_of(j), _rows_of(jnp.maximum(j - 1, 0))
                has_prev = ((j % nblk) != 0).astype(jnp.int32)

                def keys(ref, lanes):
                    return jnp.concatenate([ref[prows, lanes], ref[rows, lanes]], axis=0)

                lses = []
                for pr in range(HEAD_GROUP // 2):
                    lanes = _lane_half(pr)
                    q_ab = _split_pair(qs[rows, lanes] * SCALE)
                    k2 = keys(ks, lanes)
                    v_ab = _split_pair(keys(vs, lanes))
                    out = None
                    for ab in range(2):
                        hh = 2 * pr + ab
                        s = _dot_nt(q_ab[ab], k2) + bias[has_prev, hh]
                        m = jnp.max(s, axis=-1, keepdims=True)
                        p = jnp.exp(s - m)
                        l = jnp.sum(p, axis=-1, keepdims=True)
                        term = _dot_nn(p.astype(BF16), v_ab[ab]) / l
                        out = term if out is None else out + term
                        lses.append(m + jnp.log(l))
                    o_dst[rows, lanes] = out
                l_dst[rows, :] = _spread_stats(lses, HEAD_DIM)
                return carry

            lax.fori_loop(0, nb, block, 0, unroll=8)

        run(1, q_ref, k_ref, v_ref, o1, l1)
        _split_halves(q_ref, qf, seq)
        _split_halves(k_ref, kf, seq)
        _split_halves(v_ref, vf, seq)
        for dil, o_tok, l_tok in ((4, o2, l2), (16, o3, l3)):
            _deinterleave(qf, qd, seq, dil, BF16)
            _deinterleave(kf, kd, seq, dil, BF16)
            _deinterleave(vf, vd, seq, dil, BF16)
            run(dil, qd, kd, vd, od, ld)
            _interleave(od, o_tok, seq, dil, False)
            _interleave(ld, l_tok, seq, dil, False)

        def merge(i, carry):
            rows = _rows_of(i)

            def both(ref):
                return jnp.concatenate([ref[0, rows, :], ref[1, rows, :]], axis=1)

            ls = [l1[rows, :], both(l2), both(l3)]
            m = jnp.maximum(jnp.maximum(ls[0], ls[1]), ls[2])
            tot = m + jnp.log(jnp.exp(ls[0] - m) + jnp.exp(ls[1] - m) + jnp.exp(ls[2] - m))
            ws = [jnp.exp(x - tot) for x in ls]
            mix_ref[rows, :] = (ws[0] * o1[rows, :] + ws[1] * both(o2) + ws[2] * both(o3)).astype(BF16)
            lse_ref[rows, :] = _spread_stats([tot[:, hh * HEAD_DIM:hh * HEAD_DIM + 1] for hh in range(HEAD_GROUP)])
            return carry

        lax.fori_loop(0, nb, merge, 0)
        if ng:
            pl.when((pl.program_id(0) == bsz - 1) & (pl.program_id(1) == N_GROUPS - 1))(finish)

    def hspec(off):
        return pl.BlockSpec((seq, 256), lambda b, g: (b, off + g))

    big = lambda dt: pltpu.VMEM((seq, 256), dt)
    halves = lambda: pltpu.VMEM((2, seq, 128), F32)
    stat = lambda: pltpu.VMEM((seq, 128), F32)
    outs = pl.pallas_call(
        body, name="band_attn_fwd_gather" if ng else "band_attn_fwd", grid=(bsz, N_GROUPS),
        in_specs=[pl.BlockSpec((None, 8, 128), lambda b, g: (g, 0, 0)), hspec(0), hspec(k_off), hspec(v_off)] + [_ANY] * ng,
        out_specs=[pl.BlockSpec((seq, 256), lambda b, g: (b, g)), pl.BlockSpec((seq, 128), lambda b, g: (b, g))] + [_ANY] * ng,
        out_shape=[_sds((bsz * seq, MIX_W), BF16), _sds((bsz * seq, STAT_W), F32)] + (_gather_shapes(gather) if ng else []),
        scratch_shapes=[halves(), halves(), halves(), big(BF16), big(BF16), big(BF16), big(F32), big(F32),
                        big(F32), halves(), halves(), big(F32), halves(), halves(), _bias_scratch()]
        + (_gather_sems(ng) if ng else []),
        compiler_params=_params(("arbitrary", "arbitrary")))(slopes, h, h, h, *(gather or []))
    return outs[0], outs[1], list(outs[2:])


def _band_attn_bwd_fused(h, dcat, mix, lse, slopes, bsz, seq, exchange=None):
    width = h.shape[1]
    k_off, v_off = MIX_W // 256, 2 * MIX_W // 256
    nb = seq // BLK

    ne = 0 if exchange is None else len(exchange)

    def body(*refs):
        sl_ref, q_ref, k_ref, v_ref, do_ref, o_ref, lse_ref = refs[:7]
        dq_ref, dk_ref, dv_ref = refs[7 + ne:10 + ne]
        qf, kf, vf, dof, ddt, qd, kd, vd, dod, lsd, ddd, gq, gk, gv, aq, ak, av, bias = refs[10 + 2 * ne:28 + 2 * ne]
        if ne:
            start, finish = _exchange_protocol(refs[7:7 + ne], refs[10 + ne:10 + 2 * ne], *refs[28 + 2 * ne:])
            pl.when((pl.program_id(0) == 0) & (pl.program_id(1) == 0))(start)

        def delta(i, carry):
            rows = _rows_of(i)
            prod = do_ref[rows, :].astype(F32) * o_ref[rows, :].astype(F32)
            ddt[rows, :] = _spread_stats(
                [jnp.sum(prod[:, hh * HEAD_DIM:(hh + 1) * HEAD_DIM], axis=-1, keepdims=True) for hh in range(HEAD_GROUP)])
            return carry

        lax.fori_loop(0, nb, delta, 0)

        def zero(i, carry):
            rows = _rows_of(i)
            for ref in (gk, gv):
                ref[rows, :] = jnp.zeros((BLK, 256), F32)
            return carry

        def run(dil, qs, ks, vs, dos, lss, dds):
            nblk = seq // dil // BLK
            _fill_bias(bias, sl_ref, dil)
            if nblk > 1:
                lax.fori_loop(0, nb, zero, 0)

            def block(j, carry):
                rows, prows = _rows_of(j), _rows_of(jnp.maximum(j - 1, 0))
                has_prev = ((j % nblk) != 0).astype(jnp.int32)

                def keys(ref, lanes):
                    if nblk == 1:
                        return ref[rows, lanes]
                    return jnp.concatenate([ref[prows, lanes], ref[rows, lanes]], axis=0)

                for pr in range(HEAD_GROUP // 2):
                    lanes = _lane_half(pr)
                    q_ab = _split_pair(qs[rows, lanes] * SCALE)
                    do_ab = _split_pair(dos[rows, lanes])
                    k2, v2 = keys(ks, lanes), keys(vs, lanes)
                    k_ab = _split_pair(k2)
                    dq, dk2, dv2 = None, None, None
                    for ab in range(2):
                        hh = 2 * pr + ab
                        st = slice(hh * STAT_LANES, hh * STAT_LANES + 1)
                        s = _dot_nt(q_ab[ab], k2) + (bias[0, hh, :, BLK:] if nblk == 1 else bias[has_prev, hh])
                        p = jnp.exp(s - lss[rows, st])
                        dp = _dot_nt(do_ab[ab], v2)
                        ds = (p * (dp - dds[rows, st])).astype(BF16)
                        terms = (_dot_nn(ds, k_ab[ab]), _dot_tn(ds, q_ab[ab]), _dot_tn(p.astype(BF16), do_ab[ab]))
                        dq, dk2, dv2 = terms if dq is None else (dq + terms[0], dk2 + terms[1], dv2 + terms[2])
                    gq[rows, lanes] = dq * SCALE
                    if nblk == 1:
                        gk[rows, lanes] = dk2
                        gv[rows, lanes] = dv2
                    else:
                        gk[prows, lanes] += dk2[:BLK]
                        gv[prows, lanes] += dv2[:BLK]
                        gk[rows, lanes] += dk2[BLK:]
                        gv[rows, lanes] += dv2[BLK:]
                return carry

            lax.fori_loop(0, nb, block, 0, unroll=4)

        run(1, q_ref, k_ref, v_ref, do_ref, lse_ref, ddt)

        for src, dst in ((gq, aq), (gk, ak), (gv, av), (q_ref, qf), (k_ref, kf), (v_ref, vf), (do_ref, dof)):
            _split_halves(src, dst, seq)
        for dil in (4, 16):
            for src, dst in ((qf, qd), (kf, kd), (vf, vd), (dof, dod)):
                _deinterleave(src, dst, seq, dil, BF16)
            _deinterleave(lse_ref, lsd, seq, dil, F32)
            _deinterleave(ddt, ddd, seq, dil, F32)
            run(dil, qd, kd, vd, dod, lsd, ddd)
            for src, dst in ((gq, aq), (gk, ak), (gv, av)):
                _interleave(src, dst, seq, dil, True)

        def write(i, carry):
            rows = _rows_of(i)
            for src, dst in ((aq, dq_ref), (ak, dk_ref), (av, dv_ref)):
                for hf in range(2):
                    dst[rows, _lane_half(hf)] = src[hf, rows, :].astype(BF16)
            return carry

        lax.fori_loop(0, nb, write, 0)
        if ne:
            pl.when((pl.program_id(0) == bsz - 1) & (pl.program_id(1) == N_GROUPS - 1))(finish)

    def hspec(off):
        return pl.BlockSpec((seq, 256), lambda b, g: (b, off + g))

    io = pl.BlockSpec((seq, 256), lambda b, g: (b, g))
    big = lambda dt: pltpu.VMEM((seq, 256), dt)
    halves = lambda: pltpu.VMEM((2, seq, 128), F32)
    stat = lambda: pltpu.VMEM((seq, 128), F32)
    outs = pl.pallas_call(
        body, name="band_attn_bwd_exchange" if ne else "band_attn_bwd", grid=(bsz, N_GROUPS),
        in_specs=[pl.BlockSpec((None, 8, 128), lambda b, g: (g, 0, 0)), hspec(0), hspec(k_off), hspec(v_off), io, io,
                  pl.BlockSpec((seq, 128), lambda b, g: (b, g))] + [_ANY] * ne,
        out_specs=[io, io, io] + [_ANY] * ne,
        out_shape=[_sds((bsz * seq, MIX_W), BF16)] * 3 + [_sds(s.shape, s.dtype) for s in (exchange or [])],
        scratch_shapes=[halves(), halves(), halves(), halves(), stat(),
                        big(BF16), big(BF16), big(BF16), big(BF16), stat(), stat(),
                        big(F32), big(F32), big(F32), halves(), halves(), halves(), _bias_scratch()]
        + (_exchange_sems(ne) if ne else []),
        compiler_params=_params(("arbitrary", "arbitrary")))(slopes, h, h, h, dcat, mix, lse, *(exchange or []))
    return list(outs[:3]), list(outs[3:])


def _mem_attn_fwd(h, mkv, bsz, seq, q_col, tq=512):
    nq = seq // tq

    def body(q_ref, kv_ref, o_ref):
        for pr in range(2):
            lanes = _lane_half(pr)
            q_ab = _split_pair(q_ref[:, lanes])
            k = kv_ref[:, lanes]
            v_ab = _split_pair(kv_ref[:, MEM_W + pr * 128:MEM_W + (pr + 1) * 128])
            out = None
            for ab in range(2):
                s = _dot_nt(q_ab[ab], k) * SCALE
                m = jnp.max(s, axis=-1, keepdims=True)
                p = jnp.exp(s - m)
                l = jnp.sum(p, axis=-1, keepdims=True)
                term = _dot_nn(p.astype(BF16), v_ab[ab]) / l
                out = term if out is None else out + term
            o_ref[:, lanes] = out.astype(BF16)

    return pl.pallas_call(
        body, name="mem_attn_fwd", grid=(bsz, nq),
        in_specs=[pl.BlockSpec((tq, MEM_W), lambda b, i: (b * nq + i, q_col)),
                  pl.BlockSpec((N_MEM, 2 * MEM_W), lambda b, i: (b, 0))],
        out_specs=pl.BlockSpec((tq, MEM_W), lambda b, i: (b * nq + i, 0)),
        out_shape=_sds((bsz * seq, MEM_W), BF16), compiler_params=_params(("parallel", "parallel")))(h, mkv)


def _mem_attn_bwd(h, mkv, dcat, bsz, seq, q_col, tq=512):
    nq = seq // tq
    do_col = MIX_W // MEM_W

    def body(q_ref, kv_ref, do_ref, dq_ref, dkv_ref):
        @pl.when(pl.program_id(1) == 0)
        def _():
            dkv_ref[...] = jnp.zeros_like(dkv_ref)

        for pr in range(2):
            lanes = _lane_half(pr)
            vlanes = slice(MEM_W + pr * 128, MEM_W + (pr + 1) * 128)
            q_ab = _split_pair(q_ref[:, lanes])
            do_ab = _split_pair(do_ref[:, lanes])
            k, v = kv_ref[:, lanes], kv_ref[:, vlanes]
            k_ab = _split_pair(k)
            dq, dk, dv = None, None, None
            for ab in range(2):
                s = _dot_nt(q_ab[ab], k) * SCALE
                m = jnp.max(s, axis=-1, keepdims=True)
                e = jnp.exp(s - m)
                p = e / jnp.sum(e, axis=-1, keepdims=True)
                dp = _dot_nt(do_ab[ab], v)
                dd = jnp.sum(p * dp, axis=-1, keepdims=True)
                ds = (p * (dp - dd) * SCALE).astype(BF16)
                terms = (_dot_nn(ds, k_ab[ab]), _dot_tn(ds, q_ab[ab]), _dot_tn(p.astype(BF16), do_ab[ab]))
                dq, dk, dv = terms if dq is None else (dq + terms[0], dk + terms[1], dv + terms[2])
            dq_ref[:, lanes] = dq.astype(BF16)
            dkv_ref[:, lanes] += dk
            dkv_ref[:, vlanes] += dv

    return pl.pallas_call(
        body, name="mem_attn_bwd", grid=(bsz, nq),
        in_specs=[pl.BlockSpec((tq, MEM_W), lambda b, i: (b * nq + i, q_col)),
                  pl.BlockSpec((N_MEM, 2 * MEM_W), lambda b, i: (b, 0)),
                  pl.BlockSpec((tq, MEM_W), lambda b, i: (b * nq + i, do_col))],
        out_specs=[pl.BlockSpec((tq, MEM_W), lambda b, i: (b * nq + i, 0)),
                   pl.BlockSpec((N_MEM, 2 * MEM_W), lambda b, i: (b, 0))],
        out_shape=[_sds((bsz * seq, MEM_W), BF16), _sds((bsz * N_MEM, 2 * MEM_W), F32)],
        compiler_params=_params(("parallel", "arbitrary")))(h, mkv, dcat)


_GELU_C = math.sqrt(2.0 / math.pi)
_GELU_A = 0.044715


def _gelu(x):
    return 0.5 * x * (1.0 + jnp.tanh(_GELU_C * (x + _GELU_A * x * x * x)))


def _gelu_grad(x):
    th = jnp.tanh(_GELU_C * (x + _GELU_A * x * x * x))
    return 0.5 * (1.0 + th) + 0.5 * x * (1.0 - th * th) * (_GELU_C * (1.0 + 3.0 * _GELU_A * x * x))


def _tril_mask(lower):
    ri = lax.broadcasted_iota(jnp.int32, (BLK, BLK), 0)
    ci = lax.broadcasted_iota(jnp.int32, (BLK, BLK), 1)
    return (ri >= ci) if lower else (ci >= ri)


def _sgu_fwd(h, ws, bs_t, ln_g, ln_b, tm=512):
    t = h.shape[0]

    def body(u_ref, v_ref, ws_ref, bs_ref, g_ref, b_ref, o_ref):
        ug = _gelu(u_ref[...].astype(F32))
        vhat, _ = _ln_hat(_gelu(v_ref[...].astype(F32)))
        vn = (vhat * g_ref[...] + b_ref[...]).astype(BF16)
        mask = _tril_mask(True)
        first = lax.broadcasted_iota(jnp.int32, (1, 2 * HEAD_DIM), 1) < HEAD_DIM
        for pr in range(N_HEADS // 2):
            lanes = _lane_half(pr)
            w_ab = [jnp.where(mask, ws_ref[2 * pr + ab], 0).astype(BF16) for ab in range(2)]
            bias = jnp.where(first, bs_ref[:, 2 * pr:2 * pr + 1], bs_ref[:, 2 * pr + 1:2 * pr + 2])
            for c in range(tm // BLK):
                rs = slice(c * BLK, (c + 1) * BLK)
                v_ab = _split_pair(vn[rs, lanes])
                mixed = _dot_nn(w_ab[0], v_ab[0]) + _dot_nn(w_ab[1], v_ab[1]) + bias
                o_ref[rs, lanes] = (ug[rs, lanes] * mixed).astype(BF16)

    return pl.pallas_call(
        body, name="sgu_fwd", grid=(t // tm,),
        in_specs=[_rows(tm, MIX_W, 0), _rows(tm, MIX_W, 1), _whole(ws.shape), _whole(bs_t.shape), _whole(ln_g.shape), _whole(ln_b.shape)],
        out_specs=_rows(tm, MIX_W), out_shape=_sds((t, MIX_W), BF16),
        compiler_params=_params(("parallel",)))(h, h, ws, bs_t, ln_g, ln_b)


def _sgu_bwd(h, dcat, ws, ws_t, bs_t, ln_g, ln_b, tm=512):
    t = h.shape[0]

    def body(u_ref, v_ref, do_ref, ws_ref, wst_ref, bs_ref, g_ref, b_ref, dh_ref, dws_ref, dbs_ref, dg_ref, db_ref, dvn_ref):
        @pl.when(pl.program_id(0) == 0)
        def _():
            dws_ref[...] = jnp.zeros_like(dws_ref)
            dbs_ref[...] = jnp.zeros_like(dbs_ref)
            dg_ref[...] = jnp.zeros_like(dg_ref)
            db_ref[...] = jnp.zeros_like(db_ref)

        u = u_ref[...].astype(F32)
        v = v_ref[...].astype(F32)
        do = do_ref[...].astype(F32)
        ug = _gelu(u)
        vhat, rstd = _ln_hat(_gelu(v))
        vn = (vhat * g_ref[...] + b_ref[...]).astype(BF16)
        dmixed_f = do * ug
        dmixed = dmixed_f.astype(BF16)
        low, upp = _tril_mask(True), _tril_mask(False)
        first = lax.broadcasted_iota(jnp.int32, (1, 2 * HEAD_DIM), 1) < HEAD_DIM
        for pr in range(N_HEADS // 2):
            lanes = _lane_half(pr)
            w_ab = [jnp.where(low, ws_ref[2 * pr + ab], 0).astype(BF16) for ab in range(2)]
            wt_ab = [jnp.where(upp, wst_ref[2 * pr + ab], 0).astype(BF16) for ab in range(2)]
            bias = jnp.where(first, bs_ref[:, 2 * pr:2 * pr + 1], bs_ref[:, 2 * pr + 1:2 * pr + 2])
            dws_acc = [None, None]
            dbs_acc = [None, None]
            for c in range(tm // BLK):
                rs = slice(c * BLK, (c + 1) * BLK)
                vn_pair = vn[rs, lanes]
                v_ab = _split_pair(vn_pair)
                mixed = _dot_nn(w_ab[0], v_ab[0]) + _dot_nn(w_ab[1], v_ab[1]) + bias
                dh_ref[rs, lanes] = (do[rs, lanes] * mixed * _gelu_grad(u[rs, lanes])).astype(BF16)
                dm_ab = _split_pair(dmixed[rs, lanes])
                dmf_ab = _split_pair(dmixed_f[rs, lanes])
                for ab in range(2):
                    term = _dot_nt(dm_ab[ab], vn_pair)
                    dws_acc[ab] = term if dws_acc[ab] is None else dws_acc[ab] + term
                    rsum = jnp.sum(dmf_ab[ab], axis=-1, keepdims=True)
                    dbs_acc[ab] = rsum if dbs_acc[ab] is None else dbs_acc[ab] + rsum
                dvn_ref[rs, lanes] = _dot_nn(wt_ab[0], dm_ab[0]) + _dot_nn(wt_ab[1], dm_ab[1])
            for ab in range(2):
                g = 2 * pr + ab
                dws_ref[g] += jnp.where(low, dws_acc[ab], 0.0)
                dbs_ref[:, g:g + 1] += dbs_acc[ab]
        dvn = dvn_ref[...]
        dg_ref[...] += jnp.sum(dvn * vhat, axis=0, keepdims=True)
        db_ref[...] += jnp.sum(dvn, axis=0, keepdims=True)
        dxh = dvn * g_ref[...]
        m1 = jnp.mean(dxh, axis=-1, keepdims=True)
        m2 = jnp.mean(dxh * vhat, axis=-1, keepdims=True)
        dvg = rstd * (dxh - m1 - vhat * m2)
        dh_ref[:, MIX_W:] = (dvg * _gelu_grad(v)).astype(BF16)

    return pl.pallas_call(
        body, name="sgu_bwd", grid=(t // tm,),
        in_specs=[_rows(tm, MIX_W, 0), _rows(tm, MIX_W, 1), _rows(tm, MIX_W, 0), _whole(ws.shape), _whole(ws_t.shape),
                  _whole(bs_t.shape), _whole(ln_g.shape), _whole(ln_b.shape)],
        out_specs=[_rows(tm, 2 * MIX_W), _whole(ws.shape), _whole(bs_t.shape), _whole((1, MIX_W)), _whole((1, MIX_W))],
        out_shape=[_sds((t, 2 * MIX_W), BF16), _sds(ws.shape, F32), _sds(bs_t.shape, F32), _sds((1, MIX_W), F32), _sds((1, MIX_W), F32)],
        scratch_shapes=[pltpu.VMEM((tm, MIX_W), F32)],
        compiler_params=_params(("arbitrary",)))(h, h, dcat, ws, ws_t, bs_t, ln_g, ln_b)


def _row_tile(rows, cols, itemsize=4, limit=2 ** 20):
    best = rows
    for cand in (4096, 2048, 1024, 512, 256, 128, 64, 32, 16):
        if rows % cand == 0 and rows > cand:
            best = cand
            if cand * cols * itemsize <= limit:
                break
    return best


def _adamw(w, m, v, grad=None, parts=None, first_parts=None):
    rows, cols = w.shape
    rows0 = 0 if first_parts is None else first_parts.shape[1]
    tr = _row_tile(rows0 if rows0 else rows, cols)
    n0 = rows0 // tr

    def chip_sum(ref):
        acc = ref[0].astype(F32)
        for k in range(1, 4):
            acc = acc + ref[k].astype(F32)
        return acc

    def body(*refs):
        w_ref, m_ref, v_ref = refs[:3]
        go_ref, d_ref, nm_ref, nv_ref = refs[-4:]
        if parts is None:
            gv = refs[3][...]
        elif first_parts is None:
            gv = chip_sum(refs[3])
        else:
            gv = jnp.where(pl.program_id(0) < n0, chip_sum(refs[3]), chip_sum(refs[4]))
        nm = ADAM_B1 * m_ref[...] + (1.0 - ADAM_B1) * gv
        nv = ADAM_B2 * v_ref[...] + (1.0 - ADAM_B2) * (gv * gv)
        m_hat = nm / (1.0 - ADAM_B1 ** ADAM_STEP)
        v_hat = nv / (1.0 - ADAM_B2 ** ADAM_STEP)
        go_ref[...] = gv
        d_ref[...] = -ADAM_LR * (m_hat / (jnp.sqrt(v_hat) + ADAM_EPS) + ADAM_WD * w_ref[...])
        nm_ref[...] = nm
        nv_ref[...] = nv

    spec = _rows(tr, cols)
    if parts is None:
        g_specs, g_args = [spec], [grad]
    elif first_parts is None:
        g_specs, g_args = [pl.BlockSpec((4, tr, cols), lambda i: (0, i, 0))], [parts]
    else:
        g_specs = [pl.BlockSpec((4, tr, cols), lambda i: (0, jnp.minimum(i, n0 - 1), 0)),
                   pl.BlockSpec((4, tr, cols), lambda i: (0, jnp.maximum(i - n0, 0), 0))]
        g_args = [first_parts, parts]
    return pl.pallas_call(
        body, name="adamw" if parts is None else "adamw_sum_chips", grid=(rows // tr,), in_specs=[spec] * 3 + g_specs,
        out_specs=[spec] * 4, out_shape=[_sds(w.shape, F32)] * 4,
        compiler_params=_params(("parallel",)))(w, m, v, *g_args)


_ANY = pl.BlockSpec(memory_space=pl.ANY)
_MESH = pl.DeviceIdType.MESH


def _all_gather(name, blocks):
    nt = len(blocks)

    def body(*refs):
        start, finish = _gather_protocol(refs[:nt], refs[nt:2 * nt], *refs[2 * nt:])
        start()
        finish()

    return pl.pallas_call(
        body, name=name, out_shape=_gather_shapes(blocks), in_specs=[_ANY] * nt, out_specs=[_ANY] * nt,
        scratch_shapes=_gather_sems(nt))(*blocks)


def _gather_shapes(blocks):
    return [_sds((N_DEV,) + b.shape, b.dtype) for b in blocks]


def _gather_sems(nt):
    return [pltpu.SemaphoreType.DMA((nt, 7)), pltpu.SemaphoreType.DMA((nt, 7)), pltpu.SemaphoreType.DMA((nt,))]


def _gather_protocol(x_refs, out_refs, send_sems, recv_sems, local_sems):
    nt = len(x_refs)
    x, y, c = lax.axis_index("x"), lax.axis_index("y"), lax.axis_index("c")
    me, sibling = (x, y, c), (x, y, 1 - c)
    chips = [(1 - x, y), (x, 1 - y), (1 - x, 1 - y)]

    def slot(t, px, py, pc):
        return out_refs[t].at[4 * px + 2 * py + pc]

    def copy(t, k, blk, to, src=None):
        return pltpu.make_async_remote_copy(
            src_ref=slot(t, *blk) if src is None else src, dst_ref=slot(t, *blk),
            send_sem=send_sems.at[t, k], recv_sem=recv_sems.at[t, k], device_id=to, device_id_type=_MESH)

    def own_copies():
        mine = [pltpu.make_async_copy(x_refs[t], slot(t, *me), local_sems.at[t]) for t in range(nt)]
        first = []
        for t in range(nt):
            first.append(copy(t, 0, me, sibling, src=x_refs[t]))
            first += [copy(t, 1 + j, me, (*chip, c), src=x_refs[t]) for j, chip in enumerate(chips)]
        return mine, first

    def start():
        mine, first = own_copies()
        for cp in mine + first:
            cp.start()

    def finish():
        mine, first = own_copies()
        passed = []
        for j, chip in enumerate(chips):
            for t in range(nt):
                copy(t, 1 + j, (*chip, c), me).wait_recv()
                fwd = copy(t, 4 + j, (*chip, c), sibling)
                fwd.start()
                passed.append(fwd)
        for t in range(nt):
            copy(t, 0, sibling, me).wait_recv()
        for j, chip in enumerate(chips):
            for t in range(nt):
                copy(t, 4 + j, (*chip, 1 - c), me).wait_recv()
        for cp in first + passed:
            cp.wait_send()
        for cp in mine:
            cp.wait()

    return start, finish


def _swap_with_sibling(packed):
    nt = len(packed)

    def body(*refs):
        p_refs, got_refs = refs[:nt], refs[nt:2 * nt]
        send_sems, recv_sems = refs[2 * nt:]
        x, y, c = lax.axis_index("x"), lax.axis_index("y"), lax.axis_index("c")
        copies = [
            pltpu.make_async_remote_copy(
                src_ref=p_refs[t].at[1 - c], dst_ref=got_refs[t], send_sem=send_sems.at[t], recv_sem=recv_sems.at[t],
                device_id=(x, y, 1 - c), device_id_type=_MESH)
            for t in range(nt)]
        for cp in copies:
            cp.start()
        for cp in copies:
            cp.wait_recv()
        for cp in copies:
            cp.wait_send()

    return pl.pallas_call(
        body, name="grad_swap_sibling", out_shape=[_sds(p.shape[1:], p.dtype) for p in packed], in_specs=[_ANY] * nt,
        out_specs=[_ANY] * nt,
        scratch_shapes=[pltpu.SemaphoreType.DMA((nt,)), pltpu.SemaphoreType.DMA((nt,))])(*packed)


def _chip_sum(packed, got):
    _, nchip, rows, cols = packed.shape
    tr = _row_tile(rows, cols, 2)
    core = lax.axis_index("c").astype(jnp.int32).reshape(1)

    def body(c_ref, p_ref, g_ref, o_ref):
        o_ref[...] = (p_ref[...].astype(F32) + g_ref[...].astype(F32)).astype(o_ref.dtype)

    grid_spec = pltpu.PrefetchScalarGridSpec(
        num_scalar_prefetch=1, grid=(nchip, rows // tr),
        in_specs=[pl.BlockSpec((None, None, tr, cols), lambda k, i, c: (c[0], k, i, 0)),
                  pl.BlockSpec((None, tr, cols), lambda k, i, c: (k, i, 0))],
        out_specs=pl.BlockSpec((None, tr, cols), lambda k, i, c: (k, i, 0)))
    return pl.pallas_call(
        body, name="grad_chip_sum", grid_spec=grid_spec, out_shape=_sds(got.shape, got.dtype),
        compiler_params=_params(("parallel", "parallel")))(core, packed, got)


def _exchange_chips(chip_sums):
    nt = len(chip_sums)

    def body(*refs):
        start, finish = _exchange_protocol(refs[:nt], refs[nt:2 * nt], *refs[2 * nt:])
        start()
        finish()

    return pl.pallas_call(
        body, name="grad_exchange_chips", out_shape=[_sds(s.shape, s.dtype) for s in chip_sums], in_specs=[_ANY] * nt,
        out_specs=[_ANY] * nt, scratch_shapes=_exchange_sems(nt))(*chip_sums)


def _exchange_sems(nt):
    return [pltpu.SemaphoreType.DMA((nt, 3)), pltpu.SemaphoreType.DMA((nt, 3)), pltpu.SemaphoreType.DMA((nt,))]


def _exchange_protocol(s_refs, got_refs, send_sems, recv_sems, local_sems):
    nt = len(s_refs)
    x, y, c = lax.axis_index("x"), lax.axis_index("y"), lax.axis_index("c")
    my_chip = 2 * x + y
    chips = [(1 - x, y), (x, 1 - y), (1 - x, 1 - y)]

    def copy(t, j, src_chip, dst_chip):
        px, py = chips[j]
        return pltpu.make_async_remote_copy(
            src_ref=s_refs[t].at[src_chip], dst_ref=got_refs[t].at[dst_chip], send_sem=send_sems.at[t, j],
            recv_sem=recv_sems.at[t, j], device_id=(px, py, c), device_id_type=_MESH)

    def own_copies():
        mine = [pltpu.make_async_copy(s_refs[t].at[my_chip], got_refs[t].at[my_chip], local_sems.at[t]) for t in range(nt)]
        sends = [copy(t, j, 2 * px + py, my_chip) for t in range(nt) for j, (px, py) in enumerate(chips)]
        return mine, sends

    def start():
        mine, sends = own_copies()
        for cp in mine + sends:
            cp.start()

    def finish():
        mine, sends = own_copies()
        for j, (px, py) in enumerate(chips):
            for t in range(nt):
                copy(t, j, my_chip, 2 * px + py).wait_recv()
        for cp in sends:
            cp.wait_send()
        for cp in mine:
            cp.wait()

    return start, finish


def _sum_chips(got):
    _, rows, cols = got.shape
    tr = _row_tile(rows, cols)

    def body(g_ref, o_ref):
        acc = g_ref[0].astype(F32)
        for k in range(1, 4):
            acc = acc + g_ref[k].astype(F32)
        o_ref[...] = acc

    return pl.pallas_call(
        body, name="grad_sum_chips", grid=(rows // tr,), in_specs=[pl.BlockSpec((4, tr, cols), lambda i: (0, i, 0))],
        out_specs=pl.BlockSpec((tr, cols), lambda i: (i, 0)), out_shape=_sds((rows, cols), F32),
        compiler_params=_params(("parallel",)))(got)


_COL_SHARDED = ("a_w_in", "b_w_in", "w_gate", "w_up")
_ROW_SHARDED = ("w_mem_kv", "w_out", "w_down")
_BIG = ("a_w_in", "b_w_in", "w_mem_kv", "w_out", "w_gate", "w_up", "w_down")
_SGU_LN = ("sgu_ln_g", "sgu_ln_b")
_LN4 = ("ln_mix_g", "ln_mix_b", "ln_ffn_g", "ln_ffn_b")
_REPLICATED = ("sgu_w_s", "sgu_b_s") + _LN4


def _unshard(name, gathered):
    if name in _COL_SHARDED or name in _SGU_LN:
        moved = jnp.moveaxis(gathered, 0, -2)
        return moved.reshape(moved.shape[:-2] + (moved.shape[-2] * moved.shape[-1],))
    moved = jnp.moveaxis(gathered, 0, 1)
    return moved.reshape((moved.shape[0], moved.shape[1] * moved.shape[2]) + moved.shape[3:])


def _by_shard(name, full):
    if name in _COL_SHARDED or name in _SGU_LN:
        split = full.reshape(full.shape[:-1] + (N_DEV, full.shape[-1] // N_DEV))
        return jnp.moveaxis(split, -2, 0)
    split = full.reshape((full.shape[0], N_DEV, full.shape[1] // N_DEV) + full.shape[2:])
    return jnp.moveaxis(split, 1, 0)


_FIRST = tuple(n for n in _BIG if n != "b_w_in")


def _first_slice(name, a):
    return None if name == "b_w_in" else a[:1]


def _rest_slice(name, a):
    return a if name == "b_w_in" else a[1:]


def _gather_first(shards):
    blocks = [shards[n][:1].astype(BF16) for n in _FIRST] + [shards[n] for n in _SGU_LN]
    gathered = _all_gather("first_layer_all_gather", blocks)
    names = _FIRST + _SGU_LN
    full = {n: _unshard(n, g) for n, g in zip(names, gathered)}
    return {n: full[n] for n in _FIRST}, {n: full[n] for n in _SGU_LN}


def _rest_blocks(shards):
    return [_rest_slice(n, shards[n]).astype(BF16) for n in _BIG]


def _two_level(by_dest):
    shp = by_dest.shape[1:]
    split = by_dest.astype(BF16).reshape((4, 2) + shp).swapaxes(0, 1)
    return split.reshape(2, 4, int(np.prod(shp[:-1])), shp[-1])


_EARLY = _BIG + _SGU_LN


def _chip_sums_of_early(grads):
    packed = [_two_level(_by_shard(n, jnp.stack(grads[n][1:] if n == "a_w_in" else grads[n]))) for n in _EARLY]
    ln4 = jnp.stack([jnp.stack(grads[n]) for n in _LN4])
    rep = [jnp.stack(grads["sgu_w_s"]).reshape(N_DEV, -1, BLK), jnp.stack(grads["sgu_b_s"]).reshape(N_DEV, -1, BLK),
           ln4.reshape(N_DEV, -1, D_MODEL)]
    packed += [_two_level(r) for r in rep]
    got = _swap_with_sibling(packed)
    return [_chip_sum(p, g) for p, g in zip(packed, got)]


def _finish_replicated(parts, shapes):
    w_s, b_s, ln_all = _all_gather("replicated_grads_all_gather", [_sum_chips(p) for p in parts])
    ln_all = ln_all.reshape((len(_LN4),) + tuple(shapes[_LN4[0]]))
    rep_grads = {"sgu_w_s": w_s.reshape(shapes["sgu_w_s"]), "sgu_b_s": b_s.reshape(shapes["sgu_b_s"])}
    rep_grads.update({n: ln_all[i] for i, n in enumerate(_LN4)})
    return rep_grads


def _reduce_last(grad_a_first):
    packed = [_two_level(_by_shard("a_w_in", grad_a_first))]
    got = _swap_with_sibling(packed)
    return _exchange_chips([_chip_sum(packed[0], got[0])])[0]


def _as_2d(a):
    if a.ndim == 1:
        return a.reshape(1, -1)
    return a.reshape(-1, a.shape[-1])


def kernel(x, mem, a_w_in, b_w_in, sgu_ln_g, sgu_ln_b, sgu_w_s, sgu_b_s, w_mem_kv, w_out, ln_mix_g, ln_mix_b, w_gate, w_up, w_down, ln_ffn_g, ln_ffn_b, loss_target, m_a_w_in, m_b_w_in, m_sgu_ln_g, m_sgu_ln_b, m_sgu_w_s, m_sgu_b_s, m_w_mem_kv, m_w_out, m_ln_mix_g, m_ln_mix_b, m_w_gate, m_w_up, m_w_down, m_ln_ffn_g, m_ln_ffn_b, v_a_w_in, v_b_w_in, v_sgu_ln_g, v_sgu_ln_b, v_sgu_w_s, v_sgu_b_s, v_w_mem_kv, v_w_out, v_ln_mix_g, v_ln_mix_b, v_w_gate, v_w_up, v_w_down, v_ln_ffn_g, v_ln_ffn_b):
    names = ("a_w_in", "b_w_in", "sgu_ln_g", "sgu_ln_b", "sgu_w_s", "sgu_b_s", "w_mem_kv", "w_out", "ln_mix_g", "ln_mix_b",
             "w_gate", "w_up", "w_down", "ln_ffn_g", "ln_ffn_b")
    weights = dict(zip(names, (a_w_in, b_w_in, sgu_ln_g, sgu_ln_b, sgu_w_s, sgu_b_s, w_mem_kv, w_out, ln_mix_g, ln_mix_b,
                               w_gate, w_up, w_down, ln_ffn_g, ln_ffn_b)))
    mom_m = dict(zip(names, (m_a_w_in, m_b_w_in, m_sgu_ln_g, m_sgu_ln_b, m_sgu_w_s, m_sgu_b_s, m_w_mem_kv, m_w_out, m_ln_mix_g,
                             m_ln_mix_b, m_w_gate, m_w_up, m_w_down, m_ln_ffn_g, m_ln_ffn_b)))
    mom_v = dict(zip(names, (v_a_w_in, v_b_w_in, v_sgu_ln_g, v_sgu_ln_b, v_sgu_w_s, v_sgu_b_s, v_w_mem_kv, v_w_out, v_ln_mix_g,
                             v_ln_mix_b, v_w_gate, v_w_up, v_w_down, v_ln_ffn_g, v_ln_ffn_b)))
    full_first, sgu_ln = _gather_first(weights)
    loss_part, grad_x, local, early = _local_step(
        x, mem, loss_target, full_first, sgu_ln, {n: weights[n] for n in _REPLICATED}, _rest_blocks(weights), True)
    loss = lax.psum(loss_part[0, 0], ("x", "y", "c"))
    early_parts = dict(zip(_EARLY, early))
    rep_grads = _finish_replicated(early[len(_EARLY):], {n: weights[n].shape for n in _REPLICATED})
    a_first_parts = _reduce_last(local["a_w_in"][:1])

    reduced, deltas, new_m, new_v = {}, {}, {}, {}
    for n in names:
        w2, m2, v2 = _as_2d(weights[n]), _as_2d(mom_m[n]), _as_2d(mom_v[n])
        if n in early_parts:
            outs = _adamw(w2, m2, v2, parts=early_parts[n], first_parts=a_first_parts if n == "a_w_in" else None)
        else:
            outs = _adamw(w2, m2, v2, grad=_as_2d(rep_grads[n]))
        reduced[n], deltas[n], new_m[n], new_v[n] = (a.reshape(weights[n].shape) for a in outs)

    return (loss, grad_x, *[reduced[n] for n in names], *[deltas[n] for n in names],
            *[new_m[n] for n in names], *[new_v[n] for n in names])


def _local_step(x, mem, loss_target, full_first, sgu_ln, small, rest, rest_sharded):
    sgu_w_s, sgu_b_s = small["sgu_w_s"], small["sgu_b_s"]
    ln_mix_g, ln_mix_b, ln_ffn_g, ln_ffn_b = (small[n] for n in ("ln_mix_g", "ln_mix_b", "ln_ffn_g", "ln_ffn_b"))
    bsz, seq, _ = x.shape
    tokens = bsz * seq
    slopes = _alibi_table()
    later = {} if rest_sharded else dict(zip(_BIG, rest))

    def weight(name, idx):
        if name != "b_w_in" and idx == 0:
            return full_first[name][0]
        return later[name][idx if name == "b_w_in" else idx - 1]

    res = (x.reshape(tokens, D_MODEL),)
    xb = res[0].astype(BF16)
    memb = mem.reshape(bsz * N_MEM, D_MODEL).astype(BF16)
    tgt = loss_target.reshape(tokens, D_MODEL)

    saved = []
    for i in range(DEPTH):
        j = i // 2
        dil_layer = i % 2 == 0
        w_in = weight("a_w_in" if dil_layer else "b_w_in", j)
        mkv = _linear_nn("mem_kv", memb, weight("w_mem_kv", i))
        h = _linear_nn("in_proj_a" if dil_layer else "in_proj_b", xb, w_in)
        st = dict(xb=xb, h=h, mkv=mkv, w_in=w_in)
        if dil_layer:
            carry = rest if (i == 0 and rest_sharded) else None
            mix, st["lse"], gathered = _band_attn_fwd_fused(h, slopes, bsz, seq, gather=carry)
            if carry is not None:
                later = {n: _unshard(n, g) for n, g in zip(_BIG, gathered)}
            q_col = 3 * MIX_W // MEM_W
        else:
            st["ws"] = sgu_w_s[j]
            st["bs_t"] = sgu_b_s[j].T
            st["ln_g"] = sgu_ln["sgu_ln_g"][j].reshape(1, MIX_W)
            st["ln_b"] = sgu_ln["sgu_ln_b"][j].reshape(1, MIX_W)
            mix = _sgu_fwd(h, st["ws"], st["bs_t"], st["ln_g"], st["ln_b"])
            q_col = 2 * MIX_W // MEM_W
        mo = _mem_attn_fwd(h, mkv, bsz, seq, q_col)
        w_out, w_down = weight("w_out", i), weight("w_down", i)
        w_gu = jnp.concatenate([weight("w_gate", i), weight("w_up", i)], axis=-1)
        mix_ln = (ln_mix_g[i].reshape(1, D_MODEL), ln_mix_b[i].reshape(1, D_MODEL))
        ffn_ln = (ln_ffn_g[i].reshape(1, D_MODEL), ln_ffn_b[i].reshape(1, D_MODEL))
        r1, x1b = _proj_ln_fwd("out_proj_ln", [mix, mo], w_out, res, *mix_ln)
        gt, up, act = _ffn_up_fwd(x1b, w_gu)
        r2, xb = _proj_ln_fwd("ffn_down_ln", [act], w_down, (r1, *mix_ln), *ffn_ln)
        res = (r2, *ffn_ln)
        st.update(mix=mix, mo=mo, q_col=q_col, r1=r1, x1b=x1b, gt=gt, up=up, act=act, r2=r2,
                  w_out=w_out, w_down=w_down, w_gu=w_gu)
        saved.append(st)

    dr2, dr2b, dg, db, loss_part = _loss_ln_bwd(*res, tgt)

    early_parts = None
    per_pair = ("a_w_in", "b_w_in", "sgu_ln_g", "sgu_ln_b", "sgu_w_s", "sgu_b_s")
    grads = {n: [None] * (DEPTH // 2 if n in per_pair else DEPTH) for n in _BIG + _SGU_LN + _REPLICATED}
    for i in reversed(range(DEPTH)):
        j = i // 2
        st = saved[i]
        dil_layer = i % 2 == 0
        w_in = st["w_in"]
        grads["ln_ffn_g"][i], grads["ln_ffn_b"][i] = dg[0], db[0]
        dgu = _ffn_down_bwd(dr2b, st["w_down"], st["gt"], st["up"])
        grads["w_down"][i] = _mm_tn("grad_w_down", st["act"], dr2b)
        dr1, dr1b, dg, db = _linear_nt("ffn_up_bwd", [dgu], st["w_gu"], dr2, F32,
                                       ln=(st["r1"], ln_mix_g[i].reshape(1, D_MODEL)))
        grads["ln_mix_g"][i], grads["ln_mix_b"][i] = dg[0], db[0]
        dw_gu = _mm_tn("grad_w_gate_up", st["x1b"], dgu)
        grads["w_gate"][i], grads["w_up"][i] = dw_gu[:, :D_FF], dw_gu[:, D_FF:]
        dcat = _linear_nt("out_proj_bwd", [dr1b], st["w_out"], None, BF16)
        grads["w_out"][i] = jnp.concatenate(
            [_mm_tn("grad_w_out_mix", st["mix"], dr1b), _mm_tn("grad_w_out_mem", st["mo"], dr1b)], axis=0)
        dqm, dmkv = _mem_attn_bwd(st["h"], st["mkv"], dcat, bsz, seq, st["q_col"])
        grads["w_mem_kv"][i] = _mm_tn("grad_w_mem_kv", memb, dmkv.astype(BF16))
        if dil_layer:
            early_sums = _chip_sums_of_early(grads) if (i == 0 and rest_sharded) else None
            dh_parts, exchanged = _band_attn_bwd_fused(st["h"], dcat, st["mix"], st["lse"], slopes, bsz, seq,
                                                       exchange=early_sums)
            if early_sums is not None:
                early_parts = exchanged
        else:
            ws_t = jnp.swapaxes(st["ws"], -1, -2)
            dh_main, dws, dbs_t, dlg, dlb = _sgu_bwd(st["h"], dcat, st["ws"], ws_t, st["bs_t"], st["ln_g"], st["ln_b"])
            grads["sgu_w_s"][j], grads["sgu_b_s"][j] = dws, dbs_t.T
            grads["sgu_ln_g"][j], grads["sgu_ln_b"][j] = dlg[0], dlb[0]
            dh_parts = [dh_main]
        name = "in_proj_bwd_a" if dil_layer else "in_proj_bwd_b"
        if i > 0:
            dr2, dr2b, dg, db = _linear_nt(name, [*dh_parts, dqm], w_in, dr1, F32,
                                           ln=(saved[i - 1]["r2"], ln_ffn_g[i - 1].reshape(1, D_MODEL)))
        else:
            grad_x = _linear_nt(name + "_x", [*dh_parts, dqm], w_in, dr1, F32).reshape(x.shape)
        grads["a_w_in" if dil_layer else "b_w_in"][j] = jnp.concatenate(
            [_mm_tn("grad_w_in_part", st["xb"], part) for part in dh_parts] + [_mm_tn("grad_w_in_qm", st["xb"], dqm)], axis=1)
    return loss_part, grad_x, {n: jnp.stack(g) for n, g in grads.items()}, early_parts
```

```python
import functools
import math

import numpy as np
import jax
import jax.numpy as jnp
from jax import lax
from jax.experimental import pallas as pl
from jax.experimental.pallas import tpu as pltpu

F32 = jnp.float32
BF16 = jnp.bfloat16

D_MODEL = 1024
DEPTH = 4
N_MEM = 256
HEAD_DIM = 64
N_HEADS = 12
MIX_W = N_HEADS * HEAD_DIM
MEM_W = 4 * HEAD_DIM
DIL_PATTERNS = ((128, 1), (512, 4), (2048, 16))
BLK = 128
HEAD_GROUP = 4
N_GROUPS = N_HEADS // HEAD_GROUP
D_FF = 2816
ALPHA = (2 * DEPTH) ** 0.25
LN_EPS = 1e-5
SCALE = HEAD_DIM ** -0.5
NEG = -1e30
N_DEV = 8

ADAM_LR, ADAM_B1, ADAM_B2, ADAM_EPS, ADAM_WD, ADAM_STEP = 0.001, 0.9, 0.999, 1e-08, 0.01, 10

VMEM_LIMIT = 56 * 2 ** 20
STAT_LANES = 32
STAT_W = N_HEADS * STAT_LANES


def _dot_nn(a, b):
    return lax.dot_general(a, b, (((1,), (0,)), ((), ())), preferred_element_type=F32)


def _dot_nt(a, b):
    return lax.dot_general(a, b, (((1,), (1,)), ((), ())), preferred_element_type=F32)


def _dot_tn(a, b):
    return lax.dot_general(a, b, (((0,), (0,)), ((), ())), preferred_element_type=F32)


def _ln_hat(r):
    mu = jnp.mean(r, axis=-1, keepdims=True)
    xc = r - mu
    var = jnp.mean(xc * xc, axis=-1, keepdims=True)
    rstd = lax.rsqrt(var + LN_EPS)
    return xc * rstd, rstd


def _params(sem):
    return pltpu.CompilerParams(dimension_semantics=sem, vmem_limit_bytes=VMEM_LIMIT)


def _rows(tm, c, col=0):
    return pl.BlockSpec((tm, c), lambda i: (i, col))


def _whole(shape):
    nd = len(shape)
    return pl.BlockSpec(tuple(shape), lambda *_: (0,) * nd)


def _resident(shape):
    nd = len(shape)
    return pl.BlockSpec(tuple(shape), lambda *_: (0,) * nd, pipeline_mode=pl.Buffered(1))


def _sds(shape, dtype):
    return jax.ShapeDtypeStruct(tuple(shape), dtype)


def _linear_nn(name, a, w, tm=512):
    t, k = a.shape
    n = w.shape[1]
    tm = min(tm, t)

    def body(a_ref, w_ref, o_ref):
        o_ref[...] = _dot_nn(a_ref[...], w_ref[...]).astype(BF16)

    return pl.pallas_call(
        body, name=name, grid=(t // tm,), in_specs=[_rows(tm, k), _resident(w.shape)], out_specs=_rows(tm, n),
        out_shape=_sds((t, n), BF16), compiler_params=_params(("parallel",)))(a, w)


def _proj_ln_fwd(name, lhs, w, res, g, b, tm=256):
    t = res[0].shape[0]
    n_lhs = len(lhs)
    n_res = len(res)

    def body(*refs):
        lhs_refs = refs[:n_lhs]
        w_ref = refs[n_lhs]
        res_refs = refs[n_lhs + 1:n_lhs + 1 + n_res]
        g_ref, b_ref, r_ref, xnb_ref = refs[n_lhs + 1 + n_res:]
        y, off = None, 0
        for lr in lhs_refs:
            k = lr.shape[1]
            term = _dot_nn(lr[...], w_ref[off:off + k, :])
            y = term if y is None else y + term
            off += k
        x_res = res_refs[0][...]
        if n_res == 3:
            x_res = _ln_hat(x_res)[0] * res_refs[1][...] + res_refs[2][...]
        r = ALPHA * x_res + y
        r_ref[...] = r
        xnb_ref[...] = (_ln_hat(r)[0] * g_ref[...] + b_ref[...]).astype(BF16)

    vec = _whole((1, D_MODEL))
    in_specs = ([_rows(tm, a.shape[1]) for a in lhs] + [_resident(w.shape), _rows(tm, D_MODEL)] + [vec] * (n_res - 1) + [vec, vec])
    return pl.pallas_call(
        body, name=name, grid=(t // tm,), in_specs=in_specs, out_specs=[_rows(tm, D_MODEL)] * 2,
        out_shape=[_sds((t, D_MODEL), F32), _sds((t, D_MODEL), BF16)],
        compiler_params=_params(("parallel",)))(*lhs, w, *res, g, b)


def _ffn_up_fwd(xb, wgu, gather=None, tm=256):
    t = xb.shape[0]
    ng = 0 if gather is None else len(gather)
    steps = t // tm

    def body(*refs):
        x_ref, w_ref = refs[:2]
        g_ref, u_ref, a_ref = refs[2 + ng:5 + ng]
        if ng:
            start, finish = _gather_protocol(refs[2:2 + ng], refs[5 + ng:5 + 2 * ng], *refs[5 + 2 * ng:])
            pl.when(pl.program_id(0) == 0)(start)
        gu = _dot_nn(x_ref[...], w_ref[...])
        gt, up = gu[:, :D_FF], gu[:, D_FF:]
        g_ref[...] = gt.astype(BF16)
        u_ref[...] = up.astype(BF16)
        a_ref[...] = (gt * jax.nn.sigmoid(gt) * up).astype(BF16)
        if ng:
            pl.when(pl.program_id(0) == steps - 1)(finish)

    outs = pl.pallas_call(
        body, name="ffn_up_fwd_gather" if ng else "ffn_up_fwd", grid=(steps,),
        in_specs=[_rows(tm, D_MODEL), _resident(wgu.shape)] + [_ANY] * ng,
        out_specs=[_rows(tm, D_FF)] * 3 + [_ANY] * ng,
        out_shape=[_sds((t, D_FF), BF16)] * 3 + (_gather_shapes(gather) if ng else []),
        scratch_shapes=_gather_sems(ng) if ng else [],
        compiler_params=_params(("arbitrary" if ng else "parallel",)))(xb, wgu, *(gather or []))
    return outs[0], outs[1], outs[2], list(outs[3:])


def _ln_bwd_rows(dxn, xhat, rstd, g_ref, dr_ref, drb_ref, dg_ref, db_ref):
    @pl.when(pl.program_id(0) == 0)
    def _():
        dg_ref[...] = jnp.zeros_like(dg_ref)
        db_ref[...] = jnp.zeros_like(db_ref)

    dxh = dxn * g_ref[...]
    m1 = jnp.mean(dxh, axis=-1, keepdims=True)
    m2 = jnp.mean(dxh * xhat, axis=-1, keepdims=True)
    dr = rstd * (dxh - m1 - xhat * m2)
    dr_ref[...] = dr
    drb_ref[...] = dr.astype(BF16)
    dg_ref[...] += jnp.sum(dxn * xhat, axis=0, keepdims=True)
    db_ref[...] += jnp.sum(dxn, axis=0, keepdims=True)


def _ln_bwd_outs(t, tm):
    vec = _whole((1, D_MODEL))
    specs = [_rows(tm, D_MODEL), _rows(tm, D_MODEL), vec, vec]
    shapes = [_sds((t, D_MODEL), F32), _sds((t, D_MODEL), BF16), _sds((1, D_MODEL), F32), _sds((1, D_MODEL), F32)]
    return specs, shapes


def _loss_ln_bwd(r, g, b, tgt, tm=512):
    t = r.shape[0]

    def body(r_ref, g_ref, b_ref, t_ref, dr_ref, drb_ref, dg_ref, db_ref, l_ref):
        @pl.when(pl.program_id(0) == 0)
        def _():
            l_ref[...] = jnp.zeros_like(l_ref)

        xhat, rstd = _ln_hat(r_ref[...])
        e = xhat * g_ref[...] + b_ref[...] - t_ref[...]
        l_ref[...] += jnp.sum(e * e) * (0.5 / D_MODEL)
        _ln_bwd_rows(e * (1.0 / D_MODEL), xhat, rstd, g_ref, dr_ref, drb_ref, dg_ref, db_ref)

    vec = _whole((1, D_MODEL))
    specs, shapes = _ln_bwd_outs(t, tm)
    return pl.pallas_call(
        body, name="loss_ln_bwd", grid=(t // tm,), in_specs=[_rows(tm, D_MODEL), vec, vec, _rows(tm, D_MODEL)],
        out_specs=specs + [_whole((1, 128))], out_shape=shapes + [_sds((1, 128), F32)],
        compiler_params=_params(("arbitrary",)))(r, g, b, tgt)


def _ffn_down_bwd(drb, wd, gt, up, tm=256):
    t = drb.shape[0]

    def body(d_ref, w_ref, g_ref, u_ref, o_ref):
        da = _dot_nt(d_ref[...], w_ref[...])
        g = g_ref[...].astype(F32)
        u = u_ref[...].astype(F32)
        sg = jax.nn.sigmoid(g)
        o_ref[:, :D_FF] = (da * u * (sg * (1.0 + g * (1.0 - sg)))).astype(BF16)
        o_ref[:, D_FF:] = (da * (g * sg)).astype(BF16)

    return pl.pallas_call(
        body, name="ffn_down_bwd", grid=(t // tm,),
        in_specs=[_rows(tm, D_MODEL), _resident(wd.shape), _rows(tm, D_FF), _rows(tm, D_FF)],
        out_specs=_rows(tm, 2 * D_FF), out_shape=_sds((t, 2 * D_FF), BF16),
        compiler_params=_params(("parallel",)))(drb, wd, gt, up)


def _linear_nt(name, lhs, w, res, out_dtype, ln=None, tm=256):
    t = lhs[0].shape[0]
    n_lhs = len(lhs)
    n_out = w.shape[0]
    n_in = n_lhs + 1 + (res is not None) + (2 if ln else 0)

    def body(*refs):
        lhs_refs = refs[:n_lhs]
        w_ref = refs[n_lhs]
        y, off = None, 0
        for lr in lhs_refs:
            k = lr.shape[1]
            term = _dot_nt(lr[...], w_ref[:, off:off + k])
            y = term if y is None else y + term
            off += k
        if res is not None:
            y = ALPHA * refs[n_lhs + 1][...] + y
        if ln is None:
            refs[-1][...] = y.astype(out_dtype)
        else:
            r_ref, g_ref = refs[n_in - 2:n_in]
            xhat, rstd = _ln_hat(r_ref[...])
            _ln_bwd_rows(y, xhat, rstd, g_ref, *refs[n_in:])

    in_specs = [_rows(tm, a.shape[1]) for a in lhs] + [_resident(w.shape)]
    args = list(lhs) + [w]
    if res is not None:
        in_specs.append(_rows(tm, n_out))
        args.append(res)
    if ln is None:
        out_specs, out_shape, sem = _rows(tm, n_out), _sds((t, n_out), out_dtype), "parallel"
    else:
        in_specs += [_rows(tm, D_MODEL), _whole((1, D_MODEL))]
        args += list(ln)
        (out_specs, out_shape), sem = _ln_bwd_outs(t, tm), "arbitrary"
    return pl.pallas_call(
        body, name=name, grid=(t // tm,), in_specs=in_specs, out_specs=out_specs, out_shape=out_shape,
        compiler_params=_params((sem,)))(*args)


def _pick_tile(n, limit):
    if n <= limit:
        return n
    best = 128
    for cand in range(128, limit + 1, 128):
        if n % cand == 0:
            best = cand
    return best


def _mm_tn(name, a, b, tt=1024):
    t, k = a.shape
    n = b.shape[1]
    tt = min(tt, t)
    tk = _pick_tile(k, 1408)
    tn = _pick_tile(n, (6 * 2 ** 20) // (4 * tk) // 128 * 128)
    steps = t // tt

    def body(a_ref, b_ref, o_ref, acc_ref):
        @pl.when(pl.program_id(2) == 0)
        def _():
            acc_ref[...] = jnp.zeros_like(acc_ref)

        acc_ref[...] += _dot_tn(a_ref[...], b_ref[...])

        @pl.when(pl.program_id(2) == steps - 1)
        def _():
            o_ref[...] = acc_ref[...].astype(BF16)

    return pl.pallas_call(
        body, name=name, grid=(k // tk, n // tn, steps),
        in_specs=[pl.BlockSpec((tt, tk), lambda i, j, s: (s, i)), pl.BlockSpec((tt, tn), lambda i, j, s: (s, j))],
        out_specs=pl.BlockSpec((tk, tn), lambda i, j, s: (i, j)), out_shape=_sds((k, n), BF16),
        scratch_shapes=[pltpu.VMEM((tk, tn), F32)],
        compiler_params=_params(("parallel", "parallel", "arbitrary")))(a, b)


def _alibi_table():
    arr = np.zeros((N_GROUPS, 8, 128), np.float32)
    for g in range(N_GROUPS):
        for hh in range(HEAD_GROUP):
            arr[g, hh, :] = 2.0 ** (-8.0 * (g * HEAD_GROUP + hh + 1) / N_HEADS)
    return jnp.asarray(arr)


def _band_mask(n, dil):
    qi = lax.broadcasted_iota(jnp.int32, (BLK, 2 * BLK), 0)
    ki = lax.broadcasted_iota(jnp.int32, (BLK, 2 * BLK), 1)
    steps = qi + BLK - ki
    valid = (steps >= 0) & (steps <= BLK) & ((ki >= BLK) | (n > 0))
    return valid, (steps * dil).astype(F32)


def _band_specs(bsz, seq, dil, width):
    cb = width // 256

    def spec(off, prev=False):
        if prev:
            return pl.BlockSpec((None, BLK, 256), lambda b, r, g, n: (b, jnp.maximum(n - 1, 0), r * cb + off + g))
        return pl.BlockSpec((None, BLK, 256), lambda b, r, g, n: (b, n, r * cb + off + g))

    return spec


def _spread_stats(cols, per_head=STAT_LANES):
    lane = lax.broadcasted_iota(jnp.int32, (BLK, HEAD_GROUP * per_head), 1)
    tile = cols[HEAD_GROUP - 1]
    for hh in range(HEAD_GROUP - 2, -1, -1):
        tile = jnp.where(lane < (hh + 1) * per_head, cols[hh], tile)
    return tile


def _band_attn_fwd(h, slopes, bsz, seq, dil):
    width = h.shape[1]
    length = seq // dil
    nblk = length // BLK
    hv = h.reshape(bsz, length, dil * width)
    spec = _band_specs(bsz, seq, dil, width)
    k_off, v_off = MIX_W // 256, 2 * MIX_W // 256

    def body(sl_ref, q_ref, kc_ref, kp_ref, vc_ref, vp_ref, o_ref, lse_ref):
        valid, dist = _band_mask(pl.program_id(3), dil)
        q = q_ref[...]
        k2 = jnp.concatenate([kp_ref[...], kc_ref[...]], axis=0)
        v2 = jnp.concatenate([vp_ref[...], vc_ref[...]], axis=0)
        lses = []
        for hh in range(HEAD_GROUP):
            sl = slice(hh * HEAD_DIM, (hh + 1) * HEAD_DIM)
            s = _dot_nt(q[:, sl], k2[:, sl]) * SCALE - sl_ref[hh:hh + 1, 0:1] * dist
            s = jnp.where(valid, s, NEG)
            m = jnp.max(s, axis=-1, keepdims=True)
            p = jnp.exp(s - m)
            l = jnp.sum(p, axis=-1, keepdims=True)
            acc = _dot_nn(p.astype(BF16), v2[:, sl])
            o_ref[:, sl] = (acc / l).astype(BF16)
            lses.append(m + jnp.log(l))
        lse_ref[...] = _spread_stats(lses)

    out, lse = pl.pallas_call(
        body, name=f"band_attn_fwd_d{dil}", grid=(bsz, dil, N_GROUPS, nblk),
        in_specs=[pl.BlockSpec((None, 8, 128), lambda b, r, g, n: (g, 0, 0)),
                  spec(0), spec(k_off), spec(k_off, True), spec(v_off), spec(v_off, True)],
        out_specs=[pl.BlockSpec((None, BLK, 256), lambda b, r, g, n: (b, n, r * N_GROUPS + g)),
                   pl.BlockSpec((None, BLK, 128), lambda b, r, g, n: (b, n, r * N_GROUPS + g))],
        out_shape=[_sds((bsz, length, dil * MIX_W), BF16), _sds((bsz, length, dil * STAT_W), F32)],
        compiler_params=_params(("parallel", "parallel", "parallel", "arbitrary")))(slopes, hv, hv, hv, hv, hv)
    return out.reshape(bsz * seq, MIX_W), lse.reshape(bsz * seq, STAT_W)


def _band_merge(outs, lses, tm=512):
    t = outs[0].shape[0]

    def body(o1, o2, o3, l1, l2, l3, mix_ref, lse_ref):
        ls = [l1[...], l2[...], l3[...]]
        m = jnp.maximum(jnp.maximum(ls[0], ls[1]), ls[2])
        tot = m + jnp.log(jnp.exp(ls[0] - m) + jnp.exp(ls[1] - m) + jnp.exp(ls[2] - m))
        ws = [jnp.exp(x - tot) for x in ls]
        lse_ref[...] = tot
        for hd in range(N_HEADS):
            sl = slice(hd * HEAD_DIM, (hd + 1) * HEAD_DIM)
            acc = None
            for w, o in zip(ws, (o1, o2, o3)):
                term = w[:, hd * STAT_LANES:hd * STAT_LANES + 1] * o[:, sl].astype(F32)
                acc = term if acc is None else acc + term
            mix_ref[:, sl] = acc.astype(BF16)

    return pl.pallas_call(
        body, name="band_merge", grid=(t // tm,), in_specs=[_rows(tm, MIX_W)] * 3 + [_rows(tm, STAT_W)] * 3,
        out_specs=[_rows(tm, MIX_W), _rows(tm, STAT_W)], out_shape=[_sds((t, MIX_W), BF16), _sds((t, STAT_W), F32)],
        compiler_params=_params(("parallel",)))(*outs, *lses)


def _band_delta(dcat, mix, tm=512):
    t = mix.shape[0]

    def body(d_ref, o_ref, dd_ref):
        prod = d_ref[...].astype(F32) * o_ref[...].astype(F32)
        for hd in range(N_HEADS):
            rsum = jnp.sum(prod[:, hd * HEAD_DIM:(hd + 1) * HEAD_DIM], axis=-1, keepdims=True)
            dd_ref[:, hd * STAT_LANES:(hd + 1) * STAT_LANES] = jnp.broadcast_to(rsum, (tm, STAT_LANES))

    return pl.pallas_call(
        body, name="band_delta", grid=(t // tm,), in_specs=[_rows(tm, MIX_W), _rows(tm, MIX_W)],
        out_specs=_rows(tm, STAT_W), out_shape=_sds((t, STAT_W), F32),
        compiler_params=_params(("parallel",)))(dcat, mix)


def _band_attn_bwd(h, dcat, slopes, lse, delta, bsz, seq, dil):
    width = h.shape[1]
    length = seq // dil
    nblk = length // BLK
    hv = h.reshape(bsz, length, dil * width)
    dv_ = dcat.reshape(bsz, length, dil * D_MODEL)
    k_off, v_off = MIX_W // 256, 2 * MIX_W // 256
    cb, dcb = width // 256, D_MODEL // 256

    def cur(off, c):
        return pl.BlockSpec((None, BLK, 256), lambda b, r, g, n: (b, jnp.minimum(n, nblk - 1), r * c + off + g))

    def prev(off, c):
        return pl.BlockSpec((None, BLK, 256), lambda b, r, g, n: (b, jnp.maximum(jnp.minimum(n, nblk - 1) - 1, 0), r * c + off + g))

    stat = pl.BlockSpec((None, BLK, 128), lambda b, r, g, n: (b, jnp.minimum(n, nblk - 1), r * N_GROUPS + g))
    dq_spec = pl.BlockSpec((None, BLK, 256), lambda b, r, g, n: (b, jnp.minimum(n, nblk - 1), r * N_GROUPS + g))
    dkv_spec = pl.BlockSpec((None, BLK, 256), lambda b, r, g, n: (b, jnp.maximum(n - 1, 0), r * N_GROUPS + g))

    def body(sl_ref, q_ref, kc_ref, kp_ref, vc_ref, vp_ref, do_ref, lse_ref, dd_ref, dq_ref, dk_ref, dv_ref, kcar, vcar):
        n = pl.program_id(3)

        @pl.when(n == 0)
        def _():
            kcar[...] = jnp.zeros_like(kcar)
            vcar[...] = jnp.zeros_like(vcar)

        @pl.when(n < nblk)
        def _():
            valid, dist = _band_mask(n, dil)
            q = q_ref[...]
            do = do_ref[...]
            k2 = jnp.concatenate([kp_ref[...], kc_ref[...]], axis=0)
            v2 = jnp.concatenate([vp_ref[...], vc_ref[...]], axis=0)
            for hh in range(HEAD_GROUP):
                sl = slice(hh * HEAD_DIM, (hh + 1) * HEAD_DIM)
                s = _dot_nt(q[:, sl], k2[:, sl]) * SCALE - sl_ref[hh:hh + 1, 0:1] * dist
                s = jnp.where(valid, s, NEG)
                st = slice(hh * STAT_LANES, hh * STAT_LANES + 1)
                p = jnp.exp(s - lse_ref[:, st])
                dp = _dot_nt(do[:, sl], v2[:, sl])
                ds = (p * (dp - dd_ref[:, st]) * SCALE).astype(BF16)
                dq_ref[:, sl] = _dot_nn(ds, k2[:, sl]).astype(BF16)
                dk2 = _dot_tn(ds, q[:, sl])
                dv2 = _dot_tn(p.astype(BF16), do[:, sl])
                dk_ref[:, sl] = (kcar[:, sl] + dk2[:BLK]).astype(BF16)
                dv_ref[:, sl] = (vcar[:, sl] + dv2[:BLK]).astype(BF16)
                kcar[:, sl] = dk2[BLK:]
                vcar[:, sl] = dv2[BLK:]

        @pl.when(n == nblk)
        def _():
            dk_ref[...] = kcar[...].astype(BF16)
            dv_ref[...] = vcar[...].astype(BF16)

    outs = pl.pallas_call(
        body, name=f"band_attn_bwd_d{dil}", grid=(bsz, dil, N_GROUPS, nblk + 1),
        in_specs=[pl.BlockSpec((None, 8, 128), lambda b, r, g, n: (g, 0, 0)),
                  cur(0, cb), cur(k_off, cb), prev(k_off, cb), cur(v_off, cb), prev(v_off, cb), cur(0, dcb), stat, stat],
        out_specs=[dq_spec, dkv_spec, dkv_spec],
        out_shape=[_sds((bsz, length, dil * MIX_W), BF16)] * 3,
        scratch_shapes=[pltpu.VMEM((BLK, 256), F32), pltpu.VMEM((BLK, 256), F32)],
        compiler_params=_params(("parallel", "parallel", "parallel", "arbitrary")))(
            slopes, hv, hv, hv, hv, hv, dv_, lse.reshape(bsz, length, dil * STAT_W), delta.reshape(bsz, length, dil * STAT_W))
    return [o.reshape(bsz * seq, MIX_W) for o in outs]


def _sum_patterns(parts, tm=512):
    t = parts[0][0].shape[0]

    def body(*refs):
        o_ref = refs[-1]
        for j in range(3):
            acc = refs[j][...].astype(F32) + refs[3 + j][...].astype(F32) + refs[6 + j][...].astype(F32)
            o_ref[:, j * MIX_W:(j + 1) * MIX_W] = acc.astype(BF16)

    flat = [x for p in parts for x in p]
    return pl.pallas_call(
        body, name="band_sum", grid=(t // tm,), in_specs=[_rows(tm, MIX_W)] * 9, out_specs=_rows(tm, 3 * MIX_W),
        out_shape=_sds((t, 3 * MIX_W), BF16), compiler_params=_params(("parallel",)))(*flat)


def _block_mask(has_prev, dil):
    if has_prev is None:
        steps = lax.broadcasted_iota(jnp.int32, (BLK, BLK), 0) - lax.broadcasted_iota(jnp.int32, (BLK, BLK), 1)
        return steps >= 0, (steps * dil).astype(F32)
    qi = lax.broadcasted_iota(jnp.int32, (BLK, 2 * BLK), 0)
    ki = lax.broadcasted_iota(jnp.int32, (BLK, 2 * BLK), 1)
    steps = qi + BLK - ki
    valid = (steps >= 0) & (steps <= BLK) & ((ki >= BLK) | has_prev)
    return valid, (steps * dil).astype(F32)


def _bias_scratch():
    return pltpu.VMEM((2, HEAD_GROUP, BLK, 2 * BLK), F32)


def _fill_bias(bias, sl_ref, dil):
    for p in range(2):
        valid, dist = _block_mask(p == 1, dil)
        for hh in range(HEAD_GROUP):
            bias[p, hh] = jnp.where(valid, -sl_ref[hh:hh + 1, 0:1] * dist, NEG)


def _rows_of(j):
    return pl.ds(pl.multiple_of(j * BLK, BLK), BLK)


def _lane_half(hf):
    return slice(hf * 128, (hf + 1) * 128)


def _split_pair(x):
    first = lax.broadcasted_iota(jnp.int32, (1, 2 * HEAD_DIM), 1) < HEAD_DIM
    zero = jnp.zeros_like(x)
    return jnp.where(first, x, zero), jnp.where(first, zero, x)


def _deinterleave(src, dst, seq, dil, dtype):
    length = seq // dil
    for r in range(dil):
        for c in range(length // BLK):
            rows = pl.ds(r + c * BLK * dil, BLK, stride=dil)
            out = slice(r * length + c * BLK, r * length + (c + 1) * BLK)
            if len(src.shape) == 2:
                dst[out, :] = src[rows, :].astype(dtype)
            else:
                for hf in range(2):
                    dst[out, _lane_half(hf)] = src.at[hf][rows, :].astype(dtype)


def _interleave(src, dst, seq, dil, accumulate):
    length = seq // dil
    for r in range(dil):
        for c in range(length // BLK):
            rows = pl.ds(r + c * BLK * dil, BLK, stride=dil)
            inp = slice(r * length + c * BLK, r * length + (c + 1) * BLK)
            if len(dst.shape) == 2:
                dst[rows, :] = dst[rows, :] + src[inp, :] if accumulate else src[inp, :]
            else:
                for hf in range(2):
                    val = src[inp, _lane_half(hf)]
                    half = dst.at[hf]
                    half[rows, :] = half[rows, :] + val if accumulate else val


def _split_halves(src, dst, seq):
    def step(i, carry):
        for hf in range(2):
            dst[hf, _rows_of(i), :] = src[_rows_of(i), _lane_half(hf)].astype(F32)
        return carry

    lax.fori_loop(0, seq // BLK, step, 0)


def _band_attn_fwd_fused(h, slopes, bsz, seq, gather=None):
    width = h.shape[1]
    cb = width // 256
    k_off, v_off = MIX_W // 256, 2 * MIX_W // 256
    nb = seq // BLK

    ng = 0 if gather is None else len(gather)

    def body(*refs):
        sl_ref, q_ref, k_ref, v_ref = refs[:4]
        mix_ref, lse_ref = refs[4 + ng:6 + ng]
        qf, kf, vf, qd, kd, vd, od, ld, o1, o2, o3, l1, l2, l3, bias = refs[6 + 2 * ng:21 + 2 * ng]
        if ng:
            start, finish = _gather_protocol(refs[4:4 + ng], refs[6 + ng:6 + 2 * ng], *refs[21 + 2 * ng:])
            pl.when((pl.program_id(0) == 0) & (pl.program_id(1) == 0))(start)

        def run(dil, qs, ks, vs, o_dst, l_dst):
            nblk = seq // dil // BLK
            _fill_bias(bias, sl_ref, dil)

            def block(j, carry):
                rows, prows = _rows_of(j), _rows_of(jnp.maximum(j - 1, 0))
                has_prev = ((j % nblk) != 0).astype(jnp.int32)

                def keys(ref, lanes):
                    return jnp.concatenate([ref[prows, lanes], ref[rows, lanes]], axis=0)

                lses = []
                for pr in range(HEAD_GROUP // 2):
                    lanes = _lane_half(pr)
                    q_ab = _split_pair(qs[rows, lanes] * SCALE)
                    k2 = keys(ks, lanes)
                    v_ab = _split_pair(keys(vs, lanes))
                    out = None
                    for ab in range(2):
                        hh = 2 * pr + ab
                        s = _dot_nt(q_ab[ab], k2) + bias[has_prev, hh]
                        m = jnp.max(s, axis=-1, keepdims=True)
                        p = jnp.exp(s - m)
                        l = jnp.sum(p, axis=-1, keepdims=True)
                        term = _dot_nn(p.astype(BF16), v_ab[ab]) / l
                        out = term if out is None else out + term
                        lses.append(m + jnp.log(l))
                    o_dst[rows, lanes] = out
                l_dst[rows, :] = _spread_stats(lses, HEAD_DIM)
                return carry

            lax.fori_loop(0, nb, block, 0, unroll=8)

        run(1, q_ref, k_ref, v_ref, o1, l1)
        _split_halves(q_ref, qf, seq)
        _split_halves(k_ref, kf, seq)
        _split_halves(v_ref, vf, seq)
        for dil, o_tok, l_tok in ((4, o2, l2), (16, o3, l3)):
            _deinterleave(qf, qd, seq, dil, BF16)
            _deinterleave(kf, kd, seq, dil, BF16)
            _deinterleave(vf, vd, seq, dil, BF16)
            run(dil, qd, kd, vd, od, ld)
            _interleave(od, o_tok, seq, dil, False)
            _interleave(ld, l_tok, seq, dil, False)

        def merge(i, carry):
            rows = _rows_of(i)

            def both(ref):
                return jnp.concatenate([ref[0, rows, :], ref[1, rows, :]], axis=1)

            ls = [l1[rows, :], both(l2), both(l3)]
            m = jnp.maximum(jnp.maximum(ls[0], ls[1]), ls[2])
            tot = m + jnp.log(jnp.exp(ls[0] - m) + jnp.exp(ls[1] - m) + jnp.exp(ls[2] - m))
            ws = [jnp.exp(x - tot) for x in ls]
            mix_ref[rows, :] = (ws[0] * o1[rows, :] + ws[1] * both(o2) + ws[2] * both(o3)).astype(BF16)
            lse_ref[rows, :] = _spread_stats([tot[:, hh * HEAD_DIM:hh * HEAD_DIM + 1] for hh in range(HEAD_GROUP)])
            return carry

        lax.fori_loop(0, nb, merge, 0)
        if ng:
            pl.when((pl.program_id(0) == bsz - 1) & (pl.program_id(1) == N_GROUPS - 1))(finish)

    def hspec(off):
        return pl.BlockSpec((seq, 256), lambda b, g: (b, off + g))

    big = lambda dt: pltpu.VMEM((seq, 256), dt)
    halves = lambda: pltpu.VMEM((2, seq, 128), F32)
    stat = lambda: pltpu.VMEM((seq, 128), F32)
    outs = pl.pallas_call(
        body, name="band_attn_fwd_gather" if ng else "band_attn_fwd", grid=(bsz, N_GROUPS),
        in_specs=[pl.BlockSpec((None, 8, 128), lambda b, g: (g, 0, 0)), hspec(0), hspec(k_off), hspec(v_off)] + [_ANY] * ng,
        out_specs=[pl.BlockSpec((seq, 256), lambda b, g: (b, g)), pl.BlockSpec((seq, 128), lambda b, g: (b, g))] + [_ANY] * ng,
        out_shape=[_sds((bsz * seq, MIX_W), BF16), _sds((bsz * seq, STAT_W), F32)] + (_gather_shapes(gather) if ng else []),
        scratch_shapes=[halves(), halves(), halves(), big(BF16), big(BF16), big(BF16), big(F32), big(F32),
                        big(F32), halves(), halves(), big(F32), halves(), halves(), _bias_scratch()]
        + (_gather_sems(ng) if ng else []),
        compiler_params=_params(("arbitrary", "arbitrary")))(slopes, h, h, h, *(gather or []))
    return outs[0], outs[1], list(outs[2:])


def _band_attn_bwd_fused(h, dcat, mix, lse, slopes, bsz, seq, exchange=None):
    width = h.shape[1]
    k_off, v_off = MIX_W // 256, 2 * MIX_W // 256
    nb = seq // BLK

    ne = 0 if exchange is None else len(exchange)

    def body(*refs):
        sl_ref, q_ref, k_ref, v_ref, do_ref, o_ref, lse_ref = refs[:7]
        dq_ref, dk_ref, dv_ref = refs[7 + ne:10 + ne]
        qf, kf, vf, dof, ddt, qd, kd, vd, dod, lsd, ddd, gq, gk, gv, aq, ak, av, bias = refs[10 + 2 * ne:28 + 2 * ne]
        if ne:
            start, finish = _exchange_protocol(refs[7:7 + ne], refs[10 + ne:10 + 2 * ne], *refs[28 + 2 * ne:])
            pl.when((pl.program_id(0) == 0) & (pl.program_id(1) == 0))(start)

        same_head = (lax.broadcasted_iota(jnp.int32, (HEAD_GROUP * HEAD_DIM, HEAD_GROUP * STAT_LANES), 0) // HEAD_DIM
                     == lax.broadcasted_iota(jnp.int32, (HEAD_GROUP * HEAD_DIM, HEAD_GROUP * STAT_LANES), 1) // STAT_LANES)
        ones_map = jnp.where(same_head, 1.0, 0.0).astype(BF16)

        def delta(i, carry):
            rows = _rows_of(i)
            prod = do_ref[rows, :].astype(F32) * o_ref[rows, :].astype(F32)
            high = prod.astype(BF16)
            rest = (prod - high.astype(F32)).astype(BF16)
            ddt[rows, :] = _dot_nn(high, ones_map) + _dot_nn(rest, ones_map)
            return carry

        lax.fori_loop(0, nb, delta, 0)

        def zero(i, carry):
            rows = _rows_of(i)
            for ref in (gk, gv):
                ref[rows, :] = jnp.zeros((BLK, 256), F32)
            return carry

        def run(dil, qs, ks, vs, dos, lss, dds):
            nblk = seq // dil // BLK
            _fill_bias(bias, sl_ref, dil)
            if nblk > 1:
                lax.fori_loop(0, nb, zero, 0)

            def block(j, carry):
                rows, prows = _rows_of(j), _rows_of(jnp.maximum(j - 1, 0))
                has_prev = ((j % nblk) != 0).astype(jnp.int32)

                def keys(ref, lanes):
                    if nblk == 1:
                        return ref[rows, lanes]
                    return jnp.concatenate([ref[prows, lanes], ref[rows, lanes]], axis=0)

                for pr in range(HEAD_GROUP // 2):
                    lanes = _lane_half(pr)
                    q_ab = _split_pair(qs[rows, lanes] * SCALE)
                    do_ab = _split_pair(dos[rows, lanes])
                    k2, v2 = keys(ks, lanes), keys(vs, lanes)
                    k_ab = _split_pair(k2)
                    dq, dk2, dv2 = None, None, None
                    for ab in range(2):
                        hh = 2 * pr + ab
                        st = slice(hh * STAT_LANES, hh * STAT_LANES + 1)
                        s = _dot_nt(q_ab[ab], k2) + (bias[0, hh, :, BLK:] if nblk == 1 else bias[has_prev, hh])
                        p = jnp.exp(s - lss[rows, st])
                        dp = _dot_nt(do_ab[ab], v2)
                        ds = (p * (dp - dds[rows, st])).astype(BF16)
                        terms = (_dot_nn(ds, k_ab[ab]), _dot_tn(ds, q_ab[ab]), _dot_tn(p.astype(BF16), do_ab[ab]))
                        dq, dk2, dv2 = terms if dq is None else (dq + terms[0], dk2 + terms[1], dv2 + terms[2])
                    gq[rows, lanes] = dq * SCALE
                    if nblk == 1:
                        gk[rows, lanes] = dk2
                        gv[rows, lanes] = dv2
                    else:
                        gk[prows, lanes] += dk2[:BLK]
                        gv[prows, lanes] += dv2[:BLK]
                        gk[rows, lanes] += dk2[BLK:]
                        gv[rows, lanes] += dv2[BLK:]
                return carry

            lax.fori_loop(0, nb, block, 0, unroll=4)

        run(1, q_ref, k_ref, v_ref, do_ref, lse_ref, ddt)

        for src, dst in ((gq, aq), (gk, ak), (gv, av), (q_ref, qf), (k_ref, kf), (v_ref, vf), (do_ref, dof)):
            _split_halves(src, dst, seq)
        for dil in (4, 16):
            for src, dst in ((qf, qd), (kf, kd), (vf, vd), (dof, dod)):
                _deinterleave(src, dst, seq, dil, BF16)
            _deinterleave(lse_ref, lsd, seq, dil, F32)
            _deinterleave(ddt, ddd, seq, dil, F32)
            run(dil, qd, kd, vd, dod, lsd, ddd)
            for src, dst in ((gq, aq), (gk, ak), (gv, av)):
                _interleave(src, dst, seq, dil, True)

        def write(i, carry):
            rows = _rows_of(i)
            for src, dst in ((aq, dq_ref), (ak, dk_ref), (av, dv_ref)):
                for hf in range(2):
                    dst[rows, _lane_half(hf)] = src[hf, rows, :].astype(BF16)
            return carry

        lax.fori_loop(0, nb, write, 0)
        if ne:
            pl.when((pl.program_id(0) == bsz - 1) & (pl.program_id(1) == N_GROUPS - 1))(finish)

    def hspec(off):
        return pl.BlockSpec((seq, 256), lambda b, g: (b, off + g))

    io = pl.BlockSpec((seq, 256), lambda b, g: (b, g))
    big = lambda dt: pltpu.VMEM((seq, 256), dt)
    halves = lambda: pltpu.VMEM((2, seq, 128), F32)
    stat = lambda: pltpu.VMEM((seq, 128), F32)
    outs = pl.pallas_call(
        body, name="band_attn_bwd_exchange" if ne else "band_attn_bwd", grid=(bsz, N_GROUPS),
        in_specs=[pl.BlockSpec((None, 8, 128), lambda b, g: (g, 0, 0)), hspec(0), hspec(k_off), hspec(v_off), io, io,
                  pl.BlockSpec((seq, 128), lambda b, g: (b, g))] + [_ANY] * ne,
        out_specs=[io, io, io] + [_ANY] * ne,
        out_shape=[_sds((bsz * seq, MIX_W), BF16)] * 3 + [_sds(s.shape, s.dtype) for s in (exchange or [])],
        scratch_shapes=[halves(), halves(), halves(), halves(), stat(),
                        big(BF16), big(BF16), big(BF16), big(BF16), stat(), stat(),
                        big(F32), big(F32), big(F32), halves(), halves(), halves(), _bias_scratch()]
        + (_exchange_sems(ne) if ne else []),
        compiler_params=_params(("arbitrary", "arbitrary")))(slopes, h, h, h, dcat, mix, lse, *(exchange or []))
    return list(outs[:3]), list(outs[3:])


def _mem_attn_fwd(h, mkv, bsz, seq, q_col, tq=512):
    nq = seq // tq

    def body(q_ref, kv_ref, o_ref):
        for pr in range(2):
            lanes = _lane_half(pr)
            q_ab = _split_pair(q_ref[:, lanes])
            k = kv_ref[:, lanes]
            v_ab = _split_pair(kv_ref[:, MEM_W + pr * 128:MEM_W + (pr + 1) * 128])
            out = None
            for ab in range(2):
                s = _dot_nt(q_ab[ab], k) * SCALE
                m = jnp.max(s, axis=-1, keepdims=True)
                p = jnp.exp(s - m)
                l = jnp.sum(p, axis=-1, keepdims=True)
                term = _dot_nn(p.astype(BF16), v_ab[ab]) / l
                out = term if out is None else out + term
            o_ref[:, lanes] = out.astype(BF16)

    return pl.pallas_call(
        body, name="mem_attn_fwd", grid=(bsz, nq),
        in_specs=[pl.BlockSpec((tq, MEM_W), lambda b, i: (b * nq + i, q_col)),
                  pl.BlockSpec((N_MEM, 2 * MEM_W), lambda b, i: (b, 0))],
        out_specs=pl.BlockSpec((tq, MEM_W), lambda b, i: (b * nq + i, 0)),
        out_shape=_sds((bsz * seq, MEM_W), BF16), compiler_params=_params(("parallel", "parallel")))(h, mkv)


def _mem_attn_bwd(h, mkv, dcat, bsz, seq, q_col, tq=512):
    nq = seq // tq
    do_col = MIX_W // MEM_W

    def body(q_ref, kv_ref, do_ref, dq_ref, dkv_ref):
        @pl.when(pl.program_id(1) == 0)
        def _():
            dkv_ref[...] = jnp.zeros_like(dkv_ref)

        for pr in range(2):
            lanes = _lane_half(pr)
            vlanes = slice(MEM_W + pr * 128, MEM_W + (pr + 1) * 128)
            q_ab = _split_pair(q_ref[:, lanes])
            do_ab = _split_pair(do_ref[:, lanes])
            k, v = kv_ref[:, lanes], kv_ref[:, vlanes]
            k_ab = _split_pair(k)
            dq, dk, dv = None, None, None
            for ab in range(2):
                s = _dot_nt(q_ab[ab], k) * SCALE
                m = jnp.max(s, axis=-1, keepdims=True)
                e = jnp.exp(s - m)
                p = e / jnp.sum(e, axis=-1, keepdims=True)
                dp = _dot_nt(do_ab[ab], v)
                dd = jnp.sum(p * dp, axis=-1, keepdims=True)
                ds = (p * (dp - dd) * SCALE).astype(BF16)
                terms = (_dot_nn(ds, k_ab[ab]), _dot_tn(ds, q_ab[ab]), _dot_tn(p.astype(BF16), do_ab[ab]))
                dq, dk, dv = terms if dq is None else (dq + terms[0], dk + terms[1], dv + terms[2])
            dq_ref[:, lanes] = dq.astype(BF16)
            dkv_ref[:, lanes] += dk
            dkv_ref[:, vlanes] += dv

    return pl.pallas_call(
        body, name="mem_attn_bwd", grid=(bsz, nq),
        in_specs=[pl.BlockSpec((tq, MEM_W), lambda b, i: (b * nq + i, q_col)),
                  pl.BlockSpec((N_MEM, 2 * MEM_W), lambda b, i: (b, 0)),
                  pl.BlockSpec((tq, MEM_W), lambda b, i: (b * nq + i, do_col))],
        out_specs=[pl.BlockSpec((tq, MEM_W), lambda b, i: (b * nq + i, 0)),
                   pl.BlockSpec((N_MEM, 2 * MEM_W), lambda b, i: (b, 0))],
        out_shape=[_sds((bsz * seq, MEM_W), BF16), _sds((bsz * N_MEM, 2 * MEM_W), F32)],
        compiler_params=_params(("parallel", "arbitrary")))(h, mkv, dcat)


_GELU_C = math.sqrt(2.0 / math.pi)
_GELU_A = 0.044715


def _gelu(x):
    return 0.5 * x * (1.0 + jnp.tanh(_GELU_C * (x + _GELU_A * x * x * x)))


def _gelu_grad(x):
    th = jnp.tanh(_GELU_C * (x + _GELU_A * x * x * x))
    return 0.5 * (1.0 + th) + 0.5 * x * (1.0 - th * th) * (_GELU_C * (1.0 + 3.0 * _GELU_A * x * x))


def _tril_mask(lower):
    ri = lax.broadcasted_iota(jnp.int32, (BLK, BLK), 0)
    ci = lax.broadcasted_iota(jnp.int32, (BLK, BLK), 1)
    return (ri >= ci) if lower else (ci >= ri)


def _sgu_fwd(h, ws, bs_t, ln_g, ln_b, tm=512):
    t = h.shape[0]

    def body(u_ref, v_ref, ws_ref, bs_ref, g_ref, b_ref, o_ref):
        ug = _gelu(u_ref[...].astype(F32))
        vhat, _ = _ln_hat(_gelu(v_ref[...].astype(F32)))
        vn = (vhat * g_ref[...] + b_ref[...]).astype(BF16)
        mask = _tril_mask(True)
        first = lax.broadcasted_iota(jnp.int32, (1, 2 * HEAD_DIM), 1) < HEAD_DIM
        for pr in range(N_HEADS // 2):
            lanes = _lane_half(pr)
            w_ab = [jnp.where(mask, ws_ref[2 * pr + ab], 0).astype(BF16) for ab in range(2)]
            bias = jnp.where(first, bs_ref[:, 2 * pr:2 * pr + 1], bs_ref[:, 2 * pr + 1:2 * pr + 2])
            for c in range(tm // BLK):
                rs = slice(c * BLK, (c + 1) * BLK)
                v_ab = _split_pair(vn[rs, lanes])
                mixed = _dot_nn(w_ab[0], v_ab[0]) + _dot_nn(w_ab[1], v_ab[1]) + bias
                o_ref[rs, lanes] = (ug[rs, lanes] * mixed).astype(BF16)

    return pl.pallas_call(
        body, name="sgu_fwd", grid=(t // tm,),
        in_specs=[_rows(tm, MIX_W, 0), _rows(tm, MIX_W, 1), _whole(ws.shape), _whole(bs_t.shape), _whole(ln_g.shape), _whole(ln_b.shape)],
        out_specs=_rows(tm, MIX_W), out_shape=_sds((t, MIX_W), BF16),
        compiler_params=_params(("parallel",)))(h, h, ws, bs_t, ln_g, ln_b)


def _sgu_bwd(h, dcat, ws, ws_t, bs_t, ln_g, ln_b, tm=512):
    t = h.shape[0]

    def body(u_ref, v_ref, do_ref, ws_ref, wst_ref, bs_ref, g_ref, b_ref, dh_ref, dws_ref, dbs_ref, dg_ref, db_ref, dvn_ref):
        @pl.when(pl.program_id(0) == 0)
        def _():
            dws_ref[...] = jnp.zeros_like(dws_ref)
            dbs_ref[...] = jnp.zeros_like(dbs_ref)
            dg_ref[...] = jnp.zeros_like(dg_ref)
            db_ref[...] = jnp.zeros_like(db_ref)

        u = u_ref[...].astype(F32)
        v = v_ref[...].astype(F32)
        do = do_ref[...].astype(F32)
        ug = _gelu(u)
        vhat, rstd = _ln_hat(_gelu(v))
        vn = (vhat * g_ref[...] + b_ref[...]).astype(BF16)
        dmixed_f = do * ug
        dmixed = dmixed_f.astype(BF16)
        low, upp = _tril_mask(True), _tril_mask(False)
        first = lax.broadcasted_iota(jnp.int32, (1, 2 * HEAD_DIM), 1) < HEAD_DIM
        for pr in range(N_HEADS // 2):
            lanes = _lane_half(pr)
            w_ab = [jnp.where(low, ws_ref[2 * pr + ab], 0).astype(BF16) for ab in range(2)]
            wt_ab = [jnp.where(upp, wst_ref[2 * pr + ab], 0).astype(BF16) for ab in range(2)]
            bias = jnp.where(first, bs_ref[:, 2 * pr:2 * pr + 1], bs_ref[:, 2 * pr + 1:2 * pr + 2])
            dws_acc = [None, None]
            dbs_acc = [None, None]
            for c in range(tm // BLK):
                rs = slice(c * BLK, (c + 1) * BLK)
                vn_pair = vn[rs, lanes]
                v_ab = _split_pair(vn_pair)
                mixed = _dot_nn(w_ab[0], v_ab[0]) + _dot_nn(w_ab[1], v_ab[1]) + bias
                dh_ref[rs, lanes] = (do[rs, lanes] * mixed * _gelu_grad(u[rs, lanes])).astype(BF16)
                dm_ab = _split_pair(dmixed[rs, lanes])
                dmf_ab = _split_pair(dmixed_f[rs, lanes])
                for ab in range(2):
                    term = _dot_nt(dm_ab[ab], vn_pair)
                    dws_acc[ab] = term if dws_acc[ab] is None else dws_acc[ab] + term
                    rsum = jnp.sum(dmf_ab[ab], axis=-1, keepdims=True)
                    dbs_acc[ab] = rsum if dbs_acc[ab] is None else dbs_acc[ab] + rsum
                dvn_ref[rs, lanes] = _dot_nn(wt_ab[0], dm_ab[0]) + _dot_nn(wt_ab[1], dm_ab[1])
            for ab in range(2):
                g = 2 * pr + ab
                dws_ref[g] += jnp.where(low, dws_acc[ab], 0.0)
                dbs_ref[:, g:g + 1] += dbs_acc[ab]
        dvn = dvn_ref[...]
        dg_ref[...] += jnp.sum(dvn * vhat, axis=0, keepdims=True)
        db_ref[...] += jnp.sum(dvn, axis=0, keepdims=True)
        dxh = dvn * g_ref[...]
        m1 = jnp.mean(dxh, axis=-1, keepdims=True)
        m2 = jnp.mean(dxh * vhat, axis=-1, keepdims=True)
        dvg = rstd * (dxh - m1 - vhat * m2)
        dh_ref[:, MIX_W:] = (dvg * _gelu_grad(v)).astype(BF16)

    return pl.pallas_call(
        body, name="sgu_bwd", grid=(t // tm,),
        in_specs=[_rows(tm, MIX_W, 0), _rows(tm, MIX_W, 1), _rows(tm, MIX_W, 0), _whole(ws.shape), _whole(ws_t.shape),
                  _whole(bs_t.shape), _whole(ln_g.shape), _whole(ln_b.shape)],
        out_specs=[_rows(tm, 2 * MIX_W), _whole(ws.shape), _whole(bs_t.shape), _whole((1, MIX_W)), _whole((1, MIX_W))],
        out_shape=[_sds((t, 2 * MIX_W), BF16), _sds(ws.shape, F32), _sds(bs_t.shape, F32), _sds((1, MIX_W), F32), _sds((1, MIX_W), F32)],
        scratch_shapes=[pltpu.VMEM((tm, MIX_W), F32)],
        compiler_params=_params(("arbitrary",)))(h, h, dcat, ws, ws_t, bs_t, ln_g, ln_b)


def _row_tile(rows, cols, itemsize=4, limit=2 ** 20):
    best = rows
    for cand in (4096, 2048, 1024, 512, 256, 128, 64, 32, 16):
        if rows % cand == 0 and rows > cand:
            best = cand
            if cand * cols * itemsize <= limit:
                break
    return best


def _adamw(w, m, v, grad=None, parts=None, first_parts=None):
    rows, cols = w.shape
    rows0 = 0 if first_parts is None else first_parts.shape[1]
    tr = _row_tile(rows0 if rows0 else rows, cols)
    n0 = rows0 // tr

    def chip_sum(ref):
        acc = ref[0].astype(F32)
        for k in range(1, 4):
            acc = acc + ref[k].astype(F32)
        return acc

    def body(*refs):
        w_ref, m_ref, v_ref = refs[:3]
        go_ref, d_ref, nm_ref, nv_ref = refs[-4:]
        if parts is None:
            gv = refs[3][...]
        elif first_parts is None:
            gv = chip_sum(refs[3])
        else:
            gv = jnp.where(pl.program_id(0) < n0, chip_sum(refs[3]), chip_sum(refs[4]))
        nm = ADAM_B1 * m_ref[...] + (1.0 - ADAM_B1) * gv
        nv = ADAM_B2 * v_ref[...] + (1.0 - ADAM_B2) * (gv * gv)
        m_hat = nm / (1.0 - ADAM_B1 ** ADAM_STEP)
        v_hat = nv / (1.0 - ADAM_B2 ** ADAM_STEP)
        go_ref[...] = gv
        d_ref[...] = -ADAM_LR * (m_hat / (jnp.sqrt(v_hat) + ADAM_EPS) + ADAM_WD * w_ref[...])
        nm_ref[...] = nm
        nv_ref[...] = nv

    spec = _rows(tr, cols)
    if parts is None:
        g_specs, g_args = [spec], [grad]
    elif first_parts is None:
        g_specs, g_args = [pl.BlockSpec((4, tr, cols), lambda i: (0, i, 0))], [parts]
    else:
        g_specs = [pl.BlockSpec((4, tr, cols), lambda i: (0, jnp.minimum(i, n0 - 1), 0)),
                   pl.BlockSpec((4, tr, cols), lambda i: (0, jnp.maximum(i - n0, 0), 0))]
        g_args = [first_parts, parts]
    return pl.pallas_call(
        body, name="adamw" if parts is None else "adamw_sum_chips", grid=(rows // tr,), in_specs=[spec] * 3 + g_specs,
        out_specs=[spec] * 4, out_shape=[_sds(w.shape, F32)] * 4,
        compiler_params=_params(("parallel",)))(w, m, v, *g_args)


_ANY = pl.BlockSpec(memory_space=pl.ANY)
_MESH = pl.DeviceIdType.MESH


def _all_gather(name, blocks):
    nt = len(blocks)

    def body(*refs):
        start, finish = _gather_protocol(refs[:nt], refs[nt:2 * nt], *refs[2 * nt:])
        start()
        finish()

    return pl.pallas_call(
        body, name=name, out_shape=_gather_shapes(blocks), in_specs=[_ANY] * nt, out_specs=[_ANY] * nt,
        scratch_shapes=_gather_sems(nt))(*blocks)


def _gather_shapes(blocks):
    return [_sds((N_DEV,) + b.shape, b.dtype) for b in blocks]


def _gather_sems(nt):
    return [pltpu.SemaphoreType.DMA((nt, 7)), pltpu.SemaphoreType.DMA((nt, 7)), pltpu.SemaphoreType.DMA((nt,))]


def _gather_protocol(x_refs, out_refs, send_sems, recv_sems, local_sems):
    nt = len(x_refs)
    x, y, c = lax.axis_index("x"), lax.axis_index("y"), lax.axis_index("c")
    me, sibling = (x, y, c), (x, y, 1 - c)
    chips = [(1 - x, y), (x, 1 - y), (1 - x, 1 - y)]

    def slot(t, px, py, pc):
        return out_refs[t].at[4 * px + 2 * py + pc]

    def copy(t, k, blk, to, src=None):
        return pltpu.make_async_remote_copy(
            src_ref=slot(t, *blk) if src is None else src, dst_ref=slot(t, *blk),
            send_sem=send_sems.at[t, k], recv_sem=recv_sems.at[t, k], device_id=to, device_id_type=_MESH)

    def own_copies():
        mine = [pltpu.make_async_copy(x_refs[t], slot(t, *me), local_sems.at[t]) for t in range(nt)]
        first = []
        for t in range(nt):
            first.append(copy(t, 0, me, sibling, src=x_refs[t]))
            first += [copy(t, 1 + j, me, (*chip, c), src=x_refs[t]) for j, chip in enumerate(chips)]
        return mine, first

    def start():
        mine, first = own_copies()
        for cp in mine + first:
            cp.start()

    def finish():
        mine, first = own_copies()
        passed = []
        for j, chip in enumerate(chips):
            for t in range(nt):
                copy(t, 1 + j, (*chip, c), me).wait_recv()
                fwd = copy(t, 4 + j, (*chip, c), sibling)
                fwd.start()
                passed.append(fwd)
        for t in range(nt):
            copy(t, 0, sibling, me).wait_recv()
        for j, chip in enumerate(chips):
            for t in range(nt):
                copy(t, 4 + j, (*chip, 1 - c), me).wait_recv()
        for cp in first + passed:
            cp.wait_send()
        for cp in mine:
            cp.wait()

    return start, finish


def _swap_with_sibling(packed):
    nt = len(packed)

    def body(*refs):
        p_refs, got_refs = refs[:nt], refs[nt:2 * nt]
        send_sems, recv_sems = refs[2 * nt:]
        x, y, c = lax.axis_index("x"), lax.axis_index("y"), lax.axis_index("c")
        copies = [
            pltpu.make_async_remote_copy(
                src_ref=p_refs[t].at[1 - c], dst_ref=got_refs[t], send_sem=send_sems.at[t], recv_sem=recv_sems.at[t],
                device_id=(x, y, 1 - c), device_id_type=_MESH)
            for t in range(nt)]
        for cp in copies:
            cp.start()
        for cp in copies:
            cp.wait_recv()
        for cp in copies:
            cp.wait_send()

    return pl.pallas_call(
        body, name="grad_swap_sibling", out_shape=[_sds(p.shape[1:], p.dtype) for p in packed], in_specs=[_ANY] * nt,
        out_specs=[_ANY] * nt,
        scratch_shapes=[pltpu.SemaphoreType.DMA((nt,)), pltpu.SemaphoreType.DMA((nt,))])(*packed)


def _chip_sum(packed, got):
    _, nchip, rows, cols = packed.shape
    tr = _row_tile(rows, cols, 2)
    core = lax.axis_index("c").astype(jnp.int32).reshape(1)

    def body(c_ref, p_ref, g_ref, o_ref):
        o_ref[...] = (p_ref[...].astype(F32) + g_ref[...].astype(F32)).astype(o_ref.dtype)

    grid_spec = pltpu.PrefetchScalarGridSpec(
        num_scalar_prefetch=1, grid=(nchip, rows // tr),
        in_specs=[pl.BlockSpec((None, None, tr, cols), lambda k, i, c: (c[0], k, i, 0)),
                  pl.BlockSpec((None, tr, cols), lambda k, i, c: (k, i, 0))],
        out_specs=pl.BlockSpec((None, tr, cols), lambda k, i, c: (k, i, 0)))
    return pl.pallas_call(
        body, name="grad_chip_sum", grid_spec=grid_spec, out_shape=_sds(got.shape, got.dtype),
        compiler_params=_params(("parallel", "parallel")))(core, packed, got)


def _exchange_chips(chip_sums):
    nt = len(chip_sums)

    def body(*refs):
        start, finish = _exchange_protocol(refs[:nt], refs[nt:2 * nt], *refs[2 * nt:])
        start()
        finish()

    return pl.pallas_call(
        body, name="grad_exchange_chips", out_shape=[_sds(s.shape, s.dtype) for s in chip_sums], in_specs=[_ANY] * nt,
        out_specs=[_ANY] * nt, scratch_shapes=_exchange_sems(nt))(*chip_sums)


def _exchange_sems(nt):
    return [pltpu.SemaphoreType.DMA((nt, 3)), pltpu.SemaphoreType.DMA((nt, 3)), pltpu.SemaphoreType.DMA((nt,))]


def _exchange_protocol(s_refs, got_refs, send_sems, recv_sems, local_sems):
    nt = len(s_refs)
    x, y, c = lax.axis_index("x"), lax.axis_index("y"), lax.axis_index("c")
    my_chip = 2 * x + y
    chips = [(1 - x, y), (x, 1 - y), (1 - x, 1 - y)]

    def copy(t, j, src_chip, dst_chip):
        px, py = chips[j]
        return pltpu.make_async_remote_copy(
            src_ref=s_refs[t].at[src_chip], dst_ref=got_refs[t].at[dst_chip], send_sem=send_sems.at[t, j],
            recv_sem=recv_sems.at[t, j], device_id=(px, py, c), device_id_type=_MESH)

    def own_copies():
        mine = [pltpu.make_async_copy(s_refs[t].at[my_chip], got_refs[t].at[my_chip], local_sems.at[t]) for t in range(nt)]
        sends = [copy(t, j, 2 * px + py, my_chip) for t in range(nt) for j, (px, py) in enumerate(chips)]
        return mine, sends

    def start():
        mine, sends = own_copies()
        for cp in mine + sends:
            cp.start()

    def finish():
        mine, sends = own_copies()
        for j, (px, py) in enumerate(chips):
            for t in range(nt):
                copy(t, j, my_chip, 2 * px + py).wait_recv()
        for cp in sends:
            cp.wait_send()
        for cp in mine:
            cp.wait()

    return start, finish


def _sum_chips(got):
    _, rows, cols = got.shape
    tr = _row_tile(rows, cols)

    def body(g_ref, o_ref):
        acc = g_ref[0].astype(F32)
        for k in range(1, 4):
            acc = acc + g_ref[k].astype(F32)
        o_ref[...] = acc

    return pl.pallas_call(
        body, name="grad_sum_chips", grid=(rows // tr,), in_specs=[pl.BlockSpec((4, tr, cols), lambda i: (0, i, 0))],
        out_specs=pl.BlockSpec((tr, cols), lambda i: (i, 0)), out_shape=_sds((rows, cols), F32),
        compiler_params=_params(("parallel",)))(got)


_COL_SHARDED = ("a_w_in", "b_w_in", "w_gate", "w_up")
_ROW_SHARDED = ("w_mem_kv", "w_out", "w_down")
_BIG = ("a_w_in", "b_w_in", "w_mem_kv", "w_out", "w_gate", "w_up", "w_down")
_SGU_LN = ("sgu_ln_g", "sgu_ln_b")
_LN4 = ("ln_mix_g", "ln_mix_b", "ln_ffn_g", "ln_ffn_b")
_REPLICATED = ("sgu_w_s", "sgu_b_s") + _LN4


def _unshard(name, gathered):
    if name in _COL_SHARDED or name in _SGU_LN:
        moved = jnp.moveaxis(gathered, 0, -2)
        return moved.reshape(moved.shape[:-2] + (moved.shape[-2] * moved.shape[-1],))
    moved = jnp.moveaxis(gathered, 0, 1)
    return moved.reshape((moved.shape[0], moved.shape[1] * moved.shape[2]) + moved.shape[3:])


def _by_shard(name, full):
    if name in _COL_SHARDED or name in _SGU_LN:
        split = full.reshape(full.shape[:-1] + (N_DEV, full.shape[-1] // N_DEV))
        return jnp.moveaxis(split, -2, 0)
    split = full.reshape((full.shape[0], N_DEV, full.shape[1] // N_DEV) + full.shape[2:])
    return jnp.moveaxis(split, 1, 0)


def _layer_keys(i):
    return [("a_w_in" if i % 2 == 0 else "b_w_in", i // 2)] + [(n, i) for n in ("w_mem_kv", "w_out", "w_gate", "w_up", "w_down")]


_GATHER_FIRST = _layer_keys(0)[:2]
_GATHER_LATER = (_layer_keys(0)[2:] + _layer_keys(1), _layer_keys(2), _layer_keys(3))


def _shard_block(shards, key):
    name, idx = key
    return shards[name][idx:idx + 1].astype(BF16)


def _gather_first(shards):
    blocks = [_shard_block(shards, k) for k in _GATHER_FIRST] + [shards[n] for n in _SGU_LN]
    gathered = _all_gather("first_all_gather", blocks)
    full = {k: _unshard(k[0], g) for k, g in zip(_GATHER_FIRST, gathered)}
    sgu_ln = {n: _unshard(n, g) for n, g in zip(_SGU_LN, gathered[len(_GATHER_FIRST):])}
    return full, sgu_ln


def _two_level(by_dest):
    shp = by_dest.shape[1:]
    split = by_dest.astype(BF16).reshape((4, 2) + shp).swapaxes(0, 1)
    return split.reshape(2, 4, int(np.prod(shp[:-1])), shp[-1])


_EARLY = _BIG + _SGU_LN


def _chip_sums_of_early(grads):
    packed = [_two_level(_by_shard(n, jnp.stack(grads[n][1:] if n == "a_w_in" else grads[n]))) for n in _EARLY]
    ln4 = jnp.stack([jnp.stack(grads[n]) for n in _LN4])
    rep = [jnp.stack(grads["sgu_w_s"]).reshape(N_DEV, -1, BLK), jnp.stack(grads["sgu_b_s"]).reshape(N_DEV, -1, BLK),
           ln4.reshape(N_DEV, -1, D_MODEL)]
    packed += [_two_level(r) for r in rep]
    got = _swap_with_sibling(packed)
    return [_chip_sum(p, g) for p, g in zip(packed, got)]


def _finish_replicated(parts, shapes):
    w_s, b_s, ln_all = _all_gather("replicated_grads_all_gather", [_sum_chips(p) for p in parts])
    ln_all = ln_all.reshape((len(_LN4),) + tuple(shapes[_LN4[0]]))
    rep_grads = {"sgu_w_s": w_s.reshape(shapes["sgu_w_s"]), "sgu_b_s": b_s.reshape(shapes["sgu_b_s"])}
    rep_grads.update({n: ln_all[i] for i, n in enumerate(_LN4)})
    return rep_grads


def _reduce_last(grad_a_first):
    packed = [_two_level(_by_shard("a_w_in", grad_a_first))]
    got = _swap_with_sibling(packed)
    return _exchange_chips([_chip_sum(packed[0], got[0])])[0]


def _as_2d(a):
    if a.ndim == 1:
        return a.reshape(1, -1)
    return a.reshape(-1, a.shape[-1])


def kernel(x, mem, a_w_in, b_w_in, sgu_ln_g, sgu_ln_b, sgu_w_s, sgu_b_s, w_mem_kv, w_out, ln_mix_g, ln_mix_b, w_gate, w_up, w_down, ln_ffn_g, ln_ffn_b, loss_target, m_a_w_in, m_b_w_in, m_sgu_ln_g, m_sgu_ln_b, m_sgu_w_s, m_sgu_b_s, m_w_mem_kv, m_w_out, m_ln_mix_g, m_ln_mix_b, m_w_gate, m_w_up, m_w_down, m_ln_ffn_g, m_ln_ffn_b, v_a_w_in, v_b_w_in, v_sgu_ln_g, v_sgu_ln_b, v_sgu_w_s, v_sgu_b_s, v_w_mem_kv, v_w_out, v_ln_mix_g, v_ln_mix_b, v_w_gate, v_w_up, v_w_down, v_ln_ffn_g, v_ln_ffn_b):
    names = ("a_w_in", "b_w_in", "sgu_ln_g", "sgu_ln_b", "sgu_w_s", "sgu_b_s", "w_mem_kv", "w_out", "ln_mix_g", "ln_mix_b",
             "w_gate", "w_up", "w_down", "ln_ffn_g", "ln_ffn_b")
    weights = dict(zip(names, (a_w_in, b_w_in, sgu_ln_g, sgu_ln_b, sgu_w_s, sgu_b_s, w_mem_kv, w_out, ln_mix_g, ln_mix_b,
                               w_gate, w_up, w_down, ln_ffn_g, ln_ffn_b)))
    mom_m = dict(zip(names, (m_a_w_in, m_b_w_in, m_sgu_ln_g, m_sgu_ln_b, m_sgu_w_s, m_sgu_b_s, m_w_mem_kv, m_w_out, m_ln_mix_g,
                             m_ln_mix_b, m_w_gate, m_w_up, m_w_down, m_ln_ffn_g, m_ln_ffn_b)))
    mom_v = dict(zip(names, (v_a_w_in, v_b_w_in, v_sgu_ln_g, v_sgu_ln_b, v_sgu_w_s, v_sgu_b_s, v_w_mem_kv, v_w_out, v_ln_mix_g,
                             v_ln_mix_b, v_w_gate, v_w_up, v_w_down, v_ln_ffn_g, v_ln_ffn_b)))
    full, sgu_ln = _gather_first(weights)
    pending = [(keys, [_shard_block(weights, k) for k in keys]) for keys in _GATHER_LATER]
    loss_part, grad_x, local, early = _local_step(
        x, mem, loss_target, full, sgu_ln, {n: weights[n] for n in _REPLICATED}, pending)
    loss = lax.psum(loss_part[0, 0], ("x", "y", "c"))
    early_parts = dict(zip(_EARLY, early))
    rep_grads = _finish_replicated(early[len(_EARLY):], {n: weights[n].shape for n in _REPLICATED})
    a_first_parts = _reduce_last(local["a_w_in"][:1])

    reduced, deltas, new_m, new_v = {}, {}, {}, {}
    for n in names:
        w2, m2, v2 = _as_2d(weights[n]), _as_2d(mom_m[n]), _as_2d(mom_v[n])
        if n in early_parts:
            outs = _adamw(w2, m2, v2, parts=early_parts[n], first_parts=a_first_parts if n == "a_w_in" else None)
        else:
            outs = _adamw(w2, m2, v2, grad=_as_2d(rep_grads[n]))
        reduced[n], deltas[n], new_m[n], new_v[n] = (a.reshape(weights[n].shape) for a in outs)

    return (loss, grad_x, *[reduced[n] for n in names], *[deltas[n] for n in names],
            *[new_m[n] for n in names], *[new_v[n] for n in names])


def _local_step(x, mem, loss_target, full, sgu_ln, small, pending=None):
    sgu_w_s, sgu_b_s = small["sgu_w_s"], small["sgu_b_s"]
    ln_mix_g, ln_mix_b, ln_ffn_g, ln_ffn_b = (small[n] for n in ("ln_mix_g", "ln_mix_b", "ln_ffn_g", "ln_ffn_b"))
    bsz, seq, _ = x.shape
    tokens = bsz * seq
    slopes = _alibi_table()
    full = dict(full)
    exchanging = pending is not None
    pending = list(pending or [])

    def weight(name, idx):
        return full[(name, idx)][0]

    def next_group():
        return pending[0][1] if pending else None

    def landed(gathered):
        if gathered:
            keys, _ = pending.pop(0)
            full.update({k: _unshard(k[0], g) for k, g in zip(keys, gathered)})

    res = (x.reshape(tokens, D_MODEL),)
    xb = res[0].astype(BF16)
    memb = mem.reshape(bsz * N_MEM, D_MODEL).astype(BF16)
    tgt = loss_target.reshape(tokens, D_MODEL)

    saved = []
    for i in range(DEPTH):
        j = i // 2
        dil_layer = i % 2 == 0
        w_in = weight("a_w_in" if dil_layer else "b_w_in", j)
        mkv = _linear_nn("mem_kv", memb, weight("w_mem_kv", i))
        h = _linear_nn("in_proj_a" if dil_layer else "in_proj_b", xb, w_in)
        st = dict(xb=xb, h=h, mkv=mkv, w_in=w_in)
        if dil_layer:
            mix, st["lse"], gathered = _band_attn_fwd_fused(h, slopes, bsz, seq, gather=next_group() if i == 0 else None)
            landed(gathered)
            q_col = 3 * MIX_W // MEM_W
        else:
            st["ws"] = sgu_w_s[j]
            st["bs_t"] = sgu_b_s[j].T
            st["ln_g"] = sgu_ln["sgu_ln_g"][j].reshape(1, MIX_W)
            st["ln_b"] = sgu_ln["sgu_ln_b"][j].reshape(1, MIX_W)
            mix = _sgu_fwd(h, st["ws"], st["bs_t"], st["ln_g"], st["ln_b"])
            q_col = 2 * MIX_W // MEM_W
        mo = _mem_attn_fwd(h, mkv, bsz, seq, q_col)
        w_out, w_down = weight("w_out", i), weight("w_down", i)
        w_gu = jnp.concatenate([weight("w_gate", i), weight("w_up", i)], axis=-1)
        mix_ln = (ln_mix_g[i].reshape(1, D_MODEL), ln_mix_b[i].reshape(1, D_MODEL))
        ffn_ln = (ln_ffn_g[i].reshape(1, D_MODEL), ln_ffn_b[i].reshape(1, D_MODEL))
        r1, x1b = _proj_ln_fwd("out_proj_ln", [mix, mo], w_out, res, *mix_ln)
        gt, up, act, gathered = _ffn_up_fwd(x1b, w_gu, gather=next_group() if i < 2 else None)
        landed(gathered)
        r2, xb = _proj_ln_fwd("ffn_down_ln", [act], w_down, (r1, *mix_ln), *ffn_ln)
        res = (r2, *ffn_ln)
        st.update(mix=mix, mo=mo, q_col=q_col, r1=r1, x1b=x1b, gt=gt, up=up, act=act, r2=r2,
                  w_out=w_out, w_down=w_down, w_gu=w_gu)
        saved.append(st)

    dr2, dr2b, dg, db, loss_part = _loss_ln_bwd(*res, tgt)

    early_parts = None
    per_pair = ("a_w_in", "b_w_in", "sgu_ln_g", "sgu_ln_b", "sgu_w_s", "sgu_b_s")
    grads = {n: [None] * (DEPTH // 2 if n in per_pair else DEPTH) for n in _BIG + _SGU_LN + _REPLICATED}
    for i in reversed(range(DEPTH)):
        j = i // 2
        st = saved[i]
        dil_layer = i % 2 == 0
        w_in = st["w_in"]
        grads["ln_ffn_g"][i], grads["ln_ffn_b"][i] = dg[0], db[0]
        dgu = _ffn_down_bwd(dr2b, st["w_down"], st["gt"], st["up"])
        grads["w_down"][i] = _mm_tn("grad_w_down", st["act"], dr2b)
        dr1, dr1b, dg, db = _linear_nt("ffn_up_bwd", [dgu], st["w_gu"], dr2, F32,
                                       ln=(st["r1"], ln_mix_g[i].reshape(1, D_MODEL)))
        grads["ln_mix_g"][i], grads["ln_mix_b"][i] = dg[0], db[0]
        dw_gu = _mm_tn("grad_w_gate_up", st["x1b"], dgu)
        grads["w_gate"][i], grads["w_up"][i] = dw_gu[:, :D_FF], dw_gu[:, D_FF:]
        dcat = _linear_nt("out_proj_bwd", [dr1b], st["w_out"], None, BF16)
        grads["w_out"][i] = jnp.concatenate(
            [_mm_tn("grad_w_out_mix", st["mix"], dr1b), _mm_tn("grad_w_out_mem", st["mo"], dr1b)], axis=0)
        dqm, dmkv = _mem_attn_bwd(st["h"], st["mkv"], dcat, bsz, seq, st["q_col"])
        grads["w_mem_kv"][i] = _mm_tn("grad_w_mem_kv", memb, dmkv.astype(BF16))
        if dil_layer:
            early_sums = _chip_sums_of_early(grads) if (i == 0 and exchanging) else None
            dh_parts, exchanged = _band_attn_bwd_fused(st["h"], dcat, st["mix"], st["lse"], slopes, bsz, seq,
                                                       exchange=early_sums)
            if early_sums is not None:
                early_parts = exchanged
        else:
            ws_t = jnp.swapaxes(st["ws"], -1, -2)
            dh_main, dws, dbs_t, dlg, dlb = _sgu_bwd(st["h"], dcat, st["ws"], ws_t, st["bs_t"], st["ln_g"], st["ln_b"])
            grads["sgu_w_s"][j], grads["sgu_b_s"][j] = dws, dbs_t.T
            grads["sgu_ln_g"][j], grads["sgu_ln_b"][j] = dlg[0], dlb[0]
            dh_parts = [dh_main]
        name = "in_proj_bwd_a" if dil_layer else "in_proj_bwd_b"
        if i > 0:
            dr2, dr2b, dg, db = _linear_nt(name, [*dh_parts, dqm], w_in, dr1, F32,
                                           ln=(saved[i - 1]["r2"], ln_ffn_g[i - 1].reshape(1, D_MODEL)))
        else:
            grad_x = _linear_nt(name + "_x", [*dh_parts, dqm], w_in, dr1, F32).reshape(x.shape)
        grads["a_w_in" if dil_layer else "b_w_in"][j] = jnp.concatenate(
            [_mm_tn("grad_w_in_part", st["xb"], part) for part in dh_parts] + [_mm_tn("grad_w_in_qm", st["xb"], dqm)], axis=1)
    return loss_part, grad_x, {n: jnp.stack(g) for n, g in grads.items()}, early_parts
```

```python
import functools
import math

import numpy as np
import jax
import jax.numpy as jnp
from jax import lax
from jax.experimental import pallas as pl
from jax.experimental.pallas import tpu as pltpu

F32 = jnp.float32
BF16 = jnp.bfloat16

D_MODEL = 1024
DEPTH = 4
N_MEM = 256
HEAD_DIM = 64
N_HEADS = 12
MIX_W = N_HEADS * HEAD_DIM
MEM_W = 4 * HEAD_DIM
DIL_PATTERNS = ((128, 1), (512, 4), (2048, 16))
BLK = 128
HEAD_GROUP = 4
N_GROUPS = N_HEADS // HEAD_GROUP
D_FF = 2816
FF_CHUNKS = 2
ALPHA = (2 * DEPTH) ** 0.25
LN_EPS = 1e-5
SCALE = HEAD_DIM ** -0.5
NEG = -1e30
N_DEV = 8

ADAM_LR, ADAM_B1, ADAM_B2, ADAM_EPS, ADAM_WD, ADAM_STEP = 0.001, 0.9, 0.999, 1e-08, 0.01, 10

VMEM_LIMIT = 56 * 2 ** 20
STAT_LANES = 32
STAT_W = N_HEADS * STAT_LANES


def _dot_nn(a, b):
    return lax.dot_general(a, b, (((1,), (0,)), ((), ())), preferred_element_type=F32)


def _dot_nt(a, b):
    return lax.dot_general(a, b, (((1,), (1,)), ((), ())), preferred_element_type=F32)


def _dot_tn(a, b):
    return lax.dot_general(a, b, (((0,), (0,)), ((), ())), preferred_element_type=F32)


def _ln_hat(r):
    mu = jnp.mean(r, axis=-1, keepdims=True)
    xc = r - mu
    var = jnp.mean(xc * xc, axis=-1, keepdims=True)
    rstd = lax.rsqrt(var + LN_EPS)
    return xc * rstd, rstd


def _params(sem):
    return pltpu.CompilerParams(dimension_semantics=sem, vmem_limit_bytes=VMEM_LIMIT)


def _rows(tm, c, col=0):
    return pl.BlockSpec((tm, c), lambda i: (i, col))


def _whole(shape):
    nd = len(shape)
    return pl.BlockSpec(tuple(shape), lambda *_: (0,) * nd)


def _resident(shape):
    nd = len(shape)
    return pl.BlockSpec(tuple(shape), lambda *_: (0,) * nd, pipeline_mode=pl.Buffered(1))


def _sds(shape, dtype):
    return jax.ShapeDtypeStruct(tuple(shape), dtype)


def _linear_nn(name, a, w, tm=512):
    t, k = a.shape
    n = w.shape[1]
    tm = min(tm, t)

    def body(a_ref, w_ref, o_ref):
        o_ref[...] = _dot_nn(a_ref[...], w_ref[...]).astype(BF16)

    return pl.pallas_call(
        body, name=name, grid=(t // tm,), in_specs=[_rows(tm, k), _resident(w.shape)], out_specs=_rows(tm, n),
        out_shape=_sds((t, n), BF16), compiler_params=_params(("parallel",)))(a, w)


def _proj_ln_fwd(name, lhs, w, res, g, b, tm=512):
    t = res[0].shape[0]
    n_lhs = len(lhs)
    n_res = len(res)

    def body(*refs):
        lhs_refs = refs[:n_lhs]
        w_ref = refs[n_lhs]
        res_refs = refs[n_lhs + 1:n_lhs + 1 + n_res]
        g_ref, b_ref, r_ref, xnb_ref = refs[n_lhs + 1 + n_res:]
        y, off = None, 0
        for lr in lhs_refs:
            k = lr.shape[1]
            term = _dot_nn(lr[...], w_ref[off:off + k, :])
            y = term if y is None else y + term
            off += k
        x_res = res_refs[0][...]
        if n_res == 3:
            x_res = _ln_hat(x_res)[0] * res_refs[1][...] + res_refs[2][...]
        r = ALPHA * x_res + y
        r_ref[...] = r
        xnb_ref[...] = (_ln_hat(r)[0] * g_ref[...] + b_ref[...]).astype(BF16)

    vec = _whole((1, D_MODEL))
    in_specs = ([_rows(tm, a.shape[1]) for a in lhs] + [_resident(w.shape), _rows(tm, D_MODEL)] + [vec] * (n_res - 1) + [vec, vec])
    return pl.pallas_call(
        body, name=name, grid=(t // tm,), in_specs=in_specs, out_specs=[_rows(tm, D_MODEL)] * 2,
        out_shape=[_sds((t, D_MODEL), F32), _sds((t, D_MODEL), BF16)],
        compiler_params=_params(("parallel",)))(*lhs, w, *res, g, b)


def _ffn_up_fwd(xb, wgu, gather=None, tm=512):
    t = xb.shape[0]
    ng = 0 if gather is None else len(gather)
    steps = t // tm

    def body(*refs):
        x_ref, w_ref = refs[:2]
        g_ref, u_ref, a_ref = refs[2 + ng:5 + ng]
        if ng:
            start, finish = _gather_protocol(refs[2:2 + ng], refs[5 + ng:5 + 2 * ng], *refs[5 + 2 * ng:])
            pl.when(pl.program_id(0) == 0)(start)
        xv = x_ref[...]
        for c in range(FF_CHUNKS):
            cols = slice(c * D_FF // FF_CHUNKS, (c + 1) * D_FF // FF_CHUNKS)
            gt = _dot_nn(xv, w_ref[:, cols])
            up = _dot_nn(xv, w_ref[:, D_FF + cols.start:D_FF + cols.stop])
            g_ref[:, cols] = gt.astype(BF16)
            u_ref[:, cols] = up.astype(BF16)
            a_ref[:, cols] = (gt * jax.nn.sigmoid(gt) * up).astype(BF16)
        if ng:
            pl.when(pl.program_id(0) == steps - 1)(finish)

    outs = pl.pallas_call(
        body, name="ffn_up_fwd_gather" if ng else "ffn_up_fwd", grid=(steps,),
        in_specs=[_rows(tm, D_MODEL), _resident(wgu.shape)] + [_ANY] * ng,
        out_specs=[_rows(tm, D_FF)] * 3 + [_ANY] * ng,
        out_shape=[_sds((t, D_FF), BF16)] * 3 + (_gather_shapes(gather) if ng else []),
        scratch_shapes=_gather_sems(ng) if ng else [],
        compiler_params=_params(("arbitrary" if ng else "parallel",)))(xb, wgu, *(gather or []))
    return outs[0], outs[1], outs[2], list(outs[3:])


def _ln_bwd_rows(dxn, xhat, rstd, g_ref, dr_ref, drb_ref, dg_ref, db_ref):
    @pl.when(pl.program_id(0) == 0)
    def _():
        dg_ref[...] = jnp.zeros_like(dg_ref)
        db_ref[...] = jnp.zeros_like(db_ref)

    dxh = dxn * g_ref[...]
    m1 = jnp.mean(dxh, axis=-1, keepdims=True)
    m2 = jnp.mean(dxh * xhat, axis=-1, keepdims=True)
    dr = rstd * (dxh - m1 - xhat * m2)
    dr_ref[...] = dr
    drb_ref[...] = dr.astype(BF16)
    dg_ref[...] += jnp.sum(dxn * xhat, axis=0, keepdims=True)
    db_ref[...] += jnp.sum(dxn, axis=0, keepdims=True)


def _ln_bwd_outs(t, tm):
    vec = _whole((1, D_MODEL))
    specs = [_rows(tm, D_MODEL), _rows(tm, D_MODEL), vec, vec]
    shapes = [_sds((t, D_MODEL), F32), _sds((t, D_MODEL), BF16), _sds((1, D_MODEL), F32), _sds((1, D_MODEL), F32)]
    return specs, shapes


def _loss_ln_bwd(r, g, b, tgt, tm=512):
    t = r.shape[0]

    def body(r_ref, g_ref, b_ref, t_ref, dr_ref, drb_ref, dg_ref, db_ref, l_ref):
        @pl.when(pl.program_id(0) == 0)
        def _():
            l_ref[...] = jnp.zeros_like(l_ref)

        xhat, rstd = _ln_hat(r_ref[...])
        e = xhat * g_ref[...] + b_ref[...] - t_ref[...]
        l_ref[...] += jnp.sum(e * e) * (0.5 / D_MODEL)
        _ln_bwd_rows(e * (1.0 / D_MODEL), xhat, rstd, g_ref, dr_ref, drb_ref, dg_ref, db_ref)

    vec = _whole((1, D_MODEL))
    specs, shapes = _ln_bwd_outs(t, tm)
    return pl.pallas_call(
        body, name="loss_ln_bwd", grid=(t // tm,), in_specs=[_rows(tm, D_MODEL), vec, vec, _rows(tm, D_MODEL)],
        out_specs=specs + [_whole((1, 128))], out_shape=shapes + [_sds((1, 128), F32)],
        compiler_params=_params(("arbitrary",)))(r, g, b, tgt)


def _ffn_down_bwd(drb, wd, gt, up, tm=512):
    t = drb.shape[0]

    def body(d_ref, w_ref, g_ref, u_ref, o_ref):
        dv = d_ref[...]
        for c in range(FF_CHUNKS):
            cols = slice(c * D_FF // FF_CHUNKS, (c + 1) * D_FF // FF_CHUNKS)
            da = _dot_nt(dv, w_ref[cols, :])
            g = g_ref[:, cols].astype(F32)
            u = u_ref[:, cols].astype(F32)
            sg = jax.nn.sigmoid(g)
            o_ref[:, cols] = (da * u * (sg * (1.0 + g * (1.0 - sg)))).astype(BF16)
            o_ref[:, D_FF + cols.start:D_FF + cols.stop] = (da * (g * sg)).astype(BF16)

    return pl.pallas_call(
        body, name="ffn_down_bwd", grid=(t // tm,),
        in_specs=[_rows(tm, D_MODEL), _resident(wd.shape), _rows(tm, D_FF), _rows(tm, D_FF)],
        out_specs=_rows(tm, 2 * D_FF), out_shape=_sds((t, 2 * D_FF), BF16),
        compiler_params=_params(("parallel",)))(drb, wd, gt, up)


def _linear_nt(name, lhs, w, res, out_dtype, ln=None, tm=512):
    t = lhs[0].shape[0]
    n_lhs = len(lhs)
    n_out = w.shape[0]
    n_in = n_lhs + 1 + (res is not None) + (2 if ln else 0)

    def body(*refs):
        lhs_refs = refs[:n_lhs]
        w_ref = refs[n_lhs]
        y, off = None, 0
        for lr in lhs_refs:
            k = lr.shape[1]
            term = _dot_nt(lr[...], w_ref[:, off:off + k])
            y = term if y is None else y + term
            off += k
        if res is not None:
            y = ALPHA * refs[n_lhs + 1][...] + y
        if ln is None:
            refs[-1][...] = y.astype(out_dtype)
        else:
            r_ref, g_ref = refs[n_in - 2:n_in]
            xhat, rstd = _ln_hat(r_ref[...])
            _ln_bwd_rows(y, xhat, rstd, g_ref, *refs[n_in:])

    in_specs = [_rows(tm, a.shape[1]) for a in lhs] + [_resident(w.shape)]
    args = list(lhs) + [w]
    if res is not None:
        in_specs.append(_rows(tm, n_out))
        args.append(res)
    if ln is None:
        out_specs, out_shape, sem = _rows(tm, n_out), _sds((t, n_out), out_dtype), "parallel"
    else:
        in_specs += [_rows(tm, D_MODEL), _whole((1, D_MODEL))]
        args += list(ln)
        (out_specs, out_shape), sem = _ln_bwd_outs(t, tm), "arbitrary"
    return pl.pallas_call(
        body, name=name, grid=(t // tm,), in_specs=in_specs, out_specs=out_specs, out_shape=out_shape,
        compiler_params=_params((sem,)))(*args)


def _pick_tile(n, limit):
    if n <= limit:
        return n
    best = 128
    for cand in range(128, limit + 1, 128):
        if n % cand == 0:
            best = cand
    return best


def _mm_tn(name, a, b, tt=1024):
    t, k = a.shape
    n = b.shape[1]
    tt = min(tt, t)
    tk = _pick_tile(k, 1408)
    tn = _pick_tile(n, (6 * 2 ** 20) // (4 * tk) // 128 * 128)
    steps = t // tt

    def body(a_ref, b_ref, o_ref, acc_ref):
        @pl.when(pl.program_id(2) == 0)
        def _():
            acc_ref[...] = jnp.zeros_like(acc_ref)

        acc_ref[...] += _dot_tn(a_ref[...], b_ref[...])

        @pl.when(pl.program_id(2) == steps - 1)
        def _():
            o_ref[...] = acc_ref[...].astype(BF16)

    return pl.pallas_call(
        body, name=name, grid=(k // tk, n // tn, steps),
        in_specs=[pl.BlockSpec((tt, tk), lambda i, j, s: (s, i)), pl.BlockSpec((tt, tn), lambda i, j, s: (s, j))],
        out_specs=pl.BlockSpec((tk, tn), lambda i, j, s: (i, j)), out_shape=_sds((k, n), BF16),
        scratch_shapes=[pltpu.VMEM((tk, tn), F32)],
        compiler_params=_params(("parallel", "parallel", "arbitrary")))(a, b)


def _alibi_table():
    arr = np.zeros((N_GROUPS, 8, 128), np.float32)
    for g in range(N_GROUPS):
        for hh in range(HEAD_GROUP):
            arr[g, hh, :] = 2.0 ** (-8.0 * (g * HEAD_GROUP + hh + 1) / N_HEADS)
    return jnp.asarray(arr)


def _band_mask(n, dil):
    qi = lax.broadcasted_iota(jnp.int32, (BLK, 2 * BLK), 0)
    ki = lax.broadcasted_iota(jnp.int32, (BLK, 2 * BLK), 1)
    steps = qi + BLK - ki
    valid = (steps >= 0) & (steps <= BLK) & ((ki >= BLK) | (n > 0))
    return valid, (steps * dil).astype(F32)


def _band_specs(bsz, seq, dil, width):
    cb = width // 256

    def spec(off, prev=False):
        if prev:
            return pl.BlockSpec((None, BLK, 256), lambda b, r, g, n: (b, jnp.maximum(n - 1, 0), r * cb + off + g))
        return pl.BlockSpec((None, BLK, 256), lambda b, r, g, n: (b, n, r * cb + off + g))

    return spec


def _spread_stats(cols, per_head=STAT_LANES):
    lane = lax.broadcasted_iota(jnp.int32, (BLK, HEAD_GROUP * per_head), 1)
    tile = cols[HEAD_GROUP - 1]
    for hh in range(HEAD_GROUP - 2, -1, -1):
        tile = jnp.where(lane < (hh + 1) * per_head, cols[hh], tile)
    return tile


def _band_attn_fwd(h, slopes, bsz, seq, dil):
    width = h.shape[1]
    length = seq // dil
    nblk = length // BLK
    hv = h.reshape(bsz, length, dil * width)
    spec = _band_specs(bsz, seq, dil, width)
    k_off, v_off = MIX_W // 256, 2 * MIX_W // 256

    def body(sl_ref, q_ref, kc_ref, kp_ref, vc_ref, vp_ref, o_ref, lse_ref):
        valid, dist = _band_mask(pl.program_id(3), dil)
        q = q_ref[...]
        k2 = jnp.concatenate([kp_ref[...], kc_ref[...]], axis=0)
        v2 = jnp.concatenate([vp_ref[...], vc_ref[...]], axis=0)
        lses = []
        for hh in range(HEAD_GROUP):
            sl = slice(hh * HEAD_DIM, (hh + 1) * HEAD_DIM)
            s = _dot_nt(q[:, sl], k2[:, sl]) * SCALE - sl_ref[hh:hh + 1, 0:1] * dist
            s = jnp.where(valid, s, NEG)
            m = jnp.max(s, axis=-1, keepdims=True)
            p = jnp.exp(s - m)
            l = jnp.sum(p, axis=-1, keepdims=True)
            acc = _dot_nn(p.astype(BF16), v2[:, sl])
            o_ref[:, sl] = (acc / l).astype(BF16)
            lses.append(m + jnp.log(l))
        lse_ref[...] = _spread_stats(lses)

    out, lse = pl.pallas_call(
        body, name=f"band_attn_fwd_d{dil}", grid=(bsz, dil, N_GROUPS, nblk),
        in_specs=[pl.BlockSpec((None, 8, 128), lambda b, r, g, n: (g, 0, 0)),
                  spec(0), spec(k_off), spec(k_off, True), spec(v_off), spec(v_off, True)],
        out_specs=[pl.BlockSpec((None, BLK, 256), lambda b, r, g, n: (b, n, r * N_GROUPS + g)),
                   pl.BlockSpec((None, BLK, 128), lambda b, r, g, n: (b, n, r * N_GROUPS + g))],
        out_shape=[_sds((bsz, length, dil * MIX_W), BF16), _sds((bsz, length, dil * STAT_W), F32)],
        compiler_params=_params(("parallel", "parallel", "parallel", "arbitrary")))(slopes, hv, hv, hv, hv, hv)
    return out.reshape(bsz * seq, MIX_W), lse.reshape(bsz * seq, STAT_W)


def _band_merge(outs, lses, tm=512):
    t = outs[0].shape[0]

    def body(o1, o2, o3, l1, l2, l3, mix_ref, lse_ref):
        ls = [l1[...], l2[...], l3[...]]
        m = jnp.maximum(jnp.maximum(ls[0], ls[1]), ls[2])
        tot = m + jnp.log(jnp.exp(ls[0] - m) + jnp.exp(ls[1] - m) + jnp.exp(ls[2] - m))
        ws = [jnp.exp(x - tot) for x in ls]
        lse_ref[...] = tot
        for hd in range(N_HEADS):
            sl = slice(hd * HEAD_DIM, (hd + 1) * HEAD_DIM)
            acc = None
            for w, o in zip(ws, (o1, o2, o3)):
                term = w[:, hd * STAT_LANES:hd * STAT_LANES + 1] * o[:, sl].astype(F32)
                acc = term if acc is None else acc + term
            mix_ref[:, sl] = acc.astype(BF16)

    return pl.pallas_call(
        body, name="band_merge", grid=(t // tm,), in_specs=[_rows(tm, MIX_W)] * 3 + [_rows(tm, STAT_W)] * 3,
        out_specs=[_rows(tm, MIX_W), _rows(tm, STAT_W)], out_shape=[_sds((t, MIX_W), BF16), _sds((t, STAT_W), F32)],
        compiler_params=_params(("parallel",)))(*outs, *lses)


def _band_delta(dcat, mix, tm=512):
    t = mix.shape[0]

    def body(d_ref, o_ref, dd_ref):
        prod = d_ref[...].astype(F32) * o_ref[...].astype(F32)
        for hd in range(N_HEADS):
            rsum = jnp.sum(prod[:, hd * HEAD_DIM:(hd + 1) * HEAD_DIM], axis=-1, keepdims=True)
            dd_ref[:, hd * STAT_LANES:(hd + 1) * STAT_LANES] = jnp.broadcast_to(rsum, (tm, STAT_LANES))

    return pl.pallas_call(
        body, name="band_delta", grid=(t // tm,), in_specs=[_rows(tm, MIX_W), _rows(tm, MIX_W)],
        out_specs=_rows(tm, STAT_W), out_shape=_sds((t, STAT_W), F32),
        compiler_params=_params(("parallel",)))(dcat, mix)


def _band_attn_bwd(h, dcat, slopes, lse, delta, bsz, seq, dil):
    width = h.shape[1]
    length = seq // dil
    nblk = length // BLK
    hv = h.reshape(bsz, length, dil * width)
    dv_ = dcat.reshape(bsz, length, dil * D_MODEL)
    k_off, v_off = MIX_W // 256, 2 * MIX_W // 256
    cb, dcb = width // 256, D_MODEL // 256

    def cur(off, c):
        return pl.BlockSpec((None, BLK, 256), lambda b, r, g, n: (b, jnp.minimum(n, nblk - 1), r * c + off + g))

    def prev(off, c):
        return pl.BlockSpec((None, BLK, 256), lambda b, r, g, n: (b, jnp.maximum(jnp.minimum(n, nblk - 1) - 1, 0), r * c + off + g))

    stat = pl.BlockSpec((None, BLK, 128), lambda b, r, g, n: (b, jnp.minimum(n, nblk - 1), r * N_GROUPS + g))
    dq_spec = pl.BlockSpec((None, BLK, 256), lambda b, r, g, n: (b, jnp.minimum(n, nblk - 1), r * N_GROUPS + g))
    dkv_spec = pl.BlockSpec((None, BLK, 256), lambda b, r, g, n: (b, jnp.maximum(n - 1, 0), r * N_GROUPS + g))

    def body(sl_ref, q_ref, kc_ref, kp_ref, vc_ref, vp_ref, do_ref, lse_ref, dd_ref, dq_ref, dk_ref, dv_ref, kcar, vcar):
        n = pl.program_id(3)

        @pl.when(n == 0)
        def _():
            kcar[...] = jnp.zeros_like(kcar)
            vcar[...] = jnp.zeros_like(vcar)

        @pl.when(n < nblk)
        def _():
            valid, dist = _band_mask(n, dil)
            q = q_ref[...]
            do = do_ref[...]
            k2 = jnp.concatenate([kp_ref[...], kc_ref[...]], axis=0)
            v2 = jnp.concatenate([vp_ref[...], vc_ref[...]], axis=0)
            for hh in range(HEAD_GROUP):
                sl = slice(hh * HEAD_DIM, (hh + 1) * HEAD_DIM)
                s = _dot_nt(q[:, sl], k2[:, sl]) * SCALE - sl_ref[hh:hh + 1, 0:1] * dist
                s = jnp.where(valid, s, NEG)
                st = slice(hh * STAT_LANES, hh * STAT_LANES + 1)
                p = jnp.exp(s - lse_ref[:, st])
                dp = _dot_nt(do[:, sl], v2[:, sl])
                ds = (p * (dp - dd_ref[:, st]) * SCALE).astype(BF16)
                dq_ref[:, sl] = _dot_nn(ds, k2[:, sl]).astype(BF16)
                dk2 = _dot_tn(ds, q[:, sl])
                dv2 = _dot_tn(p.astype(BF16), do[:, sl])
                dk_ref[:, sl] = (kcar[:, sl] + dk2[:BLK]).astype(BF16)
                dv_ref[:, sl] = (vcar[:, sl] + dv2[:BLK]).astype(BF16)
                kcar[:, sl] = dk2[BLK:]
                vcar[:, sl] = dv2[BLK:]

        @pl.when(n == nblk)
        def _():
            dk_ref[...] = kcar[...].astype(BF16)
            dv_ref[...] = vcar[...].astype(BF16)

    outs = pl.pallas_call(
        body, name=f"band_attn_bwd_d{dil}", grid=(bsz, dil, N_GROUPS, nblk + 1),
        in_specs=[pl.BlockSpec((None, 8, 128), lambda b, r, g, n: (g, 0, 0)),
                  cur(0, cb), cur(k_off, cb), prev(k_off, cb), cur(v_off, cb), prev(v_off, cb), cur(0, dcb), stat, stat],
        out_specs=[dq_spec, dkv_spec, dkv_spec],
        out_shape=[_sds((bsz, length, dil * MIX_W), BF16)] * 3,
        scratch_shapes=[pltpu.VMEM((BLK, 256), F32), pltpu.VMEM((BLK, 256), F32)],
        compiler_params=_params(("parallel", "parallel", "parallel", "arbitrary")))(
            slopes, hv, hv, hv, hv, hv, dv_, lse.reshape(bsz, length, dil * STAT_W), delta.reshape(bsz, length, dil * STAT_W))
    return [o.reshape(bsz * seq, MIX_W) for o in outs]


def _sum_patterns(parts, tm=512):
    t = parts[0][0].shape[0]

    def body(*refs):
        o_ref = refs[-1]
        for j in range(3):
            acc = refs[j][...].astype(F32) + refs[3 + j][...].astype(F32) + refs[6 + j][...].astype(F32)
            o_ref[:, j * MIX_W:(j + 1) * MIX_W] = acc.astype(BF16)

    flat = [x for p in parts for x in p]
    return pl.pallas_call(
        body, name="band_sum", grid=(t // tm,), in_specs=[_rows(tm, MIX_W)] * 9, out_specs=_rows(tm, 3 * MIX_W),
        out_shape=_sds((t, 3 * MIX_W), BF16), compiler_params=_params(("parallel",)))(*flat)


def _block_mask(has_prev, dil):
    if has_prev is None:
        steps = lax.broadcasted_iota(jnp.int32, (BLK, BLK), 0) - lax.broadcasted_iota(jnp.int32, (BLK, BLK), 1)
        return steps >= 0, (steps * dil).astype(F32)
    qi = lax.broadcasted_iota(jnp.int32, (BLK, 2 * BLK), 0)
    ki = lax.broadcasted_iota(jnp.int32, (BLK, 2 * BLK), 1)
    steps = qi + BLK - ki
    valid = (steps >= 0) & (steps <= BLK) & ((ki >= BLK) | has_prev)
    return valid, (steps * dil).astype(F32)


def _bias_scratch():
    return pltpu.VMEM((2, HEAD_GROUP, BLK, 2 * BLK), F32)


def _fill_bias(bias, sl_ref, dil):
    for p in range(2):
        valid, dist = _block_mask(p == 1, dil)
        for hh in range(HEAD_GROUP):
            bias[p, hh] = jnp.where(valid, -sl_ref[hh:hh + 1, 0:1] * dist, NEG)


def _rows_of(j):
    return pl.ds(pl.multiple_of(j * BLK, BLK), BLK)


def _lane_half(hf):
    return slice(hf * 128, (hf + 1) * 128)


def _split_pair(x):
    first = lax.broadcasted_iota(jnp.int32, (1, 2 * HEAD_DIM), 1) < HEAD_DIM
    zero = jnp.zeros_like(x)
    return jnp.where(first, x, zero), jnp.where(first, zero, x)


def _deinterleave(src, dst, seq, dil, dtype):
    length = seq // dil
    for r in range(dil):
        for c in range(length // BLK):
            rows = pl.ds(r + c * BLK * dil, BLK, stride=dil)
            out = slice(r * length + c * BLK, r * length + (c + 1) * BLK)
            if len(src.shape) == 2:
                dst[out, :] = src[rows, :].astype(dtype)
            else:
                for hf in range(2):
                    dst[out, _lane_half(hf)] = src.at[hf][rows, :].astype(dtype)


def _interleave(src, dst, seq, dil, accumulate):
    length = seq // dil
    for r in range(dil):
        for c in range(length // BLK):
            rows = pl.ds(r + c * BLK * dil, BLK, stride=dil)
            inp = slice(r * length + c * BLK, r * length + (c + 1) * BLK)
            if len(dst.shape) == 2:
                dst[rows, :] = dst[rows, :] + src[inp, :] if accumulate else src[inp, :]
            else:
                for hf in range(2):
                    val = src[inp, _lane_half(hf)]
                    half = dst.at[hf]
                    half[rows, :] = half[rows, :] + val if accumulate else val


def _split_halves(src, dst, seq):
    def step(i, carry):
        for hf in range(2):
            dst[hf, _rows_of(i), :] = src[_rows_of(i), _lane_half(hf)].astype(F32)
        return carry

    lax.fori_loop(0, seq // BLK, step, 0)


def _band_attn_fwd_fused(h, slopes, bsz, seq, gather=None):
    width = h.shape[1]
    cb = width // 256
    k_off, v_off = MIX_W // 256, 2 * MIX_W // 256
    nb = seq // BLK

    ng = 0 if gather is None else len(gather)

    def body(*refs):
        sl_ref, q_ref, k_ref, v_ref = refs[:4]
        mix_ref, lse_ref = refs[4 + ng:6 + ng]
        qf, kf, vf, qd, kd, vd, od, ld, o1, o2, o3, l1, l2, l3, bias = refs[6 + 2 * ng:21 + 2 * ng]
        if ng:
            start, finish = _gather_protocol(refs[4:4 + ng], refs[6 + ng:6 + 2 * ng], *refs[21 + 2 * ng:])
            pl.when((pl.program_id(0) == 0) & (pl.program_id(1) == 0))(start)

        def run(dil, qs, ks, vs, o_dst, l_dst):
            nblk = seq // dil // BLK
            _fill_bias(bias, sl_ref, dil)

            def block(j, carry):
                rows, prows = _rows_of(j), _rows_of(jnp.maximum(j - 1, 0))
                has_prev = ((j % nblk) != 0).astype(jnp.int32)

                def keys(ref, lanes):
                    return jnp.concatenate([ref[prows, lanes], ref[rows, lanes]], axis=0)

                lses = []
                for pr in range(HEAD_GROUP // 2):
                    lanes = _lane_half(pr)
                    q_ab = _split_pair(qs[rows, lanes] * SCALE)
                    k2 = keys(ks, lanes)
                    v_ab = _split_pair(keys(vs, lanes))
                    out = None
                    for ab in range(2):
                        hh = 2 * pr + ab
                        s = _dot_nt(q_ab[ab], k2) + bias[has_prev, hh]
                        m = jnp.max(s, axis=-1, keepdims=True)
                        p = jnp.exp(s - m)
                        l = jnp.sum(p, axis=-1, keepdims=True)
                        term = _dot_nn(p.astype(BF16), v_ab[ab]) / l
                        out = term if out is None else out + term
                        lses.append(m + jnp.log(l))
                    o_dst[rows, lanes] = out
                l_dst[rows, :] = _spread_stats(lses, HEAD_DIM)
                return carry

            lax.fori_loop(0, nb, block, 0, unroll=8)

        run(1, q_ref, k_ref, v_ref, o1, l1)
        _split_halves(q_ref, qf, seq)
        _split_halves(k_ref, kf, seq)
        _split_halves(v_ref, vf, seq)
        for dil, o_tok, l_tok in ((4, o2, l2), (16, o3, l3)):
            _deinterleave(qf, qd, seq, dil, BF16)
            _deinterleave(kf, kd, seq, dil, BF16)
            _deinterleave(vf, vd, seq, dil, BF16)
            run(dil, qd, kd, vd, od, ld)
            _interleave(od, o_tok, seq, dil, False)
            _interleave(ld, l_tok, seq, dil, False)

        def merge(i, carry):
            rows = _rows_of(i)

            def both(ref):
                return jnp.concatenate([ref[0, rows, :], ref[1, rows, :]], axis=1)

            ls = [l1[rows, :], both(l2), both(l3)]
            m = jnp.maximum(jnp.maximum(ls[0], ls[1]), ls[2])
            tot = m + jnp.log(jnp.exp(ls[0] - m) + jnp.exp(ls[1] - m) + jnp.exp(ls[2] - m))
            ws = [jnp.exp(x - tot) for x in ls]
            mix_ref[rows, :] = (ws[0] * o1[rows, :] + ws[1] * both(o2) + ws[2] * both(o3)).astype(BF16)
            lse_ref[rows, :] = _spread_stats([tot[:, hh * HEAD_DIM:hh * HEAD_DIM + 1] for hh in range(HEAD_GROUP)])
            return carry

        lax.fori_loop(0, nb, merge, 0)
        if ng:
            pl.when((pl.program_id(0) == bsz - 1) & (pl.program_id(1) == N_GROUPS - 1))(finish)

    def hspec(off):
        return pl.BlockSpec((seq, 256), lambda b, g: (b, off + g))

    big = lambda dt: pltpu.VMEM((seq, 256), dt)
    halves = lambda: pltpu.VMEM((2, seq, 128), F32)
    stat = lambda: pltpu.VMEM((seq, 128), F32)
    outs = pl.pallas_call(
        body, name="band_attn_fwd_gather" if ng else "band_attn_fwd", grid=(bsz, N_GROUPS),
        in_specs=[pl.BlockSpec((None, 8, 128), lambda b, g: (g, 0, 0)), hspec(0), hspec(k_off), hspec(v_off)] + [_ANY] * ng,
        out_specs=[pl.BlockSpec((seq, 256), lambda b, g: (b, g)), pl.BlockSpec((seq, 128), lambda b, g: (b, g))] + [_ANY] * ng,
        out_shape=[_sds((bsz * seq, MIX_W), BF16), _sds((bsz * seq, STAT_W), F32)] + (_gather_shapes(gather) if ng else []),
        scratch_shapes=[halves(), halves(), halves(), big(BF16), big(BF16), big(BF16), big(F32), big(F32),
                        big(F32), halves(), halves(), big(F32), halves(), halves(), _bias_scratch()]
        + (_gather_sems(ng) if ng else []),
        compiler_params=_params(("arbitrary", "arbitrary")))(slopes, h, h, h, *(gather or []))
    return outs[0], outs[1], list(outs[2:])


def _band_attn_bwd_fused(h, dcat, mix, lse, slopes, bsz, seq, exchange=None):
    width = h.shape[1]
    k_off, v_off = MIX_W // 256, 2 * MIX_W // 256
    nb = seq // BLK

    ne = 0 if exchange is None else len(exchange)

    def body(*refs):
        sl_ref, q_ref, k_ref, v_ref, do_ref, o_ref, lse_ref = refs[:7]
        dq_ref, dk_ref, dv_ref = refs[7 + ne:10 + ne]
        qf, kf, vf, dof, ddt, qd, kd, vd, dod, lsd, ddd, gq, gk, gv, aq, ak, av, bias = refs[10 + 2 * ne:28 + 2 * ne]
        if ne:
            start, finish = _exchange_protocol(refs[7:7 + ne], refs[10 + ne:10 + 2 * ne], *refs[28 + 2 * ne:])
            pl.when((pl.program_id(0) == 0) & (pl.program_id(1) == 0))(start)

        same_head = (lax.broadcasted_iota(jnp.int32, (HEAD_GROUP * HEAD_DIM, HEAD_GROUP * STAT_LANES), 0) // HEAD_DIM
                     == lax.broadcasted_iota(jnp.int32, (HEAD_GROUP * HEAD_DIM, HEAD_GROUP * STAT_LANES), 1) // STAT_LANES)
        ones_map = jnp.where(same_head, 1.0, 0.0).astype(BF16)

        def delta(i, carry):
            rows = _rows_of(i)
            prod = do_ref[rows, :].astype(F32) * o_ref[rows, :].astype(F32)
            high = prod.astype(BF16)
            rest = (prod - high.astype(F32)).astype(BF16)
            ddt[rows, :] = _dot_nn(high, ones_map) + _dot_nn(rest, ones_map)
            return carry

        lax.fori_loop(0, nb, delta, 0)

        def zero(i, carry):
            rows = _rows_of(i)
            for ref in (gk, gv):
                ref[rows, :] = jnp.zeros((BLK, 256), F32)
            return carry

        def run(dil, qs, ks, vs, dos, lss, dds):
            nblk = seq // dil // BLK
            _fill_bias(bias, sl_ref, dil)
            if nblk > 1:
                lax.fori_loop(0, nb, zero, 0)

            def block(j, carry):
                rows, prows = _rows_of(j), _rows_of(jnp.maximum(j - 1, 0))
                has_prev = ((j % nblk) != 0).astype(jnp.int32)

                def keys(ref, lanes):
                    if nblk == 1:
                        return ref[rows, lanes]
                    return jnp.concatenate([ref[prows, lanes], ref[rows, lanes]], axis=0)

                for pr in range(HEAD_GROUP // 2):
                    lanes = _lane_half(pr)
                    q_ab = _split_pair(qs[rows, lanes] * SCALE)
                    do_ab = _split_pair(dos[rows, lanes])
                    k2, v2 = keys(ks, lanes), keys(vs, lanes)
                    k_ab = _split_pair(k2)
                    dq, dk2, dv2 = None, None, None
                    for ab in range(2):
                        hh = 2 * pr + ab
                        st = slice(hh * STAT_LANES, hh * STAT_LANES + 1)
                        s = _dot_nt(q_ab[ab], k2) + (bias[0, hh, :, BLK:] if nblk == 1 else bias[has_prev, hh])
                        p = jnp.exp(s - lss[rows, st])
                        dp = _dot_nt(do_ab[ab], v2)
                        ds = (p * (dp - dds[rows, st])).astype(BF16)
                        terms = (_dot_nn(ds, k_ab[ab]), _dot_tn(ds, q_ab[ab]), _dot_tn(p.astype(BF16), do_ab[ab]))
                        dq, dk2, dv2 = terms if dq is None else (dq + terms[0], dk2 + terms[1], dv2 + terms[2])
                    gq[rows, lanes] = dq * SCALE
                    if nblk == 1:
                        gk[rows, lanes] = dk2
                        gv[rows, lanes] = dv2
                    else:
                        gk[prows, lanes] += dk2[:BLK]
                        gv[prows, lanes] += dv2[:BLK]
                        gk[rows, lanes] += dk2[BLK:]
                        gv[rows, lanes] += dv2[BLK:]
                return carry

            lax.fori_loop(0, nb, block, 0, unroll=4)

        run(1, q_ref, k_ref, v_ref, do_ref, lse_ref, ddt)

        for src, dst in ((gq, aq), (gk, ak), (gv, av), (q_ref, qf), (k_ref, kf), (v_ref, vf), (do_ref, dof)):
            _split_halves(src, dst, seq)
        for dil in (4, 16):
            for src, dst in ((qf, qd), (kf, kd), (vf, vd), (dof, dod)):
                _deinterleave(src, dst, seq, dil, BF16)
            _deinterleave(lse_ref, lsd, seq, dil, F32)
            _deinterleave(ddt, ddd, seq, dil, F32)
            run(dil, qd, kd, vd, dod, lsd, ddd)
            for src, dst in ((gq, aq), (gk, ak), (gv, av)):
                _interleave(src, dst, seq, dil, True)

        def write(i, carry):
            rows = _rows_of(i)
            for src, dst in ((aq, dq_ref), (ak, dk_ref), (av, dv_ref)):
                for hf in range(2):
                    dst[rows, _lane_half(hf)] = src[hf, rows, :].astype(BF16)
            return carry

        lax.fori_loop(0, nb, write, 0)
        if ne:
            pl.when((pl.program_id(0) == bsz - 1) & (pl.program_id(1) == N_GROUPS - 1))(finish)

    def hspec(off):
        return pl.BlockSpec((seq, 256), lambda b, g: (b, off + g))

    io = pl.BlockSpec((seq, 256), lambda b, g: (b, g))
    big = lambda dt: pltpu.VMEM((seq, 256), dt)
    halves = lambda: pltpu.VMEM((2, seq, 128), F32)
    stat = lambda: pltpu.VMEM((seq, 128), F32)
    outs = pl.pallas_call(
        body, name="band_attn_bwd_exchange" if ne else "band_attn_bwd", grid=(bsz, N_GROUPS),
        in_specs=[pl.BlockSpec((None, 8, 128), lambda b, g: (g, 0, 0)), hspec(0), hspec(k_off), hspec(v_off), io, io,
                  pl.BlockSpec((seq, 128), lambda b, g: (b, g))] + [_ANY] * ne,
        out_specs=[io, io, io] + [_ANY] * ne,
        out_shape=[_sds((bsz * seq, MIX_W), BF16)] * 3 + [_sds(s.shape, s.dtype) for s in (exchange or [])],
        scratch_shapes=[halves(), halves(), halves(), halves(), stat(),
                        big(BF16), big(BF16), big(BF16), big(BF16), stat(), stat(),
                        big(F32), big(F32), big(F32), halves(), halves(), halves(), _bias_scratch()]
        + (_exchange_sems(ne) if ne else []),
        compiler_params=_params(("arbitrary", "arbitrary")))(slopes, h, h, h, dcat, mix, lse, *(exchange or []))
    return list(outs[:3]), list(outs[3:])


def _mem_attn_fwd(h, mkv, bsz, seq, q_col, tq=512):
    nq = seq // tq

    def body(q_ref, kv_ref, o_ref):
        for pr in range(2):
            lanes = _lane_half(pr)
            q_ab = _split_pair(q_ref[:, lanes])
            k = kv_ref[:, lanes]
            v_ab = _split_pair(kv_ref[:, MEM_W + pr * 128:MEM_W + (pr + 1) * 128])
            out = None
            for ab in range(2):
                s = _dot_nt(q_ab[ab], k) * SCALE
                m = jnp.max(s, axis=-1, keepdims=True)
                p = jnp.exp(s - m)
                l = jnp.sum(p, axis=-1, keepdims=True)
                term = _dot_nn(p.astype(BF16), v_ab[ab]) / l
                out = term if out is None else out + term
            o_ref[:, lanes] = out.astype(BF16)

    return pl.pallas_call(
        body, name="mem_attn_fwd", grid=(bsz, nq),
        in_specs=[pl.BlockSpec((tq, MEM_W), lambda b, i: (b * nq + i, q_col)),
                  pl.BlockSpec((N_MEM, 2 * MEM_W), lambda b, i: (b, 0))],
        out_specs=pl.BlockSpec((tq, MEM_W), lambda b, i: (b * nq + i, 0)),
        out_shape=_sds((bsz * seq, MEM_W), BF16), compiler_params=_params(("parallel", "parallel")))(h, mkv)


def _mem_attn_bwd(h, mkv, dcat, bsz, seq, q_col, tq=512):
    nq = seq // tq
    do_col = MIX_W // MEM_W

    def body(q_ref, kv_ref, do_ref, dq_ref, dkv_ref):
        @pl.when(pl.program_id(1) == 0)
        def _():
            dkv_ref[...] = jnp.zeros_like(dkv_ref)

        for pr in range(2):
            lanes = _lane_half(pr)
            vlanes = slice(MEM_W + pr * 128, MEM_W + (pr + 1) * 128)
            q_ab = _split_pair(q_ref[:, lanes])
            do_ab = _split_pair(do_ref[:, lanes])
            k, v = kv_ref[:, lanes], kv_ref[:, vlanes]
            k_ab = _split_pair(k)
            dq, dk, dv = None, None, None
            for ab in range(2):
                s = _dot_nt(q_ab[ab], k) * SCALE
                m = jnp.max(s, axis=-1, keepdims=True)
                e = jnp.exp(s - m)
                p = e / jnp.sum(e, axis=-1, keepdims=True)
                dp = _dot_nt(do_ab[ab], v)
                dd = jnp.sum(p * dp, axis=-1, keepdims=True)
                ds = (p * (dp - dd) * SCALE).astype(BF16)
                terms = (_dot_nn(ds, k_ab[ab]), _dot_tn(ds, q_ab[ab]), _dot_tn(p.astype(BF16), do_ab[ab]))
                dq, dk, dv = terms if dq is None else (dq + terms[0], dk + terms[1], dv + terms[2])
            dq_ref[:, lanes] = dq.astype(BF16)
            dkv_ref[:, lanes] += dk
            dkv_ref[:, vlanes] += dv

    return pl.pallas_call(
        body, name="mem_attn_bwd", grid=(bsz, nq),
        in_specs=[pl.BlockSpec((tq, MEM_W), lambda b, i: (b * nq + i, q_col)),
                  pl.BlockSpec((N_MEM, 2 * MEM_W), lambda b, i: (b, 0)),
                  pl.BlockSpec((tq, MEM_W), lambda b, i: (b * nq + i, do_col))],
        out_specs=[pl.BlockSpec((tq, MEM_W), lambda b, i: (b * nq + i, 0)),
                   pl.BlockSpec((N_MEM, 2 * MEM_W), lambda b, i: (b, 0))],
        out_shape=[_sds((bsz * seq, MEM_W), BF16), _sds((bsz * N_MEM, 2 * MEM_W), F32)],
        compiler_params=_params(("parallel", "arbitrary")))(h, mkv, dcat)


_GELU_C = math.sqrt(2.0 / math.pi)
_GELU_A = 0.044715


def _gelu(x):
    return 0.5 * x * (1.0 + jnp.tanh(_GELU_C * (x + _GELU_A * x * x * x)))


def _gelu_grad(x):
    th = jnp.tanh(_GELU_C * (x + _GELU_A * x * x * x))
    return 0.5 * (1.0 + th) + 0.5 * x * (1.0 - th * th) * (_GELU_C * (1.0 + 3.0 * _GELU_A * x * x))


def _tril_mask(lower):
    ri = lax.broadcasted_iota(jnp.int32, (BLK, BLK), 0)
    ci = lax.broadcasted_iota(jnp.int32, (BLK, BLK), 1)
    return (ri >= ci) if lower else (ci >= ri)


def _sgu_fwd(h, ws, bs_t, ln_g, ln_b, tm=512):
    t = h.shape[0]

    def body(u_ref, v_ref, ws_ref, bs_ref, g_ref, b_ref, o_ref):
        ug = _gelu(u_ref[...].astype(F32))
        vhat, _ = _ln_hat(_gelu(v_ref[...].astype(F32)))
        vn = (vhat * g_ref[...] + b_ref[...]).astype(BF16)
        mask = _tril_mask(True)
        first = lax.broadcasted_iota(jnp.int32, (1, 2 * HEAD_DIM), 1) < HEAD_DIM
        for pr in range(N_HEADS // 2):
            lanes = _lane_half(pr)
            w_ab = [jnp.where(mask, ws_ref[2 * pr + ab], 0).astype(BF16) for ab in range(2)]
            bias = jnp.where(first, bs_ref[:, 2 * pr:2 * pr + 1], bs_ref[:, 2 * pr + 1:2 * pr + 2])
            for c in range(tm // BLK):
                rs = slice(c * BLK, (c + 1) * BLK)
                v_ab = _split_pair(vn[rs, lanes])
                mixed = _dot_nn(w_ab[0], v_ab[0]) + _dot_nn(w_ab[1], v_ab[1]) + bias
                o_ref[rs, lanes] = (ug[rs, lanes] * mixed).astype(BF16)

    return pl.pallas_call(
        body, name="sgu_fwd", grid=(t // tm,),
        in_specs=[_rows(tm, MIX_W, 0), _rows(tm, MIX_W, 1), _whole(ws.shape), _whole(bs_t.shape), _whole(ln_g.shape), _whole(ln_b.shape)],
        out_specs=_rows(tm, MIX_W), out_shape=_sds((t, MIX_W), BF16),
        compiler_params=_params(("parallel",)))(h, h, ws, bs_t, ln_g, ln_b)


def _sgu_bwd(h, dcat, ws, ws_t, bs_t, ln_g, ln_b, tm=512):
    t = h.shape[0]

    def body(u_ref, v_ref, do_ref, ws_ref, wst_ref, bs_ref, g_ref, b_ref, dh_ref, dws_ref, dbs_ref, dg_ref, db_ref, dvn_ref):
        @pl.when(pl.program_id(0) == 0)
        def _():
            dws_ref[...] = jnp.zeros_like(dws_ref)
            dbs_ref[...] = jnp.zeros_like(dbs_ref)
            dg_ref[...] = jnp.zeros_like(dg_ref)
            db_ref[...] = jnp.zeros_like(db_ref)

        u = u_ref[...].astype(F32)
        v = v_ref[...].astype(F32)
        do = do_ref[...].astype(F32)
        ug = _gelu(u)
        vhat, rstd = _ln_hat(_gelu(v))
        vn = (vhat * g_ref[...] + b_ref[...]).astype(BF16)
        dmixed_f = do * ug
        dmixed = dmixed_f.astype(BF16)
        low, upp = _tril_mask(True), _tril_mask(False)
        first = lax.broadcasted_iota(jnp.int32, (1, 2 * HEAD_DIM), 1) < HEAD_DIM
        for pr in range(N_HEADS // 2):
            lanes = _lane_half(pr)
            w_ab = [jnp.where(low, ws_ref[2 * pr + ab], 0).astype(BF16) for ab in range(2)]
            wt_ab = [jnp.where(upp, wst_ref[2 * pr + ab], 0).astype(BF16) for ab in range(2)]
            bias = jnp.where(first, bs_ref[:, 2 * pr:2 * pr + 1], bs_ref[:, 2 * pr + 1:2 * pr + 2])
            dws_acc = [None, None]
            dbs_acc = [None, None]
            for c in range(tm // BLK):
                rs = slice(c * BLK, (c + 1) * BLK)
                vn_pair = vn[rs, lanes]
                v_ab = _split_pair(vn_pair)
                mixed = _dot_nn(w_ab[0], v_ab[0]) + _dot_nn(w_ab[1], v_ab[1]) + bias
                dh_ref[rs, lanes] = (do[rs, lanes] * mixed * _gelu_grad(u[rs, lanes])).astype(BF16)
                dm_ab = _split_pair(dmixed[rs, lanes])
                dmf_ab = _split_pair(dmixed_f[rs, lanes])
                for ab in range(2):
                    term = _dot_nt(dm_ab[ab], vn_pair)
                    dws_acc[ab] = term if dws_acc[ab] is None else dws_acc[ab] + term
                    rsum = jnp.sum(dmf_ab[ab], axis=-1, keepdims=True)
                    dbs_acc[ab] = rsum if dbs_acc[ab] is None else dbs_acc[ab] + rsum
                dvn_ref[rs, lanes] = _dot_nn(wt_ab[0], dm_ab[0]) + _dot_nn(wt_ab[1], dm_ab[1])
            for ab in range(2):
                g = 2 * pr + ab
                dws_ref[g] += jnp.where(low, dws_acc[ab], 0.0)
                dbs_ref[:, g:g + 1] += dbs_acc[ab]
        dvn = dvn_ref[...]
        dg_ref[...] += jnp.sum(dvn * vhat, axis=0, keepdims=True)
        db_ref[...] += jnp.sum(dvn, axis=0, keepdims=True)
        dxh = dvn * g_ref[...]
        m1 = jnp.mean(dxh, axis=-1, keepdims=True)
        m2 = jnp.mean(dxh * vhat, axis=-1, keepdims=True)
        dvg = rstd * (dxh - m1 - vhat * m2)
        dh_ref[:, MIX_W:] = (dvg * _gelu_grad(v)).astype(BF16)

    return pl.pallas_call(
        body, name="sgu_bwd", grid=(t // tm,),
        in_specs=[_rows(tm, MIX_W, 0), _rows(tm, MIX_W, 1), _rows(tm, MIX_W, 0), _whole(ws.shape), _whole(ws_t.shape),
                  _whole(bs_t.shape), _whole(ln_g.shape), _whole(ln_b.shape)],
        out_specs=[_rows(tm, 2 * MIX_W), _whole(ws.shape), _whole(bs_t.shape), _whole((1, MIX_W)), _whole((1, MIX_W))],
        out_shape=[_sds((t, 2 * MIX_W), BF16), _sds(ws.shape, F32), _sds(bs_t.shape, F32), _sds((1, MIX_W), F32), _sds((1, MIX_W), F32)],
        scratch_shapes=[pltpu.VMEM((tm, MIX_W), F32)],
        compiler_params=_params(("arbitrary",)))(h, h, dcat, ws, ws_t, bs_t, ln_g, ln_b)


def _row_tile(rows, cols, itemsize=4, limit=2 ** 20):
    best = rows
    for cand in (4096, 2048, 1024, 512, 256, 128, 64, 32, 16):
        if rows % cand == 0 and rows > cand:
            best = cand
            if cand * cols * itemsize <= limit:
                break
    return best


def _adamw(w, m, v, grad=None, parts=None, first_parts=None):
    rows, cols = w.shape
    rows0 = 0 if first_parts is None else first_parts.shape[1]
    tr = _row_tile(rows0 if rows0 else rows, cols)
    n0 = rows0 // tr

    def chip_sum(ref):
        acc = ref[0].astype(F32)
        for k in range(1, 4):
            acc = acc + ref[k].astype(F32)
        return acc

    def body(*refs):
        w_ref, m_ref, v_ref = refs[:3]
        go_ref, d_ref, nm_ref, nv_ref = refs[-4:]
        if parts is None:
            gv = refs[3][...]
        elif first_parts is None:
            gv = chip_sum(refs[3])
        else:
            gv = jnp.where(pl.program_id(0) < n0, chip_sum(refs[3]), chip_sum(refs[4]))
        nm = ADAM_B1 * m_ref[...] + (1.0 - ADAM_B1) * gv
        nv = ADAM_B2 * v_ref[...] + (1.0 - ADAM_B2) * (gv * gv)
        m_hat = nm / (1.0 - ADAM_B1 ** ADAM_STEP)
        v_hat = nv / (1.0 - ADAM_B2 ** ADAM_STEP)
        go_ref[...] = gv
        d_ref[...] = -ADAM_LR * (m_hat / (jnp.sqrt(v_hat) + ADAM_EPS) + ADAM_WD * w_ref[...])
        nm_ref[...] = nm
        nv_ref[...] = nv

    spec = _rows(tr, cols)
    if parts is None:
        g_specs, g_args = [spec], [grad]
    elif first_parts is None:
        g_specs, g_args = [pl.BlockSpec((4, tr, cols), lambda i: (0, i, 0))], [parts]
    else:
        g_specs = [pl.BlockSpec((4, tr, cols), lambda i: (0, jnp.minimum(i, n0 - 1), 0)),
                   pl.BlockSpec((4, tr, cols), lambda i: (0, jnp.maximum(i - n0, 0), 0))]
        g_args = [first_parts, parts]
    return pl.pallas_call(
        body, name="adamw" if parts is None else "adamw_sum_chips", grid=(rows // tr,), in_specs=[spec] * 3 + g_specs,
        out_specs=[spec] * 4, out_shape=[_sds(w.shape, F32)] * 4,
        compiler_params=_params(("parallel",)))(w, m, v, *g_args)


_ANY = pl.BlockSpec(memory_space=pl.ANY)
_MESH = pl.DeviceIdType.MESH


def _all_gather(name, blocks):
    nt = len(blocks)

    def body(*refs):
        start, finish = _gather_protocol(refs[:nt], refs[nt:2 * nt], *refs[2 * nt:])
        start()
        finish()

    return pl.pallas_call(
        body, name=name, out_shape=_gather_shapes(blocks), in_specs=[_ANY] * nt, out_specs=[_ANY] * nt,
        scratch_shapes=_gather_sems(nt))(*blocks)


def _gather_shapes(blocks):
    return [_sds((N_DEV,) + b.shape, b.dtype) for b in blocks]


def _gather_sems(nt):
    return [pltpu.SemaphoreType.DMA((nt, 7)), pltpu.SemaphoreType.DMA((nt, 7)), pltpu.SemaphoreType.DMA((nt,))]


def _gather_protocol(x_refs, out_refs, send_sems, recv_sems, local_sems):
    nt = len(x_refs)
    x, y, c = lax.axis_index("x"), lax.axis_index("y"), lax.axis_index("c")
    me, sibling = (x, y, c), (x, y, 1 - c)
    chips = [(1 - x, y), (x, 1 - y), (1 - x, 1 - y)]

    def slot(t, px, py, pc):
        return out_refs[t].at[4 * px + 2 * py + pc]

    def copy(t, k, blk, to, src=None):
        return pltpu.make_async_remote_copy(
            src_ref=slot(t, *blk) if src is None else src, dst_ref=slot(t, *blk),
            send_sem=send_sems.at[t, k], recv_sem=recv_sems.at[t, k], device_id=to, device_id_type=_MESH)

    def own_copies():
        mine = [pltpu.make_async_copy(x_refs[t], slot(t, *me), local_sems.at[t]) for t in range(nt)]
        first = []
        for t in range(nt):
            first.append(copy(t, 0, me, sibling, src=x_refs[t]))
            first += [copy(t, 1 + j, me, (*chip, c), src=x_refs[t]) for j, chip in enumerate(chips)]
        return mine, first

    def start():
        mine, first = own_copies()
        for cp in mine + first:
            cp.start()

    def finish():
        mine, first = own_copies()
        passed = []
        for j, chip in enumerate(chips):
            for t in range(nt):
                copy(t, 1 + j, (*chip, c), me).wait_recv()
                fwd = copy(t, 4 + j, (*chip, c), sibling)
                fwd.start()
                passed.append(fwd)
        for t in range(nt):
            copy(t, 0, sibling, me).wait_recv()
        for j, chip in enumerate(chips):
            for t in range(nt):
                copy(t, 4 + j, (*chip, 1 - c), me).wait_recv()
        for cp in first + passed:
            cp.wait_send()
        for cp in mine:
            cp.wait()

    return start, finish


def _swap_with_sibling(packed):
    nt = len(packed)

    def body(*refs):
        p_refs, got_refs = refs[:nt], refs[nt:2 * nt]
        send_sems, recv_sems = refs[2 * nt:]
        x, y, c = lax.axis_index("x"), lax.axis_index("y"), lax.axis_index("c")
        copies = [
            pltpu.make_async_remote_copy(
                src_ref=p_refs[t].at[1 - c], dst_ref=got_refs[t], send_sem=send_sems.at[t], recv_sem=recv_sems.at[t],
                device_id=(x, y, 1 - c), device_id_type=_MESH)
            for t in range(nt)]
        for cp in copies:
            cp.start()
        for cp in copies:
            cp.wait_recv()
        for cp in copies:
            cp.wait_send()

    return pl.pallas_call(
        body, name="grad_swap_sibling", out_shape=[_sds(p.shape[1:], p.dtype) for p in packed], in_specs=[_ANY] * nt,
        out_specs=[_ANY] * nt,
        scratch_shapes=[pltpu.SemaphoreType.DMA((nt,)), pltpu.SemaphoreType.DMA((nt,))])(*packed)


def _chip_sum(packed, got):
    _, nchip, rows, cols = packed.shape
    tr = _row_tile(rows, cols, 2)
    core = lax.axis_index("c").astype(jnp.int32).reshape(1)

    def body(c_ref, p_ref, g_ref, o_ref):
        o_ref[...] = (p_ref[...].astype(F32) + g_ref[...].astype(F32)).astype(o_ref.dtype)

    grid_spec = pltpu.PrefetchScalarGridSpec(
        num_scalar_prefetch=1, grid=(nchip, rows // tr),
        in_specs=[pl.BlockSpec((None, None, tr, cols), lambda k, i, c: (c[0], k, i, 0)),
                  pl.BlockSpec((None, tr, cols), lambda k, i, c: (k, i, 0))],
        out_specs=pl.BlockSpec((None, tr, cols), lambda k, i, c: (k, i, 0)))
    return pl.pallas_call(
        body, name="grad_chip_sum", grid_spec=grid_spec, out_shape=_sds(got.shape, got.dtype),
        compiler_params=_params(("parallel", "parallel")))(core, packed, got)


def _exchange_chips(chip_sums):
    nt = len(chip_sums)

    def body(*refs):
        start, finish = _exchange_protocol(refs[:nt], refs[nt:2 * nt], *refs[2 * nt:])
        start()
        finish()

    return pl.pallas_call(
        body, name="grad_exchange_chips", out_shape=[_sds(s.shape, s.dtype) for s in chip_sums], in_specs=[_ANY] * nt,
        out_specs=[_ANY] * nt, scratch_shapes=_exchange_sems(nt))(*chip_sums)


def _exchange_sems(nt):
    return [pltpu.SemaphoreType.DMA((nt, 3)), pltpu.SemaphoreType.DMA((nt, 3)), pltpu.SemaphoreType.DMA((nt,))]


def _exchange_protocol(s_refs, got_refs, send_sems, recv_sems, local_sems):
    nt = len(s_refs)
    x, y, c = lax.axis_index("x"), lax.axis_index("y"), lax.axis_index("c")
    my_chip = 2 * x + y
    chips = [(1 - x, y), (x, 1 - y), (1 - x, 1 - y)]

    def copy(t, j, src_chip, dst_chip):
        px, py = chips[j]
        return pltpu.make_async_remote_copy(
            src_ref=s_refs[t].at[src_chip], dst_ref=got_refs[t].at[dst_chip], send_sem=send_sems.at[t, j],
            recv_sem=recv_sems.at[t, j], device_id=(px, py, c), device_id_type=_MESH)

    def own_copies():
        mine = [pltpu.make_async_copy(s_refs[t].at[my_chip], got_refs[t].at[my_chip], local_sems.at[t]) for t in range(nt)]
        sends = [copy(t, j, 2 * px + py, my_chip) for t in range(nt) for j, (px, py) in enumerate(chips)]
        return mine, sends

    def start():
        mine, sends = own_copies()
        for cp in mine + sends:
            cp.start()

    def finish():
        mine, sends = own_copies()
        for j, (px, py) in enumerate(chips):
            for t in range(nt):
                copy(t, j, my_chip, 2 * px + py).wait_recv()
        for cp in sends:
            cp.wait_send()
        for cp in mine:
            cp.wait()

    return start, finish


def _sum_chips(got):
    _, rows, cols = got.shape
    tr = _row_tile(rows, cols)

    def body(g_ref, o_ref):
        acc = g_ref[0].astype(F32)
        for k in range(1, 4):
            acc = acc + g_ref[k].astype(F32)
        o_ref[...] = acc

    return pl.pallas_call(
        body, name="grad_sum_chips", grid=(rows // tr,), in_specs=[pl.BlockSpec((4, tr, cols), lambda i: (0, i, 0))],
        out_specs=pl.BlockSpec((tr, cols), lambda i: (i, 0)), out_shape=_sds((rows, cols), F32),
        compiler_params=_params(("parallel",)))(got)


_COL_SHARDED = ("a_w_in", "b_w_in", "w_gate", "w_up")
_ROW_SHARDED = ("w_mem_kv", "w_out", "w_down")
_BIG = ("a_w_in", "b_w_in", "w_mem_kv", "w_out", "w_gate", "w_up", "w_down")
_SGU_LN = ("sgu_ln_g", "sgu_ln_b")
_LN4 = ("ln_mix_g", "ln_mix_b", "ln_ffn_g", "ln_ffn_b")
_REPLICATED = ("sgu_w_s", "sgu_b_s") + _LN4


def _unshard(name, gathered):
    if name in _COL_SHARDED or name in _SGU_LN:
        moved = jnp.moveaxis(gathered, 0, -2)
        return moved.reshape(moved.shape[:-2] + (moved.shape[-2] * moved.shape[-1],))
    moved = jnp.moveaxis(gathered, 0, 1)
    return moved.reshape((moved.shape[0], moved.shape[1] * moved.shape[2]) + moved.shape[3:])


def _by_shard(name, full):
    if name in _COL_SHARDED or name in _SGU_LN:
        split = full.reshape(full.shape[:-1] + (N_DEV, full.shape[-1] // N_DEV))
        return jnp.moveaxis(split, -2, 0)
    split = full.reshape((full.shape[0], N_DEV, full.shape[1] // N_DEV) + full.shape[2:])
    return jnp.moveaxis(split, 1, 0)


def _layer_keys(i):
    return [("a_w_in" if i % 2 == 0 else "b_w_in", i // 2)] + [(n, i) for n in ("w_mem_kv", "w_out", "w_gate", "w_up", "w_down")]


_GATHER_FIRST = _layer_keys(0)[:2]
_GATHER_LATER = (_layer_keys(0)[2:] + _layer_keys(1), _layer_keys(2), _layer_keys(3))


def _shard_block(shards, key):
    name, idx = key
    return shards[name][idx:idx + 1].astype(BF16)


def _gather_first(shards):
    blocks = [_shard_block(shards, k) for k in _GATHER_FIRST] + [shards[n] for n in _SGU_LN]
    gathered = _all_gather("first_all_gather", blocks)
    full = {k: _unshard(k[0], g) for k, g in zip(_GATHER_FIRST, gathered)}
    sgu_ln = {n: _unshard(n, g) for n, g in zip(_SGU_LN, gathered[len(_GATHER_FIRST):])}
    return full, sgu_ln


def _two_level(by_dest):
    shp = by_dest.shape[1:]
    split = by_dest.astype(BF16).reshape((4, 2) + shp).swapaxes(0, 1)
    return split.reshape(2, 4, int(np.prod(shp[:-1])), shp[-1])


_EARLY = _BIG + _SGU_LN


def _chip_sums_of_early(grads):
    packed = [_two_level(_by_shard(n, jnp.stack(grads[n][1:] if n == "a_w_in" else grads[n]))) for n in _EARLY]
    ln4 = jnp.stack([jnp.stack(grads[n]) for n in _LN4])
    rep = [jnp.stack(grads["sgu_w_s"]).reshape(N_DEV, -1, BLK), jnp.stack(grads["sgu_b_s"]).reshape(N_DEV, -1, BLK),
           ln4.reshape(N_DEV, -1, D_MODEL)]
    packed += [_two_level(r) for r in rep]
    got = _swap_with_sibling(packed)
    return [_chip_sum(p, g) for p, g in zip(packed, got)]


def _finish_replicated(parts, shapes):
    w_s, b_s, ln_all = _all_gather("replicated_grads_all_gather", [_sum_chips(p) for p in parts])
    ln_all = ln_all.reshape((len(_LN4),) + tuple(shapes[_LN4[0]]))
    rep_grads = {"sgu_w_s": w_s.reshape(shapes["sgu_w_s"]), "sgu_b_s": b_s.reshape(shapes["sgu_b_s"])}
    rep_grads.update({n: ln_all[i] for i, n in enumerate(_LN4)})
    return rep_grads


def _reduce_last(grad_a_first):
    packed = [_two_level(_by_shard("a_w_in", grad_a_first))]
    got = _swap_with_sibling(packed)
    return _exchange_chips([_chip_sum(packed[0], got[0])])[0]


def _as_2d(a):
    if a.ndim == 1:
        return a.reshape(1, -1)
    return a.reshape(-1, a.shape[-1])


def kernel(x, mem, a_w_in, b_w_in, sgu_ln_g, sgu_ln_b, sgu_w_s, sgu_b_s, w_mem_kv, w_out, ln_mix_g, ln_mix_b, w_gate, w_up, w_down, ln_ffn_g, ln_ffn_b, loss_target, m_a_w_in, m_b_w_in, m_sgu_ln_g, m_sgu_ln_b, m_sgu_w_s, m_sgu_b_s, m_w_mem_kv, m_w_out, m_ln_mix_g, m_ln_mix_b, m_w_gate, m_w_up, m_w_down, m_ln_ffn_g, m_ln_ffn_b, v_a_w_in, v_b_w_in, v_sgu_ln_g, v_sgu_ln_b, v_sgu_w_s, v_sgu_b_s, v_w_mem_kv, v_w_out, v_ln_mix_g, v_ln_mix_b, v_w_gate, v_w_up, v_w_down, v_ln_ffn_g, v_ln_ffn_b):
    names = ("a_w_in", "b_w_in", "sgu_ln_g", "sgu_ln_b", "sgu_w_s", "sgu_b_s", "w_mem_kv", "w_out", "ln_mix_g", "ln_mix_b",
             "w_gate", "w_up", "w_down", "ln_ffn_g", "ln_ffn_b")
    weights = dict(zip(names, (a_w_in, b_w_in, sgu_ln_g, sgu_ln_b, sgu_w_s, sgu_b_s, w_mem_kv, w_out, ln_mix_g, ln_mix_b,
                               w_gate, w_up, w_down, ln_ffn_g, ln_ffn_b)))
    mom_m = dict(zip(names, (m_a_w_in, m_b_w_in, m_sgu_ln_g, m_sgu_ln_b, m_sgu_w_s, m_sgu_b_s, m_w_mem_kv, m_w_out, m_ln_mix_g,
                             m_ln_mix_b, m_w_gate, m_w_up, m_w_down, m_ln_ffn_g, m_ln_ffn_b)))
    mom_v = dict(zip(names, (v_a_w_in, v_b_w_in, v_sgu_ln_g, v_sgu_ln_b, v_sgu_w_s, v_sgu_b_s, v_w_mem_kv, v_w_out, v_ln_mix_g,
                             v_ln_mix_b, v_w_gate, v_w_up, v_w_down, v_ln_ffn_g, v_ln_ffn_b)))
    full, sgu_ln = _gather_first(weights)
    pending = [(keys, [_shard_block(weights, k) for k in keys]) for keys in _GATHER_LATER]
    loss_part, grad_x, local, early = _local_step(
        x, mem, loss_target, full, sgu_ln, {n: weights[n] for n in _REPLICATED}, pending)
    loss = lax.psum(loss_part[0, 0], ("x", "y", "c"))
    early_parts = dict(zip(_EARLY, early))
    rep_grads = _finish_replicated(early[len(_EARLY):], {n: weights[n].shape for n in _REPLICATED})
    a_first_parts = _reduce_last(local["a_w_in"][:1])

    reduced, deltas, new_m, new_v = {}, {}, {}, {}
    for n in names:
        w2, m2, v2 = _as_2d(weights[n]), _as_2d(mom_m[n]), _as_2d(mom_v[n])
        if n in early_parts:
            outs = _adamw(w2, m2, v2, parts=early_parts[n], first_parts=a_first_parts if n == "a_w_in" else None)
        else:
            outs = _adamw(w2, m2, v2, grad=_as_2d(rep_grads[n]))
        reduced[n], deltas[n], new_m[n], new_v[n] = (a.reshape(weights[n].shape) for a in outs)

    return (loss, grad_x, *[reduced[n] for n in names], *[deltas[n] for n in names],
            *[new_m[n] for n in names], *[new_v[n] for n in names])


def _local_step(x, mem, loss_target, full, sgu_ln, small, pending=None):
    sgu_w_s, sgu_b_s = small["sgu_w_s"], small["sgu_b_s"]
    ln_mix_g, ln_mix_b, ln_ffn_g, ln_ffn_b = (small[n] for n in ("ln_mix_g", "ln_mix_b", "ln_ffn_g", "ln_ffn_b"))
    bsz, seq, _ = x.shape
    tokens = bsz * seq
    slopes = _alibi_table()
    full = dict(full)
    exchanging = pending is not None
    pending = list(pending or [])

    def weight(name, idx):
        return full[(name, idx)][0]

    def next_group():
        return pending[0][1] if pending else None

    def landed(gathered):
        if gathered:
            keys, _ = pending.pop(0)
            full.update({k: _unshard(k[0], g) for k, g in zip(keys, gathered)})

    res = (x.reshape(tokens, D_MODEL),)
    xb = res[0].astype(BF16)
    memb = mem.reshape(bsz * N_MEM, D_MODEL).astype(BF16)
    tgt = loss_target.reshape(tokens, D_MODEL)

    saved = []
    for i in range(DEPTH):
        j = i // 2
        dil_layer = i % 2 == 0
        w_in = weight("a_w_in" if dil_layer else "b_w_in", j)
        mkv = _linear_nn("mem_kv", memb, weight("w_mem_kv", i))
        h = _linear_nn("in_proj_a" if dil_layer else "in_proj_b", xb, w_in)
        st = dict(xb=xb, h=h, mkv=mkv, w_in=w_in)
        if dil_layer:
            mix, st["lse"], gathered = _band_attn_fwd_fused(h, slopes, bsz, seq, gather=next_group() if i == 0 else None)
            landed(gathered)
            q_col = 3 * MIX_W // MEM_W
        else:
            st["ws"] = sgu_w_s[j]
            st["bs_t"] = sgu_b_s[j].T
            st["ln_g"] = sgu_ln["sgu_ln_g"][j].reshape(1, MIX_W)
            st["ln_b"] = sgu_ln["sgu_ln_b"][j].reshape(1, MIX_W)
            mix = _sgu_fwd(h, st["ws"], st["bs_t"], st["ln_g"], st["ln_b"])
            q_col = 2 * MIX_W // MEM_W
        mo = _mem_attn_fwd(h, mkv, bsz, seq, q_col)
        w_out, w_down = weight("w_out", i), weight("w_down", i)
        w_gu = jnp.concatenate([weight("w_gate", i), weight("w_up", i)], axis=-1)
        mix_ln = (ln_mix_g[i].reshape(1, D_MODEL), ln_mix_b[i].reshape(1, D_MODEL))
        ffn_ln = (ln_ffn_g[i].reshape(1, D_MODEL), ln_ffn_b[i].reshape(1, D_MODEL))
        r1, x1b = _proj_ln_fwd("out_proj_ln", [mix, mo], w_out, res, *mix_ln)
        gt, up, act, gathered = _ffn_up_fwd(x1b, w_gu, gather=next_group() if i < 2 else None)
        landed(gathered)
        r2, xb = _proj_ln_fwd("ffn_down_ln", [act], w_down, (r1, *mix_ln), *ffn_ln)
        res = (r2, *ffn_ln)
        st.update(mix=mix, mo=mo, q_col=q_col, r1=r1, x1b=x1b, gt=gt, up=up, act=act, r2=r2,
                  w_out=w_out, w_down=w_down, w_gu=w_gu)
        saved.append(st)

    dr2, dr2b, dg, db, loss_part = _loss_ln_bwd(*res, tgt)

    early_parts = None
    per_pair = ("a_w_in", "b_w_in", "sgu_ln_g", "sgu_ln_b", "sgu_w_s", "sgu_b_s")
    grads = {n: [None] * (DEPTH // 2 if n in per_pair else DEPTH) for n in _BIG + _SGU_LN + _REPLICATED}
    for i in reversed(range(DEPTH)):
        j = i // 2
        st = saved[i]
        dil_layer = i % 2 == 0
        w_in = st["w_in"]
        grads["ln_ffn_g"][i], grads["ln_ffn_b"][i] = dg[0], db[0]
        dgu = _ffn_down_bwd(dr2b, st["w_down"], st["gt"], st["up"])
        grads["w_down"][i] = _mm_tn("grad_w_down", st["act"], dr2b)
        dr1, dr1b, dg, db = _linear_nt("ffn_up_bwd", [dgu], st["w_gu"], dr2, F32,
                                       ln=(st["r1"], ln_mix_g[i].reshape(1, D_MODEL)))
        grads["ln_mix_g"][i], grads["ln_mix_b"][i] = dg[0], db[0]
        dw_gu = _mm_tn("grad_w_gate_up", st["x1b"], dgu)
        grads["w_gate"][i], grads["w_up"][i] = dw_gu[:, :D_FF], dw_gu[:, D_FF:]
        dcat = _linear_nt("out_proj_bwd", [dr1b], st["w_out"], None, BF16)
        grads["w_out"][i] = jnp.concatenate(
            [_mm_tn("grad_w_out_mix", st["mix"], dr1b), _mm_tn("grad_w_out_mem", st["mo"], dr1b)], axis=0)
        dqm, dmkv = _mem_attn_bwd(st["h"], st["mkv"], dcat, bsz, seq, st["q_col"])
        grads["w_mem_kv"][i] = _mm_tn("grad_w_mem_kv", memb, dmkv.astype(BF16))
        if dil_layer:
            early_sums = _chip_sums_of_early(grads) if (i == 0 and exchanging) else None
            dh_parts, exchanged = _band_attn_bwd_fused(st["h"], dcat, st["mix"], st["lse"], slopes, bsz, seq,
                                                       exchange=early_sums)
            if early_sums is not None:
                early_parts = exchanged
        else:
            ws_t = jnp.swapaxes(st["ws"], -1, -2)
            dh_main, dws, dbs_t, dlg, dlb = _sgu_bwd(st["h"], dcat, st["ws"], ws_t, st["bs_t"], st["ln_g"], st["ln_b"])
            grads["sgu_w_s"][j], grads["sgu_b_s"][j] = dws, dbs_t.T
            grads["sgu_ln_g"][j], grads["sgu_ln_b"][j] = dlg[0], dlb[0]
            dh_parts = [dh_main]
        name = "in_proj_bwd_a" if dil_layer else "in_proj_bwd_b"
        if i > 0:
            dr2, dr2b, dg, db = _linear_nt(name, [*dh_parts, dqm], w_in, dr1, F32,
                                           ln=(saved[i - 1]["r2"], ln_ffn_g[i - 1].reshape(1, D_MODEL)))
        else:
            grad_x = _linear_nt(name + "_x", [*dh_parts, dqm], w_in, dr1, F32).reshape(x.shape)
        grads["a_w_in" if dil_layer else "b_w_in"][j] = jnp.concatenate(
            [_mm_tn("grad_w_in_part", st["xb"], part) for part in dh_parts] + [_mm_tn("grad_w_in_qm", st["xb"], dqm)], axis=1)
    return loss_part, grad_x, {n: jnp.stack(g) for n, g in grads.items()}, early_parts
```

```python
import functools
import math

import numpy as np
import jax
import jax.numpy as jnp
from jax import lax
from jax.experimental import pallas as pl
from jax.experimental.pallas import tpu as pltpu

F32 = jnp.float32
BF16 = jnp.bfloat16

D_MODEL = 1024
DEPTH = 4
N_MEM = 256
HEAD_DIM = 64
N_HEADS = 12
MIX_W = N_HEADS * HEAD_DIM
MEM_W = 4 * HEAD_DIM
DIL_PATTERNS = ((128, 1), (512, 4), (2048, 16))
BLK = 128
HEAD_GROUP = 4
N_GROUPS = N_HEADS // HEAD_GROUP
D_FF = 2816
FF_CHUNKS = 2
ALPHA = (2 * DEPTH) ** 0.25
LN_EPS = 1e-5
SCALE = HEAD_DIM ** -0.5
NEG = -1e30
N_DEV = 8

ADAM_LR, ADAM_B1, ADAM_B2, ADAM_EPS, ADAM_WD, ADAM_STEP = 0.001, 0.9, 0.999, 1e-08, 0.01, 10

VMEM_LIMIT = 56 * 2 ** 20
STAT_LANES = 32
STAT_W = N_HEADS * STAT_LANES


def _dot_nn(a, b):
    return lax.dot_general(a, b, (((1,), (0,)), ((), ())), preferred_element_type=F32)


def _dot_nt(a, b):
    return lax.dot_general(a, b, (((1,), (1,)), ((), ())), preferred_element_type=F32)


def _dot_tn(a, b):
    return lax.dot_general(a, b, (((0,), (0,)), ((), ())), preferred_element_type=F32)


def _ln_hat(r):
    mu = jnp.mean(r, axis=-1, keepdims=True)
    xc = r - mu
    var = jnp.mean(xc * xc, axis=-1, keepdims=True)
    rstd = lax.rsqrt(var + LN_EPS)
    return xc * rstd, rstd


def _params(sem):
    return pltpu.CompilerParams(dimension_semantics=sem, vmem_limit_bytes=VMEM_LIMIT)


def _rows(tm, c, col=0):
    return pl.BlockSpec((tm, c), lambda i: (i, col))


def _whole(shape):
    nd = len(shape)
    return pl.BlockSpec(tuple(shape), lambda *_: (0,) * nd)


def _resident(shape):
    nd = len(shape)
    return pl.BlockSpec(tuple(shape), lambda *_: (0,) * nd, pipeline_mode=pl.Buffered(1))


def _sds(shape, dtype):
    return jax.ShapeDtypeStruct(tuple(shape), dtype)


def _linear_nn(name, a, w, tm=512):
    t, k = a.shape
    n = w.shape[1]
    tm = min(tm, t)

    def body(a_ref, w_ref, o_ref):
        o_ref[...] = _dot_nn(a_ref[...], w_ref[...]).astype(BF16)

    return pl.pallas_call(
        body, name=name, grid=(t // tm,), in_specs=[_rows(tm, k), _resident(w.shape)], out_specs=_rows(tm, n),
        out_shape=_sds((t, n), BF16), compiler_params=_params(("parallel",)))(a, w)


def _linear_nn_gathered(name, a, shards, tm=512):
    t, k = a.shape
    n8 = shards.shape[2]
    n = N_DEV * n8

    def body(a_ref, s_ref, o_ref, w_ref):
        @pl.when(pl.program_id(0) == 0)
        def _():
            for s in range(N_DEV):
                w_ref[:, s * n8:(s + 1) * n8] = s_ref[s]

        o_ref[...] = _dot_nn(a_ref[...], w_ref[...]).astype(BF16)

    return pl.pallas_call(
        body, name=name, grid=(t // tm,), in_specs=[_rows(tm, k), _resident(shards.shape)],
        out_specs=[_rows(tm, n), _whole((k, n))], out_shape=[_sds((t, n), BF16), _sds((k, n), BF16)],
        compiler_params=_params(("arbitrary",)))(a, shards)


def _proj_ln_fwd(name, lhs, w, res, g, b, tm=512):
    t = res[0].shape[0]
    n_lhs = len(lhs)
    n_res = len(res)

    def body(*refs):
        lhs_refs = refs[:n_lhs]
        w_ref = refs[n_lhs]
        res_refs = refs[n_lhs + 1:n_lhs + 1 + n_res]
        g_ref, b_ref, r_ref, xnb_ref = refs[n_lhs + 1 + n_res:]
        y, off = None, 0
        for lr in lhs_refs:
            k = lr.shape[1]
            term = _dot_nn(lr[...], w_ref[off:off + k, :])
            y = term if y is None else y + term
            off += k
        x_res = res_refs[0][...]
        if n_res == 3:
            x_res = _ln_hat(x_res)[0] * res_refs[1][...] + res_refs[2][...]
        r = ALPHA * x_res + y
        r_ref[...] = r
        xnb_ref[...] = (_ln_hat(r)[0] * g_ref[...] + b_ref[...]).astype(BF16)

    vec = _whole((1, D_MODEL))
    in_specs = ([_rows(tm, a.shape[1]) for a in lhs] + [_resident(w.shape), _rows(tm, D_MODEL)] + [vec] * (n_res - 1) + [vec, vec])
    return pl.pallas_call(
        body, name=name, grid=(t // tm,), in_specs=in_specs, out_specs=[_rows(tm, D_MODEL)] * 2,
        out_shape=[_sds((t, D_MODEL), F32), _sds((t, D_MODEL), BF16)],
        compiler_params=_params(("parallel",)))(*lhs, w, *res, g, b)


def _ffn_up_fwd(xb, gate_shards, up_shards, gather=None, tm=256):
    t = xb.shape[0]
    ng = 0 if gather is None else len(gather)
    steps = t // tm
    n8 = gate_shards.shape[2]

    def body(*refs):
        x_ref, gs_ref, us_ref = refs[:3]
        g_ref, u_ref, a_ref, wgu_ref = refs[3 + ng:7 + ng]
        w_ref, w_sem = refs[7 + 2 * ng:9 + 2 * ng]
        keep = pltpu.make_async_copy(w_ref, wgu_ref, w_sem)

        @pl.when(pl.program_id(0) == 0)
        def _():
            for s in range(N_DEV):
                w_ref[:, s * n8:(s + 1) * n8] = gs_ref[s]
                w_ref[:, D_FF + s * n8:D_FF + (s + 1) * n8] = us_ref[s]
            keep.start()

        if ng:
            start, finish = _gather_protocol(refs[3:3 + ng], refs[7 + ng:7 + 2 * ng], *refs[9 + 2 * ng:])
            pl.when(pl.program_id(0) == 0)(start)
        xv = x_ref[...]
        for c in range(FF_CHUNKS):
            cols = slice(c * D_FF // FF_CHUNKS, (c + 1) * D_FF // FF_CHUNKS)
            gt = _dot_nn(xv, w_ref[:, cols])
            up = _dot_nn(xv, w_ref[:, D_FF + cols.start:D_FF + cols.stop])
            g_ref[:, cols] = gt.astype(BF16)
            u_ref[:, cols] = up.astype(BF16)
            a_ref[:, cols] = (gt * jax.nn.sigmoid(gt) * up).astype(BF16)
        if ng:
            pl.when(pl.program_id(0) == steps - 1)(finish)
        pl.when(pl.program_id(0) == steps - 1)(keep.wait)

    k = gate_shards.shape[1]
    outs = pl.pallas_call(
        body, name="ffn_up_fwd_gather" if ng else "ffn_up_fwd", grid=(steps,),
        in_specs=[_rows(tm, D_MODEL), _resident(gate_shards.shape), _resident(up_shards.shape)] + [_ANY] * ng,
        out_specs=[_rows(tm, D_FF)] * 3 + [_ANY] + [_ANY] * ng,
        out_shape=[_sds((t, D_FF), BF16)] * 3 + [_sds((k, 2 * D_FF), BF16)] + (_gather_shapes(gather) if ng else []),
        scratch_shapes=[pltpu.VMEM((k, 2 * D_FF), BF16), pltpu.SemaphoreType.DMA(())] + (_gather_sems(ng) if ng else []),
        compiler_params=_params(("arbitrary",)))(xb, gate_shards, up_shards, *(gather or []))
    return outs[0], outs[1], outs[2], outs[3], list(outs[4:])


def _ln_bwd_rows(dxn, xhat, rstd, g_ref, dr_ref, drb_ref, dg_ref, db_ref):
    @pl.when(pl.program_id(0) == 0)
    def _():
        dg_ref[...] = jnp.zeros_like(dg_ref)
        db_ref[...] = jnp.zeros_like(db_ref)

    dxh = dxn * g_ref[...]
    m1 = jnp.mean(dxh, axis=-1, keepdims=True)
    m2 = jnp.mean(dxh * xhat, axis=-1, keepdims=True)
    dr = rstd * (dxh - m1 - xhat * m2)
    dr_ref[...] = dr
    drb_ref[...] = dr.astype(BF16)
    dg_ref[...] += jnp.sum(dxn * xhat, axis=0, keepdims=True)
    db_ref[...] += jnp.sum(dxn, axis=0, keepdims=True)


def _ln_bwd_outs(t, tm):
    vec = _whole((1, D_MODEL))
    specs = [_rows(tm, D_MODEL), _rows(tm, D_MODEL), vec, vec]
    shapes = [_sds((t, D_MODEL), F32), _sds((t, D_MODEL), BF16), _sds((1, D_MODEL), F32), _sds((1, D_MODEL), F32)]
    return specs, shapes


def _loss_ln_bwd(r, g, b, tgt, tm=512):
    t = r.shape[0]

    def body(r_ref, g_ref, b_ref, t_ref, dr_ref, drb_ref, dg_ref, db_ref, l_ref):
        @pl.when(pl.program_id(0) == 0)
        def _():
            l_ref[...] = jnp.zeros_like(l_ref)

        xhat, rstd = _ln_hat(r_ref[...])
        e = xhat * g_ref[...] + b_ref[...] - t_ref[...]
        l_ref[...] += jnp.sum(e * e) * (0.5 / D_MODEL)
        _ln_bwd_rows(e * (1.0 / D_MODEL), xhat, rstd, g_ref, dr_ref, drb_ref, dg_ref, db_ref)

    vec = _whole((1, D_MODEL))
    specs, shapes = _ln_bwd_outs(t, tm)
    return pl.pallas_call(
        body, name="loss_ln_bwd", grid=(t // tm,), in_specs=[_rows(tm, D_MODEL), vec, vec, _rows(tm, D_MODEL)],
        out_specs=specs + [_whole((1, 128))], out_shape=shapes + [_sds((1, 128), F32)],
        compiler_params=_params(("arbitrary",)))(r, g, b, tgt)


def _ffn_down_bwd(drb, wd, gt, up, tm=512):
    t = drb.shape[0]

    def body(d_ref, w_ref, g_ref, u_ref, o_ref):
        dv = d_ref[...]
        for c in range(FF_CHUNKS):
            cols = slice(c * D_FF // FF_CHUNKS, (c + 1) * D_FF // FF_CHUNKS)
            da = _dot_nt(dv, w_ref[cols, :])
            g = g_ref[:, cols].astype(F32)
            u = u_ref[:, cols].astype(F32)
            sg = jax.nn.sigmoid(g)
            o_ref[:, cols] = (da * u * (sg * (1.0 + g * (1.0 - sg)))).astype(BF16)
            o_ref[:, D_FF + cols.start:D_FF + cols.stop] = (da * (g * sg)).astype(BF16)

    return pl.pallas_call(
        body, name="ffn_down_bwd", grid=(t // tm,),
        in_specs=[_rows(tm, D_MODEL), _resident(wd.shape), _rows(tm, D_FF), _rows(tm, D_FF)],
        out_specs=_rows(tm, 2 * D_FF), out_shape=_sds((t, 2 * D_FF), BF16),
        compiler_params=_params(("parallel",)))(drb, wd, gt, up)


def _linear_nt(name, lhs, w, res, out_dtype, ln=None, tm=512):
    t = lhs[0].shape[0]
    n_lhs = len(lhs)
    n_out = w.shape[0]
    n_in = n_lhs + 1 + (res is not None) + (2 if ln else 0)

    def body(*refs):
        lhs_refs = refs[:n_lhs]
        w_ref = refs[n_lhs]
        y, off = None, 0
        for lr in lhs_refs:
            k = lr.shape[1]
            term = _dot_nt(lr[...], w_ref[:, off:off + k])
            y = term if y is None else y + term
            off += k
        if res is not None:
            y = ALPHA * refs[n_lhs + 1][...] + y
        if ln is None:
            refs[-1][...] = y.astype(out_dtype)
        else:
            r_ref, g_ref = refs[n_in - 2:n_in]
            xhat, rstd = _ln_hat(r_ref[...])
            _ln_bwd_rows(y, xhat, rstd, g_ref, *refs[n_in:])

    in_specs = [_rows(tm, a.shape[1]) for a in lhs] + [_resident(w.shape)]
    args = list(lhs) + [w]
    if res is not None:
        in_specs.append(_rows(tm, n_out))
        args.append(res)
    if ln is None:
        out_specs, out_shape, sem = _rows(tm, n_out), _sds((t, n_out), out_dtype), "parallel"
    else:
        in_specs += [_rows(tm, D_MODEL), _whole((1, D_MODEL))]
        args += list(ln)
        (out_specs, out_shape), sem = _ln_bwd_outs(t, tm), "arbitrary"
    return pl.pallas_call(
        body, name=name, grid=(t // tm,), in_specs=in_specs, out_specs=out_specs, out_shape=out_shape,
        compiler_params=_params((sem,)))(*args)


def _pick_tile(n, limit):
    if n <= limit:
        return n
    best = 128
    for cand in range(128, limit + 1, 128):
        if n % cand == 0:
            best = cand
    return best


def _mm_tn(name, a, b, tt=1024):
    t, k = a.shape
    n = b.shape[1]
    tt = min(tt, t)
    tk = _pick_tile(k, 1408)
    tn = _pick_tile(n, (6 * 2 ** 20) // (4 * tk) // 128 * 128)
    steps = t // tt

    def body(a_ref, b_ref, o_ref, acc_ref):
        @pl.when(pl.program_id(2) == 0)
        def _():
            acc_ref[...] = jnp.zeros_like(acc_ref)

        acc_ref[...] += _dot_tn(a_ref[...], b_ref[...])

        @pl.when(pl.program_id(2) == steps - 1)
        def _():
            o_ref[...] = acc_ref[...].astype(BF16)

    return pl.pallas_call(
        body, name=name, grid=(k // tk, n // tn, steps),
        in_specs=[pl.BlockSpec((tt, tk), lambda i, j, s: (s, i)), pl.BlockSpec((tt, tn), lambda i, j, s: (s, j))],
        out_specs=pl.BlockSpec((tk, tn), lambda i, j, s: (i, j)), out_shape=_sds((k, n), BF16),
        scratch_shapes=[pltpu.VMEM((tk, tn), F32)],
        compiler_params=_params(("parallel", "parallel", "arbitrary")))(a, b)


def _alibi_table():
    arr = np.zeros((N_GROUPS, 8, 128), np.float32)
    for g in range(N_GROUPS):
        for hh in range(HEAD_GROUP):
            arr[g, hh, :] = 2.0 ** (-8.0 * (g * HEAD_GROUP + hh + 1) / N_HEADS)
    return jnp.asarray(arr)


def _spread_stats(cols, per_head=STAT_LANES):
    lane = lax.broadcasted_iota(jnp.int32, (BLK, HEAD_GROUP * per_head), 1)
    tile = cols[HEAD_GROUP - 1]
    for hh in range(HEAD_GROUP - 2, -1, -1):
        tile = jnp.where(lane < (hh + 1) * per_head, cols[hh], tile)
    return tile


def _block_mask(has_prev, dil):
    if has_prev is None:
        steps = lax.broadcasted_iota(jnp.int32, (BLK, BLK), 0) - lax.broadcasted_iota(jnp.int32, (BLK, BLK), 1)
        return steps >= 0, (steps * dil).astype(F32)
    qi = lax.broadcasted_iota(jnp.int32, (BLK, 2 * BLK), 0)
    ki = lax.broadcasted_iota(jnp.int32, (BLK, 2 * BLK), 1)
    steps = qi + BLK - ki
    valid = (steps >= 0) & (steps <= BLK) & ((ki >= BLK) | has_prev)
    return valid, (steps * dil).astype(F32)


def _bias_scratch():
    return pltpu.VMEM((2, HEAD_GROUP, BLK, 2 * BLK), F32)


def _fill_bias(bias, sl_ref, dil):
    for p in range(2):
        valid, dist = _block_mask(p == 1, dil)
        for hh in range(HEAD_GROUP):
            bias[p, hh] = jnp.where(valid, -sl_ref[hh:hh + 1, 0:1] * dist, NEG)


def _rows_of(j):
    return pl.ds(pl.multiple_of(j * BLK, BLK), BLK)


def _lane_half(hf):
    return slice(hf * 128, (hf + 1) * 128)


def _split_pair(x):
    first = lax.broadcasted_iota(jnp.int32, (1, 2 * HEAD_DIM), 1) < HEAD_DIM
    zero = jnp.zeros_like(x)
    return jnp.where(first, x, zero), jnp.where(first, zero, x)


def _deinterleave(src, dst, seq, dil, dtype):
    length = seq // dil
    for r in range(dil):
        for c in range(length // BLK):
            rows = pl.ds(r + c * BLK * dil, BLK, stride=dil)
            out = slice(r * length + c * BLK, r * length + (c + 1) * BLK)
            if len(src.shape) == 2:
                dst[out, :] = src[rows, :].astype(dtype)
            else:
                for hf in range(2):
                    dst[out, _lane_half(hf)] = src.at[hf][rows, :].astype(dtype)


def _interleave(src, dst, seq, dil, accumulate):
    length = seq // dil
    for r in range(dil):
        for c in range(length // BLK):
            rows = pl.ds(r + c * BLK * dil, BLK, stride=dil)
            inp = slice(r * length + c * BLK, r * length + (c + 1) * BLK)
            if len(dst.shape) == 2:
                dst[rows, :] = dst[rows, :] + src[inp, :] if accumulate else src[inp, :]
            else:
                for hf in range(2):
                    val = src[inp, _lane_half(hf)]
                    half = dst.at[hf]
                    half[rows, :] = half[rows, :] + val if accumulate else val


def _split_halves(src, dst, seq):
    def step(i, carry):
        for hf in range(2):
            dst[hf, _rows_of(i), :] = src[_rows_of(i), _lane_half(hf)].astype(F32)
        return carry

    lax.fori_loop(0, seq // BLK, step, 0)


def _band_attn_fwd_fused(h, slopes, bsz, seq, gather=None):
    width = h.shape[1]
    cb = width // 256
    k_off, v_off = MIX_W // 256, 2 * MIX_W // 256
    nb = seq // BLK

    ng = 0 if gather is None else len(gather)

    def body(*refs):
        sl_ref, q_ref, k_ref, v_ref = refs[:4]
        mix_ref, lse_ref = refs[4 + ng:6 + ng]
        qf, kf, vf, qd, kd, vd, od, ld, o1, o2, o3, l1, l2, l3, bias = refs[6 + 2 * ng:21 + 2 * ng]
        if ng:
            start, finish = _gather_protocol(refs[4:4 + ng], refs[6 + ng:6 + 2 * ng], *refs[21 + 2 * ng:])
            pl.when((pl.program_id(0) == 0) & (pl.program_id(1) == 0))(start)

        def run(dil, qs, ks, vs, o_dst, l_dst):
            nblk = seq // dil // BLK
            _fill_bias(bias, sl_ref, dil)

            def block(j, carry):
                rows, prows = _rows_of(j), _rows_of(jnp.maximum(j - 1, 0))
                has_prev = ((j % nblk) != 0).astype(jnp.int32)

                def keys(ref, lanes):
                    return jnp.concatenate([ref[prows, lanes], ref[rows, lanes]], axis=0)

                lses = []
                for pr in range(HEAD_GROUP // 2):
                    lanes = _lane_half(pr)
                    q_ab = _split_pair(qs[rows, lanes] * SCALE)
                    k2 = keys(ks, lanes)
                    v_ab = _split_pair(keys(vs, lanes))
                    out = None
                    for ab in range(2):
                        hh = 2 * pr + ab
                        s = _dot_nt(q_ab[ab], k2) + bias[has_prev, hh]
                        m = jnp.max(s, axis=-1, keepdims=True)
                        p = jnp.exp(s - m)
                        l = jnp.sum(p, axis=-1, keepdims=True)
                        term = _dot_nn(p.astype(BF16), v_ab[ab]) / l
                        out = term if out is None else out + term
                        lses.append(m + jnp.log(l))
                    o_dst[rows, lanes] = out
                l_dst[rows, :] = _spread_stats(lses, HEAD_DIM)
                return carry

            lax.fori_loop(0, nb, block, 0, unroll=8)

        run(1, q_ref, k_ref, v_ref, o1, l1)
        _split_halves(q_ref, qf, seq)
        _split_halves(k_ref, kf, seq)
        _split_halves(v_ref, vf, seq)
        for dil, o_tok, l_tok in ((4, o2, l2), (16, o3, l3)):
            _deinterleave(qf, qd, seq, dil, BF16)
            _deinterleave(kf, kd, seq, dil, BF16)
            _deinterleave(vf, vd, seq, dil, BF16)
            run(dil, qd, kd, vd, od, ld)
            _interleave(od, o_tok, seq, dil, False)
            _interleave(ld, l_tok, seq, dil, False)

        def merge(i, carry):
            rows = _rows_of(i)

            def both(ref):
                return jnp.concatenate([ref[0, rows, :], ref[1, rows, :]], axis=1)

            ls = [l1[rows, :], both(l2), both(l3)]
            m = jnp.maximum(jnp.maximum(ls[0], ls[1]), ls[2])
            tot = m + jnp.log(jnp.exp(ls[0] - m) + jnp.exp(ls[1] - m) + jnp.exp(ls[2] - m))
            ws = [jnp.exp(x - tot) for x in ls]
            mix_ref[rows, :] = (ws[0] * o1[rows, :] + ws[1] * both(o2) + ws[2] * both(o3)).astype(BF16)
            lse_ref[rows, :] = _spread_stats([tot[:, hh * HEAD_DIM:hh * HEAD_DIM + 1] for hh in range(HEAD_GROUP)])
            return carry

        lax.fori_loop(0, nb, merge, 0)
        if ng:
            pl.when((pl.program_id(0) == bsz - 1) & (pl.program_id(1) == N_GROUPS - 1))(finish)

    def hspec(off):
        return pl.BlockSpec((seq, 256), lambda b, g: (b, off + g))

    big = lambda dt: pltpu.VMEM((seq, 256), dt)
    halves = lambda: pltpu.VMEM((2, seq, 128), F32)
    stat = lambda: pltpu.VMEM((seq, 128), F32)
    outs = pl.pallas_call(
        body, name="band_attn_fwd_gather" if ng else "band_attn_fwd", grid=(bsz, N_GROUPS),
        in_specs=[pl.BlockSpec((None, 8, 128), lambda b, g: (g, 0, 0)), hspec(0), hspec(k_off), hspec(v_off)] + [_ANY] * ng,
        out_specs=[pl.BlockSpec((seq, 256), lambda b, g: (b, g)), pl.BlockSpec((seq, 128), lambda b, g: (b, g))] + [_ANY] * ng,
        out_shape=[_sds((bsz * seq, MIX_W), BF16), _sds((bsz * seq, STAT_W), F32)] + (_gather_shapes(gather) if ng else []),
        scratch_shapes=[halves(), halves(), halves(), big(BF16), big(BF16), big(BF16), big(F32), big(F32),
                        big(F32), halves(), halves(), big(F32), halves(), halves(), _bias_scratch()]
        + (_gather_sems(ng) if ng else []),
        compiler_params=_params(("arbitrary", "arbitrary")))(slopes, h, h, h, *(gather or []))
    return outs[0], outs[1], list(outs[2:])


def _band_attn_bwd_fused(h, dcat, mix, lse, slopes, bsz, seq, exchange=None):
    width = h.shape[1]
    k_off, v_off = MIX_W // 256, 2 * MIX_W // 256
    nb = seq // BLK

    ne = 0 if exchange is None else len(exchange)

    def body(*refs):
        sl_ref, q_ref, k_ref, v_ref, do_ref, o_ref, lse_ref = refs[:7]
        dq_ref, dk_ref, dv_ref = refs[7 + ne:10 + ne]
        qf, kf, vf, dof, ddt, qd, kd, vd, dod, lsd, ddd, gq, gk, gv, aq, ak, av, bias = refs[10 + 2 * ne:28 + 2 * ne]
        if ne:
            start, finish = _exchange_protocol(refs[7:7 + ne], refs[10 + ne:10 + 2 * ne], *refs[28 + 2 * ne:])
            pl.when((pl.program_id(0) == 0) & (pl.program_id(1) == 0))(start)

        same_head = (lax.broadcasted_iota(jnp.int32, (HEAD_GROUP * HEAD_DIM, HEAD_GROUP * STAT_LANES), 0) // HEAD_DIM
                     == lax.broadcasted_iota(jnp.int32, (HEAD_GROUP * HEAD_DIM, HEAD_GROUP * STAT_LANES), 1) // STAT_LANES)
        ones_map = jnp.where(same_head, 1.0, 0.0).astype(BF16)

        def delta(i, carry):
            rows = _rows_of(i)
            prod = do_ref[rows, :].astype(F32) * o_ref[rows, :].astype(F32)
            high = prod.astype(BF16)
            rest = (prod - high.astype(F32)).astype(BF16)
            ddt[rows, :] = _dot_nn(high, ones_map) + _dot_nn(rest, ones_map)
            return carry

        lax.fori_loop(0, nb, delta, 0)

        def zero(i, carry):
            rows = _rows_of(i)
            for ref in (gk, gv):
                ref[rows, :] = jnp.zeros((BLK, 256), F32)
            return carry

        def run(dil, qs, ks, vs, dos, lss, dds):
            nblk = seq // dil // BLK
            _fill_bias(bias, sl_ref, dil)
            if nblk > 1:
                lax.fori_loop(0, nb, zero, 0)

            def block(j, carry):
                rows, prows = _rows_of(j), _rows_of(jnp.maximum(j - 1, 0))
                has_prev = ((j % nblk) != 0).astype(jnp.int32)

                def keys(ref, lanes):
                    if nblk == 1:
                        return ref[rows, lanes]
                    return jnp.concatenate([ref[prows, lanes], ref[rows, lanes]], axis=0)

                for pr in range(HEAD_GROUP // 2):
                    lanes = _lane_half(pr)
                    q_ab = _split_pair(qs[rows, lanes] * SCALE)
                    do_ab = _split_pair(dos[rows, lanes])
                    k2, v2 = keys(ks, lanes), keys(vs, lanes)
                    k_ab = _split_pair(k2)
                    dq, dk2, dv2 = None, None, None
                    for ab in range(2):
                        hh = 2 * pr + ab
                        st = slice(hh * STAT_LANES, hh * STAT_LANES + 1)
                        s = _dot_nt(q_ab[ab], k2) + (bias[0, hh, :, BLK:] if nblk == 1 else bias[has_prev, hh])
                        p = jnp.exp(s - lss[rows, st])
                        dp = _dot_nt(do_ab[ab], v2)
                        ds = (p * (dp - dds[rows, st])).astype(BF16)
                        terms = (_dot_nn(ds, k_ab[ab]), _dot_tn(ds, q_ab[ab]), _dot_tn(p.astype(BF16), do_ab[ab]))
                        dq, dk2, dv2 = terms if dq is None else (dq + terms[0], dk2 + terms[1], dv2 + terms[2])
                    gq[rows, lanes] = dq * SCALE
                    if nblk == 1:
                        gk[rows, lanes] = dk2
                        gv[rows, lanes] = dv2
                    else:
                        gk[prows, lanes] += dk2[:BLK]
                        gv[prows, lanes] += dv2[:BLK]
                        gk[rows, lanes] += dk2[BLK:]
                        gv[rows, lanes] += dv2[BLK:]
                return carry

            lax.fori_loop(0, nb, block, 0, unroll=4)

        run(1, q_ref, k_ref, v_ref, do_ref, lse_ref, ddt)

        for src, dst in ((gq, aq), (gk, ak), (gv, av), (q_ref, qf), (k_ref, kf), (v_ref, vf), (do_ref, dof)):
            _split_halves(src, dst, seq)
        for dil in (4, 16):
            for src, dst in ((qf, qd), (kf, kd), (vf, vd), (dof, dod)):
                _deinterleave(src, dst, seq, dil, BF16)
            _deinterleave(lse_ref, lsd, seq, dil, F32)
            _deinterleave(ddt, ddd, seq, dil, F32)
            run(dil, qd, kd, vd, dod, lsd, ddd)
            for src, dst in ((gq, aq), (gk, ak), (gv, av)):
                _interleave(src, dst, seq, dil, True)

        def write(i, carry):
            rows = _rows_of(i)
            for src, dst in ((aq, dq_ref), (ak, dk_ref), (av, dv_ref)):
                for hf in range(2):
                    dst[rows, _lane_half(hf)] = src[hf, rows, :].astype(BF16)
            return carry

        lax.fori_loop(0, nb, write, 0)
        if ne:
            pl.when((pl.program_id(0) == bsz - 1) & (pl.program_id(1) == N_GROUPS - 1))(finish)

    def hspec(off):
        return pl.BlockSpec((seq, 256), lambda b, g: (b, off + g))

    io = pl.BlockSpec((seq, 256), lambda b, g: (b, g))
    big = lambda dt: pltpu.VMEM((seq, 256), dt)
    halves = lambda: pltpu.VMEM((2, seq, 128), F32)
    stat = lambda: pltpu.VMEM((seq, 128), F32)
    outs = pl.pallas_call(
        body, name="band_attn_bwd_exchange" if ne else "band_attn_bwd", grid=(bsz, N_GROUPS),
        in_specs=[pl.BlockSpec((None, 8, 128), lambda b, g: (g, 0, 0)), hspec(0), hspec(k_off), hspec(v_off), io, io,
                  pl.BlockSpec((seq, 128), lambda b, g: (b, g))] + [_ANY] * ne,
        out_specs=[io, io, io] + [_ANY] * ne,
        out_shape=[_sds((bsz * seq, MIX_W), BF16)] * 3 + [_sds(s.shape, s.dtype) for s in (exchange or [])],
        scratch_shapes=[halves(), halves(), halves(), halves(), stat(),
                        big(BF16), big(BF16), big(BF16), big(BF16), stat(), stat(),
                        big(F32), big(F32), big(F32), halves(), halves(), halves(), _bias_scratch()]
        + (_exchange_sems(ne) if ne else []),
        compiler_params=_params(("arbitrary", "arbitrary")))(slopes, h, h, h, dcat, mix, lse, *(exchange or []))
    return list(outs[:3]), list(outs[3:])


def _mem_attn_fwd(h, mkv, bsz, seq, q_col, tq=512):
    nq = seq // tq

    def body(q_ref, kv_ref, o_ref):
        for pr in range(2):
            lanes = _lane_half(pr)
            q_ab = _split_pair(q_ref[:, lanes])
            k = kv_ref[:, lanes]
            v_ab = _split_pair(kv_ref[:, MEM_W + pr * 128:MEM_W + (pr + 1) * 128])
            out = None
            for ab in range(2):
                s = _dot_nt(q_ab[ab], k) * SCALE
                m = jnp.max(s, axis=-1, keepdims=True)
                p = jnp.exp(s - m)
                l = jnp.sum(p, axis=-1, keepdims=True)
                term = _dot_nn(p.astype(BF16), v_ab[ab]) / l
                out = term if out is None else out + term
            o_ref[:, lanes] = out.astype(BF16)

    return pl.pallas_call(
        body, name="mem_attn_fwd", grid=(bsz, nq),
        in_specs=[pl.BlockSpec((tq, MEM_W), lambda b, i: (b * nq + i, q_col)),
                  pl.BlockSpec((N_MEM, 2 * MEM_W), lambda b, i: (b, 0))],
        out_specs=pl.BlockSpec((tq, MEM_W), lambda b, i: (b * nq + i, 0)),
        out_shape=_sds((bsz * seq, MEM_W), BF16), compiler_params=_params(("parallel", "parallel")))(h, mkv)


def _mem_attn_bwd(h, mkv, dcat, bsz, seq, q_col, tq=512):
    nq = seq // tq
    do_col = MIX_W // MEM_W

    def body(q_ref, kv_ref, do_ref, dq_ref, dkv_ref):
        @pl.when(pl.program_id(1) == 0)
        def _():
            dkv_ref[...] = jnp.zeros_like(dkv_ref)

        for pr in range(2):
            lanes = _lane_half(pr)
            vlanes = slice(MEM_W + pr * 128, MEM_W + (pr + 1) * 128)
            q_ab = _split_pair(q_ref[:, lanes])
            do_ab = _split_pair(do_ref[:, lanes])
            k, v = kv_ref[:, lanes], kv_ref[:, vlanes]
            k_ab = _split_pair(k)
            dq, dk, dv = None, None, None
            for ab in range(2):
                s = _dot_nt(q_ab[ab], k) * SCALE
                m = jnp.max(s, axis=-1, keepdims=True)
                e = jnp.exp(s - m)
                p = e / jnp.sum(e, axis=-1, keepdims=True)
                dp = _dot_nt(do_ab[ab], v)
                dd = jnp.sum(p * dp, axis=-1, keepdims=True)
                ds = (p * (dp - dd) * SCALE).astype(BF16)
                terms = (_dot_nn(ds, k_ab[ab]), _dot_tn(ds, q_ab[ab]), _dot_tn(p.astype(BF16), do_ab[ab]))
                dq, dk, dv = terms if dq is None else (dq + terms[0], dk + terms[1], dv + terms[2])
            dq_ref[:, lanes] = dq.astype(BF16)
            dkv_ref[:, lanes] += dk
            dkv_ref[:, vlanes] += dv

    return pl.pallas_call(
        body, name="mem_attn_bwd", grid=(bsz, nq),
        in_specs=[pl.BlockSpec((tq, MEM_W), lambda b, i: (b * nq + i, q_col)),
                  pl.BlockSpec((N_MEM, 2 * MEM_W), lambda b, i: (b, 0)),
                  pl.BlockSpec((tq, MEM_W), lambda b, i: (b * nq + i, do_col))],
        out_specs=[pl.BlockSpec((tq, MEM_W), lambda b, i: (b * nq + i, 0)),
                   pl.BlockSpec((N_MEM, 2 * MEM_W), lambda b, i: (b, 0))],
        out_shape=[_sds((bsz * seq, MEM_W), BF16), _sds((bsz * N_MEM, 2 * MEM_W), F32)],
        compiler_params=_params(("parallel", "arbitrary")))(h, mkv, dcat)


_GELU_C = math.sqrt(2.0 / math.pi)
_GELU_A = 0.044715


def _gelu(x):
    return 0.5 * x * (1.0 + jnp.tanh(_GELU_C * (x + _GELU_A * x * x * x)))


def _gelu_grad(x):
    th = jnp.tanh(_GELU_C * (x + _GELU_A * x * x * x))
    return 0.5 * (1.0 + th) + 0.5 * x * (1.0 - th * th) * (_GELU_C * (1.0 + 3.0 * _GELU_A * x * x))


def _tril_mask(lower):
    ri = lax.broadcasted_iota(jnp.int32, (BLK, BLK), 0)
    ci = lax.broadcasted_iota(jnp.int32, (BLK, BLK), 1)
    return (ri >= ci) if lower else (ci >= ri)


def _sgu_fwd(h, ws, bs_t, ln_g, ln_b, tm=512):
    t = h.shape[0]

    def body(u_ref, v_ref, ws_ref, bs_ref, g_ref, b_ref, o_ref):
        ug = _gelu(u_ref[...].astype(F32))
        vhat, _ = _ln_hat(_gelu(v_ref[...].astype(F32)))
        vn = (vhat * g_ref[...] + b_ref[...]).astype(BF16)
        mask = _tril_mask(True)
        first = lax.broadcasted_iota(jnp.int32, (1, 2 * HEAD_DIM), 1) < HEAD_DIM
        for pr in range(N_HEADS // 2):
            lanes = _lane_half(pr)
            w_ab = [jnp.where(mask, ws_ref[2 * pr + ab], 0).astype(BF16) for ab in range(2)]
            bias = jnp.where(first, bs_ref[:, 2 * pr:2 * pr + 1], bs_ref[:, 2 * pr + 1:2 * pr + 2])
            for c in range(tm // BLK):
                rs = slice(c * BLK, (c + 1) * BLK)
                v_ab = _split_pair(vn[rs, lanes])
                mixed = _dot_nn(w_ab[0], v_ab[0]) + _dot_nn(w_ab[1], v_ab[1]) + bias
                o_ref[rs, lanes] = (ug[rs, lanes] * mixed).astype(BF16)

    return pl.pallas_call(
        body, name="sgu_fwd", grid=(t // tm,),
        in_specs=[_rows(tm, MIX_W, 0), _rows(tm, MIX_W, 1), _whole(ws.shape), _whole(bs_t.shape), _whole(ln_g.shape), _whole(ln_b.shape)],
        out_specs=_rows(tm, MIX_W), out_shape=_sds((t, MIX_W), BF16),
        compiler_params=_params(("parallel",)))(h, h, ws, bs_t, ln_g, ln_b)


def _sgu_bwd(h, dcat, ws, ws_t, bs_t, ln_g, ln_b, tm=512):
    t = h.shape[0]

    def body(u_ref, v_ref, do_ref, ws_ref, wst_ref, bs_ref, g_ref, b_ref, dh_ref, dws_ref, dbs_ref, dg_ref, db_ref, dvn_ref):
        @pl.when(pl.program_id(0) == 0)
        def _():
            dws_ref[...] = jnp.zeros_like(dws_ref)
            dbs_ref[...] = jnp.zeros_like(dbs_ref)
            dg_ref[...] = jnp.zeros_like(dg_ref)
            db_ref[...] = jnp.zeros_like(db_ref)

        u = u_ref[...].astype(F32)
        v = v_ref[...].astype(F32)
        do = do_ref[...].astype(F32)
        ug = _gelu(u)
        vhat, rstd = _ln_hat(_gelu(v))
        vn = (vhat * g_ref[...] + b_ref[...]).astype(BF16)
        dmixed_f = do * ug
        dmixed = dmixed_f.astype(BF16)
        low, upp = _tril_mask(True), _tril_mask(False)
        first = lax.broadcasted_iota(jnp.int32, (1, 2 * HEAD_DIM), 1) < HEAD_DIM
        for pr in range(N_HEADS // 2):
            lanes = _lane_half(pr)
            w_ab = [jnp.where(low, ws_ref[2 * pr + ab], 0).astype(BF16) for ab in range(2)]
            wt_ab = [jnp.where(upp, wst_ref[2 * pr + ab], 0).astype(BF16) for ab in range(2)]
            bias = jnp.where(first, bs_ref[:, 2 * pr:2 * pr + 1], bs_ref[:, 2 * pr + 1:2 * pr + 2])
            dws_acc = [None, None]
            dbs_acc = [None, None]
            for c in range(tm // BLK):
                rs = slice(c * BLK, (c + 1) * BLK)
                vn_pair = vn[rs, lanes]
                v_ab = _split_pair(vn_pair)
                mixed = _dot_nn(w_ab[0], v_ab[0]) + _dot_nn(w_ab[1], v_ab[1]) + bias
                dh_ref[rs, lanes] = (do[rs, lanes] * mixed * _gelu_grad(u[rs, lanes])).astype(BF16)
                dm_ab = _split_pair(dmixed[rs, lanes])
                dmf_ab = _split_pair(dmixed_f[rs, lanes])
                for ab in range(2):
                    term = _dot_nt(dm_ab[ab], vn_pair)
                    dws_acc[ab] = term if dws_acc[ab] is None else dws_acc[ab] + term
                    rsum = jnp.sum(dmf_ab[ab], axis=-1, keepdims=True)
                    dbs_acc[ab] = rsum if dbs_acc[ab] is None else dbs_acc[ab] + rsum
                dvn_ref[rs, lanes] = _dot_nn(wt_ab[0], dm_ab[0]) + _dot_nn(wt_ab[1], dm_ab[1])
            for ab in range(2):
                g = 2 * pr + ab
                dws_ref[g] += jnp.where(low, dws_acc[ab], 0.0)
                dbs_ref[:, g:g + 1] += dbs_acc[ab]
        dvn = dvn_ref[...]
        dg_ref[...] += jnp.sum(dvn * vhat, axis=0, keepdims=True)
        db_ref[...] += jnp.sum(dvn, axis=0, keepdims=True)
        dxh = dvn * g_ref[...]
        m1 = jnp.mean(dxh, axis=-1, keepdims=True)
        m2 = jnp.mean(dxh * vhat, axis=-1, keepdims=True)
        dvg = rstd * (dxh - m1 - vhat * m2)
        dh_ref[:, MIX_W:] = (dvg * _gelu_grad(v)).astype(BF16)

    return pl.pallas_call(
        body, name="sgu_bwd", grid=(t // tm,),
        in_specs=[_rows(tm, MIX_W, 0), _rows(tm, MIX_W, 1), _rows(tm, MIX_W, 0), _whole(ws.shape), _whole(ws_t.shape),
                  _whole(bs_t.shape), _whole(ln_g.shape), _whole(ln_b.shape)],
        out_specs=[_rows(tm, 2 * MIX_W), _whole(ws.shape), _whole(bs_t.shape), _whole((1, MIX_W)), _whole((1, MIX_W))],
        out_shape=[_sds((t, 2 * MIX_W), BF16), _sds(ws.shape, F32), _sds(bs_t.shape, F32), _sds((1, MIX_W), F32), _sds((1, MIX_W), F32)],
        scratch_shapes=[pltpu.VMEM((tm, MIX_W), F32)],
        compiler_params=_params(("arbitrary",)))(h, h, dcat, ws, ws_t, bs_t, ln_g, ln_b)


def _row_tile(rows, cols, itemsize=4, limit=2 ** 20):
    best = rows
    for cand in (4096, 2048, 1024, 512, 256, 128, 64, 32, 16):
        if rows % cand == 0 and rows > cand:
            best = cand
            if cand * cols * itemsize <= limit:
                break
    return best


def _adamw(w, m, v, grad=None, parts=None, first_parts=None):
    rows, cols = w.shape
    rows0 = 0 if first_parts is None else first_parts.shape[1]
    tr = _row_tile(rows0 if rows0 else rows, cols)
    n0 = rows0 // tr

    def chip_sum(ref):
        acc = ref[0].astype(F32)
        for k in range(1, 4):
            acc = acc + ref[k].astype(F32)
        return acc

    def body(*refs):
        w_ref, m_ref, v_ref = refs[:3]
        go_ref, d_ref, nm_ref, nv_ref = refs[-4:]
        if parts is None:
            gv = refs[3][...]
        elif first_parts is None:
            gv = chip_sum(refs[3])
        else:
            gv = jnp.where(pl.program_id(0) < n0, chip_sum(refs[3]), chip_sum(refs[4]))
        nm = ADAM_B1 * m_ref[...] + (1.0 - ADAM_B1) * gv
        nv = ADAM_B2 * v_ref[...] + (1.0 - ADAM_B2) * (gv * gv)
        m_hat = nm / (1.0 - ADAM_B1 ** ADAM_STEP)
        v_hat = nv / (1.0 - ADAM_B2 ** ADAM_STEP)
        go_ref[...] = gv
        d_ref[...] = -ADAM_LR * (m_hat / (jnp.sqrt(v_hat) + ADAM_EPS) + ADAM_WD * w_ref[...])
        nm_ref[...] = nm
        nv_ref[...] = nv

    spec = _rows(tr, cols)
    if parts is None:
        g_specs, g_args = [spec], [grad]
    elif first_parts is None:
        g_specs, g_args = [pl.BlockSpec((4, tr, cols), lambda i: (0, i, 0))], [parts]
    else:
        g_specs = [pl.BlockSpec((4, tr, cols), lambda i: (0, jnp.minimum(i, n0 - 1), 0)),
                   pl.BlockSpec((4, tr, cols), lambda i: (0, jnp.maximum(i - n0, 0), 0))]
        g_args = [first_parts, parts]
    return pl.pallas_call(
        body, name="adamw" if parts is None else "adamw_sum_chips", grid=(rows // tr,), in_specs=[spec] * 3 + g_specs,
        out_specs=[spec] * 4, out_shape=[_sds(w.shape, F32)] * 4,
        compiler_params=_params(("parallel",)))(w, m, v, *g_args)


_ANY = pl.BlockSpec(memory_space=pl.ANY)
_MESH = pl.DeviceIdType.MESH


def _all_gather(name, blocks):
    nt = len(blocks)

    def body(*refs):
        start, finish = _gather_protocol(refs[:nt], refs[nt:2 * nt], *refs[2 * nt:])
        start()
        finish()

    return pl.pallas_call(
        body, name=name, out_shape=_gather_shapes(blocks), in_specs=[_ANY] * nt, out_specs=[_ANY] * nt,
        scratch_shapes=_gather_sems(nt))(*blocks)


def _gather_shapes(blocks):
    return [_sds((N_DEV,) + b.shape, b.dtype) for b in blocks]


def _gather_sems(nt):
    return [pltpu.SemaphoreType.DMA((nt, 7)), pltpu.SemaphoreType.DMA((nt, 7)), pltpu.SemaphoreType.DMA((nt,))]


def _gather_protocol(x_refs, out_refs, send_sems, recv_sems, local_sems):
    nt = len(x_refs)
    x, y, c = lax.axis_index("x"), lax.axis_index("y"), lax.axis_index("c")
    me, sibling = (x, y, c), (x, y, 1 - c)
    chips = [(1 - x, y), (x, 1 - y), (1 - x, 1 - y)]

    def slot(t, px, py, pc):
        return out_refs[t].at[4 * px + 2 * py + pc]

    def copy(t, k, blk, to, src=None):
        return pltpu.make_async_remote_copy(
            src_ref=slot(t, *blk) if src is None else src, dst_ref=slot(t, *blk),
            send_sem=send_sems.at[t, k], recv_sem=recv_sems.at[t, k], device_id=to, device_id_type=_MESH)

    def own_copies():
        mine = [pltpu.make_async_copy(x_refs[t], slot(t, *me), local_sems.at[t]) for t in range(nt)]
        first = []
        for t in range(nt):
            first.append(copy(t, 0, me, sibling, src=x_refs[t]))
            first += [copy(t, 1 + j, me, (*chip, c), src=x_refs[t]) for j, chip in enumerate(chips)]
        return mine, first

    def start():
        mine, first = own_copies()
        for cp in mine + first:
            cp.start()

    def finish():
        mine, first = own_copies()
        passed = []
        for j, chip in enumerate(chips):
            for t in range(nt):
                copy(t, 1 + j, (*chip, c), me).wait_recv()
                fwd = copy(t, 4 + j, (*chip, c), sibling)
                fwd.start()
                passed.append(fwd)
        for t in range(nt):
            copy(t, 0, sibling, me).wait_recv()
        for j, chip in enumerate(chips):
            for t in range(nt):
                copy(t, 4 + j, (*chip, 1 - c), me).wait_recv()
        for cp in first + passed:
            cp.wait_send()
        for cp in mine:
            cp.wait()

    return start, finish


def _swap_with_sibling(packed):
    nt = len(packed)

    def body(*refs):
        p_refs, got_refs = refs[:nt], refs[nt:2 * nt]
        send_sems, recv_sems = refs[2 * nt:]
        x, y, c = lax.axis_index("x"), lax.axis_index("y"), lax.axis_index("c")
        copies = [
            pltpu.make_async_remote_copy(
                src_ref=p_refs[t].at[1 - c], dst_ref=got_refs[t], send_sem=send_sems.at[t], recv_sem=recv_sems.at[t],
                device_id=(x, y, 1 - c), device_id_type=_MESH)
            for t in range(nt)]
        for cp in copies:
            cp.start()
        for cp in copies:
            cp.wait_recv()
        for cp in copies:
            cp.wait_send()

    return pl.pallas_call(
        body, name="grad_swap_sibling", out_shape=[_sds(p.shape[1:], p.dtype) for p in packed], in_specs=[_ANY] * nt,
        out_specs=[_ANY] * nt,
        scratch_shapes=[pltpu.SemaphoreType.DMA((nt,)), pltpu.SemaphoreType.DMA((nt,))])(*packed)


def _chip_sum(packed, got):
    _, nchip, rows, cols = packed.shape
    tr = _row_tile(rows, cols, 2)
    core = lax.axis_index("c").astype(jnp.int32).reshape(1)

    def body(c_ref, p_ref, g_ref, o_ref):
        o_ref[...] = (p_ref[...].astype(F32) + g_ref[...].astype(F32)).astype(o_ref.dtype)

    grid_spec = pltpu.PrefetchScalarGridSpec(
        num_scalar_prefetch=1, grid=(nchip, rows // tr),
        in_specs=[pl.BlockSpec((None, None, tr, cols), lambda k, i, c: (c[0], k, i, 0)),
                  pl.BlockSpec((None, tr, cols), lambda k, i, c: (k, i, 0))],
        out_specs=pl.BlockSpec((None, tr, cols), lambda k, i, c: (k, i, 0)))
    return pl.pallas_call(
        body, name="grad_chip_sum", grid_spec=grid_spec, out_shape=_sds(got.shape, got.dtype),
        compiler_params=_params(("parallel", "parallel")))(core, packed, got)


def _exchange_chips(chip_sums):
    nt = len(chip_sums)

    def body(*refs):
        start, finish = _exchange_protocol(refs[:nt], refs[nt:2 * nt], *refs[2 * nt:])
        start()
        finish()

    return pl.pallas_call(
        body, name="grad_exchange_chips", out_shape=[_sds(s.shape, s.dtype) for s in chip_sums], in_specs=[_ANY] * nt,
        out_specs=[_ANY] * nt, scratch_shapes=_exchange_sems(nt))(*chip_sums)


def _exchange_sems(nt):
    return [pltpu.SemaphoreType.DMA((nt, 3)), pltpu.SemaphoreType.DMA((nt, 3)), pltpu.SemaphoreType.DMA((nt,))]


def _exchange_protocol(s_refs, got_refs, send_sems, recv_sems, local_sems):
    nt = len(s_refs)
    x, y, c = lax.axis_index("x"), lax.axis_index("y"), lax.axis_index("c")
    my_chip = 2 * x + y
    chips = [(1 - x, y), (x, 1 - y), (1 - x, 1 - y)]

    def copy(t, j, src_chip, dst_chip):
        px, py = chips[j]
        return pltpu.make_async_remote_copy(
            src_ref=s_refs[t].at[src_chip], dst_ref=got_refs[t].at[dst_chip], send_sem=send_sems.at[t, j],
            recv_sem=recv_sems.at[t, j], device_id=(px, py, c), device_id_type=_MESH)

    def own_copies():
        mine = [pltpu.make_async_copy(s_refs[t].at[my_chip], got_refs[t].at[my_chip], local_sems.at[t]) for t in range(nt)]
        sends = [copy(t, j, 2 * px + py, my_chip) for t in range(nt) for j, (px, py) in enumerate(chips)]
        return mine, sends

    def start():
        mine, sends = own_copies()
        for cp in mine + sends:
            cp.start()

    def finish():
        mine, sends = own_copies()
        for j, (px, py) in enumerate(chips):
            for t in range(nt):
                copy(t, j, my_chip, 2 * px + py).wait_recv()
        for cp in sends:
            cp.wait_send()
        for cp in mine:
            cp.wait()

    return start, finish


def _sum_chips(got):
    _, rows, cols = got.shape
    tr = _row_tile(rows, cols)

    def body(g_ref, o_ref):
        acc = g_ref[0].astype(F32)
        for k in range(1, 4):
            acc = acc + g_ref[k].astype(F32)
        o_ref[...] = acc

    return pl.pallas_call(
        body, name="grad_sum_chips", grid=(rows // tr,), in_specs=[pl.BlockSpec((4, tr, cols), lambda i: (0, i, 0))],
        out_specs=pl.BlockSpec((tr, cols), lambda i: (i, 0)), out_shape=_sds((rows, cols), F32),
        compiler_params=_params(("parallel",)))(got)


_COL_SHARDED = ("a_w_in", "b_w_in", "w_gate", "w_up")
_ROW_SHARDED = ("w_mem_kv", "w_out", "w_down")
_BIG = ("a_w_in", "b_w_in", "w_mem_kv", "w_out", "w_gate", "w_up", "w_down")
_SGU_LN = ("sgu_ln_g", "sgu_ln_b")
_LN4 = ("ln_mix_g", "ln_mix_b", "ln_ffn_g", "ln_ffn_b")
_REPLICATED = ("sgu_w_s", "sgu_b_s") + _LN4


def _unshard(name, gathered):
    if name in _COL_SHARDED or name in _SGU_LN:
        moved = jnp.moveaxis(gathered, 0, -2)
        return moved.reshape(moved.shape[:-2] + (moved.shape[-2] * moved.shape[-1],))
    moved = jnp.moveaxis(gathered, 0, 1)
    return moved.reshape((moved.shape[0], moved.shape[1] * moved.shape[2]) + moved.shape[3:])


_LAID_OUT_IN_KERNEL = _COL_SHARDED


def _after_gather(name, gathered):
    return gathered if name in _LAID_OUT_IN_KERNEL else _unshard(name, gathered)


def _by_shard(name, full):
    if name in _COL_SHARDED or name in _SGU_LN:
        split = full.reshape(full.shape[:-1] + (N_DEV, full.shape[-1] // N_DEV))
        return jnp.moveaxis(split, -2, 0)
    split = full.reshape((full.shape[0], N_DEV, full.shape[1] // N_DEV) + full.shape[2:])
    return jnp.moveaxis(split, 1, 0)


def _layer_keys(i):
    return [("a_w_in" if i % 2 == 0 else "b_w_in", i // 2)] + [(n, i) for n in ("w_mem_kv", "w_out", "w_gate", "w_up", "w_down")]


_GATHER_FIRST = _layer_keys(0)[:2]
_GATHER_LATER = (_layer_keys(0)[2:] + _layer_keys(1), _layer_keys(2), _layer_keys(3))


def _shard_block(shards, key):
    name, idx = key
    return shards[name][idx:idx + 1].astype(BF16)


def _gather_first(shards):
    blocks = [_shard_block(shards, k) for k in _GATHER_FIRST] + [shards[n] for n in _SGU_LN]
    gathered = _all_gather("first_all_gather", blocks)
    full = {k: _after_gather(k[0], g) for k, g in zip(_GATHER_FIRST, gathered)}
    sgu_ln = {n: _unshard(n, g) for n, g in zip(_SGU_LN, gathered[len(_GATHER_FIRST):])}
    return full, sgu_ln


def _two_level(by_dest):
    shp = by_dest.shape[1:]
    split = by_dest.astype(BF16).reshape((4, 2) + shp).swapaxes(0, 1)
    return split.reshape(2, 4, int(np.prod(shp[:-1])), shp[-1])


_EARLY = _BIG + _SGU_LN


def _chip_sums_of_early(grads):
    packed = [_two_level(_by_shard(n, jnp.stack(grads[n][1:] if n == "a_w_in" else grads[n]))) for n in _EARLY]
    ln4 = jnp.stack([jnp.stack(grads[n]) for n in _LN4])
    rep = [jnp.stack(grads["sgu_w_s"]).reshape(N_DEV, -1, BLK), jnp.stack(grads["sgu_b_s"]).reshape(N_DEV, -1, BLK),
           ln4.reshape(N_DEV, -1, D_MODEL)]
    packed += [_two_level(r) for r in rep]
    got = _swap_with_sibling(packed)
    return [_chip_sum(p, g) for p, g in zip(packed, got)]


def _finish_replicated(parts, shapes):
    w_s, b_s, ln_all = _all_gather("replicated_grads_all_gather", [_sum_chips(p) for p in parts])
    ln_all = ln_all.reshape((len(_LN4),) + tuple(shapes[_LN4[0]]))
    rep_grads = {"sgu_w_s": w_s.reshape(shapes["sgu_w_s"]), "sgu_b_s": b_s.reshape(shapes["sgu_b_s"])}
    rep_grads.update({n: ln_all[i] for i, n in enumerate(_LN4)})
    return rep_grads


def _reduce_last(grad_a_first):
    packed = [_two_level(_by_shard("a_w_in", grad_a_first))]
    got = _swap_with_sibling(packed)
    return _exchange_chips([_chip_sum(packed[0], got[0])])[0]


def _as_2d(a):
    if a.ndim == 1:
        return a.reshape(1, -1)
    return a.reshape(-1, a.shape[-1])


def kernel(x, mem, a_w_in, b_w_in, sgu_ln_g, sgu_ln_b, sgu_w_s, sgu_b_s, w_mem_kv, w_out, ln_mix_g, ln_mix_b, w_gate, w_up, w_down, ln_ffn_g, ln_ffn_b, loss_target, m_a_w_in, m_b_w_in, m_sgu_ln_g, m_sgu_ln_b, m_sgu_w_s, m_sgu_b_s, m_w_mem_kv, m_w_out, m_ln_mix_g, m_ln_mix_b, m_w_gate, m_w_up, m_w_down, m_ln_ffn_g, m_ln_ffn_b, v_a_w_in, v_b_w_in, v_sgu_ln_g, v_sgu_ln_b, v_sgu_w_s, v_sgu_b_s, v_w_mem_kv, v_w_out, v_ln_mix_g, v_ln_mix_b, v_w_gate, v_w_up, v_w_down, v_ln_ffn_g, v_ln_ffn_b):
    names = ("a_w_in", "b_w_in", "sgu_ln_g", "sgu_ln_b", "sgu_w_s", "sgu_b_s", "w_mem_kv", "w_out", "ln_mix_g", "ln_mix_b",
             "w_gate", "w_up", "w_down", "ln_ffn_g", "ln_ffn_b")
    weights = dict(zip(names, (a_w_in, b_w_in, sgu_ln_g, sgu_ln_b, sgu_w_s, sgu_b_s, w_mem_kv, w_out, ln_mix_g, ln_mix_b,
                               w_gate, w_up, w_down, ln_ffn_g, ln_ffn_b)))
    mom_m = dict(zip(names, (m_a_w_in, m_b_w_in, m_sgu_ln_g, m_sgu_ln_b, m_sgu_w_s, m_sgu_b_s, m_w_mem_kv, m_w_out, m_ln_mix_g,
                             m_ln_mix_b, m_w_gate, m_w_up, m_w_down, m_ln_ffn_g, m_ln_ffn_b)))
    mom_v = dict(zip(names, (v_a_w_in, v_b_w_in, v_sgu_ln_g, v_sgu_ln_b, v_sgu_w_s, v_sgu_b_s, v_w_mem_kv, v_w_out, v_ln_mix_g,
                             v_ln_mix_b, v_w_gate, v_w_up, v_w_down, v_ln_ffn_g, v_ln_ffn_b)))
    full, sgu_ln = _gather_first(weights)
    pending = [(keys, [_shard_block(weights, k) for k in keys]) for keys in _GATHER_LATER]
    loss_part, grad_x, local, early = _local_step(
        x, mem, loss_target, full, sgu_ln, {n: weights[n] for n in _REPLICATED}, pending)
    loss = lax.psum(loss_part[0, 0], ("x", "y", "c"))
    early_parts = dict(zip(_EARLY, early))
    rep_grads = _finish_replicated(early[len(_EARLY):], {n: weights[n].shape for n in _REPLICATED})
    a_first_parts = _reduce_last(local["a_w_in"][:1])

    reduced, deltas, new_m, new_v = {}, {}, {}, {}
    for n in names:
        w2, m2, v2 = _as_2d(weights[n]), _as_2d(mom_m[n]), _as_2d(mom_v[n])
        if n in early_parts:
            outs = _adamw(w2, m2, v2, parts=early_parts[n], first_parts=a_first_parts if n == "a_w_in" else None)
        else:
            outs = _adamw(w2, m2, v2, grad=_as_2d(rep_grads[n]))
        reduced[n], deltas[n], new_m[n], new_v[n] = (a.reshape(weights[n].shape) for a in outs)

    return (loss, grad_x, *[reduced[n] for n in names], *[deltas[n] for n in names],
            *[new_m[n] for n in names], *[new_v[n] for n in names])


def _local_step(x, mem, loss_target, full, sgu_ln, small, pending=None):
    sgu_w_s, sgu_b_s = small["sgu_w_s"], small["sgu_b_s"]
    ln_mix_g, ln_mix_b, ln_ffn_g, ln_ffn_b = (small[n] for n in ("ln_mix_g", "ln_mix_b", "ln_ffn_g", "ln_ffn_b"))
    bsz, seq, _ = x.shape
    tokens = bsz * seq
    slopes = _alibi_table()
    full = dict(full)
    exchanging = pending is not None
    pending = list(pending or [])

    def weight(name, idx):
        return full[(name, idx)][0]

    def next_group():
        return pending[0][1] if pending else None

    def landed(gathered):
        if gathered:
            keys, _ = pending.pop(0)
            full.update({k: _after_gather(k[0], g) for k, g in zip(keys, gathered)})

    res = (x.reshape(tokens, D_MODEL),)
    xb = res[0].astype(BF16)
    memb = mem.reshape(bsz * N_MEM, D_MODEL).astype(BF16)
    tgt = loss_target.reshape(tokens, D_MODEL)

    saved = []
    for i in range(DEPTH):
        j = i // 2
        dil_layer = i % 2 == 0
        mkv = _linear_nn("mem_kv", memb, weight("w_mem_kv", i))
        h, w_in = _linear_nn_gathered("in_proj_a" if dil_layer else "in_proj_b", xb,
                                      full[("a_w_in" if dil_layer else "b_w_in", j)][:, 0])
        st = dict(xb=xb, h=h, mkv=mkv, w_in=w_in)
        if dil_layer:
            mix, st["lse"], gathered = _band_attn_fwd_fused(h, slopes, bsz, seq, gather=next_group() if i == 0 else None)
            landed(gathered)
            q_col = 3 * MIX_W // MEM_W
        else:
            st["ws"] = sgu_w_s[j]
            st["bs_t"] = sgu_b_s[j].T
            st["ln_g"] = sgu_ln["sgu_ln_g"][j].reshape(1, MIX_W)
            st["ln_b"] = sgu_ln["sgu_ln_b"][j].reshape(1, MIX_W)
            mix = _sgu_fwd(h, st["ws"], st["bs_t"], st["ln_g"], st["ln_b"])
            q_col = 2 * MIX_W // MEM_W
        mo = _mem_attn_fwd(h, mkv, bsz, seq, q_col)
        w_out, w_down = weight("w_out", i), weight("w_down", i)
        mix_ln = (ln_mix_g[i].reshape(1, D_MODEL), ln_mix_b[i].reshape(1, D_MODEL))
        ffn_ln = (ln_ffn_g[i].reshape(1, D_MODEL), ln_ffn_b[i].reshape(1, D_MODEL))
        r1, x1b = _proj_ln_fwd("out_proj_ln", [mix, mo], w_out, res, *mix_ln)
        gt, up, act, w_gu, gathered = _ffn_up_fwd(x1b, full[("w_gate", i)][:, 0], full[("w_up", i)][:, 0],
                                                  gather=next_group() if i < 2 else None)
        landed(gathered)
        r2, xb = _proj_ln_fwd("ffn_down_ln", [act], w_down, (r1, *mix_ln), *ffn_ln)
        res = (r2, *ffn_ln)
        st.update(mix=mix, mo=mo, q_col=q_col, r1=r1, x1b=x1b, gt=gt, up=up, act=act, r2=r2,
                  w_out=w_out, w_down=w_down, w_gu=w_gu)
        saved.append(st)

    dr2, dr2b, dg, db, loss_part = _loss_ln_bwd(*res, tgt)

    early_parts = None
    per_pair = ("a_w_in", "b_w_in", "sgu_ln_g", "sgu_ln_b", "sgu_w_s", "sgu_b_s")
    grads = {n: [None] * (DEPTH // 2 if n in per_pair else DEPTH) for n in _BIG + _SGU_LN + _REPLICATED}
    for i in reversed(range(DEPTH)):
        j = i // 2
        st = saved[i]
        dil_layer = i % 2 == 0
        w_in = st["w_in"]
        grads["ln_ffn_g"][i], grads["ln_ffn_b"][i] = dg[0], db[0]
        dgu = _ffn_down_bwd(dr2b, st["w_down"], st["gt"], st["up"])
        grads["w_down"][i] = _mm_tn("grad_w_down", st["act"], dr2b)
        dr1, dr1b, dg, db = _linear_nt("ffn_up_bwd", [dgu], st["w_gu"], dr2, F32,
                                       ln=(st["r1"], ln_mix_g[i].reshape(1, D_MODEL)))
        grads["ln_mix_g"][i], grads["ln_mix_b"][i] = dg[0], db[0]
        dw_gu = _mm_tn("grad_w_gate_up", st["x1b"], dgu)
        grads["w_gate"][i], grads["w_up"][i] = dw_gu[:, :D_FF], dw_gu[:, D_FF:]
        dcat = _linear_nt("out_proj_bwd", [dr1b], st["w_out"], None, BF16)
        grads["w_out"][i] = jnp.concatenate(
            [_mm_tn("grad_w_out_mix", st["mix"], dr1b), _mm_tn("grad_w_out_mem", st["mo"], dr1b)], axis=0)
        dqm, dmkv = _mem_attn_bwd(st["h"], st["mkv"], dcat, bsz, seq, st["q_col"])
        grads["w_mem_kv"][i] = _mm_tn("grad_w_mem_kv", memb, dmkv.astype(BF16))
        if dil_layer:
            early_sums = _chip_sums_of_early(grads) if (i == 0 and exchanging) else None
            dh_parts, exchanged = _band_attn_bwd_fused(st["h"], dcat, st["mix"], st["lse"], slopes, bsz, seq,
                                                       exchange=early_sums)
            if early_sums is not None:
                early_parts = exchanged
        else:
            ws_t = jnp.swapaxes(st["ws"], -1, -2)
            dh_main, dws, dbs_t, dlg, dlb = _sgu_bwd(st["h"], dcat, st["ws"], ws_t, st["bs_t"], st["ln_g"], st["ln_b"])
            grads["sgu_w_s"][j], grads["sgu_b_s"][j] = dws, dbs_t.T
            grads["sgu_ln_g"][j], grads["sgu_ln_b"][j] = dlg[0], dlb[0]
            dh_parts = [dh_main]
        name = "in_proj_bwd_a" if dil_layer else "in_proj_bwd_b"
        if i > 0:
            dr2, dr2b, dg, db = _linear_nt(name, [*dh_parts, dqm], w_in, dr1, F32,
                                           ln=(saved[i - 1]["r2"], ln_ffn_g[i - 1].reshape(1, D_MODEL)))
        else:
            grad_x = _linear_nt(name + "_x", [*dh_parts, dqm], w_in, dr1, F32).reshape(x.shape)
        grads["a_w_in" if dil_layer else "b_w_in"][j] = jnp.concatenate(
            [_mm_tn("grad_w_in_part", st["xb"], part) for part in dh_parts] + [_mm_tn("grad_w_in_qm", st["xb"], dqm)], axis=1)
    return loss_part, grad_x, {n: jnp.stack(g) for n, g in grads.items()}, early_parts
```

```python
import functools
import math

import numpy as np
import jax
import jax.numpy as jnp
from jax import lax
from jax.experimental import pallas as pl
from jax.experimental.pallas import tpu as pltpu

F32 = jnp.float32
BF16 = jnp.bfloat16

D_MODEL = 1024
DEPTH = 4
N_MEM = 256
HEAD_DIM = 64
N_HEADS = 12
MIX_W = N_HEADS * HEAD_DIM
MEM_W = 4 * HEAD_DIM
DIL_PATTERNS = ((128, 1), (512, 4), (2048, 16))
BLK = 128
HEAD_GROUP = 4
N_GROUPS = N_HEADS // HEAD_GROUP
D_FF = 2816
FF_CHUNKS = 2
ALPHA = (2 * DEPTH) ** 0.25
LN_EPS = 1e-5
SCALE = HEAD_DIM ** -0.5
NEG = -1e30
N_DEV = 8

ADAM_LR, ADAM_B1, ADAM_B2, ADAM_EPS, ADAM_WD, ADAM_STEP = 0.001, 0.9, 0.999, 1e-08, 0.01, 10

VMEM_LIMIT = 56 * 2 ** 20
STAT_LANES = 32
STAT_W = N_HEADS * STAT_LANES


def _dot_nn(a, b):
    return lax.dot_general(a, b, (((1,), (0,)), ((), ())), preferred_element_type=F32)


def _dot_nt(a, b):
    return lax.dot_general(a, b, (((1,), (1,)), ((), ())), preferred_element_type=F32)


def _dot_tn(a, b):
    return lax.dot_general(a, b, (((0,), (0,)), ((), ())), preferred_element_type=F32)


def _ln_hat(r):
    mu = jnp.mean(r, axis=-1, keepdims=True)
    xc = r - mu
    var = jnp.mean(xc * xc, axis=-1, keepdims=True)
    rstd = lax.rsqrt(var + LN_EPS)
    return xc * rstd, rstd


def _params(sem):
    return pltpu.CompilerParams(dimension_semantics=sem, vmem_limit_bytes=VMEM_LIMIT)


def _rows(tm, c, col=0):
    return pl.BlockSpec((tm, c), lambda i: (i, col))


def _whole(shape):
    nd = len(shape)
    return pl.BlockSpec(tuple(shape), lambda *_: (0,) * nd)


def _resident(shape):
    nd = len(shape)
    return pl.BlockSpec(tuple(shape), lambda *_: (0,) * nd, pipeline_mode=pl.Buffered(1))


def _sds(shape, dtype):
    return jax.ShapeDtypeStruct(tuple(shape), dtype)


def _linear_nn(name, a, w, tm=512):
    t, k = a.shape
    n = w.shape[1]
    tm = min(tm, t)

    def body(a_ref, w_ref, o_ref):
        o_ref[...] = _dot_nn(a_ref[...], w_ref[...]).astype(BF16)

    return pl.pallas_call(
        body, name=name, grid=(t // tm,), in_specs=[_rows(tm, k), _resident(w.shape)], out_specs=_rows(tm, n),
        out_shape=_sds((t, n), BF16), compiler_params=_params(("parallel",)))(a, w)


def _linear_nn_gathered(name, a, shards, tm=512):
    t, k = a.shape
    n8 = shards.shape[2]
    n = N_DEV * n8

    def body(a_ref, s_ref, o_ref, w_ref):
        @pl.when(pl.program_id(0) == 0)
        def _():
            for s in range(N_DEV):
                w_ref[:, s * n8:(s + 1) * n8] = s_ref[s]

        o_ref[...] = _dot_nn(a_ref[...], w_ref[...]).astype(BF16)

    return pl.pallas_call(
        body, name=name, grid=(t // tm,), in_specs=[_rows(tm, k), _resident(shards.shape)],
        out_specs=[_rows(tm, n), _whole((k, n))], out_shape=[_sds((t, n), BF16), _sds((k, n), BF16)],
        compiler_params=_params(("arbitrary",)))(a, shards)


def _proj_ln_fwd(name, lhs, w, res, g, b, gather=None, tm=512):
    t = res[0].shape[0]
    n_lhs = len(lhs)
    n_res = len(res)
    ng = 0 if gather is None else len(gather)
    steps = t // tm
    n_in = n_lhs + 3 + n_res

    def body(*refs):
        lhs_refs = refs[:n_lhs]
        w_ref = refs[n_lhs]
        res_refs = refs[n_lhs + 1:n_lhs + 1 + n_res]
        g_ref, b_ref = refs[n_in - 2:n_in]
        r_ref, xnb_ref = refs[n_in + ng:n_in + ng + 2]
        if ng:
            start, finish = _gather_protocol(refs[n_in:n_in + ng], refs[n_in + ng + 2:n_in + 2 * ng + 2], *refs[n_in + 2 * ng + 2:])
            pl.when(pl.program_id(0) == 0)(start)
        y, off = None, 0
        for lr in lhs_refs:
            k = lr.shape[1]
            term = _dot_nn(lr[...], w_ref[off:off + k, :])
            y = term if y is None else y + term
            off += k
        x_res = res_refs[0][...]
        if n_res == 3:
            x_res = _ln_hat(x_res)[0] * res_refs[1][...] + res_refs[2][...]
        r = ALPHA * x_res + y
        r_ref[...] = r
        xnb_ref[...] = (_ln_hat(r)[0] * g_ref[...] + b_ref[...]).astype(BF16)
        if ng:
            pl.when(pl.program_id(0) == steps - 1)(finish)

    vec = _whole((1, D_MODEL))
    in_specs = ([_rows(tm, a.shape[1]) for a in lhs] + [_resident(w.shape), _rows(tm, D_MODEL)] + [vec] * (n_res - 1) + [vec, vec])
    outs = pl.pallas_call(
        body, name=name + "_gather" if ng else name, grid=(steps,), in_specs=in_specs + [_ANY] * ng,
        out_specs=[_rows(tm, D_MODEL)] * 2 + [_ANY] * ng,
        out_shape=[_sds((t, D_MODEL), F32), _sds((t, D_MODEL), BF16)] + (_gather_shapes(gather) if ng else []),
        scratch_shapes=_gather_sems(ng) if ng else [],
        compiler_params=_params(("arbitrary" if ng else "parallel",)))(*lhs, w, *res, g, b, *(gather or []))
    return outs[0], outs[1], list(outs[2:])


def _ffn_up_fwd(xb, gate_shards, up_shards, gather=None, tm=256):
    t = xb.shape[0]
    ng = 0 if gather is None else len(gather)
    steps = t // tm
    n8 = gate_shards.shape[2]

    def body(*refs):
        x_ref, gs_ref, us_ref = refs[:3]
        g_ref, u_ref, a_ref, wgu_ref = refs[3 + ng:7 + ng]
        w_ref, w_sem = refs[7 + 2 * ng:9 + 2 * ng]
        keep = pltpu.make_async_copy(w_ref, wgu_ref, w_sem)

        @pl.when(pl.program_id(0) == 0)
        def _():
            for s in range(N_DEV):
                w_ref[:, s * n8:(s + 1) * n8] = gs_ref[s]
                w_ref[:, D_FF + s * n8:D_FF + (s + 1) * n8] = us_ref[s]
            keep.start()

        if ng:
            start, finish = _gather_protocol(refs[3:3 + ng], refs[7 + ng:7 + 2 * ng], *refs[9 + 2 * ng:])
            pl.when(pl.program_id(0) == 0)(start)
        xv = x_ref[...]
        for c in range(FF_CHUNKS):
            cols = slice(c * D_FF // FF_CHUNKS, (c + 1) * D_FF // FF_CHUNKS)
            gt = _dot_nn(xv, w_ref[:, cols])
            up = _dot_nn(xv, w_ref[:, D_FF + cols.start:D_FF + cols.stop])
            g_ref[:, cols] = gt.astype(BF16)
            u_ref[:, cols] = up.astype(BF16)
            a_ref[:, cols] = (gt * jax.nn.sigmoid(gt) * up).astype(BF16)
        if ng:
            pl.when(pl.program_id(0) == steps - 1)(finish)
        pl.when(pl.program_id(0) == steps - 1)(keep.wait)

    k = gate_shards.shape[1]
    outs = pl.pallas_call(
        body, name="ffn_up_fwd_gather" if ng else "ffn_up_fwd", grid=(steps,),
        in_specs=[_rows(tm, D_MODEL), _resident(gate_shards.shape), _resident(up_shards.shape)] + [_ANY] * ng,
        out_specs=[_rows(tm, D_FF)] * 3 + [_ANY] + [_ANY] * ng,
        out_shape=[_sds((t, D_FF), BF16)] * 3 + [_sds((k, 2 * D_FF), BF16)] + (_gather_shapes(gather) if ng else []),
        scratch_shapes=[pltpu.VMEM((k, 2 * D_FF), BF16), pltpu.SemaphoreType.DMA(())] + (_gather_sems(ng) if ng else []),
        compiler_params=_params(("arbitrary",)))(xb, gate_shards, up_shards, *(gather or []))
    return outs[0], outs[1], outs[2], outs[3], list(outs[4:])


def _ln_bwd_rows(dxn, xhat, rstd, g_ref, dr_ref, drb_ref, dg_ref, db_ref):
    @pl.when(pl.program_id(0) == 0)
    def _():
        dg_ref[...] = jnp.zeros_like(dg_ref)
        db_ref[...] = jnp.zeros_like(db_ref)

    dxh = dxn * g_ref[...]
    m1 = jnp.mean(dxh, axis=-1, keepdims=True)
    m2 = jnp.mean(dxh * xhat, axis=-1, keepdims=True)
    dr = rstd * (dxh - m1 - xhat * m2)
    dr_ref[...] = dr
    drb_ref[...] = dr.astype(BF16)
    dg_ref[...] += jnp.sum(dxn * xhat, axis=0, keepdims=True)
    db_ref[...] += jnp.sum(dxn, axis=0, keepdims=True)


def _ln_bwd_outs(t, tm):
    vec = _whole((1, D_MODEL))
    specs = [_rows(tm, D_MODEL), _rows(tm, D_MODEL), vec, vec]
    shapes = [_sds((t, D_MODEL), F32), _sds((t, D_MODEL), BF16), _sds((1, D_MODEL), F32), _sds((1, D_MODEL), F32)]
    return specs, shapes


def _loss_ln_bwd(r, g, b, tgt, tm=512):
    t = r.shape[0]

    def body(r_ref, g_ref, b_ref, t_ref, dr_ref, drb_ref, dg_ref, db_ref, l_ref):
        @pl.when(pl.program_id(0) == 0)
        def _():
            l_ref[...] = jnp.zeros_like(l_ref)

        xhat, rstd = _ln_hat(r_ref[...])
        e = xhat * g_ref[...] + b_ref[...] - t_ref[...]
        l_ref[...] += jnp.sum(e * e) * (0.5 / D_MODEL)
        _ln_bwd_rows(e * (1.0 / D_MODEL), xhat, rstd, g_ref, dr_ref, drb_ref, dg_ref, db_ref)

    vec = _whole((1, D_MODEL))
    specs, shapes = _ln_bwd_outs(t, tm)
    return pl.pallas_call(
        body, name="loss_ln_bwd", grid=(t // tm,), in_specs=[_rows(tm, D_MODEL), vec, vec, _rows(tm, D_MODEL)],
        out_specs=specs + [_whole((1, 128))], out_shape=shapes + [_sds((1, 128), F32)],
        compiler_params=_params(("arbitrary",)))(r, g, b, tgt)


def _ffn_down_bwd(drb, wd, gt, up, tm=512):
    t = drb.shape[0]

    def body(d_ref, w_ref, g_ref, u_ref, o_ref):
        dv = d_ref[...]
        for c in range(FF_CHUNKS):
            cols = slice(c * D_FF // FF_CHUNKS, (c + 1) * D_FF // FF_CHUNKS)
            da = _dot_nt(dv, w_ref[cols, :])
            g = g_ref[:, cols].astype(F32)
            u = u_ref[:, cols].astype(F32)
            sg = jax.nn.sigmoid(g)
            o_ref[:, cols] = (da * u * (sg * (1.0 + g * (1.0 - sg)))).astype(BF16)
            o_ref[:, D_FF + cols.start:D_FF + cols.stop] = (da * (g * sg)).astype(BF16)

    return pl.pallas_call(
        body, name="ffn_down_bwd", grid=(t // tm,),
        in_specs=[_rows(tm, D_MODEL), _resident(wd.shape), _rows(tm, D_FF), _rows(tm, D_FF)],
        out_specs=_rows(tm, 2 * D_FF), out_shape=_sds((t, 2 * D_FF), BF16),
        compiler_params=_params(("parallel",)))(drb, wd, gt, up)


def _linear_nt(name, lhs, w, res, out_dtype, ln=None, tm=512):
    t = lhs[0].shape[0]
    n_lhs = len(lhs)
    n_out = w.shape[0]
    n_in = n_lhs + 1 + (res is not None) + (2 if ln else 0)

    def body(*refs):
        lhs_refs = refs[:n_lhs]
        w_ref = refs[n_lhs]
        y, off = None, 0
        for lr in lhs_refs:
            k = lr.shape[1]
            term = _dot_nt(lr[...], w_ref[:, off:off + k])
            y = term if y is None else y + term
            off += k
        if res is not None:
            y = ALPHA * refs[n_lhs + 1][...] + y
        if ln is None:
            refs[-1][...] = y.astype(out_dtype)
        else:
            r_ref, g_ref = refs[n_in - 2:n_in]
            xhat, rstd = _ln_hat(r_ref[...])
            _ln_bwd_rows(y, xhat, rstd, g_ref, *refs[n_in:])

    in_specs = [_rows(tm, a.shape[1]) for a in lhs] + [_resident(w.shape)]
    args = list(lhs) + [w]
    if res is not None:
        in_specs.append(_rows(tm, n_out))
        args.append(res)
    if ln is None:
        out_specs, out_shape, sem = _rows(tm, n_out), _sds((t, n_out), out_dtype), "parallel"
    else:
        in_specs += [_rows(tm, D_MODEL), _whole((1, D_MODEL))]
        args += list(ln)
        (out_specs, out_shape), sem = _ln_bwd_outs(t, tm), "arbitrary"
    return pl.pallas_call(
        body, name=name, grid=(t // tm,), in_specs=in_specs, out_specs=out_specs, out_shape=out_shape,
        compiler_params=_params((sem,)))(*args)


def _pick_tile(n, limit):
    if n <= limit:
        return n
    best = 128
    for cand in range(128, limit + 1, 128):
        if n % cand == 0:
            best = cand
    return best


def _mm_tn(name, a, b, tt=1024):
    t, k = a.shape
    n = b.shape[1]
    tt = min(tt, t)
    tk = _pick_tile(k, 1408)
    tn = _pick_tile(n, (6 * 2 ** 20) // (4 * tk) // 128 * 128)
    steps = t // tt

    def body(a_ref, b_ref, o_ref, acc_ref):
        @pl.when(pl.program_id(2) == 0)
        def _():
            acc_ref[...] = jnp.zeros_like(acc_ref)

        acc_ref[...] += _dot_tn(a_ref[...], b_ref[...])

        @pl.when(pl.program_id(2) == steps - 1)
        def _():
            o_ref[...] = acc_ref[...].astype(BF16)

    return pl.pallas_call(
        body, name=name, grid=(k // tk, n // tn, steps),
        in_specs=[pl.BlockSpec((tt, tk), lambda i, j, s: (s, i)), pl.BlockSpec((tt, tn), lambda i, j, s: (s, j))],
        out_specs=pl.BlockSpec((tk, tn), lambda i, j, s: (i, j)), out_shape=_sds((k, n), BF16),
        scratch_shapes=[pltpu.VMEM((tk, tn), F32)],
        compiler_params=_params(("parallel", "parallel", "arbitrary")))(a, b)


def _alibi_table():
    arr = np.zeros((N_GROUPS, 8, 128), np.float32)
    for g in range(N_GROUPS):
        for hh in range(HEAD_GROUP):
            arr[g, hh, :] = 2.0 ** (-8.0 * (g * HEAD_GROUP + hh + 1) / N_HEADS)
    return jnp.asarray(arr)


def _spread_stats(cols, per_head=STAT_LANES):
    lane = lax.broadcasted_iota(jnp.int32, (BLK, HEAD_GROUP * per_head), 1)
    tile = cols[HEAD_GROUP - 1]
    for hh in range(HEAD_GROUP - 2, -1, -1):
        tile = jnp.where(lane < (hh + 1) * per_head, cols[hh], tile)
    return tile


def _block_mask(has_prev, dil):
    if has_prev is None:
        steps = lax.broadcasted_iota(jnp.int32, (BLK, BLK), 0) - lax.broadcasted_iota(jnp.int32, (BLK, BLK), 1)
        return steps >= 0, (steps * dil).astype(F32)
    qi = lax.broadcasted_iota(jnp.int32, (BLK, 2 * BLK), 0)
    ki = lax.broadcasted_iota(jnp.int32, (BLK, 2 * BLK), 1)
    steps = qi + BLK - ki
    valid = (steps >= 0) & (steps <= BLK) & ((ki >= BLK) | has_prev)
    return valid, (steps * dil).astype(F32)


def _bias_scratch():
    return pltpu.VMEM((2, HEAD_GROUP, BLK, 2 * BLK), F32)


def _fill_bias(bias, sl_ref, dil):
    for p in range(2):
        valid, dist = _block_mask(p == 1, dil)
        for hh in range(HEAD_GROUP):
            bias[p, hh] = jnp.where(valid, -sl_ref[hh:hh + 1, 0:1] * dist, NEG)


def _rows_of(j):
    return pl.ds(pl.multiple_of(j * BLK, BLK), BLK)


def _lane_half(hf):
    return slice(hf * 128, (hf + 1) * 128)


def _split_pair(x):
    first = lax.broadcasted_iota(jnp.int32, (1, 2 * HEAD_DIM), 1) < HEAD_DIM
    zero = jnp.zeros_like(x)
    return jnp.where(first, x, zero), jnp.where(first, zero, x)


def _deinterleave(src, dst, seq, dil, dtype):
    length = seq // dil
    for r in range(dil):
        for c in range(length // BLK):
            rows = pl.ds(r + c * BLK * dil, BLK, stride=dil)
            out = slice(r * length + c * BLK, r * length + (c + 1) * BLK)
            if len(src.shape) == 2:
                dst[out, :] = src[rows, :].astype(dtype)
            else:
                for hf in range(2):
                    dst[out, _lane_half(hf)] = src.at[hf][rows, :].astype(dtype)


def _interleave(src, dst, seq, dil, accumulate):
    length = seq // dil
    for r in range(dil):
        for c in range(length // BLK):
            rows = pl.ds(r + c * BLK * dil, BLK, stride=dil)
            inp = slice(r * length + c * BLK, r * length + (c + 1) * BLK)
            if len(dst.shape) == 2:
                dst[rows, :] = dst[rows, :] + src[inp, :] if accumulate else src[inp, :]
            else:
                for hf in range(2):
                    val = src[inp, _lane_half(hf)]
                    half = dst.at[hf]
                    half[rows, :] = half[rows, :] + val if accumulate else val


def _split_halves(src, dst, seq):
    def step(i, carry):
        for hf in range(2):
            dst[hf, _rows_of(i), :] = src[_rows_of(i), _lane_half(hf)].astype(F32)
        return carry

    lax.fori_loop(0, seq // BLK, step, 0)


def _band_attn_fwd_fused(h, slopes, bsz, seq, gather=None):
    width = h.shape[1]
    cb = width // 256
    k_off, v_off = MIX_W // 256, 2 * MIX_W // 256
    nb = seq // BLK

    ng = 0 if gather is None else len(gather)

    def body(*refs):
        sl_ref, q_ref, k_ref, v_ref = refs[:4]
        mix_ref, lse_ref = refs[4 + ng:6 + ng]
        qf, kf, vf, qd, kd, vd, od, ld, o1, o2, o3, l1, l2, l3, bias = refs[6 + 2 * ng:21 + 2 * ng]
        if ng:
            start, finish = _gather_protocol(refs[4:4 + ng], refs[6 + ng:6 + 2 * ng], *refs[21 + 2 * ng:])
            pl.when((pl.program_id(0) == 0) & (pl.program_id(1) == 0))(start)

        def run(dil, qs, ks, vs, o_dst, l_dst):
            nblk = seq // dil // BLK
            _fill_bias(bias, sl_ref, dil)

            def block(j, carry):
                rows, prows = _rows_of(j), _rows_of(jnp.maximum(j - 1, 0))
                has_prev = ((j % nblk) != 0).astype(jnp.int32)

                def keys(ref, lanes):
                    return jnp.concatenate([ref[prows, lanes], ref[rows, lanes]], axis=0)

                lses = []
                for pr in range(HEAD_GROUP // 2):
                    lanes = _lane_half(pr)
                    q_ab = _split_pair(qs[rows, lanes] * SCALE)
                    k2 = keys(ks, lanes)
                    v_ab = _split_pair(keys(vs, lanes))
                    out = None
                    for ab in range(2):
                        hh = 2 * pr + ab
                        s = _dot_nt(q_ab[ab], k2) + bias[has_prev, hh]
                        m = jnp.max(s, axis=-1, keepdims=True)
                        p = jnp.exp(s - m)
                        l = jnp.sum(p, axis=-1, keepdims=True)
                        term = _dot_nn(p.astype(BF16), v_ab[ab]) / l
                        out = term if out is None else out + term
                        lses.append(m + jnp.log(l))
                    o_dst[rows, lanes] = out
                l_dst[rows, :] = _spread_stats(lses, HEAD_DIM)
                return carry

            lax.fori_loop(0, nb, block, 0, unroll=8)

        run(1, q_ref, k_ref, v_ref, o1, l1)
        _split_halves(q_ref, qf, seq)
        _split_halves(k_ref, kf, seq)
        _split_halves(v_ref, vf, seq)
        for dil, o_tok, l_tok in ((4, o2, l2), (16, o3, l3)):
            _deinterleave(qf, qd, seq, dil, BF16)
            _deinterleave(kf, kd, seq, dil, BF16)
            _deinterleave(vf, vd, seq, dil, BF16)
            run(dil, qd, kd, vd, od, ld)
            _interleave(od, o_tok, seq, dil, False)
            _interleave(ld, l_tok, seq, dil, False)

        def merge(i, carry):
            rows = _rows_of(i)

            def both(ref):
                return jnp.concatenate([ref[0, rows, :], ref[1, rows, :]], axis=1)

            ls = [l1[rows, :], both(l2), both(l3)]
            m = jnp.maximum(jnp.maximum(ls[0], ls[1]), ls[2])
            tot = m + jnp.log(jnp.exp(ls[0] - m) + jnp.exp(ls[1] - m) + jnp.exp(ls[2] - m))
            ws = [jnp.exp(x - tot) for x in ls]
            mix_ref[rows, :] = (ws[0] * o1[rows, :] + ws[1] * both(o2) + ws[2] * both(o3)).astype(BF16)
            lse_ref[rows, :] = _spread_stats([tot[:, hh * HEAD_DIM:hh * HEAD_DIM + 1] for hh in range(HEAD_GROUP)])
            return carry

        lax.fori_loop(0, nb, merge, 0)
        if ng:
            pl.when((pl.program_id(0) == bsz - 1) & (pl.program_id(1) == N_GROUPS - 1))(finish)

    def hspec(off):
        return pl.BlockSpec((seq, 256), lambda b, g: (b, off + g))

    big = lambda dt: pltpu.VMEM((seq, 256), dt)
    halves = lambda: pltpu.VMEM((2, seq, 128), F32)
    stat = lambda: pltpu.VMEM((seq, 128), F32)
    outs = pl.pallas_call(
        body, name="band_attn_fwd_gather" if ng else "band_attn_fwd", grid=(bsz, N_GROUPS),
        in_specs=[pl.BlockSpec((None, 8, 128), lambda b, g: (g, 0, 0)), hspec(0), hspec(k_off), hspec(v_off)] + [_ANY] * ng,
        out_specs=[pl.BlockSpec((seq, 256), lambda b, g: (b, g)), pl.BlockSpec((seq, 128), lambda b, g: (b, g))] + [_ANY] * ng,
        out_shape=[_sds((bsz * seq, MIX_W), BF16), _sds((bsz * seq, STAT_W), F32)] + (_gather_shapes(gather) if ng else []),
        scratch_shapes=[halves(), halves(), halves(), big(BF16), big(BF16), big(BF16), big(F32), big(F32),
                        big(F32), halves(), halves(), big(F32), halves(), halves(), _bias_scratch()]
        + (_gather_sems(ng) if ng else []),
        compiler_params=_params(("arbitrary", "arbitrary")))(slopes, h, h, h, *(gather or []))
    return outs[0], outs[1], list(outs[2:])


def _band_attn_bwd_fused(h, dcat, mix, lse, slopes, bsz, seq, exchange=None):
    width = h.shape[1]
    k_off, v_off = MIX_W // 256, 2 * MIX_W // 256
    nb = seq // BLK

    ne = 0 if exchange is None else len(exchange)

    def body(*refs):
        sl_ref, q_ref, k_ref, v_ref, do_ref, o_ref, lse_ref = refs[:7]
        dq_ref, dk_ref, dv_ref = refs[7 + ne:10 + ne]
        qf, kf, vf, dof, ddt, qd, kd, vd, dod, lsd, ddd, gq, gk, gv, aq, ak, av, bias = refs[10 + 2 * ne:28 + 2 * ne]
        if ne:
            start, finish = _exchange_protocol(refs[7:7 + ne], refs[10 + ne:10 + 2 * ne], *refs[28 + 2 * ne:])
            pl.when((pl.program_id(0) == 0) & (pl.program_id(1) == 0))(start)

        same_head = (lax.broadcasted_iota(jnp.int32, (HEAD_GROUP * HEAD_DIM, HEAD_GROUP * STAT_LANES), 0) // HEAD_DIM
                     == lax.broadcasted_iota(jnp.int32, (HEAD_GROUP * HEAD_DIM, HEAD_GROUP * STAT_LANES), 1) // STAT_LANES)
        ones_map = jnp.where(same_head, 1.0, 0.0).astype(BF16)

        def delta(i, carry):
            rows = _rows_of(i)
            prod = do_ref[rows, :].astype(F32) * o_ref[rows, :].astype(F32)
            high = prod.astype(BF16)
            rest = (prod - high.astype(F32)).astype(BF16)
            ddt[rows, :] = _dot_nn(high, ones_map) + _dot_nn(rest, ones_map)
            return carry

        lax.fori_loop(0, nb, delta, 0)

        def zero(i, carry):
            rows = _rows_of(i)
            for ref in (gk, gv):
                ref[rows, :] = jnp.zeros((BLK, 256), F32)
            return carry

        def run(dil, qs, ks, vs, dos, lss, dds):
            nblk = seq // dil // BLK
            _fill_bias(bias, sl_ref, dil)
            if nblk > 1:
                lax.fori_loop(0, nb, zero, 0)

            def block(j, carry):
                rows, prows = _rows_of(j), _rows_of(jnp.maximum(j - 1, 0))
                has_prev = ((j % nblk) != 0).astype(jnp.int32)

                def keys(ref, lanes):
                    if nblk == 1:
                        return ref[rows, lanes]
                    return jnp.concatenate([ref[prows, lanes], ref[rows, lanes]], axis=0)

                for pr in range(HEAD_GROUP // 2):
                    lanes = _lane_half(pr)
                    q_ab = _split_pair(qs[rows, lanes] * SCALE)
                    do_ab = _split_pair(dos[rows, lanes])
                    k2, v2 = keys(ks, lanes), keys(vs, lanes)
                    k_ab = _split_pair(k2)
                    dq, dk2, dv2 = None, None, None
                    for ab in range(2):
                        hh = 2 * pr + ab
                        st = slice(hh * STAT_LANES, hh * STAT_LANES + 1)
                        s = _dot_nt(q_ab[ab], k2) + (bias[0, hh, :, BLK:] if nblk == 1 else bias[has_prev, hh])
                        p = jnp.exp(s - lss[rows, st])
                        dp = _dot_nt(do_ab[ab], v2)
                        ds = (p * (dp - dds[rows, st])).astype(BF16)
                        terms = (_dot_nn(ds, k_ab[ab]), _dot_tn(ds, q_ab[ab]), _dot_tn(p.astype(BF16), do_ab[ab]))
                        dq, dk2, dv2 = terms if dq is None else (dq + terms[0], dk2 + terms[1], dv2 + terms[2])
                    gq[rows, lanes] = dq * SCALE
                    if nblk == 1:
                        gk[rows, lanes] = dk2
                        gv[rows, lanes] = dv2
                    else:
                        gk[prows, lanes] += dk2[:BLK]
                        gv[prows, lanes] += dv2[:BLK]
                        gk[rows, lanes] += dk2[BLK:]
                        gv[rows, lanes] += dv2[BLK:]
                return carry

            lax.fori_loop(0, nb, block, 0, unroll=4)

        run(1, q_ref, k_ref, v_ref, do_ref, lse_ref, ddt)

        for src, dst in ((gq, aq), (gk, ak), (gv, av), (q_ref, qf), (k_ref, kf), (v_ref, vf), (do_ref, dof)):
            _split_halves(src, dst, seq)
        for dil in (4, 16):
            for src, dst in ((qf, qd), (kf, kd), (vf, vd), (dof, dod)):
                _deinterleave(src, dst, seq, dil, BF16)
            _deinterleave(lse_ref, lsd, seq, dil, F32)
            _deinterleave(ddt, ddd, seq, dil, F32)
            run(dil, qd, kd, vd, dod, lsd, ddd)
            for src, dst in ((gq, aq), (gk, ak), (gv, av)):
                _interleave(src, dst, seq, dil, True)

        def write(i, carry):
            rows = _rows_of(i)
            for src, dst in ((aq, dq_ref), (ak, dk_ref), (av, dv_ref)):
                for hf in range(2):
                    dst[rows, _lane_half(hf)] = src[hf, rows, :].astype(BF16)
            return carry

        lax.fori_loop(0, nb, write, 0)
        if ne:
            pl.when((pl.program_id(0) == bsz - 1) & (pl.program_id(1) == N_GROUPS - 1))(finish)

    def hspec(off):
        return pl.BlockSpec((seq, 256), lambda b, g: (b, off + g))

    io = pl.BlockSpec((seq, 256), lambda b, g: (b, g))
    big = lambda dt: pltpu.VMEM((seq, 256), dt)
    halves = lambda: pltpu.VMEM((2, seq, 128), F32)
    stat = lambda: pltpu.VMEM((seq, 128), F32)
    outs = pl.pallas_call(
        body, name="band_attn_bwd_exchange" if ne else "band_attn_bwd", grid=(bsz, N_GROUPS),
        in_specs=[pl.BlockSpec((None, 8, 128), lambda b, g: (g, 0, 0)), hspec(0), hspec(k_off), hspec(v_off), io, io,
                  pl.BlockSpec((seq, 128), lambda b, g: (b, g))] + [_ANY] * ne,
        out_specs=[io, io, io] + [_ANY] * ne,
        out_shape=[_sds((bsz * seq, MIX_W), BF16)] * 3 + [_sds(s.shape, s.dtype) for s in (exchange or [])],
        scratch_shapes=[halves(), halves(), halves(), halves(), stat(),
                        big(BF16), big(BF16), big(BF16), big(BF16), stat(), stat(),
                        big(F32), big(F32), big(F32), halves(), halves(), halves(), _bias_scratch()]
        + (_exchange_sems(ne) if ne else []),
        compiler_params=_params(("arbitrary", "arbitrary")))(slopes, h, h, h, dcat, mix, lse, *(exchange or []))
    return list(outs[:3]), list(outs[3:])


def _mem_attn_fwd(h, mkv, bsz, seq, q_col, tq=1024):
    nq = seq // tq

    def body(q_ref, kv_ref, o_ref):
        for pr in range(2):
            lanes = _lane_half(pr)
            q_ab = _split_pair(q_ref[:, lanes])
            k = kv_ref[:, lanes]
            v_ab = _split_pair(kv_ref[:, MEM_W + pr * 128:MEM_W + (pr + 1) * 128])
            out = None
            for ab in range(2):
                s = _dot_nt(q_ab[ab], k) * SCALE
                m = jnp.max(s, axis=-1, keepdims=True)
                p = jnp.exp(s - m)
                l = jnp.sum(p, axis=-1, keepdims=True)
                term = _dot_nn(p.astype(BF16), v_ab[ab]) / l
                out = term if out is None else out + term
            o_ref[:, lanes] = out.astype(BF16)

    return pl.pallas_call(
        body, name="mem_attn_fwd", grid=(bsz, nq),
        in_specs=[pl.BlockSpec((tq, MEM_W), lambda b, i: (b * nq + i, q_col)),
                  pl.BlockSpec((N_MEM, 2 * MEM_W), lambda b, i: (b, 0))],
        out_specs=pl.BlockSpec((tq, MEM_W), lambda b, i: (b * nq + i, 0)),
        out_shape=_sds((bsz * seq, MEM_W), BF16), compiler_params=_params(("parallel", "parallel")))(h, mkv)


def _mem_attn_bwd(h, mkv, dcat, bsz, seq, q_col, tq=1024):
    nq = seq // tq
    do_col = MIX_W // MEM_W

    def body(q_ref, kv_ref, do_ref, dq_ref, dkv_ref):
        @pl.when(pl.program_id(1) == 0)
        def _():
            dkv_ref[...] = jnp.zeros_like(dkv_ref)

        for pr in range(2):
            lanes = _lane_half(pr)
            vlanes = slice(MEM_W + pr * 128, MEM_W + (pr + 1) * 128)
            q_ab = _split_pair(q_ref[:, lanes])
            do_ab = _split_pair(do_ref[:, lanes])
            k, v = kv_ref[:, lanes], kv_ref[:, vlanes]
            k_ab = _split_pair(k)
            dq, dk, dv = None, None, None
            for ab in range(2):
                s = _dot_nt(q_ab[ab], k) * SCALE
                m = jnp.max(s, axis=-1, keepdims=True)
                e = jnp.exp(s - m)
                p = e / jnp.sum(e, axis=-1, keepdims=True)
                dp = _dot_nt(do_ab[ab], v)
                dd = jnp.sum(p * dp, axis=-1, keepdims=True)
                ds = (p * (dp - dd) * SCALE).astype(BF16)
                terms = (_dot_nn(ds, k_ab[ab]), _dot_tn(ds, q_ab[ab]), _dot_tn(p.astype(BF16), do_ab[ab]))
                dq, dk, dv = terms if dq is None else (dq + terms[0], dk + terms[1], dv + terms[2])
            dq_ref[:, lanes] = dq.astype(BF16)
            dkv_ref[:, lanes] += dk
            dkv_ref[:, vlanes] += dv

    return pl.pallas_call(
        body, name="mem_attn_bwd", grid=(bsz, nq),
        in_specs=[pl.BlockSpec((tq, MEM_W), lambda b, i: (b * nq + i, q_col)),
                  pl.BlockSpec((N_MEM, 2 * MEM_W), lambda b, i: (b, 0)),
                  pl.BlockSpec((tq, MEM_W), lambda b, i: (b * nq + i, do_col))],
        out_specs=[pl.BlockSpec((tq, MEM_W), lambda b, i: (b * nq + i, 0)),
                   pl.BlockSpec((N_MEM, 2 * MEM_W), lambda b, i: (b, 0))],
        out_shape=[_sds((bsz * seq, MEM_W), BF16), _sds((bsz * N_MEM, 2 * MEM_W), F32)],
        compiler_params=_params(("parallel", "arbitrary")))(h, mkv, dcat)


_GELU_C = math.sqrt(2.0 / math.pi)
_GELU_A = 0.044715


def _gelu(x):
    return 0.5 * x * (1.0 + jnp.tanh(_GELU_C * (x + _GELU_A * x * x * x)))


def _gelu_grad(x):
    th = jnp.tanh(_GELU_C * (x + _GELU_A * x * x * x))
    return 0.5 * (1.0 + th) + 0.5 * x * (1.0 - th * th) * (_GELU_C * (1.0 + 3.0 * _GELU_A * x * x))


def _tril_mask(lower):
    ri = lax.broadcasted_iota(jnp.int32, (BLK, BLK), 0)
    ci = lax.broadcasted_iota(jnp.int32, (BLK, BLK), 1)
    return (ri >= ci) if lower else (ci >= ri)


def _sgu_fwd(h, ws, bs_t, ln_g, ln_b, tm=512):
    t = h.shape[0]

    def body(u_ref, v_ref, ws_ref, bs_ref, g_ref, b_ref, o_ref):
        ug = _gelu(u_ref[...].astype(F32))
        vhat, _ = _ln_hat(_gelu(v_ref[...].astype(F32)))
        vn = (vhat * g_ref[...] + b_ref[...]).astype(BF16)
        mask = _tril_mask(True)
        first = lax.broadcasted_iota(jnp.int32, (1, 2 * HEAD_DIM), 1) < HEAD_DIM
        for pr in range(N_HEADS // 2):
            lanes = _lane_half(pr)
            w_ab = [jnp.where(mask, ws_ref[2 * pr + ab], 0).astype(BF16) for ab in range(2)]
            bias = jnp.where(first, bs_ref[:, 2 * pr:2 * pr + 1], bs_ref[:, 2 * pr + 1:2 * pr + 2])
            for c in range(tm // BLK):
                rs = slice(c * BLK, (c + 1) * BLK)
                v_ab = _split_pair(vn[rs, lanes])
                mixed = _dot_nn(w_ab[0], v_ab[0]) + _dot_nn(w_ab[1], v_ab[1]) + bias
                o_ref[rs, lanes] = (ug[rs, lanes] * mixed).astype(BF16)

    return pl.pallas_call(
        body, name="sgu_fwd", grid=(t // tm,),
        in_specs=[_rows(tm, MIX_W, 0), _rows(tm, MIX_W, 1), _whole(ws.shape), _whole(bs_t.shape), _whole(ln_g.shape), _whole(ln_b.shape)],
        out_specs=_rows(tm, MIX_W), out_shape=_sds((t, MIX_W), BF16),
        compiler_params=_params(("parallel",)))(h, h, ws, bs_t, ln_g, ln_b)


def _sgu_bwd(h, dcat, ws, ws_t, bs_t, ln_g, ln_b, tm=512):
    t = h.shape[0]

    def body(u_ref, v_ref, do_ref, ws_ref, wst_ref, bs_ref, g_ref, b_ref, dh_ref, dws_ref, dbs_ref, dg_ref, db_ref, dvn_ref):
        @pl.when(pl.program_id(0) == 0)
        def _():
            dws_ref[...] = jnp.zeros_like(dws_ref)
            dbs_ref[...] = jnp.zeros_like(dbs_ref)
            dg_ref[...] = jnp.zeros_like(dg_ref)
            db_ref[...] = jnp.zeros_like(db_ref)

        u = u_ref[...].astype(F32)
        v = v_ref[...].astype(F32)
        do = do_ref[...].astype(F32)
        ug = _gelu(u)
        vhat, rstd = _ln_hat(_gelu(v))
        vn = (vhat * g_ref[...] + b_ref[...]).astype(BF16)
        dmixed_f = do * ug
        dmixed = dmixed_f.astype(BF16)
        low, upp = _tril_mask(True), _tril_mask(False)
        first = lax.broadcasted_iota(jnp.int32, (1, 2 * HEAD_DIM), 1) < HEAD_DIM
        for pr in range(N_HEADS // 2):
            lanes = _lane_half(pr)
            w_ab = [jnp.where(low, ws_ref[2 * pr + ab], 0).astype(BF16) for ab in range(2)]
            wt_ab = [jnp.where(upp, wst_ref[2 * pr + ab], 0).astype(BF16) for ab in range(2)]
            bias = jnp.where(first, bs_ref[:, 2 * pr:2 * pr + 1], bs_ref[:, 2 * pr + 1:2 * pr + 2])
            dws_acc = [None, None]
            dbs_acc = [None, None]
            for c in range(tm // BLK):
                rs = slice(c * BLK, (c + 1) * BLK)
                vn_pair = vn[rs, lanes]
                v_ab = _split_pair(vn_pair)
                mixed = _dot_nn(w_ab[0], v_ab[0]) + _dot_nn(w_ab[1], v_ab[1]) + bias
                dh_ref[rs, lanes] = (do[rs, lanes] * mixed * _gelu_grad(u[rs, lanes])).astype(BF16)
                dm_ab = _split_pair(dmixed[rs, lanes])
                dmf_ab = _split_pair(dmixed_f[rs, lanes])
                for ab in range(2):
                    term = _dot_nt(dm_ab[ab], vn_pair)
                    dws_acc[ab] = term if dws_acc[ab] is None else dws_acc[ab] + term
                    rsum = jnp.sum(dmf_ab[ab], axis=-1, keepdims=True)
                    dbs_acc[ab] = rsum if dbs_acc[ab] is None else dbs_acc[ab] + rsum
                dvn_ref[rs, lanes] = _dot_nn(wt_ab[0], dm_ab[0]) + _dot_nn(wt_ab[1], dm_ab[1])
            for ab in range(2):
                g = 2 * pr + ab
                dws_ref[g] += jnp.where(low, dws_acc[ab], 0.0)
                dbs_ref[:, g:g + 1] += dbs_acc[ab]
        dvn = dvn_ref[...]
        dg_ref[...] += jnp.sum(dvn * vhat, axis=0, keepdims=True)
        db_ref[...] += jnp.sum(dvn, axis=0, keepdims=True)
        dxh = dvn * g_ref[...]
        m1 = jnp.mean(dxh, axis=-1, keepdims=True)
        m2 = jnp.mean(dxh * vhat, axis=-1, keepdims=True)
        dvg = rstd * (dxh - m1 - vhat * m2)
        dh_ref[:, MIX_W:] = (dvg * _gelu_grad(v)).astype(BF16)

    return pl.pallas_call(
        body, name="sgu_bwd", grid=(t // tm,),
        in_specs=[_rows(tm, MIX_W, 0), _rows(tm, MIX_W, 1), _rows(tm, MIX_W, 0), _whole(ws.shape), _whole(ws_t.shape),
                  _whole(bs_t.shape), _whole(ln_g.shape), _whole(ln_b.shape)],
        out_specs=[_rows(tm, 2 * MIX_W), _whole(ws.shape), _whole(bs_t.shape), _whole((1, MIX_W)), _whole((1, MIX_W))],
        out_shape=[_sds((t, 2 * MIX_W), BF16), _sds(ws.shape, F32), _sds(bs_t.shape, F32), _sds((1, MIX_W), F32), _sds((1, MIX_W), F32)],
        scratch_shapes=[pltpu.VMEM((tm, MIX_W), F32)],
        compiler_params=_params(("arbitrary",)))(h, h, dcat, ws, ws_t, bs_t, ln_g, ln_b)


def _row_tile(rows, cols, itemsize=4, limit=2 ** 20):
    best = rows
    for cand in (4096, 2048, 1024, 512, 256, 128, 64, 32, 16):
        if rows % cand == 0 and rows > cand:
            best = cand
            if cand * cols * itemsize <= limit:
                break
    return best


def _adamw(w, m, v, grad=None, parts=None, first_parts=None):
    rows, cols = w.shape
    rows0 = 0 if first_parts is None else first_parts.shape[1]
    tr = _row_tile(rows0 if rows0 else rows, cols)
    n0 = rows0 // tr

    def chip_sum(ref):
        acc = ref[0].astype(F32)
        for k in range(1, 4):
            acc = acc + ref[k].astype(F32)
        return acc

    def body(*refs):
        w_ref, m_ref, v_ref = refs[:3]
        go_ref, d_ref, nm_ref, nv_ref = refs[-4:]
        if parts is None:
            gv = refs[3][...]
        elif first_parts is None:
            gv = chip_sum(refs[3])
        else:
            gv = jnp.where(pl.program_id(0) < n0, chip_sum(refs[3]), chip_sum(refs[4]))
        nm = ADAM_B1 * m_ref[...] + (1.0 - ADAM_B1) * gv
        nv = ADAM_B2 * v_ref[...] + (1.0 - ADAM_B2) * (gv * gv)
        m_hat = nm / (1.0 - ADAM_B1 ** ADAM_STEP)
        v_hat = nv / (1.0 - ADAM_B2 ** ADAM_STEP)
        go_ref[...] = gv
        d_ref[...] = -ADAM_LR * (m_hat / (jnp.sqrt(v_hat) + ADAM_EPS) + ADAM_WD * w_ref[...])
        nm_ref[...] = nm
        nv_ref[...] = nv

    spec = _rows(tr, cols)
    if parts is None:
        g_specs, g_args = [spec], [grad]
    elif first_parts is None:
        g_specs, g_args = [pl.BlockSpec((4, tr, cols), lambda i: (0, i, 0))], [parts]
    else:
        g_specs = [pl.BlockSpec((4, tr, cols), lambda i: (0, jnp.minimum(i, n0 - 1), 0)),
                   pl.BlockSpec((4, tr, cols), lambda i: (0, jnp.maximum(i - n0, 0), 0))]
        g_args = [first_parts, parts]
    return pl.pallas_call(
        body, name="adamw" if parts is None else "adamw_sum_chips", grid=(rows // tr,), in_specs=[spec] * 3 + g_specs,
        out_specs=[spec] * 4, out_shape=[_sds(w.shape, F32)] * 4,
        compiler_params=_params(("parallel",)))(w, m, v, *g_args)


_ANY = pl.BlockSpec(memory_space=pl.ANY)
_MESH = pl.DeviceIdType.MESH


def _all_gather(name, blocks):
    nt = len(blocks)

    def body(*refs):
        start, finish = _gather_protocol(refs[:nt], refs[nt:2 * nt], *refs[2 * nt:])
        start()
        finish()

    return pl.pallas_call(
        body, name=name, out_shape=_gather_shapes(blocks), in_specs=[_ANY] * nt, out_specs=[_ANY] * nt,
        scratch_shapes=_gather_sems(nt))(*blocks)


def _gather_shapes(blocks):
    return [_sds((N_DEV,) + b.shape, b.dtype) for b in blocks]


def _gather_sems(nt):
    return [pltpu.SemaphoreType.DMA((nt, 7)), pltpu.SemaphoreType.DMA((nt, 7)), pltpu.SemaphoreType.DMA((nt,))]


def _gather_protocol(x_refs, out_refs, send_sems, recv_sems, local_sems):
    nt = len(x_refs)
    x, y, c = lax.axis_index("x"), lax.axis_index("y"), lax.axis_index("c")
    me, sibling = (x, y, c), (x, y, 1 - c)
    chips = [(1 - x, y), (x, 1 - y), (1 - x, 1 - y)]

    def slot(t, px, py, pc):
        return out_refs[t].at[4 * px + 2 * py + pc]

    def copy(t, k, blk, to, src=None):
        return pltpu.make_async_remote_copy(
            src_ref=slot(t, *blk) if src is None else src, dst_ref=slot(t, *blk),
            send_sem=send_sems.at[t, k], recv_sem=recv_sems.at[t, k], device_id=to, device_id_type=_MESH)

    def own_copies():
        mine = [pltpu.make_async_copy(x_refs[t], slot(t, *me), local_sems.at[t]) for t in range(nt)]
        first = []
        for t in range(nt):
            first.append(copy(t, 0, me, sibling, src=x_refs[t]))
            first += [copy(t, 1 + j, me, (*chip, c), src=x_refs[t]) for j, chip in enumerate(chips)]
        return mine, first

    def start():
        mine, first = own_copies()
        for cp in mine + first:
            cp.start()

    def finish():
        mine, first = own_copies()
        passed = []
        for j, chip in enumerate(chips):
            for t in range(nt):
                copy(t, 1 + j, (*chip, c), me).wait_recv()
                fwd = copy(t, 4 + j, (*chip, c), sibling)
                fwd.start()
                passed.append(fwd)
        for t in range(nt):
            copy(t, 0, sibling, me).wait_recv()
        for j, chip in enumerate(chips):
            for t in range(nt):
                copy(t, 4 + j, (*chip, 1 - c), me).wait_recv()
        for cp in first + passed:
            cp.wait_send()
        for cp in mine:
            cp.wait()

    return start, finish


def _swap_with_sibling(packed):
    nt = len(packed)

    def body(*refs):
        p_refs, got_refs = refs[:nt], refs[nt:2 * nt]
        send_sems, recv_sems = refs[2 * nt:]
        x, y, c = lax.axis_index("x"), lax.axis_index("y"), lax.axis_index("c")
        copies = [
            pltpu.make_async_remote_copy(
                src_ref=p_refs[t].at[1 - c], dst_ref=got_refs[t], send_sem=send_sems.at[t], recv_sem=recv_sems.at[t],
                device_id=(x, y, 1 - c), device_id_type=_MESH)
            for t in range(nt)]
        for cp in copies:
            cp.start()
        for cp in copies:
            cp.wait_recv()
        for cp in copies:
            cp.wait_send()

    return pl.pallas_call(
        body, name="grad_swap_sibling", out_shape=[_sds(p.shape[1:], p.dtype) for p in packed], in_specs=[_ANY] * nt,
        out_specs=[_ANY] * nt,
        scratch_shapes=[pltpu.SemaphoreType.DMA((nt,)), pltpu.SemaphoreType.DMA((nt,))])(*packed)


def _chip_sum(packed, got):
    _, nchip, rows, cols = packed.shape
    tr = _row_tile(rows, cols, 2)
    core = lax.axis_index("c").astype(jnp.int32).reshape(1)

    def body(c_ref, p_ref, g_ref, o_ref):
        o_ref[...] = (p_ref[...].astype(F32) + g_ref[...].astype(F32)).astype(o_ref.dtype)

    grid_spec = pltpu.PrefetchScalarGridSpec(
        num_scalar_prefetch=1, grid=(nchip, rows // tr),
        in_specs=[pl.BlockSpec((None, None, tr, cols), lambda k, i, c: (c[0], k, i, 0)),
                  pl.BlockSpec((None, tr, cols), lambda k, i, c: (k, i, 0))],
        out_specs=pl.BlockSpec((None, tr, cols), lambda k, i, c: (k, i, 0)))
    return pl.pallas_call(
        body, name="grad_chip_sum", grid_spec=grid_spec, out_shape=_sds(got.shape, got.dtype),
        compiler_params=_params(("parallel", "parallel")))(core, packed, got)


def _exchange_chips(chip_sums):
    nt = len(chip_sums)

    def body(*refs):
        start, finish = _exchange_protocol(refs[:nt], refs[nt:2 * nt], *refs[2 * nt:])
        start()
        finish()

    return pl.pallas_call(
        body, name="grad_exchange_chips", out_shape=[_sds(s.shape, s.dtype) for s in chip_sums], in_specs=[_ANY] * nt,
        out_specs=[_ANY] * nt, scratch_shapes=_exchange_sems(nt))(*chip_sums)


def _exchange_sems(nt):
    return [pltpu.SemaphoreType.DMA((nt, 3)), pltpu.SemaphoreType.DMA((nt, 3)), pltpu.SemaphoreType.DMA((nt,))]


def _exchange_protocol(s_refs, got_refs, send_sems, recv_sems, local_sems):
    nt = len(s_refs)
    x, y, c = lax.axis_index("x"), lax.axis_index("y"), lax.axis_index("c")
    my_chip = 2 * x + y
    chips = [(1 - x, y), (x, 1 - y), (1 - x, 1 - y)]

    def copy(t, j, src_chip, dst_chip):
        px, py = chips[j]
        return pltpu.make_async_remote_copy(
            src_ref=s_refs[t].at[src_chip], dst_ref=got_refs[t].at[dst_chip], send_sem=send_sems.at[t, j],
            recv_sem=recv_sems.at[t, j], device_id=(px, py, c), device_id_type=_MESH)

    def own_copies():
        mine = [pltpu.make_async_copy(s_refs[t].at[my_chip], got_refs[t].at[my_chip], local_sems.at[t]) for t in range(nt)]
        sends = [copy(t, j, 2 * px + py, my_chip) for t in range(nt) for j, (px, py) in enumerate(chips)]
        return mine, sends

    def start():
        mine, sends = own_copies()
        for cp in mine + sends:
            cp.start()

    def finish():
        mine, sends = own_copies()
        for j, (px, py) in enumerate(chips):
            for t in range(nt):
                copy(t, j, my_chip, 2 * px + py).wait_recv()
        for cp in sends:
            cp.wait_send()
        for cp in mine:
            cp.wait()

    return start, finish


def _sum_chips(got):
    _, rows, cols = got.shape
    tr = _row_tile(rows, cols)

    def body(g_ref, o_ref):
        acc = g_ref[0].astype(F32)
        for k in range(1, 4):
            acc = acc + g_ref[k].astype(F32)
        o_ref[...] = acc

    return pl.pallas_call(
        body, name="grad_sum_chips", grid=(rows // tr,), in_specs=[pl.BlockSpec((4, tr, cols), lambda i: (0, i, 0))],
        out_specs=pl.BlockSpec((tr, cols), lambda i: (i, 0)), out_shape=_sds((rows, cols), F32),
        compiler_params=_params(("parallel",)))(got)


_COL_SHARDED = ("a_w_in", "b_w_in", "w_gate", "w_up")
_ROW_SHARDED = ("w_mem_kv", "w_out", "w_down")
_BIG = ("a_w_in", "b_w_in", "w_mem_kv", "w_out", "w_gate", "w_up", "w_down")
_SGU_LN = ("sgu_ln_g", "sgu_ln_b")
_LN4 = ("ln_mix_g", "ln_mix_b", "ln_ffn_g", "ln_ffn_b")
_REPLICATED = ("sgu_w_s", "sgu_b_s") + _LN4


def _unshard(name, gathered):
    if name in _COL_SHARDED or name in _SGU_LN:
        moved = jnp.moveaxis(gathered, 0, -2)
        return moved.reshape(moved.shape[:-2] + (moved.shape[-2] * moved.shape[-1],))
    moved = jnp.moveaxis(gathered, 0, 1)
    return moved.reshape((moved.shape[0], moved.shape[1] * moved.shape[2]) + moved.shape[3:])


_LAID_OUT_IN_KERNEL = _COL_SHARDED


def _after_gather(name, gathered):
    return gathered if name in _LAID_OUT_IN_KERNEL else _unshard(name, gathered)


def _by_shard(name, full):
    if name in _COL_SHARDED or name in _SGU_LN:
        split = full.reshape(full.shape[:-1] + (N_DEV, full.shape[-1] // N_DEV))
        return jnp.moveaxis(split, -2, 0)
    split = full.reshape((full.shape[0], N_DEV, full.shape[1] // N_DEV) + full.shape[2:])
    return jnp.moveaxis(split, 1, 0)


def _layer_keys(i):
    return [("a_w_in" if i % 2 == 0 else "b_w_in", i // 2)] + [(n, i) for n in ("w_mem_kv", "w_out", "w_gate", "w_up", "w_down")]


_GATHER_FIRST = _layer_keys(0)[:2]
_GATHER_LATER = (_layer_keys(0)[2:] + _layer_keys(1), _layer_keys(2)[:4], _layer_keys(2)[4:],
                 _layer_keys(3)[:4], _layer_keys(3)[4:])


def _shard_block(shards, key):
    name, idx = key
    return shards[name][idx:idx + 1].astype(BF16)


def _gather_first(shards):
    blocks = [_shard_block(shards, k) for k in _GATHER_FIRST] + [shards[n] for n in _SGU_LN]
    gathered = _all_gather("first_all_gather", blocks)
    full = {k: _after_gather(k[0], g) for k, g in zip(_GATHER_FIRST, gathered)}
    sgu_ln = {n: _unshard(n, g) for n, g in zip(_SGU_LN, gathered[len(_GATHER_FIRST):])}
    return full, sgu_ln


def _two_level(by_dest):
    shp = by_dest.shape[1:]
    split = by_dest.astype(BF16).reshape((4, 2) + shp).swapaxes(0, 1)
    return split.reshape(2, 4, int(np.prod(shp[:-1])), shp[-1])


_EARLY = _BIG + _SGU_LN


def _chip_sums_of_early(grads):
    packed = [_two_level(_by_shard(n, jnp.stack(grads[n][1:] if n == "a_w_in" else grads[n]))) for n in _EARLY]
    ln4 = jnp.stack([jnp.stack(grads[n]) for n in _LN4])
    rep = [jnp.stack(grads["sgu_w_s"]).reshape(N_DEV, -1, BLK), jnp.stack(grads["sgu_b_s"]).reshape(N_DEV, -1, BLK),
           ln4.reshape(N_DEV, -1, D_MODEL)]
    packed += [_two_level(r) for r in rep]
    got = _swap_with_sibling(packed)
    return [_chip_sum(p, g) for p, g in zip(packed, got)]


def _finish_replicated(parts, shapes):
    w_s, b_s, ln_all = _all_gather("replicated_grads_all_gather", [_sum_chips(p) for p in parts])
    ln_all = ln_all.reshape((len(_LN4),) + tuple(shapes[_LN4[0]]))
    rep_grads = {"sgu_w_s": w_s.reshape(shapes["sgu_w_s"]), "sgu_b_s": b_s.reshape(shapes["sgu_b_s"])}
    rep_grads.update({n: ln_all[i] for i, n in enumerate(_LN4)})
    return rep_grads


def _reduce_last(grad_a_first):
    packed = [_two_level(_by_shard("a_w_in", grad_a_first))]
    got = _swap_with_sibling(packed)
    return _exchange_chips([_chip_sum(packed[0], got[0])])[0]


def _as_2d(a):
    if a.ndim == 1:
        return a.reshape(1, -1)
    return a.reshape(-1, a.shape[-1])


def kernel(x, mem, a_w_in, b_w_in, sgu_ln_g, sgu_ln_b, sgu_w_s, sgu_b_s, w_mem_kv, w_out, ln_mix_g, ln_mix_b, w_gate, w_up, w_down, ln_ffn_g, ln_ffn_b, loss_target, m_a_w_in, m_b_w_in, m_sgu_ln_g, m_sgu_ln_b, m_sgu_w_s, m_sgu_b_s, m_w_mem_kv, m_w_out, m_ln_mix_g, m_ln_mix_b, m_w_gate, m_w_up, m_w_down, m_ln_ffn_g, m_ln_ffn_b, v_a_w_in, v_b_w_in, v_sgu_ln_g, v_sgu_ln_b, v_sgu_w_s, v_sgu_b_s, v_w_mem_kv, v_w_out, v_ln_mix_g, v_ln_mix_b, v_w_gate, v_w_up, v_w_down, v_ln_ffn_g, v_ln_ffn_b):
    names = ("a_w_in", "b_w_in", "sgu_ln_g", "sgu_ln_b", "sgu_w_s", "sgu_b_s", "w_mem_kv", "w_out", "ln_mix_g", "ln_mix_b",
             "w_gate", "w_up", "w_down", "ln_ffn_g", "ln_ffn_b")
    weights = dict(zip(names, (a_w_in, b_w_in, sgu_ln_g, sgu_ln_b, sgu_w_s, sgu_b_s, w_mem_kv, w_out, ln_mix_g, ln_mix_b,
                               w_gate, w_up, w_down, ln_ffn_g, ln_ffn_b)))
    mom_m = dict(zip(names, (m_a_w_in, m_b_w_in, m_sgu_ln_g, m_sgu_ln_b, m_sgu_w_s, m_sgu_b_s, m_w_mem_kv, m_w_out, m_ln_mix_g,
                             m_ln_mix_b, m_w_gate, m_w_up, m_w_down, m_ln_ffn_g, m_ln_ffn_b)))
    mom_v = dict(zip(names, (v_a_w_in, v_b_w_in, v_sgu_ln_g, v_sgu_ln_b, v_sgu_w_s, v_sgu_b_s, v_w_mem_kv, v_w_out, v_ln_mix_g,
                             v_ln_mix_b, v_w_gate, v_w_up, v_w_down, v_ln_ffn_g, v_ln_ffn_b)))
    full, sgu_ln = _gather_first(weights)
    pending = [(keys, [_shard_block(weights, k) for k in keys]) for keys in _GATHER_LATER]
    loss_part, grad_x, local, early = _local_step(
        x, mem, loss_target, full, sgu_ln, {n: weights[n] for n in _REPLICATED}, pending)
    loss = lax.psum(loss_part[0, 0], ("x", "y", "c"))
    early_parts = dict(zip(_EARLY, early))
    rep_grads = _finish_replicated(early[len(_EARLY):], {n: weights[n].shape for n in _REPLICATED})
    a_first_parts = _reduce_last(local["a_w_in"][:1])

    reduced, deltas, new_m, new_v = {}, {}, {}, {}
    for n in names:
        w2, m2, v2 = _as_2d(weights[n]), _as_2d(mom_m[n]), _as_2d(mom_v[n])
        if n in early_parts:
            outs = _adamw(w2, m2, v2, parts=early_parts[n], first_parts=a_first_parts if n == "a_w_in" else None)
        else:
            outs = _adamw(w2, m2, v2, grad=_as_2d(rep_grads[n]))
        reduced[n], deltas[n], new_m[n], new_v[n] = (a.reshape(weights[n].shape) for a in outs)

    return (loss, grad_x, *[reduced[n] for n in names], *[deltas[n] for n in names],
            *[new_m[n] for n in names], *[new_v[n] for n in names])


def _local_step(x, mem, loss_target, full, sgu_ln, small, pending=None):
    sgu_w_s, sgu_b_s = small["sgu_w_s"], small["sgu_b_s"]
    ln_mix_g, ln_mix_b, ln_ffn_g, ln_ffn_b = (small[n] for n in ("ln_mix_g", "ln_mix_b", "ln_ffn_g", "ln_ffn_b"))
    bsz, seq, _ = x.shape
    tokens = bsz * seq
    slopes = _alibi_table()
    full = dict(full)
    exchanging = pending is not None
    pending = list(pending or [])

    def weight(name, idx):
        return full[(name, idx)][0]

    def next_group():
        return pending[0][1] if pending else None

    def landed(gathered):
        if gathered:
            keys, _ = pending.pop(0)
            full.update({k: _after_gather(k[0], g) for k, g in zip(keys, gathered)})

    res = (x.reshape(tokens, D_MODEL),)
    xb = res[0].astype(BF16)
    memb = mem.reshape(bsz * N_MEM, D_MODEL).astype(BF16)
    tgt = loss_target.reshape(tokens, D_MODEL)

    saved = []
    for i in range(DEPTH):
        j = i // 2
        dil_layer = i % 2 == 0
        mkv = _linear_nn("mem_kv", memb, weight("w_mem_kv", i))
        h, w_in = _linear_nn_gathered("in_proj_a" if dil_layer else "in_proj_b", xb,
                                      full[("a_w_in" if dil_layer else "b_w_in", j)][:, 0])
        st = dict(xb=xb, h=h, mkv=mkv, w_in=w_in)
        if dil_layer:
            mix, st["lse"], gathered = _band_attn_fwd_fused(h, slopes, bsz, seq, gather=next_group() if i == 0 else None)
            landed(gathered)
            q_col = 3 * MIX_W // MEM_W
        else:
            st["ws"] = sgu_w_s[j]
            st["bs_t"] = sgu_b_s[j].T
            st["ln_g"] = sgu_ln["sgu_ln_g"][j].reshape(1, MIX_W)
            st["ln_b"] = sgu_ln["sgu_ln_b"][j].reshape(1, MIX_W)
            mix = _sgu_fwd(h, st["ws"], st["bs_t"], st["ln_g"], st["ln_b"])
            q_col = 2 * MIX_W // MEM_W
        mo = _mem_attn_fwd(h, mkv, bsz, seq, q_col)
        w_out, w_down = weight("w_out", i), weight("w_down", i)
        mix_ln = (ln_mix_g[i].reshape(1, D_MODEL), ln_mix_b[i].reshape(1, D_MODEL))
        ffn_ln = (ln_ffn_g[i].reshape(1, D_MODEL), ln_ffn_b[i].reshape(1, D_MODEL))
        r1, x1b, _ = _proj_ln_fwd("out_proj_ln", [mix, mo], w_out, res, *mix_ln)
        gt, up, act, w_gu, gathered = _ffn_up_fwd(x1b, full[("w_gate", i)][:, 0], full[("w_up", i)][:, 0],
                                                  gather=next_group() if i < 2 else None)
        landed(gathered)
        r2, xb, gathered = _proj_ln_fwd("ffn_down_ln", [act], w_down, (r1, *mix_ln), *ffn_ln,
                                        gather=next_group() if i < 2 else None)
        landed(gathered)
        res = (r2, *ffn_ln)
        st.update(mix=mix, mo=mo, q_col=q_col, r1=r1, x1b=x1b, gt=gt, up=up, act=act, r2=r2,
                  w_out=w_out, w_down=w_down, w_gu=w_gu)
        saved.append(st)

    dr2, dr2b, dg, db, loss_part = _loss_ln_bwd(*res, tgt)

    early_parts = None
    per_pair = ("a_w_in", "b_w_in", "sgu_ln_g", "sgu_ln_b", "sgu_w_s", "sgu_b_s")
    grads = {n: [None] * (DEPTH // 2 if n in per_pair else DEPTH) for n in _BIG + _SGU_LN + _REPLICATED}
    for i in reversed(range(DEPTH)):
        j = i // 2
        st = saved[i]
        dil_layer = i % 2 == 0
        w_in = st["w_in"]
        grads["ln_ffn_g"][i], grads["ln_ffn_b"][i] = dg[0], db[0]
        dgu = _ffn_down_bwd(dr2b, st["w_down"], st["gt"], st["up"])
        grads["w_down"][i] = _mm_tn("grad_w_down", st["act"], dr2b)
        dr1, dr1b, dg, db = _linear_nt("ffn_up_bwd", [dgu], st["w_gu"], dr2, F32,
                                       ln=(st["r1"], ln_mix_g[i].reshape(1, D_MODEL)))
        grads["ln_mix_g"][i], grads["ln_mix_b"][i] = dg[0], db[0]
        dw_gu = _mm_tn("grad_w_gate_up", st["x1b"], dgu)
        grads["w_gate"][i], grads["w_up"][i] = dw_gu[:, :D_FF], dw_gu[:, D_FF:]
        dcat = _linear_nt("out_proj_bwd", [dr1b], st["w_out"], None, BF16)
        grads["w_out"][i] = jnp.concatenate(
            [_mm_tn("grad_w_out_mix", st["mix"], dr1b), _mm_tn("grad_w_out_mem", st["mo"], dr1b)], axis=0)
        dqm, dmkv = _mem_attn_bwd(st["h"], st["mkv"], dcat, bsz, seq, st["q_col"])
        grads["w_mem_kv"][i] = _mm_tn("grad_w_mem_kv", memb, dmkv.astype(BF16))
        if dil_layer:
            early_sums = _chip_sums_of_early(grads) if (i == 0 and exchanging) else None
            dh_parts, exchanged = _band_attn_bwd_fused(st["h"], dcat, st["mix"], st["lse"], slopes, bsz, seq,
                                                       exchange=early_sums)
            if early_sums is not None:
                early_parts = exchanged
        else:
            ws_t = jnp.swapaxes(st["ws"], -1, -2)
            dh_main, dws, dbs_t, dlg, dlb = _sgu_bwd(st["h"], dcat, st["ws"], ws_t, st["bs_t"], st["ln_g"], st["ln_b"])
            grads["sgu_w_s"][j], grads["sgu_b_s"][j] = dws, dbs_t.T
            grads["sgu_ln_g"][j], grads["sgu_ln_b"][j] = dlg[0], dlb[0]
            dh_parts = [dh_main]
        name = "in_proj_bwd_a" if dil_layer else "in_proj_bwd_b"
        if i > 0:
            dr2, dr2b, dg, db = _linear_nt(name, [*dh_parts, dqm], w_in, dr1, F32,
                                           ln=(saved[i - 1]["r2"], ln_ffn_g[i - 1].reshape(1, D_MODEL)))
        else:
            grad_x = _linear_nt(name + "_x", [*dh_parts, dqm], w_in, dr1, F32).reshape(x.shape)
        grads["a_w_in" if dil_layer else "b_w_in"][j] = jnp.concatenate(
            [_mm_tn("grad_w_in_part", st["xb"], part) for part in dh_parts] + [_mm_tn("grad_w_in_qm", st["xb"], dqm)], axis=1)
    return loss_part, grad_x, {n: jnp.stack(g) for n, g in grads.items()}, early_parts
```

```python
import functools
import math

import numpy as np
import jax
import jax.numpy as jnp
from jax import lax
from jax.experimental import pallas as pl
from jax.experimental.pallas import tpu as pltpu

F32 = jnp.float32
BF16 = jnp.bfloat16

D_MODEL = 1024
DEPTH = 4
N_MEM = 256
HEAD_DIM = 64
N_HEADS = 12
MIX_W = N_HEADS * HEAD_DIM
MEM_W = 4 * HEAD_DIM
DIL_PATTERNS = ((128, 1), (512, 4), (2048, 16))
BLK = 128
HEAD_GROUP = 4
N_GROUPS = N_HEADS // HEAD_GROUP
D_FF = 2816
FF_CHUNKS = 2
ALPHA = (2 * DEPTH) ** 0.25
LN_EPS = 1e-5
SCALE = HEAD_DIM ** -0.5
NEG = -1e30
N_DEV = 8

ADAM_LR, ADAM_B1, ADAM_B2, ADAM_EPS, ADAM_WD, ADAM_STEP = 0.001, 0.9, 0.999, 1e-08, 0.01, 10

VMEM_LIMIT = 56 * 2 ** 20
STAT_LANES = 32
STAT_W = N_HEADS * STAT_LANES


def _dot_nn(a, b):
    return lax.dot_general(a, b, (((1,), (0,)), ((), ())), preferred_element_type=F32)


def _dot_nt(a, b):
    return lax.dot_general(a, b, (((1,), (1,)), ((), ())), preferred_element_type=F32)


def _dot_tn(a, b):
    return lax.dot_general(a, b, (((0,), (0,)), ((), ())), preferred_element_type=F32)


def _ln_hat(r):
    mu = jnp.mean(r, axis=-1, keepdims=True)
    xc = r - mu
    var = jnp.mean(xc * xc, axis=-1, keepdims=True)
    rstd = lax.rsqrt(var + LN_EPS)
    return xc * rstd, rstd


def _params(sem):
    return pltpu.CompilerParams(dimension_semantics=sem, vmem_limit_bytes=VMEM_LIMIT)


def _rows(tm, c, col=0):
    return pl.BlockSpec((tm, c), lambda i: (i, col))


def _whole(shape):
    nd = len(shape)
    return pl.BlockSpec(tuple(shape), lambda *_: (0,) * nd)


def _resident(shape):
    nd = len(shape)
    return pl.BlockSpec(tuple(shape), lambda *_: (0,) * nd, pipeline_mode=pl.Buffered(1))


def _sds(shape, dtype):
    return jax.ShapeDtypeStruct(tuple(shape), dtype)


def _linear_nn(name, a, w, tm=512):
    t, k = a.shape
    n = w.shape[1]
    tm = min(tm, t)

    def body(a_ref, w_ref, o_ref):
        o_ref[...] = _dot_nn(a_ref[...], w_ref[...]).astype(BF16)

    return pl.pallas_call(
        body, name=name, grid=(t // tm,), in_specs=[_rows(tm, k), _resident(w.shape)], out_specs=_rows(tm, n),
        out_shape=_sds((t, n), BF16), compiler_params=_params(("parallel",)))(a, w)


def _linear_nn_gathered(name, a, shards, tm=512):
    t, k = a.shape
    n8 = shards.shape[2]
    n = N_DEV * n8

    def body(a_ref, s_ref, o_ref, w_ref):
        @pl.when(pl.program_id(0) == 0)
        def _():
            for s in range(N_DEV):
                w_ref[:, s * n8:(s + 1) * n8] = s_ref[s]

        o_ref[...] = _dot_nn(a_ref[...], w_ref[...]).astype(BF16)

    return pl.pallas_call(
        body, name=name, grid=(t // tm,), in_specs=[_rows(tm, k), _resident(shards.shape)],
        out_specs=[_rows(tm, n), _whole((k, n))], out_shape=[_sds((t, n), BF16), _sds((k, n), BF16)],
        compiler_params=_params(("arbitrary",)))(a, shards)


def _proj_ln_fwd(name, lhs, w, res, g, b, gather=None, tm=512):
    t = res[0].shape[0]
    n_lhs = len(lhs)
    n_res = len(res)
    ng = 0 if gather is None else len(gather)
    steps = t // tm
    n_in = n_lhs + 3 + n_res

    def body(*refs):
        lhs_refs = refs[:n_lhs]
        w_ref = refs[n_lhs]
        res_refs = refs[n_lhs + 1:n_lhs + 1 + n_res]
        g_ref, b_ref = refs[n_in - 2:n_in]
        r_ref, xnb_ref = refs[n_in + ng:n_in + ng + 2]
        if ng:
            start, finish = _gather_protocol(refs[n_in:n_in + ng], refs[n_in + ng + 2:n_in + 2 * ng + 2], *refs[n_in + 2 * ng + 2:])
            pl.when(pl.program_id(0) == 0)(start)
        y, off = None, 0
        for lr in lhs_refs:
            k = lr.shape[1]
            term = _dot_nn(lr[...], w_ref[off:off + k, :])
            y = term if y is None else y + term
            off += k
        x_res = res_refs[0][...]
        if n_res == 3:
            x_res = _ln_hat(x_res)[0] * res_refs[1][...] + res_refs[2][...]
        r = ALPHA * x_res + y
        r_ref[...] = r
        xnb_ref[...] = (_ln_hat(r)[0] * g_ref[...] + b_ref[...]).astype(BF16)
        if ng:
            pl.when(pl.program_id(0) == steps - 1)(finish)

    vec = _whole((1, D_MODEL))
    in_specs = ([_rows(tm, a.shape[1]) for a in lhs] + [_resident(w.shape), _rows(tm, D_MODEL)] + [vec] * (n_res - 1) + [vec, vec])
    outs = pl.pallas_call(
        body, name=name + "_gather" if ng else name, grid=(steps,), in_specs=in_specs + [_ANY] * ng,
        out_specs=[_rows(tm, D_MODEL)] * 2 + [_ANY] * ng,
        out_shape=[_sds((t, D_MODEL), F32), _sds((t, D_MODEL), BF16)] + (_gather_shapes(gather) if ng else []),
        scratch_shapes=_gather_sems(ng) if ng else [],
        compiler_params=_params(("arbitrary" if ng else "parallel",)))(*lhs, w, *res, g, b, *(gather or []))
    return outs[0], outs[1], list(outs[2:])


def _ffn_up_fwd(xb, gate_shards, up_shards, gather=None, tm=256):
    t = xb.shape[0]
    ng = 0 if gather is None else len(gather)
    steps = t // tm
    n8 = gate_shards.shape[2]

    def body(*refs):
        x_ref, gs_ref, us_ref = refs[:3]
        g_ref, u_ref, a_ref, wgu_ref = refs[3 + ng:7 + ng]
        w_ref, w_sem = refs[7 + 2 * ng:9 + 2 * ng]
        keep = pltpu.make_async_copy(w_ref, wgu_ref, w_sem)

        @pl.when(pl.program_id(0) == 0)
        def _():
            for s in range(N_DEV):
                w_ref[:, s * n8:(s + 1) * n8] = gs_ref[s]
                w_ref[:, D_FF + s * n8:D_FF + (s + 1) * n8] = us_ref[s]
            keep.start()

        if ng:
            start, finish = _gather_protocol(refs[3:3 + ng], refs[7 + ng:7 + 2 * ng], *refs[9 + 2 * ng:])
            pl.when(pl.program_id(0) == 0)(start)
        xv = x_ref[...]
        for c in range(FF_CHUNKS):
            cols = slice(c * D_FF // FF_CHUNKS, (c + 1) * D_FF // FF_CHUNKS)
            gt = _dot_nn(xv, w_ref[:, cols])
            up = _dot_nn(xv, w_ref[:, D_FF + cols.start:D_FF + cols.stop])
            sg = jax.nn.sigmoid(gt)
            silu = gt * sg
            g_ref[:, cols] = (up * (sg * (1.0 + gt * (1.0 - sg)))).astype(BF16)
            u_ref[:, cols] = silu.astype(BF16)
            a_ref[:, cols] = (silu * up).astype(BF16)
        if ng:
            pl.when(pl.program_id(0) == steps - 1)(finish)
        pl.when(pl.program_id(0) == steps - 1)(keep.wait)

    k = gate_shards.shape[1]
    outs = pl.pallas_call(
        body, name="ffn_up_fwd_gather" if ng else "ffn_up_fwd", grid=(steps,),
        in_specs=[_rows(tm, D_MODEL), _resident(gate_shards.shape), _resident(up_shards.shape)] + [_ANY] * ng,
        out_specs=[_rows(tm, D_FF)] * 3 + [_ANY] + [_ANY] * ng,
        out_shape=[_sds((t, D_FF), BF16)] * 3 + [_sds((k, 2 * D_FF), BF16)] + (_gather_shapes(gather) if ng else []),
        scratch_shapes=[pltpu.VMEM((k, 2 * D_FF), BF16), pltpu.SemaphoreType.DMA(())] + (_gather_sems(ng) if ng else []),
        compiler_params=_params(("arbitrary",)))(xb, gate_shards, up_shards, *(gather or []))
    return outs[0], outs[1], outs[2], outs[3], list(outs[4:])


def _ln_bwd_rows(dxn, xhat, rstd, g_ref, dr_ref, drb_ref, dg_ref, db_ref):
    @pl.when(pl.program_id(0) == 0)
    def _():
        dg_ref[...] = jnp.zeros_like(dg_ref)
        db_ref[...] = jnp.zeros_like(db_ref)

    dxh = dxn * g_ref[...]
    m1 = jnp.mean(dxh, axis=-1, keepdims=True)
    m2 = jnp.mean(dxh * xhat, axis=-1, keepdims=True)
    dr = rstd * (dxh - m1 - xhat * m2)
    dr_ref[...] = dr
    drb_ref[...] = dr.astype(BF16)
    dg_ref[...] += jnp.sum(dxn * xhat, axis=0, keepdims=True)
    db_ref[...] += jnp.sum(dxn, axis=0, keepdims=True)


def _ln_bwd_outs(t, tm):
    vec = _whole((1, D_MODEL))
    specs = [_rows(tm, D_MODEL), _rows(tm, D_MODEL), vec, vec]
    shapes = [_sds((t, D_MODEL), F32), _sds((t, D_MODEL), BF16), _sds((1, D_MODEL), F32), _sds((1, D_MODEL), F32)]
    return specs, shapes


def _loss_ln_bwd(r, g, b, tgt, tm=512):
    t = r.shape[0]

    def body(r_ref, g_ref, b_ref, t_ref, dr_ref, drb_ref, dg_ref, db_ref, l_ref):
        @pl.when(pl.program_id(0) == 0)
        def _():
            l_ref[...] = jnp.zeros_like(l_ref)

        xhat, rstd = _ln_hat(r_ref[...])
        e = xhat * g_ref[...] + b_ref[...] - t_ref[...]
        l_ref[...] += jnp.sum(e * e) * (0.5 / D_MODEL)
        _ln_bwd_rows(e * (1.0 / D_MODEL), xhat, rstd, g_ref, dr_ref, drb_ref, dg_ref, db_ref)

    vec = _whole((1, D_MODEL))
    specs, shapes = _ln_bwd_outs(t, tm)
    return pl.pallas_call(
        body, name="loss_ln_bwd", grid=(t // tm,), in_specs=[_rows(tm, D_MODEL), vec, vec, _rows(tm, D_MODEL)],
        out_specs=specs + [_whole((1, 128))], out_shape=shapes + [_sds((1, 128), F32)],
        compiler_params=_params(("arbitrary",)))(r, g, b, tgt)


def _ffn_down_bwd(drb, wd, gt, up, tm=512):
    t = drb.shape[0]

    def body(d_ref, w_ref, g_ref, u_ref, o_ref):
        dv = d_ref[...]
        for c in range(FF_CHUNKS):
            cols = slice(c * D_FF // FF_CHUNKS, (c + 1) * D_FF // FF_CHUNKS)
            da = _dot_nt(dv, w_ref[cols, :])
            o_ref[:, cols] = (da * g_ref[:, cols].astype(F32)).astype(BF16)
            o_ref[:, D_FF + cols.start:D_FF + cols.stop] = (da * u_ref[:, cols].astype(F32)).astype(BF16)

    return pl.pallas_call(
        body, name="ffn_down_bwd", grid=(t // tm,),
        in_specs=[_rows(tm, D_MODEL), _resident(wd.shape), _rows(tm, D_FF), _rows(tm, D_FF)],
        out_specs=_rows(tm, 2 * D_FF), out_shape=_sds((t, 2 * D_FF), BF16),
        compiler_params=_params(("parallel",)))(drb, wd, gt, up)


def _linear_nt(name, lhs, w, res, out_dtype, ln=None, tm=512):
    t = lhs[0].shape[0]
    n_lhs = len(lhs)
    n_out = w.shape[0]
    n_in = n_lhs + 1 + (res is not None) + (2 if ln else 0)

    def body(*refs):
        lhs_refs = refs[:n_lhs]
        w_ref = refs[n_lhs]
        y, off = None, 0
        for lr in lhs_refs:
            k = lr.shape[1]
            term = _dot_nt(lr[...], w_ref[:, off:off + k])
            y = term if y is None else y + term
            off += k
        if res is not None:
            y = ALPHA * refs[n_lhs + 1][...] + y
        if ln is None:
            refs[-1][...] = y.astype(out_dtype)
        else:
            r_ref, g_ref = refs[n_in - 2:n_in]
            xhat, rstd = _ln_hat(r_ref[...])
            _ln_bwd_rows(y, xhat, rstd, g_ref, *refs[n_in:])

    in_specs = [_rows(tm, a.shape[1]) for a in lhs] + [_resident(w.shape)]
    args = list(lhs) + [w]
    if res is not None:
        in_specs.append(_rows(tm, n_out))
        args.append(res)
    if ln is None:
        out_specs, out_shape, sem = _rows(tm, n_out), _sds((t, n_out), out_dtype), "parallel"
    else:
        in_specs += [_rows(tm, D_MODEL), _whole((1, D_MODEL))]
        args += list(ln)
        (out_specs, out_shape), sem = _ln_bwd_outs(t, tm), "arbitrary"
    return pl.pallas_call(
        body, name=name, grid=(t // tm,), in_specs=in_specs, out_specs=out_specs, out_shape=out_shape,
        compiler_params=_params((sem,)))(*args)


def _pick_tile(n, limit):
    if n <= limit:
        return n
    best = 128
    for cand in range(128, limit + 1, 128):
        if n % cand == 0:
            best = cand
    return best


def _mm_tn(name, a, b, tt=1024):
    t, k = a.shape
    n = b.shape[1]
    tt = min(tt, t)
    tk = _pick_tile(k, 1408)
    tn = _pick_tile(n, (6 * 2 ** 20) // (4 * tk) // 128 * 128)
    steps = t // tt

    def body(a_ref, b_ref, o_ref, acc_ref):
        @pl.when(pl.program_id(2) == 0)
        def _():
            acc_ref[...] = jnp.zeros_like(acc_ref)

        acc_ref[...] += _dot_tn(a_ref[...], b_ref[...])

        @pl.when(pl.program_id(2) == steps - 1)
        def _():
            o_ref[...] = acc_ref[...].astype(BF16)

    return pl.pallas_call(
        body, name=name, grid=(k // tk, n // tn, steps),
        in_specs=[pl.BlockSpec((tt, tk), lambda i, j, s: (s, i)), pl.BlockSpec((tt, tn), lambda i, j, s: (s, j))],
        out_specs=pl.BlockSpec((tk, tn), lambda i, j, s: (i, j)), out_shape=_sds((k, n), BF16),
        scratch_shapes=[pltpu.VMEM((tk, tn), F32)],
        compiler_params=_params(("parallel", "parallel", "arbitrary")))(a, b)


def _alibi_table():
    arr = np.zeros((N_GROUPS, 8, 128), np.float32)
    for g in range(N_GROUPS):
        for hh in range(HEAD_GROUP):
            arr[g, hh, :] = 2.0 ** (-8.0 * (g * HEAD_GROUP + hh + 1) / N_HEADS)
    return jnp.asarray(arr)


def _spread_stats(cols, per_head=STAT_LANES):
    lane = lax.broadcasted_iota(jnp.int32, (BLK, HEAD_GROUP * per_head), 1)
    tile = cols[HEAD_GROUP - 1]
    for hh in range(HEAD_GROUP - 2, -1, -1):
        tile = jnp.where(lane < (hh + 1) * per_head, cols[hh], tile)
    return tile


def _block_mask(has_prev, dil):
    if has_prev is None:
        steps = lax.broadcasted_iota(jnp.int32, (BLK, BLK), 0) - lax.broadcasted_iota(jnp.int32, (BLK, BLK), 1)
        return steps >= 0, (steps * dil).astype(F32)
    qi = lax.broadcasted_iota(jnp.int32, (BLK, 2 * BLK), 0)
    ki = lax.broadcasted_iota(jnp.int32, (BLK, 2 * BLK), 1)
    steps = qi + BLK - ki
    valid = (steps >= 0) & (steps <= BLK) & ((ki >= BLK) | has_prev)
    return valid, (steps * dil).astype(F32)


def _bias_scratch():
    return pltpu.VMEM((2, HEAD_GROUP, BLK, 2 * BLK), F32)


def _fill_bias(bias, sl_ref, dil):
    for p in range(2):
        valid, dist = _block_mask(p == 1, dil)
        for hh in range(HEAD_GROUP):
            bias[p, hh] = jnp.where(valid, -sl_ref[hh:hh + 1, 0:1] * dist, NEG)


def _rows_of(j):
    return pl.ds(pl.multiple_of(j * BLK, BLK), BLK)


def _lane_half(hf):
    return slice(hf * 128, (hf + 1) * 128)


def _split_pair(x):
    first = lax.broadcasted_iota(jnp.int32, (1, 2 * HEAD_DIM), 1) < HEAD_DIM
    zero = jnp.zeros_like(x)
    return jnp.where(first, x, zero), jnp.where(first, zero, x)


def _deinterleave(src, dst, seq, dil, dtype):
    length = seq // dil
    for r in range(dil):
        for c in range(length // BLK):
            rows = pl.ds(r + c * BLK * dil, BLK, stride=dil)
            out = slice(r * length + c * BLK, r * length + (c + 1) * BLK)
            if len(src.shape) == 2:
                dst[out, :] = src[rows, :].astype(dtype)
            else:
                for hf in range(2):
                    dst[out, _lane_half(hf)] = src.at[hf][rows, :].astype(dtype)


def _interleave(src, dst, seq, dil, accumulate):
    length = seq // dil
    for r in range(dil):
        for c in range(length // BLK):
            rows = pl.ds(r + c * BLK * dil, BLK, stride=dil)
            inp = slice(r * length + c * BLK, r * length + (c + 1) * BLK)
            if len(dst.shape) == 2:
                dst[rows, :] = dst[rows, :] + src[inp, :] if accumulate else src[inp, :]
            else:
                for hf in range(2):
                    val = src[inp, _lane_half(hf)]
                    half = dst.at[hf]
                    half[rows, :] = half[rows, :] + val if accumulate else val


def _split_halves(src, dst, seq):
    def step(i, carry):
        for hf in range(2):
            dst[hf, _rows_of(i), :] = src[_rows_of(i), _lane_half(hf)].astype(F32)
        return carry

    lax.fori_loop(0, seq // BLK, step, 0)


def _band_attn_fwd_fused(h, slopes, bsz, seq, gather=None):
    width = h.shape[1]
    cb = width // 256
    k_off, v_off = MIX_W // 256, 2 * MIX_W // 256
    nb = seq // BLK

    ng = 0 if gather is None else len(gather)

    def body(*refs):
        sl_ref, q_ref, k_ref, v_ref = refs[:4]
        mix_ref, lse_ref = refs[4 + ng:6 + ng]
        qf, kf, vf, qd, kd, vd, od, ld, o1, o2, o3, l1, l2, l3, bias = refs[6 + 2 * ng:21 + 2 * ng]
        if ng:
            start, finish = _gather_protocol(refs[4:4 + ng], refs[6 + ng:6 + 2 * ng], *refs[21 + 2 * ng:])
            pl.when((pl.program_id(0) == 0) & (pl.program_id(1) == 0))(start)

        def run(dil, qs, ks, vs, o_dst, l_dst):
            nblk = seq // dil // BLK
            _fill_bias(bias, sl_ref, dil)

            def block(j, carry):
                rows, prows = _rows_of(j), _rows_of(jnp.maximum(j - 1, 0))
                has_prev = ((j % nblk) != 0).astype(jnp.int32)

                def keys(ref, lanes):
                    return jnp.concatenate([ref[prows, lanes], ref[rows, lanes]], axis=0)

                lses = []
                for pr in range(HEAD_GROUP // 2):
                    lanes = _lane_half(pr)
                    q_ab = _split_pair(qs[rows, lanes] * SCALE)
                    k2 = keys(ks, lanes)
                    v_ab = _split_pair(keys(vs, lanes))
                    out = None
                    for ab in range(2):
                        hh = 2 * pr + ab
                        s = _dot_nt(q_ab[ab], k2) + bias[has_prev, hh]
                        m = jnp.max(s, axis=-1, keepdims=True)
                        p = jnp.exp(s - m)
                        l = jnp.sum(p, axis=-1, keepdims=True)
                        term = _dot_nn(p.astype(BF16), v_ab[ab]) / l
                        out = term if out is None else out + term
                        lses.append(m + jnp.log(l))
                    o_dst[rows, lanes] = out
                l_dst[rows, :] = _spread_stats(lses, HEAD_DIM)
                return carry

            lax.fori_loop(0, nb, block, 0, unroll=8)

        run(1, q_ref, k_ref, v_ref, o1, l1)
        _split_halves(q_ref, qf, seq)
        _split_halves(k_ref, kf, seq)
        _split_halves(v_ref, vf, seq)
        for dil, o_tok, l_tok in ((4, o2, l2), (16, o3, l3)):
            _deinterleave(qf, qd, seq, dil, BF16)
            _deinterleave(kf, kd, seq, dil, BF16)
            _deinterleave(vf, vd, seq, dil, BF16)
            run(dil, qd, kd, vd, od, ld)
            _interleave(od, o_tok, seq, dil, False)
            _interleave(ld, l_tok, seq, dil, False)

        def merge(i, carry):
            rows = _rows_of(i)

            def both(ref):
                return jnp.concatenate([ref[0, rows, :], ref[1, rows, :]], axis=1)

            ls = [l1[rows, :], both(l2), both(l3)]
            m = jnp.maximum(jnp.maximum(ls[0], ls[1]), ls[2])
            tot = m + jnp.log(jnp.exp(ls[0] - m) + jnp.exp(ls[1] - m) + jnp.exp(ls[2] - m))
            ws = [jnp.exp(x - tot) for x in ls]
            mix_ref[rows, :] = (ws[0] * o1[rows, :] + ws[1] * both(o2) + ws[2] * both(o3)).astype(BF16)
            lse_ref[rows, :] = _spread_stats([tot[:, hh * HEAD_DIM:hh * HEAD_DIM + 1] for hh in range(HEAD_GROUP)])
            return carry

        lax.fori_loop(0, nb, merge, 0)
        if ng:
            pl.when((pl.program_id(0) == bsz - 1) & (pl.program_id(1) == N_GROUPS - 1))(finish)

    def hspec(off):
        return pl.BlockSpec((seq, 256), lambda b, g: (b, off + g))

    big = lambda dt: pltpu.VMEM((seq, 256), dt)
    halves = lambda: pltpu.VMEM((2, seq, 128), F32)
    stat = lambda: pltpu.VMEM((seq, 128), F32)
    outs = pl.pallas_call(
        body, name="band_attn_fwd_gather" if ng else "band_attn_fwd", grid=(bsz, N_GROUPS),
        in_specs=[pl.BlockSpec((None, 8, 128), lambda b, g: (g, 0, 0)), hspec(0), hspec(k_off), hspec(v_off)] + [_ANY] * ng,
        out_specs=[pl.BlockSpec((seq, 256), lambda b, g: (b, g)), pl.BlockSpec((seq, 128), lambda b, g: (b, g))] + [_ANY] * ng,
        out_shape=[_sds((bsz * seq, MIX_W), BF16), _sds((bsz * seq, STAT_W), F32)] + (_gather_shapes(gather) if ng else []),
        scratch_shapes=[halves(), halves(), halves(), big(BF16), big(BF16), big(BF16), big(F32), big(F32),
                        big(F32), halves(), halves(), big(F32), halves(), halves(), _bias_scratch()]
        + (_gather_sems(ng) if ng else []),
        compiler_params=_params(("arbitrary", "arbitrary")))(slopes, h, h, h, *(gather or []))
    return outs[0], outs[1], list(outs[2:])


def _band_attn_bwd_fused(h, dcat, mix, lse, slopes, bsz, seq, exchange=None):
    width = h.shape[1]
    k_off, v_off = MIX_W // 256, 2 * MIX_W // 256
    nb = seq // BLK

    ne = 0 if exchange is None else len(exchange)

    def body(*refs):
        sl_ref, q_ref, k_ref, v_ref, do_ref, o_ref, lse_ref = refs[:7]
        dq_ref, dk_ref, dv_ref = refs[7 + ne:10 + ne]
        qf, kf, vf, dof, ddt, qd, kd, vd, dod, lsd, ddd, gq, gk, gv, aq, ak, av, bias = refs[10 + 2 * ne:28 + 2 * ne]
        if ne:
            start, finish = _exchange_protocol(refs[7:7 + ne], refs[10 + ne:10 + 2 * ne], *refs[28 + 2 * ne:])
            pl.when((pl.program_id(0) == 0) & (pl.program_id(1) == 0))(start)

        same_head = (lax.broadcasted_iota(jnp.int32, (HEAD_GROUP * HEAD_DIM, HEAD_GROUP * STAT_LANES), 0) // HEAD_DIM
                     == lax.broadcasted_iota(jnp.int32, (HEAD_GROUP * HEAD_DIM, HEAD_GROUP * STAT_LANES), 1) // STAT_LANES)
        ones_map = jnp.where(same_head, 1.0, 0.0).astype(BF16)

        def delta(i, carry):
            rows = _rows_of(i)
            prod = do_ref[rows, :].astype(F32) * o_ref[rows, :].astype(F32)
            high = prod.astype(BF16)
            rest = (prod - high.astype(F32)).astype(BF16)
            ddt[rows, :] = _dot_nn(high, ones_map) + _dot_nn(rest, ones_map)
            return carry

        lax.fori_loop(0, nb, delta, 0)

        def zero(i, carry):
            rows = _rows_of(i)
            for ref in (gk, gv):
                ref[rows, :] = jnp.zeros((BLK, 256), F32)
            return carry

        def run(dil, qs, ks, vs, dos, lss, dds):
            nblk = seq // dil // BLK
            _fill_bias(bias, sl_ref, dil)
            if nblk > 1:
                lax.fori_loop(0, nb, zero, 0)

            def block(j, carry):
                rows, prows = _rows_of(j), _rows_of(jnp.maximum(j - 1, 0))
                has_prev = ((j % nblk) != 0).astype(jnp.int32)

                def keys(ref, lanes):
                    if nblk == 1:
                        return ref[rows, lanes]
                    return jnp.concatenate([ref[prows, lanes], ref[rows, lanes]], axis=0)

                for pr in range(HEAD_GROUP // 2):
                    lanes = _lane_half(pr)
                    q_ab = _split_pair(qs[rows, lanes] * SCALE)
                    do_ab = _split_pair(dos[rows, lanes])
                    k2, v2 = keys(ks, lanes), keys(vs, lanes)
                    k_ab = _split_pair(k2)
                    dq, dk2, dv2 = None, None, None
                    for ab in range(2):
                        hh = 2 * pr + ab
                        st = slice(hh * STAT_LANES, hh * STAT_LANES + 1)
                        s = _dot_nt(q_ab[ab], k2) + (bias[0, hh, :, BLK:] if nblk == 1 else bias[has_prev, hh])
                        p = jnp.exp(s - lss[rows, st])
                        dp = _dot_nt(do_ab[ab], v2)
                        ds = (p * (dp - dds[rows, st])).astype(BF16)
                        terms = (_dot_nn(ds, k_ab[ab]), _dot_tn(ds, q_ab[ab]), _dot_tn(p.astype(BF16), do_ab[ab]))
                        dq, dk2, dv2 = terms if dq is None else (dq + terms[0], dk2 + terms[1], dv2 + terms[2])
                    gq[rows, lanes] = dq * SCALE
                    if nblk == 1:
                        gk[rows, lanes] = dk2
                        gv[rows, lanes] = dv2
                    else:
                        gk[prows, lanes] += dk2[:BLK]
                        gv[prows, lanes] += dv2[:BLK]
                        gk[rows, lanes] += dk2[BLK:]
                        gv[rows, lanes] += dv2[BLK:]
                return carry

            lax.fori_loop(0, nb, block, 0, unroll=4)

        run(1, q_ref, k_ref, v_ref, do_ref, lse_ref, ddt)

        for src, dst in ((gq, aq), (gk, ak), (gv, av), (q_ref, qf), (k_ref, kf), (v_ref, vf), (do_ref, dof)):
            _split_halves(src, dst, seq)
        for dil in (4, 16):
            for src, dst in ((qf, qd), (kf, kd), (vf, vd), (dof, dod)):
                _deinterleave(src, dst, seq, dil, BF16)
            _deinterleave(lse_ref, lsd, seq, dil, F32)
            _deinterleave(ddt, ddd, seq, dil, F32)
            run(dil, qd, kd, vd, dod, lsd, ddd)
            for src, dst in ((gq, aq), (gk, ak), (gv, av)):
                _interleave(src, dst, seq, dil, True)

        def write(i, carry):
            rows = _rows_of(i)
            for src, dst in ((aq, dq_ref), (ak, dk_ref), (av, dv_ref)):
                for hf in range(2):
                    dst[rows, _lane_half(hf)] = src[hf, rows, :].astype(BF16)
            return carry

        lax.fori_loop(0, nb, write, 0)
        if ne:
            pl.when((pl.program_id(0) == bsz - 1) & (pl.program_id(1) == N_GROUPS - 1))(finish)

    def hspec(off):
        return pl.BlockSpec((seq, 256), lambda b, g: (b, off + g))

    io = pl.BlockSpec((seq, 256), lambda b, g: (b, g))
    big = lambda dt: pltpu.VMEM((seq, 256), dt)
    halves = lambda: pltpu.VMEM((2, seq, 128), F32)
    stat = lambda: pltpu.VMEM((seq, 128), F32)
    outs = pl.pallas_call(
        body, name="band_attn_bwd_exchange" if ne else "band_attn_bwd", grid=(bsz, N_GROUPS),
        in_specs=[pl.BlockSpec((None, 8, 128), lambda b, g: (g, 0, 0)), hspec(0), hspec(k_off), hspec(v_off), io, io,
                  pl.BlockSpec((seq, 128), lambda b, g: (b, g))] + [_ANY] * ne,
        out_specs=[io, io, io] + [_ANY] * ne,
        out_shape=[_sds((bsz * seq, MIX_W), BF16)] * 3 + [_sds(s.shape, s.dtype) for s in (exchange or [])],
        scratch_shapes=[halves(), halves(), halves(), halves(), stat(),
                        big(BF16), big(BF16), big(BF16), big(BF16), stat(), stat(),
                        big(F32), big(F32), big(F32), halves(), halves(), halves(), _bias_scratch()]
        + (_exchange_sems(ne) if ne else []),
        compiler_params=_params(("arbitrary", "arbitrary")))(slopes, h, h, h, dcat, mix, lse, *(exchange or []))
    return list(outs[:3]), list(outs[3:])


def _mem_attn_fwd(h, mkv, bsz, seq, q_col, tq=1024):
    nq = seq // tq

    def body(q_ref, kv_ref, o_ref):
        for pr in range(2):
            lanes = _lane_half(pr)
            q_ab = _split_pair(q_ref[:, lanes])
            k = kv_ref[:, lanes]
            v_ab = _split_pair(kv_ref[:, MEM_W + pr * 128:MEM_W + (pr + 1) * 128])
            out = None
            for ab in range(2):
                s = _dot_nt(q_ab[ab], k) * SCALE
                m = jnp.max(s, axis=-1, keepdims=True)
                p = jnp.exp(s - m)
                l = jnp.sum(p, axis=-1, keepdims=True)
                term = _dot_nn(p.astype(BF16), v_ab[ab]) / l
                out = term if out is None else out + term
            o_ref[:, lanes] = out.astype(BF16)

    return pl.pallas_call(
        body, name="mem_attn_fwd", grid=(bsz, nq),
        in_specs=[pl.BlockSpec((tq, MEM_W), lambda b, i: (b * nq + i, q_col)),
                  pl.BlockSpec((N_MEM, 2 * MEM_W), lambda b, i: (b, 0))],
        out_specs=pl.BlockSpec((tq, MEM_W), lambda b, i: (b * nq + i, 0)),
        out_shape=_sds((bsz * seq, MEM_W), BF16), compiler_params=_params(("parallel", "parallel")))(h, mkv)


def _mem_attn_bwd(h, mkv, dcat, bsz, seq, q_col, tq=1024):
    nq = seq // tq
    do_col = MIX_W // MEM_W

    def body(q_ref, kv_ref, do_ref, dq_ref, dkv_ref):
        @pl.when(pl.program_id(1) == 0)
        def _():
            dkv_ref[...] = jnp.zeros_like(dkv_ref)

        for pr in range(2):
            lanes = _lane_half(pr)
            vlanes = slice(MEM_W + pr * 128, MEM_W + (pr + 1) * 128)
            q_ab = _split_pair(q_ref[:, lanes])
            do_ab = _split_pair(do_ref[:, lanes])
            k, v = kv_ref[:, lanes], kv_ref[:, vlanes]
            k_ab = _split_pair(k)
            dq, dk, dv = None, None, None
            for ab in range(2):
                s = _dot_nt(q_ab[ab], k) * SCALE
                m = jnp.max(s, axis=-1, keepdims=True)
                e = jnp.exp(s - m)
                p = e / jnp.sum(e, axis=-1, keepdims=True)
                dp = _dot_nt(do_ab[ab], v)
                dd = jnp.sum(p * dp, axis=-1, keepdims=True)
                ds = (p * (dp - dd) * SCALE).astype(BF16)
                terms = (_dot_nn(ds, k_ab[ab]), _dot_tn(ds, q_ab[ab]), _dot_tn(p.astype(BF16), do_ab[ab]))
                dq, dk, dv = terms if dq is None else (dq + terms[0], dk + terms[1], dv + terms[2])
            dq_ref[:, lanes] = dq.astype(BF16)
            dkv_ref[:, lanes] += dk
            dkv_ref[:, vlanes] += dv

    return pl.pallas_call(
        body, name="mem_attn_bwd", grid=(bsz, nq),
        in_specs=[pl.BlockSpec((tq, MEM_W), lambda b, i: (b * nq + i, q_col)),
                  pl.BlockSpec((N_MEM, 2 * MEM_W), lambda b, i: (b, 0)),
                  pl.BlockSpec((tq, MEM_W), lambda b, i: (b * nq + i, do_col))],
        out_specs=[pl.BlockSpec((tq, MEM_W), lambda b, i: (b * nq + i, 0)),
                   pl.BlockSpec((N_MEM, 2 * MEM_W), lambda b, i: (b, 0))],
        out_shape=[_sds((bsz * seq, MEM_W), BF16), _sds((bsz * N_MEM, 2 * MEM_W), F32)],
        compiler_params=_params(("parallel", "arbitrary")))(h, mkv, dcat)


_GELU_C = math.sqrt(2.0 / math.pi)
_GELU_A = 0.044715


def _gelu(x):
    return 0.5 * x * (1.0 + jnp.tanh(_GELU_C * (x + _GELU_A * x * x * x)))


def _gelu_grad(x):
    th = jnp.tanh(_GELU_C * (x + _GELU_A * x * x * x))
    return 0.5 * (1.0 + th) + 0.5 * x * (1.0 - th * th) * (_GELU_C * (1.0 + 3.0 * _GELU_A * x * x))


def _tril_mask(lower):
    ri = lax.broadcasted_iota(jnp.int32, (BLK, BLK), 0)
    ci = lax.broadcasted_iota(jnp.int32, (BLK, BLK), 1)
    return (ri >= ci) if lower else (ci >= ri)


def _sgu_fwd(h, ws, bs_t, ln_g, ln_b, tm=512):
    t = h.shape[0]

    def body(u_ref, v_ref, ws_ref, bs_ref, g_ref, b_ref, o_ref):
        ug = _gelu(u_ref[...].astype(F32))
        vhat, _ = _ln_hat(_gelu(v_ref[...].astype(F32)))
        vn = (vhat * g_ref[...] + b_ref[...]).astype(BF16)
        mask = _tril_mask(True)
        first = lax.broadcasted_iota(jnp.int32, (1, 2 * HEAD_DIM), 1) < HEAD_DIM
        for pr in range(N_HEADS // 2):
            lanes = _lane_half(pr)
            w_ab = [jnp.where(mask, ws_ref[2 * pr + ab], 0).astype(BF16) for ab in range(2)]
            bias = jnp.where(first, bs_ref[:, 2 * pr:2 * pr + 1], bs_ref[:, 2 * pr + 1:2 * pr + 2])
            for c in range(tm // BLK):
                rs = slice(c * BLK, (c + 1) * BLK)
                v_ab = _split_pair(vn[rs, lanes])
                mixed = _dot_nn(w_ab[0], v_ab[0]) + _dot_nn(w_ab[1], v_ab[1]) + bias
                o_ref[rs, lanes] = (ug[rs, lanes] * mixed).astype(BF16)

    return pl.pallas_call(
        body, name="sgu_fwd", grid=(t // tm,),
        in_specs=[_rows(tm, MIX_W, 0), _rows(tm, MIX_W, 1), _whole(ws.shape), _whole(bs_t.shape), _whole(ln_g.shape), _whole(ln_b.shape)],
        out_specs=_rows(tm, MIX_W), out_shape=_sds((t, MIX_W), BF16),
        compiler_params=_params(("parallel",)))(h, h, ws, bs_t, ln_g, ln_b)


def _sgu_bwd(h, dcat, ws, ws_t, bs_t, ln_g, ln_b, tm=512):
    t = h.shape[0]

    def body(u_ref, v_ref, do_ref, ws_ref, wst_ref, bs_ref, g_ref, b_ref, dh_ref, dws_ref, dbs_ref, dg_ref, db_ref, dvn_ref):
        @pl.when(pl.program_id(0) == 0)
        def _():
            dws_ref[...] = jnp.zeros_like(dws_ref)
            dbs_ref[...] = jnp.zeros_like(dbs_ref)
            dg_ref[...] = jnp.zeros_like(dg_ref)
            db_ref[...] = jnp.zeros_like(db_ref)

        u = u_ref[...].astype(F32)
        v = v_ref[...].astype(F32)
        do = do_ref[...].astype(F32)
        ug = _gelu(u)
        vhat, rstd = _ln_hat(_gelu(v))
        vn = (vhat * g_ref[...] + b_ref[...]).astype(BF16)
        dmixed_f = do * ug
        dmixed = dmixed_f.astype(BF16)
        low, upp = _tril_mask(True), _tril_mask(False)
        first = lax.broadcasted_iota(jnp.int32, (1, 2 * HEAD_DIM), 1) < HEAD_DIM
        for pr in range(N_HEADS // 2):
            lanes = _lane_half(pr)
            w_ab = [jnp.where(low, ws_ref[2 * pr + ab], 0).astype(BF16) for ab in range(2)]
            wt_ab = [jnp.where(upp, wst_ref[2 * pr + ab], 0).astype(BF16) for ab in range(2)]
            bias = jnp.where(first, bs_ref[:, 2 * pr:2 * pr + 1], bs_ref[:, 2 * pr + 1:2 * pr + 2])
            dws_acc = [None, None]
            dbs_acc = [None, None]
            for c in range(tm // BLK):
                rs = slice(c * BLK, (c + 1) * BLK)
                vn_pair = vn[rs, lanes]
                v_ab = _split_pair(vn_pair)
                mixed = _dot_nn(w_ab[0], v_ab[0]) + _dot_nn(w_ab[1], v_ab[1]) + bias
                dh_ref[rs, lanes] = (do[rs, lanes] * mixed * _gelu_grad(u[rs, lanes])).astype(BF16)
                dm_ab = _split_pair(dmixed[rs, lanes])
                dmf_ab = _split_pair(dmixed_f[rs, lanes])
                for ab in range(2):
                    term = _dot_nt(dm_ab[ab], vn_pair)
                    dws_acc[ab] = term if dws_acc[ab] is None else dws_acc[ab] + term
                    rsum = jnp.sum(dmf_ab[ab], axis=-1, keepdims=True)
                    dbs_acc[ab] = rsum if dbs_acc[ab] is None else dbs_acc[ab] + rsum
                dvn_ref[rs, lanes] = _dot_nn(wt_ab[0], dm_ab[0]) + _dot_nn(wt_ab[1], dm_ab[1])
            for ab in range(2):
                g = 2 * pr + ab
                dws_ref[g] += jnp.where(low, dws_acc[ab], 0.0)
                dbs_ref[:, g:g + 1] += dbs_acc[ab]
        dvn = dvn_ref[...]
        dg_ref[...] += jnp.sum(dvn * vhat, axis=0, keepdims=True)
        db_ref[...] += jnp.sum(dvn, axis=0, keepdims=True)
        dxh = dvn * g_ref[...]
        m1 = jnp.mean(dxh, axis=-1, keepdims=True)
        m2 = jnp.mean(dxh * vhat, axis=-1, keepdims=True)
        dvg = rstd * (dxh - m1 - vhat * m2)
        dh_ref[:, MIX_W:] = (dvg * _gelu_grad(v)).astype(BF16)

    return pl.pallas_call(
        body, name="sgu_bwd", grid=(t // tm,),
        in_specs=[_rows(tm, MIX_W, 0), _rows(tm, MIX_W, 1), _rows(tm, MIX_W, 0), _whole(ws.shape), _whole(ws_t.shape),
                  _whole(bs_t.shape), _whole(ln_g.shape), _whole(ln_b.shape)],
        out_specs=[_rows(tm, 2 * MIX_W), _whole(ws.shape), _whole(bs_t.shape), _whole((1, MIX_W)), _whole((1, MIX_W))],
        out_shape=[_sds((t, 2 * MIX_W), BF16), _sds(ws.shape, F32), _sds(bs_t.shape, F32), _sds((1, MIX_W), F32), _sds((1, MIX_W), F32)],
        scratch_shapes=[pltpu.VMEM((tm, MIX_W), F32)],
        compiler_params=_params(("arbitrary",)))(h, h, dcat, ws, ws_t, bs_t, ln_g, ln_b)


def _row_tile(rows, cols, itemsize=4, limit=2 ** 20):
    best = rows
    for cand in (4096, 2048, 1024, 512, 256, 128, 64, 32, 16):
        if rows % cand == 0 and rows > cand:
            best = cand
            if cand * cols * itemsize <= limit:
                break
    return best


def _adamw(w, m, v, grad=None, parts=None, first_parts=None):
    rows, cols = w.shape
    rows0 = 0 if first_parts is None else first_parts.shape[1]
    tr = _row_tile(rows0 if rows0 else rows, cols)
    n0 = rows0 // tr

    def chip_sum(ref):
        acc = ref[0].astype(F32)
        for k in range(1, 4):
            acc = acc + ref[k].astype(F32)
        return acc

    def body(*refs):
        w_ref, m_ref, v_ref = refs[:3]
        go_ref, d_ref, nm_ref, nv_ref = refs[-4:]
        if parts is None:
            gv = refs[3][...]
        elif first_parts is None:
            gv = chip_sum(refs[3])
        else:
            gv = jnp.where(pl.program_id(0) < n0, chip_sum(refs[3]), chip_sum(refs[4]))
        nm = ADAM_B1 * m_ref[...] + (1.0 - ADAM_B1) * gv
        nv = ADAM_B2 * v_ref[...] + (1.0 - ADAM_B2) * (gv * gv)
        m_hat = nm / (1.0 - ADAM_B1 ** ADAM_STEP)
        v_hat = nv / (1.0 - ADAM_B2 ** ADAM_STEP)
        go_ref[...] = gv
        d_ref[...] = -ADAM_LR * (m_hat / (jnp.sqrt(v_hat) + ADAM_EPS) + ADAM_WD * w_ref[...])
        nm_ref[...] = nm
        nv_ref[...] = nv

    spec = _rows(tr, cols)
    if parts is None:
        g_specs, g_args = [spec], [grad]
    elif first_parts is None:
        g_specs, g_args = [pl.BlockSpec((4, tr, cols), lambda i: (0, i, 0))], [parts]
    else:
        g_specs = [pl.BlockSpec((4, tr, cols), lambda i: (0, jnp.minimum(i, n0 - 1), 0)),
                   pl.BlockSpec((4, tr, cols), lambda i: (0, jnp.maximum(i - n0, 0), 0))]
        g_args = [first_parts, parts]
    return pl.pallas_call(
        body, name="adamw" if parts is None else "adamw_sum_chips", grid=(rows // tr,), in_specs=[spec] * 3 + g_specs,
        out_specs=[spec] * 4, out_shape=[_sds(w.shape, F32)] * 4,
        compiler_params=_params(("parallel",)))(w, m, v, *g_args)


_ANY = pl.BlockSpec(memory_space=pl.ANY)
_MESH = pl.DeviceIdType.MESH


def _all_gather(name, blocks):
    nt = len(blocks)

    def body(*refs):
        start, finish = _gather_protocol(refs[:nt], refs[nt:2 * nt], *refs[2 * nt:])
        start()
        finish()

    return pl.pallas_call(
        body, name=name, out_shape=_gather_shapes(blocks), in_specs=[_ANY] * nt, out_specs=[_ANY] * nt,
        scratch_shapes=_gather_sems(nt))(*blocks)


def _gather_shapes(blocks):
    return [_sds((N_DEV,) + b.shape, b.dtype) for b in blocks]


def _gather_sems(nt):
    return [pltpu.SemaphoreType.DMA((nt, 7)), pltpu.SemaphoreType.DMA((nt, 7)), pltpu.SemaphoreType.DMA((nt,))]


def _gather_protocol(x_refs, out_refs, send_sems, recv_sems, local_sems):
    nt = len(x_refs)
    x, y, c = lax.axis_index("x"), lax.axis_index("y"), lax.axis_index("c")
    me, sibling = (x, y, c), (x, y, 1 - c)
    chips = [(1 - x, y), (x, 1 - y), (1 - x, 1 - y)]

    def slot(t, px, py, pc):
        return out_refs[t].at[4 * px + 2 * py + pc]

    def copy(t, k, blk, to, src=None):
        return pltpu.make_async_remote_copy(
            src_ref=slot(t, *blk) if src is None else src, dst_ref=slot(t, *blk),
            send_sem=send_sems.at[t, k], recv_sem=recv_sems.at[t, k], device_id=to, device_id_type=_MESH)

    def own_copies():
        mine = [pltpu.make_async_copy(x_refs[t], slot(t, *me), local_sems.at[t]) for t in range(nt)]
        first = []
        for t in range(nt):
            first.append(copy(t, 0, me, sibling, src=x_refs[t]))
            first += [copy(t, 1 + j, me, (*chip, c), src=x_refs[t]) for j, chip in enumerate(chips)]
        return mine, first

    def start():
        mine, first = own_copies()
        for cp in mine + first:
            cp.start()

    def finish():
        mine, first = own_copies()
        passed = []
        for j, chip in enumerate(chips):
            for t in range(nt):
                copy(t, 1 + j, (*chip, c), me).wait_recv()
                fwd = copy(t, 4 + j, (*chip, c), sibling)
                fwd.start()
                passed.append(fwd)
        for t in range(nt):
            copy(t, 0, sibling, me).wait_recv()
        for j, chip in enumerate(chips):
            for t in range(nt):
                copy(t, 4 + j, (*chip, 1 - c), me).wait_recv()
        for cp in first + passed:
            cp.wait_send()
        for cp in mine:
            cp.wait()

    return start, finish


def _swap_with_sibling(packed):
    nt = len(packed)

    def body(*refs):
        p_refs, got_refs = refs[:nt], refs[nt:2 * nt]
        send_sems, recv_sems = refs[2 * nt:]
        x, y, c = lax.axis_index("x"), lax.axis_index("y"), lax.axis_index("c")
        copies = [
            pltpu.make_async_remote_copy(
                src_ref=p_refs[t].at[1 - c], dst_ref=got_refs[t], send_sem=send_sems.at[t], recv_sem=recv_sems.at[t],
                device_id=(x, y, 1 - c), device_id_type=_MESH)
            for t in range(nt)]
        for cp in copies:
            cp.start()
        for cp in copies:
            cp.wait_recv()
        for cp in copies:
            cp.wait_send()

    return pl.pallas_call(
        body, name="grad_swap_sibling", out_shape=[_sds(p.shape[1:], p.dtype) for p in packed], in_specs=[_ANY] * nt,
        out_specs=[_ANY] * nt,
        scratch_shapes=[pltpu.SemaphoreType.DMA((nt,)), pltpu.SemaphoreType.DMA((nt,))])(*packed)


def _chip_sum(packed, got):
    _, nchip, rows, cols = packed.shape
    tr = _row_tile(rows, cols, 2)
    core = lax.axis_index("c").astype(jnp.int32).reshape(1)

    def body(c_ref, p_ref, g_ref, o_ref):
        o_ref[...] = (p_ref[...].astype(F32) + g_ref[...].astype(F32)).astype(o_ref.dtype)

    grid_spec = pltpu.PrefetchScalarGridSpec(
        num_scalar_prefetch=1, grid=(nchip, rows // tr),
        in_specs=[pl.BlockSpec((None, None, tr, cols), lambda k, i, c: (c[0], k, i, 0)),
                  pl.BlockSpec((None, tr, cols), lambda k, i, c: (k, i, 0))],
        out_specs=pl.BlockSpec((None, tr, cols), lambda k, i, c: (k, i, 0)))
    return pl.pallas_call(
        body, name="grad_chip_sum", grid_spec=grid_spec, out_shape=_sds(got.shape, got.dtype),
        compiler_params=_params(("parallel", "parallel")))(core, packed, got)


def _exchange_chips(chip_sums):
    nt = len(chip_sums)

    def body(*refs):
        start, finish = _exchange_protocol(refs[:nt], refs[nt:2 * nt], *refs[2 * nt:])
        start()
        finish()

    return pl.pallas_call(
        body, name="grad_exchange_chips", out_shape=[_sds(s.shape, s.dtype) for s in chip_sums], in_specs=[_ANY] * nt,
        out_specs=[_ANY] * nt, scratch_shapes=_exchange_sems(nt))(*chip_sums)


def _exchange_sems(nt):
    return [pltpu.SemaphoreType.DMA((nt, 3)), pltpu.SemaphoreType.DMA((nt, 3)), pltpu.SemaphoreType.DMA((nt,))]


def _exchange_protocol(s_refs, got_refs, send_sems, recv_sems, local_sems):
    nt = len(s_refs)
    x, y, c = lax.axis_index("x"), lax.axis_index("y"), lax.axis_index("c")
    my_chip = 2 * x + y
    chips = [(1 - x, y), (x, 1 - y), (1 - x, 1 - y)]

    def copy(t, j, src_chip, dst_chip):
        px, py = chips[j]
        return pltpu.make_async_remote_copy(
            src_ref=s_refs[t].at[src_chip], dst_ref=got_refs[t].at[dst_chip], send_sem=send_sems.at[t, j],
            recv_sem=recv_sems.at[t, j], device_id=(px, py, c), device_id_type=_MESH)

    def own_copies():
        mine = [pltpu.make_async_copy(s_refs[t].at[my_chip], got_refs[t].at[my_chip], local_sems.at[t]) for t in range(nt)]
        sends = [copy(t, j, 2 * px + py, my_chip) for t in range(nt) for j, (px, py) in enumerate(chips)]
        return mine, sends

    def start():
        mine, sends = own_copies()
        for cp in mine + sends:
            cp.start()

    def finish():
        mine, sends = own_copies()
        for j, (px, py) in enumerate(chips):
            for t in range(nt):
                copy(t, j, my_chip, 2 * px + py).wait_recv()
        for cp in sends:
            cp.wait_send()
        for cp in mine:
            cp.wait()

    return start, finish


def _sum_chips(got):
    _, rows, cols = got.shape
    tr = _row_tile(rows, cols)

    def body(g_ref, o_ref):
        acc = g_ref[0].astype(F32)
        for k in range(1, 4):
            acc = acc + g_ref[k].astype(F32)
        o_ref[...] = acc

    return pl.pallas_call(
        body, name="grad_sum_chips", grid=(rows // tr,), in_specs=[pl.BlockSpec((4, tr, cols), lambda i: (0, i, 0))],
        out_specs=pl.BlockSpec((tr, cols), lambda i: (i, 0)), out_shape=_sds((rows, cols), F32),
        compiler_params=_params(("parallel",)))(got)


_COL_SHARDED = ("a_w_in", "b_w_in", "w_gate", "w_up")
_ROW_SHARDED = ("w_mem_kv", "w_out", "w_down")
_BIG = ("a_w_in", "b_w_in", "w_mem_kv", "w_out", "w_gate", "w_up", "w_down")
_SGU_LN = ("sgu_ln_g", "sgu_ln_b")
_LN4 = ("ln_mix_g", "ln_mix_b", "ln_ffn_g", "ln_ffn_b")
_REPLICATED = ("sgu_w_s", "sgu_b_s") + _LN4


def _unshard(name, gathered):
    if name in _COL_SHARDED or name in _SGU_LN:
        moved = jnp.moveaxis(gathered, 0, -2)
        return moved.reshape(moved.shape[:-2] + (moved.shape[-2] * moved.shape[-1],))
    moved = jnp.moveaxis(gathered, 0, 1)
    return moved.reshape((moved.shape[0], moved.shape[1] * moved.shape[2]) + moved.shape[3:])


_LAID_OUT_IN_KERNEL = _COL_SHARDED


def _after_gather(name, gathered):
    return gathered if name in _LAID_OUT_IN_KERNEL else _unshard(name, gathered)


def _by_shard(name, full):
    if name in _COL_SHARDED or name in _SGU_LN:
        split = full.reshape(full.shape[:-1] + (N_DEV, full.shape[-1] // N_DEV))
        return jnp.moveaxis(split, -2, 0)
    split = full.reshape((full.shape[0], N_DEV, full.shape[1] // N_DEV) + full.shape[2:])
    return jnp.moveaxis(split, 1, 0)


def _layer_keys(i):
    return [("a_w_in" if i % 2 == 0 else "b_w_in", i // 2)] + [(n, i) for n in ("w_mem_kv", "w_out", "w_gate", "w_up", "w_down")]


_GATHER_FIRST = _layer_keys(0)[:2]
_GATHER_LATER = (_layer_keys(0)[2:] + _layer_keys(1), _layer_keys(2)[:4], _layer_keys(2)[4:],
                 _layer_keys(3)[:4], _layer_keys(3)[4:])


def _shard_block(shards, key):
    name, idx = key
    return shards[name][idx:idx + 1].astype(BF16)


def _gather_first(shards):
    blocks = [_shard_block(shards, k) for k in _GATHER_FIRST] + [shards[n] for n in _SGU_LN]
    gathered = _all_gather("first_all_gather", blocks)
    full = {k: _after_gather(k[0], g) for k, g in zip(_GATHER_FIRST, gathered)}
    sgu_ln = {n: _unshard(n, g) for n, g in zip(_SGU_LN, gathered[len(_GATHER_FIRST):])}
    return full, sgu_ln


def _two_level(by_dest):
    shp = by_dest.shape[1:]
    split = by_dest.astype(BF16).reshape((4, 2) + shp).swapaxes(0, 1)
    return split.reshape(2, 4, int(np.prod(shp[:-1])), shp[-1])


_EARLY = _BIG + _SGU_LN


def _chip_sums_of_early(grads):
    packed = [_two_level(_by_shard(n, jnp.stack(grads[n][1:] if n == "a_w_in" else grads[n]))) for n in _EARLY]
    ln4 = jnp.stack([jnp.stack(grads[n]) for n in _LN4])
    rep = [jnp.stack(grads["sgu_w_s"]).reshape(N_DEV, -1, BLK), jnp.stack(grads["sgu_b_s"]).reshape(N_DEV, -1, BLK),
           ln4.reshape(N_DEV, -1, D_MODEL)]
    packed += [_two_level(r) for r in rep]
    got = _swap_with_sibling(packed)
    return [_chip_sum(p, g) for p, g in zip(packed, got)]


def _finish_replicated(parts, shapes):
    w_s, b_s, ln_all = _all_gather("replicated_grads_all_gather", [_sum_chips(p) for p in parts])
    ln_all = ln_all.reshape((len(_LN4),) + tuple(shapes[_LN4[0]]))
    rep_grads = {"sgu_w_s": w_s.reshape(shapes["sgu_w_s"]), "sgu_b_s": b_s.reshape(shapes["sgu_b_s"])}
    rep_grads.update({n: ln_all[i] for i, n in enumerate(_LN4)})
    return rep_grads


def _reduce_last(grad_a_first):
    packed = [_two_level(_by_shard("a_w_in", grad_a_first))]
    got = _swap_with_sibling(packed)
    return _exchange_chips([_chip_sum(packed[0], got[0])])[0]


def _as_2d(a):
    if a.ndim == 1:
        return a.reshape(1, -1)
    return a.reshape(-1, a.shape[-1])


def kernel(x, mem, a_w_in, b_w_in, sgu_ln_g, sgu_ln_b, sgu_w_s, sgu_b_s, w_mem_kv, w_out, ln_mix_g, ln_mix_b, w_gate, w_up, w_down, ln_ffn_g, ln_ffn_b, loss_target, m_a_w_in, m_b_w_in, m_sgu_ln_g, m_sgu_ln_b, m_sgu_w_s, m_sgu_b_s, m_w_mem_kv, m_w_out, m_ln_mix_g, m_ln_mix_b, m_w_gate, m_w_up, m_w_down, m_ln_ffn_g, m_ln_ffn_b, v_a_w_in, v_b_w_in, v_sgu_ln_g, v_sgu_ln_b, v_sgu_w_s, v_sgu_b_s, v_w_mem_kv, v_w_out, v_ln_mix_g, v_ln_mix_b, v_w_gate, v_w_up, v_w_down, v_ln_ffn_g, v_ln_ffn_b):
    names = ("a_w_in", "b_w_in", "sgu_ln_g", "sgu_ln_b", "sgu_w_s", "sgu_b_s", "w_mem_kv", "w_out", "ln_mix_g", "ln_mix_b",
             "w_gate", "w_up", "w_down", "ln_ffn_g", "ln_ffn_b")
    weights = dict(zip(names, (a_w_in, b_w_in, sgu_ln_g, sgu_ln_b, sgu_w_s, sgu_b_s, w_mem_kv, w_out, ln_mix_g, ln_mix_b,
                               w_gate, w_up, w_down, ln_ffn_g, ln_ffn_b)))
    mom_m = dict(zip(names, (m_a_w_in, m_b_w_in, m_sgu_ln_g, m_sgu_ln_b, m_sgu_w_s, m_sgu_b_s, m_w_mem_kv, m_w_out, m_ln_mix_g,
                             m_ln_mix_b, m_w_gate, m_w_up, m_w_down, m_ln_ffn_g, m_ln_ffn_b)))
    mom_v = dict(zip(names, (v_a_w_in, v_b_w_in, v_sgu_ln_g, v_sgu_ln_b, v_sgu_w_s, v_sgu_b_s, v_w_mem_kv, v_w_out, v_ln_mix_g,
                             v_ln_mix_b, v_w_gate, v_w_up, v_w_down, v_ln_ffn_g, v_ln_ffn_b)))
    full, sgu_ln = _gather_first(weights)
    pending = [(keys, [_shard_block(weights, k) for k in keys]) for keys in _GATHER_LATER]
    loss_part, grad_x, local, early = _local_step(
        x, mem, loss_target, full, sgu_ln, {n: weights[n] for n in _REPLICATED}, pending)
    loss = lax.psum(loss_part[0, 0], ("x", "y", "c"))
    early_parts = dict(zip(_EARLY, early))
    rep_grads = _finish_replicated(early[len(_EARLY):], {n: weights[n].shape for n in _REPLICATED})
    a_first_parts = _reduce_last(local["a_w_in"][:1])

    reduced, deltas, new_m, new_v = {}, {}, {}, {}
    for n in names:
        w2, m2, v2 = _as_2d(weights[n]), _as_2d(mom_m[n]), _as_2d(mom_v[n])
        if n in early_parts:
            outs = _adamw(w2, m2, v2, parts=early_parts[n], first_parts=a_first_parts if n == "a_w_in" else None)
        else:
            outs = _adamw(w2, m2, v2, grad=_as_2d(rep_grads[n]))
        reduced[n], deltas[n], new_m[n], new_v[n] = (a.reshape(weights[n].shape) for a in outs)

    return (loss, grad_x, *[reduced[n] for n in names], *[deltas[n] for n in names],
            *[new_m[n] for n in names], *[new_v[n] for n in names])


def _local_step(x, mem, loss_target, full, sgu_ln, small, pending=None):
    sgu_w_s, sgu_b_s = small["sgu_w_s"], small["sgu_b_s"]
    ln_mix_g, ln_mix_b, ln_ffn_g, ln_ffn_b = (small[n] for n in ("ln_mix_g", "ln_mix_b", "ln_ffn_g", "ln_ffn_b"))
    bsz, seq, _ = x.shape
    tokens = bsz * seq
    slopes = _alibi_table()
    full = dict(full)
    exchanging = pending is not None
    pending = list(pending or [])

    def weight(name, idx):
        return full[(name, idx)][0]

    def next_group():
        return pending[0][1] if pending else None

    def landed(gathered):
        if gathered:
            keys, _ = pending.pop(0)
            full.update({k: _after_gather(k[0], g) for k, g in zip(keys, gathered)})

    res = (x.reshape(tokens, D_MODEL),)
    xb = res[0].astype(BF16)
    memb = mem.reshape(bsz * N_MEM, D_MODEL).astype(BF16)
    tgt = loss_target.reshape(tokens, D_MODEL)

    saved = []
    for i in range(DEPTH):
        j = i // 2
        dil_layer = i % 2 == 0
        mkv = _linear_nn("mem_kv", memb, weight("w_mem_kv", i))
        h, w_in = _linear_nn_gathered("in_proj_a" if dil_layer else "in_proj_b", xb,
                                      full[("a_w_in" if dil_layer else "b_w_in", j)][:, 0])
        st = dict(xb=xb, h=h, mkv=mkv, w_in=w_in)
        if dil_layer:
            mix, st["lse"], gathered = _band_attn_fwd_fused(h, slopes, bsz, seq, gather=next_group() if i == 0 else None)
            landed(gathered)
            q_col = 3 * MIX_W // MEM_W
        else:
            st["ws"] = sgu_w_s[j]
            st["bs_t"] = sgu_b_s[j].T
            st["ln_g"] = sgu_ln["sgu_ln_g"][j].reshape(1, MIX_W)
            st["ln_b"] = sgu_ln["sgu_ln_b"][j].reshape(1, MIX_W)
            mix = _sgu_fwd(h, st["ws"], st["bs_t"], st["ln_g"], st["ln_b"])
            q_col = 2 * MIX_W // MEM_W
        mo = _mem_attn_fwd(h, mkv, bsz, seq, q_col)
        w_out, w_down = weight("w_out", i), weight("w_down", i)
        mix_ln = (ln_mix_g[i].reshape(1, D_MODEL), ln_mix_b[i].reshape(1, D_MODEL))
        ffn_ln = (ln_ffn_g[i].reshape(1, D_MODEL), ln_ffn_b[i].reshape(1, D_MODEL))
        r1, x1b, _ = _proj_ln_fwd("out_proj_ln", [mix, mo], w_out, res, *mix_ln)
        gt, up, act, w_gu, gathered = _ffn_up_fwd(x1b, full[("w_gate", i)][:, 0], full[("w_up", i)][:, 0],
                                                  gather=next_group() if i < 2 else None)
        landed(gathered)
        r2, xb, gathered = _proj_ln_fwd("ffn_down_ln", [act], w_down, (r1, *mix_ln), *ffn_ln,
                                        gather=next_group() if i < 2 else None)
        landed(gathered)
        res = (r2, *ffn_ln)
        st.update(mix=mix, mo=mo, q_col=q_col, r1=r1, x1b=x1b, gt=gt, up=up, act=act, r2=r2,
                  w_out=w_out, w_down=w_down, w_gu=w_gu)
        saved.append(st)

    dr2, dr2b, dg, db, loss_part = _loss_ln_bwd(*res, tgt)

    early_parts = None
    per_pair = ("a_w_in", "b_w_in", "sgu_ln_g", "sgu_ln_b", "sgu_w_s", "sgu_b_s")
    grads = {n: [None] * (DEPTH // 2 if n in per_pair else DEPTH) for n in _BIG + _SGU_LN + _REPLICATED}
    for i in reversed(range(DEPTH)):
        j = i // 2
        st = saved[i]
        dil_layer = i % 2 == 0
        w_in = st["w_in"]
        grads["ln_ffn_g"][i], grads["ln_ffn_b"][i] = dg[0], db[0]
        dgu = _ffn_down_bwd(dr2b, st["w_down"], st["gt"], st["up"])
        grads["w_down"][i] = _mm_tn("grad_w_down", st["act"], dr2b)
        dr1, dr1b, dg, db = _linear_nt("ffn_up_bwd", [dgu], st["w_gu"], dr2, F32,
                                       ln=(st["r1"], ln_mix_g[i].reshape(1, D_MODEL)))
        grads["ln_mix_g"][i], grads["ln_mix_b"][i] = dg[0], db[0]
        dw_gu = _mm_tn("grad_w_gate_up", st["x1b"], dgu)
        grads["w_gate"][i], grads["w_up"][i] = dw_gu[:, :D_FF], dw_gu[:, D_FF:]
        dcat = _linear_nt("out_proj_bwd", [dr1b], st["w_out"], None, BF16)
        grads["w_out"][i] = jnp.concatenate(
            [_mm_tn("grad_w_out_mix", st["mix"], dr1b), _mm_tn("grad_w_out_mem", st["mo"], dr1b)], axis=0)
        dqm, dmkv = _mem_attn_bwd(st["h"], st["mkv"], dcat, bsz, seq, st["q_col"])
        grads["w_mem_kv"][i] = _mm_tn("grad_w_mem_kv", memb, dmkv.astype(BF16))
        if dil_layer:
            early_sums = _chip_sums_of_early(grads) if (i == 0 and exchanging) else None
            dh_parts, exchanged = _band_attn_bwd_fused(st["h"], dcat, st["mix"], st["lse"], slopes, bsz, seq,
                                                       exchange=early_sums)
            if early_sums is not None:
                early_parts = exchanged
        else:
            ws_t = jnp.swapaxes(st["ws"], -1, -2)
            dh_main, dws, dbs_t, dlg, dlb = _sgu_bwd(st["h"], dcat, st["ws"], ws_t, st["bs_t"], st["ln_g"], st["ln_b"])
            grads["sgu_w_s"][j], grads["sgu_b_s"][j] = dws, dbs_t.T
            grads["sgu_ln_g"][j], grads["sgu_ln_b"][j] = dlg[0], dlb[0]
            dh_parts = [dh_main]
        name = "in_proj_bwd_a" if dil_layer else "in_proj_bwd_b"
        if i > 0:
            dr2, dr2b, dg, db = _linear_nt(name, [*dh_parts, dqm], w_in, dr1, F32,
                                           ln=(saved[i - 1]["r2"], ln_ffn_g[i - 1].reshape(1, D_MODEL)))
        else:
            grad_x = _linear_nt(name + "_x", [*dh_parts, dqm], w_in, dr1, F32).reshape(x.shape)
        grads["a_w_in" if dil_layer else "b_w_in"][j] = jnp.concatenate(
            [_mm_tn("grad_w_in_part", st["xb"], part) for part in dh_parts] + [_mm_tn("grad_w_in_qm", st["xb"], dqm)], axis=1)
    return loss_part, grad_x, {n: jnp.stack(g) for n, g in grads.items()}, early_parts
```

```python
import functools
import math

import numpy as np
import jax
import jax.numpy as jnp
from jax import lax
from jax.experimental import pallas as pl
from jax.experimental.pallas import tpu as pltpu

F32 = jnp.float32
BF16 = jnp.bfloat16

D_MODEL = 1024
DEPTH = 4
N_MEM = 256
HEAD_DIM = 64
N_HEADS = 12
MIX_W = N_HEADS * HEAD_DIM
MEM_W = 4 * HEAD_DIM
DIL_PATTERNS = ((128, 1), (512, 4), (2048, 16))
BLK = 128
HEAD_GROUP = 4
N_GROUPS = N_HEADS // HEAD_GROUP
D_FF = 2816
FF_CHUNKS = 2
ALPHA = (2 * DEPTH) ** 0.25
LN_EPS = 1e-5
SCALE = HEAD_DIM ** -0.5
NEG = -1e30
N_DEV = 8

ADAM_LR, ADAM_B1, ADAM_B2, ADAM_EPS, ADAM_WD, ADAM_STEP = 0.001, 0.9, 0.999, 1e-08, 0.01, 10

VMEM_LIMIT = 56 * 2 ** 20
STAT_LANES = 32
STAT_W = N_HEADS * STAT_LANES


def _dot_nn(a, b):
    return lax.dot_general(a, b, (((1,), (0,)), ((), ())), preferred_element_type=F32)


def _dot_nt(a, b):
    return lax.dot_general(a, b, (((1,), (1,)), ((), ())), preferred_element_type=F32)


def _dot_tn(a, b):
    return lax.dot_general(a, b, (((0,), (0,)), ((), ())), preferred_element_type=F32)


def _ln_hat(r):
    mu = jnp.mean(r, axis=-1, keepdims=True)
    xc = r - mu
    var = jnp.mean(xc * xc, axis=-1, keepdims=True)
    rstd = lax.rsqrt(var + LN_EPS)
    return xc * rstd, rstd


def _params(sem):
    return pltpu.CompilerParams(dimension_semantics=sem, vmem_limit_bytes=VMEM_LIMIT)


def _rows(tm, c, col=0):
    return pl.BlockSpec((tm, c), lambda i: (i, col))


def _whole(shape):
    nd = len(shape)
    return pl.BlockSpec(tuple(shape), lambda *_: (0,) * nd)


def _resident(shape):
    nd = len(shape)
    return pl.BlockSpec(tuple(shape), lambda *_: (0,) * nd, pipeline_mode=pl.Buffered(1))


def _sds(shape, dtype):
    return jax.ShapeDtypeStruct(tuple(shape), dtype)


def _linear_nn(name, a, w, tm=512):
    t, k = a.shape
    n = w.shape[1]
    tm = min(tm, t)

    def body(a_ref, w_ref, o_ref):
        o_ref[...] = _dot_nn(a_ref[...], w_ref[...]).astype(BF16)

    return pl.pallas_call(
        body, name=name, grid=(t // tm,), in_specs=[_rows(tm, k), _resident(w.shape)], out_specs=_rows(tm, n),
        out_shape=_sds((t, n), BF16), compiler_params=_params(("parallel",)))(a, w)


def _linear_nn_gathered(name, a, shards, tm=512):
    t, k = a.shape
    n8 = shards.shape[2]
    n = N_DEV * n8

    def body(a_ref, s_ref, o_ref, w_ref):
        @pl.when(pl.program_id(0) == 0)
        def _():
            for s in range(N_DEV):
                w_ref[:, s * n8:(s + 1) * n8] = s_ref[s]

        o_ref[...] = _dot_nn(a_ref[...], w_ref[...]).astype(BF16)

    return pl.pallas_call(
        body, name=name, grid=(t // tm,), in_specs=[_rows(tm, k), _resident(shards.shape)],
        out_specs=[_rows(tm, n), _whole((k, n))], out_shape=[_sds((t, n), BF16), _sds((k, n), BF16)],
        compiler_params=_params(("arbitrary",)))(a, shards)


def _proj_ln_fwd(name, lhs, w, res, g, b, gather=None, tm=512):
    t = res[0].shape[0]
    n_lhs = len(lhs)
    n_res = len(res)
    ng = 0 if gather is None else len(gather)
    steps = t // tm
    n_in = n_lhs + 3 + n_res

    def body(*refs):
        lhs_refs = refs[:n_lhs]
        w_ref = refs[n_lhs]
        res_refs = refs[n_lhs + 1:n_lhs + 1 + n_res]
        g_ref, b_ref = refs[n_in - 2:n_in]
        r_ref, xnb_ref = refs[n_in + ng:n_in + ng + 2]
        if ng:
            start, finish = _gather_protocol(refs[n_in:n_in + ng], refs[n_in + ng + 2:n_in + 2 * ng + 2], *refs[n_in + 2 * ng + 2:])
            pl.when(pl.program_id(0) == 0)(start)
        y, off = None, 0
        for lr in lhs_refs:
            k = lr.shape[1]
            term = _dot_nn(lr[...], w_ref[off:off + k, :])
            y = term if y is None else y + term
            off += k
        x_res = res_refs[0][...]
        if n_res == 3:
            x_res = _ln_hat(x_res)[0] * res_refs[1][...] + res_refs[2][...]
        r = ALPHA * x_res + y
        r_ref[...] = r
        xnb_ref[...] = (_ln_hat(r)[0] * g_ref[...] + b_ref[...]).astype(BF16)
        if ng:
            pl.when(pl.program_id(0) == steps - 1)(finish)

    vec = _whole((1, D_MODEL))
    in_specs = ([_rows(tm, a.shape[1]) for a in lhs] + [_resident(w.shape), _rows(tm, D_MODEL)] + [vec] * (n_res - 1) + [vec, vec])
    outs = pl.pallas_call(
        body, name=name + "_gather" if ng else name, grid=(steps,), in_specs=in_specs + [_ANY] * ng,
        out_specs=[_rows(tm, D_MODEL)] * 2 + [_ANY] * ng,
        out_shape=[_sds((t, D_MODEL), F32), _sds((t, D_MODEL), BF16)] + (_gather_shapes(gather) if ng else []),
        scratch_shapes=_gather_sems(ng) if ng else [],
        compiler_params=_params(("arbitrary" if ng else "parallel",)))(*lhs, w, *res, g, b, *(gather or []))
    return outs[0], outs[1], list(outs[2:])


def _ffn_up_fwd(xb, gate_shards, up_shards, gather=None, tm=256):
    t = xb.shape[0]
    ng = 0 if gather is None else len(gather)
    steps = t // tm
    n8 = gate_shards.shape[2]

    def body(*refs):
        x_ref, gs_ref, us_ref = refs[:3]
        g_ref, u_ref, a_ref, wgu_ref = refs[3 + ng:7 + ng]
        w_ref, w_sem = refs[7 + 2 * ng:9 + 2 * ng]
        keep = pltpu.make_async_copy(w_ref, wgu_ref, w_sem)

        @pl.when(pl.program_id(0) == 0)
        def _():
            for s in range(N_DEV):
                w_ref[:, s * n8:(s + 1) * n8] = gs_ref[s]
                w_ref[:, D_FF + s * n8:D_FF + (s + 1) * n8] = us_ref[s]
            keep.start()

        if ng:
            start, finish = _gather_protocol(refs[3:3 + ng], refs[7 + ng:7 + 2 * ng], *refs[9 + 2 * ng:])
            pl.when(pl.program_id(0) == 0)(start)
        xv = x_ref[...]
        for c in range(FF_CHUNKS):
            cols = slice(c * D_FF // FF_CHUNKS, (c + 1) * D_FF // FF_CHUNKS)
            gt = _dot_nn(xv, w_ref[:, cols])
            up = _dot_nn(xv, w_ref[:, D_FF + cols.start:D_FF + cols.stop])
            sg = jax.nn.sigmoid(gt)
            silu = gt * sg
            g_ref[:, cols] = (up * (sg * (1.0 + gt * (1.0 - sg)))).astype(BF16)
            u_ref[:, cols] = silu.astype(BF16)
            a_ref[:, cols] = (silu * up).astype(BF16)
        if ng:
            pl.when(pl.program_id(0) == steps - 1)(finish)
        pl.when(pl.program_id(0) == steps - 1)(keep.wait)

    k = gate_shards.shape[1]
    outs = pl.pallas_call(
        body, name="ffn_up_fwd_gather" if ng else "ffn_up_fwd", grid=(steps,),
        in_specs=[_rows(tm, D_MODEL), _resident(gate_shards.shape), _resident(up_shards.shape)] + [_ANY] * ng,
        out_specs=[_rows(tm, D_FF)] * 3 + [_ANY] + [_ANY] * ng,
        out_shape=[_sds((t, D_FF), BF16)] * 3 + [_sds((k, 2 * D_FF), BF16)] + (_gather_shapes(gather) if ng else []),
        scratch_shapes=[pltpu.VMEM((k, 2 * D_FF), BF16), pltpu.SemaphoreType.DMA(())] + (_gather_sems(ng) if ng else []),
        compiler_params=_params(("arbitrary",)))(xb, gate_shards, up_shards, *(gather or []))
    return outs[0], outs[1], outs[2], outs[3], list(outs[4:])


def _ln_bwd_rows(dxn, xhat, rstd, g_ref, dr_ref, drb_ref, dg_ref, db_ref):
    @pl.when(pl.program_id(0) == 0)
    def _():
        dg_ref[...] = jnp.zeros_like(dg_ref)
        db_ref[...] = jnp.zeros_like(db_ref)

    dxh = dxn * g_ref[...]
    m1 = jnp.mean(dxh, axis=-1, keepdims=True)
    m2 = jnp.mean(dxh * xhat, axis=-1, keepdims=True)
    dr = rstd * (dxh - m1 - xhat * m2)
    dr_ref[...] = dr
    drb_ref[...] = dr.astype(BF16)
    dg_ref[...] += jnp.sum(dxn * xhat, axis=0, keepdims=True)
    db_ref[...] += jnp.sum(dxn, axis=0, keepdims=True)


def _ln_bwd_outs(t, tm):
    vec = _whole((1, D_MODEL))
    specs = [_rows(tm, D_MODEL), _rows(tm, D_MODEL), vec, vec]
    shapes = [_sds((t, D_MODEL), F32), _sds((t, D_MODEL), BF16), _sds((1, D_MODEL), F32), _sds((1, D_MODEL), F32)]
    return specs, shapes


def _loss_ln_bwd(r, g, b, tgt, tm=512):
    t = r.shape[0]

    def body(r_ref, g_ref, b_ref, t_ref, dr_ref, drb_ref, dg_ref, db_ref, l_ref):
        @pl.when(pl.program_id(0) == 0)
        def _():
            l_ref[...] = jnp.zeros_like(l_ref)

        xhat, rstd = _ln_hat(r_ref[...])
        e = xhat * g_ref[...] + b_ref[...] - t_ref[...]
        l_ref[...] += jnp.sum(e * e) * (0.5 / D_MODEL)
        _ln_bwd_rows(e * (1.0 / D_MODEL), xhat, rstd, g_ref, dr_ref, drb_ref, dg_ref, db_ref)

    vec = _whole((1, D_MODEL))
    specs, shapes = _ln_bwd_outs(t, tm)
    return pl.pallas_call(
        body, name="loss_ln_bwd", grid=(t // tm,), in_specs=[_rows(tm, D_MODEL), vec, vec, _rows(tm, D_MODEL)],
        out_specs=specs + [_whole((1, 128))], out_shape=shapes + [_sds((1, 128), F32)],
        compiler_params=_params(("arbitrary",)))(r, g, b, tgt)


def _ffn_down_bwd(drb, wd, gt, up, tm=512):
    t = drb.shape[0]

    def body(d_ref, w_ref, g_ref, u_ref, o_ref):
        dv = d_ref[...]
        for c in range(FF_CHUNKS):
            cols = slice(c * D_FF // FF_CHUNKS, (c + 1) * D_FF // FF_CHUNKS)
            da = _dot_nt(dv, w_ref[cols, :])
            o_ref[:, cols] = (da * g_ref[:, cols].astype(F32)).astype(BF16)
            o_ref[:, D_FF + cols.start:D_FF + cols.stop] = (da * u_ref[:, cols].astype(F32)).astype(BF16)

    return pl.pallas_call(
        body, name="ffn_down_bwd", grid=(t // tm,),
        in_specs=[_rows(tm, D_MODEL), _resident(wd.shape), _rows(tm, D_FF), _rows(tm, D_FF)],
        out_specs=_rows(tm, 2 * D_FF), out_shape=_sds((t, 2 * D_FF), BF16),
        compiler_params=_params(("parallel",)))(drb, wd, gt, up)


def _linear_nt(name, lhs, w, res, out_dtype, ln=None, tm=512):
    t = lhs[0].shape[0]
    n_lhs = len(lhs)
    n_out = w.shape[0]
    n_in = n_lhs + 1 + (res is not None) + (2 if ln else 0)

    def body(*refs):
        lhs_refs = refs[:n_lhs]
        w_ref = refs[n_lhs]
        y, off = None, 0
        for lr in lhs_refs:
            k = lr.shape[1]
            term = _dot_nt(lr[...], w_ref[:, off:off + k])
            y = term if y is None else y + term
            off += k
        if res is not None:
            y = ALPHA * refs[n_lhs + 1][...] + y
        if ln is None:
            refs[-1][...] = y.astype(out_dtype)
        else:
            r_ref, g_ref = refs[n_in - 2:n_in]
            xhat, rstd = _ln_hat(r_ref[...])
            _ln_bwd_rows(y, xhat, rstd, g_ref, *refs[n_in:])

    in_specs = [_rows(tm, a.shape[1]) for a in lhs] + [_resident(w.shape)]
    args = list(lhs) + [w]
    if res is not None:
        in_specs.append(_rows(tm, n_out))
        args.append(res)
    if ln is None:
        out_specs, out_shape, sem = _rows(tm, n_out), _sds((t, n_out), out_dtype), "parallel"
    else:
        in_specs += [_rows(tm, D_MODEL), _whole((1, D_MODEL))]
        args += list(ln)
        (out_specs, out_shape), sem = _ln_bwd_outs(t, tm), "arbitrary"
    return pl.pallas_call(
        body, name=name, grid=(t // tm,), in_specs=in_specs, out_specs=out_specs, out_shape=out_shape,
        compiler_params=_params((sem,)))(*args)


def _pick_tile(n, limit):
    if n <= limit:
        return n
    best = 128
    for cand in range(128, limit + 1, 128):
        if n % cand == 0:
            best = cand
    return best


def _mm_tn(name, a, b, tt=1024):
    t, k = a.shape
    n = b.shape[1]
    tt = min(tt, t)
    tk = _pick_tile(k, 1408)
    tn = _pick_tile(n, (6 * 2 ** 20) // (4 * tk) // 128 * 128)
    steps = t // tt

    def body(a_ref, b_ref, o_ref, acc_ref):
        @pl.when(pl.program_id(2) == 0)
        def _():
            acc_ref[...] = jnp.zeros_like(acc_ref)

        acc_ref[...] += _dot_tn(a_ref[...], b_ref[...])

        @pl.when(pl.program_id(2) == steps - 1)
        def _():
            o_ref[...] = acc_ref[...].astype(BF16)

    return pl.pallas_call(
        body, name=name, grid=(k // tk, n // tn, steps),
        in_specs=[pl.BlockSpec((tt, tk), lambda i, j, s: (s, i)), pl.BlockSpec((tt, tn), lambda i, j, s: (s, j))],
        out_specs=pl.BlockSpec((tk, tn), lambda i, j, s: (i, j)), out_shape=_sds((k, n), BF16),
        scratch_shapes=[pltpu.VMEM((tk, tn), F32)],
        compiler_params=_params(("parallel", "parallel", "arbitrary")))(a, b)


def _mm_tn_parts(name, a, parts, tt=1024):
    t, k = a.shape
    widths = [p.shape[1] for p in parts]
    n_parts = len(parts)
    steps = t // tt

    def body(*refs):
        a_ref, b_refs, o_ref, acc_ref = refs[0], refs[1:1 + n_parts], refs[1 + n_parts], refs[2 + n_parts]

        @pl.when(pl.program_id(0) == 0)
        def _():
            acc_ref[...] = jnp.zeros_like(acc_ref)

        av, off = a_ref[...], 0
        for b_ref, width in zip(b_refs, widths):
            acc_ref[:, off:off + width] += _dot_tn(av, b_ref[...])
            off += width

        @pl.when(pl.program_id(0) == steps - 1)
        def _():
            o_ref[...] = acc_ref[...].astype(BF16)

    return pl.pallas_call(
        body, name=name, grid=(steps,), in_specs=[_rows(tt, k)] + [_rows(tt, w) for w in widths],
        out_specs=_whole((k, sum(widths))), out_shape=_sds((k, sum(widths)), BF16),
        scratch_shapes=[pltpu.VMEM((k, sum(widths)), F32)],
        compiler_params=_params(("arbitrary",)))(a, *parts)


def _alibi_table():
    arr = np.zeros((N_GROUPS, 8, 128), np.float32)
    for g in range(N_GROUPS):
        for hh in range(HEAD_GROUP):
            arr[g, hh, :] = 2.0 ** (-8.0 * (g * HEAD_GROUP + hh + 1) / N_HEADS)
    return jnp.asarray(arr)


def _spread_stats(cols, per_head=STAT_LANES):
    lane = lax.broadcasted_iota(jnp.int32, (BLK, HEAD_GROUP * per_head), 1)
    tile = cols[HEAD_GROUP - 1]
    for hh in range(HEAD_GROUP - 2, -1, -1):
        tile = jnp.where(lane < (hh + 1) * per_head, cols[hh], tile)
    return tile


def _block_mask(has_prev, dil):
    if has_prev is None:
        steps = lax.broadcasted_iota(jnp.int32, (BLK, BLK), 0) - lax.broadcasted_iota(jnp.int32, (BLK, BLK), 1)
        return steps >= 0, (steps * dil).astype(F32)
    qi = lax.broadcasted_iota(jnp.int32, (BLK, 2 * BLK), 0)
    ki = lax.broadcasted_iota(jnp.int32, (BLK, 2 * BLK), 1)
    steps = qi + BLK - ki
    valid = (steps >= 0) & (steps <= BLK) & ((ki >= BLK) | has_prev)
    return valid, (steps * dil).astype(F32)


def _bias_scratch():
    return pltpu.VMEM((2, HEAD_GROUP, BLK, 2 * BLK), F32)


def _fill_bias(bias, sl_ref, dil):
    for p in range(2):
        valid, dist = _block_mask(p == 1, dil)
        for hh in range(HEAD_GROUP):
            bias[p, hh] = jnp.where(valid, -sl_ref[hh:hh + 1, 0:1] * dist, NEG)


def _rows_of(j):
    return pl.ds(pl.multiple_of(j * BLK, BLK), BLK)


def _lane_half(hf):
    return slice(hf * 128, (hf + 1) * 128)


def _split_pair(x):
    first = lax.broadcasted_iota(jnp.int32, (1, 2 * HEAD_DIM), 1) < HEAD_DIM
    zero = jnp.zeros_like(x)
    return jnp.where(first, x, zero), jnp.where(first, zero, x)


def _deinterleave(src, dst, seq, dil, dtype):
    length = seq // dil
    for r in range(dil):
        for c in range(length // BLK):
            rows = pl.ds(r + c * BLK * dil, BLK, stride=dil)
            out = slice(r * length + c * BLK, r * length + (c + 1) * BLK)
            if len(src.shape) == 2:
                dst[out, :] = src[rows, :].astype(dtype)
            else:
                for hf in range(2):
                    dst[out, _lane_half(hf)] = src.at[hf][rows, :].astype(dtype)


def _interleave(src, dst, seq, dil, accumulate):
    length = seq // dil
    for r in range(dil):
        for c in range(length // BLK):
            rows = pl.ds(r + c * BLK * dil, BLK, stride=dil)
            inp = slice(r * length + c * BLK, r * length + (c + 1) * BLK)
            if len(dst.shape) == 2:
                dst[rows, :] = dst[rows, :] + src[inp, :] if accumulate else src[inp, :]
            else:
                for hf in range(2):
                    val = src[inp, _lane_half(hf)]
                    half = dst.at[hf]
                    half[rows, :] = half[rows, :] + val if accumulate else val


def _split_halves(src, dst, seq):
    def step(i, carry):
        for hf in range(2):
            dst[hf, _rows_of(i), :] = src[_rows_of(i), _lane_half(hf)].astype(F32)
        return carry

    lax.fori_loop(0, seq // BLK, step, 0)


def _band_attn_fwd_fused(h, slopes, bsz, seq, gather=None):
    width = h.shape[1]
    cb = width // 256
    k_off, v_off = MIX_W // 256, 2 * MIX_W // 256
    nb = seq // BLK

    ng = 0 if gather is None else len(gather)

    def body(*refs):
        sl_ref, q_ref, k_ref, v_ref = refs[:4]
        mix_ref, lse_ref = refs[4 + ng:6 + ng]
        qf, kf, vf, qd, kd, vd, od, ld, o1, o2, o3, l1, l2, l3, bias = refs[6 + 2 * ng:21 + 2 * ng]
        if ng:
            start, finish = _gather_protocol(refs[4:4 + ng], refs[6 + ng:6 + 2 * ng], *refs[21 + 2 * ng:])
            pl.when((pl.program_id(0) == 0) & (pl.program_id(1) == 0))(start)

        def run(dil, qs, ks, vs, o_dst, l_dst):
            nblk = seq // dil // BLK
            _fill_bias(bias, sl_ref, dil)

            def block(j, carry):
                rows, prows = _rows_of(j), _rows_of(jnp.maximum(j - 1, 0))
                has_prev = ((j % nblk) != 0).astype(jnp.int32)

                def keys(ref, lanes):
                    return jnp.concatenate([ref[prows, lanes], ref[rows, lanes]], axis=0)

                lses = []
                for pr in range(HEAD_GROUP // 2):
                    lanes = _lane_half(pr)
                    q_ab = _split_pair(qs[rows, lanes] * SCALE)
                    k2 = keys(ks, lanes)
                    v_ab = _split_pair(keys(vs, lanes))
                    out = None
                    for ab in range(2):
                        hh = 2 * pr + ab
                        s = _dot_nt(q_ab[ab], k2) + bias[has_prev, hh]
                        m = jnp.max(s, axis=-1, keepdims=True)
                        p = jnp.exp(s - m)
                        l = jnp.sum(p, axis=-1, keepdims=True)
                        term = _dot_nn(p.astype(BF16), v_ab[ab]) / l
                        out = term if out is None else out + term
                        lses.append(m + jnp.log(l))
                    o_dst[rows, lanes] = out
                l_dst[rows, :] = _spread_stats(lses, HEAD_DIM)
                return carry

            lax.fori_loop(0, nb, block, 0, unroll=8)

        run(1, q_ref, k_ref, v_ref, o1, l1)
        _split_halves(q_ref, qf, seq)
        _split_halves(k_ref, kf, seq)
        _split_halves(v_ref, vf, seq)
        for dil, o_tok, l_tok in ((4, o2, l2), (16, o3, l3)):
            _deinterleave(qf, qd, seq, dil, BF16)
            _deinterleave(kf, kd, seq, dil, BF16)
            _deinterleave(vf, vd, seq, dil, BF16)
            run(dil, qd, kd, vd, od, ld)
            _interleave(od, o_tok, seq, dil, False)
            _interleave(ld, l_tok, seq, dil, False)

        def merge(i, carry):
            rows = _rows_of(i)

            def both(ref):
                return jnp.concatenate([ref[0, rows, :], ref[1, rows, :]], axis=1)

            ls = [l1[rows, :], both(l2), both(l3)]
            m = jnp.maximum(jnp.maximum(ls[0], ls[1]), ls[2])
            tot = m + jnp.log(jnp.exp(ls[0] - m) + jnp.exp(ls[1] - m) + jnp.exp(ls[2] - m))
            ws = [jnp.exp(x - tot) for x in ls]
            mix_ref[rows, :] = (ws[0] * o1[rows, :] + ws[1] * both(o2) + ws[2] * both(o3)).astype(BF16)
            lse_ref[rows, :] = _spread_stats([tot[:, hh * HEAD_DIM:hh * HEAD_DIM + 1] for hh in range(HEAD_GROUP)])
            return carry

        lax.fori_loop(0, nb, merge, 0, unroll=2)
        if ng:
            pl.when((pl.program_id(0) == bsz - 1) & (pl.program_id(1) == N_GROUPS - 1))(finish)

    def hspec(off):
        return pl.BlockSpec((seq, 256), lambda b, g: (b, off + g))

    big = lambda dt: pltpu.VMEM((seq, 256), dt)
    halves = lambda: pltpu.VMEM((2, seq, 128), F32)
    stat = lambda: pltpu.VMEM((seq, 128), F32)
    outs = pl.pallas_call(
        body, name="band_attn_fwd_gather" if ng else "band_attn_fwd", grid=(bsz, N_GROUPS),
        in_specs=[pl.BlockSpec((None, 8, 128), lambda b, g: (g, 0, 0)), hspec(0), hspec(k_off), hspec(v_off)] + [_ANY] * ng,
        out_specs=[pl.BlockSpec((seq, 256), lambda b, g: (b, g)), pl.BlockSpec((seq, 128), lambda b, g: (b, g))] + [_ANY] * ng,
        out_shape=[_sds((bsz * seq, MIX_W), BF16), _sds((bsz * seq, STAT_W), F32)] + (_gather_shapes(gather) if ng else []),
        scratch_shapes=[halves(), halves(), halves(), big(BF16), big(BF16), big(BF16), big(F32), big(F32),
                        big(F32), halves(), halves(), big(F32), halves(), halves(), _bias_scratch()]
        + (_gather_sems(ng) if ng else []),
        compiler_params=_params(("arbitrary", "arbitrary")))(slopes, h, h, h, *(gather or []))
    return outs[0], outs[1], list(outs[2:])


def _band_attn_bwd_fused(h, dcat, mix, lse, slopes, bsz, seq, exchange=None):
    width = h.shape[1]
    k_off, v_off = MIX_W // 256, 2 * MIX_W // 256
    nb = seq // BLK

    ne = 0 if exchange is None else len(exchange)

    def body(*refs):
        sl_ref, q_ref, k_ref, v_ref, do_ref, o_ref, lse_ref = refs[:7]
        dq_ref, dk_ref, dv_ref = refs[7 + ne:10 + ne]
        qf, kf, vf, dof, ddt, qd, kd, vd, dod, lsd, ddd, gq, gk, gv, aq, ak, av, bias = refs[10 + 2 * ne:28 + 2 * ne]
        if ne:
            start, finish = _exchange_protocol(refs[7:7 + ne], refs[10 + ne:10 + 2 * ne], *refs[28 + 2 * ne:])
            pl.when((pl.program_id(0) == 0) & (pl.program_id(1) == 0))(start)

        same_head = (lax.broadcasted_iota(jnp.int32, (HEAD_GROUP * HEAD_DIM, HEAD_GROUP * STAT_LANES), 0) // HEAD_DIM
                     == lax.broadcasted_iota(jnp.int32, (HEAD_GROUP * HEAD_DIM, HEAD_GROUP * STAT_LANES), 1) // STAT_LANES)
        ones_map = jnp.where(same_head, 1.0, 0.0).astype(BF16)

        def delta(i, carry):
            rows = _rows_of(i)
            prod = do_ref[rows, :].astype(F32) * o_ref[rows, :].astype(F32)
            high = prod.astype(BF16)
            rest = (prod - high.astype(F32)).astype(BF16)
            ddt[rows, :] = _dot_nn(high, ones_map) + _dot_nn(rest, ones_map)
            return carry

        lax.fori_loop(0, nb, delta, 0, unroll=2)

        def zero(i, carry):
            rows = _rows_of(i)
            for ref in (gk, gv):
                ref[rows, :] = jnp.zeros((BLK, 256), F32)
            return carry

        def run(dil, qs, ks, vs, dos, lss, dds):
            nblk = seq // dil // BLK
            _fill_bias(bias, sl_ref, dil)
            if nblk > 1:
                lax.fori_loop(0, nb, zero, 0)

            def block(j, carry):
                rows, prows = _rows_of(j), _rows_of(jnp.maximum(j - 1, 0))
                has_prev = ((j % nblk) != 0).astype(jnp.int32)

                def keys(ref, lanes):
                    if nblk == 1:
                        return ref[rows, lanes]
                    return jnp.concatenate([ref[prows, lanes], ref[rows, lanes]], axis=0)

                for pr in range(HEAD_GROUP // 2):
                    lanes = _lane_half(pr)
                    q_ab = _split_pair(qs[rows, lanes] * SCALE)
                    do_ab = _split_pair(dos[rows, lanes])
                    k2, v2 = keys(ks, lanes), keys(vs, lanes)
                    k_ab = _split_pair(k2)
                    dq, dk2, dv2 = None, None, None
                    for ab in range(2):
                        hh = 2 * pr + ab
                        st = slice(hh * STAT_LANES, hh * STAT_LANES + 1)
                        s = _dot_nt(q_ab[ab], k2) + (bias[0, hh, :, BLK:] if nblk == 1 else bias[has_prev, hh])
                        p = jnp.exp(s - lss[rows, st])
                        dp = _dot_nt(do_ab[ab], v2)
                        ds = (p * (dp - dds[rows, st])).astype(BF16)
                        terms = (_dot_nn(ds, k_ab[ab]), _dot_tn(ds, q_ab[ab]), _dot_tn(p.astype(BF16), do_ab[ab]))
                        dq, dk2, dv2 = terms if dq is None else (dq + terms[0], dk2 + terms[1], dv2 + terms[2])
                    gq[rows, lanes] = dq * SCALE
                    if nblk == 1:
                        gk[rows, lanes] = dk2
                        gv[rows, lanes] = dv2
                    else:
                        gk[prows, lanes] += dk2[:BLK]
                        gv[prows, lanes] += dv2[:BLK]
                        gk[rows, lanes] += dk2[BLK:]
                        gv[rows, lanes] += dv2[BLK:]
                return carry

            lax.fori_loop(0, nb, block, 0, unroll=4)

        run(1, q_ref, k_ref, v_ref, do_ref, lse_ref, ddt)

        for src, dst in ((gq, aq), (gk, ak), (gv, av), (q_ref, qf), (k_ref, kf), (v_ref, vf), (do_ref, dof)):
            _split_halves(src, dst, seq)
        for dil in (4, 16):
            for src, dst in ((qf, qd), (kf, kd), (vf, vd), (dof, dod)):
                _deinterleave(src, dst, seq, dil, BF16)
            _deinterleave(lse_ref, lsd, seq, dil, F32)
            _deinterleave(ddt, ddd, seq, dil, F32)
            run(dil, qd, kd, vd, dod, lsd, ddd)
            for src, dst in ((gq, aq), (gk, ak), (gv, av)):
                _interleave(src, dst, seq, dil, True)

        def write(i, carry):
            rows = _rows_of(i)
            for src, dst in ((aq, dq_ref), (ak, dk_ref), (av, dv_ref)):
                for hf in range(2):
                    dst[rows, _lane_half(hf)] = src[hf, rows, :].astype(BF16)
            return carry

        lax.fori_loop(0, nb, write, 0)
        if ne:
            pl.when((pl.program_id(0) == bsz - 1) & (pl.program_id(1) == N_GROUPS - 1))(finish)

    def hspec(off):
        return pl.BlockSpec((seq, 256), lambda b, g: (b, off + g))

    io = pl.BlockSpec((seq, 256), lambda b, g: (b, g))
    big = lambda dt: pltpu.VMEM((seq, 256), dt)
    halves = lambda: pltpu.VMEM((2, seq, 128), F32)
    stat = lambda: pltpu.VMEM((seq, 128), F32)
    outs = pl.pallas_call(
        body, name="band_attn_bwd_exchange" if ne else "band_attn_bwd", grid=(bsz, N_GROUPS),
        in_specs=[pl.BlockSpec((None, 8, 128), lambda b, g: (g, 0, 0)), hspec(0), hspec(k_off), hspec(v_off), io, io,
                  pl.BlockSpec((seq, 128), lambda b, g: (b, g))] + [_ANY] * ne,
        out_specs=[io, io, io] + [_ANY] * ne,
        out_shape=[_sds((bsz * seq, MIX_W), BF16)] * 3 + [_sds(s.shape, s.dtype) for s in (exchange or [])],
        scratch_shapes=[halves(), halves(), halves(), halves(), stat(),
                        big(BF16), big(BF16), big(BF16), big(BF16), stat(), stat(),
                        big(F32), big(F32), big(F32), halves(), halves(), halves(), _bias_scratch()]
        + (_exchange_sems(ne) if ne else []),
        compiler_params=_params(("arbitrary", "arbitrary")))(slopes, h, h, h, dcat, mix, lse, *(exchange or []))
    return list(outs[:3]), list(outs[3:])


def _mem_attn_fwd(h, mkv, bsz, seq, q_col, tq=1024):
    nq = seq // tq

    def body(q_ref, kv_ref, o_ref):
        for pr in range(2):
            lanes = _lane_half(pr)
            q_ab = _split_pair(q_ref[:, lanes])
            k = kv_ref[:, lanes]
            v_ab = _split_pair(kv_ref[:, MEM_W + pr * 128:MEM_W + (pr + 1) * 128])
            out = None
            for ab in range(2):
                s = _dot_nt(q_ab[ab], k) * SCALE
                m = jnp.max(s, axis=-1, keepdims=True)
                p = jnp.exp(s - m)
                l = jnp.sum(p, axis=-1, keepdims=True)
                term = _dot_nn(p.astype(BF16), v_ab[ab]) / l
                out = term if out is None else out + term
            o_ref[:, lanes] = out.astype(BF16)

    return pl.pallas_call(
        body, name="mem_attn_fwd", grid=(bsz, nq),
        in_specs=[pl.BlockSpec((tq, MEM_W), lambda b, i: (b * nq + i, q_col)),
                  pl.BlockSpec((N_MEM, 2 * MEM_W), lambda b, i: (b, 0))],
        out_specs=pl.BlockSpec((tq, MEM_W), lambda b, i: (b * nq + i, 0)),
        out_shape=_sds((bsz * seq, MEM_W), BF16), compiler_params=_params(("parallel", "parallel")))(h, mkv)


def _mem_attn_bwd(h, mkv, dcat, bsz, seq, q_col, tq=1024):
    nq = seq // tq
    do_col = MIX_W // MEM_W

    def body(q_ref, kv_ref, do_ref, dq_ref, dkv_ref):
        @pl.when(pl.program_id(1) == 0)
        def _():
            dkv_ref[...] = jnp.zeros_like(dkv_ref)

        for pr in range(2):
            lanes = _lane_half(pr)
            vlanes = slice(MEM_W + pr * 128, MEM_W + (pr + 1) * 128)
            q_ab = _split_pair(q_ref[:, lanes])
            do_ab = _split_pair(do_ref[:, lanes])
            k, v = kv_ref[:, lanes], kv_ref[:, vlanes]
            k_ab = _split_pair(k)
            dq, dk, dv = None, None, None
            for ab in range(2):
                s = _dot_nt(q_ab[ab], k) * SCALE
                m = jnp.max(s, axis=-1, keepdims=True)
                e = jnp.exp(s - m)
                p = e / jnp.sum(e, axis=-1, keepdims=True)
                dp = _dot_nt(do_ab[ab], v)
                dd = jnp.sum(p * dp, axis=-1, keepdims=True)
                ds = (p * (dp - dd) * SCALE).astype(BF16)
                terms = (_dot_nn(ds, k_ab[ab]), _dot_tn(ds, q_ab[ab]), _dot_tn(p.astype(BF16), do_ab[ab]))
                dq, dk, dv = terms if dq is None else (dq + terms[0], dk + terms[1], dv + terms[2])
            dq_ref[:, lanes] = dq.astype(BF16)
            dkv_ref[:, lanes] += dk
            dkv_ref[:, vlanes] += dv

    return pl.pallas_call(
        body, name="mem_attn_bwd", grid=(bsz, nq),
        in_specs=[pl.BlockSpec((tq, MEM_W), lambda b, i: (b * nq + i, q_col)),
                  pl.BlockSpec((N_MEM, 2 * MEM_W), lambda b, i: (b, 0)),
                  pl.BlockSpec((tq, MEM_W), lambda b, i: (b * nq + i, do_col))],
        out_specs=[pl.BlockSpec((tq, MEM_W), lambda b, i: (b * nq + i, 0)),
                   pl.BlockSpec((N_MEM, 2 * MEM_W), lambda b, i: (b, 0))],
        out_shape=[_sds((bsz * seq, MEM_W), BF16), _sds((bsz * N_MEM, 2 * MEM_W), F32)],
        compiler_params=_params(("parallel", "arbitrary")))(h, mkv, dcat)


_GELU_C = math.sqrt(2.0 / math.pi)
_GELU_A = 0.044715


def _gelu(x):
    return 0.5 * x * (1.0 + jnp.tanh(_GELU_C * (x + _GELU_A * x * x * x)))


def _gelu_grad(x):
    th = jnp.tanh(_GELU_C * (x + _GELU_A * x * x * x))
    return 0.5 * (1.0 + th) + 0.5 * x * (1.0 - th * th) * (_GELU_C * (1.0 + 3.0 * _GELU_A * x * x))


def _tril_mask(lower):
    ri = lax.broadcasted_iota(jnp.int32, (BLK, BLK), 0)
    ci = lax.broadcasted_iota(jnp.int32, (BLK, BLK), 1)
    return (ri >= ci) if lower else (ci >= ri)


def _sgu_fwd(h, ws, bs_t, ln_g, ln_b, tm=512):
    t = h.shape[0]

    def body(u_ref, v_ref, ws_ref, bs_ref, g_ref, b_ref, o_ref):
        ug = _gelu(u_ref[...].astype(F32))
        vhat, _ = _ln_hat(_gelu(v_ref[...].astype(F32)))
        vn = (vhat * g_ref[...] + b_ref[...]).astype(BF16)
        mask = _tril_mask(True)
        first = lax.broadcasted_iota(jnp.int32, (1, 2 * HEAD_DIM), 1) < HEAD_DIM
        for pr in range(N_HEADS // 2):
            lanes = _lane_half(pr)
            w_ab = [jnp.where(mask, ws_ref[2 * pr + ab], 0).astype(BF16) for ab in range(2)]
            bias = jnp.where(first, bs_ref[:, 2 * pr:2 * pr + 1], bs_ref[:, 2 * pr + 1:2 * pr + 2])
            for c in range(tm // BLK):
                rs = slice(c * BLK, (c + 1) * BLK)
                v_ab = _split_pair(vn[rs, lanes])
                mixed = _dot_nn(w_ab[0], v_ab[0]) + _dot_nn(w_ab[1], v_ab[1]) + bias
                o_ref[rs, lanes] = (ug[rs, lanes] * mixed).astype(BF16)

    return pl.pallas_call(
        body, name="sgu_fwd", grid=(t // tm,),
        in_specs=[_rows(tm, MIX_W, 0), _rows(tm, MIX_W, 1), _whole(ws.shape), _whole(bs_t.shape), _whole(ln_g.shape), _whole(ln_b.shape)],
        out_specs=_rows(tm, MIX_W), out_shape=_sds((t, MIX_W), BF16),
        compiler_params=_params(("parallel",)))(h, h, ws, bs_t, ln_g, ln_b)


def _sgu_bwd(h, dcat, ws, ws_t, bs_t, ln_g, ln_b, tm=512):
    t = h.shape[0]

    def body(u_ref, v_ref, do_ref, ws_ref, wst_ref, bs_ref, g_ref, b_ref, dh_ref, dws_ref, dbs_ref, dg_ref, db_ref, dvn_ref):
        @pl.when(pl.program_id(0) == 0)
        def _():
            dws_ref[...] = jnp.zeros_like(dws_ref)
            dbs_ref[...] = jnp.zeros_like(dbs_ref)
            dg_ref[...] = jnp.zeros_like(dg_ref)
            db_ref[...] = jnp.zeros_like(db_ref)

        u = u_ref[...].astype(F32)
        v = v_ref[...].astype(F32)
        do = do_ref[...].astype(F32)
        ug = _gelu(u)
        vhat, rstd = _ln_hat(_gelu(v))
        vn = (vhat * g_ref[...] + b_ref[...]).astype(BF16)
        dmixed_f = do * ug
        dmixed = dmixed_f.astype(BF16)
        low, upp = _tril_mask(True), _tril_mask(False)
        first = lax.broadcasted_iota(jnp.int32, (1, 2 * HEAD_DIM), 1) < HEAD_DIM
        for pr in range(N_HEADS // 2):
            lanes = _lane_half(pr)
            w_ab = [jnp.where(low, ws_ref[2 * pr + ab], 0).astype(BF16) for ab in range(2)]
            wt_ab = [jnp.where(upp, wst_ref[2 * pr + ab], 0).astype(BF16) for ab in range(2)]
            bias = jnp.where(first, bs_ref[:, 2 * pr:2 * pr + 1], bs_ref[:, 2 * pr + 1:2 * pr + 2])
            dws_acc = [None, None]
            dbs_acc = [None, None]
            for c in range(tm // BLK):
                rs = slice(c * BLK, (c + 1) * BLK)
                vn_pair = vn[rs, lanes]
                v_ab = _split_pair(vn_pair)
                mixed = _dot_nn(w_ab[0], v_ab[0]) + _dot_nn(w_ab[1], v_ab[1]) + bias
                dh_ref[rs, lanes] = (do[rs, lanes] * mixed * _gelu_grad(u[rs, lanes])).astype(BF16)
                dm_ab = _split_pair(dmixed[rs, lanes])
                dmf_ab = _split_pair(dmixed_f[rs, lanes])
                for ab in range(2):
                    term = _dot_nt(dm_ab[ab], vn_pair)
                    dws_acc[ab] = term if dws_acc[ab] is None else dws_acc[ab] + term
                    rsum = jnp.sum(dmf_ab[ab], axis=-1, keepdims=True)
                    dbs_acc[ab] = rsum if dbs_acc[ab] is None else dbs_acc[ab] + rsum
                dvn_ref[rs, lanes] = _dot_nn(wt_ab[0], dm_ab[0]) + _dot_nn(wt_ab[1], dm_ab[1])
            for ab in range(2):
                g = 2 * pr + ab
                dws_ref[g] += jnp.where(low, dws_acc[ab], 0.0)
                dbs_ref[:, g:g + 1] += dbs_acc[ab]
        dvn = dvn_ref[...]
        dg_ref[...] += jnp.sum(dvn * vhat, axis=0, keepdims=True)
        db_ref[...] += jnp.sum(dvn, axis=0, keepdims=True)
        dxh = dvn * g_ref[...]
        m1 = jnp.mean(dxh, axis=-1, keepdims=True)
        m2 = jnp.mean(dxh * vhat, axis=-1, keepdims=True)
        dvg = rstd * (dxh - m1 - vhat * m2)
        dh_ref[:, MIX_W:] = (dvg * _gelu_grad(v)).astype(BF16)

    return pl.pallas_call(
        body, name="sgu_bwd", grid=(t // tm,),
        in_specs=[_rows(tm, MIX_W, 0), _rows(tm, MIX_W, 1), _rows(tm, MIX_W, 0), _whole(ws.shape), _whole(ws_t.shape),
                  _whole(bs_t.shape), _whole(ln_g.shape), _whole(ln_b.shape)],
        out_specs=[_rows(tm, 2 * MIX_W), _whole(ws.shape), _whole(bs_t.shape), _whole((1, MIX_W)), _whole((1, MIX_W))],
        out_shape=[_sds((t, 2 * MIX_W), BF16), _sds(ws.shape, F32), _sds(bs_t.shape, F32), _sds((1, MIX_W), F32), _sds((1, MIX_W), F32)],
        scratch_shapes=[pltpu.VMEM((tm, MIX_W), F32)],
        compiler_params=_params(("arbitrary",)))(h, h, dcat, ws, ws_t, bs_t, ln_g, ln_b)


def _row_tile(rows, cols, itemsize=4, limit=2 ** 20):
    best = rows
    for cand in (4096, 2048, 1024, 512, 256, 128, 64, 32, 16):
        if rows % cand == 0 and rows > cand:
            best = cand
            if cand * cols * itemsize <= limit:
                break
    return best


def _adamw(w, m, v, grad=None, parts=None, first_parts=None):
    rows, cols = w.shape
    rows0 = 0 if first_parts is None else first_parts.shape[1]
    tr = _row_tile(rows0 if rows0 else rows, cols)
    n0 = rows0 // tr

    def chip_sum(ref):
        acc = ref[0].astype(F32)
        for k in range(1, 4):
            acc = acc + ref[k].astype(F32)
        return acc

    def body(*refs):
        w_ref, m_ref, v_ref = refs[:3]
        go_ref, d_ref, nm_ref, nv_ref = refs[-4:]
        if parts is None:
            gv = refs[3][...]
        elif first_parts is None:
            gv = chip_sum(refs[3])
        else:
            gv = jnp.where(pl.program_id(0) < n0, chip_sum(refs[3]), chip_sum(refs[4]))
        nm = ADAM_B1 * m_ref[...] + (1.0 - ADAM_B1) * gv
        nv = ADAM_B2 * v_ref[...] + (1.0 - ADAM_B2) * (gv * gv)
        m_hat = nm / (1.0 - ADAM_B1 ** ADAM_STEP)
        v_hat = nv / (1.0 - ADAM_B2 ** ADAM_STEP)
        go_ref[...] = gv
        d_ref[...] = -ADAM_LR * (m_hat / (jnp.sqrt(v_hat) + ADAM_EPS) + ADAM_WD * w_ref[...])
        nm_ref[...] = nm
        nv_ref[...] = nv

    spec = _rows(tr, cols)
    if parts is None:
        g_specs, g_args = [spec], [grad]
    elif first_parts is None:
        g_specs, g_args = [pl.BlockSpec((4, tr, cols), lambda i: (0, i, 0))], [parts]
    else:
        g_specs = [pl.BlockSpec((4, tr, cols), lambda i: (0, jnp.minimum(i, n0 - 1), 0)),
                   pl.BlockSpec((4, tr, cols), lambda i: (0, jnp.maximum(i - n0, 0), 0))]
        g_args = [first_parts, parts]
    return pl.pallas_call(
        body, name="adamw" if parts is None else "adamw_sum_chips", grid=(rows // tr,), in_specs=[spec] * 3 + g_specs,
        out_specs=[spec] * 4, out_shape=[_sds(w.shape, F32)] * 4,
        compiler_params=_params(("parallel",)))(w, m, v, *g_args)


_ANY = pl.BlockSpec(memory_space=pl.ANY)
_MESH = pl.DeviceIdType.MESH


def _all_gather(name, blocks):
    nt = len(blocks)

    def body(*refs):
        start, finish = _gather_protocol(refs[:nt], refs[nt:2 * nt], *refs[2 * nt:])
        start()
        finish()

    return pl.pallas_call(
        body, name=name, out_shape=_gather_shapes(blocks), in_specs=[_ANY] * nt, out_specs=[_ANY] * nt,
        scratch_shapes=_gather_sems(nt))(*blocks)


def _gather_shapes(blocks):
    return [_sds((N_DEV,) + b.shape, b.dtype) for b in blocks]


def _gather_sems(nt):
    return [pltpu.SemaphoreType.DMA((nt, 7)), pltpu.SemaphoreType.DMA((nt, 7)), pltpu.SemaphoreType.DMA((nt,))]


def _gather_protocol(x_refs, out_refs, send_sems, recv_sems, local_sems):
    nt = len(x_refs)
    x, y, c = lax.axis_index("x"), lax.axis_index("y"), lax.axis_index("c")
    me, sibling = (x, y, c), (x, y, 1 - c)
    chips = [(1 - x, y), (x, 1 - y), (1 - x, 1 - y)]

    def slot(t, px, py, pc):
        return out_refs[t].at[4 * px + 2 * py + pc]

    def copy(t, k, blk, to, src=None):
        return pltpu.make_async_remote_copy(
            src_ref=slot(t, *blk) if src is None else src, dst_ref=slot(t, *blk),
            send_sem=send_sems.at[t, k], recv_sem=recv_sems.at[t, k], device_id=to, device_id_type=_MESH)

    def own_copies():
        mine = [pltpu.make_async_copy(x_refs[t], slot(t, *me), local_sems.at[t]) for t in range(nt)]
        first = []
        for t in range(nt):
            first.append(copy(t, 0, me, sibling, src=x_refs[t]))
            first += [copy(t, 1 + j, me, (*chip, c), src=x_refs[t]) for j, chip in enumerate(chips)]
        return mine, first

    def start():
        mine, first = own_copies()
        for cp in mine + first:
            cp.start()

    def finish():
        mine, first = own_copies()
        passed = []
        for j, chip in enumerate(chips):
            for t in range(nt):
                copy(t, 1 + j, (*chip, c), me).wait_recv()
                fwd = copy(t, 4 + j, (*chip, c), sibling)
                fwd.start()
                passed.append(fwd)
        for t in range(nt):
            copy(t, 0, sibling, me).wait_recv()
        for j, chip in enumerate(chips):
            for t in range(nt):
                copy(t, 4 + j, (*chip, 1 - c), me).wait_recv()
        for cp in first + passed:
            cp.wait_send()
        for cp in mine:
            cp.wait()

    return start, finish


def _swap_with_sibling(packed):
    nt = len(packed)

    def body(*refs):
        p_refs, got_refs = refs[:nt], refs[nt:2 * nt]
        send_sems, recv_sems = refs[2 * nt:]
        x, y, c = lax.axis_index("x"), lax.axis_index("y"), lax.axis_index("c")
        copies = [
            pltpu.make_async_remote_copy(
                src_ref=p_refs[t].at[1 - c], dst_ref=got_refs[t], send_sem=send_sems.at[t], recv_sem=recv_sems.at[t],
                device_id=(x, y, 1 - c), device_id_type=_MESH)
            for t in range(nt)]
        for cp in copies:
            cp.start()
        for cp in copies:
            cp.wait_recv()
        for cp in copies:
            cp.wait_send()

    return pl.pallas_call(
        body, name="grad_swap_sibling", out_shape=[_sds(p.shape[1:], p.dtype) for p in packed], in_specs=[_ANY] * nt,
        out_specs=[_ANY] * nt,
        scratch_shapes=[pltpu.SemaphoreType.DMA((nt,)), pltpu.SemaphoreType.DMA((nt,))])(*packed)


def _chip_sum(packed, got):
    _, nchip, rows, cols = packed.shape
    tr = _row_tile(rows, cols, 2)
    core = lax.axis_index("c").astype(jnp.int32).reshape(1)

    def body(c_ref, p_ref, g_ref, o_ref):
        o_ref[...] = (p_ref[...].astype(F32) + g_ref[...].astype(F32)).astype(o_ref.dtype)

    grid_spec = pltpu.PrefetchScalarGridSpec(
        num_scalar_prefetch=1, grid=(nchip, rows // tr),
        in_specs=[pl.BlockSpec((None, None, tr, cols), lambda k, i, c: (c[0], k, i, 0)),
                  pl.BlockSpec((None, tr, cols), lambda k, i, c: (k, i, 0))],
        out_specs=pl.BlockSpec((None, tr, cols), lambda k, i, c: (k, i, 0)))
    return pl.pallas_call(
        body, name="grad_chip_sum", grid_spec=grid_spec, out_shape=_sds(got.shape, got.dtype),
        compiler_params=_params(("parallel", "parallel")))(core, packed, got)


def _exchange_chips(chip_sums):
    nt = len(chip_sums)

    def body(*refs):
        start, finish = _exchange_protocol(refs[:nt], refs[nt:2 * nt], *refs[2 * nt:])
        start()
        finish()

    return pl.pallas_call(
        body, name="grad_exchange_chips", out_shape=[_sds(s.shape, s.dtype) for s in chip_sums], in_specs=[_ANY] * nt,
        out_specs=[_ANY] * nt, scratch_shapes=_exchange_sems(nt))(*chip_sums)


def _exchange_sems(nt):
    return [pltpu.SemaphoreType.DMA((nt, 3)), pltpu.SemaphoreType.DMA((nt, 3)), pltpu.SemaphoreType.DMA((nt,))]


def _exchange_protocol(s_refs, got_refs, send_sems, recv_sems, local_sems):
    nt = len(s_refs)
    x, y, c = lax.axis_index("x"), lax.axis_index("y"), lax.axis_index("c")
    my_chip = 2 * x + y
    chips = [(1 - x, y), (x, 1 - y), (1 - x, 1 - y)]

    def copy(t, j, src_chip, dst_chip):
        px, py = chips[j]
        return pltpu.make_async_remote_copy(
            src_ref=s_refs[t].at[src_chip], dst_ref=got_refs[t].at[dst_chip], send_sem=send_sems.at[t, j],
            recv_sem=recv_sems.at[t, j], device_id=(px, py, c), device_id_type=_MESH)

    def own_copies():
        mine = [pltpu.make_async_copy(s_refs[t].at[my_chip], got_refs[t].at[my_chip], local_sems.at[t]) for t in range(nt)]
        sends = [copy(t, j, 2 * px + py, my_chip) for t in range(nt) for j, (px, py) in enumerate(chips)]
        return mine, sends

    def start():
        mine, sends = own_copies()
        for cp in mine + sends:
            cp.start()

    def finish():
        mine, sends = own_copies()
        for j, (px, py) in enumerate(chips):
            for t in range(nt):
                copy(t, j, my_chip, 2 * px + py).wait_recv()
        for cp in sends:
            cp.wait_send()
        for cp in mine:
            cp.wait()

    return start, finish


def _sum_chips(got):
    _, rows, cols = got.shape
    tr = _row_tile(rows, cols)

    def body(g_ref, o_ref):
        acc = g_ref[0].astype(F32)
        for k in range(1, 4):
            acc = acc + g_ref[k].astype(F32)
        o_ref[...] = acc

    return pl.pallas_call(
        body, name="grad_sum_chips", grid=(rows // tr,), in_specs=[pl.BlockSpec((4, tr, cols), lambda i: (0, i, 0))],
        out_specs=pl.BlockSpec((tr, cols), lambda i: (i, 0)), out_shape=_sds((rows, cols), F32),
        compiler_params=_params(("parallel",)))(got)


_COL_SHARDED = ("a_w_in", "b_w_in", "w_gate", "w_up")
_ROW_SHARDED = ("w_mem_kv", "w_out", "w_down")
_BIG = ("a_w_in", "b_w_in", "w_mem_kv", "w_out", "w_gate", "w_up", "w_down")
_SGU_LN = ("sgu_ln_g", "sgu_ln_b")
_LN4 = ("ln_mix_g", "ln_mix_b", "ln_ffn_g", "ln_ffn_b")
_REPLICATED = ("sgu_w_s", "sgu_b_s") + _LN4


def _unshard(name, gathered):
    if name in _COL_SHARDED or name in _SGU_LN:
        moved = jnp.moveaxis(gathered, 0, -2)
        return moved.reshape(moved.shape[:-2] + (moved.shape[-2] * moved.shape[-1],))
    moved = jnp.moveaxis(gathered, 0, 1)
    return moved.reshape((moved.shape[0], moved.shape[1] * moved.shape[2]) + moved.shape[3:])


_LAID_OUT_IN_KERNEL = _COL_SHARDED


def _after_gather(name, gathered):
    return gathered if name in _LAID_OUT_IN_KERNEL else _unshard(name, gathered)


def _by_shard(name, full):
    if name in _COL_SHARDED or name in _SGU_LN:
        split = full.reshape(full.shape[:-1] + (N_DEV, full.shape[-1] // N_DEV))
        return jnp.moveaxis(split, -2, 0)
    split = full.reshape((full.shape[0], N_DEV, full.shape[1] // N_DEV) + full.shape[2:])
    return jnp.moveaxis(split, 1, 0)


def _layer_keys(i):
    return [("a_w_in" if i % 2 == 0 else "b_w_in", i // 2)] + [(n, i) for n in ("w_mem_kv", "w_out", "w_gate", "w_up", "w_down")]


_GATHER_FIRST = _layer_keys(0)[:2]
_GATHER_LATER = (_layer_keys(0)[2:] + _layer_keys(1), _layer_keys(2)[:4], _layer_keys(2)[4:],
                 _layer_keys(3)[:4], _layer_keys(3)[4:])


def _shard_block(shards, key):
    name, idx = key
    return shards[name][idx:idx + 1].astype(BF16)


def _gather_first(shards):
    blocks = [_shard_block(shards, k) for k in _GATHER_FIRST] + [shards[n] for n in _SGU_LN]
    gathered = _all_gather("first_all_gather", blocks)
    full = {k: _after_gather(k[0], g) for k, g in zip(_GATHER_FIRST, gathered)}
    sgu_ln = {n: _unshard(n, g) for n, g in zip(_SGU_LN, gathered[len(_GATHER_FIRST):])}
    return full, sgu_ln


def _two_level(by_dest):
    shp = by_dest.shape[1:]
    split = by_dest.astype(BF16).reshape((4, 2) + shp).swapaxes(0, 1)
    return split.reshape(2, 4, int(np.prod(shp[:-1])), shp[-1])


_EARLY = _BIG + _SGU_LN


def _chip_sums_of_early(grads):
    packed = [_two_level(_by_shard(n, jnp.stack(grads[n][1:] if n == "a_w_in" else grads[n]))) for n in _EARLY]
    ln4 = jnp.stack([jnp.stack(grads[n]) for n in _LN4])
    rep = [jnp.stack(grads["sgu_w_s"]).reshape(N_DEV, -1, BLK), jnp.stack(grads["sgu_b_s"]).reshape(N_DEV, -1, BLK),
           ln4.reshape(N_DEV, -1, D_MODEL)]
    packed += [_two_level(r) for r in rep]
    got = _swap_with_sibling(packed)
    return [_chip_sum(p, g) for p, g in zip(packed, got)]


def _finish_replicated(parts, shapes):
    w_s, b_s, ln_all = _all_gather("replicated_grads_all_gather", [_sum_chips(p) for p in parts])
    ln_all = ln_all.reshape((len(_LN4),) + tuple(shapes[_LN4[0]]))
    rep_grads = {"sgu_w_s": w_s.reshape(shapes["sgu_w_s"]), "sgu_b_s": b_s.reshape(shapes["sgu_b_s"])}
    rep_grads.update({n: ln_all[i] for i, n in enumerate(_LN4)})
    return rep_grads


def _reduce_last(grad_a_first):
    packed = [_two_level(_by_shard("a_w_in", grad_a_first))]
    got = _swap_with_sibling(packed)
    return _exchange_chips([_chip_sum(packed[0], got[0])])[0]


def _as_2d(a):
    if a.ndim == 1:
        return a.reshape(1, -1)
    return a.reshape(-1, a.shape[-1])


def kernel(x, mem, a_w_in, b_w_in, sgu_ln_g, sgu_ln_b, sgu_w_s, sgu_b_s, w_mem_kv, w_out, ln_mix_g, ln_mix_b, w_gate, w_up, w_down, ln_ffn_g, ln_ffn_b, loss_target, m_a_w_in, m_b_w_in, m_sgu_ln_g, m_sgu_ln_b, m_sgu_w_s, m_sgu_b_s, m_w_mem_kv, m_w_out, m_ln_mix_g, m_ln_mix_b, m_w_gate, m_w_up, m_w_down, m_ln_ffn_g, m_ln_ffn_b, v_a_w_in, v_b_w_in, v_sgu_ln_g, v_sgu_ln_b, v_sgu_w_s, v_sgu_b_s, v_w_mem_kv, v_w_out, v_ln_mix_g, v_ln_mix_b, v_w_gate, v_w_up, v_w_down, v_ln_ffn_g, v_ln_ffn_b):
    names = ("a_w_in", "b_w_in", "sgu_ln_g", "sgu_ln_b", "sgu_w_s", "sgu_b_s", "w_mem_kv", "w_out", "ln_mix_g", "ln_mix_b",
             "w_gate", "w_up", "w_down", "ln_ffn_g", "ln_ffn_b")
    weights = dict(zip(names, (a_w_in, b_w_in, sgu_ln_g, sgu_ln_b, sgu_w_s, sgu_b_s, w_mem_kv, w_out, ln_mix_g, ln_mix_b,
                               w_gate, w_up, w_down, ln_ffn_g, ln_ffn_b)))
    mom_m = dict(zip(names, (m_a_w_in, m_b_w_in, m_sgu_ln_g, m_sgu_ln_b, m_sgu_w_s, m_sgu_b_s, m_w_mem_kv, m_w_out, m_ln_mix_g,
                             m_ln_mix_b, m_w_gate, m_w_up, m_w_down, m_ln_ffn_g, m_ln_ffn_b)))
    mom_v = dict(zip(names, (v_a_w_in, v_b_w_in, v_sgu_ln_g, v_sgu_ln_b, v_sgu_w_s, v_sgu_b_s, v_w_mem_kv, v_w_out, v_ln_mix_g,
                             v_ln_mix_b, v_w_gate, v_w_up, v_w_down, v_ln_ffn_g, v_ln_ffn_b)))
    full, sgu_ln = _gather_first(weights)
    pending = [(keys, [_shard_block(weights, k) for k in keys]) for keys in _GATHER_LATER]
    loss_part, grad_x, local, early = _local_step(
        x, mem, loss_target, full, sgu_ln, {n: weights[n] for n in _REPLICATED}, pending)
    loss = lax.psum(loss_part[0, 0], ("x", "y", "c"))
    early_parts = dict(zip(_EARLY, early))
    rep_grads = _finish_replicated(early[len(_EARLY):], {n: weights[n].shape for n in _REPLICATED})
    a_first_parts = _reduce_last(local["a_w_in"][:1])

    reduced, deltas, new_m, new_v = {}, {}, {}, {}
    for n in names:
        w2, m2, v2 = _as_2d(weights[n]), _as_2d(mom_m[n]), _as_2d(mom_v[n])
        if n in early_parts:
            outs = _adamw(w2, m2, v2, parts=early_parts[n], first_parts=a_first_parts if n == "a_w_in" else None)
        else:
            outs = _adamw(w2, m2, v2, grad=_as_2d(rep_grads[n]))
        reduced[n], deltas[n], new_m[n], new_v[n] = (a.reshape(weights[n].shape) for a in outs)

    return (loss, grad_x, *[reduced[n] for n in names], *[deltas[n] for n in names],
            *[new_m[n] for n in names], *[new_v[n] for n in names])


def _local_step(x, mem, loss_target, full, sgu_ln, small, pending=None):
    sgu_w_s, sgu_b_s = small["sgu_w_s"], small["sgu_b_s"]
    ln_mix_g, ln_mix_b, ln_ffn_g, ln_ffn_b = (small[n] for n in ("ln_mix_g", "ln_mix_b", "ln_ffn_g", "ln_ffn_b"))
    bsz, seq, _ = x.shape
    tokens = bsz * seq
    slopes = _alibi_table()
    full = dict(full)
    exchanging = pending is not None
    pending = list(pending or [])

    def weight(name, idx):
        return full[(name, idx)][0]

    def next_group():
        return pending[0][1] if pending else None

    def landed(gathered):
        if gathered:
            keys, _ = pending.pop(0)
            full.update({k: _after_gather(k[0], g) for k, g in zip(keys, gathered)})

    res = (x.reshape(tokens, D_MODEL),)
    xb = res[0].astype(BF16)
    memb = mem.reshape(bsz * N_MEM, D_MODEL).astype(BF16)
    tgt = loss_target.reshape(tokens, D_MODEL)

    saved = []
    for i in range(DEPTH):
        j = i // 2
        dil_layer = i % 2 == 0
        mkv = _linear_nn("mem_kv", memb, weight("w_mem_kv", i))
        h, w_in = _linear_nn_gathered("in_proj_a" if dil_layer else "in_proj_b", xb,
                                      full[("a_w_in" if dil_layer else "b_w_in", j)][:, 0])
        st = dict(xb=xb, h=h, mkv=mkv, w_in=w_in)
        if dil_layer:
            mix, st["lse"], gathered = _band_attn_fwd_fused(h, slopes, bsz, seq, gather=next_group() if i == 0 else None)
            landed(gathered)
            q_col = 3 * MIX_W // MEM_W
        else:
            st["ws"] = sgu_w_s[j]
            st["bs_t"] = sgu_b_s[j].T
            st["ln_g"] = sgu_ln["sgu_ln_g"][j].reshape(1, MIX_W)
            st["ln_b"] = sgu_ln["sgu_ln_b"][j].reshape(1, MIX_W)
            mix = _sgu_fwd(h, st["ws"], st["bs_t"], st["ln_g"], st["ln_b"])
            q_col = 2 * MIX_W // MEM_W
        mo = _mem_attn_fwd(h, mkv, bsz, seq, q_col)
        w_out, w_down = weight("w_out", i), weight("w_down", i)
        mix_ln = (ln_mix_g[i].reshape(1, D_MODEL), ln_mix_b[i].reshape(1, D_MODEL))
        ffn_ln = (ln_ffn_g[i].reshape(1, D_MODEL), ln_ffn_b[i].reshape(1, D_MODEL))
        r1, x1b, _ = _proj_ln_fwd("out_proj_ln", [mix, mo], w_out, res, *mix_ln)
        gt, up, act, w_gu, gathered = _ffn_up_fwd(x1b, full[("w_gate", i)][:, 0], full[("w_up", i)][:, 0],
                                                  gather=next_group() if i < 2 else None)
        landed(gathered)
        r2, xb, gathered = _proj_ln_fwd("ffn_down_ln", [act], w_down, (r1, *mix_ln), *ffn_ln,
                                        gather=next_group() if i < 2 else None)
        landed(gathered)
        res = (r2, *ffn_ln)
        st.update(mix=mix, mo=mo, q_col=q_col, r1=r1, x1b=x1b, gt=gt, up=up, act=act, r2=r2,
                  w_out=w_out, w_down=w_down, w_gu=w_gu)
        saved.append(st)

    dr2, dr2b, dg, db, loss_part = _loss_ln_bwd(*res, tgt)

    early_parts = None
    per_pair = ("a_w_in", "b_w_in", "sgu_ln_g", "sgu_ln_b", "sgu_w_s", "sgu_b_s")
    grads = {n: [None] * (DEPTH // 2 if n in per_pair else DEPTH) for n in _BIG + _SGU_LN + _REPLICATED}
    for i in reversed(range(DEPTH)):
        j = i // 2
        st = saved[i]
        dil_layer = i % 2 == 0
        w_in = st["w_in"]
        grads["ln_ffn_g"][i], grads["ln_ffn_b"][i] = dg[0], db[0]
        dgu = _ffn_down_bwd(dr2b, st["w_down"], st["gt"], st["up"])
        grads["w_down"][i] = _mm_tn("grad_w_down", st["act"], dr2b)
        dr1, dr1b, dg, db = _linear_nt("ffn_up_bwd", [dgu], st["w_gu"], dr2, F32,
                                       ln=(st["r1"], ln_mix_g[i].reshape(1, D_MODEL)))
        grads["ln_mix_g"][i], grads["ln_mix_b"][i] = dg[0], db[0]
        dw_gu = _mm_tn("grad_w_gate_up", st["x1b"], dgu)
        grads["w_gate"][i], grads["w_up"][i] = dw_gu[:, :D_FF], dw_gu[:, D_FF:]
        dcat = _linear_nt("out_proj_bwd", [dr1b], st["w_out"], None, BF16)
        grads["w_out"][i] = jnp.concatenate(
            [_mm_tn("grad_w_out_mix", st["mix"], dr1b), _mm_tn("grad_w_out_mem", st["mo"], dr1b)], axis=0)
        dqm, dmkv = _mem_attn_bwd(st["h"], st["mkv"], dcat, bsz, seq, st["q_col"])
        grads["w_mem_kv"][i] = _mm_tn("grad_w_mem_kv", memb, dmkv.astype(BF16))
        if dil_layer:
            early_sums = _chip_sums_of_early(grads) if (i == 0 and exchanging) else None
            dh_parts, exchanged = _band_attn_bwd_fused(st["h"], dcat, st["mix"], st["lse"], slopes, bsz, seq,
                                                       exchange=early_sums)
            if early_sums is not None:
                early_parts = exchanged
        else:
            ws_t = jnp.swapaxes(st["ws"], -1, -2)
            dh_main, dws, dbs_t, dlg, dlb = _sgu_bwd(st["h"], dcat, st["ws"], ws_t, st["bs_t"], st["ln_g"], st["ln_b"])
            grads["sgu_w_s"][j], grads["sgu_b_s"][j] = dws, dbs_t.T
            grads["sgu_ln_g"][j], grads["sgu_ln_b"][j] = dlg[0], dlb[0]
            dh_parts = [dh_main]
        name = "in_proj_bwd_a" if dil_layer else "in_proj_bwd_b"
        if i > 0:
            dr2, dr2b, dg, db = _linear_nt(name, [*dh_parts, dqm], w_in, dr1, F32,
                                           ln=(saved[i - 1]["r2"], ln_ffn_g[i - 1].reshape(1, D_MODEL)))
        else:
            grad_x = _linear_nt(name + "_x", [*dh_parts, dqm], w_in, dr1, F32).reshape(x.shape)
        grads["a_w_in" if dil_layer else "b_w_in"][j] = _mm_tn_parts(
            "grad_w_in_a" if dil_layer else "grad_w_in_b", st["xb"], [*dh_parts, dqm])
    return loss_part, grad_x, {n: jnp.stack(g) for n, g in grads.items()}, early_parts
```

```python
import functools
import math

import numpy as np
import jax
import jax.numpy as jnp
from jax import lax
from jax.experimental import pallas as pl
from jax.experimental.pallas import tpu as pltpu

F32 = jnp.float32
BF16 = jnp.bfloat16

D_MODEL = 1024
DEPTH = 4
N_MEM = 256
HEAD_DIM = 64
N_HEADS = 12
MIX_W = N_HEADS * HEAD_DIM
MEM_W = 4 * HEAD_DIM
DIL_PATTERNS = ((128, 1), (512, 4), (2048, 16))
BLK = 128
HEAD_GROUP = 4
N_GROUPS = N_HEADS // HEAD_GROUP
D_FF = 2816
FF_CHUNKS = 2
ALPHA = (2 * DEPTH) ** 0.25
LN_EPS = 1e-5
SCALE = HEAD_DIM ** -0.5
NEG = -1e30
N_DEV = 8

ADAM_LR, ADAM_B1, ADAM_B2, ADAM_EPS, ADAM_WD, ADAM_STEP = 0.001, 0.9, 0.999, 1e-08, 0.01, 10

VMEM_LIMIT = 56 * 2 ** 20
STAT_LANES = 32
STAT_W = N_HEADS * STAT_LANES


def _dot_nn(a, b):
    return lax.dot_general(a, b, (((1,), (0,)), ((), ())), preferred_element_type=F32)


def _dot_nt(a, b):
    return lax.dot_general(a, b, (((1,), (1,)), ((), ())), preferred_element_type=F32)


def _dot_tn(a, b):
    return lax.dot_general(a, b, (((0,), (0,)), ((), ())), preferred_element_type=F32)


def _ln_hat(r):
    mu = jnp.mean(r, axis=-1, keepdims=True)
    xc = r - mu
    var = jnp.mean(xc * xc, axis=-1, keepdims=True)
    rstd = lax.rsqrt(var + LN_EPS)
    return xc * rstd, rstd


def _params(sem):
    return pltpu.CompilerParams(dimension_semantics=sem, vmem_limit_bytes=VMEM_LIMIT)


def _rows(tm, c, col=0):
    return pl.BlockSpec((tm, c), lambda i: (i, col))


def _whole(shape):
    nd = len(shape)
    return pl.BlockSpec(tuple(shape), lambda *_: (0,) * nd)


def _resident(shape):
    nd = len(shape)
    return pl.BlockSpec(tuple(shape), lambda *_: (0,) * nd, pipeline_mode=pl.Buffered(1))


def _sds(shape, dtype):
    return jax.ShapeDtypeStruct(tuple(shape), dtype)


def _linear_nn(name, a, w, tm=512):
    t, k = a.shape
    n = w.shape[1]
    tm = min(tm, t)

    def body(a_ref, w_ref, o_ref):
        o_ref[...] = _dot_nn(a_ref[...], w_ref[...]).astype(BF16)

    return pl.pallas_call(
        body, name=name, grid=(t // tm,), in_specs=[_rows(tm, k), _resident(w.shape)], out_specs=_rows(tm, n),
        out_shape=_sds((t, n), BF16), compiler_params=_params(("parallel",)))(a, w)


def _linear_nn_gathered(name, a, shards, tm=512):
    t, k = a.shape
    n8 = shards.shape[2]
    n = N_DEV * n8

    def body(a_ref, s_ref, o_ref, w_ref):
        @pl.when(pl.program_id(0) == 0)
        def _():
            for s in range(N_DEV):
                w_ref[:, s * n8:(s + 1) * n8] = s_ref[s]

        o_ref[...] = _dot_nn(a_ref[...], w_ref[...]).astype(BF16)

    return pl.pallas_call(
        body, name=name, grid=(t // tm,), in_specs=[_rows(tm, k), _resident(shards.shape)],
        out_specs=[_rows(tm, n), _whole((k, n))], out_shape=[_sds((t, n), BF16), _sds((k, n), BF16)],
        compiler_params=_params(("arbitrary",)))(a, shards)


def _proj_ln_fwd(name, lhs, w, res, g, b, gather=None, tm=512):
    t = res[0].shape[0]
    n_lhs = len(lhs)
    n_res = len(res)
    ng = 0 if gather is None else len(gather)
    steps = t // tm
    n_in = n_lhs + 3 + n_res

    def body(*refs):
        lhs_refs = refs[:n_lhs]
        w_ref = refs[n_lhs]
        res_refs = refs[n_lhs + 1:n_lhs + 1 + n_res]
        g_ref, b_ref = refs[n_in - 2:n_in]
        r_ref, xnb_ref = refs[n_in + ng:n_in + ng + 2]
        if ng:
            start, finish = _gather_protocol(refs[n_in:n_in + ng], refs[n_in + ng + 2:n_in + 2 * ng + 2], *refs[n_in + 2 * ng + 2:])
            pl.when(pl.program_id(0) == 0)(start)
        y, off = None, 0
        for lr in lhs_refs:
            k = lr.shape[1]
            term = _dot_nn(lr[...], w_ref[off:off + k, :])
            y = term if y is None else y + term
            off += k
        x_res = res_refs[0][...]
        if n_res == 3:
            x_res = _ln_hat(x_res)[0] * res_refs[1][...] + res_refs[2][...]
        r = ALPHA * x_res + y
        r_ref[...] = r
        xnb_ref[...] = (_ln_hat(r)[0] * g_ref[...] + b_ref[...]).astype(BF16)
        if ng:
            pl.when(pl.program_id(0) == steps - 1)(finish)

    vec = _whole((1, D_MODEL))
    in_specs = ([_rows(tm, a.shape[1]) for a in lhs] + [_resident(w.shape), _rows(tm, D_MODEL)] + [vec] * (n_res - 1) + [vec, vec])
    outs = pl.pallas_call(
        body, name=name + "_gather" if ng else name, grid=(steps,), in_specs=in_specs + [_ANY] * ng,
        out_specs=[_rows(tm, D_MODEL)] * 2 + [_ANY] * ng,
        out_shape=[_sds((t, D_MODEL), F32), _sds((t, D_MODEL), BF16)] + (_gather_shapes(gather) if ng else []),
        scratch_shapes=_gather_sems(ng) if ng else [],
        compiler_params=_params(("arbitrary" if ng else "parallel",)))(*lhs, w, *res, g, b, *(gather or []))
    return outs[0], outs[1], list(outs[2:])


def _ffn_up_fwd(xb, gate_shards, up_shards, gather=None, tm=256):
    t = xb.shape[0]
    ng = 0 if gather is None else len(gather)
    steps = t // tm
    n8 = gate_shards.shape[2]

    def body(*refs):
        x_ref, gs_ref, us_ref = refs[:3]
        g_ref, u_ref, a_ref, wgu_ref = refs[3 + ng:7 + ng]
        w_ref, w_sem = refs[7 + 2 * ng:9 + 2 * ng]
        keep = pltpu.make_async_copy(w_ref, wgu_ref, w_sem)

        @pl.when(pl.program_id(0) == 0)
        def _():
            for s in range(N_DEV):
                w_ref[:, s * n8:(s + 1) * n8] = gs_ref[s]
                w_ref[:, D_FF + s * n8:D_FF + (s + 1) * n8] = us_ref[s]
            keep.start()

        if ng:
            start, finish = _gather_protocol(refs[3:3 + ng], refs[7 + ng:7 + 2 * ng], *refs[9 + 2 * ng:])
            pl.when(pl.program_id(0) == 0)(start)
        xv = x_ref[...]
        for c in range(FF_CHUNKS):
            cols = slice(c * D_FF // FF_CHUNKS, (c + 1) * D_FF // FF_CHUNKS)
            gt = _dot_nn(xv, w_ref[:, cols])
            up = _dot_nn(xv, w_ref[:, D_FF + cols.start:D_FF + cols.stop])
            sg = jax.nn.sigmoid(gt)
            silu = gt * sg
            g_ref[:, cols] = (up * (sg * (1.0 + gt * (1.0 - sg)))).astype(BF16)
            u_ref[:, cols] = silu.astype(BF16)
            a_ref[:, cols] = (silu * up).astype(BF16)
        if ng:
            pl.when(pl.program_id(0) == steps - 1)(finish)
        pl.when(pl.program_id(0) == steps - 1)(keep.wait)

    k = gate_shards.shape[1]
    outs = pl.pallas_call(
        body, name="ffn_up_fwd_gather" if ng else "ffn_up_fwd", grid=(steps,),
        in_specs=[_rows(tm, D_MODEL), _resident(gate_shards.shape), _resident(up_shards.shape)] + [_ANY] * ng,
        out_specs=[_rows(tm, D_FF)] * 3 + [_ANY] + [_ANY] * ng,
        out_shape=[_sds((t, D_FF), BF16)] * 3 + [_sds((k, 2 * D_FF), BF16)] + (_gather_shapes(gather) if ng else []),
        scratch_shapes=[pltpu.VMEM((k, 2 * D_FF), BF16), pltpu.SemaphoreType.DMA(())] + (_gather_sems(ng) if ng else []),
        compiler_params=_params(("arbitrary",)))(xb, gate_shards, up_shards, *(gather or []))
    return outs[0], outs[1], outs[2], outs[3], list(outs[4:])


def _ln_bwd_rows(dxn, xhat, rstd, g_ref, dr_ref, drb_ref, dg_ref, db_ref):
    @pl.when(pl.program_id(0) == 0)
    def _():
        dg_ref[...] = jnp.zeros_like(dg_ref)
        db_ref[...] = jnp.zeros_like(db_ref)

    dxh = dxn * g_ref[...]
    m1 = jnp.mean(dxh, axis=-1, keepdims=True)
    m2 = jnp.mean(dxh * xhat, axis=-1, keepdims=True)
    dr = rstd * (dxh - m1 - xhat * m2)
    dr_ref[...] = dr
    drb_ref[...] = dr.astype(BF16)
    dg_ref[...] += jnp.sum(dxn * xhat, axis=0, keepdims=True)
    db_ref[...] += jnp.sum(dxn, axis=0, keepdims=True)


def _ln_bwd_outs(t, tm):
    vec = _whole((1, D_MODEL))
    specs = [_rows(tm, D_MODEL), _rows(tm, D_MODEL), vec, vec]
    shapes = [_sds((t, D_MODEL), F32), _sds((t, D_MODEL), BF16), _sds((1, D_MODEL), F32), _sds((1, D_MODEL), F32)]
    return specs, shapes


def _loss_ln_bwd(r, g, b, tgt, tm=512):
    t = r.shape[0]

    def body(r_ref, g_ref, b_ref, t_ref, dr_ref, drb_ref, dg_ref, db_ref, l_ref):
        @pl.when(pl.program_id(0) == 0)
        def _():
            l_ref[...] = jnp.zeros_like(l_ref)

        xhat, rstd = _ln_hat(r_ref[...])
        e = xhat * g_ref[...] + b_ref[...] - t_ref[...]
        l_ref[...] += jnp.sum(e * e) * (0.5 / D_MODEL)
        _ln_bwd_rows(e * (1.0 / D_MODEL), xhat, rstd, g_ref, dr_ref, drb_ref, dg_ref, db_ref)

    vec = _whole((1, D_MODEL))
    specs, shapes = _ln_bwd_outs(t, tm)
    return pl.pallas_call(
        body, name="loss_ln_bwd", grid=(t // tm,), in_specs=[_rows(tm, D_MODEL), vec, vec, _rows(tm, D_MODEL)],
        out_specs=specs + [_whole((1, 128))], out_shape=shapes + [_sds((1, 128), F32)],
        compiler_params=_params(("arbitrary",)))(r, g, b, tgt)


def _ffn_down_bwd(drb, wd, gt, up, tm=512):
    t = drb.shape[0]

    def body(d_ref, w_ref, g_ref, u_ref, o_ref):
        dv = d_ref[...]
        for c in range(FF_CHUNKS):
            cols = slice(c * D_FF // FF_CHUNKS, (c + 1) * D_FF // FF_CHUNKS)
            da = _dot_nt(dv, w_ref[cols, :])
            o_ref[:, cols] = (da * g_ref[:, cols].astype(F32)).astype(BF16)
            o_ref[:, D_FF + cols.start:D_FF + cols.stop] = (da * u_ref[:, cols].astype(F32)).astype(BF16)

    return pl.pallas_call(
        body, name="ffn_down_bwd", grid=(t // tm,),
        in_specs=[_rows(tm, D_MODEL), _resident(wd.shape), _rows(tm, D_FF), _rows(tm, D_FF)],
        out_specs=_rows(tm, 2 * D_FF), out_shape=_sds((t, 2 * D_FF), BF16),
        compiler_params=_params(("parallel",)))(drb, wd, gt, up)


def _linear_nt(name, lhs, w, res, out_dtype, ln=None, tm=512):
    t = lhs[0].shape[0]
    n_lhs = len(lhs)
    n_out = w.shape[0]
    n_in = n_lhs + 1 + (res is not None) + (2 if ln else 0)

    def body(*refs):
        lhs_refs = refs[:n_lhs]
        w_ref = refs[n_lhs]
        y, off = None, 0
        for lr in lhs_refs:
            k = lr.shape[1]
            term = _dot_nt(lr[...], w_ref[:, off:off + k])
            y = term if y is None else y + term
            off += k
        if res is not None:
            y = ALPHA * refs[n_lhs + 1][...] + y
        if ln is None:
            refs[-1][...] = y.astype(out_dtype)
        else:
            r_ref, g_ref = refs[n_in - 2:n_in]
            xhat, rstd = _ln_hat(r_ref[...])
            _ln_bwd_rows(y, xhat, rstd, g_ref, *refs[n_in:])

    in_specs = [_rows(tm, a.shape[1]) for a in lhs] + [_resident(w.shape)]
    args = list(lhs) + [w]
    if res is not None:
        in_specs.append(_rows(tm, n_out))
        args.append(res)
    if ln is None:
        out_specs, out_shape, sem = _rows(tm, n_out), _sds((t, n_out), out_dtype), "parallel"
    else:
        in_specs += [_rows(tm, D_MODEL), _whole((1, D_MODEL))]
        args += list(ln)
        (out_specs, out_shape), sem = _ln_bwd_outs(t, tm), "arbitrary"
    return pl.pallas_call(
        body, name=name, grid=(t // tm,), in_specs=in_specs, out_specs=out_specs, out_shape=out_shape,
        compiler_params=_params((sem,)))(*args)


def _pick_tile(n, limit):
    if n <= limit:
        return n
    best = 128
    for cand in range(128, limit + 1, 128):
        if n % cand == 0:
            best = cand
    return best


def _mm_tn(name, a, b, cols=None, tt=1024):
    t, k = a.shape
    first_col, n = (0, b.shape[1]) if cols is None else (cols[0], cols[1] - cols[0])
    tt = min(tt, t)
    tk = _pick_tile(k, 1408)
    tn = _pick_tile(n, (6 * 2 ** 20) // (4 * tk) // 128 * 128)
    steps = t // tt
    first_block = first_col // tn
    assert first_block * tn == first_col, (first_col, tn)

    def body(a_ref, b_ref, o_ref, acc_ref):
        @pl.when(pl.program_id(2) == 0)
        def _():
            acc_ref[...] = jnp.zeros_like(acc_ref)

        acc_ref[...] += _dot_tn(a_ref[...], b_ref[...])

        @pl.when(pl.program_id(2) == steps - 1)
        def _():
            o_ref[...] = acc_ref[...].astype(BF16)

    return pl.pallas_call(
        body, name=name, grid=(k // tk, n // tn, steps),
        in_specs=[pl.BlockSpec((tt, tk), lambda i, j, s: (s, i)), pl.BlockSpec((tt, tn), lambda i, j, s: (s, first_block + j))],
        out_specs=pl.BlockSpec((tk, tn), lambda i, j, s: (i, j)), out_shape=_sds((k, n), BF16),
        scratch_shapes=[pltpu.VMEM((tk, tn), F32)],
        compiler_params=_params(("parallel", "parallel", "arbitrary")))(a, b)


def _mm_tn_parts(name, a, parts, split_rows=False, tt=1024):
    t, k = a.shape
    widths = [p.shape[1] for p in parts]
    n_parts = len(parts)
    steps = t // tt
    shape = (sum(widths), k) if split_rows else (k, sum(widths))

    def body(*refs):
        a_ref, b_refs, o_ref, acc_ref = refs[0], refs[1:1 + n_parts], refs[1 + n_parts], refs[2 + n_parts]

        @pl.when(pl.program_id(0) == 0)
        def _():
            acc_ref[...] = jnp.zeros_like(acc_ref)

        av, off = a_ref[...], 0
        for b_ref, width in zip(b_refs, widths):
            if split_rows:
                acc_ref[off:off + width, :] += _dot_tn(b_ref[...], av)
            else:
                acc_ref[:, off:off + width] += _dot_tn(av, b_ref[...])
            off += width

        @pl.when(pl.program_id(0) == steps - 1)
        def _():
            o_ref[...] = acc_ref[...].astype(BF16)

    return pl.pallas_call(
        body, name=name, grid=(steps,), in_specs=[_rows(tt, k)] + [_rows(tt, w) for w in widths],
        out_specs=_whole(shape), out_shape=_sds(shape, BF16), scratch_shapes=[pltpu.VMEM(shape, F32)],
        compiler_params=_params(("arbitrary",)))(a, *parts)


def _alibi_table():
    arr = np.zeros((N_GROUPS, 8, 128), np.float32)
    for g in range(N_GROUPS):
        for hh in range(HEAD_GROUP):
            arr[g, hh, :] = 2.0 ** (-8.0 * (g * HEAD_GROUP + hh + 1) / N_HEADS)
    return jnp.asarray(arr)


def _spread_stats(cols, per_head=STAT_LANES):
    lane = lax.broadcasted_iota(jnp.int32, (BLK, HEAD_GROUP * per_head), 1)
    tile = cols[HEAD_GROUP - 1]
    for hh in range(HEAD_GROUP - 2, -1, -1):
        tile = jnp.where(lane < (hh + 1) * per_head, cols[hh], tile)
    return tile


def _block_mask(has_prev, dil):
    if has_prev is None:
        steps = lax.broadcasted_iota(jnp.int32, (BLK, BLK), 0) - lax.broadcasted_iota(jnp.int32, (BLK, BLK), 1)
        return steps >= 0, (steps * dil).astype(F32)
    qi = lax.broadcasted_iota(jnp.int32, (BLK, 2 * BLK), 0)
    ki = lax.broadcasted_iota(jnp.int32, (BLK, 2 * BLK), 1)
    steps = qi + BLK - ki
    valid = (steps >= 0) & (steps <= BLK) & ((ki >= BLK) | has_prev)
    return valid, (steps * dil).astype(F32)


def _bias_scratch():
    return pltpu.VMEM((2, HEAD_GROUP, BLK, 2 * BLK), F32)


def _fill_bias(bias, sl_ref, dil):
    for p in range(2):
        valid, dist = _block_mask(p == 1, dil)
        for hh in range(HEAD_GROUP):
            bias[p, hh] = jnp.where(valid, -sl_ref[hh:hh + 1, 0:1] * dist, NEG)


def _rows_of(j):
    return pl.ds(pl.multiple_of(j * BLK, BLK), BLK)


def _lane_half(hf):
    return slice(hf * 128, (hf + 1) * 128)


def _split_pair(x):
    first = lax.broadcasted_iota(jnp.int32, (1, 2 * HEAD_DIM), 1) < HEAD_DIM
    zero = jnp.zeros_like(x)
    return jnp.where(first, x, zero), jnp.where(first, zero, x)


def _deinterleave(src, dst, seq, dil, dtype):
    length = seq // dil
    for r in range(dil):
        for c in range(length // BLK):
            rows = pl.ds(r + c * BLK * dil, BLK, stride=dil)
            out = slice(r * length + c * BLK, r * length + (c + 1) * BLK)
            if len(src.shape) == 2:
                dst[out, :] = src[rows, :].astype(dtype)
            else:
                for hf in range(2):
                    dst[out, _lane_half(hf)] = src.at[hf][rows, :].astype(dtype)


def _interleave(src, dst, seq, dil, accumulate):
    length = seq // dil
    for r in range(dil):
        for c in range(length // BLK):
            rows = pl.ds(r + c * BLK * dil, BLK, stride=dil)
            inp = slice(r * length + c * BLK, r * length + (c + 1) * BLK)
            if len(dst.shape) == 2:
                dst[rows, :] = dst[rows, :] + src[inp, :] if accumulate else src[inp, :]
            else:
                for hf in range(2):
                    val = src[inp, _lane_half(hf)]
                    half = dst.at[hf]
                    half[rows, :] = half[rows, :] + val if accumulate else val


def _split_halves(src, dst, seq):
    def step(i, carry):
        for hf in range(2):
            dst[hf, _rows_of(i), :] = src[_rows_of(i), _lane_half(hf)].astype(F32)
        return carry

    lax.fori_loop(0, seq // BLK, step, 0)


def _band_attn_fwd_fused(h, slopes, bsz, seq, gather=None):
    width = h.shape[1]
    cb = width // 256
    k_off, v_off = MIX_W // 256, 2 * MIX_W // 256
    nb = seq // BLK

    ng = 0 if gather is None else len(gather)

    def body(*refs):
        sl_ref, q_ref, k_ref, v_ref = refs[:4]
        mix_ref, lse_ref = refs[4 + ng:6 + ng]
        qf, kf, vf, qd, kd, vd, od, ld, o1, o2, o3, l1, l2, l3, bias = refs[6 + 2 * ng:21 + 2 * ng]
        if ng:
            start, finish = _gather_protocol(refs[4:4 + ng], refs[6 + ng:6 + 2 * ng], *refs[21 + 2 * ng:])
            pl.when((pl.program_id(0) == 0) & (pl.program_id(1) == 0))(start)

        def run(dil, qs, ks, vs, o_dst, l_dst):
            nblk = seq // dil // BLK
            _fill_bias(bias, sl_ref, dil)

            def block(j, carry):
                rows, prows = _rows_of(j), _rows_of(jnp.maximum(j - 1, 0))
                has_prev = ((j % nblk) != 0).astype(jnp.int32)

                def keys(ref, lanes):
                    return jnp.concatenate([ref[prows, lanes], ref[rows, lanes]], axis=0)

                lses = []
                for pr in range(HEAD_GROUP // 2):
                    lanes = _lane_half(pr)
                    q_ab = _split_pair(qs[rows, lanes] * SCALE)
                    k2 = keys(ks, lanes)
                    v_ab = _split_pair(keys(vs, lanes))
                    out = None
                    for ab in range(2):
                        hh = 2 * pr + ab
                        s = _dot_nt(q_ab[ab], k2) + bias[has_prev, hh]
                        m = jnp.max(s, axis=-1, keepdims=True)
                        p = jnp.exp(s - m)
                        l = jnp.sum(p, axis=-1, keepdims=True)
                        term = _dot_nn(p.astype(BF16), v_ab[ab]) / l
                        out = term if out is None else out + term
                        lses.append(m + jnp.log(l))
                    o_dst[rows, lanes] = out
                l_dst[rows, :] = _spread_stats(lses, HEAD_DIM)
                return carry

            lax.fori_loop(0, nb, block, 0, unroll=8)

        run(1, q_ref, k_ref, v_ref, o1, l1)
        _split_halves(q_ref, qf, seq)
        _split_halves(k_ref, kf, seq)
        _split_halves(v_ref, vf, seq)
        for dil, o_tok, l_tok in ((4, o2, l2), (16, o3, l3)):
            _deinterleave(qf, qd, seq, dil, BF16)
            _deinterleave(kf, kd, seq, dil, BF16)
            _deinterleave(vf, vd, seq, dil, BF16)
            run(dil, qd, kd, vd, od, ld)
            _interleave(od, o_tok, seq, dil, False)
            _interleave(ld, l_tok, seq, dil, False)

        def merge(i, carry):
            rows = _rows_of(i)

            def both(ref):
                return jnp.concatenate([ref[0, rows, :], ref[1, rows, :]], axis=1)

            ls = [l1[rows, :], both(l2), both(l3)]
            m = jnp.maximum(jnp.maximum(ls[0], ls[1]), ls[2])
            tot = m + jnp.log(jnp.exp(ls[0] - m) + jnp.exp(ls[1] - m) + jnp.exp(ls[2] - m))
            ws = [jnp.exp(x - tot) for x in ls]
            mix_ref[rows, :] = (ws[0] * o1[rows, :] + ws[1] * both(o2) + ws[2] * both(o3)).astype(BF16)
            lse_ref[rows, :] = _spread_stats([tot[:, hh * HEAD_DIM:hh * HEAD_DIM + 1] for hh in range(HEAD_GROUP)])
            return carry

        lax.fori_loop(0, nb, merge, 0, unroll=2)
        if ng:
            pl.when((pl.program_id(0) == bsz - 1) & (pl.program_id(1) == N_GROUPS - 1))(finish)

    def hspec(off):
        return pl.BlockSpec((seq, 256), lambda b, g: (b, off + g))

    big = lambda dt: pltpu.VMEM((seq, 256), dt)
    halves = lambda: pltpu.VMEM((2, seq, 128), F32)
    stat = lambda: pltpu.VMEM((seq, 128), F32)
    outs = pl.pallas_call(
        body, name="band_attn_fwd_gather" if ng else "band_attn_fwd", grid=(bsz, N_GROUPS),
        in_specs=[pl.BlockSpec((None, 8, 128), lambda b, g: (g, 0, 0)), hspec(0), hspec(k_off), hspec(v_off)] + [_ANY] * ng,
        out_specs=[pl.BlockSpec((seq, 256), lambda b, g: (b, g)), pl.BlockSpec((seq, 128), lambda b, g: (b, g))] + [_ANY] * ng,
        out_shape=[_sds((bsz * seq, MIX_W), BF16), _sds((bsz * seq, STAT_W), F32)] + (_gather_shapes(gather) if ng else []),
        scratch_shapes=[halves(), halves(), halves(), big(BF16), big(BF16), big(BF16), big(F32), big(F32),
                        big(F32), halves(), halves(), big(F32), halves(), halves(), _bias_scratch()]
        + (_gather_sems(ng) if ng else []),
        compiler_params=_params(("arbitrary", "arbitrary")))(slopes, h, h, h, *(gather or []))
    return outs[0], outs[1], list(outs[2:])


def _band_attn_bwd_fused(h, dcat, mix, lse, slopes, bsz, seq, exchange=None):
    width = h.shape[1]
    k_off, v_off = MIX_W // 256, 2 * MIX_W // 256
    nb = seq // BLK

    ne = 0 if exchange is None else len(exchange)

    def body(*refs):
        sl_ref, q_ref, k_ref, v_ref, do_ref, o_ref, lse_ref = refs[:7]
        dq_ref, dk_ref, dv_ref = refs[7 + ne:10 + ne]
        qf, kf, vf, dof, ddt, qd, kd, vd, dod, lsd, ddd, gq, gk, gv, aq, ak, av, bias = refs[10 + 2 * ne:28 + 2 * ne]
        if ne:
            start, finish = _exchange_protocol(refs[7:7 + ne], refs[10 + ne:10 + 2 * ne], *refs[28 + 2 * ne:])
            pl.when((pl.program_id(0) == 0) & (pl.program_id(1) == 0))(start)

        same_head = (lax.broadcasted_iota(jnp.int32, (HEAD_GROUP * HEAD_DIM, HEAD_GROUP * STAT_LANES), 0) // HEAD_DIM
                     == lax.broadcasted_iota(jnp.int32, (HEAD_GROUP * HEAD_DIM, HEAD_GROUP * STAT_LANES), 1) // STAT_LANES)
        ones_map = jnp.where(same_head, 1.0, 0.0).astype(BF16)

        def delta(i, carry):
            rows = _rows_of(i)
            prod = do_ref[rows, :].astype(F32) * o_ref[rows, :].astype(F32)
            high = prod.astype(BF16)
            rest = (prod - high.astype(F32)).astype(BF16)
            ddt[rows, :] = _dot_nn(high, ones_map) + _dot_nn(rest, ones_map)
            return carry

        lax.fori_loop(0, nb, delta, 0, unroll=2)

        def zero(i, carry):
            rows = _rows_of(i)
            for ref in (gk, gv):
                ref[rows, :] = jnp.zeros((BLK, 256), F32)
            return carry

        def run(dil, qs, ks, vs, dos, lss, dds):
            nblk = seq // dil // BLK
            _fill_bias(bias, sl_ref, dil)
            if nblk > 1:
                lax.fori_loop(0, nb, zero, 0)

            def block(j, carry):
                rows, prows = _rows_of(j), _rows_of(jnp.maximum(j - 1, 0))
                has_prev = ((j % nblk) != 0).astype(jnp.int32)

                def keys(ref, lanes):
                    if nblk == 1:
                        return ref[rows, lanes]
                    return jnp.concatenate([ref[prows, lanes], ref[rows, lanes]], axis=0)

                for pr in range(HEAD_GROUP // 2):
                    lanes = _lane_half(pr)
                    q_ab = _split_pair(qs[rows, lanes] * SCALE)
                    do_ab = _split_pair(dos[rows, lanes])
                    k2, v2 = keys(ks, lanes), keys(vs, lanes)
                    k_ab = _split_pair(k2)
                    dq, dk2, dv2 = None, None, None
                    for ab in range(2):
                        hh = 2 * pr + ab
                        st = slice(hh * STAT_LANES, hh * STAT_LANES + 1)
                        s = _dot_nt(q_ab[ab], k2) + (bias[0, hh, :, BLK:] if nblk == 1 else bias[has_prev, hh])
                        p = jnp.exp(s - lss[rows, st])
                        dp = _dot_nt(do_ab[ab], v2)
                        ds = (p * (dp - dds[rows, st])).astype(BF16)
                        terms = (_dot_nn(ds, k_ab[ab]), _dot_tn(ds, q_ab[ab]), _dot_tn(p.astype(BF16), do_ab[ab]))
                        dq, dk2, dv2 = terms if dq is None else (dq + terms[0], dk2 + terms[1], dv2 + terms[2])
                    gq[rows, lanes] = dq * SCALE
                    if nblk == 1:
                        gk[rows, lanes] = dk2
                        gv[rows, lanes] = dv2
                    else:
                        gk[prows, lanes] += dk2[:BLK]
                        gv[prows, lanes] += dv2[:BLK]
                        gk[rows, lanes] += dk2[BLK:]
                        gv[rows, lanes] += dv2[BLK:]
                return carry

            lax.fori_loop(0, nb, block, 0, unroll=4)

        run(1, q_ref, k_ref, v_ref, do_ref, lse_ref, ddt)

        for src, dst in ((gq, aq), (gk, ak), (gv, av), (q_ref, qf), (k_ref, kf), (v_ref, vf), (do_ref, dof)):
            _split_halves(src, dst, seq)
        for dil in (4, 16):
            for src, dst in ((qf, qd), (kf, kd), (vf, vd), (dof, dod)):
                _deinterleave(src, dst, seq, dil, BF16)
            _deinterleave(lse_ref, lsd, seq, dil, F32)
            _deinterleave(ddt, ddd, seq, dil, F32)
            run(dil, qd, kd, vd, dod, lsd, ddd)
            for src, dst in ((gq, aq), (gk, ak), (gv, av)):
                _interleave(src, dst, seq, dil, True)

        def write(i, carry):
            rows = _rows_of(i)
            for src, dst in ((aq, dq_ref), (ak, dk_ref), (av, dv_ref)):
                for hf in range(2):
                    dst[rows, _lane_half(hf)] = src[hf, rows, :].astype(BF16)
            return carry

        lax.fori_loop(0, nb, write, 0)
        if ne:
            pl.when((pl.program_id(0) == bsz - 1) & (pl.program_id(1) == N_GROUPS - 1))(finish)

    def hspec(off):
        return pl.BlockSpec((seq, 256), lambda b, g: (b, off + g))

    io = pl.BlockSpec((seq, 256), lambda b, g: (b, g))
    big = lambda dt: pltpu.VMEM((seq, 256), dt)
    halves = lambda: pltpu.VMEM((2, seq, 128), F32)
    stat = lambda: pltpu.VMEM((seq, 128), F32)
    outs = pl.pallas_call(
        body, name="band_attn_bwd_exchange" if ne else "band_attn_bwd", grid=(bsz, N_GROUPS),
        in_specs=[pl.BlockSpec((None, 8, 128), lambda b, g: (g, 0, 0)), hspec(0), hspec(k_off), hspec(v_off), io, io,
                  pl.BlockSpec((seq, 128), lambda b, g: (b, g))] + [_ANY] * ne,
        out_specs=[io, io, io] + [_ANY] * ne,
        out_shape=[_sds((bsz * seq, MIX_W), BF16)] * 3 + [_sds(s.shape, s.dtype) for s in (exchange or [])],
        scratch_shapes=[halves(), halves(), halves(), halves(), stat(),
                        big(BF16), big(BF16), big(BF16), big(BF16), stat(), stat(),
                        big(F32), big(F32), big(F32), halves(), halves(), halves(), _bias_scratch()]
        + (_exchange_sems(ne) if ne else []),
        compiler_params=_params(("arbitrary", "arbitrary")))(slopes, h, h, h, dcat, mix, lse, *(exchange or []))
    return list(outs[:3]), list(outs[3:])


def _mem_attn_fwd(h, mkv, bsz, seq, q_col, tq=1024):
    nq = seq // tq

    def body(q_ref, kv_ref, o_ref):
        for pr in range(2):
            lanes = _lane_half(pr)
            q_ab = _split_pair(q_ref[:, lanes])
            k = kv_ref[:, lanes]
            v_ab = _split_pair(kv_ref[:, MEM_W + pr * 128:MEM_W + (pr + 1) * 128])
            out = None
            for ab in range(2):
                s = _dot_nt(q_ab[ab], k) * SCALE
                m = jnp.max(s, axis=-1, keepdims=True)
                p = jnp.exp(s - m)
                l = jnp.sum(p, axis=-1, keepdims=True)
                term = _dot_nn(p.astype(BF16), v_ab[ab]) / l
                out = term if out is None else out + term
            o_ref[:, lanes] = out.astype(BF16)

    return pl.pallas_call(
        body, name="mem_attn_fwd", grid=(bsz, nq),
        in_specs=[pl.BlockSpec((tq, MEM_W), lambda b, i: (b * nq + i, q_col)),
                  pl.BlockSpec((N_MEM, 2 * MEM_W), lambda b, i: (b, 0))],
        out_specs=pl.BlockSpec((tq, MEM_W), lambda b, i: (b * nq + i, 0)),
        out_shape=_sds((bsz * seq, MEM_W), BF16), compiler_params=_params(("parallel", "parallel")))(h, mkv)


def _mem_attn_bwd(h, mkv, dcat, bsz, seq, q_col, tq=1024):
    nq = seq // tq
    do_col = MIX_W // MEM_W

    def body(q_ref, kv_ref, do_ref, dq_ref, dkv_ref):
        @pl.when(pl.program_id(1) == 0)
        def _():
            dkv_ref[...] = jnp.zeros_like(dkv_ref)

        for pr in range(2):
            lanes = _lane_half(pr)
            vlanes = slice(MEM_W + pr * 128, MEM_W + (pr + 1) * 128)
            q_ab = _split_pair(q_ref[:, lanes])
            do_ab = _split_pair(do_ref[:, lanes])
            k, v = kv_ref[:, lanes], kv_ref[:, vlanes]
            k_ab = _split_pair(k)
            dq, dk, dv = None, None, None
            for ab in range(2):
                s = _dot_nt(q_ab[ab], k) * SCALE
                m = jnp.max(s, axis=-1, keepdims=True)
                e = jnp.exp(s - m)
                p = e / jnp.sum(e, axis=-1, keepdims=True)
                dp = _dot_nt(do_ab[ab], v)
                dd = jnp.sum(p * dp, axis=-1, keepdims=True)
                ds = (p * (dp - dd) * SCALE).astype(BF16)
                terms = (_dot_nn(ds, k_ab[ab]), _dot_tn(ds, q_ab[ab]), _dot_tn(p.astype(BF16), do_ab[ab]))
                dq, dk, dv = terms if dq is None else (dq + terms[0], dk + terms[1], dv + terms[2])
            dq_ref[:, lanes] = dq.astype(BF16)
            dkv_ref[:, lanes] += dk
            dkv_ref[:, vlanes] += dv

    return pl.pallas_call(
        body, name="mem_attn_bwd", grid=(bsz, nq),
        in_specs=[pl.BlockSpec((tq, MEM_W), lambda b, i: (b * nq + i, q_col)),
                  pl.BlockSpec((N_MEM, 2 * MEM_W), lambda b, i: (b, 0)),
                  pl.BlockSpec((tq, MEM_W), lambda b, i: (b * nq + i, do_col))],
        out_specs=[pl.BlockSpec((tq, MEM_W), lambda b, i: (b * nq + i, 0)),
                   pl.BlockSpec((N_MEM, 2 * MEM_W), lambda b, i: (b, 0))],
        out_shape=[_sds((bsz * seq, MEM_W), BF16), _sds((bsz * N_MEM, 2 * MEM_W), F32)],
        compiler_params=_params(("parallel", "arbitrary")))(h, mkv, dcat)


_GELU_C = math.sqrt(2.0 / math.pi)
_GELU_A = 0.044715


def _gelu(x):
    return 0.5 * x * (1.0 + jnp.tanh(_GELU_C * (x + _GELU_A * x * x * x)))


def _gelu_grad(x):
    th = jnp.tanh(_GELU_C * (x + _GELU_A * x * x * x))
    return 0.5 * (1.0 + th) + 0.5 * x * (1.0 - th * th) * (_GELU_C * (1.0 + 3.0 * _GELU_A * x * x))


def _tril_mask(lower):
    ri = lax.broadcasted_iota(jnp.int32, (BLK, BLK), 0)
    ci = lax.broadcasted_iota(jnp.int32, (BLK, BLK), 1)
    return (ri >= ci) if lower else (ci >= ri)


def _sgu_fwd(h, ws, bs_t, ln_g, ln_b, tm=512):
    t = h.shape[0]

    def body(u_ref, v_ref, ws_ref, bs_ref, g_ref, b_ref, o_ref):
        ug = _gelu(u_ref[...].astype(F32))
        vhat, _ = _ln_hat(_gelu(v_ref[...].astype(F32)))
        vn = (vhat * g_ref[...] + b_ref[...]).astype(BF16)
        mask = _tril_mask(True)
        first = lax.broadcasted_iota(jnp.int32, (1, 2 * HEAD_DIM), 1) < HEAD_DIM
        for pr in range(N_HEADS // 2):
            lanes = _lane_half(pr)
            w_ab = [jnp.where(mask, ws_ref[2 * pr + ab], 0).astype(BF16) for ab in range(2)]
            bias = jnp.where(first, bs_ref[:, 2 * pr:2 * pr + 1], bs_ref[:, 2 * pr + 1:2 * pr + 2])
            for c in range(tm // BLK):
                rs = slice(c * BLK, (c + 1) * BLK)
                v_ab = _split_pair(vn[rs, lanes])
                mixed = _dot_nn(w_ab[0], v_ab[0]) + _dot_nn(w_ab[1], v_ab[1]) + bias
                o_ref[rs, lanes] = (ug[rs, lanes] * mixed).astype(BF16)

    return pl.pallas_call(
        body, name="sgu_fwd", grid=(t // tm,),
        in_specs=[_rows(tm, MIX_W, 0), _rows(tm, MIX_W, 1), _whole(ws.shape), _whole(bs_t.shape), _whole(ln_g.shape), _whole(ln_b.shape)],
        out_specs=_rows(tm, MIX_W), out_shape=_sds((t, MIX_W), BF16),
        compiler_params=_params(("parallel",)))(h, h, ws, bs_t, ln_g, ln_b)


def _sgu_bwd(h, dcat, ws, ws_t, bs_t, ln_g, ln_b, tm=512):
    t = h.shape[0]

    def body(u_ref, v_ref, do_ref, ws_ref, wst_ref, bs_ref, g_ref, b_ref, dh_ref, dws_ref, dbs_ref, dg_ref, db_ref, dvn_ref):
        @pl.when(pl.program_id(0) == 0)
        def _():
            dws_ref[...] = jnp.zeros_like(dws_ref)
            dbs_ref[...] = jnp.zeros_like(dbs_ref)
            dg_ref[...] = jnp.zeros_like(dg_ref)
            db_ref[...] = jnp.zeros_like(db_ref)

        u = u_ref[...].astype(F32)
        v = v_ref[...].astype(F32)
        do = do_ref[...].astype(F32)
        ug = _gelu(u)
        vhat, rstd = _ln_hat(_gelu(v))
        vn = (vhat * g_ref[...] + b_ref[...]).astype(BF16)
        dmixed_f = do * ug
        dmixed = dmixed_f.astype(BF16)
        low, upp = _tril_mask(True), _tril_mask(False)
        first = lax.broadcasted_iota(jnp.int32, (1, 2 * HEAD_DIM), 1) < HEAD_DIM
        for pr in range(N_HEADS // 2):
            lanes = _lane_half(pr)
            w_ab = [jnp.where(low, ws_ref[2 * pr + ab], 0).astype(BF16) for ab in range(2)]
            wt_ab = [jnp.where(upp, wst_ref[2 * pr + ab], 0).astype(BF16) for ab in range(2)]
            bias = jnp.where(first, bs_ref[:, 2 * pr:2 * pr + 1], bs_ref[:, 2 * pr + 1:2 * pr + 2])
            dws_acc = [None, None]
            dbs_acc = [None, None]
            for c in range(tm // BLK):
                rs = slice(c * BLK, (c + 1) * BLK)
                vn_pair = vn[rs, lanes]
                v_ab = _split_pair(vn_pair)
                mixed = _dot_nn(w_ab[0], v_ab[0]) + _dot_nn(w_ab[1], v_ab[1]) + bias
                dh_ref[rs, lanes] = (do[rs, lanes] * mixed * _gelu_grad(u[rs, lanes])).astype(BF16)
                dm_ab = _split_pair(dmixed[rs, lanes])
                dmf_ab = _split_pair(dmixed_f[rs, lanes])
                for ab in range(2):
                    term = _dot_nt(dm_ab[ab], vn_pair)
                    dws_acc[ab] = term if dws_acc[ab] is None else dws_acc[ab] + term
                    rsum = jnp.sum(dmf_ab[ab], axis=-1, keepdims=True)
                    dbs_acc[ab] = rsum if dbs_acc[ab] is None else dbs_acc[ab] + rsum
                dvn_ref[rs, lanes] = _dot_nn(wt_ab[0], dm_ab[0]) + _dot_nn(wt_ab[1], dm_ab[1])
            for ab in range(2):
                g = 2 * pr + ab
                dws_ref[g] += jnp.where(low, dws_acc[ab], 0.0)
                dbs_ref[:, g:g + 1] += dbs_acc[ab]
        dvn = dvn_ref[...]
        dg_ref[...] += jnp.sum(dvn * vhat, axis=0, keepdims=True)
        db_ref[...] += jnp.sum(dvn, axis=0, keepdims=True)
        dxh = dvn * g_ref[...]
        m1 = jnp.mean(dxh, axis=-1, keepdims=True)
        m2 = jnp.mean(dxh * vhat, axis=-1, keepdims=True)
        dvg = rstd * (dxh - m1 - vhat * m2)
        dh_ref[:, MIX_W:] = (dvg * _gelu_grad(v)).astype(BF16)

    return pl.pallas_call(
        body, name="sgu_bwd", grid=(t // tm,),
        in_specs=[_rows(tm, MIX_W, 0), _rows(tm, MIX_W, 1), _rows(tm, MIX_W, 0), _whole(ws.shape), _whole(ws_t.shape),
                  _whole(bs_t.shape), _whole(ln_g.shape), _whole(ln_b.shape)],
        out_specs=[_rows(tm, 2 * MIX_W), _whole(ws.shape), _whole(bs_t.shape), _whole((1, MIX_W)), _whole((1, MIX_W))],
        out_shape=[_sds((t, 2 * MIX_W), BF16), _sds(ws.shape, F32), _sds(bs_t.shape, F32), _sds((1, MIX_W), F32), _sds((1, MIX_W), F32)],
        scratch_shapes=[pltpu.VMEM((tm, MIX_W), F32)],
        compiler_params=_params(("arbitrary",)))(h, h, dcat, ws, ws_t, bs_t, ln_g, ln_b)


def _row_tile(rows, cols, itemsize=4, limit=2 ** 20):
    best = rows
    for cand in (4096, 2048, 1024, 512, 256, 128, 64, 32, 16):
        if rows % cand == 0 and rows > cand:
            best = cand
            if cand * cols * itemsize <= limit:
                break
    return best


def _adamw(w, m, v, grad=None, parts=None, first_parts=None):
    rows, cols = w.shape
    rows0 = 0 if first_parts is None else first_parts.shape[1]
    tr = _row_tile(rows0 if rows0 else rows, cols)
    n0 = rows0 // tr

    def chip_sum(ref):
        acc = ref[0].astype(F32)
        for k in range(1, 4):
            acc = acc + ref[k].astype(F32)
        return acc

    def body(*refs):
        w_ref, m_ref, v_ref = refs[:3]
        go_ref, d_ref, nm_ref, nv_ref = refs[-4:]
        if parts is None:
            gv = refs[3][...]
        elif first_parts is None:
            gv = chip_sum(refs[3])
        else:
            gv = jnp.where(pl.program_id(0) < n0, chip_sum(refs[3]), chip_sum(refs[4]))
        nm = ADAM_B1 * m_ref[...] + (1.0 - ADAM_B1) * gv
        nv = ADAM_B2 * v_ref[...] + (1.0 - ADAM_B2) * (gv * gv)
        m_hat = nm / (1.0 - ADAM_B1 ** ADAM_STEP)
        v_hat = nv / (1.0 - ADAM_B2 ** ADAM_STEP)
        go_ref[...] = gv
        d_ref[...] = -ADAM_LR * (m_hat / (jnp.sqrt(v_hat) + ADAM_EPS) + ADAM_WD * w_ref[...])
        nm_ref[...] = nm
        nv_ref[...] = nv

    spec = _rows(tr, cols)
    if parts is None:
        g_specs, g_args = [spec], [grad]
    elif first_parts is None:
        g_specs, g_args = [pl.BlockSpec((4, tr, cols), lambda i: (0, i, 0))], [parts]
    else:
        g_specs = [pl.BlockSpec((4, tr, cols), lambda i: (0, jnp.minimum(i, n0 - 1), 0)),
                   pl.BlockSpec((4, tr, cols), lambda i: (0, jnp.maximum(i - n0, 0), 0))]
        g_args = [first_parts, parts]
    return pl.pallas_call(
        body, name="adamw" if parts is None else "adamw_sum_chips", grid=(rows // tr,), in_specs=[spec] * 3 + g_specs,
        out_specs=[spec] * 4, out_shape=[_sds(w.shape, F32)] * 4,
        compiler_params=_params(("parallel",)))(w, m, v, *g_args)


_ANY = pl.BlockSpec(memory_space=pl.ANY)
_MESH = pl.DeviceIdType.MESH


def _all_gather(name, blocks):
    nt = len(blocks)

    def body(*refs):
        start, finish = _gather_protocol(refs[:nt], refs[nt:2 * nt], *refs[2 * nt:])
        start()
        finish()

    return pl.pallas_call(
        body, name=name, out_shape=_gather_shapes(blocks), in_specs=[_ANY] * nt, out_specs=[_ANY] * nt,
        scratch_shapes=_gather_sems(nt))(*blocks)


def _gather_shapes(blocks):
    return [_sds((N_DEV,) + b.shape, b.dtype) for b in blocks]


def _gather_sems(nt):
    return [pltpu.SemaphoreType.DMA((nt, 7)), pltpu.SemaphoreType.DMA((nt, 7)), pltpu.SemaphoreType.DMA((nt,))]


def _gather_protocol(x_refs, out_refs, send_sems, recv_sems, local_sems):
    nt = len(x_refs)
    x, y, c = lax.axis_index("x"), lax.axis_index("y"), lax.axis_index("c")
    me, sibling = (x, y, c), (x, y, 1 - c)
    chips = [(1 - x, y), (x, 1 - y), (1 - x, 1 - y)]

    def slot(t, px, py, pc):
        return out_refs[t].at[4 * px + 2 * py + pc]

    def copy(t, k, blk, to, src=None):
        return pltpu.make_async_remote_copy(
            src_ref=slot(t, *blk) if src is None else src, dst_ref=slot(t, *blk),
            send_sem=send_sems.at[t, k], recv_sem=recv_sems.at[t, k], device_id=to, device_id_type=_MESH)

    def own_copies():
        mine = [pltpu.make_async_copy(x_refs[t], slot(t, *me), local_sems.at[t]) for t in range(nt)]
        first = []
        for t in range(nt):
            first.append(copy(t, 0, me, sibling, src=x_refs[t]))
            first += [copy(t, 1 + j, me, (*chip, c), src=x_refs[t]) for j, chip in enumerate(chips)]
        return mine, first

    def start():
        mine, first = own_copies()
        for cp in mine + first:
            cp.start()

    def finish():
        mine, first = own_copies()
        passed = []
        for j, chip in enumerate(chips):
            for t in range(nt):
                copy(t, 1 + j, (*chip, c), me).wait_recv()
                fwd = copy(t, 4 + j, (*chip, c), sibling)
                fwd.start()
                passed.append(fwd)
        for t in range(nt):
            copy(t, 0, sibling, me).wait_recv()
        for j, chip in enumerate(chips):
            for t in range(nt):
                copy(t, 4 + j, (*chip, 1 - c), me).wait_recv()
        for cp in first + passed:
            cp.wait_send()
        for cp in mine:
            cp.wait()

    return start, finish


def _swap_with_sibling(packed):
    nt = len(packed)

    def body(*refs):
        p_refs, got_refs = refs[:nt], refs[nt:2 * nt]
        send_sems, recv_sems = refs[2 * nt:]
        x, y, c = lax.axis_index("x"), lax.axis_index("y"), lax.axis_index("c")
        copies = [
            pltpu.make_async_remote_copy(
                src_ref=p_refs[t].at[1 - c], dst_ref=got_refs[t], send_sem=send_sems.at[t], recv_sem=recv_sems.at[t],
                device_id=(x, y, 1 - c), device_id_type=_MESH)
            for t in range(nt)]
        for cp in copies:
            cp.start()
        for cp in copies:
            cp.wait_recv()
        for cp in copies:
            cp.wait_send()

    return pl.pallas_call(
        body, name="grad_swap_sibling", out_shape=[_sds(p.shape[1:], p.dtype) for p in packed], in_specs=[_ANY] * nt,
        out_specs=[_ANY] * nt,
        scratch_shapes=[pltpu.SemaphoreType.DMA((nt,)), pltpu.SemaphoreType.DMA((nt,))])(*packed)


def _chip_sum(packed, got):
    _, nchip, rows, cols = packed.shape
    tr = _row_tile(rows, cols, 2)
    core = lax.axis_index("c").astype(jnp.int32).reshape(1)

    def body(c_ref, p_ref, g_ref, o_ref):
        o_ref[...] = (p_ref[...].astype(F32) + g_ref[...].astype(F32)).astype(o_ref.dtype)

    grid_spec = pltpu.PrefetchScalarGridSpec(
        num_scalar_prefetch=1, grid=(nchip, rows // tr),
        in_specs=[pl.BlockSpec((None, None, tr, cols), lambda k, i, c: (c[0], k, i, 0)),
                  pl.BlockSpec((None, tr, cols), lambda k, i, c: (k, i, 0))],
        out_specs=pl.BlockSpec((None, tr, cols), lambda k, i, c: (k, i, 0)))
    return pl.pallas_call(
        body, name="grad_chip_sum", grid_spec=grid_spec, out_shape=_sds(got.shape, got.dtype),
        compiler_params=_params(("parallel", "parallel")))(core, packed, got)


def _exchange_chips(chip_sums):
    nt = len(chip_sums)

    def body(*refs):
        start, finish = _exchange_protocol(refs[:nt], refs[nt:2 * nt], *refs[2 * nt:])
        start()
        finish()

    return pl.pallas_call(
        body, name="grad_exchange_chips", out_shape=[_sds(s.shape, s.dtype) for s in chip_sums], in_specs=[_ANY] * nt,
        out_specs=[_ANY] * nt, scratch_shapes=_exchange_sems(nt))(*chip_sums)


def _exchange_sems(nt):
    return [pltpu.SemaphoreType.DMA((nt, 3)), pltpu.SemaphoreType.DMA((nt, 3)), pltpu.SemaphoreType.DMA((nt,))]


def _exchange_protocol(s_refs, got_refs, send_sems, recv_sems, local_sems):
    nt = len(s_refs)
    x, y, c = lax.axis_index("x"), lax.axis_index("y"), lax.axis_index("c")
    my_chip = 2 * x + y
    chips = [(1 - x, y), (x, 1 - y), (1 - x, 1 - y)]

    def copy(t, j, src_chip, dst_chip):
        px, py = chips[j]
        return pltpu.make_async_remote_copy(
            src_ref=s_refs[t].at[src_chip], dst_ref=got_refs[t].at[dst_chip], send_sem=send_sems.at[t, j],
            recv_sem=recv_sems.at[t, j], device_id=(px, py, c), device_id_type=_MESH)

    def own_copies():
        mine = [pltpu.make_async_copy(s_refs[t].at[my_chip], got_refs[t].at[my_chip], local_sems.at[t]) for t in range(nt)]
        sends = [copy(t, j, 2 * px + py, my_chip) for t in range(nt) for j, (px, py) in enumerate(chips)]
        return mine, sends

    def start():
        mine, sends = own_copies()
        for cp in mine + sends:
            cp.start()

    def finish():
        mine, sends = own_copies()
        for j, (px, py) in enumerate(chips):
            for t in range(nt):
                copy(t, j, my_chip, 2 * px + py).wait_recv()
        for cp in sends:
            cp.wait_send()
        for cp in mine:
            cp.wait()

    return start, finish


def _sum_chips(got):
    _, rows, cols = got.shape
    tr = _row_tile(rows, cols)

    def body(g_ref, o_ref):
        acc = g_ref[0].astype(F32)
        for k in range(1, 4):
            acc = acc + g_ref[k].astype(F32)
        o_ref[...] = acc

    return pl.pallas_call(
        body, name="grad_sum_chips", grid=(rows // tr,), in_specs=[pl.BlockSpec((4, tr, cols), lambda i: (0, i, 0))],
        out_specs=pl.BlockSpec((tr, cols), lambda i: (i, 0)), out_shape=_sds((rows, cols), F32),
        compiler_params=_params(("parallel",)))(got)


_COL_SHARDED = ("a_w_in", "b_w_in", "w_gate", "w_up")
_ROW_SHARDED = ("w_mem_kv", "w_out", "w_down")
_BIG = ("a_w_in", "b_w_in", "w_mem_kv", "w_out", "w_gate", "w_up", "w_down")
_SGU_LN = ("sgu_ln_g", "sgu_ln_b")
_LN4 = ("ln_mix_g", "ln_mix_b", "ln_ffn_g", "ln_ffn_b")
_REPLICATED = ("sgu_w_s", "sgu_b_s") + _LN4


def _unshard(name, gathered):
    if name in _COL_SHARDED or name in _SGU_LN:
        moved = jnp.moveaxis(gathered, 0, -2)
        return moved.reshape(moved.shape[:-2] + (moved.shape[-2] * moved.shape[-1],))
    moved = jnp.moveaxis(gathered, 0, 1)
    return moved.reshape((moved.shape[0], moved.shape[1] * moved.shape[2]) + moved.shape[3:])


_LAID_OUT_IN_KERNEL = _COL_SHARDED


def _after_gather(name, gathered):
    return gathered if name in _LAID_OUT_IN_KERNEL else _unshard(name, gathered)


def _by_shard(name, full):
    if name in _COL_SHARDED or name in _SGU_LN:
        split = full.reshape(full.shape[:-1] + (N_DEV, full.shape[-1] // N_DEV))
        return jnp.moveaxis(split, -2, 0)
    split = full.reshape((full.shape[0], N_DEV, full.shape[1] // N_DEV) + full.shape[2:])
    return jnp.moveaxis(split, 1, 0)


def _layer_keys(i):
    return [("a_w_in" if i % 2 == 0 else "b_w_in", i // 2)] + [(n, i) for n in ("w_mem_kv", "w_out", "w_gate", "w_up", "w_down")]


_GATHER_FIRST = _layer_keys(0)[:2]
_GATHER_LATER = (_layer_keys(0)[2:] + _layer_keys(1), _layer_keys(2)[:4], _layer_keys(2)[4:],
                 _layer_keys(3)[:4], _layer_keys(3)[4:])


def _shard_block(shards, key):
    name, idx = key
    return shards[name][idx:idx + 1].astype(BF16)


def _gather_first(shards):
    blocks = [_shard_block(shards, k) for k in _GATHER_FIRST] + [shards[n] for n in _SGU_LN]
    gathered = _all_gather("first_all_gather", blocks)
    full = {k: _after_gather(k[0], g) for k, g in zip(_GATHER_FIRST, gathered)}
    sgu_ln = {n: _unshard(n, g) for n, g in zip(_SGU_LN, gathered[len(_GATHER_FIRST):])}
    return full, sgu_ln


def _two_level(by_dest):
    shp = by_dest.shape[1:]
    split = by_dest.astype(BF16).reshape((4, 2) + shp).swapaxes(0, 1)
    return split.reshape(2, 4, int(np.prod(shp[:-1])), shp[-1])


_EARLY = _BIG + _SGU_LN


def _chip_sums_of_early(grads):
    packed = [_two_level(_by_shard(n, jnp.stack(grads[n][1:] if n == "a_w_in" else grads[n]))) for n in _EARLY]
    ln4 = jnp.stack([jnp.stack(grads[n]) for n in _LN4])
    rep = [jnp.stack(grads["sgu_w_s"]).reshape(N_DEV, -1, BLK), jnp.stack(grads["sgu_b_s"]).reshape(N_DEV, -1, BLK),
           ln4.reshape(N_DEV, -1, D_MODEL)]
    packed += [_two_level(r) for r in rep]
    got = _swap_with_sibling(packed)
    return [_chip_sum(p, g) for p, g in zip(packed, got)]


def _finish_replicated(parts, shapes):
    w_s, b_s, ln_all = _all_gather("replicated_grads_all_gather", [_sum_chips(p) for p in parts])
    ln_all = ln_all.reshape((len(_LN4),) + tuple(shapes[_LN4[0]]))
    rep_grads = {"sgu_w_s": w_s.reshape(shapes["sgu_w_s"]), "sgu_b_s": b_s.reshape(shapes["sgu_b_s"])}
    rep_grads.update({n: ln_all[i] for i, n in enumerate(_LN4)})
    return rep_grads


def _reduce_last(grad_a_first):
    packed = [_two_level(_by_shard("a_w_in", grad_a_first))]
    got = _swap_with_sibling(packed)
    return _exchange_chips([_chip_sum(packed[0], got[0])])[0]


def _as_2d(a):
    if a.ndim == 1:
        return a.reshape(1, -1)
    return a.reshape(-1, a.shape[-1])


def kernel(x, mem, a_w_in, b_w_in, sgu_ln_g, sgu_ln_b, sgu_w_s, sgu_b_s, w_mem_kv, w_out, ln_mix_g, ln_mix_b, w_gate, w_up, w_down, ln_ffn_g, ln_ffn_b, loss_target, m_a_w_in, m_b_w_in, m_sgu_ln_g, m_sgu_ln_b, m_sgu_w_s, m_sgu_b_s, m_w_mem_kv, m_w_out, m_ln_mix_g, m_ln_mix_b, m_w_gate, m_w_up, m_w_down, m_ln_ffn_g, m_ln_ffn_b, v_a_w_in, v_b_w_in, v_sgu_ln_g, v_sgu_ln_b, v_sgu_w_s, v_sgu_b_s, v_w_mem_kv, v_w_out, v_ln_mix_g, v_ln_mix_b, v_w_gate, v_w_up, v_w_down, v_ln_ffn_g, v_ln_ffn_b):
    names = ("a_w_in", "b_w_in", "sgu_ln_g", "sgu_ln_b", "sgu_w_s", "sgu_b_s", "w_mem_kv", "w_out", "ln_mix_g", "ln_mix_b",
             "w_gate", "w_up", "w_down", "ln_ffn_g", "ln_ffn_b")
    weights = dict(zip(names, (a_w_in, b_w_in, sgu_ln_g, sgu_ln_b, sgu_w_s, sgu_b_s, w_mem_kv, w_out, ln_mix_g, ln_mix_b,
                               w_gate, w_up, w_down, ln_ffn_g, ln_ffn_b)))
    mom_m = dict(zip(names, (m_a_w_in, m_b_w_in, m_sgu_ln_g, m_sgu_ln_b, m_sgu_w_s, m_sgu_b_s, m_w_mem_kv, m_w_out, m_ln_mix_g,
                             m_ln_mix_b, m_w_gate, m_w_up, m_w_down, m_ln_ffn_g, m_ln_ffn_b)))
    mom_v = dict(zip(names, (v_a_w_in, v_b_w_in, v_sgu_ln_g, v_sgu_ln_b, v_sgu_w_s, v_sgu_b_s, v_w_mem_kv, v_w_out, v_ln_mix_g,
                             v_ln_mix_b, v_w_gate, v_w_up, v_w_down, v_ln_ffn_g, v_ln_ffn_b)))
    full, sgu_ln = _gather_first(weights)
    pending = [(keys, [_shard_block(weights, k) for k in keys]) for keys in _GATHER_LATER]
    loss_part, grad_x, local, early = _local_step(
        x, mem, loss_target, full, sgu_ln, {n: weights[n] for n in _REPLICATED}, pending)
    loss = lax.psum(loss_part[0, 0], ("x", "y", "c"))
    early_parts = dict(zip(_EARLY, early))
    rep_grads = _finish_replicated(early[len(_EARLY):], {n: weights[n].shape for n in _REPLICATED})
    a_first_parts = _reduce_last(local["a_w_in"][:1])

    reduced, deltas, new_m, new_v = {}, {}, {}, {}
    for n in names:
        w2, m2, v2 = _as_2d(weights[n]), _as_2d(mom_m[n]), _as_2d(mom_v[n])
        if n in early_parts:
            outs = _adamw(w2, m2, v2, parts=early_parts[n], first_parts=a_first_parts if n == "a_w_in" else None)
        else:
            outs = _adamw(w2, m2, v2, grad=_as_2d(rep_grads[n]))
        reduced[n], deltas[n], new_m[n], new_v[n] = (a.reshape(weights[n].shape) for a in outs)

    return (loss, grad_x, *[reduced[n] for n in names], *[deltas[n] for n in names],
            *[new_m[n] for n in names], *[new_v[n] for n in names])


def _local_step(x, mem, loss_target, full, sgu_ln, small, pending=None):
    sgu_w_s, sgu_b_s = small["sgu_w_s"], small["sgu_b_s"]
    ln_mix_g, ln_mix_b, ln_ffn_g, ln_ffn_b = (small[n] for n in ("ln_mix_g", "ln_mix_b", "ln_ffn_g", "ln_ffn_b"))
    bsz, seq, _ = x.shape
    tokens = bsz * seq
    slopes = _alibi_table()
    full = dict(full)
    exchanging = pending is not None
    pending = list(pending or [])

    def weight(name, idx):
        return full[(name, idx)][0]

    def next_group():
        return pending[0][1] if pending else None

    def landed(gathered):
        if gathered:
            keys, _ = pending.pop(0)
            full.update({k: _after_gather(k[0], g) for k, g in zip(keys, gathered)})

    res = (x.reshape(tokens, D_MODEL),)
    xb = res[0].astype(BF16)
    memb = mem.reshape(bsz * N_MEM, D_MODEL).astype(BF16)
    tgt = loss_target.reshape(tokens, D_MODEL)

    saved = []
    for i in range(DEPTH):
        j = i // 2
        dil_layer = i % 2 == 0
        mkv = _linear_nn("mem_kv", memb, weight("w_mem_kv", i))
        h, w_in = _linear_nn_gathered("in_proj_a" if dil_layer else "in_proj_b", xb,
                                      full[("a_w_in" if dil_layer else "b_w_in", j)][:, 0])
        st = dict(xb=xb, h=h, mkv=mkv, w_in=w_in)
        if dil_layer:
            mix, st["lse"], gathered = _band_attn_fwd_fused(h, slopes, bsz, seq, gather=next_group() if i == 0 else None)
            landed(gathered)
            q_col = 3 * MIX_W // MEM_W
        else:
            st["ws"] = sgu_w_s[j]
            st["bs_t"] = sgu_b_s[j].T
            st["ln_g"] = sgu_ln["sgu_ln_g"][j].reshape(1, MIX_W)
            st["ln_b"] = sgu_ln["sgu_ln_b"][j].reshape(1, MIX_W)
            mix = _sgu_fwd(h, st["ws"], st["bs_t"], st["ln_g"], st["ln_b"])
            q_col = 2 * MIX_W // MEM_W
        mo = _mem_attn_fwd(h, mkv, bsz, seq, q_col)
        w_out, w_down = weight("w_out", i), weight("w_down", i)
        mix_ln = (ln_mix_g[i].reshape(1, D_MODEL), ln_mix_b[i].reshape(1, D_MODEL))
        ffn_ln = (ln_ffn_g[i].reshape(1, D_MODEL), ln_ffn_b[i].reshape(1, D_MODEL))
        r1, x1b, _ = _proj_ln_fwd("out_proj_ln", [mix, mo], w_out, res, *mix_ln)
        gt, up, act, w_gu, gathered = _ffn_up_fwd(x1b, full[("w_gate", i)][:, 0], full[("w_up", i)][:, 0],
                                                  gather=next_group() if i < 2 else None)
        landed(gathered)
        r2, xb, gathered = _proj_ln_fwd("ffn_down_ln", [act], w_down, (r1, *mix_ln), *ffn_ln,
                                        gather=next_group() if i < 2 else None)
        landed(gathered)
        res = (r2, *ffn_ln)
        st.update(mix=mix, mo=mo, q_col=q_col, r1=r1, x1b=x1b, gt=gt, up=up, act=act, r2=r2,
                  w_out=w_out, w_down=w_down, w_gu=w_gu)
        saved.append(st)

    dr2, dr2b, dg, db, loss_part = _loss_ln_bwd(*res, tgt)

    early_parts = None
    per_pair = ("a_w_in", "b_w_in", "sgu_ln_g", "sgu_ln_b", "sgu_w_s", "sgu_b_s")
    grads = {n: [None] * (DEPTH // 2 if n in per_pair else DEPTH) for n in _BIG + _SGU_LN + _REPLICATED}
    for i in reversed(range(DEPTH)):
        j = i // 2
        st = saved[i]
        dil_layer = i % 2 == 0
        w_in = st["w_in"]
        grads["ln_ffn_g"][i], grads["ln_ffn_b"][i] = dg[0], db[0]
        dgu = _ffn_down_bwd(dr2b, st["w_down"], st["gt"], st["up"])
        grads["w_down"][i] = _mm_tn("grad_w_down", st["act"], dr2b)
        dr1, dr1b, dg, db = _linear_nt("ffn_up_bwd", [dgu], st["w_gu"], dr2, F32,
                                       ln=(st["r1"], ln_mix_g[i].reshape(1, D_MODEL)))
        grads["ln_mix_g"][i], grads["ln_mix_b"][i] = dg[0], db[0]
        grads["w_gate"][i] = _mm_tn("grad_w_gate_up", st["x1b"], dgu, cols=(0, D_FF))
        grads["w_up"][i] = _mm_tn("grad_w_gate_up", st["x1b"], dgu, cols=(D_FF, 2 * D_FF))
        dcat = _linear_nt("out_proj_bwd", [dr1b], st["w_out"], None, BF16)
        grads["w_out"][i] = _mm_tn_parts("grad_w_out", dr1b, [st["mix"], st["mo"]], split_rows=True)
        dqm, dmkv = _mem_attn_bwd(st["h"], st["mkv"], dcat, bsz, seq, st["q_col"])
        grads["w_mem_kv"][i] = _mm_tn("grad_w_mem_kv", memb, dmkv.astype(BF16))
        if dil_layer:
            early_sums = _chip_sums_of_early(grads) if (i == 0 and exchanging) else None
            dh_parts, exchanged = _band_attn_bwd_fused(st["h"], dcat, st["mix"], st["lse"], slopes, bsz, seq,
                                                       exchange=early_sums)
            if early_sums is not None:
                early_parts = exchanged
        else:
            ws_t = jnp.swapaxes(st["ws"], -1, -2)
            dh_main, dws, dbs_t, dlg, dlb = _sgu_bwd(st["h"], dcat, st["ws"], ws_t, st["bs_t"], st["ln_g"], st["ln_b"])
            grads["sgu_w_s"][j], grads["sgu_b_s"][j] = dws, dbs_t.T
            grads["sgu_ln_g"][j], grads["sgu_ln_b"][j] = dlg[0], dlb[0]
            dh_parts = [dh_main]
        name = "in_proj_bwd_a" if dil_layer else "in_proj_bwd_b"
        if i > 0:
            dr2, dr2b, dg, db = _linear_nt(name, [*dh_parts, dqm], w_in, dr1, F32,
                                           ln=(saved[i - 1]["r2"], ln_ffn_g[i - 1].reshape(1, D_MODEL)))
        else:
            grad_x = _linear_nt(name + "_x", [*dh_parts, dqm], w_in, dr1, F32).reshape(x.shape)
        grads["a_w_in" if dil_layer else "b_w_in"][j] = _mm_tn_parts(
            "grad_w_in_a" if dil_layer else "grad_w_in_b", st["xb"], [*dh_parts, dqm])
    return loss_part, grad_x, {n: jnp.stack(g) for n, g in grads.items()}, early_parts
```

```python
import math

import numpy as np
import jax
import jax.numpy as jnp
from jax import lax
from jax.experimental import pallas as pl
from jax.experimental.pallas import tpu as pltpu

F32 = jnp.float32
BF16 = jnp.bfloat16

D_MODEL = 1024
DEPTH = 4
N_MEM = 256
HEAD_DIM = 64
N_HEADS = 12
MIX_W = N_HEADS * HEAD_DIM
MEM_W = 4 * HEAD_DIM
BLK = 128
HEAD_GROUP = 4
N_GROUPS = N_HEADS // HEAD_GROUP
D_FF = 2816
FF_CHUNKS = 2
ALPHA = (2 * DEPTH) ** 0.25
LN_EPS = 1e-5
SCALE = HEAD_DIM ** -0.5
NEG = -1e30
N_DEV = 8

ADAM_LR, ADAM_B1, ADAM_B2, ADAM_EPS, ADAM_WD, ADAM_STEP = 0.001, 0.9, 0.999, 1e-08, 0.01, 10

VMEM_LIMIT = 56 * 2 ** 20
STAT_LANES = 32
STAT_W = N_HEADS * STAT_LANES


def _dot_nn(a, b):
    return lax.dot_general(a, b, (((1,), (0,)), ((), ())), preferred_element_type=F32)


def _dot_nt(a, b):
    return lax.dot_general(a, b, (((1,), (1,)), ((), ())), preferred_element_type=F32)


def _dot_tn(a, b):
    return lax.dot_general(a, b, (((0,), (0,)), ((), ())), preferred_element_type=F32)


def _ln_hat(r):
    mu = jnp.mean(r, axis=-1, keepdims=True)
    xc = r - mu
    var = jnp.mean(xc * xc, axis=-1, keepdims=True)
    rstd = lax.rsqrt(var + LN_EPS)
    return xc * rstd, rstd


def _params(sem):
    return pltpu.CompilerParams(dimension_semantics=sem, vmem_limit_bytes=VMEM_LIMIT)


def _rows(tm, c, col=0):
    return pl.BlockSpec((tm, c), lambda i: (i, col))


def _whole(shape):
    nd = len(shape)
    return pl.BlockSpec(tuple(shape), lambda *_: (0,) * nd)


def _resident(shape):
    nd = len(shape)
    return pl.BlockSpec(tuple(shape), lambda *_: (0,) * nd, pipeline_mode=pl.Buffered(1))


def _sds(shape, dtype):
    return jax.ShapeDtypeStruct(tuple(shape), dtype)


def _linear_nn(name, a, w, tm=512):
    t, k = a.shape
    n = w.shape[1]
    tm = min(tm, t)

    def body(a_ref, w_ref, o_ref):
        o_ref[...] = _dot_nn(a_ref[...], w_ref[...]).astype(BF16)

    return pl.pallas_call(
        body, name=name, grid=(t // tm,), in_specs=[_rows(tm, k), _resident(w.shape)], out_specs=_rows(tm, n),
        out_shape=_sds((t, n), BF16), compiler_params=_params(("parallel",)))(a, w)


def _linear_nn_gathered(name, a, shards, tm=512):
    t, k = a.shape
    n8 = shards.shape[2]
    n = N_DEV * n8

    def body(a_ref, s_ref, o_ref, w_ref):
        @pl.when(pl.program_id(0) == 0)
        def _():
            for s in range(N_DEV):
                w_ref[:, s * n8:(s + 1) * n8] = s_ref[s]

        o_ref[...] = _dot_nn(a_ref[...], w_ref[...]).astype(BF16)

    return pl.pallas_call(
        body, name=name, grid=(t // tm,), in_specs=[_rows(tm, k), _resident(shards.shape)],
        out_specs=[_rows(tm, n), _whole((k, n))], out_shape=[_sds((t, n), BF16), _sds((k, n), BF16)],
        compiler_params=_params(("arbitrary",)))(a, shards)


def _proj_ln_fwd(name, lhs, w, res, g, b, gather=None, tm=512):
    t = res[0].shape[0]
    n_lhs = len(lhs)
    n_res = len(res)
    ng = 0 if gather is None else len(gather)
    steps = t // tm
    n_in = n_lhs + 3 + n_res

    def body(*refs):
        lhs_refs = refs[:n_lhs]
        w_ref = refs[n_lhs]
        res_refs = refs[n_lhs + 1:n_lhs + 1 + n_res]
        g_ref, b_ref = refs[n_in - 2:n_in]
        r_ref, xnb_ref = refs[n_in + ng:n_in + ng + 2]
        if ng:
            start, finish = _gather_protocol(refs[n_in:n_in + ng], refs[n_in + ng + 2:n_in + 2 * ng + 2], *refs[n_in + 2 * ng + 2:])
            pl.when(pl.program_id(0) == 0)(start)
        y, off = None, 0
        for lr in lhs_refs:
            k = lr.shape[1]
            term = _dot_nn(lr[...], w_ref[off:off + k, :])
            y = term if y is None else y + term
            off += k
        x_res = res_refs[0][...]
        if n_res == 3:
            x_res = _ln_hat(x_res)[0] * res_refs[1][...] + res_refs[2][...]
        r = ALPHA * x_res + y
        r_ref[...] = r
        xnb_ref[...] = (_ln_hat(r)[0] * g_ref[...] + b_ref[...]).astype(BF16)
        if ng:
            pl.when(pl.program_id(0) == steps - 1)(finish)

    vec = _whole((1, D_MODEL))
    in_specs = ([_rows(tm, a.shape[1]) for a in lhs] + [_resident(w.shape), _rows(tm, D_MODEL)] + [vec] * (n_res - 1) + [vec, vec])
    outs = pl.pallas_call(
        body, name=name + "_gather" if ng else name, grid=(steps,), in_specs=in_specs + [_ANY] * ng,
        out_specs=[_rows(tm, D_MODEL)] * 2 + [_ANY] * ng,
        out_shape=[_sds((t, D_MODEL), F32), _sds((t, D_MODEL), BF16)] + (_gather_shapes(gather) if ng else []),
        scratch_shapes=_gather_sems(ng) if ng else [],
        compiler_params=_params(("arbitrary" if ng else "parallel",)))(*lhs, w, *res, g, b, *(gather or []))
    return outs[0], outs[1], list(outs[2:])


def _ffn_up_fwd(xb, gate_shards, up_shards, gather=None, tm=256):
    t = xb.shape[0]
    ng = 0 if gather is None else len(gather)
    steps = t // tm
    n8 = gate_shards.shape[2]

    def body(*refs):
        x_ref, gs_ref, us_ref = refs[:3]
        g_ref, u_ref, a_ref, wgu_ref = refs[3 + ng:7 + ng]
        w_ref, w_sem = refs[7 + 2 * ng:9 + 2 * ng]
        keep = pltpu.make_async_copy(w_ref, wgu_ref, w_sem)

        @pl.when(pl.program_id(0) == 0)
        def _():
            for s in range(N_DEV):
                w_ref[:, s * n8:(s + 1) * n8] = gs_ref[s]
                w_ref[:, D_FF + s * n8:D_FF + (s + 1) * n8] = us_ref[s]
            keep.start()

        if ng:
            start, finish = _gather_protocol(refs[3:3 + ng], refs[7 + ng:7 + 2 * ng], *refs[9 + 2 * ng:])
            pl.when(pl.program_id(0) == 0)(start)
        xv = x_ref[...]
        for c in range(FF_CHUNKS):
            cols = slice(c * D_FF // FF_CHUNKS, (c + 1) * D_FF // FF_CHUNKS)
            gt = _dot_nn(xv, w_ref[:, cols])
            up = _dot_nn(xv, w_ref[:, D_FF + cols.start:D_FF + cols.stop])
            sg = jax.nn.sigmoid(gt)
            silu = gt * sg
            g_ref[:, cols] = (up * (sg * (1.0 + gt * (1.0 - sg)))).astype(BF16)
            u_ref[:, cols] = silu.astype(BF16)
            a_ref[:, cols] = (silu * up).astype(BF16)
        if ng:
            pl.when(pl.program_id(0) == steps - 1)(finish)
        pl.when(pl.program_id(0) == steps - 1)(keep.wait)

    k = gate_shards.shape[1]
    outs = pl.pallas_call(
        body, name="ffn_up_fwd_gather" if ng else "ffn_up_fwd", grid=(steps,),
        in_specs=[_rows(tm, D_MODEL), _resident(gate_shards.shape), _resident(up_shards.shape)] + [_ANY] * ng,
        out_specs=[_rows(tm, D_FF)] * 3 + [_ANY] + [_ANY] * ng,
        out_shape=[_sds((t, D_FF), BF16)] * 3 + [_sds((k, 2 * D_FF), BF16)] + (_gather_shapes(gather) if ng else []),
        scratch_shapes=[pltpu.VMEM((k, 2 * D_FF), BF16), pltpu.SemaphoreType.DMA(())] + (_gather_sems(ng) if ng else []),
        compiler_params=_params(("arbitrary",)))(xb, gate_shards, up_shards, *(gather or []))
    return outs[0], outs[1], outs[2], outs[3], list(outs[4:])


def _ln_bwd_rows(dxn, xhat, rstd, g_ref, dr_ref, drb_ref, dg_ref, db_ref):
    @pl.when(pl.program_id(0) == 0)
    def _():
        dg_ref[...] = jnp.zeros_like(dg_ref)
        db_ref[...] = jnp.zeros_like(db_ref)

    dxh = dxn * g_ref[...]
    m1 = jnp.mean(dxh, axis=-1, keepdims=True)
    m2 = jnp.mean(dxh * xhat, axis=-1, keepdims=True)
    dr = rstd * (dxh - m1 - xhat * m2)
    dr_ref[...] = dr
    drb_ref[...] = dr.astype(BF16)
    dg_ref[...] += jnp.sum(dxn * xhat, axis=0, keepdims=True)
    db_ref[...] += jnp.sum(dxn, axis=0, keepdims=True)


def _ln_bwd_outs(t, tm):
    vec = _whole((1, D_MODEL))
    specs = [_rows(tm, D_MODEL), _rows(tm, D_MODEL), vec, vec]
    shapes = [_sds((t, D_MODEL), F32), _sds((t, D_MODEL), BF16), _sds((1, D_MODEL), F32), _sds((1, D_MODEL), F32)]
    return specs, shapes


def _loss_ln_bwd(r, g, b, tgt, tm=512):
    t = r.shape[0]

    def body(r_ref, g_ref, b_ref, t_ref, dr_ref, drb_ref, dg_ref, db_ref, l_ref):
        @pl.when(pl.program_id(0) == 0)
        def _():
            l_ref[...] = jnp.zeros_like(l_ref)

        xhat, rstd = _ln_hat(r_ref[...])
        e = xhat * g_ref[...] + b_ref[...] - t_ref[...]
        l_ref[...] += jnp.sum(e * e) * (0.5 / D_MODEL)
        _ln_bwd_rows(e * (1.0 / D_MODEL), xhat, rstd, g_ref, dr_ref, drb_ref, dg_ref, db_ref)

    vec = _whole((1, D_MODEL))
    specs, shapes = _ln_bwd_outs(t, tm)
    return pl.pallas_call(
        body, name="loss_ln_bwd", grid=(t // tm,), in_specs=[_rows(tm, D_MODEL), vec, vec, _rows(tm, D_MODEL)],
        out_specs=specs + [_whole((1, 128))], out_shape=shapes + [_sds((1, 128), F32)],
        compiler_params=_params(("arbitrary",)))(r, g, b, tgt)


def _ffn_down_bwd(drb, wd, gt, up, tm=512):
    t = drb.shape[0]

    def body(d_ref, w_ref, g_ref, u_ref, o_ref):
        dv = d_ref[...]
        for c in range(FF_CHUNKS):
            cols = slice(c * D_FF // FF_CHUNKS, (c + 1) * D_FF // FF_CHUNKS)
            da = _dot_nt(dv, w_ref[cols, :])
            o_ref[:, cols] = (da * g_ref[:, cols].astype(F32)).astype(BF16)
            o_ref[:, D_FF + cols.start:D_FF + cols.stop] = (da * u_ref[:, cols].astype(F32)).astype(BF16)

    return pl.pallas_call(
        body, name="ffn_down_bwd", grid=(t // tm,),
        in_specs=[_rows(tm, D_MODEL), _resident(wd.shape), _rows(tm, D_FF), _rows(tm, D_FF)],
        out_specs=_rows(tm, 2 * D_FF), out_shape=_sds((t, 2 * D_FF), BF16),
        compiler_params=_params(("parallel",)))(drb, wd, gt, up)


def _linear_nt(name, lhs, w, res, out_dtype, ln=None, tm=512):
    t = lhs[0].shape[0]
    n_lhs = len(lhs)
    n_out = w.shape[0]
    n_in = n_lhs + 1 + (res is not None) + (2 if ln else 0)

    def body(*refs):
        lhs_refs = refs[:n_lhs]
        w_ref = refs[n_lhs]
        y, off = None, 0
        for lr in lhs_refs:
            k = lr.shape[1]
            term = _dot_nt(lr[...], w_ref[:, off:off + k])
            y = term if y is None else y + term
            off += k
        if res is not None:
            y = ALPHA * refs[n_lhs + 1][...] + y
        if ln is None:
            refs[-1][...] = y.astype(out_dtype)
        else:
            r_ref, g_ref = refs[n_in - 2:n_in]
            xhat, rstd = _ln_hat(r_ref[...])
            _ln_bwd_rows(y, xhat, rstd, g_ref, *refs[n_in:])

    in_specs = [_rows(tm, a.shape[1]) for a in lhs] + [_resident(w.shape)]
    args = list(lhs) + [w]
    if res is not None:
        in_specs.append(_rows(tm, n_out))
        args.append(res)
    if ln is None:
        out_specs, out_shape, sem = _rows(tm, n_out), _sds((t, n_out), out_dtype), "parallel"
    else:
        in_specs += [_rows(tm, D_MODEL), _whole((1, D_MODEL))]
        args += list(ln)
        (out_specs, out_shape), sem = _ln_bwd_outs(t, tm), "arbitrary"
    return pl.pallas_call(
        body, name=name, grid=(t // tm,), in_specs=in_specs, out_specs=out_specs, out_shape=out_shape,
        compiler_params=_params((sem,)))(*args)


def _pick_tile(n, limit):
    if n <= limit:
        return n
    best = 128
    for cand in range(128, limit + 1, 128):
        if n % cand == 0:
            best = cand
    return best


def _mm_tn(name, a, b, cols=None, tt=1024):
    t, k = a.shape
    first_col, n = (0, b.shape[1]) if cols is None else (cols[0], cols[1] - cols[0])
    tt = min(tt, t)
    tk = _pick_tile(k, 1408)
    tn = _pick_tile(n, (6 * 2 ** 20) // (4 * tk) // 128 * 128)
    steps = t // tt
    first_block = first_col // tn
    assert first_block * tn == first_col, (first_col, tn)

    def body(a_ref, b_ref, o_ref, acc_ref):
        @pl.when(pl.program_id(2) == 0)
        def _():
            acc_ref[...] = jnp.zeros_like(acc_ref)

        acc_ref[...] += _dot_tn(a_ref[...], b_ref[...])

        @pl.when(pl.program_id(2) == steps - 1)
        def _():
            o_ref[...] = acc_ref[...].astype(BF16)

    return pl.pallas_call(
        body, name=name, grid=(k // tk, n // tn, steps),
        in_specs=[pl.BlockSpec((tt, tk), lambda i, j, s: (s, i)), pl.BlockSpec((tt, tn), lambda i, j, s: (s, first_block + j))],
        out_specs=pl.BlockSpec((tk, tn), lambda i, j, s: (i, j)), out_shape=_sds((k, n), BF16),
        scratch_shapes=[pltpu.VMEM((tk, tn), F32)],
        compiler_params=_params(("parallel", "parallel", "arbitrary")))(a, b)


def _mm_tn_parts(name, a, parts, split_rows=False, tt=1024):
    t, k = a.shape
    widths = [p.shape[1] for p in parts]
    n_parts = len(parts)
    steps = t // tt
    shape = (sum(widths), k) if split_rows else (k, sum(widths))

    def body(*refs):
        a_ref, b_refs, o_ref, acc_ref = refs[0], refs[1:1 + n_parts], refs[1 + n_parts], refs[2 + n_parts]

        @pl.when(pl.program_id(0) == 0)
        def _():
            acc_ref[...] = jnp.zeros_like(acc_ref)

        av, off = a_ref[...], 0
        for b_ref, width in zip(b_refs, widths):
            if split_rows:
                acc_ref[off:off + width, :] += _dot_tn(b_ref[...], av)
            else:
                acc_ref[:, off:off + width] += _dot_tn(av, b_ref[...])
            off += width

        @pl.when(pl.program_id(0) == steps - 1)
        def _():
            o_ref[...] = acc_ref[...].astype(BF16)

    return pl.pallas_call(
        body, name=name, grid=(steps,), in_specs=[_rows(tt, k)] + [_rows(tt, w) for w in widths],
        out_specs=_whole(shape), out_shape=_sds(shape, BF16), scratch_shapes=[pltpu.VMEM(shape, F32)],
        compiler_params=_params(("arbitrary",)))(a, *parts)


def _alibi_table():
    arr = np.zeros((N_GROUPS, 8, 128), np.float32)
    for g in range(N_GROUPS):
        for hh in range(HEAD_GROUP):
            arr[g, hh, :] = 2.0 ** (-8.0 * (g * HEAD_GROUP + hh + 1) / N_HEADS)
    return jnp.asarray(arr)


def _spread_stats(cols, per_head=STAT_LANES):
    lane = lax.broadcasted_iota(jnp.int32, (BLK, HEAD_GROUP * per_head), 1)
    tile = cols[HEAD_GROUP - 1]
    for hh in range(HEAD_GROUP - 2, -1, -1):
        tile = jnp.where(lane < (hh + 1) * per_head, cols[hh], tile)
    return tile


def _block_mask(has_prev, dil):
    if has_prev is None:
        steps = lax.broadcasted_iota(jnp.int32, (BLK, BLK), 0) - lax.broadcasted_iota(jnp.int32, (BLK, BLK), 1)
        return steps >= 0, (steps * dil).astype(F32)
    qi = lax.broadcasted_iota(jnp.int32, (BLK, 2 * BLK), 0)
    ki = lax.broadcasted_iota(jnp.int32, (BLK, 2 * BLK), 1)
    steps = qi + BLK - ki
    valid = (steps >= 0) & (steps <= BLK) & ((ki >= BLK) | has_prev)
    return valid, (steps * dil).astype(F32)


def _bias_scratch():
    return pltpu.VMEM((2, HEAD_GROUP, BLK, 2 * BLK), F32)


def _fill_bias(bias, sl_ref, dil):
    for p in range(2):
        valid, dist = _block_mask(p == 1, dil)
        for hh in range(HEAD_GROUP):
            bias[p, hh] = jnp.where(valid, -sl_ref[hh:hh + 1, 0:1] * dist, NEG)


def _rows_of(j):
    return pl.ds(pl.multiple_of(j * BLK, BLK), BLK)


def _lane_half(hf):
    return slice(hf * 128, (hf + 1) * 128)


def _split_pair(x):
    first = lax.broadcasted_iota(jnp.int32, (1, 2 * HEAD_DIM), 1) < HEAD_DIM
    zero = jnp.zeros_like(x)
    return jnp.where(first, x, zero), jnp.where(first, zero, x)


def _deinterleave(src, dst, seq, dil, dtype):
    length = seq // dil
    for r in range(dil):
        for c in range(length // BLK):
            rows = pl.ds(r + c * BLK * dil, BLK, stride=dil)
            out = slice(r * length + c * BLK, r * length + (c + 1) * BLK)
            if len(src.shape) == 2:
                dst[out, :] = src[rows, :].astype(dtype)
            else:
                for hf in range(2):
                    dst[out, _lane_half(hf)] = src.at[hf][rows, :].astype(dtype)


def _interleave(src, dst, seq, dil, accumulate):
    length = seq // dil
    for r in range(dil):
        for c in range(length // BLK):
            rows = pl.ds(r + c * BLK * dil, BLK, stride=dil)
            inp = slice(r * length + c * BLK, r * length + (c + 1) * BLK)
            if len(dst.shape) == 2:
                dst[rows, :] = dst[rows, :] + src[inp, :] if accumulate else src[inp, :]
            else:
                for hf in range(2):
                    val = src[inp, _lane_half(hf)]
                    half = dst.at[hf]
                    half[rows, :] = half[rows, :] + val if accumulate else val


def _split_halves(src, dst, seq):
    def step(i, carry):
        for hf in range(2):
            dst[hf, _rows_of(i), :] = src[_rows_of(i), _lane_half(hf)].astype(F32)
        return carry

    lax.fori_loop(0, seq // BLK, step, 0)


def _band_attn_fwd_fused(h, slopes, bsz, seq, gather=None):
    k_off, v_off = MIX_W // 256, 2 * MIX_W // 256
    nb = seq // BLK

    ng = 0 if gather is None else len(gather)

    def body(*refs):
        sl_ref, q_ref, k_ref, v_ref = refs[:4]
        mix_ref, lse_ref = refs[4 + ng:6 + ng]
        qf, kf, vf, qd, kd, vd, od, ld, o1, o2, o3, l1, l2, l3, bias = refs[6 + 2 * ng:21 + 2 * ng]
        if ng:
            start, finish = _gather_protocol(refs[4:4 + ng], refs[6 + ng:6 + 2 * ng], *refs[21 + 2 * ng:])
            pl.when((pl.program_id(0) == 0) & (pl.program_id(1) == 0))(start)

        def run(dil, qs, ks, vs, o_dst, l_dst):
            nblk = seq // dil // BLK
            _fill_bias(bias, sl_ref, dil)

            def block(j, carry):
                rows, prows = _rows_of(j), _rows_of(jnp.maximum(j - 1, 0))
                has_prev = ((j % nblk) != 0).astype(jnp.int32)

                def keys(ref, lanes):
                    return jnp.concatenate([ref[prows, lanes], ref[rows, lanes]], axis=0)

                lses = []
                for pr in range(HEAD_GROUP // 2):
                    lanes = _lane_half(pr)
                    q_ab = _split_pair(qs[rows, lanes] * SCALE)
                    k2 = keys(ks, lanes)
                    v_ab = _split_pair(keys(vs, lanes))
                    out = None
                    for ab in range(2):
                        hh = 2 * pr + ab
                        s = _dot_nt(q_ab[ab], k2) + bias[has_prev, hh]
                        m = jnp.max(s, axis=-1, keepdims=True)
                        p = jnp.exp(s - m)
                        l = jnp.sum(p, axis=-1, keepdims=True)
                        term = _dot_nn(p.astype(BF16), v_ab[ab]) / l
                        out = term if out is None else out + term
                        lses.append(m + jnp.log(l))
                    o_dst[rows, lanes] = out
                l_dst[rows, :] = _spread_stats(lses, HEAD_DIM)
                return carry

            lax.fori_loop(0, nb, block, 0, unroll=8)

        run(1, q_ref, k_ref, v_ref, o1, l1)
        _split_halves(q_ref, qf, seq)
        _split_halves(k_ref, kf, seq)
        _split_halves(v_ref, vf, seq)
        for dil, o_tok, l_tok in ((4, o2, l2), (16, o3, l3)):
            _deinterleave(qf, qd, seq, dil, BF16)
            _deinterleave(kf, kd, seq, dil, BF16)
            _deinterleave(vf, vd, seq, dil, BF16)
            run(dil, qd, kd, vd, od, ld)
            _interleave(od, o_tok, seq, dil, False)
            _interleave(ld, l_tok, seq, dil, False)

        def merge(i, carry):
            rows = _rows_of(i)

            def both(ref):
                return jnp.concatenate([ref[0, rows, :], ref[1, rows, :]], axis=1)

            ls = [l1[rows, :], both(l2), both(l3)]
            m = jnp.maximum(jnp.maximum(ls[0], ls[1]), ls[2])
            tot = m + jnp.log(jnp.exp(ls[0] - m) + jnp.exp(ls[1] - m) + jnp.exp(ls[2] - m))
            ws = [jnp.exp(x - tot) for x in ls]
            mix_ref[rows, :] = (ws[0] * o1[rows, :] + ws[1] * both(o2) + ws[2] * both(o3)).astype(BF16)
            lse_ref[rows, :] = _spread_stats([tot[:, hh * HEAD_DIM:hh * HEAD_DIM + 1] for hh in range(HEAD_GROUP)])
            return carry

        lax.fori_loop(0, nb, merge, 0, unroll=2)
        if ng:
            pl.when((pl.program_id(0) == bsz - 1) & (pl.program_id(1) == N_GROUPS - 1))(finish)

    def hspec(off):
        return pl.BlockSpec((seq, 256), lambda b, g: (b, off + g))

    big = lambda dt: pltpu.VMEM((seq, 256), dt)
    halves = lambda: pltpu.VMEM((2, seq, 128), F32)
    outs = pl.pallas_call(
        body, name="band_attn_fwd_gather" if ng else "band_attn_fwd", grid=(bsz, N_GROUPS),
        in_specs=[pl.BlockSpec((None, 8, 128), lambda b, g: (g, 0, 0)), hspec(0), hspec(k_off), hspec(v_off)] + [_ANY] * ng,
        out_specs=[pl.BlockSpec((seq, 256), lambda b, g: (b, g)), pl.BlockSpec((seq, 128), lambda b, g: (b, g))] + [_ANY] * ng,
        out_shape=[_sds((bsz * seq, MIX_W), BF16), _sds((bsz * seq, STAT_W), F32)] + (_gather_shapes(gather) if ng else []),
        scratch_shapes=[halves(), halves(), halves(), big(BF16), big(BF16), big(BF16), big(F32), big(F32),
                        big(F32), halves(), halves(), big(F32), halves(), halves(), _bias_scratch()]
        + (_gather_sems(ng) if ng else []),
        compiler_params=_params(("arbitrary", "arbitrary")))(slopes, h, h, h, *(gather or []))
    return outs[0], outs[1], list(outs[2:])


def _band_attn_bwd_fused(h, dcat, mix, lse, slopes, bsz, seq, exchange=None):
    k_off, v_off = MIX_W // 256, 2 * MIX_W // 256
    nb = seq // BLK

    ne = 0 if exchange is None else len(exchange)

    def body(*refs):
        sl_ref, q_ref, k_ref, v_ref, do_ref, o_ref, lse_ref = refs[:7]
        dq_ref, dk_ref, dv_ref = refs[7 + ne:10 + ne]
        qf, kf, vf, dof, ddt, qd, kd, vd, dod, lsd, ddd, gq, gk, gv, aq, ak, av, bias = refs[10 + 2 * ne:28 + 2 * ne]
        if ne:
            start, finish = _exchange_protocol(refs[7:7 + ne], refs[10 + ne:10 + 2 * ne], *refs[28 + 2 * ne:])
            pl.when((pl.program_id(0) == 0) & (pl.program_id(1) == 0))(start)

        same_head = (lax.broadcasted_iota(jnp.int32, (HEAD_GROUP * HEAD_DIM, HEAD_GROUP * STAT_LANES), 0) // HEAD_DIM
                     == lax.broadcasted_iota(jnp.int32, (HEAD_GROUP * HEAD_DIM, HEAD_GROUP * STAT_LANES), 1) // STAT_LANES)
        ones_map = jnp.where(same_head, 1.0, 0.0).astype(BF16)

        def delta(i, carry):
            rows = _rows_of(i)
            prod = do_ref[rows, :].astype(F32) * o_ref[rows, :].astype(F32)
            high = prod.astype(BF16)
            rest = (prod - high.astype(F32)).astype(BF16)
            ddt[rows, :] = _dot_nn(high, ones_map) + _dot_nn(rest, ones_map)
            return carry

        lax.fori_loop(0, nb, delta, 0, unroll=2)

        def zero(i, carry):
            rows = _rows_of(i)
            for ref in (gk, gv):
                ref[rows, :] = jnp.zeros((BLK, 256), F32)
            return carry

        def run(dil, qs, ks, vs, dos, lss, dds):
            nblk = seq // dil // BLK
            _fill_bias(bias, sl_ref, dil)
            if nblk > 1:
                lax.fori_loop(0, nb, zero, 0)

            def block(j, carry):
                rows, prows = _rows_of(j), _rows_of(jnp.maximum(j - 1, 0))
                has_prev = ((j % nblk) != 0).astype(jnp.int32)

                def keys(ref, lanes):
                    if nblk == 1:
                        return ref[rows, lanes]
                    return jnp.concatenate([ref[prows, lanes], ref[rows, lanes]], axis=0)

                for pr in range(HEAD_GROUP // 2):
                    lanes = _lane_half(pr)
                    q_ab = _split_pair(qs[rows, lanes] * SCALE)
                    do_ab = _split_pair(dos[rows, lanes])
                    k2, v2 = keys(ks, lanes), keys(vs, lanes)
                    k_ab = _split_pair(k2)
                    dq, dk2, dv2 = None, None, None
                    for ab in range(2):
                        hh = 2 * pr + ab
                        st = slice(hh * STAT_LANES, hh * STAT_LANES + 1)
                        s = _dot_nt(q_ab[ab], k2) + (bias[0, hh, :, BLK:] if nblk == 1 else bias[has_prev, hh])
                        p = jnp.exp(s - lss[rows, st])
                        dp = _dot_nt(do_ab[ab], v2)
                        ds = (p * (dp - dds[rows, st])).astype(BF16)
                        terms = (_dot_nn(ds, k_ab[ab]), _dot_tn(ds, q_ab[ab]), _dot_tn(p.astype(BF16), do_ab[ab]))
                        dq, dk2, dv2 = terms if dq is None else (dq + terms[0], dk2 + terms[1], dv2 + terms[2])
                    gq[rows, lanes] = dq * SCALE
                    if nblk == 1:
                        gk[rows, lanes] = dk2
                        gv[rows, lanes] = dv2
                    else:
                        gk[prows, lanes] += dk2[:BLK]
                        gv[prows, lanes] += dv2[:BLK]
                        gk[rows, lanes] += dk2[BLK:]
                        gv[rows, lanes] += dv2[BLK:]
                return carry

            lax.fori_loop(0, nb, block, 0, unroll=4)

        run(1, q_ref, k_ref, v_ref, do_ref, lse_ref, ddt)

        for src, dst in ((gq, aq), (gk, ak), (gv, av), (q_ref, qf), (k_ref, kf), (v_ref, vf), (do_ref, dof)):
            _split_halves(src, dst, seq)
        for dil in (4, 16):
            for src, dst in ((qf, qd), (kf, kd), (vf, vd), (dof, dod)):
                _deinterleave(src, dst, seq, dil, BF16)
            _deinterleave(lse_ref, lsd, seq, dil, F32)
            _deinterleave(ddt, ddd, seq, dil, F32)
            run(dil, qd, kd, vd, dod, lsd, ddd)
            for src, dst in ((gq, aq), (gk, ak), (gv, av)):
                _interleave(src, dst, seq, dil, True)

        def write(i, carry):
            rows = _rows_of(i)
            for src, dst in ((aq, dq_ref), (ak, dk_ref), (av, dv_ref)):
                for hf in range(2):
                    dst[rows, _lane_half(hf)] = src[hf, rows, :].astype(BF16)
            return carry

        lax.fori_loop(0, nb, write, 0)
        if ne:
            pl.when((pl.program_id(0) == bsz - 1) & (pl.program_id(1) == N_GROUPS - 1))(finish)

    def hspec(off):
        return pl.BlockSpec((seq, 256), lambda b, g: (b, off + g))

    io = pl.BlockSpec((seq, 256), lambda b, g: (b, g))
    big = lambda dt: pltpu.VMEM((seq, 256), dt)
    halves = lambda: pltpu.VMEM((2, seq, 128), F32)
    stat = lambda: pltpu.VMEM((seq, 128), F32)
    outs = pl.pallas_call(
        body, name="band_attn_bwd_exchange" if ne else "band_attn_bwd", grid=(bsz, N_GROUPS),
        in_specs=[pl.BlockSpec((None, 8, 128), lambda b, g: (g, 0, 0)), hspec(0), hspec(k_off), hspec(v_off), io, io,
                  pl.BlockSpec((seq, 128), lambda b, g: (b, g))] + [_ANY] * ne,
        out_specs=[io, io, io] + [_ANY] * ne,
        out_shape=[_sds((bsz * seq, MIX_W), BF16)] * 3 + [_sds(s.shape, s.dtype) for s in (exchange or [])],
        scratch_shapes=[halves(), halves(), halves(), halves(), stat(),
                        big(BF16), big(BF16), big(BF16), big(BF16), stat(), stat(),
                        big(F32), big(F32), big(F32), halves(), halves(), halves(), _bias_scratch()]
        + (_exchange_sems(ne) if ne else []),
        compiler_params=_params(("arbitrary", "arbitrary")))(slopes, h, h, h, dcat, mix, lse, *(exchange or []))
    return list(outs[:3]), list(outs[3:])


def _mem_attn_fwd(h, mkv, bsz, seq, q_col, tq=1024):
    nq = seq // tq

    def body(q_ref, kv_ref, o_ref):
        for pr in range(2):
            lanes = _lane_half(pr)
            q_ab = _split_pair(q_ref[:, lanes])
            k = kv_ref[:, lanes]
            v_ab = _split_pair(kv_ref[:, MEM_W + pr * 128:MEM_W + (pr + 1) * 128])
            out = None
            for ab in range(2):
                s = _dot_nt(q_ab[ab], k) * SCALE
                m = jnp.max(s, axis=-1, keepdims=True)
                p = jnp.exp(s - m)
                l = jnp.sum(p, axis=-1, keepdims=True)
                term = _dot_nn(p.astype(BF16), v_ab[ab]) / l
                out = term if out is None else out + term
            o_ref[:, lanes] = out.astype(BF16)

    return pl.pallas_call(
        body, name="mem_attn_fwd", grid=(bsz, nq),
        in_specs=[pl.BlockSpec((tq, MEM_W), lambda b, i: (b * nq + i, q_col)),
                  pl.BlockSpec((N_MEM, 2 * MEM_W), lambda b, i: (b, 0))],
        out_specs=pl.BlockSpec((tq, MEM_W), lambda b, i: (b * nq + i, 0)),
        out_shape=_sds((bsz * seq, MEM_W), BF16), compiler_params=_params(("parallel", "parallel")))(h, mkv)


def _mem_attn_bwd(h, mkv, dcat, bsz, seq, q_col, tq=1024):
    nq = seq // tq
    do_col = MIX_W // MEM_W

    def body(q_ref, kv_ref, do_ref, dq_ref, dkv_ref):
        @pl.when(pl.program_id(1) == 0)
        def _():
            dkv_ref[...] = jnp.zeros_like(dkv_ref)

        for pr in range(2):
            lanes = _lane_half(pr)
            vlanes = slice(MEM_W + pr * 128, MEM_W + (pr + 1) * 128)
            q_ab = _split_pair(q_ref[:, lanes])
            do_ab = _split_pair(do_ref[:, lanes])
            k, v = kv_ref[:, lanes], kv_ref[:, vlanes]
            k_ab = _split_pair(k)
            dq, dk, dv = None, None, None
            for ab in range(2):
                s = _dot_nt(q_ab[ab], k) * SCALE
                m = jnp.max(s, axis=-1, keepdims=True)
                e = jnp.exp(s - m)
                p = e / jnp.sum(e, axis=-1, keepdims=True)
                dp = _dot_nt(do_ab[ab], v)
                dd = jnp.sum(p * dp, axis=-1, keepdims=True)
                ds = (p * (dp - dd) * SCALE).astype(BF16)
                terms = (_dot_nn(ds, k_ab[ab]), _dot_tn(ds, q_ab[ab]), _dot_tn(p.astype(BF16), do_ab[ab]))
                dq, dk, dv = terms if dq is None else (dq + terms[0], dk + terms[1], dv + terms[2])
            dq_ref[:, lanes] = dq.astype(BF16)
            dkv_ref[:, lanes] += dk
            dkv_ref[:, vlanes] += dv

    return pl.pallas_call(
        body, name="mem_attn_bwd", grid=(bsz, nq),
        in_specs=[pl.BlockSpec((tq, MEM_W), lambda b, i: (b * nq + i, q_col)),
                  pl.BlockSpec((N_MEM, 2 * MEM_W), lambda b, i: (b, 0)),
                  pl.BlockSpec((tq, MEM_W), lambda b, i: (b * nq + i, do_col))],
        out_specs=[pl.BlockSpec((tq, MEM_W), lambda b, i: (b * nq + i, 0)),
                   pl.BlockSpec((N_MEM, 2 * MEM_W), lambda b, i: (b, 0))],
        out_shape=[_sds((bsz * seq, MEM_W), BF16), _sds((bsz * N_MEM, 2 * MEM_W), F32)],
        compiler_params=_params(("parallel", "arbitrary")))(h, mkv, dcat)


_GELU_C = math.sqrt(2.0 / math.pi)
_GELU_A = 0.044715


def _gelu(x):
    return 0.5 * x * (1.0 + jnp.tanh(_GELU_C * (x + _GELU_A * x * x * x)))


def _gelu_grad(x):
    th = jnp.tanh(_GELU_C * (x + _GELU_A * x * x * x))
    return 0.5 * (1.0 + th) + 0.5 * x * (1.0 - th * th) * (_GELU_C * (1.0 + 3.0 * _GELU_A * x * x))


def _tril_mask(lower):
    ri = lax.broadcasted_iota(jnp.int32, (BLK, BLK), 0)
    ci = lax.broadcasted_iota(jnp.int32, (BLK, BLK), 1)
    return (ri >= ci) if lower else (ci >= ri)


def _sgu_fwd(h, ws, bs_t, ln_g, ln_b, tm=512):
    t = h.shape[0]

    def body(u_ref, v_ref, ws_ref, bs_ref, g_ref, b_ref, o_ref):
        ug = _gelu(u_ref[...].astype(F32))
        vhat, _ = _ln_hat(_gelu(v_ref[...].astype(F32)))
        vn = (vhat * g_ref[...] + b_ref[...]).astype(BF16)
        mask = _tril_mask(True)
        first = lax.broadcasted_iota(jnp.int32, (1, 2 * HEAD_DIM), 1) < HEAD_DIM
        for pr in range(N_HEADS // 2):
            lanes = _lane_half(pr)
            w_ab = [jnp.where(mask, ws_ref[2 * pr + ab], 0).astype(BF16) for ab in range(2)]
            bias = jnp.where(first, bs_ref[:, 2 * pr:2 * pr + 1], bs_ref[:, 2 * pr + 1:2 * pr + 2])
            for c in range(tm // BLK):
                rs = slice(c * BLK, (c + 1) * BLK)
                v_ab = _split_pair(vn[rs, lanes])
                mixed = _dot_nn(w_ab[0], v_ab[0]) + _dot_nn(w_ab[1], v_ab[1]) + bias
                o_ref[rs, lanes] = (ug[rs, lanes] * mixed).astype(BF16)

    return pl.pallas_call(
        body, name="sgu_fwd", grid=(t // tm,),
        in_specs=[_rows(tm, MIX_W, 0), _rows(tm, MIX_W, 1), _whole(ws.shape), _whole(bs_t.shape), _whole(ln_g.shape), _whole(ln_b.shape)],
        out_specs=_rows(tm, MIX_W), out_shape=_sds((t, MIX_W), BF16),
        compiler_params=_params(("parallel",)))(h, h, ws, bs_t, ln_g, ln_b)


def _sgu_bwd(h, dcat, ws, ws_t, bs_t, ln_g, ln_b, tm=512):
    t = h.shape[0]

    def body(u_ref, v_ref, do_ref, ws_ref, wst_ref, bs_ref, g_ref, b_ref, dh_ref, dws_ref, dbs_ref, dg_ref, db_ref, dvn_ref):
        @pl.when(pl.program_id(0) == 0)
        def _():
            dws_ref[...] = jnp.zeros_like(dws_ref)
            dbs_ref[...] = jnp.zeros_like(dbs_ref)
            dg_ref[...] = jnp.zeros_like(dg_ref)
            db_ref[...] = jnp.zeros_like(db_ref)

        u = u_ref[...].astype(F32)
        v = v_ref[...].astype(F32)
        do = do_ref[...].astype(F32)
        ug = _gelu(u)
        vhat, rstd = _ln_hat(_gelu(v))
        vn = (vhat * g_ref[...] + b_ref[...]).astype(BF16)
        dmixed_f = do * ug
        dmixed = dmixed_f.astype(BF16)
        low, upp = _tril_mask(True), _tril_mask(False)
        first = lax.broadcasted_iota(jnp.int32, (1, 2 * HEAD_DIM), 1) < HEAD_DIM
        for pr in range(N_HEADS // 2):
            lanes = _lane_half(pr)
            w_ab = [jnp.where(low, ws_ref[2 * pr + ab], 0).astype(BF16) for ab in range(2)]
            wt_ab = [jnp.where(upp, wst_ref[2 * pr + ab], 0).astype(BF16) for ab in range(2)]
            bias = jnp.where(first, bs_ref[:, 2 * pr:2 * pr + 1], bs_ref[:, 2 * pr + 1:2 * pr + 2])
            dws_acc = [None, None]
            dbs_acc = [None, None]
            for c in range(tm // BLK):
                rs = slice(c * BLK, (c + 1) * BLK)
                vn_pair = vn[rs, lanes]
                v_ab = _split_pair(vn_pair)
                mixed = _dot_nn(w_ab[0], v_ab[0]) + _dot_nn(w_ab[1], v_ab[1]) + bias
                dh_ref[rs, lanes] = (do[rs, lanes] * mixed * _gelu_grad(u[rs, lanes])).astype(BF16)
                dm_ab = _split_pair(dmixed[rs, lanes])
                dmf_ab = _split_pair(dmixed_f[rs, lanes])
                for ab in range(2):
                    term = _dot_nt(dm_ab[ab], vn_pair)
                    dws_acc[ab] = term if dws_acc[ab] is None else dws_acc[ab] + term
                    rsum = jnp.sum(dmf_ab[ab], axis=-1, keepdims=True)
                    dbs_acc[ab] = rsum if dbs_acc[ab] is None else dbs_acc[ab] + rsum
                dvn_ref[rs, lanes] = _dot_nn(wt_ab[0], dm_ab[0]) + _dot_nn(wt_ab[1], dm_ab[1])
            for ab in range(2):
                g = 2 * pr + ab
                dws_ref[g] += jnp.where(low, dws_acc[ab], 0.0)
                dbs_ref[:, g:g + 1] += dbs_acc[ab]
        dvn = dvn_ref[...]
        dg_ref[...] += jnp.sum(dvn * vhat, axis=0, keepdims=True)
        db_ref[...] += jnp.sum(dvn, axis=0, keepdims=True)
        dxh = dvn * g_ref[...]
        m1 = jnp.mean(dxh, axis=-1, keepdims=True)
        m2 = jnp.mean(dxh * vhat, axis=-1, keepdims=True)
        dvg = rstd * (dxh - m1 - vhat * m2)
        dh_ref[:, MIX_W:] = (dvg * _gelu_grad(v)).astype(BF16)

    return pl.pallas_call(
        body, name="sgu_bwd", grid=(t // tm,),
        in_specs=[_rows(tm, MIX_W, 0), _rows(tm, MIX_W, 1), _rows(tm, MIX_W, 0), _whole(ws.shape), _whole(ws_t.shape),
                  _whole(bs_t.shape), _whole(ln_g.shape), _whole(ln_b.shape)],
        out_specs=[_rows(tm, 2 * MIX_W), _whole(ws.shape), _whole(bs_t.shape), _whole((1, MIX_W)), _whole((1, MIX_W))],
        out_shape=[_sds((t, 2 * MIX_W), BF16), _sds(ws.shape, F32), _sds(bs_t.shape, F32), _sds((1, MIX_W), F32), _sds((1, MIX_W), F32)],
        scratch_shapes=[pltpu.VMEM((tm, MIX_W), F32)],
        compiler_params=_params(("arbitrary",)))(h, h, dcat, ws, ws_t, bs_t, ln_g, ln_b)


def _row_tile(rows, cols, itemsize=4, limit=2 ** 21):
    best = rows
    for cand in (4096, 2048, 1024, 512, 256, 128, 64, 32, 16):
        if rows % cand == 0 and rows > cand:
            best = cand
            if cand * cols * itemsize <= limit:
                break
    return best


def _adamw(w, m, v, grad=None, parts=None, first_parts=None):
    rows, cols = w.shape
    rows0 = 0 if first_parts is None else first_parts.shape[1]
    tr = _row_tile(rows0 if rows0 else rows, cols)
    n0 = rows0 // tr

    def chip_sum(ref):
        acc = ref[0].astype(F32)
        for k in range(1, 4):
            acc = acc + ref[k].astype(F32)
        return acc

    def body(*refs):
        w_ref, m_ref, v_ref = refs[:3]
        go_ref, d_ref, nm_ref, nv_ref = refs[-4:]
        if parts is None:
            gv = refs[3][...]
        elif first_parts is None:
            gv = chip_sum(refs[3])
        else:
            gv = jnp.where(pl.program_id(0) < n0, chip_sum(refs[3]), chip_sum(refs[4]))
        nm = ADAM_B1 * m_ref[...] + (1.0 - ADAM_B1) * gv
        nv = ADAM_B2 * v_ref[...] + (1.0 - ADAM_B2) * (gv * gv)
        m_hat = nm / (1.0 - ADAM_B1 ** ADAM_STEP)
        v_hat = nv / (1.0 - ADAM_B2 ** ADAM_STEP)
        go_ref[...] = gv
        d_ref[...] = -ADAM_LR * (m_hat / (jnp.sqrt(v_hat) + ADAM_EPS) + ADAM_WD * w_ref[...])
        nm_ref[...] = nm
        nv_ref[...] = nv

    spec = _rows(tr, cols)
    if parts is None:
        g_specs, g_args = [spec], [grad]
    elif first_parts is None:
        g_specs, g_args = [pl.BlockSpec((4, tr, cols), lambda i: (0, i, 0))], [parts]
    else:
        g_specs = [pl.BlockSpec((4, tr, cols), lambda i: (0, jnp.minimum(i, n0 - 1), 0)),
                   pl.BlockSpec((4, tr, cols), lambda i: (0, jnp.maximum(i - n0, 0), 0))]
        g_args = [first_parts, parts]
    return pl.pallas_call(
        body, name="adamw" if parts is None else "adamw_sum_chips", grid=(rows // tr,), in_specs=[spec] * 3 + g_specs,
        out_specs=[spec] * 4, out_shape=[_sds(w.shape, F32)] * 4,
        compiler_params=_params(("parallel",)))(w, m, v, *g_args)


_ANY = pl.BlockSpec(memory_space=pl.ANY)
_MESH = pl.DeviceIdType.MESH


def _all_gather(name, blocks):
    nt = len(blocks)

    def body(*refs):
        start, finish = _gather_protocol(refs[:nt], refs[nt:2 * nt], *refs[2 * nt:])
        start()
        finish()

    return pl.pallas_call(
        body, name=name, out_shape=_gather_shapes(blocks), in_specs=[_ANY] * nt, out_specs=[_ANY] * nt,
        scratch_shapes=_gather_sems(nt))(*blocks)


def _gather_shapes(blocks):
    return [_sds((N_DEV,) + b.shape, b.dtype) for b in blocks]


def _gather_sems(nt):
    return [pltpu.SemaphoreType.DMA((nt, 7)), pltpu.SemaphoreType.DMA((nt, 7)), pltpu.SemaphoreType.DMA((nt,))]


def _gather_protocol(x_refs, out_refs, send_sems, recv_sems, local_sems):
    nt = len(x_refs)
    x, y, c = lax.axis_index("x"), lax.axis_index("y"), lax.axis_index("c")
    me, sibling = (x, y, c), (x, y, 1 - c)
    chips = [(1 - x, y), (x, 1 - y), (1 - x, 1 - y)]

    def slot(t, px, py, pc):
        return out_refs[t].at[4 * px + 2 * py + pc]

    def copy(t, k, blk, to, src=None):
        return pltpu.make_async_remote_copy(
            src_ref=slot(t, *blk) if src is None else src, dst_ref=slot(t, *blk),
            send_sem=send_sems.at[t, k], recv_sem=recv_sems.at[t, k], device_id=to, device_id_type=_MESH)

    def own_copies():
        mine = [pltpu.make_async_copy(x_refs[t], slot(t, *me), local_sems.at[t]) for t in range(nt)]
        first = []
        for t in range(nt):
            first.append(copy(t, 0, me, sibling, src=x_refs[t]))
            first += [copy(t, 1 + j, me, (*chip, c), src=x_refs[t]) for j, chip in enumerate(chips)]
        return mine, first

    def start():
        mine, first = own_copies()
        for cp in mine + first:
            cp.start()

    def finish():
        mine, first = own_copies()
        passed = []
        for j, chip in enumerate(chips):
            for t in range(nt):
                copy(t, 1 + j, (*chip, c), me).wait_recv()
                fwd = copy(t, 4 + j, (*chip, c), sibling)
                fwd.start()
                passed.append(fwd)
        for t in range(nt):
            copy(t, 0, sibling, me).wait_recv()
        for j, chip in enumerate(chips):
            for t in range(nt):
                copy(t, 4 + j, (*chip, 1 - c), me).wait_recv()
        for cp in first + passed:
            cp.wait_send()
        for cp in mine:
            cp.wait()

    return start, finish


def _swap_with_sibling(packed):
    nt = len(packed)

    def body(*refs):
        p_refs, got_refs = refs[:nt], refs[nt:2 * nt]
        send_sems, recv_sems = refs[2 * nt:]
        x, y, c = lax.axis_index("x"), lax.axis_index("y"), lax.axis_index("c")
        copies = [
            pltpu.make_async_remote_copy(
                src_ref=p_refs[t].at[1 - c], dst_ref=got_refs[t], send_sem=send_sems.at[t], recv_sem=recv_sems.at[t],
                device_id=(x, y, 1 - c), device_id_type=_MESH)
            for t in range(nt)]
        for cp in copies:
            cp.start()
        for cp in copies:
            cp.wait_recv()
        for cp in copies:
            cp.wait_send()

    return pl.pallas_call(
        body, name="grad_swap_sibling", out_shape=[_sds(p.shape[1:], p.dtype) for p in packed], in_specs=[_ANY] * nt,
        out_specs=[_ANY] * nt,
        scratch_shapes=[pltpu.SemaphoreType.DMA((nt,)), pltpu.SemaphoreType.DMA((nt,))])(*packed)


def _chip_sum(packed, got):
    _, nchip, rows, cols = packed.shape
    tr = _row_tile(rows, cols, 2)
    core = lax.axis_index("c").astype(jnp.int32).reshape(1)

    def body(c_ref, p_ref, g_ref, o_ref):
        o_ref[...] = (p_ref[...].astype(F32) + g_ref[...].astype(F32)).astype(o_ref.dtype)

    grid_spec = pltpu.PrefetchScalarGridSpec(
        num_scalar_prefetch=1, grid=(nchip, rows // tr),
        in_specs=[pl.BlockSpec((None, None, tr, cols), lambda k, i, c: (c[0], k, i, 0)),
                  pl.BlockSpec((None, tr, cols), lambda k, i, c: (k, i, 0))],
        out_specs=pl.BlockSpec((None, tr, cols), lambda k, i, c: (k, i, 0)))
    return pl.pallas_call(
        body, name="grad_chip_sum", grid_spec=grid_spec, out_shape=_sds(got.shape, got.dtype),
        compiler_params=_params(("parallel", "parallel")))(core, packed, got)


def _exchange_chips(chip_sums):
    nt = len(chip_sums)

    def body(*refs):
        start, finish = _exchange_protocol(refs[:nt], refs[nt:2 * nt], *refs[2 * nt:])
        start()
        finish()

    return pl.pallas_call(
        body, name="grad_exchange_chips", out_shape=[_sds(s.shape, s.dtype) for s in chip_sums], in_specs=[_ANY] * nt,
        out_specs=[_ANY] * nt, scratch_shapes=_exchange_sems(nt))(*chip_sums)


def _exchange_sems(nt):
    return [pltpu.SemaphoreType.DMA((nt, 3)), pltpu.SemaphoreType.DMA((nt, 3)), pltpu.SemaphoreType.DMA((nt,))]


def _exchange_protocol(s_refs, got_refs, send_sems, recv_sems, local_sems):
    nt = len(s_refs)
    x, y, c = lax.axis_index("x"), lax.axis_index("y"), lax.axis_index("c")
    my_chip = 2 * x + y
    chips = [(1 - x, y), (x, 1 - y), (1 - x, 1 - y)]

    def copy(t, j, src_chip, dst_chip):
        px, py = chips[j]
        return pltpu.make_async_remote_copy(
            src_ref=s_refs[t].at[src_chip], dst_ref=got_refs[t].at[dst_chip], send_sem=send_sems.at[t, j],
            recv_sem=recv_sems.at[t, j], device_id=(px, py, c), device_id_type=_MESH)

    def own_copies():
        mine = [pltpu.make_async_copy(s_refs[t].at[my_chip], got_refs[t].at[my_chip], local_sems.at[t]) for t in range(nt)]
        sends = [copy(t, j, 2 * px + py, my_chip) for t in range(nt) for j, (px, py) in enumerate(chips)]
        return mine, sends

    def start():
        mine, sends = own_copies()
        for cp in mine + sends:
            cp.start()

    def finish():
        mine, sends = own_copies()
        for j, (px, py) in enumerate(chips):
            for t in range(nt):
                copy(t, j, my_chip, 2 * px + py).wait_recv()
        for cp in sends:
            cp.wait_send()
        for cp in mine:
            cp.wait()

    return start, finish


def _sum_chips(got):
    _, rows, cols = got.shape
    tr = _row_tile(rows, cols)

    def body(g_ref, o_ref):
        acc = g_ref[0].astype(F32)
        for k in range(1, 4):
            acc = acc + g_ref[k].astype(F32)
        o_ref[...] = acc

    return pl.pallas_call(
        body, name="grad_sum_chips", grid=(rows // tr,), in_specs=[pl.BlockSpec((4, tr, cols), lambda i: (0, i, 0))],
        out_specs=pl.BlockSpec((tr, cols), lambda i: (i, 0)), out_shape=_sds((rows, cols), F32),
        compiler_params=_params(("parallel",)))(got)


_COL_SHARDED = ("a_w_in", "b_w_in", "w_gate", "w_up")
_ROW_SHARDED = ("w_mem_kv", "w_out", "w_down")
_BIG = ("a_w_in", "b_w_in", "w_mem_kv", "w_out", "w_gate", "w_up", "w_down")
_SGU_LN = ("sgu_ln_g", "sgu_ln_b")
_LN4 = ("ln_mix_g", "ln_mix_b", "ln_ffn_g", "ln_ffn_b")
_REPLICATED = ("sgu_w_s", "sgu_b_s") + _LN4


def _unshard(name, gathered):
    if name in _COL_SHARDED or name in _SGU_LN:
        moved = jnp.moveaxis(gathered, 0, -2)
        return moved.reshape(moved.shape[:-2] + (moved.shape[-2] * moved.shape[-1],))
    assert name in _ROW_SHARDED, name
    moved = jnp.moveaxis(gathered, 0, 1)
    return moved.reshape((moved.shape[0], moved.shape[1] * moved.shape[2]) + moved.shape[3:])


_LAID_OUT_IN_KERNEL = _COL_SHARDED


def _after_gather(name, gathered):
    return gathered if name in _LAID_OUT_IN_KERNEL else _unshard(name, gathered)


def _by_shard(name, full):
    if name in _COL_SHARDED or name in _SGU_LN:
        split = full.reshape(full.shape[:-1] + (N_DEV, full.shape[-1] // N_DEV))
        return jnp.moveaxis(split, -2, 0)
    split = full.reshape((full.shape[0], N_DEV, full.shape[1] // N_DEV) + full.shape[2:])
    return jnp.moveaxis(split, 1, 0)


def _layer_keys(i):
    return [("a_w_in" if i % 2 == 0 else "b_w_in", i // 2)] + [(n, i) for n in ("w_mem_kv", "w_out", "w_gate", "w_up", "w_down")]


_GATHER_FIRST = _layer_keys(0)[:2]
_GATHER_LATER = (_layer_keys(0)[2:] + _layer_keys(1), _layer_keys(2)[:4], _layer_keys(2)[4:],
                 _layer_keys(3)[:4], _layer_keys(3)[4:])


def _shard_block(shards, key):
    name, idx = key
    return shards[name][idx:idx + 1].astype(BF16)


def _gather_first(shards):
    blocks = [_shard_block(shards, k) for k in _GATHER_FIRST] + [shards[n] for n in _SGU_LN]
    gathered = _all_gather("first_all_gather", blocks)
    full = {k: _after_gather(k[0], g) for k, g in zip(_GATHER_FIRST, gathered)}
    sgu_ln = {n: _unshard(n, g) for n, g in zip(_SGU_LN, gathered[len(_GATHER_FIRST):])}
    return full, sgu_ln


def _two_level(by_dest):
    shp = by_dest.shape[1:]
    split = by_dest.astype(BF16).reshape((4, 2) + shp).swapaxes(0, 1)
    return split.reshape(2, 4, int(np.prod(shp[:-1])), shp[-1])


_EARLY = _BIG + _SGU_LN


def _chip_sums_of_early(grads):
    packed = [_two_level(_by_shard(n, jnp.stack(grads[n][1:] if n == "a_w_in" else grads[n]))) for n in _EARLY]
    ln4 = jnp.stack([jnp.stack(grads[n]) for n in _LN4])
    rep = [jnp.stack(grads["sgu_w_s"]).reshape(N_DEV, -1, BLK), jnp.stack(grads["sgu_b_s"]).reshape(N_DEV, -1, BLK),
           ln4.reshape(N_DEV, -1, D_MODEL)]
    packed += [_two_level(r) for r in rep]
    got = _swap_with_sibling(packed)
    return [_chip_sum(p, g) for p, g in zip(packed, got)]


def _finish_replicated(parts, shapes):
    w_s, b_s, ln_all = _all_gather("replicated_grads_all_gather", [_sum_chips(p) for p in parts])
    ln_all = ln_all.reshape((len(_LN4),) + tuple(shapes[_LN4[0]]))
    rep_grads = {"sgu_w_s": w_s.reshape(shapes["sgu_w_s"]), "sgu_b_s": b_s.reshape(shapes["sgu_b_s"])}
    rep_grads.update({n: ln_all[i] for i, n in enumerate(_LN4)})
    return rep_grads


def _reduce_last(grad_a_first):
    packed = [_two_level(_by_shard("a_w_in", grad_a_first))]
    got = _swap_with_sibling(packed)
    return _exchange_chips([_chip_sum(packed[0], got[0])])[0]


def _as_2d(a):
    if a.ndim == 1:
        return a.reshape(1, -1)
    return a.reshape(-1, a.shape[-1])


def kernel(x, mem, a_w_in, b_w_in, sgu_ln_g, sgu_ln_b, sgu_w_s, sgu_b_s, w_mem_kv, w_out, ln_mix_g, ln_mix_b, w_gate, w_up, w_down, ln_ffn_g, ln_ffn_b, loss_target, m_a_w_in, m_b_w_in, m_sgu_ln_g, m_sgu_ln_b, m_sgu_w_s, m_sgu_b_s, m_w_mem_kv, m_w_out, m_ln_mix_g, m_ln_mix_b, m_w_gate, m_w_up, m_w_down, m_ln_ffn_g, m_ln_ffn_b, v_a_w_in, v_b_w_in, v_sgu_ln_g, v_sgu_ln_b, v_sgu_w_s, v_sgu_b_s, v_w_mem_kv, v_w_out, v_ln_mix_g, v_ln_mix_b, v_w_gate, v_w_up, v_w_down, v_ln_ffn_g, v_ln_ffn_b):
    names = ("a_w_in", "b_w_in", "sgu_ln_g", "sgu_ln_b", "sgu_w_s", "sgu_b_s", "w_mem_kv", "w_out", "ln_mix_g", "ln_mix_b",
             "w_gate", "w_up", "w_down", "ln_ffn_g", "ln_ffn_b")
    weights = dict(zip(names, (a_w_in, b_w_in, sgu_ln_g, sgu_ln_b, sgu_w_s, sgu_b_s, w_mem_kv, w_out, ln_mix_g, ln_mix_b,
                               w_gate, w_up, w_down, ln_ffn_g, ln_ffn_b)))
    mom_m = dict(zip(names, (m_a_w_in, m_b_w_in, m_sgu_ln_g, m_sgu_ln_b, m_sgu_w_s, m_sgu_b_s, m_w_mem_kv, m_w_out, m_ln_mix_g,
                             m_ln_mix_b, m_w_gate, m_w_up, m_w_down, m_ln_ffn_g, m_ln_ffn_b)))
    mom_v = dict(zip(names, (v_a_w_in, v_b_w_in, v_sgu_ln_g, v_sgu_ln_b, v_sgu_w_s, v_sgu_b_s, v_w_mem_kv, v_w_out, v_ln_mix_g,
                             v_ln_mix_b, v_w_gate, v_w_up, v_w_down, v_ln_ffn_g, v_ln_ffn_b)))
    full, sgu_ln = _gather_first(weights)
    pending = [(keys, [_shard_block(weights, k) for k in keys]) for keys in _GATHER_LATER]
    loss_part, grad_x, local, early = _local_step(
        x, mem, loss_target, full, sgu_ln, {n: weights[n] for n in _REPLICATED}, pending)
    loss = lax.psum(loss_part[0, 0], ("x", "y", "c"))
    early_parts = dict(zip(_EARLY, early))
    rep_grads = _finish_replicated(early[len(_EARLY):], {n: weights[n].shape for n in _REPLICATED})
    a_first_parts = _reduce_last(local["a_w_in"][:1])

    reduced, deltas, new_m, new_v = {}, {}, {}, {}
    for n in names:
        w2, m2, v2 = _as_2d(weights[n]), _as_2d(mom_m[n]), _as_2d(mom_v[n])
        if n in early_parts:
            outs = _adamw(w2, m2, v2, parts=early_parts[n], first_parts=a_first_parts if n == "a_w_in" else None)
        else:
            outs = _adamw(w2, m2, v2, grad=_as_2d(rep_grads[n]))
        reduced[n], deltas[n], new_m[n], new_v[n] = (a.reshape(weights[n].shape) for a in outs)

    return (loss, grad_x, *[reduced[n] for n in names], *[deltas[n] for n in names],
            *[new_m[n] for n in names], *[new_v[n] for n in names])


def _local_step(x, mem, loss_target, full, sgu_ln, small, pending=None):
    sgu_w_s, sgu_b_s = small["sgu_w_s"], small["sgu_b_s"]
    ln_mix_g, ln_mix_b, ln_ffn_g, ln_ffn_b = (small[n] for n in ("ln_mix_g", "ln_mix_b", "ln_ffn_g", "ln_ffn_b"))
    bsz, seq, _ = x.shape
    tokens = bsz * seq
    slopes = _alibi_table()
    full = dict(full)
    exchanging = pending is not None
    pending = list(pending or [])

    def weight(name, idx):
        return full[(name, idx)][0]

    def next_group():
        return pending[0][1] if pending else None

    def landed(gathered):
        if gathered:
            keys, _ = pending.pop(0)
            full.update({k: _after_gather(k[0], g) for k, g in zip(keys, gathered)})

    res = (x.reshape(tokens, D_MODEL),)
    xb = res[0].astype(BF16)
    memb = mem.reshape(bsz * N_MEM, D_MODEL).astype(BF16)
    tgt = loss_target.reshape(tokens, D_MODEL)

    saved = []
    for i in range(DEPTH):
        j = i // 2
        dil_layer = i % 2 == 0
        mkv = _linear_nn("mem_kv", memb, weight("w_mem_kv", i))
        h, w_in = _linear_nn_gathered("in_proj_a" if dil_layer else "in_proj_b", xb,
                                      full[("a_w_in" if dil_layer else "b_w_in", j)][:, 0])
        st = dict(xb=xb, h=h, mkv=mkv, w_in=w_in)
        if dil_layer:
            mix, st["lse"], gathered = _band_attn_fwd_fused(h, slopes, bsz, seq, gather=next_group() if i == 0 else None)
            landed(gathered)
            q_col = 3 * MIX_W // MEM_W
        else:
            st["ws"] = sgu_w_s[j]
            st["bs_t"] = sgu_b_s[j].T
            st["ln_g"] = sgu_ln["sgu_ln_g"][j].reshape(1, MIX_W)
            st["ln_b"] = sgu_ln["sgu_ln_b"][j].reshape(1, MIX_W)
            mix = _sgu_fwd(h, st["ws"], st["bs_t"], st["ln_g"], st["ln_b"])
            q_col = 2 * MIX_W // MEM_W
        mo = _mem_attn_fwd(h, mkv, bsz, seq, q_col)
        w_out, w_down = weight("w_out", i), weight("w_down", i)
        mix_ln = (ln_mix_g[i].reshape(1, D_MODEL), ln_mix_b[i].reshape(1, D_MODEL))
        ffn_ln = (ln_ffn_g[i].reshape(1, D_MODEL), ln_ffn_b[i].reshape(1, D_MODEL))
        r1, x1b, _ = _proj_ln_fwd("out_proj_ln", [mix, mo], w_out, res, *mix_ln)
        gt, up, act, w_gu, gathered = _ffn_up_fwd(x1b, full[("w_gate", i)][:, 0], full[("w_up", i)][:, 0],
                                                  gather=next_group() if i < 2 else None)
        landed(gathered)
        r2, xb, gathered = _proj_ln_fwd("ffn_down_ln", [act], w_down, (r1, *mix_ln), *ffn_ln,
                                        gather=next_group() if i < 2 else None)
        landed(gathered)
        res = (r2, *ffn_ln)
        st.update(mix=mix, mo=mo, q_col=q_col, r1=r1, x1b=x1b, gt=gt, up=up, act=act, r2=r2,
                  w_out=w_out, w_down=w_down, w_gu=w_gu)
        saved.append(st)

    dr2, dr2b, dg, db, loss_part = _loss_ln_bwd(*res, tgt)

    early_parts = None
    per_pair = ("a_w_in", "b_w_in", "sgu_ln_g", "sgu_ln_b", "sgu_w_s", "sgu_b_s")
    grads = {n: [None] * (DEPTH // 2 if n in per_pair else DEPTH) for n in _BIG + _SGU_LN + _REPLICATED}
    for i in reversed(range(DEPTH)):
        j = i // 2
        st = saved[i]
        dil_layer = i % 2 == 0
        w_in = st["w_in"]
        grads["ln_ffn_g"][i], grads["ln_ffn_b"][i] = dg[0], db[0]
        dgu = _ffn_down_bwd(dr2b, st["w_down"], st["gt"], st["up"])
        grads["w_down"][i] = _mm_tn("grad_w_down", st["act"], dr2b)
        dr1, dr1b, dg, db = _linear_nt("ffn_up_bwd", [dgu], st["w_gu"], dr2, F32,
                                       ln=(st["r1"], ln_mix_g[i].reshape(1, D_MODEL)))
        grads["ln_mix_g"][i], grads["ln_mix_b"][i] = dg[0], db[0]
        grads["w_gate"][i] = _mm_tn("grad_w_gate_up", st["x1b"], dgu, cols=(0, D_FF))
        grads["w_up"][i] = _mm_tn("grad_w_gate_up", st["x1b"], dgu, cols=(D_FF, 2 * D_FF))
        dcat = _linear_nt("out_proj_bwd", [dr1b], st["w_out"], None, BF16)
        grads["w_out"][i] = _mm_tn_parts("grad_w_out", dr1b, [st["mix"], st["mo"]], split_rows=True)
        dqm, dmkv = _mem_attn_bwd(st["h"], st["mkv"], dcat, bsz, seq, st["q_col"])
        grads["w_mem_kv"][i] = _mm_tn("grad_w_mem_kv", memb, dmkv.astype(BF16))
        if dil_layer:
            early_sums = _chip_sums_of_early(grads) if (i == 0 and exchanging) else None
            dh_parts, exchanged = _band_attn_bwd_fused(st["h"], dcat, st["mix"], st["lse"], slopes, bsz, seq,
                                                       exchange=early_sums)
            if early_sums is not None:
                early_parts = exchanged
        else:
            ws_t = jnp.swapaxes(st["ws"], -1, -2)
            dh_main, dws, dbs_t, dlg, dlb = _sgu_bwd(st["h"], dcat, st["ws"], ws_t, st["bs_t"], st["ln_g"], st["ln_b"])
            grads["sgu_w_s"][j], grads["sgu_b_s"][j] = dws, dbs_t.T
            grads["sgu_ln_g"][j], grads["sgu_ln_b"][j] = dlg[0], dlb[0]
            dh_parts = [dh_main]
        name = "in_proj_bwd_a" if dil_layer else "in_proj_bwd_b"
        if i > 0:
            dr2, dr2b, dg, db = _linear_nt(name, [*dh_parts, dqm], w_in, dr1, F32,
                                           ln=(saved[i - 1]["r2"], ln_ffn_g[i - 1].reshape(1, D_MODEL)))
        else:
            grad_x = _linear_nt(name + "_x", [*dh_parts, dqm], w_in, dr1, F32).reshape(x.shape)
        grads["a_w_in" if dil_layer else "b_w_in"][j] = _mm_tn_parts(
            "grad_w_in_a" if dil_layer else "grad_w_in_b", st["xb"], [*dh_parts, dqm])
    return loss_part, grad_x, {n: jnp.stack(g) for n, g in grads.items()}, early_parts
```

```python
import math

import numpy as np
import jax
import jax.numpy as jnp
from jax import lax
from jax.experimental import pallas as pl
from jax.experimental.pallas import tpu as pltpu

F32 = jnp.float32
BF16 = jnp.bfloat16

D_MODEL = 1024
DEPTH = 4
N_MEM = 256
HEAD_DIM = 64
N_HEADS = 12
MIX_W = N_HEADS * HEAD_DIM
MEM_W = 4 * HEAD_DIM
BLK = 128
HEAD_GROUP = 4
N_GROUPS = N_HEADS // HEAD_GROUP
D_FF = 2816
FF_CHUNKS = 2
ALPHA = (2 * DEPTH) ** 0.25
LN_EPS = 1e-5
SCALE = HEAD_DIM ** -0.5
NEG = -1e30
N_DEV = 8

ADAM_LR, ADAM_B1, ADAM_B2, ADAM_EPS, ADAM_WD, ADAM_STEP = 0.001, 0.9, 0.999, 1e-08, 0.01, 10

VMEM_LIMIT = 56 * 2 ** 20
STAT_LANES = 32
STAT_W = N_HEADS * STAT_LANES


def _dot_nn(a, b):
    return lax.dot_general(a, b, (((1,), (0,)), ((), ())), preferred_element_type=F32)


def _dot_nt(a, b):
    return lax.dot_general(a, b, (((1,), (1,)), ((), ())), preferred_element_type=F32)


def _dot_tn(a, b):
    return lax.dot_general(a, b, (((0,), (0,)), ((), ())), preferred_element_type=F32)


def _ln_hat(r):
    mu = jnp.mean(r, axis=-1, keepdims=True)
    xc = r - mu
    var = jnp.mean(xc * xc, axis=-1, keepdims=True)
    rstd = lax.rsqrt(var + LN_EPS)
    return xc * rstd, rstd


def _params(sem):
    return pltpu.CompilerParams(dimension_semantics=sem, vmem_limit_bytes=VMEM_LIMIT)


def _rows(tm, c, col=0):
    return pl.BlockSpec((tm, c), lambda i: (i, col))


def _whole(shape):
    nd = len(shape)
    return pl.BlockSpec(tuple(shape), lambda *_: (0,) * nd)


def _resident(shape):
    nd = len(shape)
    return pl.BlockSpec(tuple(shape), lambda *_: (0,) * nd, pipeline_mode=pl.Buffered(1))


def _sds(shape, dtype):
    return jax.ShapeDtypeStruct(tuple(shape), dtype)


def _linear_nn(name, a, w, tm=512):
    t, k = a.shape
    n = w.shape[1]
    tm = min(tm, t)

    def body(a_ref, w_ref, o_ref):
        o_ref[...] = _dot_nn(a_ref[...], w_ref[...]).astype(BF16)

    return pl.pallas_call(
        body, name=name, grid=(t // tm,), in_specs=[_rows(tm, k), _resident(w.shape)], out_specs=_rows(tm, n),
        out_shape=_sds((t, n), BF16), compiler_params=_params(("parallel",)))(a, w)


def _linear_nn_gathered(name, a, shards, tm=512):
    t, k = a.shape
    n8 = shards.shape[2]
    n = N_DEV * n8

    def body(a_ref, s_ref, o_ref, w_ref):
        @pl.when(pl.program_id(0) == 0)
        def _():
            for s in range(N_DEV):
                w_ref[:, s * n8:(s + 1) * n8] = s_ref[s]

        o_ref[...] = _dot_nn(a_ref[...], w_ref[...]).astype(BF16)

    return pl.pallas_call(
        body, name=name, grid=(t // tm,), in_specs=[_rows(tm, k), _resident(shards.shape)],
        out_specs=[_rows(tm, n), _whole((k, n))], out_shape=[_sds((t, n), BF16), _sds((k, n), BF16)],
        compiler_params=_params(("arbitrary",)))(a, shards)


def _proj_ln_fwd(name, lhs, w, res, g, b, gather=None, tm=512):
    t = res[0].shape[0]
    n_lhs = len(lhs)
    n_res = len(res)
    ng = 0 if gather is None else len(gather)
    steps = t // tm
    n_in = n_lhs + 3 + n_res

    def body(*refs):
        lhs_refs = refs[:n_lhs]
        w_ref = refs[n_lhs]
        res_refs = refs[n_lhs + 1:n_lhs + 1 + n_res]
        g_ref, b_ref = refs[n_in - 2:n_in]
        r_ref, xnb_ref = refs[n_in + ng:n_in + ng + 2]
        if ng:
            start, finish = _gather_protocol(refs[n_in:n_in + ng], refs[n_in + ng + 2:n_in + 2 * ng + 2], *refs[n_in + 2 * ng + 2:])
            pl.when(pl.program_id(0) == 0)(start)
        y, off = None, 0
        for lr in lhs_refs:
            k = lr.shape[1]
            term = _dot_nn(lr[...], w_ref[off:off + k, :])
            y = term if y is None else y + term
            off += k
        x_res = res_refs[0][...]
        if n_res == 3:
            x_res = _ln_hat(x_res)[0] * res_refs[1][...] + res_refs[2][...]
        r = ALPHA * x_res + y
        r_ref[...] = r
        xnb_ref[...] = (_ln_hat(r)[0] * g_ref[...] + b_ref[...]).astype(BF16)
        if ng:
            pl.when(pl.program_id(0) == steps - 1)(finish)

    vec = _whole((1, D_MODEL))
    in_specs = ([_rows(tm, a.shape[1]) for a in lhs] + [_resident(w.shape), _rows(tm, D_MODEL)] + [vec] * (n_res - 1) + [vec, vec])
    outs = pl.pallas_call(
        body, name=name + "_gather" if ng else name, grid=(steps,), in_specs=in_specs + [_ANY] * ng,
        out_specs=[_rows(tm, D_MODEL)] * 2 + [_ANY] * ng,
        out_shape=[_sds((t, D_MODEL), F32), _sds((t, D_MODEL), BF16)] + (_gather_shapes(gather) if ng else []),
        scratch_shapes=_gather_sems(ng) if ng else [],
        compiler_params=_params(("arbitrary" if ng else "parallel",)))(*lhs, w, *res, g, b, *(gather or []))
    return outs[0], outs[1], list(outs[2:])


def _ffn_up_fwd(xb, gate_shards, up_shards, gather=None, tm=256):
    t = xb.shape[0]
    ng = 0 if gather is None else len(gather)
    steps = t // tm
    n8 = gate_shards.shape[2]

    def body(*refs):
        x_ref, gs_ref, us_ref = refs[:3]
        g_ref, u_ref, a_ref, wgu_ref = refs[3 + ng:7 + ng]
        w_ref, w_sem = refs[7 + 2 * ng:9 + 2 * ng]
        keep = pltpu.make_async_copy(w_ref, wgu_ref, w_sem)

        @pl.when(pl.program_id(0) == 0)
        def _():
            for s in range(N_DEV):
                w_ref[:, s * n8:(s + 1) * n8] = gs_ref[s]
                w_ref[:, D_FF + s * n8:D_FF + (s + 1) * n8] = us_ref[s]
            keep.start()

        if ng:
            start, finish = _gather_protocol(refs[3:3 + ng], refs[7 + ng:7 + 2 * ng], *refs[9 + 2 * ng:])
            pl.when(pl.program_id(0) == 0)(start)
        xv = x_ref[...]
        for c in range(FF_CHUNKS):
            cols = slice(c * D_FF // FF_CHUNKS, (c + 1) * D_FF // FF_CHUNKS)
            gt = _dot_nn(xv, w_ref[:, cols])
            up = _dot_nn(xv, w_ref[:, D_FF + cols.start:D_FF + cols.stop])
            sg = jax.nn.sigmoid(gt)
            silu = gt * sg
            g_ref[:, cols] = (up * (sg * (1.0 + gt * (1.0 - sg)))).astype(BF16)
            u_ref[:, cols] = silu.astype(BF16)
            a_ref[:, cols] = (silu * up).astype(BF16)
        if ng:
            pl.when(pl.program_id(0) == steps - 1)(finish)
        pl.when(pl.program_id(0) == steps - 1)(keep.wait)

    k = gate_shards.shape[1]
    outs = pl.pallas_call(
        body, name="ffn_up_fwd_gather" if ng else "ffn_up_fwd", grid=(steps,),
        in_specs=[_rows(tm, D_MODEL), _resident(gate_shards.shape), _resident(up_shards.shape)] + [_ANY] * ng,
        out_specs=[_rows(tm, D_FF)] * 3 + [_ANY] + [_ANY] * ng,
        out_shape=[_sds((t, D_FF), BF16)] * 3 + [_sds((k, 2 * D_FF), BF16)] + (_gather_shapes(gather) if ng else []),
        scratch_shapes=[pltpu.VMEM((k, 2 * D_FF), BF16), pltpu.SemaphoreType.DMA(())] + (_gather_sems(ng) if ng else []),
        compiler_params=_params(("arbitrary",)))(xb, gate_shards, up_shards, *(gather or []))
    return outs[0], outs[1], outs[2], outs[3], list(outs[4:])


def _ln_bwd_rows(dxn, xhat, rstd, g_ref, dr_ref, drb_ref, dg_ref, db_ref):
    @pl.when(pl.program_id(0) == 0)
    def _():
        dg_ref[...] = jnp.zeros_like(dg_ref)
        db_ref[...] = jnp.zeros_like(db_ref)

    dxh = dxn * g_ref[...]
    m1 = jnp.mean(dxh, axis=-1, keepdims=True)
    m2 = jnp.mean(dxh * xhat, axis=-1, keepdims=True)
    dr = rstd * (dxh - m1 - xhat * m2)
    dr_ref[...] = dr
    drb_ref[...] = dr.astype(BF16)
    dg_ref[...] += jnp.sum(dxn * xhat, axis=0, keepdims=True)
    db_ref[...] += jnp.sum(dxn, axis=0, keepdims=True)


def _ln_bwd_outs(t, tm):
    vec = _whole((1, D_MODEL))
    specs = [_rows(tm, D_MODEL), _rows(tm, D_MODEL), vec, vec]
    shapes = [_sds((t, D_MODEL), F32), _sds((t, D_MODEL), BF16), _sds((1, D_MODEL), F32), _sds((1, D_MODEL), F32)]
    return specs, shapes


def _loss_ln_bwd(r, g, b, tgt, tm=512):
    t = r.shape[0]

    def body(r_ref, g_ref, b_ref, t_ref, dr_ref, drb_ref, dg_ref, db_ref, l_ref):
        @pl.when(pl.program_id(0) == 0)
        def _():
            l_ref[...] = jnp.zeros_like(l_ref)

        xhat, rstd = _ln_hat(r_ref[...])
        e = xhat * g_ref[...] + b_ref[...] - t_ref[...]
        l_ref[...] += jnp.sum(e * e) * (0.5 / D_MODEL)
        _ln_bwd_rows(e * (1.0 / D_MODEL), xhat, rstd, g_ref, dr_ref, drb_ref, dg_ref, db_ref)

    vec = _whole((1, D_MODEL))
    specs, shapes = _ln_bwd_outs(t, tm)
    return pl.pallas_call(
        body, name="loss_ln_bwd", grid=(t // tm,), in_specs=[_rows(tm, D_MODEL), vec, vec, _rows(tm, D_MODEL)],
        out_specs=specs + [_whole((1, 128))], out_shape=shapes + [_sds((1, 128), F32)],
        compiler_params=_params(("arbitrary",)))(r, g, b, tgt)


def _ffn_down_bwd(drb, wd, gt, up, tm=512):
    t = drb.shape[0]

    def body(d_ref, w_ref, g_ref, u_ref, o_ref):
        dv = d_ref[...]
        for c in range(FF_CHUNKS):
            cols = slice(c * D_FF // FF_CHUNKS, (c + 1) * D_FF // FF_CHUNKS)
            da = _dot_nt(dv, w_ref[cols, :])
            o_ref[:, cols] = (da * g_ref[:, cols].astype(F32)).astype(BF16)
            o_ref[:, D_FF + cols.start:D_FF + cols.stop] = (da * u_ref[:, cols].astype(F32)).astype(BF16)

    return pl.pallas_call(
        body, name="ffn_down_bwd", grid=(t // tm,),
        in_specs=[_rows(tm, D_MODEL), _resident(wd.shape), _rows(tm, D_FF), _rows(tm, D_FF)],
        out_specs=_rows(tm, 2 * D_FF), out_shape=_sds((t, 2 * D_FF), BF16),
        compiler_params=_params(("parallel",)))(drb, wd, gt, up)


def _linear_nt(name, lhs, w, res, out_dtype, ln=None, tm=512):
    t = lhs[0].shape[0]
    n_lhs = len(lhs)
    n_out = w.shape[0]
    n_in = n_lhs + 1 + (res is not None) + (2 if ln else 0)

    def body(*refs):
        lhs_refs = refs[:n_lhs]
        w_ref = refs[n_lhs]
        y, off = None, 0
        for lr in lhs_refs:
            k = lr.shape[1]
            term = _dot_nt(lr[...], w_ref[:, off:off + k])
            y = term if y is None else y + term
            off += k
        if res is not None:
            y = ALPHA * refs[n_lhs + 1][...] + y
        if ln is None:
            refs[-1][...] = y.astype(out_dtype)
        else:
            r_ref, g_ref = refs[n_in - 2:n_in]
            xhat, rstd = _ln_hat(r_ref[...])
            _ln_bwd_rows(y, xhat, rstd, g_ref, *refs[n_in:])

    in_specs = [_rows(tm, a.shape[1]) for a in lhs] + [_resident(w.shape)]
    args = list(lhs) + [w]
    if res is not None:
        in_specs.append(_rows(tm, n_out))
        args.append(res)
    if ln is None:
        out_specs, out_shape, sem = _rows(tm, n_out), _sds((t, n_out), out_dtype), "parallel"
    else:
        in_specs += [_rows(tm, D_MODEL), _whole((1, D_MODEL))]
        args += list(ln)
        (out_specs, out_shape), sem = _ln_bwd_outs(t, tm), "arbitrary"
    return pl.pallas_call(
        body, name=name, grid=(t // tm,), in_specs=in_specs, out_specs=out_specs, out_shape=out_shape,
        compiler_params=_params((sem,)))(*args)


def _pick_tile(n, limit):
    if n <= limit:
        return n
    best = 128
    for cand in range(128, limit + 1, 128):
        if n % cand == 0:
            best = cand
    return best


def _mm_tn(name, a, b, cols=None, tt=2048):
    t, k = a.shape
    first_col, n = (0, b.shape[1]) if cols is None else (cols[0], cols[1] - cols[0])
    tt = min(tt, t)
    tk = _pick_tile(k, 1408)
    tn = _pick_tile(n, (6 * 2 ** 20) // (4 * tk) // 128 * 128)
    steps = t // tt
    first_block = first_col // tn
    assert first_block * tn == first_col, (first_col, tn)

    def body(a_ref, b_ref, o_ref, acc_ref):
        @pl.when(pl.program_id(2) == 0)
        def _():
            acc_ref[...] = jnp.zeros_like(acc_ref)

        acc_ref[...] += _dot_tn(a_ref[...], b_ref[...])

        @pl.when(pl.program_id(2) == steps - 1)
        def _():
            o_ref[...] = acc_ref[...].astype(BF16)

    return pl.pallas_call(
        body, name=name, grid=(k // tk, n // tn, steps),
        in_specs=[pl.BlockSpec((tt, tk), lambda i, j, s: (s, i)), pl.BlockSpec((tt, tn), lambda i, j, s: (s, first_block + j))],
        out_specs=pl.BlockSpec((tk, tn), lambda i, j, s: (i, j)), out_shape=_sds((k, n), BF16),
        scratch_shapes=[pltpu.VMEM((tk, tn), F32)],
        compiler_params=_params(("parallel", "parallel", "arbitrary")))(a, b)


def _mm_tn_parts(name, a, parts, split_rows=False, tt=1024):
    t, k = a.shape
    widths = [p.shape[1] for p in parts]
    n_parts = len(parts)
    steps = t // tt
    shape = (sum(widths), k) if split_rows else (k, sum(widths))

    def body(*refs):
        a_ref, b_refs, o_ref, acc_ref = refs[0], refs[1:1 + n_parts], refs[1 + n_parts], refs[2 + n_parts]

        @pl.when(pl.program_id(0) == 0)
        def _():
            acc_ref[...] = jnp.zeros_like(acc_ref)

        av, off = a_ref[...], 0
        for b_ref, width in zip(b_refs, widths):
            if split_rows:
                acc_ref[off:off + width, :] += _dot_tn(b_ref[...], av)
            else:
                acc_ref[:, off:off + width] += _dot_tn(av, b_ref[...])
            off += width

        @pl.when(pl.program_id(0) == steps - 1)
        def _():
            o_ref[...] = acc_ref[...].astype(BF16)

    return pl.pallas_call(
        body, name=name, grid=(steps,), in_specs=[_rows(tt, k)] + [_rows(tt, w) for w in widths],
        out_specs=_whole(shape), out_shape=_sds(shape, BF16), scratch_shapes=[pltpu.VMEM(shape, F32)],
        compiler_params=_params(("arbitrary",)))(a, *parts)


def _alibi_table():
    arr = np.zeros((N_GROUPS, 8, 128), np.float32)
    for g in range(N_GROUPS):
        for hh in range(HEAD_GROUP):
            arr[g, hh, :] = 2.0 ** (-8.0 * (g * HEAD_GROUP + hh + 1) / N_HEADS)
    return jnp.asarray(arr)


def _spread_stats(cols, per_head=STAT_LANES):
    lane = lax.broadcasted_iota(jnp.int32, (BLK, HEAD_GROUP * per_head), 1)
    tile = cols[HEAD_GROUP - 1]
    for hh in range(HEAD_GROUP - 2, -1, -1):
        tile = jnp.where(lane < (hh + 1) * per_head, cols[hh], tile)
    return tile


def _block_mask(has_prev, dil):
    if has_prev is None:
        steps = lax.broadcasted_iota(jnp.int32, (BLK, BLK), 0) - lax.broadcasted_iota(jnp.int32, (BLK, BLK), 1)
        return steps >= 0, (steps * dil).astype(F32)
    qi = lax.broadcasted_iota(jnp.int32, (BLK, 2 * BLK), 0)
    ki = lax.broadcasted_iota(jnp.int32, (BLK, 2 * BLK), 1)
    steps = qi + BLK - ki
    valid = (steps >= 0) & (steps <= BLK) & ((ki >= BLK) | has_prev)
    return valid, (steps * dil).astype(F32)


def _bias_scratch():
    return pltpu.VMEM((2, HEAD_GROUP, BLK, 2 * BLK), F32)


def _fill_bias(bias, sl_ref, dil):
    for p in range(2):
        valid, dist = _block_mask(p == 1, dil)
        for hh in range(HEAD_GROUP):
            bias[p, hh] = jnp.where(valid, -sl_ref[hh:hh + 1, 0:1] * dist, NEG)


def _rows_of(j):
    return pl.ds(pl.multiple_of(j * BLK, BLK), BLK)


def _lane_half(hf):
    return slice(hf * 128, (hf + 1) * 128)


def _split_pair(x):
    first = lax.broadcasted_iota(jnp.int32, (1, 2 * HEAD_DIM), 1) < HEAD_DIM
    zero = jnp.zeros_like(x)
    return jnp.where(first, x, zero), jnp.where(first, zero, x)


def _deinterleave(src, dst, seq, dil, dtype):
    length = seq // dil
    for r in range(dil):
        for c in range(length // BLK):
            rows = pl.ds(r + c * BLK * dil, BLK, stride=dil)
            out = slice(r * length + c * BLK, r * length + (c + 1) * BLK)
            if len(src.shape) == 2:
                dst[out, :] = src[rows, :].astype(dtype)
            else:
                for hf in range(2):
                    dst[out, _lane_half(hf)] = src.at[hf][rows, :].astype(dtype)


def _interleave(src, dst, seq, dil, accumulate):
    length = seq // dil
    for r in range(dil):
        for c in range(length // BLK):
            rows = pl.ds(r + c * BLK * dil, BLK, stride=dil)
            inp = slice(r * length + c * BLK, r * length + (c + 1) * BLK)
            if len(dst.shape) == 2:
                dst[rows, :] = dst[rows, :] + src[inp, :] if accumulate else src[inp, :]
            else:
                for hf in range(2):
                    val = src[inp, _lane_half(hf)]
                    half = dst.at[hf]
                    half[rows, :] = half[rows, :] + val if accumulate else val


def _split_halves(src, dst, seq):
    def step(i, carry):
        for hf in range(2):
            dst[hf, _rows_of(i), :] = src[_rows_of(i), _lane_half(hf)].astype(F32)
        return carry

    lax.fori_loop(0, seq // BLK, step, 0)


def _band_attn_fwd_fused(h, slopes, bsz, seq, gather=None):
    k_off, v_off = MIX_W // 256, 2 * MIX_W // 256
    nb = seq // BLK

    ng = 0 if gather is None else len(gather)

    def body(*refs):
        sl_ref, q_ref, k_ref, v_ref = refs[:4]
        mix_ref, lse_ref = refs[4 + ng:6 + ng]
        qf, kf, vf, qd, kd, vd, od, ld, o1, o2, o3, l1, l2, l3, bias = refs[6 + 2 * ng:21 + 2 * ng]
        if ng:
            start, finish = _gather_protocol(refs[4:4 + ng], refs[6 + ng:6 + 2 * ng], *refs[21 + 2 * ng:])
            pl.when((pl.program_id(0) == 0) & (pl.program_id(1) == 0))(start)

        def run(dil, qs, ks, vs, o_dst, l_dst):
            nblk = seq // dil // BLK
            _fill_bias(bias, sl_ref, dil)

            def block(j, carry):
                rows, prows = _rows_of(j), _rows_of(jnp.maximum(j - 1, 0))
                has_prev = ((j % nblk) != 0).astype(jnp.int32)

                def keys(ref, lanes):
                    return jnp.concatenate([ref[prows, lanes], ref[rows, lanes]], axis=0)

                lses = []
                for pr in range(HEAD_GROUP // 2):
                    lanes = _lane_half(pr)
                    q_ab = _split_pair(qs[rows, lanes] * SCALE)
                    k2 = keys(ks, lanes)
                    v_ab = _split_pair(keys(vs, lanes))
                    out = None
                    for ab in range(2):
                        hh = 2 * pr + ab
                        s = _dot_nt(q_ab[ab], k2) + bias[has_prev, hh]
                        m = jnp.max(s, axis=-1, keepdims=True)
                        p = jnp.exp(s - m)
                        l = jnp.sum(p, axis=-1, keepdims=True)
                        term = _dot_nn(p.astype(BF16), v_ab[ab]) / l
                        out = term if out is None else out + term
                        lses.append(m + jnp.log(l))
                    o_dst[rows, lanes] = out
                l_dst[rows, :] = _spread_stats(lses, HEAD_DIM)
                return carry

            lax.fori_loop(0, nb, block, 0, unroll=8)

        run(1, q_ref, k_ref, v_ref, o1, l1)
        _split_halves(q_ref, qf, seq)
        _split_halves(k_ref, kf, seq)
        _split_halves(v_ref, vf, seq)
        for dil, o_tok, l_tok in ((4, o2, l2), (16, o3, l3)):
            _deinterleave(qf, qd, seq, dil, BF16)
            _deinterleave(kf, kd, seq, dil, BF16)
            _deinterleave(vf, vd, seq, dil, BF16)
            run(dil, qd, kd, vd, od, ld)
            _interleave(od, o_tok, seq, dil, False)
            _interleave(ld, l_tok, seq, dil, False)

        def merge(i, carry):
            rows = _rows_of(i)

            def both(ref):
                return jnp.concatenate([ref[0, rows, :], ref[1, rows, :]], axis=1)

            ls = [l1[rows, :], both(l2), both(l3)]
            m = jnp.maximum(jnp.maximum(ls[0], ls[1]), ls[2])
            tot = m + jnp.log(jnp.exp(ls[0] - m) + jnp.exp(ls[1] - m) + jnp.exp(ls[2] - m))
            ws = [jnp.exp(x - tot) for x in ls]
            mix_ref[rows, :] = (ws[0] * o1[rows, :] + ws[1] * both(o2) + ws[2] * both(o3)).astype(BF16)
            lse_ref[rows, :] = _spread_stats([tot[:, hh * HEAD_DIM:hh * HEAD_DIM + 1] for hh in range(HEAD_GROUP)])
            return carry

        lax.fori_loop(0, nb, merge, 0, unroll=2)
        if ng:
            pl.when((pl.program_id(0) == bsz - 1) & (pl.program_id(1) == N_GROUPS - 1))(finish)

    def hspec(off):
        return pl.BlockSpec((seq, 256), lambda b, g: (b, off + g))

    big = lambda dt: pltpu.VMEM((seq, 256), dt)
    halves = lambda: pltpu.VMEM((2, seq, 128), F32)
    outs = pl.pallas_call(
        body, name="band_attn_fwd_gather" if ng else "band_attn_fwd", grid=(bsz, N_GROUPS),
        in_specs=[pl.BlockSpec((None, 8, 128), lambda b, g: (g, 0, 0)), hspec(0), hspec(k_off), hspec(v_off)] + [_ANY] * ng,
        out_specs=[pl.BlockSpec((seq, 256), lambda b, g: (b, g)), pl.BlockSpec((seq, 128), lambda b, g: (b, g))] + [_ANY] * ng,
        out_shape=[_sds((bsz * seq, MIX_W), BF16), _sds((bsz * seq, STAT_W), F32)] + (_gather_shapes(gather) if ng else []),
        scratch_shapes=[halves(), halves(), halves(), big(BF16), big(BF16), big(BF16), big(F32), big(F32),
                        big(F32), halves(), halves(), big(F32), halves(), halves(), _bias_scratch()]
        + (_gather_sems(ng) if ng else []),
        compiler_params=_params(("arbitrary", "arbitrary")))(slopes, h, h, h, *(gather or []))
    return outs[0], outs[1], list(outs[2:])


def _band_attn_bwd_fused(h, dcat, mix, lse, slopes, bsz, seq, exchange=None):
    k_off, v_off = MIX_W // 256, 2 * MIX_W // 256
    nb = seq // BLK

    ne = 0 if exchange is None else len(exchange)

    def body(*refs):
        sl_ref, q_ref, k_ref, v_ref, do_ref, o_ref, lse_ref = refs[:7]
        dq_ref, dk_ref, dv_ref = refs[7 + ne:10 + ne]
        qf, kf, vf, dof, ddt, qd, kd, vd, dod, lsd, ddd, gq, gk, gv, aq, ak, av, bias = refs[10 + 2 * ne:28 + 2 * ne]
        if ne:
            start, finish = _exchange_protocol(refs[7:7 + ne], refs[10 + ne:10 + 2 * ne], *refs[28 + 2 * ne:])
            pl.when((pl.program_id(0) == 0) & (pl.program_id(1) == 0))(start)

        same_head = (lax.broadcasted_iota(jnp.int32, (HEAD_GROUP * HEAD_DIM, HEAD_GROUP * STAT_LANES), 0) // HEAD_DIM
                     == lax.broadcasted_iota(jnp.int32, (HEAD_GROUP * HEAD_DIM, HEAD_GROUP * STAT_LANES), 1) // STAT_LANES)
        ones_map = jnp.where(same_head, 1.0, 0.0).astype(BF16)

        def delta(i, carry):
            rows = _rows_of(i)
            prod = do_ref[rows, :].astype(F32) * o_ref[rows, :].astype(F32)
            high = prod.astype(BF16)
            rest = (prod - high.astype(F32)).astype(BF16)
            ddt[rows, :] = _dot_nn(high, ones_map) + _dot_nn(rest, ones_map)
            return carry

        lax.fori_loop(0, nb, delta, 0, unroll=2)

        def zero(i, carry):
            rows = _rows_of(i)
            for ref in (gk, gv):
                ref[rows, :] = jnp.zeros((BLK, 256), F32)
            return carry

        def run(dil, qs, ks, vs, dos, lss, dds):
            nblk = seq // dil // BLK
            _fill_bias(bias, sl_ref, dil)
            if nblk > 1:
                lax.fori_loop(0, nb, zero, 0)

            def block(j, carry):
                rows, prows = _rows_of(j), _rows_of(jnp.maximum(j - 1, 0))
                has_prev = ((j % nblk) != 0).astype(jnp.int32)

                def keys(ref, lanes):
                    if nblk == 1:
                        return ref[rows, lanes]
                    return jnp.concatenate([ref[prows, lanes], ref[rows, lanes]], axis=0)

                for pr in range(HEAD_GROUP // 2):
                    lanes = _lane_half(pr)
                    q_ab = _split_pair(qs[rows, lanes] * SCALE)
                    do_ab = _split_pair(dos[rows, lanes])
                    k2, v2 = keys(ks, lanes), keys(vs, lanes)
                    k_ab = _split_pair(k2)
                    dq, dk2, dv2 = None, None, None
                    for ab in range(2):
                        hh = 2 * pr + ab
                        st = slice(hh * STAT_LANES, hh * STAT_LANES + 1)
                        s = _dot_nt(q_ab[ab], k2) + (bias[0, hh, :, BLK:] if nblk == 1 else bias[has_prev, hh])
                        p = jnp.exp(s - lss[rows, st])
                        dp = _dot_nt(do_ab[ab], v2)
                        ds = (p * (dp - dds[rows, st])).astype(BF16)
                        terms = (_dot_nn(ds, k_ab[ab]), _dot_tn(ds, q_ab[ab]), _dot_tn(p.astype(BF16), do_ab[ab]))
                        dq, dk2, dv2 = terms if dq is None else (dq + terms[0], dk2 + terms[1], dv2 + terms[2])
                    gq[rows, lanes] = dq * SCALE
                    if nblk == 1:
                        gk[rows, lanes] = dk2
                        gv[rows, lanes] = dv2
                    else:
                        gk[prows, lanes] += dk2[:BLK]
                        gv[prows, lanes] += dv2[:BLK]
                        gk[rows, lanes] += dk2[BLK:]
                        gv[rows, lanes] += dv2[BLK:]
                return carry

            lax.fori_loop(0, nb, block, 0, unroll=4)

        run(1, q_ref, k_ref, v_ref, do_ref, lse_ref, ddt)

        for src, dst in ((gq, aq), (gk, ak), (gv, av), (q_ref, qf), (k_ref, kf), (v_ref, vf), (do_ref, dof)):
            _split_halves(src, dst, seq)
        for dil in (4, 16):
            for src, dst in ((qf, qd), (kf, kd), (vf, vd), (dof, dod)):
                _deinterleave(src, dst, seq, dil, BF16)
            _deinterleave(lse_ref, lsd, seq, dil, F32)
            _deinterleave(ddt, ddd, seq, dil, F32)
            run(dil, qd, kd, vd, dod, lsd, ddd)
            for src, dst in ((gq, aq), (gk, ak), (gv, av)):
                _interleave(src, dst, seq, dil, True)

        def write(i, carry):
            rows = _rows_of(i)
            for src, dst in ((aq, dq_ref), (ak, dk_ref), (av, dv_ref)):
                for hf in range(2):
                    dst[rows, _lane_half(hf)] = src[hf, rows, :].astype(BF16)
            return carry

        lax.fori_loop(0, nb, write, 0)
        if ne:
            pl.when((pl.program_id(0) == bsz - 1) & (pl.program_id(1) == N_GROUPS - 1))(finish)

    def hspec(off):
        return pl.BlockSpec((seq, 256), lambda b, g: (b, off + g))

    io = pl.BlockSpec((seq, 256), lambda b, g: (b, g))
    big = lambda dt: pltpu.VMEM((seq, 256), dt)
    halves = lambda: pltpu.VMEM((2, seq, 128), F32)
    stat = lambda: pltpu.VMEM((seq, 128), F32)
    outs = pl.pallas_call(
        body, name="band_attn_bwd_exchange" if ne else "band_attn_bwd", grid=(bsz, N_GROUPS),
        in_specs=[pl.BlockSpec((None, 8, 128), lambda b, g: (g, 0, 0)), hspec(0), hspec(k_off), hspec(v_off), io, io,
                  pl.BlockSpec((seq, 128), lambda b, g: (b, g))] + [_ANY] * ne,
        out_specs=[io, io, io] + [_ANY] * ne,
        out_shape=[_sds((bsz * seq, MIX_W), BF16)] * 3 + [_sds(s.shape, s.dtype) for s in (exchange or [])],
        scratch_shapes=[halves(), halves(), halves(), halves(), stat(),
                        big(BF16), big(BF16), big(BF16), big(BF16), stat(), stat(),
                        big(F32), big(F32), big(F32), halves(), halves(), halves(), _bias_scratch()]
        + (_exchange_sems(ne) if ne else []),
        compiler_params=_params(("arbitrary", "arbitrary")))(slopes, h, h, h, dcat, mix, lse, *(exchange or []))
    return list(outs[:3]), list(outs[3:])


def _mem_attn_fwd(h, mkv, bsz, seq, q_col, tq=1024):
    nq = seq // tq

    def body(q_ref, kv_ref, o_ref):
        for pr in range(2):
            lanes = _lane_half(pr)
            q_ab = _split_pair(q_ref[:, lanes])
            k = kv_ref[:, lanes]
            v_ab = _split_pair(kv_ref[:, MEM_W + pr * 128:MEM_W + (pr + 1) * 128])
            out = None
            for ab in range(2):
                s = _dot_nt(q_ab[ab], k) * SCALE
                m = jnp.max(s, axis=-1, keepdims=True)
                p = jnp.exp(s - m)
                l = jnp.sum(p, axis=-1, keepdims=True)
                term = _dot_nn(p.astype(BF16), v_ab[ab]) / l
                out = term if out is None else out + term
            o_ref[:, lanes] = out.astype(BF16)

    return pl.pallas_call(
        body, name="mem_attn_fwd", grid=(bsz, nq),
        in_specs=[pl.BlockSpec((tq, MEM_W), lambda b, i: (b * nq + i, q_col)),
                  pl.BlockSpec((N_MEM, 2 * MEM_W), lambda b, i: (b, 0))],
        out_specs=pl.BlockSpec((tq, MEM_W), lambda b, i: (b * nq + i, 0)),
        out_shape=_sds((bsz * seq, MEM_W), BF16), compiler_params=_params(("parallel", "parallel")))(h, mkv)


def _mem_attn_bwd(h, mkv, dcat, bsz, seq, q_col, tq=2048):
    nq = seq // tq
    do_col = MIX_W // MEM_W

    def body(q_ref, kv_ref, do_ref, dq_ref, dkv_ref):
        @pl.when(pl.program_id(1) == 0)
        def _():
            dkv_ref[...] = jnp.zeros_like(dkv_ref)

        for pr in range(2):
            lanes = _lane_half(pr)
            vlanes = slice(MEM_W + pr * 128, MEM_W + (pr + 1) * 128)
            q_ab = _split_pair(q_ref[:, lanes])
            do_ab = _split_pair(do_ref[:, lanes])
            k, v = kv_ref[:, lanes], kv_ref[:, vlanes]
            k_ab = _split_pair(k)
            dq, dk, dv = None, None, None
            for ab in range(2):
                s = _dot_nt(q_ab[ab], k) * SCALE
                m = jnp.max(s, axis=-1, keepdims=True)
                e = jnp.exp(s - m)
                p = e / jnp.sum(e, axis=-1, keepdims=True)
                dp = _dot_nt(do_ab[ab], v)
                dd = jnp.sum(p * dp, axis=-1, keepdims=True)
                ds = (p * (dp - dd) * SCALE).astype(BF16)
                terms = (_dot_nn(ds, k_ab[ab]), _dot_tn(ds, q_ab[ab]), _dot_tn(p.astype(BF16), do_ab[ab]))
                dq, dk, dv = terms if dq is None else (dq + terms[0], dk + terms[1], dv + terms[2])
            dq_ref[:, lanes] = dq.astype(BF16)
            dkv_ref[:, lanes] += dk
            dkv_ref[:, vlanes] += dv

    return pl.pallas_call(
        body, name="mem_attn_bwd", grid=(bsz, nq),
        in_specs=[pl.BlockSpec((tq, MEM_W), lambda b, i: (b * nq + i, q_col)),
                  pl.BlockSpec((N_MEM, 2 * MEM_W), lambda b, i: (b, 0)),
                  pl.BlockSpec((tq, MEM_W), lambda b, i: (b * nq + i, do_col))],
        out_specs=[pl.BlockSpec((tq, MEM_W), lambda b, i: (b * nq + i, 0)),
                   pl.BlockSpec((N_MEM, 2 * MEM_W), lambda b, i: (b, 0))],
        out_shape=[_sds((bsz * seq, MEM_W), BF16), _sds((bsz * N_MEM, 2 * MEM_W), F32)],
        compiler_params=_params(("parallel", "arbitrary")))(h, mkv, dcat)


_GELU_C = math.sqrt(2.0 / math.pi)
_GELU_A = 0.044715


def _gelu(x):
    return 0.5 * x * (1.0 + jnp.tanh(_GELU_C * (x + _GELU_A * x * x * x)))


def _gelu_grad(x):
    th = jnp.tanh(_GELU_C * (x + _GELU_A * x * x * x))
    return 0.5 * (1.0 + th) + 0.5 * x * (1.0 - th * th) * (_GELU_C * (1.0 + 3.0 * _GELU_A * x * x))


def _tril_mask(lower):
    ri = lax.broadcasted_iota(jnp.int32, (BLK, BLK), 0)
    ci = lax.broadcasted_iota(jnp.int32, (BLK, BLK), 1)
    return (ri >= ci) if lower else (ci >= ri)


def _sgu_fwd(h, ws, bs_t, ln_g, ln_b, tm=512):
    t = h.shape[0]

    def body(u_ref, v_ref, ws_ref, bs_ref, g_ref, b_ref, o_ref):
        ug = _gelu(u_ref[...].astype(F32))
        vhat, _ = _ln_hat(_gelu(v_ref[...].astype(F32)))
        vn = (vhat * g_ref[...] + b_ref[...]).astype(BF16)
        mask = _tril_mask(True)
        first = lax.broadcasted_iota(jnp.int32, (1, 2 * HEAD_DIM), 1) < HEAD_DIM
        for pr in range(N_HEADS // 2):
            lanes = _lane_half(pr)
            w_ab = [jnp.where(mask, ws_ref[2 * pr + ab], 0).astype(BF16) for ab in range(2)]
            bias = jnp.where(first, bs_ref[:, 2 * pr:2 * pr + 1], bs_ref[:, 2 * pr + 1:2 * pr + 2])
            for c in range(tm // BLK):
                rs = slice(c * BLK, (c + 1) * BLK)
                v_ab = _split_pair(vn[rs, lanes])
                mixed = _dot_nn(w_ab[0], v_ab[0]) + _dot_nn(w_ab[1], v_ab[1]) + bias
                o_ref[rs, lanes] = (ug[rs, lanes] * mixed).astype(BF16)

    return pl.pallas_call(
        body, name="sgu_fwd", grid=(t // tm,),
        in_specs=[_rows(tm, MIX_W, 0), _rows(tm, MIX_W, 1), _whole(ws.shape), _whole(bs_t.shape), _whole(ln_g.shape), _whole(ln_b.shape)],
        out_specs=_rows(tm, MIX_W), out_shape=_sds((t, MIX_W), BF16),
        compiler_params=_params(("parallel",)))(h, h, ws, bs_t, ln_g, ln_b)


def _sgu_bwd(h, dcat, ws, ws_t, bs_t, ln_g, ln_b, tm=512):
    t = h.shape[0]

    def body(u_ref, v_ref, do_ref, ws_ref, wst_ref, bs_ref, g_ref, b_ref, dh_ref, dws_ref, dbs_ref, dg_ref, db_ref, dvn_ref):
        @pl.when(pl.program_id(0) == 0)
        def _():
            dws_ref[...] = jnp.zeros_like(dws_ref)
            dbs_ref[...] = jnp.zeros_like(dbs_ref)
            dg_ref[...] = jnp.zeros_like(dg_ref)
            db_ref[...] = jnp.zeros_like(db_ref)

        u = u_ref[...].astype(F32)
        v = v_ref[...].astype(F32)
        do = do_ref[...].astype(F32)
        ug = _gelu(u)
        vhat, rstd = _ln_hat(_gelu(v))
        vn = (vhat * g_ref[...] + b_ref[...]).astype(BF16)
        dmixed_f = do * ug
        dmixed = dmixed_f.astype(BF16)
        low, upp = _tril_mask(True), _tril_mask(False)
        first = lax.broadcasted_iota(jnp.int32, (1, 2 * HEAD_DIM), 1) < HEAD_DIM
        for pr in range(N_HEADS // 2):
            lanes = _lane_half(pr)
            w_ab = [jnp.where(low, ws_ref[2 * pr + ab], 0).astype(BF16) for ab in range(2)]
            wt_ab = [jnp.where(upp, wst_ref[2 * pr + ab], 0).astype(BF16) for ab in range(2)]
            bias = jnp.where(first, bs_ref[:, 2 * pr:2 * pr + 1], bs_ref[:, 2 * pr + 1:2 * pr + 2])
            dws_acc = [None, None]
            dbs_acc = [None, None]
            for c in range(tm // BLK):
                rs = slice(c * BLK, (c + 1) * BLK)
                vn_pair = vn[rs, lanes]
                v_ab = _split_pair(vn_pair)
                mixed = _dot_nn(w_ab[0], v_ab[0]) + _dot_nn(w_ab[1], v_ab[1]) + bias
                dh_ref[rs, lanes] = (do[rs, lanes] * mixed * _gelu_grad(u[rs, lanes])).astype(BF16)
                dm_ab = _split_pair(dmixed[rs, lanes])
                dmf_ab = _split_pair(dmixed_f[rs, lanes])
                for ab in range(2):
                    term = _dot_nt(dm_ab[ab], vn_pair)
                    dws_acc[ab] = term if dws_acc[ab] is None else dws_acc[ab] + term
                    rsum = jnp.sum(dmf_ab[ab], axis=-1, keepdims=True)
                    dbs_acc[ab] = rsum if dbs_acc[ab] is None else dbs_acc[ab] + rsum
                dvn_ref[rs, lanes] = _dot_nn(wt_ab[0], dm_ab[0]) + _dot_nn(wt_ab[1], dm_ab[1])
            for ab in range(2):
                g = 2 * pr + ab
                dws_ref[g] += jnp.where(low, dws_acc[ab], 0.0)
                dbs_ref[:, g:g + 1] += dbs_acc[ab]
        dvn = dvn_ref[...]
        dg_ref[...] += jnp.sum(dvn * vhat, axis=0, keepdims=True)
        db_ref[...] += jnp.sum(dvn, axis=0, keepdims=True)
        dxh = dvn * g_ref[...]
        m1 = jnp.mean(dxh, axis=-1, keepdims=True)
        m2 = jnp.mean(dxh * vhat, axis=-1, keepdims=True)
        dvg = rstd * (dxh - m1 - vhat * m2)
        dh_ref[:, MIX_W:] = (dvg * _gelu_grad(v)).astype(BF16)

    return pl.pallas_call(
        body, name="sgu_bwd", grid=(t // tm,),
        in_specs=[_rows(tm, MIX_W, 0), _rows(tm, MIX_W, 1), _rows(tm, MIX_W, 0), _whole(ws.shape), _whole(ws_t.shape),
                  _whole(bs_t.shape), _whole(ln_g.shape), _whole(ln_b.shape)],
        out_specs=[_rows(tm, 2 * MIX_W), _whole(ws.shape), _whole(bs_t.shape), _whole((1, MIX_W)), _whole((1, MIX_W))],
        out_shape=[_sds((t, 2 * MIX_W), BF16), _sds(ws.shape, F32), _sds(bs_t.shape, F32), _sds((1, MIX_W), F32), _sds((1, MIX_W), F32)],
        scratch_shapes=[pltpu.VMEM((tm, MIX_W), F32)],
        compiler_params=_params(("arbitrary",)))(h, h, dcat, ws, ws_t, bs_t, ln_g, ln_b)


def _row_tile(rows, cols, itemsize=4, limit=2 ** 21):
    best = rows
    for cand in (4096, 2048, 1024, 512, 256, 128, 64, 32, 16):
        if rows % cand == 0 and rows > cand:
            best = cand
            if cand * cols * itemsize <= limit:
                break
    return best


def _adamw(w, m, v, grad=None, parts=None, first_parts=None):
    rows, cols = w.shape
    rows0 = 0 if first_parts is None else first_parts.shape[1]
    tr = _row_tile(rows0 if rows0 else rows, cols)
    n0 = rows0 // tr

    def chip_sum(ref):
        acc = ref[0].astype(F32)
        for k in range(1, 4):
            acc = acc + ref[k].astype(F32)
        return acc

    def body(*refs):
        w_ref, m_ref, v_ref = refs[:3]
        go_ref, d_ref, nm_ref, nv_ref = refs[-4:]
        if parts is None:
            gv = refs[3][...]
        elif first_parts is None:
            gv = chip_sum(refs[3])
        else:
            gv = jnp.where(pl.program_id(0) < n0, chip_sum(refs[3]), chip_sum(refs[4]))
        nm = ADAM_B1 * m_ref[...] + (1.0 - ADAM_B1) * gv
        nv = ADAM_B2 * v_ref[...] + (1.0 - ADAM_B2) * (gv * gv)
        m_hat = nm / (1.0 - ADAM_B1 ** ADAM_STEP)
        v_hat = nv / (1.0 - ADAM_B2 ** ADAM_STEP)
        go_ref[...] = gv
        d_ref[...] = -ADAM_LR * (m_hat / (jnp.sqrt(v_hat) + ADAM_EPS) + ADAM_WD * w_ref[...])
        nm_ref[...] = nm
        nv_ref[...] = nv

    spec = _rows(tr, cols)
    if parts is None:
        g_specs, g_args = [spec], [grad]
    elif first_parts is None:
        g_specs, g_args = [pl.BlockSpec((4, tr, cols), lambda i: (0, i, 0))], [parts]
    else:
        g_specs = [pl.BlockSpec((4, tr, cols), lambda i: (0, jnp.minimum(i, n0 - 1), 0)),
                   pl.BlockSpec((4, tr, cols), lambda i: (0, jnp.maximum(i - n0, 0), 0))]
        g_args = [first_parts, parts]
    return pl.pallas_call(
        body, name="adamw" if parts is None else "adamw_sum_chips", grid=(rows // tr,), in_specs=[spec] * 3 + g_specs,
        out_specs=[spec] * 4, out_shape=[_sds(w.shape, F32)] * 4,
        compiler_params=_params(("parallel",)))(w, m, v, *g_args)


_ANY = pl.BlockSpec(memory_space=pl.ANY)
_MESH = pl.DeviceIdType.MESH


def _all_gather(name, blocks):
    nt = len(blocks)

    def body(*refs):
        start, finish = _gather_protocol(refs[:nt], refs[nt:2 * nt], *refs[2 * nt:])
        start()
        finish()

    return pl.pallas_call(
        body, name=name, out_shape=_gather_shapes(blocks), in_specs=[_ANY] * nt, out_specs=[_ANY] * nt,
        scratch_shapes=_gather_sems(nt))(*blocks)


def _gather_shapes(blocks):
    return [_sds((N_DEV,) + b.shape, b.dtype) for b in blocks]


def _gather_sems(nt):
    return [pltpu.SemaphoreType.DMA((nt, 7)), pltpu.SemaphoreType.DMA((nt, 7)), pltpu.SemaphoreType.DMA((nt,))]


def _gather_protocol(x_refs, out_refs, send_sems, recv_sems, local_sems):
    nt = len(x_refs)
    x, y, c = lax.axis_index("x"), lax.axis_index("y"), lax.axis_index("c")
    me, sibling = (x, y, c), (x, y, 1 - c)
    chips = [(1 - x, y), (x, 1 - y), (1 - x, 1 - y)]

    def slot(t, px, py, pc):
        return out_refs[t].at[4 * px + 2 * py + pc]

    def copy(t, k, blk, to, src=None):
        return pltpu.make_async_remote_copy(
            src_ref=slot(t, *blk) if src is None else src, dst_ref=slot(t, *blk),
            send_sem=send_sems.at[t, k], recv_sem=recv_sems.at[t, k], device_id=to, device_id_type=_MESH)

    def own_copies():
        mine = [pltpu.make_async_copy(x_refs[t], slot(t, *me), local_sems.at[t]) for t in range(nt)]
        first = []
        for t in range(nt):
            first.append(copy(t, 0, me, sibling, src=x_refs[t]))
            first += [copy(t, 1 + j, me, (*chip, c), src=x_refs[t]) for j, chip in enumerate(chips)]
        return mine, first

    def start():
        mine, first = own_copies()
        for cp in mine + first:
            cp.start()

    def finish():
        mine, first = own_copies()
        passed = []
        for j, chip in enumerate(chips):
            for t in range(nt):
                copy(t, 1 + j, (*chip, c), me).wait_recv()
                fwd = copy(t, 4 + j, (*chip, c), sibling)
                fwd.start()
                passed.append(fwd)
        for t in range(nt):
            copy(t, 0, sibling, me).wait_recv()
        for j, chip in enumerate(chips):
            for t in range(nt):
                copy(t, 4 + j, (*chip, 1 - c), me).wait_recv()
        for cp in first + passed:
            cp.wait_send()
        for cp in mine:
            cp.wait()

    return start, finish


def _swap_with_sibling(packed):
    nt = len(packed)

    def body(*refs):
        p_refs, got_refs = refs[:nt], refs[nt:2 * nt]
        send_sems, recv_sems = refs[2 * nt:]
        x, y, c = lax.axis_index("x"), lax.axis_index("y"), lax.axis_index("c")
        copies = [
            pltpu.make_async_remote_copy(
                src_ref=p_refs[t].at[1 - c], dst_ref=got_refs[t], send_sem=send_sems.at[t], recv_sem=recv_sems.at[t],
                device_id=(x, y, 1 - c), device_id_type=_MESH)
            for t in range(nt)]
        for cp in copies:
            cp.start()
        for cp in copies:
            cp.wait_recv()
        for cp in copies:
            cp.wait_send()

    return pl.pallas_call(
        body, name="grad_swap_sibling", out_shape=[_sds(p.shape[1:], p.dtype) for p in packed], in_specs=[_ANY] * nt,
        out_specs=[_ANY] * nt,
        scratch_shapes=[pltpu.SemaphoreType.DMA((nt,)), pltpu.SemaphoreType.DMA((nt,))])(*packed)


def _chip_sum(packed, got):
    _, nchip, rows, cols = packed.shape
    tr = _row_tile(rows, cols, 2)
    core = lax.axis_index("c").astype(jnp.int32).reshape(1)

    def body(c_ref, p_ref, g_ref, o_ref):
        o_ref[...] = (p_ref[...].astype(F32) + g_ref[...].astype(F32)).astype(o_ref.dtype)

    grid_spec = pltpu.PrefetchScalarGridSpec(
        num_scalar_prefetch=1, grid=(nchip, rows // tr),
        in_specs=[pl.BlockSpec((None, None, tr, cols), lambda k, i, c: (c[0], k, i, 0)),
                  pl.BlockSpec((None, tr, cols), lambda k, i, c: (k, i, 0))],
        out_specs=pl.BlockSpec((None, tr, cols), lambda k, i, c: (k, i, 0)))
    return pl.pallas_call(
        body, name="grad_chip_sum", grid_spec=grid_spec, out_shape=_sds(got.shape, got.dtype),
        compiler_params=_params(("parallel", "parallel")))(core, packed, got)


def _exchange_chips(chip_sums):
    nt = len(chip_sums)

    def body(*refs):
        start, finish = _exchange_protocol(refs[:nt], refs[nt:2 * nt], *refs[2 * nt:])
        start()
        finish()

    return pl.pallas_call(
        body, name="grad_exchange_chips", out_shape=[_sds(s.shape, s.dtype) for s in chip_sums], in_specs=[_ANY] * nt,
        out_specs=[_ANY] * nt, scratch_shapes=_exchange_sems(nt))(*chip_sums)


def _exchange_sems(nt):
    return [pltpu.SemaphoreType.DMA((nt, 3)), pltpu.SemaphoreType.DMA((nt, 3)), pltpu.SemaphoreType.DMA((nt,))]


def _exchange_protocol(s_refs, got_refs, send_sems, recv_sems, local_sems):
    nt = len(s_refs)
    x, y, c = lax.axis_index("x"), lax.axis_index("y"), lax.axis_index("c")
    my_chip = 2 * x + y
    chips = [(1 - x, y), (x, 1 - y), (1 - x, 1 - y)]

    def copy(t, j, src_chip, dst_chip):
        px, py = chips[j]
        return pltpu.make_async_remote_copy(
            src_ref=s_refs[t].at[src_chip], dst_ref=got_refs[t].at[dst_chip], send_sem=send_sems.at[t, j],
            recv_sem=recv_sems.at[t, j], device_id=(px, py, c), device_id_type=_MESH)

    def own_copies():
        mine = [pltpu.make_async_copy(s_refs[t].at[my_chip], got_refs[t].at[my_chip], local_sems.at[t]) for t in range(nt)]
        sends = [copy(t, j, 2 * px + py, my_chip) for t in range(nt) for j, (px, py) in enumerate(chips)]
        return mine, sends

    def start():
        mine, sends = own_copies()
        for cp in mine + sends:
            cp.start()

    def finish():
        mine, sends = own_copies()
        for j, (px, py) in enumerate(chips):
            for t in range(nt):
                copy(t, j, my_chip, 2 * px + py).wait_recv()
        for cp in sends:
            cp.wait_send()
        for cp in mine:
            cp.wait()

    return start, finish


def _sum_chips(got):
    _, rows, cols = got.shape
    tr = _row_tile(rows, cols)

    def body(g_ref, o_ref):
        acc = g_ref[0].astype(F32)
        for k in range(1, 4):
            acc = acc + g_ref[k].astype(F32)
        o_ref[...] = acc

    return pl.pallas_call(
        body, name="grad_sum_chips", grid=(rows // tr,), in_specs=[pl.BlockSpec((4, tr, cols), lambda i: (0, i, 0))],
        out_specs=pl.BlockSpec((tr, cols), lambda i: (i, 0)), out_shape=_sds((rows, cols), F32),
        compiler_params=_params(("parallel",)))(got)


_COL_SHARDED = ("a_w_in", "b_w_in", "w_gate", "w_up")
_ROW_SHARDED = ("w_mem_kv", "w_out", "w_down")
_BIG = ("a_w_in", "b_w_in", "w_mem_kv", "w_out", "w_gate", "w_up", "w_down")
_SGU_LN = ("sgu_ln_g", "sgu_ln_b")
_LN4 = ("ln_mix_g", "ln_mix_b", "ln_ffn_g", "ln_ffn_b")
_REPLICATED = ("sgu_w_s", "sgu_b_s") + _LN4


def _unshard(name, gathered):
    if name in _COL_SHARDED or name in _SGU_LN:
        moved = jnp.moveaxis(gathered, 0, -2)
        return moved.reshape(moved.shape[:-2] + (moved.shape[-2] * moved.shape[-1],))
    assert name in _ROW_SHARDED, name
    moved = jnp.moveaxis(gathered, 0, 1)
    return moved.reshape((moved.shape[0], moved.shape[1] * moved.shape[2]) + moved.shape[3:])


_LAID_OUT_IN_KERNEL = _COL_SHARDED


def _after_gather(name, gathered):
    return gathered if name in _LAID_OUT_IN_KERNEL else _unshard(name, gathered)


def _by_shard(name, full):
    if name in _COL_SHARDED or name in _SGU_LN:
        split = full.reshape(full.shape[:-1] + (N_DEV, full.shape[-1] // N_DEV))
        return jnp.moveaxis(split, -2, 0)
    split = full.reshape((full.shape[0], N_DEV, full.shape[1] // N_DEV) + full.shape[2:])
    return jnp.moveaxis(split, 1, 0)


def _layer_keys(i):
    return [("a_w_in" if i % 2 == 0 else "b_w_in", i // 2)] + [(n, i) for n in ("w_mem_kv", "w_out", "w_gate", "w_up", "w_down")]


_GATHER_FIRST = _layer_keys(0)[:2]
_GATHER_LATER = (_layer_keys(0)[2:] + _layer_keys(1), _layer_keys(2)[:4], _layer_keys(2)[4:],
                 _layer_keys(3)[:4], _layer_keys(3)[4:])


def _shard_block(shards, key):
    name, idx = key
    return shards[name][idx:idx + 1].astype(BF16)


def _gather_first(shards):
    blocks = [_shard_block(shards, k) for k in _GATHER_FIRST] + [shards[n] for n in _SGU_LN]
    gathered = _all_gather("first_all_gather", blocks)
    full = {k: _after_gather(k[0], g) for k, g in zip(_GATHER_FIRST, gathered)}
    sgu_ln = {n: _unshard(n, g) for n, g in zip(_SGU_LN, gathered[len(_GATHER_FIRST):])}
    return full, sgu_ln


def _two_level(by_dest):
    shp = by_dest.shape[1:]
    split = by_dest.astype(BF16).reshape((4, 2) + shp).swapaxes(0, 1)
    return split.reshape(2, 4, int(np.prod(shp[:-1])), shp[-1])


_EARLY = _BIG + _SGU_LN


def _chip_sums_of_early(grads):
    packed = [_two_level(_by_shard(n, jnp.stack(grads[n][1:] if n == "a_w_in" else grads[n]))) for n in _EARLY]
    ln4 = jnp.stack([jnp.stack(grads[n]) for n in _LN4])
    rep = [jnp.stack(grads["sgu_w_s"]).reshape(N_DEV, -1, BLK), jnp.stack(grads["sgu_b_s"]).reshape(N_DEV, -1, BLK),
           ln4.reshape(N_DEV, -1, D_MODEL)]
    packed += [_two_level(r) for r in rep]
    got = _swap_with_sibling(packed)
    return [_chip_sum(p, g) for p, g in zip(packed, got)]


def _finish_replicated(parts, shapes):
    w_s, b_s, ln_all = _all_gather("replicated_grads_all_gather", [_sum_chips(p) for p in parts])
    ln_all = ln_all.reshape((len(_LN4),) + tuple(shapes[_LN4[0]]))
    rep_grads = {"sgu_w_s": w_s.reshape(shapes["sgu_w_s"]), "sgu_b_s": b_s.reshape(shapes["sgu_b_s"])}
    rep_grads.update({n: ln_all[i] for i, n in enumerate(_LN4)})
    return rep_grads


def _reduce_last(grad_a_first):
    packed = [_two_level(_by_shard("a_w_in", grad_a_first))]
    got = _swap_with_sibling(packed)
    return _exchange_chips([_chip_sum(packed[0], got[0])])[0]


def _as_2d(a):
    if a.ndim == 1:
        return a.reshape(1, -1)
    return a.reshape(-1, a.shape[-1])


def kernel(x, mem, a_w_in, b_w_in, sgu_ln_g, sgu_ln_b, sgu_w_s, sgu_b_s, w_mem_kv, w_out, ln_mix_g, ln_mix_b, w_gate, w_up, w_down, ln_ffn_g, ln_ffn_b, loss_target, m_a_w_in, m_b_w_in, m_sgu_ln_g, m_sgu_ln_b, m_sgu_w_s, m_sgu_b_s, m_w_mem_kv, m_w_out, m_ln_mix_g, m_ln_mix_b, m_w_gate, m_w_up, m_w_down, m_ln_ffn_g, m_ln_ffn_b, v_a_w_in, v_b_w_in, v_sgu_ln_g, v_sgu_ln_b, v_sgu_w_s, v_sgu_b_s, v_w_mem_kv, v_w_out, v_ln_mix_g, v_ln_mix_b, v_w_gate, v_w_up, v_w_down, v_ln_ffn_g, v_ln_ffn_b):
    names = ("a_w_in", "b_w_in", "sgu_ln_g", "sgu_ln_b", "sgu_w_s", "sgu_b_s", "w_mem_kv", "w_out", "ln_mix_g", "ln_mix_b",
             "w_gate", "w_up", "w_down", "ln_ffn_g", "ln_ffn_b")
    weights = dict(zip(names, (a_w_in, b_w_in, sgu_ln_g, sgu_ln_b, sgu_w_s, sgu_b_s, w_mem_kv, w_out, ln_mix_g, ln_mix_b,
                               w_gate, w_up, w_down, ln_ffn_g, ln_ffn_b)))
    mom_m = dict(zip(names, (m_a_w_in, m_b_w_in, m_sgu_ln_g, m_sgu_ln_b, m_sgu_w_s, m_sgu_b_s, m_w_mem_kv, m_w_out, m_ln_mix_g,
                             m_ln_mix_b, m_w_gate, m_w_up, m_w_down, m_ln_ffn_g, m_ln_ffn_b)))
    mom_v = dict(zip(names, (v_a_w_in, v_b_w_in, v_sgu_ln_g, v_sgu_ln_b, v_sgu_w_s, v_sgu_b_s, v_w_mem_kv, v_w_out, v_ln_mix_g,
                             v_ln_mix_b, v_w_gate, v_w_up, v_w_down, v_ln_ffn_g, v_ln_ffn_b)))
    full, sgu_ln = _gather_first(weights)
    pending = [(keys, [_shard_block(weights, k) for k in keys]) for keys in _GATHER_LATER]
    loss_part, grad_x, local, early = _local_step(
        x, mem, loss_target, full, sgu_ln, {n: weights[n] for n in _REPLICATED}, pending)
    loss = lax.psum(loss_part[0, 0], ("x", "y", "c"))
    early_parts = dict(zip(_EARLY, early))
    rep_grads = _finish_replicated(early[len(_EARLY):], {n: weights[n].shape for n in _REPLICATED})
    a_first_parts = _reduce_last(local["a_w_in"][:1])

    reduced, deltas, new_m, new_v = {}, {}, {}, {}
    for n in names:
        w2, m2, v2 = _as_2d(weights[n]), _as_2d(mom_m[n]), _as_2d(mom_v[n])
        if n in early_parts:
            outs = _adamw(w2, m2, v2, parts=early_parts[n], first_parts=a_first_parts if n == "a_w_in" else None)
        else:
            outs = _adamw(w2, m2, v2, grad=_as_2d(rep_grads[n]))
        reduced[n], deltas[n], new_m[n], new_v[n] = (a.reshape(weights[n].shape) for a in outs)

    return (loss, grad_x, *[reduced[n] for n in names], *[deltas[n] for n in names],
            *[new_m[n] for n in names], *[new_v[n] for n in names])


def _local_step(x, mem, loss_target, full, sgu_ln, small, pending=None):
    sgu_w_s, sgu_b_s = small["sgu_w_s"], small["sgu_b_s"]
    ln_mix_g, ln_mix_b, ln_ffn_g, ln_ffn_b = (small[n] for n in ("ln_mix_g", "ln_mix_b", "ln_ffn_g", "ln_ffn_b"))
    bsz, seq, _ = x.shape
    tokens = bsz * seq
    slopes = _alibi_table()
    full = dict(full)
    exchanging = pending is not None
    pending = list(pending or [])

    def weight(name, idx):
        return full[(name, idx)][0]

    def next_group():
        return pending[0][1] if pending else None

    def landed(gathered):
        if gathered:
            keys, _ = pending.pop(0)
            full.update({k: _after_gather(k[0], g) for k, g in zip(keys, gathered)})

    res = (x.reshape(tokens, D_MODEL),)
    xb = res[0].astype(BF16)
    memb = mem.reshape(bsz * N_MEM, D_MODEL).astype(BF16)
    tgt = loss_target.reshape(tokens, D_MODEL)

    saved = []
    for i in range(DEPTH):
        j = i // 2
        dil_layer = i % 2 == 0
        mkv = _linear_nn("mem_kv", memb, weight("w_mem_kv", i))
        h, w_in = _linear_nn_gathered("in_proj_a" if dil_layer else "in_proj_b", xb,
                                      full[("a_w_in" if dil_layer else "b_w_in", j)][:, 0])
        st = dict(xb=xb, h=h, mkv=mkv, w_in=w_in)
        if dil_layer:
            mix, st["lse"], gathered = _band_attn_fwd_fused(h, slopes, bsz, seq, gather=next_group() if i == 0 else None)
            landed(gathered)
            q_col = 3 * MIX_W // MEM_W
        else:
            st["ws"] = sgu_w_s[j]
            st["bs_t"] = sgu_b_s[j].T
            st["ln_g"] = sgu_ln["sgu_ln_g"][j].reshape(1, MIX_W)
            st["ln_b"] = sgu_ln["sgu_ln_b"][j].reshape(1, MIX_W)
            mix = _sgu_fwd(h, st["ws"], st["bs_t"], st["ln_g"], st["ln_b"])
            q_col = 2 * MIX_W // MEM_W
        mo = _mem_attn_fwd(h, mkv, bsz, seq, q_col)
        w_out, w_down = weight("w_out", i), weight("w_down", i)
        mix_ln = (ln_mix_g[i].reshape(1, D_MODEL), ln_mix_b[i].reshape(1, D_MODEL))
        ffn_ln = (ln_ffn_g[i].reshape(1, D_MODEL), ln_ffn_b[i].reshape(1, D_MODEL))
        r1, x1b, _ = _proj_ln_fwd("out_proj_ln", [mix, mo], w_out, res, *mix_ln)
        gt, up, act, w_gu, gathered = _ffn_up_fwd(x1b, full[("w_gate", i)][:, 0], full[("w_up", i)][:, 0],
                                                  gather=next_group() if i < 2 else None)
        landed(gathered)
        r2, xb, gathered = _proj_ln_fwd("ffn_down_ln", [act], w_down, (r1, *mix_ln), *ffn_ln,
                                        gather=next_group() if i < 2 else None)
        landed(gathered)
        res = (r2, *ffn_ln)
        st.update(mix=mix, mo=mo, q_col=q_col, r1=r1, x1b=x1b, gt=gt, up=up, act=act, r2=r2,
                  w_out=w_out, w_down=w_down, w_gu=w_gu)
        saved.append(st)

    dr2, dr2b, dg, db, loss_part = _loss_ln_bwd(*res, tgt)

    early_parts = None
    per_pair = ("a_w_in", "b_w_in", "sgu_ln_g", "sgu_ln_b", "sgu_w_s", "sgu_b_s")
    grads = {n: [None] * (DEPTH // 2 if n in per_pair else DEPTH) for n in _BIG + _SGU_LN + _REPLICATED}
    for i in reversed(range(DEPTH)):
        j = i // 2
        st = saved[i]
        dil_layer = i % 2 == 0
        w_in = st["w_in"]
        grads["ln_ffn_g"][i], grads["ln_ffn_b"][i] = dg[0], db[0]
        dgu = _ffn_down_bwd(dr2b, st["w_down"], st["gt"], st["up"])
        grads["w_down"][i] = _mm_tn("grad_w_down", st["act"], dr2b)
        dr1, dr1b, dg, db = _linear_nt("ffn_up_bwd", [dgu], st["w_gu"], dr2, F32,
                                       ln=(st["r1"], ln_mix_g[i].reshape(1, D_MODEL)))
        grads["ln_mix_g"][i], grads["ln_mix_b"][i] = dg[0], db[0]
        grads["w_gate"][i] = _mm_tn("grad_w_gate_up", st["x1b"], dgu, cols=(0, D_FF))
        grads["w_up"][i] = _mm_tn("grad_w_gate_up", st["x1b"], dgu, cols=(D_FF, 2 * D_FF))
        dcat = _linear_nt("out_proj_bwd", [dr1b], st["w_out"], None, BF16)
        grads["w_out"][i] = _mm_tn_parts("grad_w_out", dr1b, [st["mix"], st["mo"]], split_rows=True)
        dqm, dmkv = _mem_attn_bwd(st["h"], st["mkv"], dcat, bsz, seq, st["q_col"])
        grads["w_mem_kv"][i] = _mm_tn("grad_w_mem_kv", memb, dmkv.astype(BF16))
        if dil_layer:
            early_sums = _chip_sums_of_early(grads) if (i == 0 and exchanging) else None
            dh_parts, exchanged = _band_attn_bwd_fused(st["h"], dcat, st["mix"], st["lse"], slopes, bsz, seq,
                                                       exchange=early_sums)
            if early_sums is not None:
                early_parts = exchanged
        else:
            ws_t = jnp.swapaxes(st["ws"], -1, -2)
            dh_main, dws, dbs_t, dlg, dlb = _sgu_bwd(st["h"], dcat, st["ws"], ws_t, st["bs_t"], st["ln_g"], st["ln_b"])
            grads["sgu_w_s"][j], grads["sgu_b_s"][j] = dws, dbs_t.T
            grads["sgu_ln_g"][j], grads["sgu_ln_b"][j] = dlg[0], dlb[0]
            dh_parts = [dh_main]
        name = "in_proj_bwd_a" if dil_layer else "in_proj_bwd_b"
        if i > 0:
            dr2, dr2b, dg, db = _linear_nt(name, [*dh_parts, dqm], w_in, dr1, F32,
                                           ln=(saved[i - 1]["r2"], ln_ffn_g[i - 1].reshape(1, D_MODEL)))
        else:
            grad_x = _linear_nt(name + "_x", [*dh_parts, dqm], w_in, dr1, F32).reshape(x.shape)
        grads["a_w_in" if dil_layer else "b_w_in"][j] = _mm_tn_parts(
            "grad_w_in_a" if dil_layer else "grad_w_in_b", st["xb"], [*dh_parts, dqm])
    return loss_part, grad_x, {n: jnp.stack(g) for n, g in grads.items()}, early_parts
```

```python
import math

import numpy as np
import jax
import jax.numpy as jnp
from jax import lax
from jax.experimental import pallas as pl
from jax.experimental.pallas import tpu as pltpu

F32 = jnp.float32
BF16 = jnp.bfloat16

D_MODEL = 1024
DEPTH = 4
N_MEM = 256
HEAD_DIM = 64
N_HEADS = 12
MIX_W = N_HEADS * HEAD_DIM
MEM_W = 4 * HEAD_DIM
BLK = 128
HEAD_GROUP = 4
N_GROUPS = N_HEADS // HEAD_GROUP
D_FF = 2816
FF_CHUNKS = 2
ALPHA = (2 * DEPTH) ** 0.25
LN_EPS = 1e-5
SCALE = HEAD_DIM ** -0.5
NEG = -1e30
N_DEV = 8

ADAM_LR, ADAM_B1, ADAM_B2, ADAM_EPS, ADAM_WD, ADAM_STEP = 0.001, 0.9, 0.999, 1e-08, 0.01, 10

VMEM_LIMIT = 56 * 2 ** 20
STAT_LANES = 32
STAT_W = N_HEADS * STAT_LANES


def _dot_nn(a, b):
    return lax.dot_general(a, b, (((1,), (0,)), ((), ())), preferred_element_type=F32)


def _dot_nt(a, b):
    return lax.dot_general(a, b, (((1,), (1,)), ((), ())), preferred_element_type=F32)


def _dot_tn(a, b):
    return lax.dot_general(a, b, (((0,), (0,)), ((), ())), preferred_element_type=F32)


def _ln_hat(r):
    mu = jnp.mean(r, axis=-1, keepdims=True)
    xc = r - mu
    var = jnp.mean(xc * xc, axis=-1, keepdims=True)
    rstd = lax.rsqrt(var + LN_EPS)
    return xc * rstd, rstd


def _params(sem):
    return pltpu.CompilerParams(dimension_semantics=sem, vmem_limit_bytes=VMEM_LIMIT)


def _rows(tm, c, col=0):
    return pl.BlockSpec((tm, c), lambda i: (i, col))


def _whole(shape):
    nd = len(shape)
    return pl.BlockSpec(tuple(shape), lambda *_: (0,) * nd)


def _resident(shape):
    nd = len(shape)
    return pl.BlockSpec(tuple(shape), lambda *_: (0,) * nd, pipeline_mode=pl.Buffered(1))


def _sds(shape, dtype):
    return jax.ShapeDtypeStruct(tuple(shape), dtype)


def _linear_nn(name, a, w, tm=512):
    t, k = a.shape
    n = w.shape[1]
    tm = min(tm, t)

    def body(a_ref, w_ref, o_ref):
        o_ref[...] = _dot_nn(a_ref[...], w_ref[...]).astype(BF16)

    return pl.pallas_call(
        body, name=name, grid=(t // tm,), in_specs=[_rows(tm, k), _resident(w.shape)], out_specs=_rows(tm, n),
        out_shape=_sds((t, n), BF16), compiler_params=_params(("parallel",)))(a, w)


def _linear_nn_gathered(name, a, shards, tm=512):
    t, k = a.shape
    n8 = shards.shape[2]
    n = N_DEV * n8

    def body(a_ref, s_ref, o_ref, w_ref):
        @pl.when(pl.program_id(0) == 0)
        def _():
            for s in range(N_DEV):
                w_ref[:, s * n8:(s + 1) * n8] = s_ref[s]

        o_ref[...] = _dot_nn(a_ref[...], w_ref[...]).astype(BF16)

    return pl.pallas_call(
        body, name=name, grid=(t // tm,), in_specs=[_rows(tm, k), _resident(shards.shape)],
        out_specs=[_rows(tm, n), _whole((k, n))], out_shape=[_sds((t, n), BF16), _sds((k, n), BF16)],
        compiler_params=_params(("arbitrary",)))(a, shards)


def _proj_ln_fwd(name, lhs, w, res, g, b, gather=None, tm=512):
    t = res[0].shape[0]
    n_lhs = len(lhs)
    n_res = len(res)
    ng = 0 if gather is None else len(gather)
    steps = t // tm
    n_in = n_lhs + 3 + n_res

    def body(*refs):
        lhs_refs = refs[:n_lhs]
        w_ref = refs[n_lhs]
        res_refs = refs[n_lhs + 1:n_lhs + 1 + n_res]
        g_ref, b_ref = refs[n_in - 2:n_in]
        r_ref, xnb_ref = refs[n_in + ng:n_in + ng + 2]
        if ng:
            start, finish = _gather_protocol(refs[n_in:n_in + ng], refs[n_in + ng + 2:n_in + 2 * ng + 2], *refs[n_in + 2 * ng + 2:])
            pl.when(pl.program_id(0) == 0)(start)
        y, off = None, 0
        for lr in lhs_refs:
            k = lr.shape[1]
            term = _dot_nn(lr[...], w_ref[off:off + k, :])
            y = term if y is None else y + term
            off += k
        x_res = res_refs[0][...]
        if n_res == 3:
            x_res = _ln_hat(x_res)[0] * res_refs[1][...] + res_refs[2][...]
        r = ALPHA * x_res + y
        r_ref[...] = r
        xnb_ref[...] = (_ln_hat(r)[0] * g_ref[...] + b_ref[...]).astype(BF16)
        if ng:
            pl.when(pl.program_id(0) == steps - 1)(finish)

    vec = _whole((1, D_MODEL))
    in_specs = ([_rows(tm, a.shape[1]) for a in lhs] + [_resident(w.shape), _rows(tm, D_MODEL)] + [vec] * (n_res - 1) + [vec, vec])
    outs = pl.pallas_call(
        body, name=name + "_gather" if ng else name, grid=(steps,), in_specs=in_specs + [_ANY] * ng,
        out_specs=[_rows(tm, D_MODEL)] * 2 + [_ANY] * ng,
        out_shape=[_sds((t, D_MODEL), F32), _sds((t, D_MODEL), BF16)] + (_gather_shapes(gather) if ng else []),
        scratch_shapes=_gather_sems(ng) if ng else [],
        compiler_params=_params(("arbitrary" if ng else "parallel",)))(*lhs, w, *res, g, b, *(gather or []))
    return outs[0], outs[1], list(outs[2:])


def _ffn_up_fwd(xb, gate_shards, up_shards, gather=None, tm=256):
    t = xb.shape[0]
    ng = 0 if gather is None else len(gather)
    steps = t // tm
    n8 = gate_shards.shape[2]

    def body(*refs):
        x_ref, gs_ref, us_ref = refs[:3]
        g_ref, u_ref, a_ref, wgu_ref = refs[3 + ng:7 + ng]
        w_ref, w_sem = refs[7 + 2 * ng:9 + 2 * ng]
        keep = pltpu.make_async_copy(w_ref, wgu_ref, w_sem)

        @pl.when(pl.program_id(0) == 0)
        def _():
            for s in range(N_DEV):
                w_ref[:, s * n8:(s + 1) * n8] = gs_ref[s]
                w_ref[:, D_FF + s * n8:D_FF + (s + 1) * n8] = us_ref[s]
            keep.start()

        if ng:
            start, finish = _gather_protocol(refs[3:3 + ng], refs[7 + ng:7 + 2 * ng], *refs[9 + 2 * ng:])
            pl.when(pl.program_id(0) == 0)(start)
        xv = x_ref[...]
        for c in range(FF_CHUNKS):
            cols = slice(c * D_FF // FF_CHUNKS, (c + 1) * D_FF // FF_CHUNKS)
            gt = _dot_nn(xv, w_ref[:, cols])
            up = _dot_nn(xv, w_ref[:, D_FF + cols.start:D_FF + cols.stop])
            sg = jax.nn.sigmoid(gt)
            silu = gt * sg
            g_ref[:, cols] = (up * (sg * (1.0 + gt * (1.0 - sg)))).astype(BF16)
            u_ref[:, cols] = silu.astype(BF16)
            a_ref[:, cols] = (silu * up).astype(BF16)
        if ng:
            pl.when(pl.program_id(0) == steps - 1)(finish)
        pl.when(pl.program_id(0) == steps - 1)(keep.wait)

    k = gate_shards.shape[1]
    outs = pl.pallas_call(
        body, name="ffn_up_fwd_gather" if ng else "ffn_up_fwd", grid=(steps,),
        in_specs=[_rows(tm, D_MODEL), _resident(gate_shards.shape), _resident(up_shards.shape)] + [_ANY] * ng,
        out_specs=[_rows(tm, D_FF)] * 3 + [_ANY] + [_ANY] * ng,
        out_shape=[_sds((t, D_FF), BF16)] * 3 + [_sds((k, 2 * D_FF), BF16)] + (_gather_shapes(gather) if ng else []),
        scratch_shapes=[pltpu.VMEM((k, 2 * D_FF), BF16), pltpu.SemaphoreType.DMA(())] + (_gather_sems(ng) if ng else []),
        compiler_params=_params(("arbitrary",)))(xb, gate_shards, up_shards, *(gather or []))
    return outs[0], outs[1], outs[2], outs[3], list(outs[4:])


def _ln_bwd_rows(dxn, xhat, rstd, g_ref, dr_ref, drb_ref, dg_ref, db_ref):
    @pl.when(pl.program_id(0) == 0)
    def _():
        dg_ref[...] = jnp.zeros_like(dg_ref)
        db_ref[...] = jnp.zeros_like(db_ref)

    dxh = dxn * g_ref[...]
    m1 = jnp.mean(dxh, axis=-1, keepdims=True)
    m2 = jnp.mean(dxh * xhat, axis=-1, keepdims=True)
    dr = rstd * (dxh - m1 - xhat * m2)
    dr_ref[...] = dr
    drb_ref[...] = dr.astype(BF16)
    dg_ref[...] += jnp.sum(dxn * xhat, axis=0, keepdims=True)
    db_ref[...] += jnp.sum(dxn, axis=0, keepdims=True)


def _ln_bwd_outs(t, tm):
    vec = _whole((1, D_MODEL))
    specs = [_rows(tm, D_MODEL), _rows(tm, D_MODEL), vec, vec]
    shapes = [_sds((t, D_MODEL), F32), _sds((t, D_MODEL), BF16), _sds((1, D_MODEL), F32), _sds((1, D_MODEL), F32)]
    return specs, shapes


def _loss_ln_bwd(r, g, b, tgt, tm=512):
    t = r.shape[0]

    def body(r_ref, g_ref, b_ref, t_ref, dr_ref, drb_ref, dg_ref, db_ref, l_ref):
        @pl.when(pl.program_id(0) == 0)
        def _():
            l_ref[...] = jnp.zeros_like(l_ref)

        xhat, rstd = _ln_hat(r_ref[...])
        e = xhat * g_ref[...] + b_ref[...] - t_ref[...]
        l_ref[...] += jnp.sum(e * e) * (0.5 / D_MODEL)
        _ln_bwd_rows(e * (1.0 / D_MODEL), xhat, rstd, g_ref, dr_ref, drb_ref, dg_ref, db_ref)

    vec = _whole((1, D_MODEL))
    specs, shapes = _ln_bwd_outs(t, tm)
    return pl.pallas_call(
        body, name="loss_ln_bwd", grid=(t // tm,), in_specs=[_rows(tm, D_MODEL), vec, vec, _rows(tm, D_MODEL)],
        out_specs=specs + [_whole((1, 128))], out_shape=shapes + [_sds((1, 128), F32)],
        compiler_params=_params(("arbitrary",)))(r, g, b, tgt)


def _ffn_down_bwd(drb, wd, gt, up, tm=512):
    t = drb.shape[0]

    def body(d_ref, w_ref, g_ref, u_ref, o_ref):
        dv = d_ref[...]
        for c in range(FF_CHUNKS):
            cols = slice(c * D_FF // FF_CHUNKS, (c + 1) * D_FF // FF_CHUNKS)
            da = _dot_nt(dv, w_ref[cols, :])
            o_ref[:, cols] = (da * g_ref[:, cols].astype(F32)).astype(BF16)
            o_ref[:, D_FF + cols.start:D_FF + cols.stop] = (da * u_ref[:, cols].astype(F32)).astype(BF16)

    return pl.pallas_call(
        body, name="ffn_down_bwd", grid=(t // tm,),
        in_specs=[_rows(tm, D_MODEL), _resident(wd.shape), _rows(tm, D_FF), _rows(tm, D_FF)],
        out_specs=_rows(tm, 2 * D_FF), out_shape=_sds((t, 2 * D_FF), BF16),
        compiler_params=_params(("parallel",)))(drb, wd, gt, up)


def _linear_nt(name, lhs, w, res, out_dtype, ln=None, tm=512):
    t = lhs[0].shape[0]
    n_lhs = len(lhs)
    n_out = w.shape[0]
    n_in = n_lhs + 1 + (res is not None) + (2 if ln else 0)

    def body(*refs):
        lhs_refs = refs[:n_lhs]
        w_ref = refs[n_lhs]
        y, off = None, 0
        for lr in lhs_refs:
            k = lr.shape[1]
            term = _dot_nt(lr[...], w_ref[:, off:off + k])
            y = term if y is None else y + term
            off += k
        if res is not None:
            y = ALPHA * refs[n_lhs + 1][...] + y
        if ln is None:
            refs[-1][...] = y.astype(out_dtype)
        else:
            r_ref, g_ref = refs[n_in - 2:n_in]
            xhat, rstd = _ln_hat(r_ref[...])
            _ln_bwd_rows(y, xhat, rstd, g_ref, *refs[n_in:])

    in_specs = [_rows(tm, a.shape[1]) for a in lhs] + [_resident(w.shape)]
    args = list(lhs) + [w]
    if res is not None:
        in_specs.append(_rows(tm, n_out))
        args.append(res)
    if ln is None:
        out_specs, out_shape, sem = _rows(tm, n_out), _sds((t, n_out), out_dtype), "parallel"
    else:
        in_specs += [_rows(tm, D_MODEL), _whole((1, D_MODEL))]
        args += list(ln)
        (out_specs, out_shape), sem = _ln_bwd_outs(t, tm), "arbitrary"
    return pl.pallas_call(
        body, name=name, grid=(t // tm,), in_specs=in_specs, out_specs=out_specs, out_shape=out_shape,
        compiler_params=_params((sem,)))(*args)


def _pick_tile(n, limit):
    if n <= limit:
        return n
    best = 128
    for cand in range(128, limit + 1, 128):
        if n % cand == 0:
            best = cand
    return best


def _mm_tn(name, a, b, cols=None, tt=2048):
    t, k = a.shape
    first_col, n = (0, b.shape[1]) if cols is None else (cols[0], cols[1] - cols[0])
    tt = min(tt, t)
    tk = _pick_tile(k, 1408)
    tn = _pick_tile(n, (6 * 2 ** 20) // (4 * tk) // 128 * 128)
    steps = t // tt
    first_block = first_col // tn
    assert first_block * tn == first_col, (first_col, tn)

    def body(a_ref, b_ref, o_ref, acc_ref):
        @pl.when(pl.program_id(2) == 0)
        def _():
            acc_ref[...] = jnp.zeros_like(acc_ref)

        acc_ref[...] += _dot_tn(a_ref[...], b_ref[...])

        @pl.when(pl.program_id(2) == steps - 1)
        def _():
            o_ref[...] = acc_ref[...].astype(BF16)

    return pl.pallas_call(
        body, name=name, grid=(k // tk, n // tn, steps),
        in_specs=[pl.BlockSpec((tt, tk), lambda i, j, s: (s, i)), pl.BlockSpec((tt, tn), lambda i, j, s: (s, first_block + j))],
        out_specs=pl.BlockSpec((tk, tn), lambda i, j, s: (i, j)), out_shape=_sds((k, n), BF16),
        scratch_shapes=[pltpu.VMEM((tk, tn), F32)],
        compiler_params=_params(("parallel", "parallel", "arbitrary")))(a, b)


def _mm_tn_parts(name, a, parts, split_rows=False, tt=1024):
    t, k = a.shape
    widths = [p.shape[1] for p in parts]
    n_parts = len(parts)
    steps = t // tt
    shape = (sum(widths), k) if split_rows else (k, sum(widths))

    def body(*refs):
        a_ref, b_refs, o_ref, acc_ref = refs[0], refs[1:1 + n_parts], refs[1 + n_parts], refs[2 + n_parts]

        @pl.when(pl.program_id(0) == 0)
        def _():
            acc_ref[...] = jnp.zeros_like(acc_ref)

        av, off = a_ref[...], 0
        for b_ref, width in zip(b_refs, widths):
            if split_rows:
                acc_ref[off:off + width, :] += _dot_tn(b_ref[...], av)
            else:
                acc_ref[:, off:off + width] += _dot_tn(av, b_ref[...])
            off += width

        @pl.when(pl.program_id(0) == steps - 1)
        def _():
            o_ref[...] = acc_ref[...].astype(BF16)

    return pl.pallas_call(
        body, name=name, grid=(steps,), in_specs=[_rows(tt, k)] + [_rows(tt, w) for w in widths],
        out_specs=_whole(shape), out_shape=_sds(shape, BF16), scratch_shapes=[pltpu.VMEM(shape, F32)],
        compiler_params=_params(("arbitrary",)))(a, *parts)


def _alibi_table():
    arr = np.zeros((N_GROUPS, 8, 128), np.float32)
    for g in range(N_GROUPS):
        for hh in range(HEAD_GROUP):
            arr[g, hh, :] = 2.0 ** (-8.0 * (g * HEAD_GROUP + hh + 1) / N_HEADS)
    return jnp.asarray(arr)


def _spread_stats(cols, per_head=STAT_LANES):
    lane = lax.broadcasted_iota(jnp.int32, (BLK, HEAD_GROUP * per_head), 1)
    tile = cols[HEAD_GROUP - 1]
    for hh in range(HEAD_GROUP - 2, -1, -1):
        tile = jnp.where(lane < (hh + 1) * per_head, cols[hh], tile)
    return tile


def _block_mask(has_prev, dil):
    if has_prev is None:
        steps = lax.broadcasted_iota(jnp.int32, (BLK, BLK), 0) - lax.broadcasted_iota(jnp.int32, (BLK, BLK), 1)
        return steps >= 0, (steps * dil).astype(F32)
    qi = lax.broadcasted_iota(jnp.int32, (BLK, 2 * BLK), 0)
    ki = lax.broadcasted_iota(jnp.int32, (BLK, 2 * BLK), 1)
    steps = qi + BLK - ki
    valid = (steps >= 0) & (steps <= BLK) & ((ki >= BLK) | has_prev)
    return valid, (steps * dil).astype(F32)


def _bias_scratch():
    return pltpu.VMEM((2, HEAD_GROUP, BLK, 2 * BLK), F32)


def _fill_bias(bias, sl_ref, dil):
    for p in range(2):
        valid, dist = _block_mask(p == 1, dil)
        for hh in range(HEAD_GROUP):
            bias[p, hh] = jnp.where(valid, -sl_ref[hh:hh + 1, 0:1] * dist, NEG)


def _rows_of(j):
    return pl.ds(pl.multiple_of(j * BLK, BLK), BLK)


def _lane_half(hf):
    return slice(hf * 128, (hf + 1) * 128)


def _split_pair(x):
    first = lax.broadcasted_iota(jnp.int32, (1, 2 * HEAD_DIM), 1) < HEAD_DIM
    zero = jnp.zeros_like(x)
    return jnp.where(first, x, zero), jnp.where(first, zero, x)


def _deinterleave(src, dst, seq, dil, dtype):
    length = seq // dil
    for r in range(dil):
        for c in range(length // BLK):
            rows = pl.ds(r + c * BLK * dil, BLK, stride=dil)
            out = slice(r * length + c * BLK, r * length + (c + 1) * BLK)
            if len(src.shape) == 2:
                dst[out, :] = src[rows, :].astype(dtype)
            else:
                for hf in range(2):
                    dst[out, _lane_half(hf)] = src.at[hf][rows, :].astype(dtype)


def _interleave(src, dst, seq, dil, accumulate):
    length = seq // dil
    for r in range(dil):
        for c in range(length // BLK):
            rows = pl.ds(r + c * BLK * dil, BLK, stride=dil)
            inp = slice(r * length + c * BLK, r * length + (c + 1) * BLK)
            if len(dst.shape) == 2:
                dst[rows, :] = dst[rows, :] + src[inp, :] if accumulate else src[inp, :]
            else:
                for hf in range(2):
                    val = src[inp, _lane_half(hf)]
                    half = dst.at[hf]
                    half[rows, :] = half[rows, :] + val if accumulate else val


def _split_halves(src, dst, seq):
    def step(i, carry):
        for hf in range(2):
            dst[hf, _rows_of(i), :] = src[_rows_of(i), _lane_half(hf)].astype(F32)
        return carry

    lax.fori_loop(0, seq // BLK, step, 0, unroll=4)


def _band_attn_fwd_fused(h, slopes, bsz, seq, gather=None):
    k_off, v_off = MIX_W // 256, 2 * MIX_W // 256
    nb = seq // BLK

    ng = 0 if gather is None else len(gather)

    def body(*refs):
        sl_ref, q_ref, k_ref, v_ref = refs[:4]
        mix_ref, lse_ref = refs[4 + ng:6 + ng]
        qf, kf, vf, qd, kd, vd, od, ld, o1, o2, o3, l1, l2, l3, bias = refs[6 + 2 * ng:21 + 2 * ng]
        if ng:
            start, finish = _gather_protocol(refs[4:4 + ng], refs[6 + ng:6 + 2 * ng], *refs[21 + 2 * ng:])
            pl.when((pl.program_id(0) == 0) & (pl.program_id(1) == 0))(start)

        def run(dil, qs, ks, vs, o_dst, l_dst):
            nblk = seq // dil // BLK
            _fill_bias(bias, sl_ref, dil)

            def block(j, carry):
                rows, prows = _rows_of(j), _rows_of(jnp.maximum(j - 1, 0))
                has_prev = ((j % nblk) != 0).astype(jnp.int32)

                def keys(ref, lanes):
                    return jnp.concatenate([ref[prows, lanes], ref[rows, lanes]], axis=0)

                lses = []
                for pr in range(HEAD_GROUP // 2):
                    lanes = _lane_half(pr)
                    q_ab = _split_pair(qs[rows, lanes] * SCALE)
                    k2 = keys(ks, lanes)
                    v_ab = _split_pair(keys(vs, lanes))
                    out = None
                    for ab in range(2):
                        hh = 2 * pr + ab
                        s = _dot_nt(q_ab[ab], k2) + bias[has_prev, hh]
                        m = jnp.max(s, axis=-1, keepdims=True)
                        p = jnp.exp(s - m)
                        l = jnp.sum(p, axis=-1, keepdims=True)
                        term = _dot_nn(p.astype(BF16), v_ab[ab]) / l
                        out = term if out is None else out + term
                        lses.append(m + jnp.log(l))
                    o_dst[rows, lanes] = out
                l_dst[rows, :] = _spread_stats(lses, HEAD_DIM)
                return carry

            lax.fori_loop(0, nb, block, 0, unroll=8)

        run(1, q_ref, k_ref, v_ref, o1, l1)
        _split_halves(q_ref, qf, seq)
        _split_halves(k_ref, kf, seq)
        _split_halves(v_ref, vf, seq)
        for dil, o_tok, l_tok in ((4, o2, l2), (16, o3, l3)):
            _deinterleave(qf, qd, seq, dil, BF16)
            _deinterleave(kf, kd, seq, dil, BF16)
            _deinterleave(vf, vd, seq, dil, BF16)
            run(dil, qd, kd, vd, od, ld)
            _interleave(od, o_tok, seq, dil, False)
            _interleave(ld, l_tok, seq, dil, False)

        def merge(i, carry):
            rows = _rows_of(i)

            def both(ref):
                return jnp.concatenate([ref[0, rows, :], ref[1, rows, :]], axis=1)

            ls = [l1[rows, :], both(l2), both(l3)]
            m = jnp.maximum(jnp.maximum(ls[0], ls[1]), ls[2])
            tot = m + jnp.log(jnp.exp(ls[0] - m) + jnp.exp(ls[1] - m) + jnp.exp(ls[2] - m))
            ws = [jnp.exp(x - tot) for x in ls]
            mix_ref[rows, :] = (ws[0] * o1[rows, :] + ws[1] * both(o2) + ws[2] * both(o3)).astype(BF16)
            lse_ref[rows, :] = _spread_stats([tot[:, hh * HEAD_DIM:hh * HEAD_DIM + 1] for hh in range(HEAD_GROUP)])
            return carry

        lax.fori_loop(0, nb, merge, 0, unroll=2)
        if ng:
            pl.when((pl.program_id(0) == bsz - 1) & (pl.program_id(1) == N_GROUPS - 1))(finish)

    def hspec(off):
        return pl.BlockSpec((seq, 256), lambda b, g: (b, off + g))

    big = lambda dt: pltpu.VMEM((seq, 256), dt)
    halves = lambda: pltpu.VMEM((2, seq, 128), F32)
    outs = pl.pallas_call(
        body, name="band_attn_fwd_gather" if ng else "band_attn_fwd", grid=(bsz, N_GROUPS),
        in_specs=[pl.BlockSpec((None, 8, 128), lambda b, g: (g, 0, 0)), hspec(0), hspec(k_off), hspec(v_off)] + [_ANY] * ng,
        out_specs=[pl.BlockSpec((seq, 256), lambda b, g: (b, g)), pl.BlockSpec((seq, 128), lambda b, g: (b, g))] + [_ANY] * ng,
        out_shape=[_sds((bsz * seq, MIX_W), BF16), _sds((bsz * seq, STAT_W), F32)] + (_gather_shapes(gather) if ng else []),
        scratch_shapes=[halves(), halves(), halves(), big(BF16), big(BF16), big(BF16), big(F32), big(F32),
                        big(F32), halves(), halves(), big(F32), halves(), halves(), _bias_scratch()]
        + (_gather_sems(ng) if ng else []),
        compiler_params=_params(("arbitrary", "arbitrary")))(slopes, h, h, h, *(gather or []))
    return outs[0], outs[1], list(outs[2:])


def _band_attn_bwd_fused(h, dcat, mix, lse, slopes, bsz, seq, exchange=None):
    k_off, v_off = MIX_W // 256, 2 * MIX_W // 256
    nb = seq // BLK

    ne = 0 if exchange is None else len(exchange)

    def body(*refs):
        sl_ref, q_ref, k_ref, v_ref, do_ref, o_ref, lse_ref = refs[:7]
        dq_ref, dk_ref, dv_ref = refs[7 + ne:10 + ne]
        qf, kf, vf, dof, ddt, qd, kd, vd, dod, lsd, ddd, gq, gk, gv, aq, ak, av, bias = refs[10 + 2 * ne:28 + 2 * ne]
        if ne:
            start, finish = _exchange_protocol(refs[7:7 + ne], refs[10 + ne:10 + 2 * ne], *refs[28 + 2 * ne:])
            pl.when((pl.program_id(0) == 0) & (pl.program_id(1) == 0))(start)

        same_head = (lax.broadcasted_iota(jnp.int32, (HEAD_GROUP * HEAD_DIM, HEAD_GROUP * STAT_LANES), 0) // HEAD_DIM
                     == lax.broadcasted_iota(jnp.int32, (HEAD_GROUP * HEAD_DIM, HEAD_GROUP * STAT_LANES), 1) // STAT_LANES)
        ones_map = jnp.where(same_head, 1.0, 0.0).astype(BF16)

        def delta(i, carry):
            rows = _rows_of(i)
            prod = do_ref[rows, :].astype(F32) * o_ref[rows, :].astype(F32)
            high = prod.astype(BF16)
            rest = (prod - high.astype(F32)).astype(BF16)
            ddt[rows, :] = _dot_nn(high, ones_map) + _dot_nn(rest, ones_map)
            return carry

        lax.fori_loop(0, nb, delta, 0, unroll=2)

        def zero(i, carry):
            rows = _rows_of(i)
            for ref in (gk, gv):
                ref[rows, :] = jnp.zeros((BLK, 256), F32)
            return carry

        def run(dil, qs, ks, vs, dos, lss, dds):
            nblk = seq // dil // BLK
            _fill_bias(bias, sl_ref, dil)
            if nblk > 1:
                lax.fori_loop(0, nb, zero, 0)

            def block(j, carry):
                rows, prows = _rows_of(j), _rows_of(jnp.maximum(j - 1, 0))
                has_prev = ((j % nblk) != 0).astype(jnp.int32)

                def keys(ref, lanes):
                    if nblk == 1:
                        return ref[rows, lanes]
                    return jnp.concatenate([ref[prows, lanes], ref[rows, lanes]], axis=0)

                for pr in range(HEAD_GROUP // 2):
                    lanes = _lane_half(pr)
                    q_ab = _split_pair(qs[rows, lanes] * SCALE)
                    do_ab = _split_pair(dos[rows, lanes])
                    k2, v2 = keys(ks, lanes), keys(vs, lanes)
                    k_ab = _split_pair(k2)
                    dq, dk2, dv2 = None, None, None
                    for ab in range(2):
                        hh = 2 * pr + ab
                        st = slice(hh * STAT_LANES, hh * STAT_LANES + 1)
                        s = _dot_nt(q_ab[ab], k2) + (bias[0, hh, :, BLK:] if nblk == 1 else bias[has_prev, hh])
                        p = jnp.exp(s - lss[rows, st])
                        dp = _dot_nt(do_ab[ab], v2)
                        ds = (p * (dp - dds[rows, st])).astype(BF16)
                        terms = (_dot_nn(ds, k_ab[ab]), _dot_tn(ds, q_ab[ab]), _dot_tn(p.astype(BF16), do_ab[ab]))
                        dq, dk2, dv2 = terms if dq is None else (dq + terms[0], dk2 + terms[1], dv2 + terms[2])
                    gq[rows, lanes] = dq * SCALE
                    if nblk == 1:
                        gk[rows, lanes] = dk2
                        gv[rows, lanes] = dv2
                    else:
                        gk[prows, lanes] += dk2[:BLK]
                        gv[prows, lanes] += dv2[:BLK]
                        gk[rows, lanes] += dk2[BLK:]
                        gv[rows, lanes] += dv2[BLK:]
                return carry

            lax.fori_loop(0, nb, block, 0, unroll=4)

        run(1, q_ref, k_ref, v_ref, do_ref, lse_ref, ddt)

        for src, dst in ((gq, aq), (gk, ak), (gv, av), (q_ref, qf), (k_ref, kf), (v_ref, vf), (do_ref, dof)):
            _split_halves(src, dst, seq)
        for dil in (4, 16):
            for src, dst in ((qf, qd), (kf, kd), (vf, vd), (dof, dod)):
                _deinterleave(src, dst, seq, dil, BF16)
            _deinterleave(lse_ref, lsd, seq, dil, F32)
            _deinterleave(ddt, ddd, seq, dil, F32)
            run(dil, qd, kd, vd, dod, lsd, ddd)
            for src, dst in ((gq, aq), (gk, ak), (gv, av)):
                _interleave(src, dst, seq, dil, True)

        def write(i, carry):
            rows = _rows_of(i)
            for src, dst in ((aq, dq_ref), (ak, dk_ref), (av, dv_ref)):
                for hf in range(2):
                    dst[rows, _lane_half(hf)] = src[hf, rows, :].astype(BF16)
            return carry

        lax.fori_loop(0, nb, write, 0, unroll=4)
        if ne:
            pl.when((pl.program_id(0) == bsz - 1) & (pl.program_id(1) == N_GROUPS - 1))(finish)

    def hspec(off):
        return pl.BlockSpec((seq, 256), lambda b, g: (b, off + g))

    io = pl.BlockSpec((seq, 256), lambda b, g: (b, g))
    big = lambda dt: pltpu.VMEM((seq, 256), dt)
    halves = lambda: pltpu.VMEM((2, seq, 128), F32)
    stat = lambda: pltpu.VMEM((seq, 128), F32)
    outs = pl.pallas_call(
        body, name="band_attn_bwd_exchange" if ne else "band_attn_bwd", grid=(bsz, N_GROUPS),
        in_specs=[pl.BlockSpec((None, 8, 128), lambda b, g: (g, 0, 0)), hspec(0), hspec(k_off), hspec(v_off), io, io,
                  pl.BlockSpec((seq, 128), lambda b, g: (b, g))] + [_ANY] * ne,
        out_specs=[io, io, io] + [_ANY] * ne,
        out_shape=[_sds((bsz * seq, MIX_W), BF16)] * 3 + [_sds(s.shape, s.dtype) for s in (exchange or [])],
        scratch_shapes=[halves(), halves(), halves(), halves(), stat(),
                        big(BF16), big(BF16), big(BF16), big(BF16), stat(), stat(),
                        big(F32), big(F32), big(F32), halves(), halves(), halves(), _bias_scratch()]
        + (_exchange_sems(ne) if ne else []),
        compiler_params=_params(("arbitrary", "arbitrary")))(slopes, h, h, h, dcat, mix, lse, *(exchange or []))
    return list(outs[:3]), list(outs[3:])


def _mem_attn_fwd(h, mkv, bsz, seq, q_col, tq=1024):
    nq = seq // tq

    def body(q_ref, kv_ref, o_ref):
        for pr in range(2):
            lanes = _lane_half(pr)
            q_ab = _split_pair(q_ref[:, lanes])
            k = kv_ref[:, lanes]
            v_ab = _split_pair(kv_ref[:, MEM_W + pr * 128:MEM_W + (pr + 1) * 128])
            out = None
            for ab in range(2):
                s = _dot_nt(q_ab[ab], k) * SCALE
                m = jnp.max(s, axis=-1, keepdims=True)
                p = jnp.exp(s - m)
                l = jnp.sum(p, axis=-1, keepdims=True)
                term = _dot_nn(p.astype(BF16), v_ab[ab]) / l
                out = term if out is None else out + term
            o_ref[:, lanes] = out.astype(BF16)

    return pl.pallas_call(
        body, name="mem_attn_fwd", grid=(bsz, nq),
        in_specs=[pl.BlockSpec((tq, MEM_W), lambda b, i: (b * nq + i, q_col)),
                  pl.BlockSpec((N_MEM, 2 * MEM_W), lambda b, i: (b, 0))],
        out_specs=pl.BlockSpec((tq, MEM_W), lambda b, i: (b * nq + i, 0)),
        out_shape=_sds((bsz * seq, MEM_W), BF16), compiler_params=_params(("parallel", "parallel")))(h, mkv)


def _mem_attn_bwd(h, mkv, dcat, bsz, seq, q_col, tq=2048):
    nq = seq // tq
    do_col = MIX_W // MEM_W

    def body(q_ref, kv_ref, do_ref, dq_ref, dkv_ref):
        @pl.when(pl.program_id(1) == 0)
        def _():
            dkv_ref[...] = jnp.zeros_like(dkv_ref)

        for pr in range(2):
            lanes = _lane_half(pr)
            vlanes = slice(MEM_W + pr * 128, MEM_W + (pr + 1) * 128)
            q_ab = _split_pair(q_ref[:, lanes])
            do_ab = _split_pair(do_ref[:, lanes])
            k, v = kv_ref[:, lanes], kv_ref[:, vlanes]
            k_ab = _split_pair(k)
            dq, dk, dv = None, None, None
            for ab in range(2):
                s = _dot_nt(q_ab[ab], k) * SCALE
                m = jnp.max(s, axis=-1, keepdims=True)
                e = jnp.exp(s - m)
                p = e / jnp.sum(e, axis=-1, keepdims=True)
                dp = _dot_nt(do_ab[ab], v)
                dd = jnp.sum(p * dp, axis=-1, keepdims=True)
                ds = (p * (dp - dd) * SCALE).astype(BF16)
                terms = (_dot_nn(ds, k_ab[ab]), _dot_tn(ds, q_ab[ab]), _dot_tn(p.astype(BF16), do_ab[ab]))
                dq, dk, dv = terms if dq is None else (dq + terms[0], dk + terms[1], dv + terms[2])
            dq_ref[:, lanes] = dq.astype(BF16)
            dkv_ref[:, lanes] += dk
            dkv_ref[:, vlanes] += dv

    return pl.pallas_call(
        body, name="mem_attn_bwd", grid=(bsz, nq),
        in_specs=[pl.BlockSpec((tq, MEM_W), lambda b, i: (b * nq + i, q_col)),
                  pl.BlockSpec((N_MEM, 2 * MEM_W), lambda b, i: (b, 0)),
                  pl.BlockSpec((tq, MEM_W), lambda b, i: (b * nq + i, do_col))],
        out_specs=[pl.BlockSpec((tq, MEM_W), lambda b, i: (b * nq + i, 0)),
                   pl.BlockSpec((N_MEM, 2 * MEM_W), lambda b, i: (b, 0))],
        out_shape=[_sds((bsz * seq, MEM_W), BF16), _sds((bsz * N_MEM, 2 * MEM_W), F32)],
        compiler_params=_params(("parallel", "arbitrary")))(h, mkv, dcat)


_GELU_C = math.sqrt(2.0 / math.pi)
_GELU_A = 0.044715


def _gelu(x):
    return 0.5 * x * (1.0 + jnp.tanh(_GELU_C * (x + _GELU_A * x * x * x)))


def _gelu_grad(x):
    th = jnp.tanh(_GELU_C * (x + _GELU_A * x * x * x))
    return 0.5 * (1.0 + th) + 0.5 * x * (1.0 - th * th) * (_GELU_C * (1.0 + 3.0 * _GELU_A * x * x))


def _tril_mask(lower):
    ri = lax.broadcasted_iota(jnp.int32, (BLK, BLK), 0)
    ci = lax.broadcasted_iota(jnp.int32, (BLK, BLK), 1)
    return (ri >= ci) if lower else (ci >= ri)


def _sgu_fwd(h, ws, bs_t, ln_g, ln_b, tm=512):
    t = h.shape[0]

    def body(u_ref, v_ref, ws_ref, bs_ref, g_ref, b_ref, o_ref):
        ug = _gelu(u_ref[...].astype(F32))
        vhat, _ = _ln_hat(_gelu(v_ref[...].astype(F32)))
        vn = (vhat * g_ref[...] + b_ref[...]).astype(BF16)
        mask = _tril_mask(True)
        first = lax.broadcasted_iota(jnp.int32, (1, 2 * HEAD_DIM), 1) < HEAD_DIM
        for pr in range(N_HEADS // 2):
            lanes = _lane_half(pr)
            w_ab = [jnp.where(mask, ws_ref[2 * pr + ab], 0).astype(BF16) for ab in range(2)]
            bias = jnp.where(first, bs_ref[:, 2 * pr:2 * pr + 1], bs_ref[:, 2 * pr + 1:2 * pr + 2])
            for c in range(tm // BLK):
                rs = slice(c * BLK, (c + 1) * BLK)
                v_ab = _split_pair(vn[rs, lanes])
                mixed = _dot_nn(w_ab[0], v_ab[0]) + _dot_nn(w_ab[1], v_ab[1]) + bias
                o_ref[rs, lanes] = (ug[rs, lanes] * mixed).astype(BF16)

    return pl.pallas_call(
        body, name="sgu_fwd", grid=(t // tm,),
        in_specs=[_rows(tm, MIX_W, 0), _rows(tm, MIX_W, 1), _whole(ws.shape), _whole(bs_t.shape), _whole(ln_g.shape), _whole(ln_b.shape)],
        out_specs=_rows(tm, MIX_W), out_shape=_sds((t, MIX_W), BF16),
        compiler_params=_params(("parallel",)))(h, h, ws, bs_t, ln_g, ln_b)


def _sgu_bwd(h, dcat, ws, ws_t, bs_t, ln_g, ln_b, tm=512):
    t = h.shape[0]

    def body(u_ref, v_ref, do_ref, ws_ref, wst_ref, bs_ref, g_ref, b_ref, dh_ref, dws_ref, dbs_ref, dg_ref, db_ref, dvn_ref):
        @pl.when(pl.program_id(0) == 0)
        def _():
            dws_ref[...] = jnp.zeros_like(dws_ref)
            dbs_ref[...] = jnp.zeros_like(dbs_ref)
            dg_ref[...] = jnp.zeros_like(dg_ref)
            db_ref[...] = jnp.zeros_like(db_ref)

        u = u_ref[...].astype(F32)
        v = v_ref[...].astype(F32)
        do = do_ref[...].astype(F32)
        ug = _gelu(u)
        vhat, rstd = _ln_hat(_gelu(v))
        vn = (vhat * g_ref[...] + b_ref[...]).astype(BF16)
        dmixed_f = do * ug
        dmixed = dmixed_f.astype(BF16)
        low, upp = _tril_mask(True), _tril_mask(False)
        first = lax.broadcasted_iota(jnp.int32, (1, 2 * HEAD_DIM), 1) < HEAD_DIM
        for pr in range(N_HEADS // 2):
            lanes = _lane_half(pr)
            w_ab = [jnp.where(low, ws_ref[2 * pr + ab], 0).astype(BF16) for ab in range(2)]
            wt_ab = [jnp.where(upp, wst_ref[2 * pr + ab], 0).astype(BF16) for ab in range(2)]
            bias = jnp.where(first, bs_ref[:, 2 * pr:2 * pr + 1], bs_ref[:, 2 * pr + 1:2 * pr + 2])
            dws_acc = [None, None]
            dbs_acc = [None, None]
            for c in range(tm // BLK):
                rs = slice(c * BLK, (c + 1) * BLK)
                vn_pair = vn[rs, lanes]
                v_ab = _split_pair(vn_pair)
                mixed = _dot_nn(w_ab[0], v_ab[0]) + _dot_nn(w_ab[1], v_ab[1]) + bias
                dh_ref[rs, lanes] = (do[rs, lanes] * mixed * _gelu_grad(u[rs, lanes])).astype(BF16)
                dm_ab = _split_pair(dmixed[rs, lanes])
                dmf_ab = _split_pair(dmixed_f[rs, lanes])
                for ab in range(2):
                    term = _dot_nt(dm_ab[ab], vn_pair)
                    dws_acc[ab] = term if dws_acc[ab] is None else dws_acc[ab] + term
                    rsum = jnp.sum(dmf_ab[ab], axis=-1, keepdims=True)
                    dbs_acc[ab] = rsum if dbs_acc[ab] is None else dbs_acc[ab] + rsum
                dvn_ref[rs, lanes] = _dot_nn(wt_ab[0], dm_ab[0]) + _dot_nn(wt_ab[1], dm_ab[1])
            for ab in range(2):
                g = 2 * pr + ab
                dws_ref[g] += jnp.where(low, dws_acc[ab], 0.0)
                dbs_ref[:, g:g + 1] += dbs_acc[ab]
        dvn = dvn_ref[...]
        dg_ref[...] += jnp.sum(dvn * vhat, axis=0, keepdims=True)
        db_ref[...] += jnp.sum(dvn, axis=0, keepdims=True)
        dxh = dvn * g_ref[...]
        m1 = jnp.mean(dxh, axis=-1, keepdims=True)
        m2 = jnp.mean(dxh * vhat, axis=-1, keepdims=True)
        dvg = rstd * (dxh - m1 - vhat * m2)
        dh_ref[:, MIX_W:] = (dvg * _gelu_grad(v)).astype(BF16)

    return pl.pallas_call(
        body, name="sgu_bwd", grid=(t // tm,),
        in_specs=[_rows(tm, MIX_W, 0), _rows(tm, MIX_W, 1), _rows(tm, MIX_W, 0), _whole(ws.shape), _whole(ws_t.shape),
                  _whole(bs_t.shape), _whole(ln_g.shape), _whole(ln_b.shape)],
        out_specs=[_rows(tm, 2 * MIX_W), _whole(ws.shape), _whole(bs_t.shape), _whole((1, MIX_W)), _whole((1, MIX_W))],
        out_shape=[_sds((t, 2 * MIX_W), BF16), _sds(ws.shape, F32), _sds(bs_t.shape, F32), _sds((1, MIX_W), F32), _sds((1, MIX_W), F32)],
        scratch_shapes=[pltpu.VMEM((tm, MIX_W), F32)],
        compiler_params=_params(("arbitrary",)))(h, h, dcat, ws, ws_t, bs_t, ln_g, ln_b)


def _row_tile(rows, cols, itemsize=4, limit=2 ** 21):
    best = rows
    for cand in (4096, 2048, 1024, 512, 256, 128, 64, 32, 16):
        if rows % cand == 0 and rows > cand:
            best = cand
            if cand * cols * itemsize <= limit:
                break
    return best


def _adamw(w, m, v, grad=None, parts=None, first_parts=None):
    rows, cols = w.shape
    rows0 = 0 if first_parts is None else first_parts.shape[1]
    tr = _row_tile(rows0 if rows0 else rows, cols)
    n0 = rows0 // tr

    def chip_sum(ref):
        acc = ref[0].astype(F32)
        for k in range(1, 4):
            acc = acc + ref[k].astype(F32)
        return acc

    def body(*refs):
        w_ref, m_ref, v_ref = refs[:3]
        go_ref, d_ref, nm_ref, nv_ref = refs[-4:]
        if parts is None:
            gv = refs[3][...]
        elif first_parts is None:
            gv = chip_sum(refs[3])
        else:
            gv = jnp.where(pl.program_id(0) < n0, chip_sum(refs[3]), chip_sum(refs[4]))
        nm = ADAM_B1 * m_ref[...] + (1.0 - ADAM_B1) * gv
        nv = ADAM_B2 * v_ref[...] + (1.0 - ADAM_B2) * (gv * gv)
        m_hat = nm / (1.0 - ADAM_B1 ** ADAM_STEP)
        v_hat = nv / (1.0 - ADAM_B2 ** ADAM_STEP)
        go_ref[...] = gv
        d_ref[...] = -ADAM_LR * (m_hat / (jnp.sqrt(v_hat) + ADAM_EPS) + ADAM_WD * w_ref[...])
        nm_ref[...] = nm
        nv_ref[...] = nv

    spec = _rows(tr, cols)
    if parts is None:
        g_specs, g_args = [spec], [grad]
    elif first_parts is None:
        g_specs, g_args = [pl.BlockSpec((4, tr, cols), lambda i: (0, i, 0))], [parts]
    else:
        g_specs = [pl.BlockSpec((4, tr, cols), lambda i: (0, jnp.minimum(i, n0 - 1), 0)),
                   pl.BlockSpec((4, tr, cols), lambda i: (0, jnp.maximum(i - n0, 0), 0))]
        g_args = [first_parts, parts]
    return pl.pallas_call(
        body, name="adamw" if parts is None else "adamw_sum_chips", grid=(rows // tr,), in_specs=[spec] * 3 + g_specs,
        out_specs=[spec] * 4, out_shape=[_sds(w.shape, F32)] * 4,
        compiler_params=_params(("parallel",)))(w, m, v, *g_args)


_ANY = pl.BlockSpec(memory_space=pl.ANY)
_MESH = pl.DeviceIdType.MESH


def _all_gather(name, blocks):
    nt = len(blocks)

    def body(*refs):
        start, finish = _gather_protocol(refs[:nt], refs[nt:2 * nt], *refs[2 * nt:])
        start()
        finish()

    return pl.pallas_call(
        body, name=name, out_shape=_gather_shapes(blocks), in_specs=[_ANY] * nt, out_specs=[_ANY] * nt,
        scratch_shapes=_gather_sems(nt))(*blocks)


def _gather_shapes(blocks):
    return [_sds((N_DEV,) + b.shape, b.dtype) for b in blocks]


def _gather_sems(nt):
    return [pltpu.SemaphoreType.DMA((nt, 7)), pltpu.SemaphoreType.DMA((nt, 7)), pltpu.SemaphoreType.DMA((nt,))]


def _gather_protocol(x_refs, out_refs, send_sems, recv_sems, local_sems):
    nt = len(x_refs)
    x, y, c = lax.axis_index("x"), lax.axis_index("y"), lax.axis_index("c")
    me, sibling = (x, y, c), (x, y, 1 - c)
    chips = [(1 - x, y), (x, 1 - y), (1 - x, 1 - y)]

    def slot(t, px, py, pc):
        return out_refs[t].at[4 * px + 2 * py + pc]

    def copy(t, k, blk, to, src=None):
        return pltpu.make_async_remote_copy(
            src_ref=slot(t, *blk) if src is None else src, dst_ref=slot(t, *blk),
            send_sem=send_sems.at[t, k], recv_sem=recv_sems.at[t, k], device_id=to, device_id_type=_MESH)

    def own_copies():
        mine = [pltpu.make_async_copy(x_refs[t], slot(t, *me), local_sems.at[t]) for t in range(nt)]
        first = []
        for t in range(nt):
            first.append(copy(t, 0, me, sibling, src=x_refs[t]))
            first += [copy(t, 1 + j, me, (*chip, c), src=x_refs[t]) for j, chip in enumerate(chips)]
        return mine, first

    def start():
        mine, first = own_copies()
        for cp in mine + first:
            cp.start()

    def finish():
        mine, first = own_copies()
        passed = []
        for j, chip in enumerate(chips):
            for t in range(nt):
                copy(t, 1 + j, (*chip, c), me).wait_recv()
                fwd = copy(t, 4 + j, (*chip, c), sibling)
                fwd.start()
                passed.append(fwd)
        for t in range(nt):
            copy(t, 0, sibling, me).wait_recv()
        for j, chip in enumerate(chips):
            for t in range(nt):
                copy(t, 4 + j, (*chip, 1 - c), me).wait_recv()
        for cp in first + passed:
            cp.wait_send()
        for cp in mine:
            cp.wait()

    return start, finish


def _swap_with_sibling(packed):
    nt = len(packed)

    def body(*refs):
        p_refs, got_refs = refs[:nt], refs[nt:2 * nt]
        send_sems, recv_sems = refs[2 * nt:]
        x, y, c = lax.axis_index("x"), lax.axis_index("y"), lax.axis_index("c")
        copies = [
            pltpu.make_async_remote_copy(
                src_ref=p_refs[t].at[1 - c], dst_ref=got_refs[t], send_sem=send_sems.at[t], recv_sem=recv_sems.at[t],
                device_id=(x, y, 1 - c), device_id_type=_MESH)
            for t in range(nt)]
        for cp in copies:
            cp.start()
        for cp in copies:
            cp.wait_recv()
        for cp in copies:
            cp.wait_send()

    return pl.pallas_call(
        body, name="grad_swap_sibling", out_shape=[_sds(p.shape[1:], p.dtype) for p in packed], in_specs=[_ANY] * nt,
        out_specs=[_ANY] * nt,
        scratch_shapes=[pltpu.SemaphoreType.DMA((nt,)), pltpu.SemaphoreType.DMA((nt,))])(*packed)


def _chip_sum(packed, got):
    _, nchip, rows, cols = packed.shape
    tr = _row_tile(rows, cols, 2)
    core = lax.axis_index("c").astype(jnp.int32).reshape(1)

    def body(c_ref, p_ref, g_ref, o_ref):
        o_ref[...] = (p_ref[...].astype(F32) + g_ref[...].astype(F32)).astype(o_ref.dtype)

    grid_spec = pltpu.PrefetchScalarGridSpec(
        num_scalar_prefetch=1, grid=(nchip, rows // tr),
        in_specs=[pl.BlockSpec((None, None, tr, cols), lambda k, i, c: (c[0], k, i, 0)),
                  pl.BlockSpec((None, tr, cols), lambda k, i, c: (k, i, 0))],
        out_specs=pl.BlockSpec((None, tr, cols), lambda k, i, c: (k, i, 0)))
    return pl.pallas_call(
        body, name="grad_chip_sum", grid_spec=grid_spec, out_shape=_sds(got.shape, got.dtype),
        compiler_params=_params(("parallel", "parallel")))(core, packed, got)


def _exchange_chips(chip_sums):
    nt = len(chip_sums)

    def body(*refs):
        start, finish = _exchange_protocol(refs[:nt], refs[nt:2 * nt], *refs[2 * nt:])
        start()
        finish()

    return pl.pallas_call(
        body, name="grad_exchange_chips", out_shape=[_sds(s.shape, s.dtype) for s in chip_sums], in_specs=[_ANY] * nt,
        out_specs=[_ANY] * nt, scratch_shapes=_exchange_sems(nt))(*chip_sums)


def _exchange_sems(nt):
    return [pltpu.SemaphoreType.DMA((nt, 3)), pltpu.SemaphoreType.DMA((nt, 3)), pltpu.SemaphoreType.DMA((nt,))]


def _exchange_protocol(s_refs, got_refs, send_sems, recv_sems, local_sems):
    nt = len(s_refs)
    x, y, c = lax.axis_index("x"), lax.axis_index("y"), lax.axis_index("c")
    my_chip = 2 * x + y
    chips = [(1 - x, y), (x, 1 - y), (1 - x, 1 - y)]

    def copy(t, j, src_chip, dst_chip):
        px, py = chips[j]
        return pltpu.make_async_remote_copy(
            src_ref=s_refs[t].at[src_chip], dst_ref=got_refs[t].at[dst_chip], send_sem=send_sems.at[t, j],
            recv_sem=recv_sems.at[t, j], device_id=(px, py, c), device_id_type=_MESH)

    def own_copies():
        mine = [pltpu.make_async_copy(s_refs[t].at[my_chip], got_refs[t].at[my_chip], local_sems.at[t]) for t in range(nt)]
        sends = [copy(t, j, 2 * px + py, my_chip) for t in range(nt) for j, (px, py) in enumerate(chips)]
        return mine, sends

    def start():
        mine, sends = own_copies()
        for cp in mine + sends:
            cp.start()

    def finish():
        mine, sends = own_copies()
        for j, (px, py) in enumerate(chips):
            for t in range(nt):
                copy(t, j, my_chip, 2 * px + py).wait_recv()
        for cp in sends:
            cp.wait_send()
        for cp in mine:
            cp.wait()

    return start, finish


def _sum_chips(got):
    _, rows, cols = got.shape
    tr = _row_tile(rows, cols)

    def body(g_ref, o_ref):
        acc = g_ref[0].astype(F32)
        for k in range(1, 4):
            acc = acc + g_ref[k].astype(F32)
        o_ref[...] = acc

    return pl.pallas_call(
        body, name="grad_sum_chips", grid=(rows // tr,), in_specs=[pl.BlockSpec((4, tr, cols), lambda i: (0, i, 0))],
        out_specs=pl.BlockSpec((tr, cols), lambda i: (i, 0)), out_shape=_sds((rows, cols), F32),
        compiler_params=_params(("parallel",)))(got)


_COL_SHARDED = ("a_w_in", "b_w_in", "w_gate", "w_up")
_ROW_SHARDED = ("w_mem_kv", "w_out", "w_down")
_BIG = ("a_w_in", "b_w_in", "w_mem_kv", "w_out", "w_gate", "w_up", "w_down")
_SGU_LN = ("sgu_ln_g", "sgu_ln_b")
_LN4 = ("ln_mix_g", "ln_mix_b", "ln_ffn_g", "ln_ffn_b")
_REPLICATED = ("sgu_w_s", "sgu_b_s") + _LN4


def _unshard(name, gathered):
    if name in _COL_SHARDED or name in _SGU_LN:
        moved = jnp.moveaxis(gathered, 0, -2)
        return moved.reshape(moved.shape[:-2] + (moved.shape[-2] * moved.shape[-1],))
    assert name in _ROW_SHARDED, name
    moved = jnp.moveaxis(gathered, 0, 1)
    return moved.reshape((moved.shape[0], moved.shape[1] * moved.shape[2]) + moved.shape[3:])


_LAID_OUT_IN_KERNEL = _COL_SHARDED


def _after_gather(name, gathered):
    return gathered if name in _LAID_OUT_IN_KERNEL else _unshard(name, gathered)


def _by_shard(name, full):
    if name in _COL_SHARDED or name in _SGU_LN:
        split = full.reshape(full.shape[:-1] + (N_DEV, full.shape[-1] // N_DEV))
        return jnp.moveaxis(split, -2, 0)
    split = full.reshape((full.shape[0], N_DEV, full.shape[1] // N_DEV) + full.shape[2:])
    return jnp.moveaxis(split, 1, 0)


def _layer_keys(i):
    return [("a_w_in" if i % 2 == 0 else "b_w_in", i // 2)] + [(n, i) for n in ("w_mem_kv", "w_out", "w_gate", "w_up", "w_down")]


_GATHER_FIRST = _layer_keys(0)[:2]
_GATHER_LATER = (_layer_keys(0)[2:] + _layer_keys(1), _layer_keys(2)[:4], _layer_keys(2)[4:],
                 _layer_keys(3)[:4], _layer_keys(3)[4:])


def _shard_block(shards, key):
    name, idx = key
    return shards[name][idx:idx + 1].astype(BF16)


def _gather_first(shards):
    blocks = [_shard_block(shards, k) for k in _GATHER_FIRST] + [shards[n] for n in _SGU_LN]
    gathered = _all_gather("first_all_gather", blocks)
    full = {k: _after_gather(k[0], g) for k, g in zip(_GATHER_FIRST, gathered)}
    sgu_ln = {n: _unshard(n, g) for n, g in zip(_SGU_LN, gathered[len(_GATHER_FIRST):])}
    return full, sgu_ln


def _two_level(by_dest):
    shp = by_dest.shape[1:]
    split = by_dest.astype(BF16).reshape((4, 2) + shp).swapaxes(0, 1)
    return split.reshape(2, 4, int(np.prod(shp[:-1])), shp[-1])


_EARLY = _BIG + _SGU_LN


def _chip_sums_of_early(grads):
    packed = [_two_level(_by_shard(n, jnp.stack(grads[n][1:] if n == "a_w_in" else grads[n]))) for n in _EARLY]
    ln4 = jnp.stack([jnp.stack(grads[n]) for n in _LN4])
    rep = [jnp.stack(grads["sgu_w_s"]).reshape(N_DEV, -1, BLK), jnp.stack(grads["sgu_b_s"]).reshape(N_DEV, -1, BLK),
           ln4.reshape(N_DEV, -1, D_MODEL)]
    packed += [_two_level(r) for r in rep]
    got = _swap_with_sibling(packed)
    return [_chip_sum(p, g) for p, g in zip(packed, got)]


def _finish_replicated(parts, shapes):
    w_s, b_s, ln_all = _all_gather("replicated_grads_all_gather", [_sum_chips(p) for p in parts])
    ln_all = ln_all.reshape((len(_LN4),) + tuple(shapes[_LN4[0]]))
    rep_grads = {"sgu_w_s": w_s.reshape(shapes["sgu_w_s"]), "sgu_b_s": b_s.reshape(shapes["sgu_b_s"])}
    rep_grads.update({n: ln_all[i] for i, n in enumerate(_LN4)})
    return rep_grads


def _reduce_last(grad_a_first):
    packed = [_two_level(_by_shard("a_w_in", grad_a_first))]
    got = _swap_with_sibling(packed)
    return _exchange_chips([_chip_sum(packed[0], got[0])])[0]


def _as_2d(a):
    if a.ndim == 1:
        return a.reshape(1, -1)
    return a.reshape(-1, a.shape[-1])


def kernel(x, mem, a_w_in, b_w_in, sgu_ln_g, sgu_ln_b, sgu_w_s, sgu_b_s, w_mem_kv, w_out, ln_mix_g, ln_mix_b, w_gate, w_up, w_down, ln_ffn_g, ln_ffn_b, loss_target, m_a_w_in, m_b_w_in, m_sgu_ln_g, m_sgu_ln_b, m_sgu_w_s, m_sgu_b_s, m_w_mem_kv, m_w_out, m_ln_mix_g, m_ln_mix_b, m_w_gate, m_w_up, m_w_down, m_ln_ffn_g, m_ln_ffn_b, v_a_w_in, v_b_w_in, v_sgu_ln_g, v_sgu_ln_b, v_sgu_w_s, v_sgu_b_s, v_w_mem_kv, v_w_out, v_ln_mix_g, v_ln_mix_b, v_w_gate, v_w_up, v_w_down, v_ln_ffn_g, v_ln_ffn_b):
    names = ("a_w_in", "b_w_in", "sgu_ln_g", "sgu_ln_b", "sgu_w_s", "sgu_b_s", "w_mem_kv", "w_out", "ln_mix_g", "ln_mix_b",
             "w_gate", "w_up", "w_down", "ln_ffn_g", "ln_ffn_b")
    weights = dict(zip(names, (a_w_in, b_w_in, sgu_ln_g, sgu_ln_b, sgu_w_s, sgu_b_s, w_mem_kv, w_out, ln_mix_g, ln_mix_b,
                               w_gate, w_up, w_down, ln_ffn_g, ln_ffn_b)))
    mom_m = dict(zip(names, (m_a_w_in, m_b_w_in, m_sgu_ln_g, m_sgu_ln_b, m_sgu_w_s, m_sgu_b_s, m_w_mem_kv, m_w_out, m_ln_mix_g,
                             m_ln_mix_b, m_w_gate, m_w_up, m_w_down, m_ln_ffn_g, m_ln_ffn_b)))
    mom_v = dict(zip(names, (v_a_w_in, v_b_w_in, v_sgu_ln_g, v_sgu_ln_b, v_sgu_w_s, v_sgu_b_s, v_w_mem_kv, v_w_out, v_ln_mix_g,
                             v_ln_mix_b, v_w_gate, v_w_up, v_w_down, v_ln_ffn_g, v_ln_ffn_b)))
    full, sgu_ln = _gather_first(weights)
    pending = [(keys, [_shard_block(weights, k) for k in keys]) for keys in _GATHER_LATER]
    loss_part, grad_x, local, early = _local_step(
        x, mem, loss_target, full, sgu_ln, {n: weights[n] for n in _REPLICATED}, pending)
    loss = lax.psum(loss_part[0, 0], ("x", "y", "c"))
    early_parts = dict(zip(_EARLY, early))
    rep_grads = _finish_replicated(early[len(_EARLY):], {n: weights[n].shape for n in _REPLICATED})
    a_first_parts = _reduce_last(local["a_w_in"][:1])

    reduced, deltas, new_m, new_v = {}, {}, {}, {}
    for n in names:
        w2, m2, v2 = _as_2d(weights[n]), _as_2d(mom_m[n]), _as_2d(mom_v[n])
        if n in early_parts:
            outs = _adamw(w2, m2, v2, parts=early_parts[n], first_parts=a_first_parts if n == "a_w_in" else None)
        else:
            outs = _adamw(w2, m2, v2, grad=_as_2d(rep_grads[n]))
        reduced[n], deltas[n], new_m[n], new_v[n] = (a.reshape(weights[n].shape) for a in outs)

    return (loss, grad_x, *[reduced[n] for n in names], *[deltas[n] for n in names],
            *[new_m[n] for n in names], *[new_v[n] for n in names])


def _local_step(x, mem, loss_target, full, sgu_ln, small, pending=None):
    sgu_w_s, sgu_b_s = small["sgu_w_s"], small["sgu_b_s"]
    ln_mix_g, ln_mix_b, ln_ffn_g, ln_ffn_b = (small[n] for n in ("ln_mix_g", "ln_mix_b", "ln_ffn_g", "ln_ffn_b"))
    bsz, seq, _ = x.shape
    tokens = bsz * seq
    slopes = _alibi_table()
    full = dict(full)
    exchanging = pending is not None
    pending = list(pending or [])

    def weight(name, idx):
        return full[(name, idx)][0]

    def next_group():
        return pending[0][1] if pending else None

    def landed(gathered):
        if gathered:
            keys, _ = pending.pop(0)
            full.update({k: _after_gather(k[0], g) for k, g in zip(keys, gathered)})

    res = (x.reshape(tokens, D_MODEL),)
    xb = res[0].astype(BF16)
    memb = mem.reshape(bsz * N_MEM, D_MODEL).astype(BF16)
    tgt = loss_target.reshape(tokens, D_MODEL)

    saved = []
    for i in range(DEPTH):
        j = i // 2
        dil_layer = i % 2 == 0
        mkv = _linear_nn("mem_kv", memb, weight("w_mem_kv", i))
        h, w_in = _linear_nn_gathered("in_proj_a" if dil_layer else "in_proj_b", xb,
                                      full[("a_w_in" if dil_layer else "b_w_in", j)][:, 0])
        st = dict(xb=xb, h=h, mkv=mkv, w_in=w_in)
        if dil_layer:
            mix, st["lse"], gathered = _band_attn_fwd_fused(h, slopes, bsz, seq, gather=next_group() if i == 0 else None)
            landed(gathered)
            q_col = 3 * MIX_W // MEM_W
        else:
            st["ws"] = sgu_w_s[j]
            st["bs_t"] = sgu_b_s[j].T
            st["ln_g"] = sgu_ln["sgu_ln_g"][j].reshape(1, MIX_W)
            st["ln_b"] = sgu_ln["sgu_ln_b"][j].reshape(1, MIX_W)
            mix = _sgu_fwd(h, st["ws"], st["bs_t"], st["ln_g"], st["ln_b"])
            q_col = 2 * MIX_W // MEM_W
        mo = _mem_attn_fwd(h, mkv, bsz, seq, q_col)
        w_out, w_down = weight("w_out", i), weight("w_down", i)
        mix_ln = (ln_mix_g[i].reshape(1, D_MODEL), ln_mix_b[i].reshape(1, D_MODEL))
        ffn_ln = (ln_ffn_g[i].reshape(1, D_MODEL), ln_ffn_b[i].reshape(1, D_MODEL))
        r1, x1b, _ = _proj_ln_fwd("out_proj_ln", [mix, mo], w_out, res, *mix_ln)
        gt, up, act, w_gu, gathered = _ffn_up_fwd(x1b, full[("w_gate", i)][:, 0], full[("w_up", i)][:, 0],
                                                  gather=next_group() if i < 2 else None)
        landed(gathered)
        r2, xb, gathered = _proj_ln_fwd("ffn_down_ln", [act], w_down, (r1, *mix_ln), *ffn_ln,
                                        gather=next_group() if i < 2 else None)
        landed(gathered)
        res = (r2, *ffn_ln)
        st.update(mix=mix, mo=mo, q_col=q_col, r1=r1, x1b=x1b, gt=gt, up=up, act=act, r2=r2,
                  w_out=w_out, w_down=w_down, w_gu=w_gu)
        saved.append(st)

    dr2, dr2b, dg, db, loss_part = _loss_ln_bwd(*res, tgt)

    early_parts = None
    per_pair = ("a_w_in", "b_w_in", "sgu_ln_g", "sgu_ln_b", "sgu_w_s", "sgu_b_s")
    grads = {n: [None] * (DEPTH // 2 if n in per_pair else DEPTH) for n in _BIG + _SGU_LN + _REPLICATED}
    for i in reversed(range(DEPTH)):
        j = i // 2
        st = saved[i]
        dil_layer = i % 2 == 0
        w_in = st["w_in"]
        grads["ln_ffn_g"][i], grads["ln_ffn_b"][i] = dg[0], db[0]
        dgu = _ffn_down_bwd(dr2b, st["w_down"], st["gt"], st["up"])
        grads["w_down"][i] = _mm_tn("grad_w_down", st["act"], dr2b)
        dr1, dr1b, dg, db = _linear_nt("ffn_up_bwd", [dgu], st["w_gu"], dr2, F32,
                                       ln=(st["r1"], ln_mix_g[i].reshape(1, D_MODEL)))
        grads["ln_mix_g"][i], grads["ln_mix_b"][i] = dg[0], db[0]
        grads["w_gate"][i] = _mm_tn("grad_w_gate_up", st["x1b"], dgu, cols=(0, D_FF))
        grads["w_up"][i] = _mm_tn("grad_w_gate_up", st["x1b"], dgu, cols=(D_FF, 2 * D_FF))
        dcat = _linear_nt("out_proj_bwd", [dr1b], st["w_out"], None, BF16)
        grads["w_out"][i] = _mm_tn_parts("grad_w_out", dr1b, [st["mix"], st["mo"]], split_rows=True)
        dqm, dmkv = _mem_attn_bwd(st["h"], st["mkv"], dcat, bsz, seq, st["q_col"])
        grads["w_mem_kv"][i] = _mm_tn("grad_w_mem_kv", memb, dmkv.astype(BF16))
        if dil_layer:
            early_sums = _chip_sums_of_early(grads) if (i == 0 and exchanging) else None
            dh_parts, exchanged = _band_attn_bwd_fused(st["h"], dcat, st["mix"], st["lse"], slopes, bsz, seq,
                                                       exchange=early_sums)
            if early_sums is not None:
                early_parts = exchanged
        else:
            ws_t = jnp.swapaxes(st["ws"], -1, -2)
            dh_main, dws, dbs_t, dlg, dlb = _sgu_bwd(st["h"], dcat, st["ws"], ws_t, st["bs_t"], st["ln_g"], st["ln_b"])
            grads["sgu_w_s"][j], grads["sgu_b_s"][j] = dws, dbs_t.T
            grads["sgu_ln_g"][j], grads["sgu_ln_b"][j] = dlg[0], dlb[0]
            dh_parts = [dh_main]
        name = "in_proj_bwd_a" if dil_layer else "in_proj_bwd_b"
        if i > 0:
            dr2, dr2b, dg, db = _linear_nt(name, [*dh_parts, dqm], w_in, dr1, F32,
                                           ln=(saved[i - 1]["r2"], ln_ffn_g[i - 1].reshape(1, D_MODEL)))
        else:
            grad_x = _linear_nt(name + "_x", [*dh_parts, dqm], w_in, dr1, F32).reshape(x.shape)
        grads["a_w_in" if dil_layer else "b_w_in"][j] = _mm_tn_parts(
            "grad_w_in_a" if dil_layer else "grad_w_in_b", st["xb"], [*dh_parts, dqm])
    return loss_part, grad_x, {n: jnp.stack(g) for n, g in grads.items()}, early_parts
```
